```python
import math
import jax, jax.numpy as jnp
from jax import lax
import numpy as np

D_MODEL = 2048
BATCH = 8
SEQ = 8192
DEPTH = 1

PLE_DIM = 256
MIX_WIDTH = D_MODEL
ATTN_WIDTH = MIX_WIDTH // 2
HEAD_DIM = 64
N_HEADS = ATTN_WIDTH // HEAD_DIM
SSM_WIDTH = MIX_WIDTH - ATTN_WIDTH
SSM_GROUP = 16
N_SSM_GROUPS = SSM_WIDTH // SSM_GROUP
SSM_STATE = 64
D_FF = ((8 * D_MODEL // 3 + 127) // 128) * 128
DILATIONS = ((128, 1), (512, 4), (2048, 16))
SWA_BLOCK = 128
NORM_EPS = 1e-6
DT_MIN = 1e-3
DT_MAX = 1e-1
MASK_VALUE = -1e30

kernel_name = 'hymba_dilated_s5_macaron'


def rms_norm(x, g):
    xf = x.astype(jnp.float32)
    y = xf * lax.rsqrt(jnp.mean(xf * xf, axis=-1, keepdims=True) + NORM_EPS)
    return (y * g.astype(jnp.float32)).astype(x.dtype)


def swiglu(x, w_gate, w_up, w_down):
    return (jax.nn.silu(x @ w_gate) * (x @ w_up)) @ w_down


def banded_attention_stats(q, k, v, span):
    N, H, L, E = q.shape
    Q = SWA_BLOCK
    nb = -(-L // Q)
    Lp = nb * Q
    qf, kf, vf = (t.astype(jnp.float32) for t in (q, k, v))
    qb = jnp.pad(qf, ((0, 0), (0, 0), (0, Lp - L), (0, 0))).reshape(N, H, nb, Q, E)
    kp = jnp.pad(kf, ((0, 0), (0, 0), (Q, Lp - L), (0, 0)))
    vp = jnp.pad(vf, ((0, 0), (0, 0), (Q, Lp - L), (0, 0)))
    kb = jnp.concatenate([kp[:, :, :Lp].reshape(N, H, nb, Q, E), kp[:, :, Q:].reshape(N, H, nb, Q, E)], axis=3)
    vb = jnp.concatenate([vp[:, :, :Lp].reshape(N, H, nb, Q, E), vp[:, :, Q:].reshape(N, H, nb, Q, E)], axis=3)
    s = jnp.einsum('nhbqe,nhbke->nhbqk', qb, kb) * (E ** -0.5)
    qi = jnp.arange(Q)[:, None]
    ki = jnp.arange(2 * Q)[None, :]
    dist = qi + Q - ki
    blk = jnp.arange(nb)[:, None, None]
    valid = (dist >= 0) & (dist <= span) & (blk * Q + ki - Q >= 0)
    s = jnp.where(valid, s, MASK_VALUE)
    m = jnp.max(s, axis=-1)
    pexp = jnp.exp(s - m[..., None])
    l = jnp.sum(pexp, axis=-1)
    o = jnp.einsum('nhbqk,nhbke->nhbqe', pexp, vb)
    o = o.reshape(N, H, Lp, E)[:, :, :L]
    m = m.reshape(N, H, Lp)[:, :, :L]
    l = l.reshape(N, H, Lp)[:, :, :L]
    return o, m, l


def dilated_attention(q, k, v):
    B, S, H, E = q.shape
    outs, maxes, dens = [], [], []
    for window, d in DILATIONS:
        L = S // d
        span = window // d

        def to_residue(t):
            return t.reshape(B, L, d, H, E).transpose(0, 2, 3, 1, 4).reshape(B * d, H, L, E)

        o, m, l = banded_attention_stats(to_residue(q), to_residue(k), to_residue(v), span)
        outs.append(o.reshape(B, d, H, L, E).transpose(0, 3, 1, 2, 4).reshape(B, S, H, E))
        maxes.append(m.reshape(B, d, H, L).transpose(0, 3, 1, 2).reshape(B, S, H))
        dens.append(l.reshape(B, d, H, L).transpose(0, 3, 1, 2).reshape(B, S, H))
    m_all = jnp.stack(maxes, axis=0)
    m_glob = jnp.max(m_all, axis=0)
    w = jnp.exp(m_all - m_glob[None])
    num = sum(w[i][..., None] * outs[i] for i in range(len(DILATIONS)))
    den = sum(w[i] * dens[i] for i in range(len(DILATIONS)))
    return (num / den[..., None]).astype(q.dtype)


def _ssm_combine(left, right):
    ar_l, ai_l, br_l, bi_l = left
    ar_r, ai_r, br_r, bi_r = right
    return (ar_r * ar_l - ai_r * ai_l,
            ar_r * ai_l + ai_r * ar_l,
            ar_r * br_l - ai_r * bi_l + br_r,
            ar_r * bi_l + ai_r * br_l + bi_r)


def s5_mixer(u, lam_re, lam_im, log_dt, b_re, b_im, c_re, c_im, d_skip, w_glu, b_glu):
    B, S, _ = u.shape
    G, P, C = N_SSM_GROUPS, SSM_STATE, SSM_GROUP
    uf = u.astype(jnp.float32).reshape(B, S, G, C)
    lr = lam_re.astype(jnp.float32)
    li = lam_im.astype(jnp.float32)
    dt = jnp.exp(log_dt.astype(jnp.float32))[:, None]
    mag = jnp.exp(lr * dt)
    ar = mag * jnp.cos(li * dt)
    ai = mag * jnp.sin(li * dt)
    nr, ni = ar - 1.0, ai
    den = lr * lr + li * li
    cr = (nr * lr + ni * li) / den
    ci = (ni * lr - nr * li) / den
    br, bi = b_re.astype(jnp.float32), b_im.astype(jnp.float32)
    bbr = cr[..., None] * br - ci[..., None] * bi
    bbi = cr[..., None] * bi + ci[..., None] * br
    xr = jnp.einsum('gpc,bsgc->bsgp', bbr, uf)
    xi = jnp.einsum('gpc,bsgc->bsgp', bbi, uf)
    a_r = jnp.broadcast_to(ar[None, None], (1, S, G, P))
    a_i = jnp.broadcast_to(ai[None, None], (1, S, G, P))
    _, _, hr, hi = lax.associative_scan(_ssm_combine, (a_r, a_i, xr, xi), axis=1)
    y = (jnp.einsum('gcp,bsgp->bsgc', c_re.astype(jnp.float32), hr)
         - jnp.einsum('gcp,bsgp->bsgc', c_im.astype(jnp.float32), hi)
         + d_skip.astype(jnp.float32).reshape(G, C) * uf)
    y = jax.nn.gelu(y.reshape(B, S, SSM_WIDTH)).astype(u.dtype)
    return y * jax.nn.sigmoid(y @ w_glu + b_glu)


def _fwd_setup_inputs(seed: int = 0) -> dict:
    key = jax.random.key(seed)
    ks = iter(jax.random.split(key, 40))

    def nrm(shape, scale):
        return jax.random.normal(next(ks), shape, jnp.float32) * scale

    def gain(shape):
        return 1.0 + nrm(shape, 0.02)

    L_ = DEPTH
    G, P, C = N_SSM_GROUPS, SSM_STATE, SSM_GROUP
    return {
        'x': nrm((BATCH, SEQ, D_MODEL), 1.0),
        'p': nrm((DEPTH, BATCH, SEQ, PLE_DIM), 1.0),
        'ffn1_norm': gain((L_, D_MODEL)),
        'ffn1_w_gate': nrm((L_, D_MODEL, D_FF), D_MODEL ** -0.5),
        'ffn1_w_up': nrm((L_, D_MODEL, D_FF), D_MODEL ** -0.5),
        'ffn1_w_down': nrm((L_, D_FF, D_MODEL), D_FF ** -0.5),
        'mix_norm': gain((L_, D_MODEL)),
        'w_in': nrm((L_, D_MODEL, 3 * ATTN_WIDTH + SSM_WIDTH), D_MODEL ** -0.5),
        'attn_out_norm': gain((L_, ATTN_WIDTH)),
        'ssm_lambda_re': -0.5 + nrm((L_, G, P), 0.01),
        'ssm_lambda_im': math.pi * jnp.arange(P, dtype=jnp.float32)[None, None, :] + nrm((L_, G, P), 0.01),
        'ssm_log_dt': jax.random.uniform(next(ks), (L_, G), jnp.float32, math.log(DT_MIN), math.log(DT_MAX)),
        'ssm_b_re': nrm((L_, G, P, C), (2.0 * C) ** -0.5),
        'ssm_b_im': nrm((L_, G, P, C), (2.0 * C) ** -0.5),
        'ssm_c_re': nrm((L_, G, C, P), (2.0 * P) ** -0.5),
        'ssm_c_im': nrm((L_, G, C, P), (2.0 * P) ** -0.5),
        'ssm_d': nrm((L_, SSM_WIDTH), 1.0),
        'ssm_w_glu': nrm((L_, SSM_WIDTH, SSM_WIDTH), SSM_WIDTH ** -0.5),
        'ssm_b_glu': nrm((L_, SSM_WIDTH), 0.01),
        'ssm_out_norm': gain((L_, SSM_WIDTH)),
        'w_out': nrm((L_, MIX_WIDTH, D_MODEL), MIX_WIDTH ** -0.5),
        'ffn2_norm': gain((L_, D_MODEL)),
        'ffn2_w_gate': nrm((L_, D_MODEL, D_FF), D_MODEL ** -0.5),
        'ffn2_w_up': nrm((L_, D_MODEL, D_FF), D_MODEL ** -0.5),
        'ffn2_w_down': nrm((L_, D_FF, D_MODEL), D_FF ** -0.5),
        'ple_norm': gain((L_, D_MODEL)),
        'ple_w_gate': nrm((L_, D_MODEL, D_MODEL), D_MODEL ** -0.5),
        'ple_w_proj': nrm((L_, PLE_DIM, D_MODEL), PLE_DIM ** -0.5),
        'final_norm': gain((D_MODEL,)),
    }


def _fwd_reference(x, p, ffn1_norm, ffn1_w_gate, ffn1_w_up, ffn1_w_down, mix_norm, w_in,
              attn_out_norm, ssm_lambda_re, ssm_lambda_im, ssm_log_dt, ssm_b_re, ssm_b_im,
              ssm_c_re, ssm_c_im, ssm_d, ssm_w_glu, ssm_b_glu, ssm_out_norm, w_out,
              ffn2_norm, ffn2_w_gate, ffn2_w_up, ffn2_w_down, ple_norm, ple_w_gate,
              ple_w_proj, final_norm):
    B, S, _ = x.shape
    h = x
    for i in range(DEPTH):
        h = h + 0.5 * swiglu(rms_norm(h, ffn1_norm[i]), ffn1_w_gate[i], ffn1_w_up[i], ffn1_w_down[i])
        u = rms_norm(h, mix_norm[i])
        z = u @ w_in[i]
        q = z[..., :ATTN_WIDTH].reshape(B, S, N_HEADS, HEAD_DIM)
        k = z[..., ATTN_WIDTH:2 * ATTN_WIDTH].reshape(B, S, N_HEADS, HEAD_DIM)
        v = z[..., 2 * ATTN_WIDTH:3 * ATTN_WIDTH].reshape(B, S, N_HEADS, HEAD_DIM)
        s_in = z[..., 3 * ATTN_WIDTH:]
        ya = dilated_attention(q, k, v).reshape(B, S, ATTN_WIDTH)
        yb = s5_mixer(s_in, ssm_lambda_re[i], ssm_lambda_im[i], ssm_log_dt[i], ssm_b_re[i], ssm_b_im[i],
                      ssm_c_re[i], ssm_c_im[i], ssm_d[i], ssm_w_glu[i], ssm_b_glu[i])
        y = jnp.concatenate([rms_norm(ya, attn_out_norm[i]), rms_norm(yb, ssm_out_norm[i])], axis=-1)
        h = h + y @ w_out[i]
        h = h + 0.5 * swiglu(rms_norm(h, ffn2_norm[i]), ffn2_w_gate[i], ffn2_w_up[i], ffn2_w_down[i])
        gate = jax.nn.sigmoid(rms_norm(h, ple_norm[i]) @ ple_w_gate[i])
        h = h + gate * (p[i] @ ple_w_proj[i])
    return rms_norm(h, final_norm)


import jax as _jax
import jax.numpy as _jnp

TWIN_FORMAT = 'train_step'
FWD_PARAMS = ['x', 'p', 'ffn1_norm', 'ffn1_w_gate', 'ffn1_w_up', 'ffn1_w_down', 'mix_norm', 'w_in', 'attn_out_norm', 'ssm_lambda_re', 'ssm_lambda_im', 'ssm_log_dt', 'ssm_b_re', 'ssm_b_im', 'ssm_c_re', 'ssm_c_im', 'ssm_d', 'ssm_w_glu', 'ssm_b_glu', 'ssm_out_norm', 'w_out', 'ffn2_norm', 'ffn2_w_gate', 'ffn2_w_up', 'ffn2_w_down', 'ple_norm', 'ple_w_gate', 'ple_w_proj', 'final_norm']
TWIN_WEIGHTS = ['ffn1_norm', 'ffn1_w_gate', 'ffn1_w_up', 'ffn1_w_down', 'mix_norm', 'w_in', 'attn_out_norm', 'ssm_lambda_re', 'ssm_lambda_im', 'ssm_log_dt', 'ssm_b_re', 'ssm_b_im', 'ssm_c_re', 'ssm_c_im', 'ssm_d', 'ssm_w_glu', 'ssm_b_glu', 'ssm_out_norm', 'w_out', 'ffn2_norm', 'ffn2_w_gate', 'ffn2_w_up', 'ffn2_w_down', 'ple_norm', 'ple_w_gate', 'ple_w_proj', 'final_norm']
TWIN_DIFF_INPUT = 'x'
TWIN_INPUTS = ['x', 'p', 'ffn1_norm', 'ffn1_w_gate', 'ffn1_w_up', 'ffn1_w_down', 'mix_norm', 'w_in', 'attn_out_norm', 'ssm_lambda_re', 'ssm_lambda_im', 'ssm_log_dt', 'ssm_b_re', 'ssm_b_im', 'ssm_c_re', 'ssm_c_im', 'ssm_d', 'ssm_w_glu', 'ssm_b_glu', 'ssm_out_norm', 'w_out', 'ffn2_norm', 'ffn2_w_gate', 'ffn2_w_up', 'ffn2_w_down', 'ple_norm', 'ple_w_gate', 'ple_w_proj', 'final_norm', 'loss_target', 'm_ffn1_norm', 'm_ffn1_w_gate', 'm_ffn1_w_up', 'm_ffn1_w_down', 'm_mix_norm', 'm_w_in', 'm_attn_out_norm', 'm_ssm_lambda_re', 'm_ssm_lambda_im', 'm_ssm_log_dt', 'm_ssm_b_re', 'm_ssm_b_im', 'm_ssm_c_re', 'm_ssm_c_im', 'm_ssm_d', 'm_ssm_w_glu', 'm_ssm_b_glu', 'm_ssm_out_norm', 'm_w_out', 'm_ffn2_norm', 'm_ffn2_w_gate', 'm_ffn2_w_up', 'm_ffn2_w_down', 'm_ple_norm', 'm_ple_w_gate', 'm_ple_w_proj', 'm_final_norm', 'v_ffn1_norm', 'v_ffn1_w_gate', 'v_ffn1_w_up', 'v_ffn1_w_down', 'v_mix_norm', 'v_w_in', 'v_attn_out_norm', 'v_ssm_lambda_re', 'v_ssm_lambda_im', 'v_ssm_log_dt', 'v_ssm_b_re', 'v_ssm_b_im', 'v_ssm_c_re', 'v_ssm_c_im', 'v_ssm_d', 'v_ssm_w_glu', 'v_ssm_b_glu', 'v_ssm_out_norm', 'v_w_out', 'v_ffn2_norm', 'v_ffn2_w_gate', 'v_ffn2_w_up', 'v_ffn2_w_down', 'v_ple_norm', 'v_ple_w_gate', 'v_ple_w_proj', 'v_final_norm']
TWIN_OUTPUTS = ['loss', 'grad_x', 'grad_ffn1_norm', 'grad_ffn1_w_gate', 'grad_ffn1_w_up', 'grad_ffn1_w_down', 'grad_mix_norm', 'grad_w_in', 'grad_attn_out_norm', 'grad_ssm_lambda_re', 'grad_ssm_lambda_im', 'grad_ssm_log_dt', 'grad_ssm_b_re', 'grad_ssm_b_im', 'grad_ssm_c_re', 'grad_ssm_c_im', 'grad_ssm_d', 'grad_ssm_w_glu', 'grad_ssm_b_glu', 'grad_ssm_out_norm', 'grad_w_out', 'grad_ffn2_norm', 'grad_ffn2_w_gate', 'grad_ffn2_w_up', 'grad_ffn2_w_down', 'grad_ple_norm', 'grad_ple_w_gate', 'grad_ple_w_proj', 'grad_final_norm', 'delta_ffn1_norm', 'delta_ffn1_w_gate', 'delta_ffn1_w_up', 'delta_ffn1_w_down', 'delta_mix_norm', 'delta_w_in', 'delta_attn_out_norm', 'delta_ssm_lambda_re', 'delta_ssm_lambda_im', 'delta_ssm_log_dt', 'delta_ssm_b_re', 'delta_ssm_b_im', 'delta_ssm_c_re', 'delta_ssm_c_im', 'delta_ssm_d', 'delta_ssm_w_glu', 'delta_ssm_b_glu', 'delta_ssm_out_norm', 'delta_w_out', 'delta_ffn2_norm', 'delta_ffn2_w_gate', 'delta_ffn2_w_up', 'delta_ffn2_w_down', 'delta_ple_norm', 'delta_ple_w_gate', 'delta_ple_w_proj', 'delta_final_norm', 'new_m_ffn1_norm', 'new_m_ffn1_w_gate', 'new_m_ffn1_w_up', 'new_m_ffn1_w_down', 'new_m_mix_norm', 'new_m_w_in', 'new_m_attn_out_norm', 'new_m_ssm_lambda_re', 'new_m_ssm_lambda_im', 'new_m_ssm_log_dt', 'new_m_ssm_b_re', 'new_m_ssm_b_im', 'new_m_ssm_c_re', 'new_m_ssm_c_im', 'new_m_ssm_d', 'new_m_ssm_w_glu', 'new_m_ssm_b_glu', 'new_m_ssm_out_norm', 'new_m_w_out', 'new_m_ffn2_norm', 'new_m_ffn2_w_gate', 'new_m_ffn2_w_up', 'new_m_ffn2_w_down', 'new_m_ple_norm', 'new_m_ple_w_gate', 'new_m_ple_w_proj', 'new_m_final_norm', 'new_v_ffn1_norm', 'new_v_ffn1_w_gate', 'new_v_ffn1_w_up', 'new_v_ffn1_w_down', 'new_v_mix_norm', 'new_v_w_in', 'new_v_attn_out_norm', 'new_v_ssm_lambda_re', 'new_v_ssm_lambda_im', 'new_v_ssm_log_dt', 'new_v_ssm_b_re', 'new_v_ssm_b_im', 'new_v_ssm_c_re', 'new_v_ssm_c_im', 'new_v_ssm_d', 'new_v_ssm_w_glu', 'new_v_ssm_b_glu', 'new_v_ssm_out_norm', 'new_v_w_out', 'new_v_ffn2_norm', 'new_v_ffn2_w_gate', 'new_v_ffn2_w_up', 'new_v_ffn2_w_down', 'new_v_ple_norm', 'new_v_ple_w_gate', 'new_v_ple_w_proj', 'new_v_final_norm']
TWIN_LEAF_KINDS = {'loss': 'loss', 'grad_x': 'grad_x', 'grad_ffn1_norm': 'grad_w', 'grad_ffn1_w_gate': 'grad_w', 'grad_ffn1_w_up': 'grad_w', 'grad_ffn1_w_down': 'grad_w', 'grad_mix_norm': 'grad_w', 'grad_w_in': 'grad_w', 'grad_attn_out_norm': 'grad_w', 'grad_ssm_lambda_re': 'grad_w', 'grad_ssm_lambda_im': 'grad_w', 'grad_ssm_log_dt': 'grad_w', 'grad_ssm_b_re': 'grad_w', 'grad_ssm_b_im': 'grad_w', 'grad_ssm_c_re': 'grad_w', 'grad_ssm_c_im': 'grad_w', 'grad_ssm_d': 'grad_w', 'grad_ssm_w_glu': 'grad_w', 'grad_ssm_b_glu': 'grad_w', 'grad_ssm_out_norm': 'grad_w', 'grad_w_out': 'grad_w', 'grad_ffn2_norm': 'grad_w', 'grad_ffn2_w_gate': 'grad_w', 'grad_ffn2_w_up': 'grad_w', 'grad_ffn2_w_down': 'grad_w', 'grad_ple_norm': 'grad_w', 'grad_ple_w_gate': 'grad_w', 'grad_ple_w_proj': 'grad_w', 'grad_final_norm': 'grad_w', 'delta_ffn1_norm': 'delta_w', 'delta_ffn1_w_gate': 'delta_w', 'delta_ffn1_w_up': 'delta_w', 'delta_ffn1_w_down': 'delta_w', 'delta_mix_norm': 'delta_w', 'delta_w_in': 'delta_w', 'delta_attn_out_norm': 'delta_w', 'delta_ssm_lambda_re': 'delta_w', 'delta_ssm_lambda_im': 'delta_w', 'delta_ssm_log_dt': 'delta_w', 'delta_ssm_b_re': 'delta_w', 'delta_ssm_b_im': 'delta_w', 'delta_ssm_c_re': 'delta_w', 'delta_ssm_c_im': 'delta_w', 'delta_ssm_d': 'delta_w', 'delta_ssm_w_glu': 'delta_w', 'delta_ssm_b_glu': 'delta_w', 'delta_ssm_out_norm': 'delta_w', 'delta_w_out': 'delta_w', 'delta_ffn2_norm': 'delta_w', 'delta_ffn2_w_gate': 'delta_w', 'delta_ffn2_w_up': 'delta_w', 'delta_ffn2_w_down': 'delta_w', 'delta_ple_norm': 'delta_w', 'delta_ple_w_gate': 'delta_w', 'delta_ple_w_proj': 'delta_w', 'delta_final_norm': 'delta_w', 'new_m_ffn1_norm': 'new_m', 'new_m_ffn1_w_gate': 'new_m', 'new_m_ffn1_w_up': 'new_m', 'new_m_ffn1_w_down': 'new_m', 'new_m_mix_norm': 'new_m', 'new_m_w_in': 'new_m', 'new_m_attn_out_norm': 'new_m', 'new_m_ssm_lambda_re': 'new_m', 'new_m_ssm_lambda_im': 'new_m', 'new_m_ssm_log_dt': 'new_m', 'new_m_ssm_b_re': 'new_m', 'new_m_ssm_b_im': 'new_m', 'new_m_ssm_c_re': 'new_m', 'new_m_ssm_c_im': 'new_m', 'new_m_ssm_d': 'new_m', 'new_m_ssm_w_glu': 'new_m', 'new_m_ssm_b_glu': 'new_m', 'new_m_ssm_out_norm': 'new_m', 'new_m_w_out': 'new_m', 'new_m_ffn2_norm': 'new_m', 'new_m_ffn2_w_gate': 'new_m', 'new_m_ffn2_w_up': 'new_m', 'new_m_ffn2_w_down': 'new_m', 'new_m_ple_norm': 'new_m', 'new_m_ple_w_gate': 'new_m', 'new_m_ple_w_proj': 'new_m', 'new_m_final_norm': 'new_m', 'new_v_ffn1_norm': 'new_v', 'new_v_ffn1_w_gate': 'new_v', 'new_v_ffn1_w_up': 'new_v', 'new_v_ffn1_w_down': 'new_v', 'new_v_mix_norm': 'new_v', 'new_v_w_in': 'new_v', 'new_v_attn_out_norm': 'new_v', 'new_v_ssm_lambda_re': 'new_v', 'new_v_ssm_lambda_im': 'new_v', 'new_v_ssm_log_dt': 'new_v', 'new_v_ssm_b_re': 'new_v', 'new_v_ssm_b_im': 'new_v', 'new_v_ssm_c_re': 'new_v', 'new_v_ssm_c_im': 'new_v', 'new_v_ssm_d': 'new_v', 'new_v_ssm_w_glu': 'new_v', 'new_v_ssm_b_glu': 'new_v', 'new_v_ssm_out_norm': 'new_v', 'new_v_w_out': 'new_v', 'new_v_ffn2_norm': 'new_v', 'new_v_ffn2_w_gate': 'new_v', 'new_v_ffn2_w_up': 'new_v', 'new_v_ffn2_w_down': 'new_v', 'new_v_ple_norm': 'new_v', 'new_v_ple_w_gate': 'new_v', 'new_v_ple_w_proj': 'new_v', 'new_v_final_norm': 'new_v'}


def _forward(args):
    return _fwd_reference(*[args[k] for k in FWD_PARAMS])


def _output_shape():
    def fwd():
        inp = _fwd_setup_inputs(0)
        return _fwd_reference(*[inp[k] for k in FWD_PARAMS])
    out = _jax.eval_shape(fwd)
    return out.shape, out.dtype

N_MICROBATCH = 1
ADAM_LR = 0.001
ADAM_B1 = 0.9
ADAM_B2 = 0.999
ADAM_EPS = 1e-08
ADAM_WD = 0.01
ADAM_STEP = 10
PER_EXAMPLE_BATCH_AXIS = {'x': 0, 'p': 1, 'loss_target': 0}
SHARED_INPUTS = []
_WEIGHT_DTYPES = {'ffn1_norm': _jnp.float32, 'ffn1_w_gate': _jnp.float32, 'ffn1_w_up': _jnp.float32, 'ffn1_w_down': _jnp.float32, 'mix_norm': _jnp.float32, 'w_in': _jnp.float32, 'attn_out_norm': _jnp.float32, 'ssm_lambda_re': _jnp.float32, 'ssm_lambda_im': _jnp.float32, 'ssm_log_dt': _jnp.float32, 'ssm_b_re': _jnp.float32, 'ssm_b_im': _jnp.float32, 'ssm_c_re': _jnp.float32, 'ssm_c_im': _jnp.float32, 'ssm_d': _jnp.float32, 'ssm_w_glu': _jnp.float32, 'ssm_b_glu': _jnp.float32, 'ssm_out_norm': _jnp.float32, 'w_out': _jnp.float32, 'ffn2_norm': _jnp.float32, 'ffn2_w_gate': _jnp.float32, 'ffn2_w_up': _jnp.float32, 'ffn2_w_down': _jnp.float32, 'ple_norm': _jnp.float32, 'ple_w_gate': _jnp.float32, 'ple_w_proj': _jnp.float32, 'final_norm': _jnp.float32}
MOMENT_SCALE = {'ffn1_norm': 6.746570e-02, 'ffn1_w_gate': 2.701007e-02, 'ffn1_w_up': 2.619539e-02, 'ffn1_w_down': 4.290510e-02, 'mix_norm': 1.232865e-01, 'w_in': 8.663973e-02, 'attn_out_norm': 8.780294e-02, 'ssm_lambda_re': 4.864107e-03, 'ssm_lambda_im': 5.117978e-03, 'ssm_log_dt': 6.773311e+00, 'ssm_b_re': 3.017217e-03, 'ssm_b_im': 2.936963e-03, 'ssm_c_re': 6.054444e-03, 'ssm_c_im': 6.034844e-03, 'ssm_d': 1.100185e-01, 'ssm_w_glu': 2.522003e-02, 'ssm_b_glu': 4.258190e-02, 'ssm_out_norm': 9.241335e-02, 'w_out': 8.995866e-02, 'ffn2_norm': 3.497258e-02, 'ffn2_w_gate': 1.517128e-02, 'ffn2_w_up': 1.470841e-02, 'ffn2_w_down': 2.416167e-02, 'ple_norm': 1.810862e-02, 'ple_w_gate': 1.803335e-02, 'ple_w_proj': 4.353756e-02, 'final_norm': 3.204761e+01}


def _to_microbatches(a, axis):
    t = _jnp.moveaxis(a, axis, 0)
    t = t.reshape((N_MICROBATCH, t.shape[0] // N_MICROBATCH) + t.shape[1:])
    return _jnp.moveaxis(t, 1, axis + 1)


def setup_inputs(seed: int = 0) -> dict:
    inp = _fwd_setup_inputs(seed)
    key = _jax.random.fold_in(_jax.random.key(seed), 7919)
    shape, _ = _output_shape()
    out = dict(inp)
    out["loss_target"] = _jax.random.normal(_jax.random.fold_in(key, 0), shape, _jnp.float32)
    for i, name in enumerate(TWIN_WEIGHTS):
        w = inp[name].astype(_jnp.float32)
        if MOMENT_SCALE is None:
            s = _jnp.sqrt(_jnp.mean(_jnp.square(w)) + 1e-30)
        else:
            s = MOMENT_SCALE[name]
        km, kv = _jax.random.split(_jax.random.fold_in(key, i + 1))
        out[name] = w
        out["m_" + name] = s * _jax.random.normal(km, w.shape, _jnp.float32)
        out["v_" + name] = (s * s) * _jax.random.uniform(kv, w.shape, _jnp.float32, 0.5, 1.5)
    if N_MICROBATCH > 1:
        for name, axis in PER_EXAMPLE_BATCH_AXIS.items():
            out[name] = _to_microbatches(out[name], axis)
    return {'x': out['x'], 'p': out['p'], 'ffn1_norm': out['ffn1_norm'], 'ffn1_w_gate': out['ffn1_w_gate'], 'ffn1_w_up': out['ffn1_w_up'], 'ffn1_w_down': out['ffn1_w_down'], 'mix_norm': out['mix_norm'], 'w_in': out['w_in'], 'attn_out_norm': out['attn_out_norm'], 'ssm_lambda_re': out['ssm_lambda_re'], 'ssm_lambda_im': out['ssm_lambda_im'], 'ssm_log_dt': out['ssm_log_dt'], 'ssm_b_re': out['ssm_b_re'], 'ssm_b_im': out['ssm_b_im'], 'ssm_c_re': out['ssm_c_re'], 'ssm_c_im': out['ssm_c_im'], 'ssm_d': out['ssm_d'], 'ssm_w_glu': out['ssm_w_glu'], 'ssm_b_glu': out['ssm_b_glu'], 'ssm_out_norm': out['ssm_out_norm'], 'w_out': out['w_out'], 'ffn2_norm': out['ffn2_norm'], 'ffn2_w_gate': out['ffn2_w_gate'], 'ffn2_w_up': out['ffn2_w_up'], 'ffn2_w_down': out['ffn2_w_down'], 'ple_norm': out['ple_norm'], 'ple_w_gate': out['ple_w_gate'], 'ple_w_proj': out['ple_w_proj'], 'final_norm': out['final_norm'], 'loss_target': out['loss_target'], 'm_ffn1_norm': out['m_ffn1_norm'], 'm_ffn1_w_gate': out['m_ffn1_w_gate'], 'm_ffn1_w_up': out['m_ffn1_w_up'], 'm_ffn1_w_down': out['m_ffn1_w_down'], 'm_mix_norm': out['m_mix_norm'], 'm_w_in': out['m_w_in'], 'm_attn_out_norm': out['m_attn_out_norm'], 'm_ssm_lambda_re': out['m_ssm_lambda_re'], 'm_ssm_lambda_im': out['m_ssm_lambda_im'], 'm_ssm_log_dt': out['m_ssm_log_dt'], 'm_ssm_b_re': out['m_ssm_b_re'], 'm_ssm_b_im': out['m_ssm_b_im'], 'm_ssm_c_re': out['m_ssm_c_re'], 'm_ssm_c_im': out['m_ssm_c_im'], 'm_ssm_d': out['m_ssm_d'], 'm_ssm_w_glu': out['m_ssm_w_glu'], 'm_ssm_b_glu': out['m_ssm_b_glu'], 'm_ssm_out_norm': out['m_ssm_out_norm'], 'm_w_out': out['m_w_out'], 'm_ffn2_norm': out['m_ffn2_norm'], 'm_ffn2_w_gate': out['m_ffn2_w_gate'], 'm_ffn2_w_up': out['m_ffn2_w_up'], 'm_ffn2_w_down': out['m_ffn2_w_down'], 'm_ple_norm': out['m_ple_norm'], 'm_ple_w_gate': out['m_ple_w_gate'], 'm_ple_w_proj': out['m_ple_w_proj'], 'm_final_norm': out['m_final_norm'], 'v_ffn1_norm': out['v_ffn1_norm'], 'v_ffn1_w_gate': out['v_ffn1_w_gate'], 'v_ffn1_w_up': out['v_ffn1_w_up'], 'v_ffn1_w_down': out['v_ffn1_w_down'], 'v_mix_norm': out['v_mix_norm'], 'v_w_in': out['v_w_in'], 'v_attn_out_norm': out['v_attn_out_norm'], 'v_ssm_lambda_re': out['v_ssm_lambda_re'], 'v_ssm_lambda_im': out['v_ssm_lambda_im'], 'v_ssm_log_dt': out['v_ssm_log_dt'], 'v_ssm_b_re': out['v_ssm_b_re'], 'v_ssm_b_im': out['v_ssm_b_im'], 'v_ssm_c_re': out['v_ssm_c_re'], 'v_ssm_c_im': out['v_ssm_c_im'], 'v_ssm_d': out['v_ssm_d'], 'v_ssm_w_glu': out['v_ssm_w_glu'], 'v_ssm_b_glu': out['v_ssm_b_glu'], 'v_ssm_out_norm': out['v_ssm_out_norm'], 'v_w_out': out['v_w_out'], 'v_ffn2_norm': out['v_ffn2_norm'], 'v_ffn2_w_gate': out['v_ffn2_w_gate'], 'v_ffn2_w_up': out['v_ffn2_w_up'], 'v_ffn2_w_down': out['v_ffn2_w_down'], 'v_ple_norm': out['v_ple_norm'], 'v_ple_w_gate': out['v_ple_w_gate'], 'v_ple_w_proj': out['v_ple_w_proj'], 'v_final_norm': out['v_final_norm']}


def _loss(weights, diff, rest, loss_target):
    with _jax.named_scope("forward"):
        args = {**rest, TWIN_DIFF_INPUT: diff, **{k: w.astype(_WEIGHT_DTYPES[k]) for k, w in weights.items()}}
        y = _forward(args)
    with _jax.named_scope("loss_head"):
        err = _jnp.square(y.astype(_jnp.float32) - loss_target)
        return 0.5 * _jnp.sum(_jnp.mean(err, axis=-1)) if err.ndim else 0.5 * err


def _adamw(w, g, m, v):
    m = ADAM_B1 * m + (1.0 - ADAM_B1) * g
    v = ADAM_B2 * v + (1.0 - ADAM_B2) * _jnp.square(g)
    m_hat = m / (1.0 - ADAM_B1 ** ADAM_STEP)
    v_hat = v / (1.0 - ADAM_B2 ** ADAM_STEP)
    delta = -ADAM_LR * (m_hat / (_jnp.sqrt(v_hat) + ADAM_EPS) + ADAM_WD * w)
    return delta, m, v


def reference(x, p, ffn1_norm, ffn1_w_gate, ffn1_w_up, ffn1_w_down, mix_norm, w_in, attn_out_norm, ssm_lambda_re, ssm_lambda_im, ssm_log_dt, ssm_b_re, ssm_b_im, ssm_c_re, ssm_c_im, ssm_d, ssm_w_glu, ssm_b_glu, ssm_out_norm, w_out, ffn2_norm, ffn2_w_gate, ffn2_w_up, ffn2_w_down, ple_norm, ple_w_gate, ple_w_proj, final_norm, loss_target, m_ffn1_norm, m_ffn1_w_gate, m_ffn1_w_up, m_ffn1_w_down, m_mix_norm, m_w_in, m_attn_out_norm, m_ssm_lambda_re, m_ssm_lambda_im, m_ssm_log_dt, m_ssm_b_re, m_ssm_b_im, m_ssm_c_re, m_ssm_c_im, m_ssm_d, m_ssm_w_glu, m_ssm_b_glu, m_ssm_out_norm, m_w_out, m_ffn2_norm, m_ffn2_w_gate, m_ffn2_w_up, m_ffn2_w_down, m_ple_norm, m_ple_w_gate, m_ple_w_proj, m_final_norm, v_ffn1_norm, v_ffn1_w_gate, v_ffn1_w_up, v_ffn1_w_down, v_mix_norm, v_w_in, v_attn_out_norm, v_ssm_lambda_re, v_ssm_lambda_im, v_ssm_log_dt, v_ssm_b_re, v_ssm_b_im, v_ssm_c_re, v_ssm_c_im, v_ssm_d, v_ssm_w_glu, v_ssm_b_glu, v_ssm_out_norm, v_w_out, v_ffn2_norm, v_ffn2_w_gate, v_ffn2_w_up, v_ffn2_w_down, v_ple_norm, v_ple_w_gate, v_ple_w_proj, v_final_norm):
    given = dict(x=x, p=p, ffn1_norm=ffn1_norm, ffn1_w_gate=ffn1_w_gate, ffn1_w_up=ffn1_w_up, ffn1_w_down=ffn1_w_down, mix_norm=mix_norm, w_in=w_in, attn_out_norm=attn_out_norm, ssm_lambda_re=ssm_lambda_re, ssm_lambda_im=ssm_lambda_im, ssm_log_dt=ssm_log_dt, ssm_b_re=ssm_b_re, ssm_b_im=ssm_b_im, ssm_c_re=ssm_c_re, ssm_c_im=ssm_c_im, ssm_d=ssm_d, ssm_w_glu=ssm_w_glu, ssm_b_glu=ssm_b_glu, ssm_out_norm=ssm_out_norm, w_out=w_out, ffn2_norm=ffn2_norm, ffn2_w_gate=ffn2_w_gate, ffn2_w_up=ffn2_w_up, ffn2_w_down=ffn2_w_down, ple_norm=ple_norm, ple_w_gate=ple_w_gate, ple_w_proj=ple_w_proj, final_norm=final_norm, loss_target=loss_target, m_ffn1_norm=m_ffn1_norm, m_ffn1_w_gate=m_ffn1_w_gate, m_ffn1_w_up=m_ffn1_w_up, m_ffn1_w_down=m_ffn1_w_down, m_mix_norm=m_mix_norm, m_w_in=m_w_in, m_attn_out_norm=m_attn_out_norm, m_ssm_lambda_re=m_ssm_lambda_re, m_ssm_lambda_im=m_ssm_lambda_im, m_ssm_log_dt=m_ssm_log_dt, m_ssm_b_re=m_ssm_b_re, m_ssm_b_im=m_ssm_b_im, m_ssm_c_re=m_ssm_c_re, m_ssm_c_im=m_ssm_c_im, m_ssm_d=m_ssm_d, m_ssm_w_glu=m_ssm_w_glu, m_ssm_b_glu=m_ssm_b_glu, m_ssm_out_norm=m_ssm_out_norm, m_w_out=m_w_out, m_ffn2_norm=m_ffn2_norm, m_ffn2_w_gate=m_ffn2_w_gate, m_ffn2_w_up=m_ffn2_w_up, m_ffn2_w_down=m_ffn2_w_down, m_ple_norm=m_ple_norm, m_ple_w_gate=m_ple_w_gate, m_ple_w_proj=m_ple_w_proj, m_final_norm=m_final_norm, v_ffn1_norm=v_ffn1_norm, v_ffn1_w_gate=v_ffn1_w_gate, v_ffn1_w_up=v_ffn1_w_up, v_ffn1_w_down=v_ffn1_w_down, v_mix_norm=v_mix_norm, v_w_in=v_w_in, v_attn_out_norm=v_attn_out_norm, v_ssm_lambda_re=v_ssm_lambda_re, v_ssm_lambda_im=v_ssm_lambda_im, v_ssm_log_dt=v_ssm_log_dt, v_ssm_b_re=v_ssm_b_re, v_ssm_b_im=v_ssm_b_im, v_ssm_c_re=v_ssm_c_re, v_ssm_c_im=v_ssm_c_im, v_ssm_d=v_ssm_d, v_ssm_w_glu=v_ssm_w_glu, v_ssm_b_glu=v_ssm_b_glu, v_ssm_out_norm=v_ssm_out_norm, v_w_out=v_w_out, v_ffn2_norm=v_ffn2_norm, v_ffn2_w_gate=v_ffn2_w_gate, v_ffn2_w_up=v_ffn2_w_up, v_ffn2_w_down=v_ffn2_w_down, v_ple_norm=v_ple_norm, v_ple_w_gate=v_ple_w_gate, v_ple_w_proj=v_ple_w_proj, v_final_norm=v_final_norm)
    weights = {n: given[n] for n in TWIN_WEIGHTS}
    shared = {n: given[n] for n in SHARED_INPUTS}
    per_example = {n: given[n] for n in ['x', 'p']}
    grad_fn = _jax.value_and_grad(_loss, argnums=(0, 1))

    def one_microbatch(ex, loss_target):
        ex = dict(ex)
        diff = ex.pop(TWIN_DIFF_INPUT)
        return grad_fn(weights, diff, {**shared, **ex}, loss_target)

    if N_MICROBATCH == 1:
        loss, (grad_w, grad_x) = one_microbatch(per_example, given["loss_target"])
    else:
        def body(carry, xs):
            loss_sum, grad_sum = carry
            l_k, (gw_k, gx_k) = one_microbatch(xs[0], xs[1])
            with _jax.named_scope("update"):
                return (loss_sum + l_k, _jax.tree.map(_jnp.add, grad_sum, gw_k)), gx_k

        init = (_jnp.zeros((), _jnp.float32), _jax.tree.map(_jnp.zeros_like, weights))
        (loss, grad_w), grad_x = _jax.lax.scan(body, init, (per_example, given["loss_target"]))
    with _jax.named_scope("update"):
        delta_w, new_m, new_v = {}, {}, {}
        for n in TWIN_WEIGHTS:
            delta_w[n], new_m[n], new_v[n] = _adamw(weights[n], grad_w[n], given["m_" + n], given["v_" + n])
    return (loss, grad_x, *[grad_w[n] for n in TWIN_WEIGHTS], *[delta_w[n] for n in TWIN_WEIGHTS],
            *[new_m[n] for n in TWIN_WEIGHTS], *[new_v[n] for n in TWIN_WEIGHTS])
```

```python
import functools
import math

import jax
import jax.numpy as jnp
from jax import lax
from jax.experimental import pallas as pl
from jax.experimental.pallas import tpu as pltpu

F32, BF16 = jnp.float32, jnp.bfloat16
MESH = pl.DeviceIdType.MESH
NDEV = 8
AXES = ("x", "y", "c")
LANE = 128
VMEM_LIMIT = 56 * 1024 * 1024

ATTN_W = 1024
HEAD_DIM = 64
SSM_W = 1024
MIX_W = ATTN_W + SSM_W
SSM_G, SSM_P, SSM_C = 64, 64, 16
PACK = 8
DILATIONS = (1, 4, 16)
QB = 128
NORM_EPS = 1e-6
MASK_VALUE = -1e30
LR, B1, B2, EPS, WD, STEP = 0.001, 0.9, 0.999, 1e-08, 0.01, 10


def _cp(sem=None):
    return pltpu.CompilerParams(dimension_semantics=sem, vmem_limit_bytes=VMEM_LIMIT)


def _tile(n, target, mult=8):
    if n <= target:
        return n
    for t in range(target - target % mult, 0, -mult):
        if n % t == 0:
            return t
    return n


def _rms(x, g):
    return x * lax.rsqrt(jnp.mean(x * x, axis=-1, keepdims=True) + NORM_EPS) * g


def _rowwise(name, fn, S, tr, rows, fulls, outs, accs=(), ncol=1):
    nr, nf, no, na = len(rows), len(fulls), len(outs), len(accs)

    def body(*refs):
        ins = [r[...] for r in refs[:nr + nf]]
        o_refs = refs[nr + nf:nr + nf + no]
        a_refs = refs[nr + nf + no:]
        o_vals, a_vals = fn(*ins)
        for r, v in zip(o_refs, o_vals):
            r[...] = v.astype(r.dtype)
        if na:
            @pl.when(pl.program_id(1) == 0)
            def _():
                for r in a_refs:
                    r[...] = jnp.zeros_like(r)
            for r, v in zip(a_refs, a_vals):
                r[...] += v

    in_specs = [pl.BlockSpec((tr, w), functools.partial(lambda j, i, cm: (i, cm(j)), cm=cm)) for _, w, cm in rows]
    in_specs += [pl.BlockSpec(f.shape, functools.partial(lambda j, i, nd: (0,) * nd, nd=f.ndim)) for f in fulls]
    out_specs = [pl.BlockSpec((tr, w), functools.partial(lambda j, i, cm: (i, cm(j)), cm=cm)) for _, w, cm, _ in outs]
    out_specs += [pl.BlockSpec((1, w), functools.partial(lambda j, i, cm: (0, cm(j)), cm=cm)) for _, w, cm in accs]
    out_shape = [jax.ShapeDtypeStruct((S, c), dt) for c, _, _, dt in outs]
    out_shape += [jax.ShapeDtypeStruct((1, c), F32) for c, _, _ in accs]
    res = pl.pallas_call(
        body, name=name, grid=(ncol, S // tr), in_specs=in_specs, out_specs=out_specs, out_shape=out_shape,
        compiler_params=_cp(("parallel", "arbitrary" if na else "parallel")),
    )(*[a for a, _, _ in rows], *fulls)
    return res[:no], res[no:]


def _c0(j):
    return 0


def _full(a):
    return (a, a.shape[1], _c0)


def _mm_nn(name, a, w, *, out_dtype=F32, tm=512, tn=768, tk=2048, res=None, scale=1.0):
    M, K = a.shape
    J, K2, Np = w.shape
    assert K == K2
    tm, tn, tk = _tile(M, tm), _tile(Np, tn, LANE), _tile(K, tk, LANE)
    npj = Np // tn
    nk = K // tk

    def body(*refs):
        if res is None:
            a_ref, w_ref, o_ref, acc = refs
        else:
            a_ref, w_ref, r_ref, o_ref, acc = refs
        k = pl.program_id(2)

        @pl.when(k == 0)
        def _():
            acc[...] = jnp.zeros_like(acc)

        acc[...] += jnp.dot(a_ref[...].astype(BF16), w_ref[...], preferred_element_type=F32)

        @pl.when(k == nk - 1)
        def _():
            v = acc[...]
            if res is not None:
                v = r_ref[...] + scale * v
            o_ref[...] = v.astype(o_ref.dtype)

    in_specs = [pl.BlockSpec((tm, tk), lambda i, n, k: (i, k)),
                pl.BlockSpec((None, tk, tn), lambda i, n, k: (n // npj, k, n % npj))]
    args = [a, w]
    if res is not None:
        in_specs.append(pl.BlockSpec((tm, tn), lambda i, n, k: (i, n)))
        args.append(res)
    return pl.pallas_call(
        body, name=name, grid=(M // tm, J * npj, nk), in_specs=in_specs,
        out_specs=pl.BlockSpec((tm, tn), lambda i, n, k: (i, n)),
        out_shape=jax.ShapeDtypeStruct((M, J * Np), out_dtype),
        scratch_shapes=[pltpu.VMEM((tm, tn), F32)],
        compiler_params=_cp(("parallel", "parallel", "arbitrary")),
    )(*args)


def _mm_nt(name, dy, w, *, out_dtype=F32, tm=512, tn=2048, tk=768, scale=1.0):
    M, N = dy.shape
    J, K, Np = w.shape
    assert N == J * Np
    tm, tn, tk = _tile(M, tm), _tile(K, tn, LANE), _tile(Np, tk, LANE)
    npj = Np // tk
    nc = J * npj

    def body(a_ref, w_ref, o_ref, acc):
        c = pl.program_id(2)

        @pl.when(c == 0)
        def _():
            acc[...] = jnp.zeros_like(acc)

        acc[...] += lax.dot_general(a_ref[...].astype(BF16), w_ref[...], (((1,), (1,)), ((), ())),
                                    preferred_element_type=F32)

        @pl.when(c == nc - 1)
        def _():
            o_ref[...] = (scale * acc[...]).astype(o_ref.dtype)

    return pl.pallas_call(
        body, name=name, grid=(M // tm, K // tn, nc),
        in_specs=[pl.BlockSpec((tm, tk), lambda i, n, c: (i, c)),
                  pl.BlockSpec((None, tn, tk), lambda i, n, c: (c // npj, n, c % npj))],
        out_specs=pl.BlockSpec((tm, tn), lambda i, n, c: (i, n)),
        out_shape=jax.ShapeDtypeStruct((M, K), out_dtype),
        scratch_shapes=[pltpu.VMEM((tm, tn), F32)],
        compiler_params=_cp(("parallel", "parallel", "arbitrary")),
    )(dy, w)


def _mm_tn(name, x, dy, J, *, tm=1024, tko=1024, tn=768, scale=1.0):
    M, K = x.shape
    M2, N = dy.shape
    assert M == M2 and N % J == 0
    Np = N // J
    tm, tko, tn = _tile(M, tm, LANE), _tile(K, tko, LANE), _tile(Np, tn, LANE)
    npj = Np // tn
    nm = M // tm

    def body(x_ref, d_ref, o_ref, acc):
        m = pl.program_id(2)

        @pl.when(m == 0)
        def _():
            acc[...] = jnp.zeros_like(acc)

        acc[...] += lax.dot_general(x_ref[...].astype(BF16), d_ref[...].astype(BF16), (((0,), (0,)), ((), ())),
                                    preferred_element_type=F32)

        @pl.when(m == nm - 1)
        def _():
            o_ref[...] = scale * acc[...]

    return pl.pallas_call(
        body, name=name, grid=(K // tko, J * npj, nm),
        in_specs=[pl.BlockSpec((tm, tko), lambda k, n, m: (m, k)),
                  pl.BlockSpec((tm, tn), lambda k, n, m: (m, n))],
        out_specs=pl.BlockSpec((None, tko, tn), lambda k, n, m: (n // npj, k, n % npj)),
        out_shape=jax.ShapeDtypeStruct((J, K, Np), F32),
        scratch_shapes=[pltpu.VMEM((tko, tn), F32)],
        compiler_params=_cp(("parallel", "parallel", "arbitrary")),
    )(x, dy)


def _all_gather(name, shards):
    n = len(shards)

    def body(*refs):
        ins, outs = refs[:n], refs[n:2 * n]
        send_sems, recv_sems, local_sems = refs[2 * n:]
        x, y, c = lax.axis_index("x"), lax.axis_index("y"), lax.axis_index("c")
        me, sibling = (x, y, c), (x, y, 1 - c)
        chips = [(1 - x, y), (x, 1 - y), (1 - x, 1 - y)]

        def blk(i, px, py, pc):
            return outs[i].at[4 * px + 2 * py + pc]

        def copy(i, k, block, to, src=None):
            return pltpu.make_async_remote_copy(
                src_ref=blk(i, *block) if src is None else src, dst_ref=blk(i, *block),
                send_sem=send_sems.at[i, k], recv_sem=recv_sems.at[i, k], device_id=to, device_id_type=MESH)

        mine = [pltpu.make_async_copy(ins[i], blk(i, *me), local_sems.at[i]) for i in range(n)]
        for cp in mine:
            cp.start()
        first = []
        for i in range(n):
            first.append(copy(i, 0, me, sibling, src=ins[i]))
            first += [copy(i, 1 + j, me, (*chip, c), src=ins[i]) for j, chip in enumerate(chips)]
        for cp in first:
            cp.start()
        passed = []
        for i in range(n):
            for j, chip in enumerate(chips):
                copy(i, 1 + j, (*chip, c), me).wait_recv()
                fwd = copy(i, 4 + j, (*chip, c), sibling)
                fwd.start()
                passed.append(fwd)
        for i in range(n):
            copy(i, 0, sibling, me).wait_recv()
            for j, chip in enumerate(chips):
                copy(i, 4 + j, (*chip, 1 - c), me).wait_recv()
        for cp in first + passed:
            cp.wait_send()
        for cp in mine:
            cp.wait()

    any_spec = pl.BlockSpec(memory_space=pl.ANY)
    return pl.pallas_call(
        body, name=name, in_specs=[any_spec] * n, out_specs=[any_spec] * n,
        out_shape=[jax.ShapeDtypeStruct((NDEV,) + s.shape, s.dtype) for s in shards],
        scratch_shapes=[pltpu.SemaphoreType.DMA((n, 7)), pltpu.SemaphoreType.DMA((n, 7)), pltpu.SemaphoreType.DMA((n,))],
    )(*shards)


def _exchange(name, bufs, axis):
    n = len(bufs)
    views = [b.reshape((b.shape[0] // 2, 2) + b.shape[1:]) for b in bufs]

    def body(*refs):
        ins, kept, got = refs[:n], refs[n:2 * n], refs[2 * n:3 * n]
        send_sems, recv_sems, local_sems = refs[3 * n:]
        pos = {a: lax.axis_index(a) for a in AXES}
        mine = pos[axis]
        peer = tuple(1 - pos[a] if a == axis else pos[a] for a in AXES)
        sends = [pltpu.make_async_remote_copy(
            src_ref=ins[i].at[:, 1 - mine], dst_ref=got[i], send_sem=send_sems.at[i], recv_sem=recv_sems.at[i],
            device_id=peer, device_id_type=MESH) for i in range(n)]
        keeps = [pltpu.make_async_copy(ins[i].at[:, mine], kept[i], local_sems.at[i]) for i in range(n)]
        for cp in sends:
            cp.start()
        for cp in keeps:
            cp.start()
        for cp in sends:
            cp.wait()
        for cp in keeps:
            cp.wait()

    any_spec = pl.BlockSpec(memory_space=pl.ANY)
    half = [jax.ShapeDtypeStruct((v.shape[0],) + v.shape[2:], v.dtype) for v in views]
    res = pl.pallas_call(
        body, name=name, in_specs=[any_spec] * n, out_specs=[any_spec] * (2 * n), out_shape=half + half,
        scratch_shapes=[pltpu.SemaphoreType.DMA((n,)), pltpu.SemaphoreType.DMA((n,)), pltpu.SemaphoreType.DMA((n,))],
    )(*views)
    return res[:n], res[n:]


def _add(name, a, b):
    shp = a.shape
    a2, b2 = a.reshape(-1, shp[-1]), b.reshape(-1, shp[-1])
    R = a2.shape[0]
    (o,), _ = _rowwise(name, lambda p, q: ([p + q], []), R, _tile(R, 512), [_full(a2), _full(b2)], [],
                       [(shp[-1], shp[-1], _c0, F32)])
    return o.reshape(shp)


def _reduce_scatter(name, grads):
    kept, got = _exchange(name + "_c", grads, "c")
    cur = [_add(f"{name}_addc{i}", k, g) for i, (k, g) in enumerate(zip(kept, got))]
    kept, got = _exchange(name + "_y", cur, "y")
    cur = [_add(f"{name}_addy{i}", k, g) for i, (k, g) in enumerate(zip(kept, got))]
    kept, got = _exchange(name + "_x", cur, "x")
    return [k[0] for k in kept], [g[0] for g in got]


def _sum8(name, g):
    _, R, C = g.shape
    tr = _tile(R, 512)

    def body(g_ref, o_ref):
        acc = g_ref[0]
        for d in range(1, NDEV):
            acc = acc + g_ref[d]
        o_ref[...] = acc

    return pl.pallas_call(
        body, name=name, grid=(R // tr,), in_specs=[pl.BlockSpec((NDEV, tr, C), lambda i: (0, i, 0))],
        out_specs=pl.BlockSpec((tr, C), lambda i: (i, 0)), out_shape=jax.ShapeDtypeStruct((R, C), F32),
        compiler_params=_cp(("parallel",)),
    )(g)


def _adamw_math(w, g, m, v):
    m = B1 * m + (1.0 - B1) * g
    v = B2 * v + (1.0 - B2) * jnp.square(g)
    m_hat = m / (1.0 - B1 ** STEP)
    v_hat = v / (1.0 - B2 ** STEP)
    delta = -LR * (m_hat / (jnp.sqrt(v_hat) + EPS) + WD * w)
    return delta, m, v


def _adamw(name, w, m, v, ga, gb, *, tr, cw, gw, goff=0):
    R, C = w.shape
    nc = C // cw
    nr = R // tr

    def body(w_ref, m_ref, v_ref, ga_ref, gb_ref, g_out, d_out, m_out, v_out):
        g = (ga_ref[...] + gb_ref[...])[:, :cw]
        d, mn, vn = _adamw_math(w_ref[...], g, m_ref[...], v_ref[...])
        g_out[...] = g
        d_out[...] = d
        m_out[...] = mn
        v_out[...] = vn

    wspec = pl.BlockSpec((tr, cw), lambda i, j: (i, j))
    gspec = pl.BlockSpec((tr, gw), lambda i, j: (i, goff + j))
    return pl.pallas_call(
        body, name=name, grid=(nr, nc), in_specs=[wspec, wspec, wspec, gspec, gspec], out_specs=[wspec] * 4,
        out_shape=[jax.ShapeDtypeStruct((R, C), F32)] * 4, compiler_params=_cp(("parallel", "parallel")),
    )(w, m, v, ga, gb)


def _adamw_small(name, w, m, v, g):
    R, C = w.shape

    def body(w_ref, m_ref, v_ref, g_ref, d_out, m_out, v_out):
        d, mn, vn = _adamw_math(w_ref[...], g_ref[...], m_ref[...], v_ref[...])
        d_out[...] = d
        m_out[...] = mn
        v_out[...] = vn

    tr = _tile(R, 512)
    spec = pl.BlockSpec((tr, C), lambda i: (i, 0))
    return pl.pallas_call(
        body, name=name, grid=(R // tr,), in_specs=[spec] * 4, out_specs=[spec] * 3,
        out_shape=[jax.ShapeDtypeStruct((R, C), F32)] * 3, compiler_params=_cp(("parallel",)),
    )(w, m, v, g)


def _prep(name, parts, rows_p, cols_p):
    R, C = parts[0].shape
    n = len(parts)

    def body(*refs):
        o_ref = refs[n]
        if (R, C) != (rows_p, cols_p):
            o_ref[...] = jnp.zeros_like(o_ref)
        for i in range(n):
            o_ref[0:R, i * cols_p:i * cols_p + C] = refs[i][...].astype(BF16)

    return pl.pallas_call(
        body, name=name, out_shape=jax.ShapeDtypeStruct((rows_p, n * cols_p), BF16), compiler_params=_cp(),
    )(*parts)


def _attn_masks():
    lane = lax.broadcasted_iota(jnp.int32, (1, LANE), 1)
    return [(lane < HEAD_DIM), (lane >= HEAD_DIM)]


def _band_valid(base):
    qi = lax.broadcasted_iota(jnp.int32, (QB, 2 * QB), 0)
    ki = lax.broadcasted_iota(jnp.int32, (QB, 2 * QB), 1)
    dist = qi + QB - ki
    return (dist >= 0) & (dist <= QB) & (base + ki - QB >= 0)


def _attn_specs(d, nb, width, offs):
    per = width // LANE
    cur = pl.BlockSpec((nb * QB, LANE), lambda hp, r, b: (b, r * per + offs + hp))
    prev = pl.BlockSpec((QB, LANE), lambda hp, r, b: (jnp.maximum(b * nb - 1, 0), r * per + offs + hp))
    return cur, prev


def _attn_fwd(z, d):
    S, ZW = z.shape
    L = S // d
    nb = min(4, L // QB)
    assert L % (nb * QB) == 0
    zv = z.reshape(L, d * ZW)
    scale = HEAD_DIM ** -0.5

    def body(q_ref, kc_ref, kp_ref, vc_ref, vp_ref, o_ref, m_ref, l_ref, kcat, vcat):
        b = pl.program_id(2)
        kcat[0:QB, :] = kp_ref[...].astype(BF16)
        kcat[QB:, :] = kc_ref[...].astype(BF16)
        vcat[0:QB, :] = vp_ref[...].astype(BF16)
        vcat[QB:, :] = vc_ref[...].astype(BF16)
        masks = _attn_masks()
        for i in range(nb):
            q = q_ref[i * QB:(i + 1) * QB, :]
            kk = kcat[i * QB:(i + 2) * QB, :]
            vv = vcat[i * QB:(i + 2) * QB, :]
            valid = _band_valid((b * nb + i) * QB)
            o_acc = m_acc = l_acc = None
            for hm in masks:
                qh = jnp.where(hm, q, 0.0).astype(BF16)
                s = lax.dot_general(qh, kk, (((1,), (1,)), ((), ())), preferred_element_type=F32) * scale
                s = jnp.where(valid, s, MASK_VALUE)
                m = jnp.max(s, axis=-1, keepdims=True)
                p = jnp.exp(s - m)
                l = jnp.sum(p, axis=-1, keepdims=True)
                o = jnp.dot(p.astype(BF16), vv, preferred_element_type=F32)
                if o_acc is None:
                    o_acc, m_acc, l_acc = o, jnp.broadcast_to(m, (QB, LANE)), jnp.broadcast_to(l, (QB, LANE))
                else:
                    o_acc = jnp.where(hm, o, o_acc)
                    m_acc = jnp.where(hm, m, m_acc)
                    l_acc = jnp.where(hm, l, l_acc)
            o_ref[i * QB:(i + 1) * QB, :] = o_acc
            m_ref[i * QB:(i + 1) * QB, :] = m_acc
            l_ref[i * QB:(i + 1) * QB, :] = l_acc

    qc, _ = _attn_specs(d, nb, ZW, 0)
    kc, kp = _attn_specs(d, nb, ZW, ATTN_W // LANE)
    vc, vp = _attn_specs(d, nb, ZW, 2 * ATTN_W // LANE)
    oc, _ = _attn_specs(d, nb, ATTN_W, 0)
    shp = jax.ShapeDtypeStruct((L, d * ATTN_W), F32)
    o, m, l = pl.pallas_call(
        body, name=f"attn_fwd_d{d}", grid=(ATTN_W // LANE, d, L // (nb * QB)),
        in_specs=[qc, kc, kp, vc, vp], out_specs=[oc, oc, oc], out_shape=[shp, shp, shp],
        scratch_shapes=[pltpu.VMEM(((nb + 1) * QB, LANE), BF16), pltpu.VMEM(((nb + 1) * QB, LANE), BF16)],
        compiler_params=_cp(("parallel", "parallel", "parallel")),
    )(zv, zv, zv, zv, zv)
    return o.reshape(S, ATTN_W), m.reshape(S, ATTN_W), l.reshape(S, ATTN_W)


def _attn_merge_math(o1, m1, l1, o2, m2, l2, o3, m3, l3):
    mg = jnp.maximum(jnp.maximum(m1, m2), m3)
    w1, w2, w3 = jnp.exp(m1 - mg), jnp.exp(m2 - mg), jnp.exp(m3 - mg)
    den = w1 * l1 + w2 * l2 + w3 * l3
    return (w1 * o1 + w2 * o2 + w3 * o3) / den, mg, den


def _attn_bwd(z, dya, ya, mg, den, d):
    S, ZW = z.shape
    L = S // d
    nb = min(4, L // QB)
    nsteps = L // (nb * QB)
    zv = z.reshape(L, d * ZW)
    view = lambda a: a.reshape(L, d * ATTN_W)
    scale = HEAD_DIM ** -0.5

    def body(q_ref, kc_ref, kp_ref, vc_ref, vp_ref, dy_ref, y_ref, m_ref, n_ref, dq_ref, dk_ref, dv_ref,
             kcat, vcat, dkcat, dvcat):
        b = pl.program_id(2)

        @pl.when(b == 0)
        def _():
            dk_ref[...] = jnp.zeros_like(dk_ref)
            dv_ref[...] = jnp.zeros_like(dv_ref)

        kcat[0:QB, :] = kp_ref[...].astype(BF16)
        kcat[QB:, :] = kc_ref[...].astype(BF16)
        vcat[0:QB, :] = vp_ref[...].astype(BF16)
        vcat[QB:, :] = vc_ref[...].astype(BF16)
        dkcat[...] = jnp.zeros_like(dkcat)
        dvcat[...] = jnp.zeros_like(dvcat)
        masks = _attn_masks()
        for i in range(nb):
            rows = slice(i * QB, (i + 1) * QB)
            q, dy, y = q_ref[rows, :], dy_ref[rows, :], y_ref[rows, :]
            mrow, nrow = m_ref[rows, :], n_ref[rows, :]
            kk = kcat[i * QB:(i + 2) * QB, :]
            vv = vcat[i * QB:(i + 2) * QB, :]
            valid = _band_valid((b * nb + i) * QB)
            dq_acc = jnp.zeros((QB, LANE), F32)
            dk_acc = jnp.zeros((2 * QB, LANE), F32)
            dv_acc = jnp.zeros((2 * QB, LANE), F32)
            for hm in masks:
                qh = jnp.where(hm, q, 0.0).astype(BF16)
                dyh = jnp.where(hm, dy, 0.0)
                dyb = dyh.astype(BF16)
                dsum = jnp.sum(dyh * y, axis=-1, keepdims=True)
                mh = jnp.max(jnp.where(hm, mrow, MASK_VALUE), axis=-1, keepdims=True)
                nh = jnp.max(jnp.where(hm, nrow, 0.0), axis=-1, keepdims=True)
                s = lax.dot_general(qh, kk, (((1,), (1,)), ((), ())), preferred_element_type=F32) * scale
                p = jnp.where(valid, jnp.exp(s - mh), 0.0) / nh
                pb = p.astype(BF16)
                dv_h = lax.dot_general(pb, dyb, (((0,), (0,)), ((), ())), preferred_element_type=F32)
                dp = lax.dot_general(dyb, vv, (((1,), (1,)), ((), ())), preferred_element_type=F32)
                ds = (p * (dp - dsum) * scale).astype(BF16)
                dq_h = jnp.dot(ds, kk, preferred_element_type=F32)
                dk_h = lax.dot_general(ds, qh, (((0,), (0,)), ((), ())), preferred_element_type=F32)
                dq_acc += jnp.where(hm, dq_h, 0.0)
                dk_acc += dk_h
                dv_acc += dv_h
            dq_ref[rows, :] = dq_acc
            dkcat[i * QB:(i + 2) * QB, :] += dk_acc
            dvcat[i * QB:(i + 2) * QB, :] += dv_acc

        base = pl.multiple_of(b * (nb * QB), QB)
        dk_ref[pl.ds(base, nb * QB), :] += dkcat[QB:, :]
        dv_ref[pl.ds(base, nb * QB), :] += dvcat[QB:, :]

        @pl.when(b > 0)
        def _():
            prev = pl.multiple_of(b * (nb * QB) - QB, QB)
            dk_ref[pl.ds(prev, QB), :] += dkcat[0:QB, :]
            dv_ref[pl.ds(prev, QB), :] += dvcat[0:QB, :]

    qc, _ = _attn_specs(d, nb, ZW, 0)
    kc, kp = _attn_specs(d, nb, ZW, ATTN_W // LANE)
    vc, vp = _attn_specs(d, nb, ZW, 2 * ATTN_W // LANE)
    oc, _ = _attn_specs(d, nb, ATTN_W, 0)
    per = ATTN_W // LANE
    whole = pl.BlockSpec((L, LANE), lambda hp, r, b: (0, r * per + hp))
    shp = jax.ShapeDtypeStruct((L, d * ATTN_W), F32)
    dq, dk, dv = pl.pallas_call(
        body, name=f"attn_bwd_d{d}", grid=(ATTN_W // LANE, d, nsteps),
        in_specs=[qc, kc, kp, vc, vp, oc, oc, oc, oc], out_specs=[oc, whole, whole], out_shape=[shp, shp, shp],
        scratch_shapes=[pltpu.VMEM(((nb + 1) * QB, LANE), BF16), pltpu.VMEM(((nb + 1) * QB, LANE), BF16),
                        pltpu.VMEM(((nb + 1) * QB, LANE), F32), pltpu.VMEM(((nb + 1) * QB, LANE), F32)],
        compiler_params=_cp(("parallel", "parallel", "arbitrary")),
    )(zv, zv, zv, zv, zv, view(dya), view(ya), view(mg), view(den))
    return dq.reshape(S, ATTN_W), dk.reshape(S, ATTN_W), dv.reshape(S, ATTN_W)


def _ssm_disc(lr, li, logdt, br, bi):
    dt = jnp.exp(logdt)
    mag = jnp.exp(lr * dt)
    ar = mag * jnp.cos(li * dt)
    ai = mag * jnp.sin(li * dt)
    nr, ni = ar - 1.0, ai
    den = lr * lr + li * li
    cr = (nr * lr + ni * li) / den
    ci = (ni * lr - nr * li) / den
    return ar, ai, cr * br - ci * bi, cr * bi + ci * br


def _ssm_prep(lr, li, logdt, br, bi):
    n, c = br.shape
    outs, _ = _rowwise("ssm_prep", lambda *a: (list(_ssm_disc(*a)), []), n, _tile(n, 512),
                       [_full(a) for a in (lr, li, logdt, br, bi)], [],
                       [(1, 1, _c0, F32), (1, 1, _c0, F32), (c, c, _c0, F32), (c, c, _c0, F32)])
    return outs


def _ssm_prep_bwd(lr, li, logdt, br, bi, dar, dai, dbbr, dbbi):
    n, c = br.shape

    def f(lrb, lib, dtb, brb, bib, *cts):
        _, vjp = jax.vjp(_ssm_disc, lrb, lib, dtb, brb, bib)
        return list(vjp(cts)), []

    outs, _ = _rowwise("ssm_prep_bwd", f, n, _tile(n, 512),
                       [_full(a) for a in (lr, li, logdt, br, bi, dar, dai, dbbr, dbbi)], [],
                       [(1, 1, _c0, F32)] * 3 + [(c, c, _c0, F32)] * 2)
    return outs


def _cmul(ar, ai, br, bi):
    return ar * br - ai * bi, ar * bi + ai * br


def _scan_consts(ar, ai, reverse):
    w = ar.shape[-1]
    a1 = (jnp.broadcast_to(ar, (8, w)), jnp.broadcast_to(ai, (8, w)))
    a2 = _cmul(*a1, *a1)
    a4 = _cmul(*a2, *a2)
    a8 = _cmul(*a4, *a4)
    row = lax.broadcasted_iota(jnp.int32, (8, w), 0)
    e = (8 - row) if reverse else (row + 1)
    one, zero = jnp.ones((8, w), F32), jnp.zeros((8, w), F32)
    pw = (one, zero)
    for bit, ap in ((1, a1), (2, a2), (4, a4), (8, a8)):
        sel = (e & bit) != 0
        nxt = _cmul(*pw, *ap)
        pw = (jnp.where(sel, nxt[0], pw[0]), jnp.where(sel, nxt[1], pw[1]))
    return (a1, a2, a4), pw, row


def _scan_group(xr, xi, cr, ci, consts, reverse):
    steps, pw, row = consts
    for sh, (pr, pi) in zip((1, 2, 4), steps):
        if reverse:
            sr, si = pltpu.roll(xr, 8 - sh, 0), pltpu.roll(xi, 8 - sh, 0)
            keep = row < 8 - sh
        else:
            sr, si = pltpu.roll(xr, sh, 0), pltpu.roll(xi, sh, 0)
            keep = row >= sh
        tr_, ti_ = _cmul(pr, pi, sr, si)
        xr = xr + jnp.where(keep, tr_, 0.0)
        xi = xi + jnp.where(keep, ti_, 0.0)
    tr_, ti_ = _cmul(pw[0], pw[1], cr, ci)
    return xr + tr_, xi + ti_


def _ssm_fwd(z, a_r, a_i, bdr, bdi, cmr, cmi, dskip, ts):
    S, ZW = z.shape
    NS = SSM_G * SSM_P
    PW = PACK * SSM_P
    uoff = (ZW - SSM_W) // LANE
    nsteps = S // ts

    def body(u_ref, ar_ref, ai_ref, bdr_ref, bdi_ref, cmr_ref, cmi_ref, d_ref, hr_ref, hi_ref, y_ref, car_r, car_i):
        s = pl.program_id(1)

        @pl.when(s == 0)
        def _():
            car_r[...] = jnp.zeros_like(car_r)
            car_i[...] = jnp.zeros_like(car_i)

        u = u_ref[...]
        ub = u.astype(BF16)
        nt = (((1,), (1,)), ((), ()))
        hr_ref[...] = lax.dot_general(ub, bdr_ref[...], nt, preferred_element_type=F32)
        hi_ref[...] = lax.dot_general(ub, bdi_ref[...], nt, preferred_element_type=F32)
        consts = _scan_consts(ar_ref[...], ai_ref[...], False)

        def step(j, carry):
            rows = pl.ds(pl.multiple_of(j * 8, 8), 8)
            hr, hi = _scan_group(hr_ref[rows, :], hi_ref[rows, :], carry[0], carry[1], consts, False)
            hr_ref[rows, :] = hr
            hi_ref[rows, :] = hi
            return jnp.broadcast_to(hr[7:8, :], (8, PW)), jnp.broadcast_to(hi[7:8, :], (8, PW))

        cr, ci = lax.fori_loop(0, ts // 8, step, (car_r[...], car_i[...]))
        car_r[...] = cr
        car_i[...] = ci
        y = lax.dot_general(hr_ref[...].astype(BF16), cmr_ref[...], nt, preferred_element_type=F32)
        y -= lax.dot_general(hi_ref[...].astype(BF16), cmi_ref[...], nt, preferred_element_type=F32)
        y_ref[...] = y + d_ref[...] * u

    row_a = pl.BlockSpec((1, PW), lambda i, s: (0, i))
    return pl.pallas_call(
        body, name="ssm_fwd", grid=(SSM_G // PACK, nsteps),
        in_specs=[pl.BlockSpec((ts, LANE), lambda i, s: (s, uoff + i)), row_a, row_a,
                  pl.BlockSpec((None, PW, LANE), lambda i, s: (i, 0, 0)), pl.BlockSpec((None, PW, LANE), lambda i, s: (i, 0, 0)),
                  pl.BlockSpec((None, LANE, PW), lambda i, s: (i, 0, 0)), pl.BlockSpec((None, LANE, PW), lambda i, s: (i, 0, 0)),
                  pl.BlockSpec((1, LANE), lambda i, s: (0, i))],
        out_specs=[pl.BlockSpec((ts, PW), lambda i, s: (s, i)), pl.BlockSpec((ts, PW), lambda i, s: (s, i)),
                   pl.BlockSpec((ts, LANE), lambda i, s: (s, i))],
        out_shape=[jax.ShapeDtypeStruct((S, NS), F32), jax.ShapeDtypeStruct((S, NS), F32),
                   jax.ShapeDtypeStruct((S, SSM_W), F32)],
        scratch_shapes=[pltpu.VMEM((8, PW), F32), pltpu.VMEM((8, PW), F32)],
        compiler_params=_cp(("parallel", "arbitrary")),
    )(z, a_r, a_i, bdr, bdi, cmr, cmi, dskip)


def _ssm_bwd(z, dyp, hr, hi, a_r, a_i, bdr, bdi, cmr, cmi, dskip, ts):
    S, ZW = z.shape
    NS = SSM_G * SSM_P
    PW = PACK * SSM_P
    uoff = (ZW - SSM_W) // LANE
    nsteps = S // ts
    npk = SSM_G // PACK

    def body(u_ref, dy_ref, hr_ref, hi_ref, hpr_ref, hpi_ref, ar_ref, ai_ref, bdr_ref, bdi_ref, cmr_ref, cmi_ref,
             d_ref, du_ref, dbdr_ref, dbdi_ref, dcmr_ref, dcmi_ref, dar_ref, dai_ref, dd_ref,
             lr_s, li_s, hcr, hci, car_r, car_i):
        s = pl.program_id(1)
        first_tile = s == nsteps - 1

        @pl.when(s == 0)
        def _():
            car_r[...] = jnp.zeros_like(car_r)
            car_i[...] = jnp.zeros_like(car_i)
            for r in (dbdr_ref, dbdi_ref, dcmr_ref, dcmi_ref, dar_ref, dai_ref, dd_ref):
                r[...] = jnp.zeros_like(r)

        u, dy = u_ref[...], dy_ref[...]
        ub, dyb = u.astype(BF16), dy.astype(BF16)
        lr_s[...] = jnp.dot(dyb, cmr_ref[...], preferred_element_type=F32)
        li_s[...] = -jnp.dot(dyb, cmi_ref[...], preferred_element_type=F32)
        keep_prev = jnp.where(first_tile, 0.0, 1.0)
        hcr[0:8, :] = hpr_ref[...] * keep_prev
        hci[0:8, :] = hpi_ref[...] * keep_prev
        hcr[8:, :] = hr_ref[...]
        hci[8:, :] = hi_ref[...]
        consts = _scan_consts(ar_ref[...], -ai_ref[...], True)
        row = consts[2]
        ngrp = ts // 8

        def step(jj, carry):
            cr, ci, accr, acci = carry
            j = ngrp - 1 - jj
            rows = pl.ds(pl.multiple_of(j * 8, 8), 8)
            nxt = pl.ds(pl.multiple_of(j * 8 + 8, 8), 8)
            lr, li = _scan_group(lr_s[rows, :], li_s[rows, :], cr, ci, consts, True)
            lr_s[rows, :] = lr
            li_s[rows, :] = li
            pr, pi = hcr[rows, :], hci[rows, :]
            hsr = jnp.where(row == 0, jnp.broadcast_to(pr[7:8, :], (8, PW)), pltpu.roll(hcr[nxt, :], 1, 0))
            hsi = jnp.where(row == 0, jnp.broadcast_to(pi[7:8, :], (8, PW)), pltpu.roll(hci[nxt, :], 1, 0))
            accr = accr + lr * hsr + li * hsi
            acci = acci + li * hsr - lr * hsi
            return jnp.broadcast_to(lr[0:1, :], (8, PW)), jnp.broadcast_to(li[0:1, :], (8, PW)), accr, acci

        zero = jnp.zeros((8, PW), F32)
        cr, ci, accr, acci = lax.fori_loop(0, ngrp, step, (car_r[...], car_i[...], zero, zero))
        car_r[...] = cr
        car_i[...] = ci
        dar_ref[...] += jnp.sum(accr, axis=0, keepdims=True)
        dai_ref[...] += jnp.sum(acci, axis=0, keepdims=True)
        lrb, lib = lr_s[...].astype(BF16), li_s[...].astype(BF16)
        du = jnp.dot(lrb, bdr_ref[...], preferred_element_type=F32)
        du += jnp.dot(lib, bdi_ref[...], preferred_element_type=F32)
        du_ref[...] = du + dy * d_ref[...]
        tn = (((0,), (0,)), ((), ()))
        dbdr_ref[...] += lax.dot_general(lrb, ub, tn, preferred_element_type=F32)
        dbdi_ref[...] += lax.dot_general(lib, ub, tn, preferred_element_type=F32)
        dcmr_ref[...] += lax.dot_general(dyb, hr_ref[...].astype(BF16), tn, preferred_element_type=F32)
        dcmi_ref[...] -= lax.dot_general(dyb, hi_ref[...].astype(BF16), tn, preferred_element_type=F32)
        dd_ref[...] += jnp.sum(dy * u, axis=0, keepdims=True)

    rev = lambda s: nsteps - 1 - s
    row_a = pl.BlockSpec((1, PW), lambda i, s: (0, i))
    tile = pl.BlockSpec((ts, PW), lambda i, s: (rev(s), i))
    prev8 = pl.BlockSpec((8, PW), lambda i, s: (jnp.maximum(rev(s) * (ts // 8) - 1, 0), i))
    cols = pl.BlockSpec((ts, LANE), lambda i, s: (rev(s), i))
    bd = pl.BlockSpec((None, PW, LANE), lambda i, s: (i, 0, 0))
    cm = pl.BlockSpec((None, LANE, PW), lambda i, s: (i, 0, 0))
    return pl.pallas_call(
        body, name="ssm_bwd", grid=(npk, nsteps),
        in_specs=[pl.BlockSpec((ts, LANE), lambda i, s: (rev(s), uoff + i)), cols, tile, tile, prev8, prev8,
                  row_a, row_a, bd, bd, cm, cm, pl.BlockSpec((1, LANE), lambda i, s: (0, i))],
        out_specs=[cols, bd, bd, cm, cm, row_a, row_a, pl.BlockSpec((1, LANE), lambda i, s: (0, i))],
        out_shape=[jax.ShapeDtypeStruct((S, SSM_W), F32),
                   jax.ShapeDtypeStruct((npk, PW, LANE), F32), jax.ShapeDtypeStruct((npk, PW, LANE), F32),
                   jax.ShapeDtypeStruct((npk, LANE, PW), F32), jax.ShapeDtypeStruct((npk, LANE, PW), F32),
                   jax.ShapeDtypeStruct((1, NS), F32), jax.ShapeDtypeStruct((1, NS), F32),
                   jax.ShapeDtypeStruct((1, SSM_W), F32)],
        scratch_shapes=[pltpu.VMEM((ts, PW), F32), pltpu.VMEM((ts, PW), F32),
                        pltpu.VMEM((ts + 8, PW), F32), pltpu.VMEM((ts + 8, PW), F32),
                        pltpu.VMEM((8, PW), F32), pltpu.VMEM((8, PW), F32)],
        compiler_params=_cp(("parallel", "arbitrary")),
    )(z, dyp, hr, hi, hr, hi, a_r, a_i, bdr, bdi, cmr, cmi, dskip)


def _block_diag(m4):
    npk, g, a, b = m4.shape
    eye = jnp.eye(g, dtype=m4.dtype)
    return (m4[:, :, :, None, :] * eye[None, :, None, :, None]).reshape(npk, g * a, g * b)


def _block_diag_take(m, a, b):
    npk = m.shape[0]
    m5 = m.reshape(npk, PACK, a, PACK, b)
    return jnp.stack([m5[:, g, :, g, :] for g in range(PACK)], axis=1)


def _swiglu_act(g, u):
    return jax.nn.silu(g) * u


def _mix_out(ya, ypre, gl, ga, gb, bglu):
    yg = jax.nn.gelu(ypre)
    yb = yg * jax.nn.sigmoid(gl + bglu)
    return jnp.concatenate([_rms(ya, ga), _rms(yb, gb)], axis=-1)


def _tail_loss(h3, gl, pe, gf, tgt):
    h4 = h3 + jax.nn.sigmoid(gl) * pe
    err = jnp.square(_rms(h4, gf) - tgt)
    return 0.5 * jnp.mean(err, axis=-1, keepdims=True)


def kernel(x, p, ffn1_norm, ffn1_w_gate, ffn1_w_up, ffn1_w_down, mix_norm, w_in, attn_out_norm, ssm_lambda_re, ssm_lambda_im, ssm_log_dt, ssm_b_re, ssm_b_im, ssm_c_re, ssm_c_im, ssm_d, ssm_w_glu, ssm_b_glu, ssm_out_norm, w_out, ffn2_norm, ffn2_w_gate, ffn2_w_up, ffn2_w_down, ple_norm, ple_w_gate, ple_w_proj, final_norm, loss_target, m_ffn1_norm, m_ffn1_w_gate, m_ffn1_w_up, m_ffn1_w_down, m_mix_norm, m_w_in, m_attn_out_norm, m_ssm_lambda_re, m_ssm_lambda_im, m_ssm_log_dt, m_ssm_b_re, m_ssm_b_im, m_ssm_c_re, m_ssm_c_im, m_ssm_d, m_ssm_w_glu, m_ssm_b_glu, m_ssm_out_norm, m_w_out, m_ffn2_norm, m_ffn2_w_gate, m_ffn2_w_up, m_ffn2_w_down, m_ple_norm, m_ple_w_gate, m_ple_w_proj, m_final_norm, v_ffn1_norm, v_ffn1_w_gate, v_ffn1_w_up, v_ffn1_w_down, v_mix_norm, v_w_in, v_attn_out_norm, v_ssm_lambda_re, v_ssm_lambda_im, v_ssm_log_dt, v_ssm_b_re, v_ssm_b_im, v_ssm_c_re, v_ssm_c_im, v_ssm_d, v_ssm_w_glu, v_ssm_b_glu, v_ssm_out_norm, v_w_out, v_ffn2_norm, v_ffn2_w_gate, v_ffn2_w_up, v_ffn2_w_down, v_ple_norm, v_ple_w_gate, v_ple_w_proj, v_final_norm):
    A = dict(locals())
    xs = x[0]
    ps = p[0, 0]
    tgt = loss_target[0]
    S, D = xs.shape
    FSH = ffn1_w_gate.shape[-1]
    FSP = -(-FSH // LANE) * LANE
    TR = _tile(S, 256)
    ZW = 3 * ATTN_W + SSM_W

    wgu1 = _prep("prep_gu1", [ffn1_w_gate[0], ffn1_w_up[0]], D, FSP)
    wgu2 = _prep("prep_gu2", [ffn2_w_gate[0], ffn2_w_up[0]], D, FSP)
    wd1 = _prep("prep_d1", [ffn1_w_down[0]], FSP, D)
    wd2 = _prep("prep_d2", [ffn2_w_down[0]], FSP, D)
    win = _prep("prep_in", [w_in[0]], D, w_in.shape[-1])
    wglu = _prep("prep_glu", [ssm_w_glu[0]], ssm_w_glu.shape[1], SSM_W)
    wout = _prep("prep_out", [w_out[0]], w_out.shape[1], D)
    wpg = _prep("prep_pg", [ple_w_gate[0]], ple_w_gate.shape[1], D)
    wpp = _prep("prep_pp", [ple_w_proj[0]], ple_w_proj.shape[1], ple_w_proj.shape[2])
    Wgu1, Wd1, Win, Wglu, Wout, Wgu2, Wd2, Wpg, Wpp = _all_gather(
        "ag_weights", [wgu1, wd1, win, wglu, wout, wgu2, wd2, wpg, wpp])
    rowstack = lambda w: w.reshape(1, w.shape[0] * w.shape[1], w.shape[2])
    Wd1, Wd2, Wglu, Wout, Wpg = (rowstack(w) for w in (Wd1, Wd2, Wglu, Wout, Wpg))

    def ffn_fwd(tag, h, gain, Wgu, Wd):
        (xn,), _ = _rowwise(f"{tag}_norm", lambda a, g: ([_rms(a, g)], []), S, TR, [_full(h)], [gain], [(D, D, _c0, BF16)])
        gu = _mm_nn(f"{tag}_up", xn, Wgu, tn=FSP, tk=D)
        (hid,), _ = _rowwise(f"{tag}_act", lambda gub: ([_swiglu_act(gub[:, :FSP], gub[:, FSP:])], []), S, TR,
                             [(gu, 2 * FSP, lambda j: j)], [], [(NDEV * FSP, FSP, lambda j: j, BF16)], ncol=NDEV)
        out = _mm_nn(f"{tag}_down", hid, Wd, tn=D, tk=FSP, res=h, scale=0.5)
        return out, (xn, gu, hid)

    h1, (xn1, gu1, hid1) = ffn_fwd("ffn1", xs, ffn1_norm, Wgu1, Wd1)

    (un,), _ = _rowwise("mix_norm", lambda a, g: ([_rms(a, g)], []), S, TR, [_full(h1)], [mix_norm], [(D, D, _c0, BF16)])
    z = _mm_nn("mix_in", un, Win, tn=512, tk=D)
    pats = [_attn_fwd(z, d) for d in DILATIONS]
    (ya, mg, den), _ = _rowwise("attn_merge", lambda *a: (list(_attn_merge_math(*a)), []), S, TR,
                                [_full(t) for pat in pats for t in pat], [], [(ATTN_W, ATTN_W, _c0, F32)] * 3)

    col = lambda a: a.reshape(-1, 1)
    lr_c, li_c = col(ssm_lambda_re), col(ssm_lambda_im)
    dt_c = col(jnp.broadcast_to(ssm_log_dt.reshape(SSM_G, 1), (SSM_G, SSM_P)))
    b_re2, b_im2 = ssm_b_re.reshape(-1, SSM_C), ssm_b_im.reshape(-1, SSM_C)
    ar_c, ai_c, bbr, bbi = _ssm_prep(lr_c, li_c, dt_c, b_re2, b_im2)
    a_r, a_i = ar_c.reshape(1, -1), ai_c.reshape(1, -1)
    npk = SSM_G // PACK
    bdr = _block_diag(bbr.reshape(npk, PACK, SSM_P, SSM_C)).astype(BF16)
    bdi = _block_diag(bbi.reshape(npk, PACK, SSM_P, SSM_C)).astype(BF16)
    cmr = _block_diag(ssm_c_re.reshape(npk, PACK, SSM_C, SSM_P)).astype(BF16)
    cmi = _block_diag(ssm_c_im.reshape(npk, PACK, SSM_C, SSM_P)).astype(BF16)
    TS = _tile(S, 512)
    hr, hi, ypre = _ssm_fwd(z, a_r, a_i, bdr, bdi, cmr, cmi, ssm_d, TS)
    (yg,), _ = _rowwise("ssm_gelu", lambda a: ([jax.nn.gelu(a)], []), S, TR, [_full(ypre)], [], [(SSM_W, SSM_W, _c0, BF16)])
    gl = _mm_nn("ssm_glu", yg, Wglu, tn=SSM_W, tk=SSM_W)
    (ycat,), _ = _rowwise("mix_out", lambda *a: ([_mix_out(*a)], []), S, TR, [_full(ya), _full(ypre), _full(gl)],
                          [attn_out_norm, ssm_out_norm, ssm_b_glu], [(MIX_W, MIX_W, _c0, BF16)])
    h2 = _mm_nn("mix_proj", ycat, Wout, tn=D // 2, tk=D, res=h1, scale=1.0)

    h3, (xn2, gu2, hid2) = ffn_fwd("ffn2", h2, ffn2_norm, Wgu2, Wd2)

    (hn, pb), _ = _rowwise("ple_norm", lambda a, q, g: ([_rms(a, g), q], []), S, TR, [_full(h3), _full(ps)], [ple_norm],
                           [(D, D, _c0, BF16), (ps.shape[1], ps.shape[1], _c0, BF16)])
    pgl = _mm_nn("ple_gate", hn, Wpg, tn=D // 2, tk=D)
    pe = _mm_nn("ple_proj", pb, Wpp, tn=Wpp.shape[2], tk=Wpp.shape[1])

    def tail(h3b, glb, peb, tb, gf):
        rows, vjp = jax.vjp(lambda a, b, c, g: _tail_loss(a, b, c, g, tb), h3b, glb, peb, gf)
        dh, dgl, dpe, dgf = vjp(jnp.ones_like(rows))
        return [dh, dgl, dpe], [jnp.broadcast_to(jnp.sum(rows, axis=0, keepdims=True), (1, LANE)), dgf]

    (dh3_dir, dpgl, dpe), (loss_row, g_final) = _rowwise(
        "tail", tail, S, TR, [_full(h3), _full(pgl), _full(pe), _full(tgt)], [final_norm.reshape(1, D)],
        [(D, D, _c0, F32), (D, D, _c0, BF16), (D, D, _c0, BF16)], [(LANE, LANE, _c0), (D, D, _c0)])
    loss = lax.psum(loss_row[0, 0], AXES)

    def norm_bwd(tag, h, gain, dn, dres):
        def f(hb, dnb, drb, g):
            _, vjp = jax.vjp(_rms, hb, g)
            dh, dg = vjp(dnb)
            dh = dh + drb
            return [dh, dh], [dg]
        (dh, dhb), (dg,) = _rowwise(f"{tag}_norm_bwd", f, S, TR, [_full(h), _full(dn), _full(dres)], [gain],
                                    [(D, D, _c0, F32), (D, D, _c0, BF16)], [(D, D, _c0)])
        return dh, dhb, dg

    dhn = _mm_nt("ple_gate_dx", dpgl, Wpg, tn=D, tk=D)
    g_wpg = _mm_tn("ple_gate_dw", hn, dpgl, 1)
    g_wpp = _mm_tn("ple_proj_dw", pb, dpe, NDEV)
    dh3, dh3b, g_ple_norm = norm_bwd("ple", h3, ple_norm, dhn, dh3_dir)

    def ffn_bwd(tag, h, gain, Wgu, Wd, saved, dout, doutb):
        xn, gu, hid = saved
        dhid = _mm_nt(f"{tag}_down_dx", doutb, Wd, tn=FSP, tk=D, scale=0.5)
        g_wd = _mm_tn(f"{tag}_down_dw", hid, doutb, 1, tko=FSP, tn=D // 2, scale=0.5)

        def act_bwd(gub, dhb):
            _, vjp = jax.vjp(_swiglu_act, gub[:, :FSP], gub[:, FSP:])
            return [jnp.concatenate(vjp(dhb), axis=-1)], []
        (dgu,), _ = _rowwise(f"{tag}_act_bwd", act_bwd, S, TR, [(gu, 2 * FSP, lambda j: j), (dhid, FSP, lambda j: j)], [],
                             [(2 * NDEV * FSP, 2 * FSP, lambda j: j, BF16)], ncol=NDEV)
        dxn = _mm_nt(f"{tag}_up_dx", dgu, Wgu, tn=D, tk=FSP)
        g_wgu = _mm_tn(f"{tag}_up_dw", xn, dgu, NDEV, tn=FSP)
        dh, dhb, g_norm = norm_bwd(tag, h, gain, dxn, dout)
        return dh, dhb, g_norm, g_wgu, g_wd

    dh2, dh2b, g_ffn2_norm, g_wgu2, g_wd2 = ffn_bwd("ffn2", h2, ffn2_norm, Wgu2, Wd2, (xn2, gu2, hid2), dh3, dh3b)

    dycat = _mm_nt("mix_proj_dx", dh2b, Wout, tn=D, tk=D)
    g_wout = _mm_tn("mix_proj_dw", ycat, dh2b, 1)

    def mix_out_bwd(yab, ypb, glb, dyc, ga, gb, bglu):
        _, vjp = jax.vjp(_mix_out, yab, ypb, glb, ga, gb, bglu)
        dya_, dyp_, dgl_, dga, dgb, dbg = vjp(dyc)
        return [dya_, dyp_, dgl_], [dga, dgb, dbg]
    (dya, dyp_dir, dglb), (g_attn_norm, g_ssm_norm, g_bglu) = _rowwise(
        "mix_out_bwd", mix_out_bwd, S, TR, [_full(ya), _full(ypre), _full(gl), _full(dycat)],
        [attn_out_norm, ssm_out_norm, ssm_b_glu],
        [(ATTN_W, ATTN_W, _c0, F32), (SSM_W, SSM_W, _c0, F32), (SSM_W, SSM_W, _c0, BF16)],
        [(ATTN_W, ATTN_W, _c0), (SSM_W, SSM_W, _c0), (SSM_W, SSM_W, _c0)])
    dyg = _mm_nt("ssm_glu_dx", dglb, Wglu, tn=SSM_W, tk=SSM_W)
    g_wglu = _mm_tn("ssm_glu_dw", yg, dglb, 1)

    def gelu_bwd(ypb, dygb, ddir):
        _, vjp = jax.vjp(jax.nn.gelu, ypb)
        return [ddir + vjp(dygb)[0]], []
    (dyp,), _ = _rowwise("ssm_gelu_bwd", gelu_bwd, S, TR, [_full(ypre), _full(dyg), _full(dyp_dir)], [],
                         [(SSM_W, SSM_W, _c0, F32)])
    du, dbdr, dbdi, dcmr, dcmi, da_r, da_i, g_ssm_d = _ssm_bwd(z, dyp, hr, hi, a_r, a_i, bdr, bdi, cmr, cmi, ssm_d, TS)
    dbbr = _block_diag_take(dbdr, SSM_P, SSM_C).reshape(-1, SSM_C)
    dbbi = _block_diag_take(dbdi, SSM_P, SSM_C).reshape(-1, SSM_C)
    g_c_re = _block_diag_take(dcmr, SSM_C, SSM_P).reshape(ssm_c_re.shape)
    g_c_im = _block_diag_take(dcmi, SSM_C, SSM_P).reshape(ssm_c_im.shape)
    dlr, dli, ddt, g_b_re, g_b_im = _ssm_prep_bwd(lr_c, li_c, dt_c, b_re2, b_im2, col(da_r), col(da_i), dbbr, dbbi)
    g_lam_re, g_lam_im = dlr.reshape(ssm_lambda_re.shape), dli.reshape(ssm_lambda_im.shape)
    g_log_dt = jnp.sum(ddt.reshape(SSM_G, SSM_P), axis=1).reshape(ssm_log_dt.shape)
    g_b_re, g_b_im = g_b_re.reshape(ssm_b_re.shape), g_b_im.reshape(ssm_b_im.shape)

    grads_qkv = [_attn_bwd(z, dya, ya, mg, den, d) for d in DILATIONS]

    def dz_cat(q1, k1, v1, q2, k2, v2, q3, k3, v3, dub):
        return [jnp.concatenate([q1 + q2 + q3, k1 + k2 + k3, v1 + v2 + v3, dub], axis=-1)], []
    (dz,), _ = _rowwise("mix_dz", dz_cat, S, TR, [_full(t) for g3 in grads_qkv for t in g3] + [_full(du)], [],
                        [(ZW, ZW, _c0, BF16)])
    dun = _mm_nt("mix_in_dx", dz, Win, tn=D, tk=512)
    g_win = _mm_tn("mix_in_dw", un, dz, NDEV, tn=512)
    dh1, dh1b, g_mix_norm = norm_bwd("mix", h1, mix_norm, dun, dh2)

    dx, _dxb, g_ffn1_norm, g_wgu1, g_wd1 = ffn_bwd("ffn1", xs, ffn1_norm, Wgu1, Wd1, (xn1, gu1, hid1), dh1, dh1b)

    restack = lambda g: g.reshape((NDEV, g.shape[1] // NDEV) + g.shape[2:])
    big = [g_wgu1, restack(g_wd1), g_win, restack(g_wglu), restack(g_wout), g_wgu2, restack(g_wd2), restack(g_wpg), g_wpp]
    ka, kb = _reduce_scatter("rs", big)
    out = {}

    def upd(name, idx, *, tr, cw, gw, goff=0):
        w, m, v = A[name][0], A["m_" + name][0], A["v_" + name][0]
        g, dlt, mn, vn = _adamw("adamw_" + name, w, m, v, ka[idx], kb[idx], tr=tr, cw=cw, gw=gw, goff=goff)
        for k, val in (("grad_", g), ("delta_", dlt), ("new_m_", mn), ("new_v_", vn)):
            out[k + name] = val[None]

    DT = _tile(D, 256)
    FT = _tile(FSH, 512)
    DC = _tile(D, 1024, LANE)
    upd("ffn1_w_gate", 0, tr=DT, cw=FSH, gw=FSP, goff=0)
    upd("ffn1_w_up", 0, tr=DT, cw=FSH, gw=FSP, goff=1)
    upd("ffn1_w_down", 1, tr=FT, cw=DC, gw=DC)
    upd("w_in", 2, tr=DT, cw=w_in.shape[-1], gw=w_in.shape[-1])
    upd("ssm_w_glu", 3, tr=ssm_w_glu.shape[1], cw=SSM_W, gw=SSM_W)
    upd("w_out", 4, tr=w_out.shape[1], cw=DC, gw=DC)
    upd("ffn2_w_gate", 5, tr=DT, cw=FSH, gw=FSP, goff=0)
    upd("ffn2_w_up", 5, tr=DT, cw=FSH, gw=FSP, goff=1)
    upd("ffn2_w_down", 6, tr=FT, cw=DC, gw=DC)
    upd("ple_w_gate", 7, tr=ple_w_gate.shape[1], cw=DC, gw=DC)
    upd("ple_w_proj", 8, tr=ple_w_proj.shape[1], cw=ple_w_proj.shape[2], gw=ple_w_proj.shape[2])

    small = [("ffn1_norm", g_ffn1_norm), ("mix_norm", g_mix_norm), ("attn_out_norm", g_attn_norm),
             ("ssm_lambda_re", g_lam_re), ("ssm_lambda_im", g_lam_im), ("ssm_log_dt", g_log_dt),
             ("ssm_b_re", g_b_re), ("ssm_b_im", g_b_im), ("ssm_c_re", g_c_re), ("ssm_c_im", g_c_im),
             ("ssm_d", g_ssm_d), ("ssm_b_glu", g_bglu), ("ssm_out_norm", g_ssm_norm), ("ffn2_norm", g_ffn2_norm),
             ("ple_norm", g_ple_norm), ("final_norm", g_final)]
    chunk = 8 * LANE

    def pack(arrs):
        parts = []
        for a in arrs:
            flat = a.reshape(-1)
            padn = -(-flat.shape[0] // chunk) * chunk
            parts.append(jnp.pad(flat, (0, padn - flat.shape[0])).reshape(-1, LANE))
        return jnp.concatenate(parts, axis=0)

    g_pack = pack([g for _, g in small])
    (g_all,) = _all_gather("ag_small", [g_pack])
    g_sum = _sum8("small_sum", g_all)
    w_pack = pack([A[n] for n, _ in small])
    m_pack = pack([A["m_" + n] for n, _ in small])
    v_pack = pack([A["v_" + n] for n, _ in small])
    d_pack, mn_pack, vn_pack = _adamw_small("adamw_small", w_pack, m_pack, v_pack, g_sum)
    off = 0
    for n, _ in small:
        shape = A[n].shape
        size = math.prod(shape)
        rows = -(-size // chunk) * 8
        for k, buf in (("grad_", g_sum), ("delta_", d_pack), ("new_m_", mn_pack), ("new_v_", vn_pack)):
            out[k + n] = buf[off:off + rows].reshape(-1)[:size].reshape(shape)
        off += rows

    names = ['ffn1_norm', 'ffn1_w_gate', 'ffn1_w_up', 'ffn1_w_down', 'mix_norm', 'w_in', 'attn_out_norm',
             'ssm_lambda_re', 'ssm_lambda_im', 'ssm_log_dt', 'ssm_b_re', 'ssm_b_im', 'ssm_c_re', 'ssm_c_im', 'ssm_d',
             'ssm_w_glu', 'ssm_b_glu', 'ssm_out_norm', 'w_out', 'ffn2_norm', 'ffn2_w_gate', 'ffn2_w_up', 'ffn2_w_down',
             'ple_norm', 'ple_w_gate', 'ple_w_proj', 'final_norm']
    return (loss, dx[None], *[out[k + n] for k in ("grad_", "delta_", "new_m_", "new_v_") for n in names])
```

```python
import functools
import math

import jax
import jax.numpy as jnp
from jax import lax
from jax.experimental import pallas as pl
from jax.experimental.pallas import tpu as pltpu

F32, BF16 = jnp.float32, jnp.bfloat16
MESH = pl.DeviceIdType.MESH
NDEV = 8
AXES = ("x", "y", "c")
LANE = 128
VMEM_LIMIT = 56 * 1024 * 1024

ATTN_W = 1024
HEAD_DIM = 64
SSM_W = 1024
MIX_W = ATTN_W + SSM_W
SSM_G, SSM_P, SSM_C = 64, 64, 16
PACK = 8
DILATIONS = (1, 4, 16)
QB = 128
NORM_EPS = 1e-6
MASK_VALUE = -1e30
LR, B1, B2, EPS, WD, STEP = 0.001, 0.9, 0.999, 1e-08, 0.01, 10


def _cp(sem=None):
    return pltpu.CompilerParams(dimension_semantics=sem, vmem_limit_bytes=VMEM_LIMIT)


def _tile(n, target, mult=8):
    if n <= target:
        return n
    for t in range(target - target % mult, 0, -mult):
        if n % t == 0:
            return t
    return n


def _rms(x, g):
    return x * lax.rsqrt(jnp.mean(x * x, axis=-1, keepdims=True) + NORM_EPS) * g


def _rowwise(name, fn, S, tr, rows, fulls, outs, accs=(), ncol=1):
    nr, nf, no, na = len(rows), len(fulls), len(outs), len(accs)

    def body(*refs):
        ins = [r[...] for r in refs[:nr + nf]]
        o_refs = refs[nr + nf:nr + nf + no]
        a_refs = refs[nr + nf + no:]
        o_vals, a_vals = fn(*ins)
        for r, v in zip(o_refs, o_vals):
            r[...] = v.astype(r.dtype)
        if na:
            @pl.when(pl.program_id(1) == 0)
            def _():
                for r in a_refs:
                    r[...] = jnp.zeros_like(r)
            for r, v in zip(a_refs, a_vals):
                r[...] += v

    in_specs = [pl.BlockSpec((tr, w), functools.partial(lambda j, i, cm: (i, cm(j)), cm=cm)) for _, w, cm in rows]
    in_specs += [pl.BlockSpec(f.shape, functools.partial(lambda j, i, nd: (0,) * nd, nd=f.ndim)) for f in fulls]
    out_specs = [pl.BlockSpec((tr, w), functools.partial(lambda j, i, cm: (i, cm(j)), cm=cm)) for _, w, cm, _ in outs]
    out_specs += [pl.BlockSpec((1, w), functools.partial(lambda j, i, cm: (0, cm(j)), cm=cm)) for _, w, cm in accs]
    out_shape = [jax.ShapeDtypeStruct((S, c), dt) for c, _, _, dt in outs]
    out_shape += [jax.ShapeDtypeStruct((1, c), F32) for c, _, _ in accs]
    res = pl.pallas_call(
        body, name=name, grid=(ncol, S // tr), in_specs=in_specs, out_specs=out_specs, out_shape=out_shape,
        compiler_params=_cp(("parallel", "arbitrary" if na else "parallel")),
    )(*[a for a, _, _ in rows], *fulls)
    return res[:no], res[no:]


def _c0(j):
    return 0


def _full(a):
    return (a, a.shape[1], _c0)


def _mm_nn(name, a, w, *, out_dtype=F32, tm=512, tn=768, tk=2048, res=None, scale=1.0):
    M, K = a.shape
    J, K2, Np = w.shape
    assert K == K2
    tm, tn, tk = _tile(M, tm), _tile(Np, tn, LANE), _tile(K, tk, LANE)
    npj = Np // tn
    nk = K // tk

    def body(*refs):
        if res is None:
            a_ref, w_ref, o_ref, acc = refs
        else:
            a_ref, w_ref, r_ref, o_ref, acc = refs
        k = pl.program_id(2)

        @pl.when(k == 0)
        def _():
            acc[...] = jnp.zeros_like(acc)

        acc[...] += jnp.dot(a_ref[...].astype(BF16), w_ref[...], preferred_element_type=F32)

        @pl.when(k == nk - 1)
        def _():
            v = acc[...]
            if res is not None:
                v = r_ref[...] + scale * v
            o_ref[...] = v.astype(o_ref.dtype)

    in_specs = [pl.BlockSpec((tm, tk), lambda i, n, k: (i, k)),
                pl.BlockSpec((None, tk, tn), lambda i, n, k: (n // npj, k, n % npj))]
    args = [a, w]
    if res is not None:
        in_specs.append(pl.BlockSpec((tm, tn), lambda i, n, k: (i, n)))
        args.append(res)
    return pl.pallas_call(
        body, name=name, grid=(M // tm, J * npj, nk), in_specs=in_specs,
        out_specs=pl.BlockSpec((tm, tn), lambda i, n, k: (i, n)),
        out_shape=jax.ShapeDtypeStruct((M, J * Np), out_dtype),
        scratch_shapes=[pltpu.VMEM((tm, tn), F32)],
        compiler_params=_cp(("parallel", "parallel", "arbitrary")),
    )(*args)


def _mm_nt(name, dy, w, *, out_dtype=F32, tm=512, tn=2048, tk=768, scale=1.0):
    M, N = dy.shape
    J, K, Np = w.shape
    assert N == J * Np
    tm, tn, tk = _tile(M, tm), _tile(K, tn, LANE), _tile(Np, tk, LANE)
    npj = Np // tk
    nc = J * npj

    def body(a_ref, w_ref, o_ref, acc):
        c = pl.program_id(2)

        @pl.when(c == 0)
        def _():
            acc[...] = jnp.zeros_like(acc)

        acc[...] += lax.dot_general(a_ref[...].astype(BF16), w_ref[...], (((1,), (1,)), ((), ())),
                                    preferred_element_type=F32)

        @pl.when(c == nc - 1)
        def _():
            o_ref[...] = (scale * acc[...]).astype(o_ref.dtype)

    return pl.pallas_call(
        body, name=name, grid=(M // tm, K // tn, nc),
        in_specs=[pl.BlockSpec((tm, tk), lambda i, n, c: (i, c)),
                  pl.BlockSpec((None, tn, tk), lambda i, n, c: (c // npj, n, c % npj))],
        out_specs=pl.BlockSpec((tm, tn), lambda i, n, c: (i, n)),
        out_shape=jax.ShapeDtypeStruct((M, K), out_dtype),
        scratch_shapes=[pltpu.VMEM((tm, tn), F32)],
        compiler_params=_cp(("parallel", "parallel", "arbitrary")),
    )(dy, w)


def _mm_tn(name, x, dy, J, *, tm=1024, tko=1024, tn=768, scale=1.0):
    M, K = x.shape
    M2, N = dy.shape
    assert M == M2 and N % J == 0
    Np = N // J
    tm, tko, tn = _tile(M, tm, LANE), _tile(K, tko, LANE), _tile(Np, tn, LANE)
    npj = Np // tn
    nm = M // tm

    def body(x_ref, d_ref, o_ref, acc):
        m = pl.program_id(2)

        @pl.when(m == 0)
        def _():
            acc[...] = jnp.zeros_like(acc)

        acc[...] += lax.dot_general(x_ref[...].astype(BF16), d_ref[...].astype(BF16), (((0,), (0,)), ((), ())),
                                    preferred_element_type=F32)

        @pl.when(m == nm - 1)
        def _():
            o_ref[...] = scale * acc[...]

    return pl.pallas_call(
        body, name=name, grid=(K // tko, J * npj, nm),
        in_specs=[pl.BlockSpec((tm, tko), lambda k, n, m: (m, k)),
                  pl.BlockSpec((tm, tn), lambda k, n, m: (m, n))],
        out_specs=pl.BlockSpec((None, tko, tn), lambda k, n, m: (n // npj, k, n % npj)),
        out_shape=jax.ShapeDtypeStruct((J, K, Np), F32),
        scratch_shapes=[pltpu.VMEM((tko, tn), F32)],
        compiler_params=_cp(("parallel", "parallel", "arbitrary")),
    )(x, dy)


def _all_gather(name, shards):
    n = len(shards)

    def body(*refs):
        ins, outs = refs[:n], refs[n:2 * n]
        send_sems, recv_sems, local_sems = refs[2 * n:]
        x, y, c = lax.axis_index("x"), lax.axis_index("y"), lax.axis_index("c")
        me, sibling = (x, y, c), (x, y, 1 - c)
        chips = [(1 - x, y), (x, 1 - y), (1 - x, 1 - y)]

        def blk(i, px, py, pc):
            return outs[i].at[4 * px + 2 * py + pc]

        def copy(i, k, block, to, src=None):
            return pltpu.make_async_remote_copy(
                src_ref=blk(i, *block) if src is None else src, dst_ref=blk(i, *block),
                send_sem=send_sems.at[i, k], recv_sem=recv_sems.at[i, k], device_id=to, device_id_type=MESH)

        mine = [pltpu.make_async_copy(ins[i], blk(i, *me), local_sems.at[i]) for i in range(n)]
        for cp in mine:
            cp.start()
        first = []
        for i in range(n):
            first.append(copy(i, 0, me, sibling, src=ins[i]))
            first += [copy(i, 1 + j, me, (*chip, c), src=ins[i]) for j, chip in enumerate(chips)]
        for cp in first:
            cp.start()
        passed = []
        for i in range(n):
            for j, chip in enumerate(chips):
                copy(i, 1 + j, (*chip, c), me).wait_recv()
                fwd = copy(i, 4 + j, (*chip, c), sibling)
                fwd.start()
                passed.append(fwd)
        for i in range(n):
            copy(i, 0, sibling, me).wait_recv()
            for j, chip in enumerate(chips):
                copy(i, 4 + j, (*chip, 1 - c), me).wait_recv()
        for cp in first + passed:
            cp.wait_send()
        for cp in mine:
            cp.wait()

    any_spec = pl.BlockSpec(memory_space=pl.ANY)
    return pl.pallas_call(
        body, name=name, in_specs=[any_spec] * n, out_specs=[any_spec] * n,
        out_shape=[jax.ShapeDtypeStruct((NDEV,) + s.shape, s.dtype) for s in shards],
        scratch_shapes=[pltpu.SemaphoreType.DMA((n, 7)), pltpu.SemaphoreType.DMA((n, 7)), pltpu.SemaphoreType.DMA((n,))],
    )(*shards)


def _halve(name, buf, axis, wire, tr):
    h2, R, C = buf.shape
    h = h2 // 2
    tr = _tile(R, tr, 16)
    nr = R // tr
    n = h * nr
    view = buf.reshape(h, 2, R, C)
    pos = {a: lax.axis_index(a) for a in AXES}
    sel = jnp.stack([pos[axis], 1 - pos[axis]]).astype(jnp.int32)

    def body(sel_ref, keep_ref, send_ref, o_ref, land, stage, send_sems, recv_sems, credit):
        t = pl.program_id(0) * nr + pl.program_id(1)
        slot = t % 2
        here = {a: lax.axis_index(a) for a in AXES}
        peer = tuple(1 - here[a] if a == axis else here[a] for a in AXES)

        @pl.when(t >= 2)
        def _():
            pl.semaphore_wait(credit, 1)

        stage[slot] = send_ref[...].astype(wire)
        rdma = pltpu.make_async_remote_copy(
            src_ref=stage.at[slot], dst_ref=land.at[slot], send_sem=send_sems.at[slot], recv_sem=recv_sems.at[slot],
            device_id=peer, device_id_type=MESH)
        rdma.start()
        rdma.wait_recv()
        o_ref[...] = keep_ref[...] + land[slot].astype(F32)
        rdma.wait_send()

        @pl.when(t + 2 < n)
        def _():
            pl.semaphore_signal(credit, inc=1, device_id=peer, device_id_type=MESH)

    grid_spec = pltpu.PrefetchScalarGridSpec(
        num_scalar_prefetch=1, grid=(h, nr),
        in_specs=[pl.BlockSpec((None, None, tr, C), lambda b, i, s: (b, s[0], i, 0)),
                  pl.BlockSpec((None, None, tr, C), lambda b, i, s: (b, s[1], i, 0))],
        out_specs=pl.BlockSpec((None, tr, C), lambda b, i, s: (b, i, 0)),
        scratch_shapes=[pltpu.VMEM((2, tr, C), wire), pltpu.VMEM((2, tr, C), wire),
                        pltpu.SemaphoreType.DMA((2,)), pltpu.SemaphoreType.DMA((2,)), pltpu.SemaphoreType.REGULAR])
    return pl.pallas_call(
        body, name=name, grid_spec=grid_spec, out_shape=jax.ShapeDtypeStruct((h, R, C), F32),
        compiler_params=_cp(("arbitrary", "arbitrary")),
    )(sel, view, view)


def _reduce_scatter(name, grads):
    outs = []
    for i, g in enumerate(grads):
        g = _halve(f"{name}{i}_c", g, "c", F32, 512)
        g = _halve(f"{name}{i}_y", g, "y", BF16, 512)
        g = _halve(f"{name}{i}_x", g, "x", BF16, 512)
        outs.append(g[0])
    return outs


def _sum8(name, g):
    _, R, C = g.shape
    tr = _tile(R, 512)

    def body(g_ref, o_ref):
        acc = g_ref[0]
        for d in range(1, NDEV):
            acc = acc + g_ref[d]
        o_ref[...] = acc

    return pl.pallas_call(
        body, name=name, grid=(R // tr,), in_specs=[pl.BlockSpec((NDEV, tr, C), lambda i: (0, i, 0))],
        out_specs=pl.BlockSpec((tr, C), lambda i: (i, 0)), out_shape=jax.ShapeDtypeStruct((R, C), F32),
        compiler_params=_cp(("parallel",)),
    )(g)


def _adamw_math(w, g, m, v):
    m = B1 * m + (1.0 - B1) * g
    v = B2 * v + (1.0 - B2) * jnp.square(g)
    m_hat = m / (1.0 - B1 ** STEP)
    v_hat = v / (1.0 - B2 ** STEP)
    delta = -LR * (m_hat / (jnp.sqrt(v_hat) + EPS) + WD * w)
    return delta, m, v


def _adamw(name, w, m, v, gp, *, tr, cw, gw, goff=0):
    R, C = w.shape
    nc = C // cw
    nr = R // tr

    def body(w_ref, m_ref, v_ref, g_ref, g_out, d_out, m_out, v_out):
        g = g_ref[...][:, :cw]
        d, mn, vn = _adamw_math(w_ref[...], g, m_ref[...], v_ref[...])
        g_out[...] = g
        d_out[...] = d
        m_out[...] = mn
        v_out[...] = vn

    wspec = pl.BlockSpec((tr, cw), lambda i, j: (i, j))
    gspec = pl.BlockSpec((tr, gw), lambda i, j: (i, goff + j))
    return pl.pallas_call(
        body, name=name, grid=(nr, nc), in_specs=[wspec, wspec, wspec, gspec], out_specs=[wspec] * 4,
        out_shape=[jax.ShapeDtypeStruct((R, C), F32)] * 4, compiler_params=_cp(("parallel", "parallel")),
    )(w, m, v, gp)


def _adamw_small(name, w, m, v, g):
    R, C = w.shape

    def body(w_ref, m_ref, v_ref, g_ref, d_out, m_out, v_out):
        d, mn, vn = _adamw_math(w_ref[...], g_ref[...], m_ref[...], v_ref[...])
        d_out[...] = d
        m_out[...] = mn
        v_out[...] = vn

    tr = _tile(R, 512)
    spec = pl.BlockSpec((tr, C), lambda i: (i, 0))
    return pl.pallas_call(
        body, name=name, grid=(R // tr,), in_specs=[spec] * 4, out_specs=[spec] * 3,
        out_shape=[jax.ShapeDtypeStruct((R, C), F32)] * 3, compiler_params=_cp(("parallel",)),
    )(w, m, v, g)


def _prep(name, parts, rows_p, cols_p):
    R, C = parts[0].shape
    n = len(parts)

    def body(*refs):
        o_ref = refs[n]
        if (R, C) != (rows_p, cols_p):
            o_ref[...] = jnp.zeros_like(o_ref)
        for i in range(n):
            o_ref[0:R, i * cols_p:i * cols_p + C] = refs[i][...].astype(BF16)

    return pl.pallas_call(
        body, name=name, out_shape=jax.ShapeDtypeStruct((rows_p, n * cols_p), BF16), compiler_params=_cp(),
    )(*parts)


def _attn_masks():
    lane = lax.broadcasted_iota(jnp.int32, (1, LANE), 1)
    return [(lane < HEAD_DIM), (lane >= HEAD_DIM)]


def _band_valid(base):
    qi = lax.broadcasted_iota(jnp.int32, (QB, 2 * QB), 0)
    ki = lax.broadcasted_iota(jnp.int32, (QB, 2 * QB), 1)
    dist = qi + QB - ki
    return (dist >= 0) & (dist <= QB) & (base + ki - QB >= 0)


def _attn_specs(d, nb, width, offs):
    per = width // LANE
    cur = pl.BlockSpec((nb * QB, LANE), lambda hp, r, b: (b, r * per + offs + hp))
    prev = pl.BlockSpec((QB, LANE), lambda hp, r, b: (jnp.maximum(b * nb - 1, 0), r * per + offs + hp))
    return cur, prev


def _attn_fwd(z, d):
    S, ZW = z.shape
    L = S // d
    nb = min(4, L // QB)
    assert L % (nb * QB) == 0
    zv = z.reshape(L, d * ZW)
    scale = HEAD_DIM ** -0.5

    def body(q_ref, kc_ref, kp_ref, vc_ref, vp_ref, o_ref, m_ref, l_ref, kcat, vcat):
        b = pl.program_id(2)
        kcat[0:QB, :] = kp_ref[...].astype(BF16)
        kcat[QB:, :] = kc_ref[...].astype(BF16)
        vcat[0:QB, :] = vp_ref[...].astype(BF16)
        vcat[QB:, :] = vc_ref[...].astype(BF16)
        masks = _attn_masks()
        for i in range(nb):
            q = q_ref[i * QB:(i + 1) * QB, :]
            kk = kcat[i * QB:(i + 2) * QB, :]
            vv = vcat[i * QB:(i + 2) * QB, :]
            valid = _band_valid((b * nb + i) * QB)
            o_acc = m_acc = l_acc = None
            for hm in masks:
                qh = jnp.where(hm, q, 0.0).astype(BF16)
                s = lax.dot_general(qh, kk, (((1,), (1,)), ((), ())), preferred_element_type=F32) * scale
                s = jnp.where(valid, s, MASK_VALUE)
                m = jnp.max(s, axis=-1, keepdims=True)
                p = jnp.exp(s - m)
                l = jnp.sum(p, axis=-1, keepdims=True)
                o = jnp.dot(p.astype(BF16), vv, preferred_element_type=F32)
                if o_acc is None:
                    o_acc, m_acc, l_acc = o, jnp.broadcast_to(m, (QB, LANE)), jnp.broadcast_to(l, (QB, LANE))
                else:
                    o_acc = jnp.where(hm, o, o_acc)
                    m_acc = jnp.where(hm, m, m_acc)
                    l_acc = jnp.where(hm, l, l_acc)
            o_ref[i * QB:(i + 1) * QB, :] = o_acc
            m_ref[i * QB:(i + 1) * QB, :] = m_acc
            l_ref[i * QB:(i + 1) * QB, :] = l_acc

    qc, _ = _attn_specs(d, nb, ZW, 0)
    kc, kp = _attn_specs(d, nb, ZW, ATTN_W // LANE)
    vc, vp = _attn_specs(d, nb, ZW, 2 * ATTN_W // LANE)
    oc, _ = _attn_specs(d, nb, ATTN_W, 0)
    shp = jax.ShapeDtypeStruct((L, d * ATTN_W), F32)
    o, m, l = pl.pallas_call(
        body, name=f"attn_fwd_d{d}", grid=(ATTN_W // LANE, d, L // (nb * QB)),
        in_specs=[qc, kc, kp, vc, vp], out_specs=[oc, oc, oc], out_shape=[shp, shp, shp],
        scratch_shapes=[pltpu.VMEM(((nb + 1) * QB, LANE), BF16), pltpu.VMEM(((nb + 1) * QB, LANE), BF16)],
        compiler_params=_cp(("parallel", "parallel", "parallel")),
    )(zv, zv, zv, zv, zv)
    return o.reshape(S, ATTN_W), m.reshape(S, ATTN_W), l.reshape(S, ATTN_W)


def _attn_merge_math(o1, m1, l1, o2, m2, l2, o3, m3, l3):
    mg = jnp.maximum(jnp.maximum(m1, m2), m3)
    w1, w2, w3 = jnp.exp(m1 - mg), jnp.exp(m2 - mg), jnp.exp(m3 - mg)
    den = w1 * l1 + w2 * l2 + w3 * l3
    return (w1 * o1 + w2 * o2 + w3 * o3) / den, mg, den


def _attn_bwd(z, dya, ya, mg, den, d):
    S, ZW = z.shape
    L = S // d
    nb = min(4, L // QB)
    nsteps = L // (nb * QB)
    zv = z.reshape(L, d * ZW)
    view = lambda a: a.reshape(L, d * ATTN_W)
    scale = HEAD_DIM ** -0.5

    def body(q_ref, kc_ref, kp_ref, vc_ref, vp_ref, dy_ref, y_ref, m_ref, n_ref, dq_ref, dk_ref, dv_ref,
             kcat, vcat, dkcat, dvcat):
        b = pl.program_id(2)

        @pl.when(b == 0)
        def _():
            dk_ref[...] = jnp.zeros_like(dk_ref)
            dv_ref[...] = jnp.zeros_like(dv_ref)

        kcat[0:QB, :] = kp_ref[...].astype(BF16)
        kcat[QB:, :] = kc_ref[...].astype(BF16)
        vcat[0:QB, :] = vp_ref[...].astype(BF16)
        vcat[QB:, :] = vc_ref[...].astype(BF16)
        dkcat[...] = jnp.zeros_like(dkcat)
        dvcat[...] = jnp.zeros_like(dvcat)
        masks = _attn_masks()
        for i in range(nb):
            rows = slice(i * QB, (i + 1) * QB)
            q, dy, y = q_ref[rows, :], dy_ref[rows, :], y_ref[rows, :]
            mrow, nrow = m_ref[rows, :], n_ref[rows, :]
            kk = kcat[i * QB:(i + 2) * QB, :]
            vv = vcat[i * QB:(i + 2) * QB, :]
            valid = _band_valid((b * nb + i) * QB)
            dq_acc = jnp.zeros((QB, LANE), F32)
            dk_acc = jnp.zeros((2 * QB, LANE), F32)
            dv_acc = jnp.zeros((2 * QB, LANE), F32)
            for hm in masks:
                qh = jnp.where(hm, q, 0.0).astype(BF16)
                dyh = jnp.where(hm, dy, 0.0)
                dyb = dyh.astype(BF16)
                dsum = jnp.sum(dyh * y, axis=-1, keepdims=True)
                mh = jnp.max(jnp.where(hm, mrow, MASK_VALUE), axis=-1, keepdims=True)
                nh = jnp.max(jnp.where(hm, nrow, 0.0), axis=-1, keepdims=True)
                s = lax.dot_general(qh, kk, (((1,), (1,)), ((), ())), preferred_element_type=F32) * scale
                p = jnp.where(valid, jnp.exp(s - mh), 0.0) / nh
                pb = p.astype(BF16)
                dv_h = lax.dot_general(pb, dyb, (((0,), (0,)), ((), ())), preferred_element_type=F32)
                dp = lax.dot_general(dyb, vv, (((1,), (1,)), ((), ())), preferred_element_type=F32)
                ds = (p * (dp - dsum) * scale).astype(BF16)
                dq_h = jnp.dot(ds, kk, preferred_element_type=F32)
                dk_h = lax.dot_general(ds, qh, (((0,), (0,)), ((), ())), preferred_element_type=F32)
                dq_acc += jnp.where(hm, dq_h, 0.0)
                dk_acc += dk_h
                dv_acc += dv_h
            dq_ref[rows, :] = dq_acc
            dkcat[i * QB:(i + 2) * QB, :] += dk_acc
            dvcat[i * QB:(i + 2) * QB, :] += dv_acc

        base = pl.multiple_of(b * (nb * QB), QB)
        dk_ref[pl.ds(base, nb * QB), :] += dkcat[QB:, :]
        dv_ref[pl.ds(base, nb * QB), :] += dvcat[QB:, :]

        @pl.when(b > 0)
        def _():
            prev = pl.multiple_of(b * (nb * QB) - QB, QB)
            dk_ref[pl.ds(prev, QB), :] += dkcat[0:QB, :]
            dv_ref[pl.ds(prev, QB), :] += dvcat[0:QB, :]

    qc, _ = _attn_specs(d, nb, ZW, 0)
    kc, kp = _attn_specs(d, nb, ZW, ATTN_W // LANE)
    vc, vp = _attn_specs(d, nb, ZW, 2 * ATTN_W // LANE)
    oc, _ = _attn_specs(d, nb, ATTN_W, 0)
    per = ATTN_W // LANE
    whole = pl.BlockSpec((L, LANE), lambda hp, r, b: (0, r * per + hp))
    shp = jax.ShapeDtypeStruct((L, d * ATTN_W), F32)
    dq, dk, dv = pl.pallas_call(
        body, name=f"attn_bwd_d{d}", grid=(ATTN_W // LANE, d, nsteps),
        in_specs=[qc, kc, kp, vc, vp, oc, oc, oc, oc], out_specs=[oc, whole, whole], out_shape=[shp, shp, shp],
        scratch_shapes=[pltpu.VMEM(((nb + 1) * QB, LANE), BF16), pltpu.VMEM(((nb + 1) * QB, LANE), BF16),
                        pltpu.VMEM(((nb + 1) * QB, LANE), F32), pltpu.VMEM(((nb + 1) * QB, LANE), F32)],
        compiler_params=_cp(("parallel", "parallel", "arbitrary")),
    )(zv, zv, zv, zv, zv, view(dya), view(ya), view(mg), view(den))
    return dq.reshape(S, ATTN_W), dk.reshape(S, ATTN_W), dv.reshape(S, ATTN_W)


def _ssm_disc(lr, li, logdt, br, bi):
    dt = jnp.exp(logdt)
    mag = jnp.exp(lr * dt)
    ar = mag * jnp.cos(li * dt)
    ai = mag * jnp.sin(li * dt)
    nr, ni = ar - 1.0, ai
    den = lr * lr + li * li
    cr = (nr * lr + ni * li) / den
    ci = (ni * lr - nr * li) / den
    return ar, ai, cr * br - ci * bi, cr * bi + ci * br


def _ssm_prep(lr, li, logdt, br, bi):
    n, c = br.shape
    outs, _ = _rowwise("ssm_prep", lambda *a: (list(_ssm_disc(*a)), []), n, _tile(n, 512),
                       [_full(a) for a in (lr, li, logdt, br, bi)], [],
                       [(1, 1, _c0, F32), (1, 1, _c0, F32), (c, c, _c0, F32), (c, c, _c0, F32)])
    return outs


def _ssm_prep_bwd(lr, li, logdt, br, bi, dar, dai, dbbr, dbbi):
    n, c = br.shape

    def f(lrb, lib, dtb, brb, bib, *cts):
        _, vjp = jax.vjp(_ssm_disc, lrb, lib, dtb, brb, bib)
        return list(vjp(cts)), []

    outs, _ = _rowwise("ssm_prep_bwd", f, n, _tile(n, 512),
                       [_full(a) for a in (lr, li, logdt, br, bi, dar, dai, dbbr, dbbi)], [],
                       [(1, 1, _c0, F32)] * 3 + [(c, c, _c0, F32)] * 2)
    return outs


def _cmul(ar, ai, br, bi):
    return ar * br - ai * bi, ar * bi + ai * br


def _scan_consts(ar, ai, reverse):
    w = ar.shape[-1]
    a1 = (jnp.broadcast_to(ar, (8, w)), jnp.broadcast_to(ai, (8, w)))
    a2 = _cmul(*a1, *a1)
    a4 = _cmul(*a2, *a2)
    a8 = _cmul(*a4, *a4)
    row = lax.broadcasted_iota(jnp.int32, (8, w), 0)
    e = (8 - row) if reverse else (row + 1)
    one, zero = jnp.ones((8, w), F32), jnp.zeros((8, w), F32)
    pw = (one, zero)
    for bit, ap in ((1, a1), (2, a2), (4, a4), (8, a8)):
        sel = (e & bit) != 0
        nxt = _cmul(*pw, *ap)
        pw = (jnp.where(sel, nxt[0], pw[0]), jnp.where(sel, nxt[1], pw[1]))
    return (a1, a2, a4), pw, row


def _scan_group(xr, xi, cr, ci, consts, reverse):
    steps, pw, row = consts
    for sh, (pr, pi) in zip((1, 2, 4), steps):
        if reverse:
            sr, si = pltpu.roll(xr, 8 - sh, 0), pltpu.roll(xi, 8 - sh, 0)
            keep = row < 8 - sh
        else:
            sr, si = pltpu.roll(xr, sh, 0), pltpu.roll(xi, sh, 0)
            keep = row >= sh
        tr_, ti_ = _cmul(pr, pi, sr, si)
        xr = xr + jnp.where(keep, tr_, 0.0)
        xi = xi + jnp.where(keep, ti_, 0.0)
    tr_, ti_ = _cmul(pw[0], pw[1], cr, ci)
    return xr + tr_, xi + ti_


def _ssm_fwd(z, a_r, a_i, bdr, bdi, cmr, cmi, dskip, ts):
    S, ZW = z.shape
    NS = SSM_G * SSM_P
    PW = PACK * SSM_P
    uoff = (ZW - SSM_W) // LANE
    nsteps = S // ts

    def body(u_ref, ar_ref, ai_ref, bdr_ref, bdi_ref, cmr_ref, cmi_ref, d_ref, hr_ref, hi_ref, y_ref, car_r, car_i):
        s = pl.program_id(1)

        @pl.when(s == 0)
        def _():
            car_r[...] = jnp.zeros_like(car_r)
            car_i[...] = jnp.zeros_like(car_i)

        u = u_ref[...]
        ub = u.astype(BF16)
        nt = (((1,), (1,)), ((), ()))
        hr_ref[...] = lax.dot_general(ub, bdr_ref[...], nt, preferred_element_type=F32)
        hi_ref[...] = lax.dot_general(ub, bdi_ref[...], nt, preferred_element_type=F32)
        consts = _scan_consts(ar_ref[...], ai_ref[...], False)

        def step(j, carry):
            rows = pl.ds(pl.multiple_of(j * 8, 8), 8)
            hr, hi = _scan_group(hr_ref[rows, :], hi_ref[rows, :], carry[0], carry[1], consts, False)
            hr_ref[rows, :] = hr
            hi_ref[rows, :] = hi
            return jnp.broadcast_to(hr[7:8, :], (8, PW)), jnp.broadcast_to(hi[7:8, :], (8, PW))

        cr, ci = lax.fori_loop(0, ts // 8, step, (car_r[...], car_i[...]))
        car_r[...] = cr
        car_i[...] = ci
        y = lax.dot_general(hr_ref[...].astype(BF16), cmr_ref[...], nt, preferred_element_type=F32)
        y -= lax.dot_general(hi_ref[...].astype(BF16), cmi_ref[...], nt, preferred_element_type=F32)
        y_ref[...] = y + d_ref[...] * u

    row_a = pl.BlockSpec((1, PW), lambda i, s: (0, i))
    return pl.pallas_call(
        body, name="ssm_fwd", grid=(SSM_G // PACK, nsteps),
        in_specs=[pl.BlockSpec((ts, LANE), lambda i, s: (s, uoff + i)), row_a, row_a,
                  pl.BlockSpec((None, PW, LANE), lambda i, s: (i, 0, 0)), pl.BlockSpec((None, PW, LANE), lambda i, s: (i, 0, 0)),
                  pl.BlockSpec((None, LANE, PW), lambda i, s: (i, 0, 0)), pl.BlockSpec((None, LANE, PW), lambda i, s: (i, 0, 0)),
                  pl.BlockSpec((1, LANE), lambda i, s: (0, i))],
        out_specs=[pl.BlockSpec((ts, PW), lambda i, s: (s, i)), pl.BlockSpec((ts, PW), lambda i, s: (s, i)),
                   pl.BlockSpec((ts, LANE), lambda i, s: (s, i))],
        out_shape=[jax.ShapeDtypeStruct((S, NS), F32), jax.ShapeDtypeStruct((S, NS), F32),
                   jax.ShapeDtypeStruct((S, SSM_W), F32)],
        scratch_shapes=[pltpu.VMEM((8, PW), F32), pltpu.VMEM((8, PW), F32)],
        compiler_params=_cp(("parallel", "arbitrary")),
    )(z, a_r, a_i, bdr, bdi, cmr, cmi, dskip)


def _ssm_bwd(z, dyp, hr, hi, a_r, a_i, bdr, bdi, cmr, cmi, dskip, ts):
    S, ZW = z.shape
    NS = SSM_G * SSM_P
    PW = PACK * SSM_P
    uoff = (ZW - SSM_W) // LANE
    nsteps = S // ts
    npk = SSM_G // PACK

    def body(u_ref, dy_ref, hr_ref, hi_ref, hpr_ref, hpi_ref, ar_ref, ai_ref, bdr_ref, bdi_ref, cmr_ref, cmi_ref,
             d_ref, du_ref, dbdr_ref, dbdi_ref, dcmr_ref, dcmi_ref, dar_ref, dai_ref, dd_ref,
             lr_s, li_s, hcr, hci, car_r, car_i):
        s = pl.program_id(1)
        first_tile = s == nsteps - 1

        @pl.when(s == 0)
        def _():
            car_r[...] = jnp.zeros_like(car_r)
            car_i[...] = jnp.zeros_like(car_i)
            for r in (dbdr_ref, dbdi_ref, dcmr_ref, dcmi_ref, dar_ref, dai_ref, dd_ref):
                r[...] = jnp.zeros_like(r)

        u, dy = u_ref[...], dy_ref[...]
        ub, dyb = u.astype(BF16), dy.astype(BF16)
        lr_s[...] = jnp.dot(dyb, cmr_ref[...], preferred_element_type=F32)
        li_s[...] = -jnp.dot(dyb, cmi_ref[...], preferred_element_type=F32)
        keep_prev = jnp.where(first_tile, 0.0, 1.0)
        hcr[0:8, :] = hpr_ref[...] * keep_prev
        hci[0:8, :] = hpi_ref[...] * keep_prev
        hcr[8:, :] = hr_ref[...]
        hci[8:, :] = hi_ref[...]
        consts = _scan_consts(ar_ref[...], -ai_ref[...], True)
        row = consts[2]
        ngrp = ts // 8

        def step(jj, carry):
            cr, ci, accr, acci = carry
            j = ngrp - 1 - jj
            rows = pl.ds(pl.multiple_of(j * 8, 8), 8)
            nxt = pl.ds(pl.multiple_of(j * 8 + 8, 8), 8)
            lr, li = _scan_group(lr_s[rows, :], li_s[rows, :], cr, ci, consts, True)
            lr_s[rows, :] = lr
            li_s[rows, :] = li
            pr, pi = hcr[rows, :], hci[rows, :]
            hsr = jnp.where(row == 0, jnp.broadcast_to(pr[7:8, :], (8, PW)), pltpu.roll(hcr[nxt, :], 1, 0))
            hsi = jnp.where(row == 0, jnp.broadcast_to(pi[7:8, :], (8, PW)), pltpu.roll(hci[nxt, :], 1, 0))
            accr = accr + lr * hsr + li * hsi
            acci = acci + li * hsr - lr * hsi
            return jnp.broadcast_to(lr[0:1, :], (8, PW)), jnp.broadcast_to(li[0:1, :], (8, PW)), accr, acci

        zero = jnp.zeros((8, PW), F32)
        cr, ci, accr, acci = lax.fori_loop(0, ngrp, step, (car_r[...], car_i[...], zero, zero))
        car_r[...] = cr
        car_i[...] = ci
        dar_ref[...] += jnp.sum(accr, axis=0, keepdims=True)
        dai_ref[...] += jnp.sum(acci, axis=0, keepdims=True)
        lrb, lib = lr_s[...].astype(BF16), li_s[...].astype(BF16)
        du = jnp.dot(lrb, bdr_ref[...], preferred_element_type=F32)
        du += jnp.dot(lib, bdi_ref[...], preferred_element_type=F32)
        du_ref[...] = du + dy * d_ref[...]
        tn = (((0,), (0,)), ((), ()))
        dbdr_ref[...] += lax.dot_general(lrb, ub, tn, preferred_element_type=F32)
        dbdi_ref[...] += lax.dot_general(lib, ub, tn, preferred_element_type=F32)
        dcmr_ref[...] += lax.dot_general(dyb, hr_ref[...].astype(BF16), tn, preferred_element_type=F32)
        dcmi_ref[...] -= lax.dot_general(dyb, hi_ref[...].astype(BF16), tn, preferred_element_type=F32)
        dd_ref[...] += jnp.sum(dy * u, axis=0, keepdims=True)

    rev = lambda s: nsteps - 1 - s
    row_a = pl.BlockSpec((1, PW), lambda i, s: (0, i))
    tile = pl.BlockSpec((ts, PW), lambda i, s: (rev(s), i))
    prev8 = pl.BlockSpec((8, PW), lambda i, s: (jnp.maximum(rev(s) * (ts // 8) - 1, 0), i))
    cols = pl.BlockSpec((ts, LANE), lambda i, s: (rev(s), i))
    bd = pl.BlockSpec((None, PW, LANE), lambda i, s: (i, 0, 0))
    cm = pl.BlockSpec((None, LANE, PW), lambda i, s: (i, 0, 0))
    return pl.pallas_call(
        body, name="ssm_bwd", grid=(npk, nsteps),
        in_specs=[pl.BlockSpec((ts, LANE), lambda i, s: (rev(s), uoff + i)), cols, tile, tile, prev8, prev8,
                  row_a, row_a, bd, bd, cm, cm, pl.BlockSpec((1, LANE), lambda i, s: (0, i))],
        out_specs=[cols, bd, bd, cm, cm, row_a, row_a, pl.BlockSpec((1, LANE), lambda i, s: (0, i))],
        out_shape=[jax.ShapeDtypeStruct((S, SSM_W), F32),
                   jax.ShapeDtypeStruct((npk, PW, LANE), F32), jax.ShapeDtypeStruct((npk, PW, LANE), F32),
                   jax.ShapeDtypeStruct((npk, LANE, PW), F32), jax.ShapeDtypeStruct((npk, LANE, PW), F32),
                   jax.ShapeDtypeStruct((1, NS), F32), jax.ShapeDtypeStruct((1, NS), F32),
                   jax.ShapeDtypeStruct((1, SSM_W), F32)],
        scratch_shapes=[pltpu.VMEM((ts, PW), F32), pltpu.VMEM((ts, PW), F32),
                        pltpu.VMEM((ts + 8, PW), F32), pltpu.VMEM((ts + 8, PW), F32),
                        pltpu.VMEM((8, PW), F32), pltpu.VMEM((8, PW), F32)],
        compiler_params=_cp(("parallel", "arbitrary")),
    )(z, dyp, hr, hi, hr, hi, a_r, a_i, bdr, bdi, cmr, cmi, dskip)


def _block_diag(m4):
    npk, g, a, b = m4.shape
    eye = jnp.eye(g, dtype=m4.dtype)
    return (m4[:, :, :, None, :] * eye[None, :, None, :, None]).reshape(npk, g * a, g * b)


def _block_diag_take(m, a, b):
    npk = m.shape[0]
    m5 = m.reshape(npk, PACK, a, PACK, b)
    return jnp.stack([m5[:, g, :, g, :] for g in range(PACK)], axis=1)


def _swiglu_act(g, u):
    return jax.nn.silu(g) * u


def _mix_out(ya, ypre, gl, ga, gb, bglu):
    yg = jax.nn.gelu(ypre)
    yb = yg * jax.nn.sigmoid(gl + bglu)
    return jnp.concatenate([_rms(ya, ga), _rms(yb, gb)], axis=-1)


def _tail_loss(h3, gl, pe, gf, tgt):
    h4 = h3 + jax.nn.sigmoid(gl) * pe
    err = jnp.square(_rms(h4, gf) - tgt)
    return 0.5 * jnp.mean(err, axis=-1, keepdims=True)


def kernel(x, p, ffn1_norm, ffn1_w_gate, ffn1_w_up, ffn1_w_down, mix_norm, w_in, attn_out_norm, ssm_lambda_re, ssm_lambda_im, ssm_log_dt, ssm_b_re, ssm_b_im, ssm_c_re, ssm_c_im, ssm_d, ssm_w_glu, ssm_b_glu, ssm_out_norm, w_out, ffn2_norm, ffn2_w_gate, ffn2_w_up, ffn2_w_down, ple_norm, ple_w_gate, ple_w_proj, final_norm, loss_target, m_ffn1_norm, m_ffn1_w_gate, m_ffn1_w_up, m_ffn1_w_down, m_mix_norm, m_w_in, m_attn_out_norm, m_ssm_lambda_re, m_ssm_lambda_im, m_ssm_log_dt, m_ssm_b_re, m_ssm_b_im, m_ssm_c_re, m_ssm_c_im, m_ssm_d, m_ssm_w_glu, m_ssm_b_glu, m_ssm_out_norm, m_w_out, m_ffn2_norm, m_ffn2_w_gate, m_ffn2_w_up, m_ffn2_w_down, m_ple_norm, m_ple_w_gate, m_ple_w_proj, m_final_norm, v_ffn1_norm, v_ffn1_w_gate, v_ffn1_w_up, v_ffn1_w_down, v_mix_norm, v_w_in, v_attn_out_norm, v_ssm_lambda_re, v_ssm_lambda_im, v_ssm_log_dt, v_ssm_b_re, v_ssm_b_im, v_ssm_c_re, v_ssm_c_im, v_ssm_d, v_ssm_w_glu, v_ssm_b_glu, v_ssm_out_norm, v_w_out, v_ffn2_norm, v_ffn2_w_gate, v_ffn2_w_up, v_ffn2_w_down, v_ple_norm, v_ple_w_gate, v_ple_w_proj, v_final_norm):
    A = dict(locals())
    xs = x[0]
    ps = p[0, 0]
    tgt = loss_target[0]
    S, D = xs.shape
    FSH = ffn1_w_gate.shape[-1]
    FSP = -(-FSH // LANE) * LANE
    TR = _tile(S, 256)
    ZW = 3 * ATTN_W + SSM_W

    wgu1 = _prep("prep_gu1", [ffn1_w_gate[0], ffn1_w_up[0]], D, FSP)
    wgu2 = _prep("prep_gu2", [ffn2_w_gate[0], ffn2_w_up[0]], D, FSP)
    wd1 = _prep("prep_d1", [ffn1_w_down[0]], FSP, D)
    wd2 = _prep("prep_d2", [ffn2_w_down[0]], FSP, D)
    win = _prep("prep_in", [w_in[0]], D, w_in.shape[-1])
    wglu = _prep("prep_glu", [ssm_w_glu[0]], ssm_w_glu.shape[1], SSM_W)
    wout = _prep("prep_out", [w_out[0]], w_out.shape[1], D)
    wpg = _prep("prep_pg", [ple_w_gate[0]], ple_w_gate.shape[1], D)
    wpp = _prep("prep_pp", [ple_w_proj[0]], ple_w_proj.shape[1], ple_w_proj.shape[2])
    Wgu1, Wd1, Win, Wglu, Wout, Wgu2, Wd2, Wpg, Wpp = _all_gather(
        "ag_weights", [wgu1, wd1, win, wglu, wout, wgu2, wd2, wpg, wpp])
    rowstack = lambda w: w.reshape(1, w.shape[0] * w.shape[1], w.shape[2])
    Wd1, Wd2, Wglu, Wout, Wpg = (rowstack(w) for w in (Wd1, Wd2, Wglu, Wout, Wpg))

    def ffn_fwd(tag, h, gain, Wgu, Wd):
        (xn,), _ = _rowwise(f"{tag}_norm", lambda a, g: ([_rms(a, g)], []), S, TR, [_full(h)], [gain], [(D, D, _c0, BF16)])
        gu = _mm_nn(f"{tag}_up", xn, Wgu, tn=FSP, tk=D)
        (hid,), _ = _rowwise(f"{tag}_act", lambda gub: ([_swiglu_act(gub[:, :FSP], gub[:, FSP:])], []), S, TR,
                             [(gu, 2 * FSP, lambda j: j)], [], [(NDEV * FSP, FSP, lambda j: j, BF16)], ncol=NDEV)
        out = _mm_nn(f"{tag}_down", hid, Wd, tn=D, tk=FSP, res=h, scale=0.5)
        return out, (xn, gu, hid)

    h1, (xn1, gu1, hid1) = ffn_fwd("ffn1", xs, ffn1_norm, Wgu1, Wd1)

    (un,), _ = _rowwise("mix_norm", lambda a, g: ([_rms(a, g)], []), S, TR, [_full(h1)], [mix_norm], [(D, D, _c0, BF16)])
    z = _mm_nn("mix_in", un, Win, tn=512, tk=D)
    pats = [_attn_fwd(z, d) for d in DILATIONS]
    (ya, mg, den), _ = _rowwise("attn_merge", lambda *a: (list(_attn_merge_math(*a)), []), S, TR,
                                [_full(t) for pat in pats for t in pat], [], [(ATTN_W, ATTN_W, _c0, F32)] * 3)

    col = lambda a: a.reshape(-1, 1)
    lr_c, li_c = col(ssm_lambda_re), col(ssm_lambda_im)
    dt_c = col(jnp.broadcast_to(ssm_log_dt.reshape(SSM_G, 1), (SSM_G, SSM_P)))
    b_re2, b_im2 = ssm_b_re.reshape(-1, SSM_C), ssm_b_im.reshape(-1, SSM_C)
    ar_c, ai_c, bbr, bbi = _ssm_prep(lr_c, li_c, dt_c, b_re2, b_im2)
    a_r, a_i = ar_c.reshape(1, -1), ai_c.reshape(1, -1)
    npk = SSM_G // PACK
    bdr = _block_diag(bbr.reshape(npk, PACK, SSM_P, SSM_C)).astype(BF16)
    bdi = _block_diag(bbi.reshape(npk, PACK, SSM_P, SSM_C)).astype(BF16)
    cmr = _block_diag(ssm_c_re.reshape(npk, PACK, SSM_C, SSM_P)).astype(BF16)
    cmi = _block_diag(ssm_c_im.reshape(npk, PACK, SSM_C, SSM_P)).astype(BF16)
    TS = _tile(S, 512)
    hr, hi, ypre = _ssm_fwd(z, a_r, a_i, bdr, bdi, cmr, cmi, ssm_d, TS)
    (yg,), _ = _rowwise("ssm_gelu", lambda a: ([jax.nn.gelu(a)], []), S, TR, [_full(ypre)], [], [(SSM_W, SSM_W, _c0, BF16)])
    gl = _mm_nn("ssm_glu", yg, Wglu, tn=SSM_W, tk=SSM_W)
    (ycat,), _ = _rowwise("mix_out", lambda *a: ([_mix_out(*a)], []), S, TR, [_full(ya), _full(ypre), _full(gl)],
                          [attn_out_norm, ssm_out_norm, ssm_b_glu], [(MIX_W, MIX_W, _c0, BF16)])
    h2 = _mm_nn("mix_proj", ycat, Wout, tn=D // 2, tk=D, res=h1, scale=1.0)

    h3, (xn2, gu2, hid2) = ffn_fwd("ffn2", h2, ffn2_norm, Wgu2, Wd2)

    (hn, pb), _ = _rowwise("ple_norm", lambda a, q, g: ([_rms(a, g), q], []), S, TR, [_full(h3), _full(ps)], [ple_norm],
                           [(D, D, _c0, BF16), (ps.shape[1], ps.shape[1], _c0, BF16)])
    pgl = _mm_nn("ple_gate", hn, Wpg, tn=D // 2, tk=D)
    pe = _mm_nn("ple_proj", pb, Wpp, tn=Wpp.shape[2], tk=Wpp.shape[1])

    def tail(h3b, glb, peb, tb, gf):
        rows, vjp = jax.vjp(lambda a, b, c, g: _tail_loss(a, b, c, g, tb), h3b, glb, peb, gf)
        dh, dgl, dpe, dgf = vjp(jnp.ones_like(rows))
        return [dh, dgl, dpe], [jnp.broadcast_to(jnp.sum(rows, axis=0, keepdims=True), (1, LANE)), dgf]

    (dh3_dir, dpgl, dpe), (loss_row, g_final) = _rowwise(
        "tail", tail, S, TR, [_full(h3), _full(pgl), _full(pe), _full(tgt)], [final_norm.reshape(1, D)],
        [(D, D, _c0, F32), (D, D, _c0, BF16), (D, D, _c0, BF16)], [(LANE, LANE, _c0), (D, D, _c0)])
    loss = lax.psum(loss_row[0, 0], AXES)

    def norm_bwd(tag, h, gain, dn, dres):
        def f(hb, dnb, drb, g):
            _, vjp = jax.vjp(_rms, hb, g)
            dh, dg = vjp(dnb)
            dh = dh + drb
            return [dh, dh], [dg]
        (dh, dhb), (dg,) = _rowwise(f"{tag}_norm_bwd", f, S, TR, [_full(h), _full(dn), _full(dres)], [gain],
                                    [(D, D, _c0, F32), (D, D, _c0, BF16)], [(D, D, _c0)])
        return dh, dhb, dg

    dhn = _mm_nt("ple_gate_dx", dpgl, Wpg, tn=D, tk=D)
    g_wpg = _mm_tn("ple_gate_dw", hn, dpgl, 1)
    g_wpp = _mm_tn("ple_proj_dw", pb, dpe, NDEV)
    dh3, dh3b, g_ple_norm = norm_bwd("ple", h3, ple_norm, dhn, dh3_dir)

    def ffn_bwd(tag, h, gain, Wgu, Wd, saved, dout, doutb):
        xn, gu, hid = saved
        dhid = _mm_nt(f"{tag}_down_dx", doutb, Wd, tn=FSP, tk=D, scale=0.5)
        g_wd = _mm_tn(f"{tag}_down_dw", hid, doutb, 1, tko=FSP, tn=D // 2, scale=0.5)

        def act_bwd(gub, dhb):
            _, vjp = jax.vjp(_swiglu_act, gub[:, :FSP], gub[:, FSP:])
            return [jnp.concatenate(vjp(dhb), axis=-1)], []
        (dgu,), _ = _rowwise(f"{tag}_act_bwd", act_bwd, S, TR, [(gu, 2 * FSP, lambda j: j), (dhid, FSP, lambda j: j)], [],
                             [(2 * NDEV * FSP, 2 * FSP, lambda j: j, BF16)], ncol=NDEV)
        dxn = _mm_nt(f"{tag}_up_dx", dgu, Wgu, tn=D, tk=FSP)
        g_wgu = _mm_tn(f"{tag}_up_dw", xn, dgu, NDEV, tn=FSP)
        dh, dhb, g_norm = norm_bwd(tag, h, gain, dxn, dout)
        return dh, dhb, g_norm, g_wgu, g_wd

    dh2, dh2b, g_ffn2_norm, g_wgu2, g_wd2 = ffn_bwd("ffn2", h2, ffn2_norm, Wgu2, Wd2, (xn2, gu2, hid2), dh3, dh3b)

    dycat = _mm_nt("mix_proj_dx", dh2b, Wout, tn=D, tk=D)
    g_wout = _mm_tn("mix_proj_dw", ycat, dh2b, 1)

    def mix_out_bwd(yab, ypb, glb, dyc, ga, gb, bglu):
        _, vjp = jax.vjp(_mix_out, yab, ypb, glb, ga, gb, bglu)
        dya_, dyp_, dgl_, dga, dgb, dbg = vjp(dyc)
        return [dya_, dyp_, dgl_], [dga, dgb, dbg]
    (dya, dyp_dir, dglb), (g_attn_norm, g_ssm_norm, g_bglu) = _rowwise(
        "mix_out_bwd", mix_out_bwd, S, TR, [_full(ya), _full(ypre), _full(gl), _full(dycat)],
        [attn_out_norm, ssm_out_norm, ssm_b_glu],
        [(ATTN_W, ATTN_W, _c0, F32), (SSM_W, SSM_W, _c0, F32), (SSM_W, SSM_W, _c0, BF16)],
        [(ATTN_W, ATTN_W, _c0), (SSM_W, SSM_W, _c0), (SSM_W, SSM_W, _c0)])
    dyg = _mm_nt("ssm_glu_dx", dglb, Wglu, tn=SSM_W, tk=SSM_W)
    g_wglu = _mm_tn("ssm_glu_dw", yg, dglb, 1)

    def gelu_bwd(ypb, dygb, ddir):
        _, vjp = jax.vjp(jax.nn.gelu, ypb)
        return [ddir + vjp(dygb)[0]], []
    (dyp,), _ = _rowwise("ssm_gelu_bwd", gelu_bwd, S, TR, [_full(ypre), _full(dyg), _full(dyp_dir)], [],
                         [(SSM_W, SSM_W, _c0, F32)])
    du, dbdr, dbdi, dcmr, dcmi, da_r, da_i, g_ssm_d = _ssm_bwd(z, dyp, hr, hi, a_r, a_i, bdr, bdi, cmr, cmi, ssm_d, TS)
    dbbr = _block_diag_take(dbdr, SSM_P, SSM_C).reshape(-1, SSM_C)
    dbbi = _block_diag_take(dbdi, SSM_P, SSM_C).reshape(-1, SSM_C)
    g_c_re = _block_diag_take(dcmr, SSM_C, SSM_P).reshape(ssm_c_re.shape)
    g_c_im = _block_diag_take(dcmi, SSM_C, SSM_P).reshape(ssm_c_im.shape)
    dlr, dli, ddt, g_b_re, g_b_im = _ssm_prep_bwd(lr_c, li_c, dt_c, b_re2, b_im2, col(da_r), col(da_i), dbbr, dbbi)
    g_lam_re, g_lam_im = dlr.reshape(ssm_lambda_re.shape), dli.reshape(ssm_lambda_im.shape)
    g_log_dt = jnp.sum(ddt.reshape(SSM_G, SSM_P), axis=1).reshape(ssm_log_dt.shape)
    g_b_re, g_b_im = g_b_re.reshape(ssm_b_re.shape), g_b_im.reshape(ssm_b_im.shape)

    grads_qkv = [_attn_bwd(z, dya, ya, mg, den, d) for d in DILATIONS]

    def dz_cat(q1, k1, v1, q2, k2, v2, q3, k3, v3, dub):
        return [jnp.concatenate([q1 + q2 + q3, k1 + k2 + k3, v1 + v2 + v3, dub], axis=-1)], []
    (dz,), _ = _rowwise("mix_dz", dz_cat, S, TR, [_full(t) for g3 in grads_qkv for t in g3] + [_full(du)], [],
                        [(ZW, ZW, _c0, BF16)])
    dun = _mm_nt("mix_in_dx", dz, Win, tn=D, tk=512)
    g_win = _mm_tn("mix_in_dw", un, dz, NDEV, tn=512)
    dh1, dh1b, g_mix_norm = norm_bwd("mix", h1, mix_norm, dun, dh2)

    dx, _dxb, g_ffn1_norm, g_wgu1, g_wd1 = ffn_bwd("ffn1", xs, ffn1_norm, Wgu1, Wd1, (xn1, gu1, hid1), dh1, dh1b)

    restack = lambda g: g.reshape((NDEV, g.shape[1] // NDEV) + g.shape[2:])
    big = [g_wgu1, restack(g_wd1), g_win, restack(g_wglu), restack(g_wout), g_wgu2, restack(g_wd2), restack(g_wpg), g_wpp]
    mine = _reduce_scatter("rs", big)
    out = {}

    def upd(name, idx, *, tr, cw, gw, goff=0):
        w, m, v = A[name][0], A["m_" + name][0], A["v_" + name][0]
        g, dlt, mn, vn = _adamw("adamw_" + name, w, m, v, mine[idx], tr=tr, cw=cw, gw=gw, goff=goff)
        for k, val in (("grad_", g), ("delta_", dlt), ("new_m_", mn), ("new_v_", vn)):
            out[k + name] = val[None]

    DT = _tile(D, 256)
    FT = _tile(FSH, 512)
    DC = _tile(D, 1024, LANE)
    upd("ffn1_w_gate", 0, tr=DT, cw=FSH, gw=FSP, goff=0)
    upd("ffn1_w_up", 0, tr=DT, cw=FSH, gw=FSP, goff=1)
    upd("ffn1_w_down", 1, tr=FT, cw=DC, gw=DC)
    upd("w_in", 2, tr=DT, cw=w_in.shape[-1], gw=w_in.shape[-1])
    upd("ssm_w_glu", 3, tr=ssm_w_glu.shape[1], cw=SSM_W, gw=SSM_W)
    upd("w_out", 4, tr=w_out.shape[1], cw=DC, gw=DC)
    upd("ffn2_w_gate", 5, tr=DT, cw=FSH, gw=FSP, goff=0)
    upd("ffn2_w_up", 5, tr=DT, cw=FSH, gw=FSP, goff=1)
    upd("ffn2_w_down", 6, tr=FT, cw=DC, gw=DC)
    upd("ple_w_gate", 7, tr=ple_w_gate.shape[1], cw=DC, gw=DC)
    upd("ple_w_proj", 8, tr=ple_w_proj.shape[1], cw=ple_w_proj.shape[2], gw=ple_w_proj.shape[2])

    small = [("ffn1_norm", g_ffn1_norm), ("mix_norm", g_mix_norm), ("attn_out_norm", g_attn_norm),
             ("ssm_lambda_re", g_lam_re), ("ssm_lambda_im", g_lam_im), ("ssm_log_dt", g_log_dt),
             ("ssm_b_re", g_b_re), ("ssm_b_im", g_b_im), ("ssm_c_re", g_c_re), ("ssm_c_im", g_c_im),
             ("ssm_d", g_ssm_d), ("ssm_b_glu", g_bglu), ("ssm_out_norm", g_ssm_norm), ("ffn2_norm", g_ffn2_norm),
             ("ple_norm", g_ple_norm), ("final_norm", g_final)]
    chunk = 8 * LANE

    def pack(arrs):
        parts = []
        for a in arrs:
            flat = a.reshape(-1)
            padn = -(-flat.shape[0] // chunk) * chunk
            parts.append(jnp.pad(flat, (0, padn - flat.shape[0])).reshape(-1, LANE))
        return jnp.concatenate(parts, axis=0)

    g_pack = pack([g for _, g in small])
    (g_all,) = _all_gather("ag_small", [g_pack])
    g_sum = _sum8("small_sum", g_all)
    w_pack = pack([A[n] for n, _ in small])
    m_pack = pack([A["m_" + n] for n, _ in small])
    v_pack = pack([A["v_" + n] for n, _ in small])
    d_pack, mn_pack, vn_pack = _adamw_small("adamw_small", w_pack, m_pack, v_pack, g_sum)
    off = 0
    for n, _ in small:
        shape = A[n].shape
        size = math.prod(shape)
        rows = -(-size // chunk) * 8
        for k, buf in (("grad_", g_sum), ("delta_", d_pack), ("new_m_", mn_pack), ("new_v_", vn_pack)):
            out[k + n] = buf[off:off + rows].reshape(-1)[:size].reshape(shape)
        off += rows

    names = ['ffn1_norm', 'ffn1_w_gate', 'ffn1_w_up', 'ffn1_w_down', 'mix_norm', 'w_in', 'attn_out_norm',
             'ssm_lambda_re', 'ssm_lambda_im', 'ssm_log_dt', 'ssm_b_re', 'ssm_b_im', 'ssm_c_re', 'ssm_c_im', 'ssm_d',
             'ssm_w_glu', 'ssm_b_glu', 'ssm_out_norm', 'w_out', 'ffn2_norm', 'ffn2_w_gate', 'ffn2_w_up', 'ffn2_w_down',
             'ple_norm', 'ple_w_gate', 'ple_w_proj', 'final_norm']
    return (loss, dx[None], *[out[k + n] for k in ("grad_", "delta_", "new_m_", "new_v_") for n in names])
```

```python
import functools
import math

import jax
import jax.numpy as jnp
from jax import lax
from jax.experimental import pallas as pl
from jax.experimental.pallas import tpu as pltpu

F32, BF16 = jnp.float32, jnp.bfloat16
MESH = pl.DeviceIdType.MESH
NDEV = 8
AXES = ("x", "y", "c")
LANE = 128
VMEM_LIMIT = 56 * 1024 * 1024

ATTN_W = 1024
HEAD_DIM = 64
SSM_W = 1024
MIX_W = ATTN_W + SSM_W
SSM_G, SSM_P, SSM_C = 64, 64, 16
PACK = 8
DILATIONS = (1, 4, 16)
QB = 128
NORM_EPS = 1e-6
MASK_VALUE = -1e30
LR, B1, B2, EPS, WD, STEP = 0.001, 0.9, 0.999, 1e-08, 0.01, 10


def _cp(sem=None):
    return pltpu.CompilerParams(dimension_semantics=sem, vmem_limit_bytes=VMEM_LIMIT)


def _tile(n, target, mult=8):
    if n <= target:
        return n
    for t in range(target - target % mult, 0, -mult):
        if n % t == 0:
            return t
    return n


def _rms(x, g):
    return x * lax.rsqrt(jnp.mean(x * x, axis=-1, keepdims=True) + NORM_EPS) * g


def _rowwise(name, fn, S, tr, rows, fulls, outs, accs=(), ncol=1):
    nr, nf, no, na = len(rows), len(fulls), len(outs), len(accs)

    def body(*refs):
        ins = [r[...] for r in refs[:nr + nf]]
        o_refs = refs[nr + nf:nr + nf + no]
        a_refs = refs[nr + nf + no:]
        o_vals, a_vals = fn(*ins)
        for r, v in zip(o_refs, o_vals):
            r[...] = v.astype(r.dtype)
        if na:
            @pl.when(pl.program_id(1) == 0)
            def _():
                for r in a_refs:
                    r[...] = jnp.zeros_like(r)
            for r, v in zip(a_refs, a_vals):
                r[...] += v

    in_specs = [pl.BlockSpec((tr, w), functools.partial(lambda j, i, cm: (i, cm(j)), cm=cm)) for _, w, cm in rows]
    in_specs += [pl.BlockSpec(f.shape, functools.partial(lambda j, i, nd: (0,) * nd, nd=f.ndim)) for f in fulls]
    out_specs = [pl.BlockSpec((tr, w), functools.partial(lambda j, i, cm: (i, cm(j)), cm=cm)) for _, w, cm, _ in outs]
    out_specs += [pl.BlockSpec((1, w), functools.partial(lambda j, i, cm: (0, cm(j)), cm=cm)) for _, w, cm in accs]
    out_shape = [jax.ShapeDtypeStruct((S, c), dt) for c, _, _, dt in outs]
    out_shape += [jax.ShapeDtypeStruct((1, c), F32) for c, _, _ in accs]
    res = pl.pallas_call(
        body, name=name, grid=(ncol, S // tr), in_specs=in_specs, out_specs=out_specs, out_shape=out_shape,
        compiler_params=_cp(("parallel", "arbitrary" if na else "parallel")),
    )(*[a for a, _, _ in rows], *fulls)
    return res[:no], res[no:]


def _c0(j):
    return 0


def _full(a):
    return (a, a.shape[1], _c0)


def _mm_nn(name, a, w, *, out_dtype=F32, tm=512, tn=768, tk=2048, res=None, scale=1.0):
    M, K = a.shape
    J, K2, Np = w.shape
    assert K == K2
    tm, tn, tk = _tile(M, tm), _tile(Np, tn, LANE), _tile(K, tk, LANE)
    npj = Np // tn
    nk = K // tk

    def body(*refs):
        if res is None:
            a_ref, w_ref, o_ref, acc = refs
        else:
            a_ref, w_ref, r_ref, o_ref, acc = refs
        k = pl.program_id(2)
        part = jnp.dot(a_ref[...].astype(BF16), w_ref[...], preferred_element_type=F32)

        def finish(v):
            if res is not None:
                v = r_ref[...] + scale * v
            o_ref[...] = v.astype(o_ref.dtype)

        if nk == 1:
            finish(part)
            return

        @pl.when(k == 0)
        def _():
            acc[...] = part

        @pl.when(k > 0)
        def _():
            acc[...] += part

        @pl.when(k == nk - 1)
        def _():
            finish(acc[...])

    in_specs = [pl.BlockSpec((tm, tk), lambda i, n, k: (i, k)),
                pl.BlockSpec((None, tk, tn), lambda i, n, k: (n // npj, k, n % npj))]
    args = [a, w]
    if res is not None:
        in_specs.append(pl.BlockSpec((tm, tn), lambda i, n, k: (i, n)))
        args.append(res)
    return pl.pallas_call(
        body, name=name, grid=(M // tm, J * npj, nk), in_specs=in_specs,
        out_specs=pl.BlockSpec((tm, tn), lambda i, n, k: (i, n)),
        out_shape=jax.ShapeDtypeStruct((M, J * Np), out_dtype),
        scratch_shapes=[pltpu.VMEM((tm, tn), F32)],
        compiler_params=_cp(("parallel", "parallel", "arbitrary")),
    )(*args)


def _mm_nt(name, dy, w, *, out_dtype=F32, tm=512, tn=2048, tk=768, scale=1.0):
    M, N = dy.shape
    J, K, Np = w.shape
    assert N == J * Np
    tm, tn, tk = _tile(M, tm), _tile(K, tn, LANE), _tile(Np, tk, LANE)
    npj = Np // tk
    nc = J * npj

    def body(a_ref, w_ref, o_ref, acc):
        c = pl.program_id(2)
        part = lax.dot_general(a_ref[...].astype(BF16), w_ref[...], (((1,), (1,)), ((), ())),
                               preferred_element_type=F32)
        if nc == 1:
            o_ref[...] = (scale * part).astype(o_ref.dtype)
            return

        @pl.when(c == 0)
        def _():
            acc[...] = part

        @pl.when(c > 0)
        def _():
            acc[...] += part

        @pl.when(c == nc - 1)
        def _():
            o_ref[...] = (scale * acc[...]).astype(o_ref.dtype)

    return pl.pallas_call(
        body, name=name, grid=(M // tm, K // tn, nc),
        in_specs=[pl.BlockSpec((tm, tk), lambda i, n, c: (i, c)),
                  pl.BlockSpec((None, tn, tk), lambda i, n, c: (c // npj, n, c % npj))],
        out_specs=pl.BlockSpec((tm, tn), lambda i, n, c: (i, n)),
        out_shape=jax.ShapeDtypeStruct((M, K), out_dtype),
        scratch_shapes=[pltpu.VMEM((tm, tn), F32)],
        compiler_params=_cp(("parallel", "parallel", "arbitrary")),
    )(dy, w)


def _mm_tn(name, x, dy, J, *, tm=1024, tko=1024, tn=768, scale=1.0):
    M, K = x.shape
    M2, N = dy.shape
    assert M == M2 and N % J == 0
    Np = N // J
    tm, tko, tn = _tile(M, tm, LANE), _tile(K, tko, LANE), _tile(Np, tn, LANE)
    npj = Np // tn
    nm = M // tm

    def body(x_ref, d_ref, o_ref, acc):
        m = pl.program_id(2)
        part = lax.dot_general(x_ref[...].astype(BF16), d_ref[...].astype(BF16), (((0,), (0,)), ((), ())),
                               preferred_element_type=F32)
        if nm == 1:
            o_ref[...] = scale * part
            return

        @pl.when(m == 0)
        def _():
            acc[...] = part

        @pl.when(m > 0)
        def _():
            acc[...] += part

        @pl.when(m == nm - 1)
        def _():
            o_ref[...] = scale * acc[...]

    return pl.pallas_call(
        body, name=name, grid=(K // tko, J * npj, nm),
        in_specs=[pl.BlockSpec((tm, tko), lambda k, n, m: (m, k)),
                  pl.BlockSpec((tm, tn), lambda k, n, m: (m, n))],
        out_specs=pl.BlockSpec((None, tko, tn), lambda k, n, m: (n // npj, k, n % npj)),
        out_shape=jax.ShapeDtypeStruct((J, K, Np), F32),
        scratch_shapes=[pltpu.VMEM((tko, tn), F32)],
        compiler_params=_cp(("parallel", "parallel", "arbitrary")),
    )(x, dy)


def _swiglu_act(g, u):
    return jax.nn.silu(g) * u


def _ffn_up(name, xn, wgu, *, tm=1024):
    M, K = xn.shape
    J, _, F2 = wgu.shape
    F = F2 // 2
    tm = _tile(M, tm)

    def body(a_ref, w_ref, gu_ref, h_ref):
        r = jnp.dot(a_ref[...], w_ref[...], preferred_element_type=F32)
        gu_ref[...] = r.astype(gu_ref.dtype)
        h_ref[...] = _swiglu_act(r[:, :F], r[:, F:]).astype(h_ref.dtype)

    return pl.pallas_call(
        body, name=name, grid=(M // tm, J),
        in_specs=[pl.BlockSpec((tm, K), lambda i, j: (i, 0)), pl.BlockSpec((None, K, F2), lambda i, j: (j, 0, 0))],
        out_specs=[pl.BlockSpec((tm, F2), lambda i, j: (i, j)), pl.BlockSpec((tm, F), lambda i, j: (i, j))],
        out_shape=[jax.ShapeDtypeStruct((M, J * F2), BF16), jax.ShapeDtypeStruct((M, J * F), BF16)],
        compiler_params=_cp(("parallel", "parallel")),
    )(xn, wgu)


def _ffn_down_dx(name, dout, wd, gu, J, *, scale, tm=512):
    M, D = dout.shape
    F = wd.shape[1] // J
    tm = _tile(M, tm)

    def body(d_ref, w_ref, gu_ref, o_ref):
        dh = scale * lax.dot_general(d_ref[...], w_ref[...], (((1,), (1,)), ((), ())), preferred_element_type=F32)
        gu = gu_ref[...].astype(F32)
        _, vjp = jax.vjp(_swiglu_act, gu[:, :F], gu[:, F:])
        o_ref[...] = jnp.concatenate(vjp(dh), axis=-1).astype(o_ref.dtype)

    return pl.pallas_call(
        body, name=name, grid=(M // tm, J),
        in_specs=[pl.BlockSpec((tm, D), lambda i, j: (i, 0)), pl.BlockSpec((None, F, D), lambda i, j: (0, j, 0)),
                  pl.BlockSpec((tm, 2 * F), lambda i, j: (i, j))],
        out_specs=pl.BlockSpec((tm, 2 * F), lambda i, j: (i, j)),
        out_shape=jax.ShapeDtypeStruct((M, J * 2 * F), BF16),
        compiler_params=_cp(("parallel", "parallel")),
    )(dout, wd, gu)


def _all_gather(name, shards):
    n = len(shards)

    def body(*refs):
        ins, outs = refs[:n], refs[n:2 * n]
        send_sems, recv_sems, local_sems = refs[2 * n:]
        x, y, c = lax.axis_index("x"), lax.axis_index("y"), lax.axis_index("c")
        me, sibling = (x, y, c), (x, y, 1 - c)
        chips = [(1 - x, y), (x, 1 - y), (1 - x, 1 - y)]

        def blk(i, px, py, pc):
            return outs[i].at[4 * px + 2 * py + pc]

        def copy(i, k, block, to, src=None):
            return pltpu.make_async_remote_copy(
                src_ref=blk(i, *block) if src is None else src, dst_ref=blk(i, *block),
                send_sem=send_sems.at[i, k], recv_sem=recv_sems.at[i, k], device_id=to, device_id_type=MESH)

        mine = [pltpu.make_async_copy(ins[i], blk(i, *me), local_sems.at[i]) for i in range(n)]
        for cp in mine:
            cp.start()
        first = []
        for i in range(n):
            first.append(copy(i, 0, me, sibling, src=ins[i]))
            first += [copy(i, 1 + j, me, (*chip, c), src=ins[i]) for j, chip in enumerate(chips)]
        for cp in first:
            cp.start()
        passed = []
        for i in range(n):
            for j, chip in enumerate(chips):
                copy(i, 1 + j, (*chip, c), me).wait_recv()
                fwd = copy(i, 4 + j, (*chip, c), sibling)
                fwd.start()
                passed.append(fwd)
        for i in range(n):
            copy(i, 0, sibling, me).wait_recv()
            for j, chip in enumerate(chips):
                copy(i, 4 + j, (*chip, 1 - c), me).wait_recv()
        for cp in first + passed:
            cp.wait_send()
        for cp in mine:
            cp.wait()

    any_spec = pl.BlockSpec(memory_space=pl.ANY)
    return pl.pallas_call(
        body, name=name, in_specs=[any_spec] * n, out_specs=[any_spec] * n,
        out_shape=[jax.ShapeDtypeStruct((NDEV,) + s.shape, s.dtype) for s in shards],
        scratch_shapes=[pltpu.SemaphoreType.DMA((n, 7)), pltpu.SemaphoreType.DMA((n, 7)), pltpu.SemaphoreType.DMA((n,))],
    )(*shards)


def _swap_add(name, arr, streams, grid, tile, out_shape, out_spec):
    ns = len(streams)
    tr, C = tile
    n = grid[0] * grid[1]
    pos = {a: lax.axis_index(a) for a in AXES}
    sel = jnp.stack([v for a in AXES for v in (pos[a], 1 - pos[a])]).astype(jnp.int32)
    lead = (None,) * (arr.ndim - 2)

    def body(sel_ref, *refs):
        keeps, sends, o_ref = refs[0:2 * ns:2], refs[1:2 * ns:2], refs[2 * ns]
        scratch = refs[2 * ns + 1:]
        lands, stages = scratch[:ns], scratch[ns:2 * ns]
        send_sems, recv_sems, credits = scratch[2 * ns:]
        t = pl.program_id(0) * grid[1] + pl.program_id(1)
        slot = t % 2
        peers = []
        for axis, _, _, _ in streams:
            here = {a: lax.axis_index(a) for a in AXES}
            peers.append(tuple(1 - here[a] if a == axis else here[a] for a in AXES))

        @pl.when(t >= 2)
        def _():
            for s in range(ns):
                pl.semaphore_wait(credits.at[s], 1)

        rdmas = []
        for s in range(ns):
            stages[s][slot] = sends[s][...].astype(stages[s].dtype)
            rdma = pltpu.make_async_remote_copy(
                src_ref=stages[s].at[slot], dst_ref=lands[s].at[slot], send_sem=send_sems.at[s, slot],
                recv_sem=recv_sems.at[s, slot], device_id=peers[s], device_id_type=MESH)
            rdma.start()
            rdmas.append(rdma)
        for s in range(ns):
            rdmas[s].wait_recv()
            total = keeps[s][...] + lands[s][slot].astype(F32)
            if ns == 1:
                o_ref[...] = total
            else:
                o_ref[s] = total
        for rdma in rdmas:
            rdma.wait_send()

        @pl.when(t + 2 < n)
        def _():
            for s in range(ns):
                pl.semaphore_signal(credits.at[s], inc=1, device_id=peers[s], device_id_type=MESH)

    in_specs = []
    for _, _, keep_map, send_map in streams:
        in_specs += [pl.BlockSpec(lead + (tr, C), keep_map), pl.BlockSpec(lead + (tr, C), send_map)]
    wires = [w for _, w, _, _ in streams]
    grid_spec = pltpu.PrefetchScalarGridSpec(
        num_scalar_prefetch=1, grid=grid, in_specs=in_specs, out_specs=out_spec,
        scratch_shapes=[pltpu.VMEM((2, tr, C), w) for w in wires] + [pltpu.VMEM((2, tr, C), w) for w in wires]
        + [pltpu.SemaphoreType.DMA((ns, 2)), pltpu.SemaphoreType.DMA((ns, 2)), pltpu.SemaphoreType.REGULAR((ns,))])
    return pl.pallas_call(
        body, name=name, grid_spec=grid_spec, out_shape=jax.ShapeDtypeStruct(out_shape, F32),
        compiler_params=_cp(("arbitrary", "arbitrary")),
    )(sel, *([arr] * (2 * ns)))


def _reduce_scatter(name, grads):
    outs = []
    for i, g in enumerate(grads):
        _, R, C = g.shape
        R2 = R // 2
        tr = _tile(R2, 256, 16)
        nrh = R2 // tr
        g1 = _swap_add(
            f"{name}{i}_c", g.reshape(4, 2, R, C),
            [("c", F32, lambda b, i, s: (b, s[4], i, 0), lambda b, i, s: (b, s[5], i, 0))],
            (4, 2 * nrh), (tr, C), (2, 4, R2, C),
            pl.BlockSpec((None, None, tr, C), lambda b, i, s: (i // nrh, b, i % nrh, 0)))
        g2 = _swap_add(
            f"{name}{i}_yx", g1.reshape(2, 2, 2, R2, C),
            [("y", BF16, lambda b, i, s: (0, b, s[2], i, 0), lambda b, i, s: (0, b, s[3], i, 0)),
             ("x", BF16, lambda b, i, s: (1, s[0], b, i, 0), lambda b, i, s: (1, s[1], b, i, 0))],
            (2, nrh), (tr, C), (2, 2, R2, C), pl.BlockSpec((2, None, tr, C), lambda b, i, s: (0, b, i, 0)))
        g3 = _swap_add(
            f"{name}{i}_xy", g2,
            [("x", BF16, lambda b, i, s: (0, s[0], i, 0), lambda b, i, s: (0, s[1], i, 0)),
             ("y", BF16, lambda b, i, s: (1, s[2], i, 0), lambda b, i, s: (1, s[3], i, 0))],
            (1, nrh), (tr, C), (2, R2, C), pl.BlockSpec((2, tr, C), lambda b, i, s: (0, i, 0)))
        outs.append(g3.reshape(R, C))
    return outs


def _sum8(name, g):
    _, R, C = g.shape
    tr = _tile(R, 512)

    def body(g_ref, o_ref):
        acc = g_ref[0]
        for d in range(1, NDEV):
            acc = acc + g_ref[d]
        o_ref[...] = acc

    return pl.pallas_call(
        body, name=name, grid=(R // tr,), in_specs=[pl.BlockSpec((NDEV, tr, C), lambda i: (0, i, 0))],
        out_specs=pl.BlockSpec((tr, C), lambda i: (i, 0)), out_shape=jax.ShapeDtypeStruct((R, C), F32),
        compiler_params=_cp(("parallel",)),
    )(g)


def _adamw_math(w, g, m, v):
    m = B1 * m + (1.0 - B1) * g
    v = B2 * v + (1.0 - B2) * jnp.square(g)
    m_hat = m / (1.0 - B1 ** STEP)
    v_hat = v / (1.0 - B2 ** STEP)
    delta = -LR * (m_hat / (jnp.sqrt(v_hat) + EPS) + WD * w)
    return delta, m, v


def _adamw(name, w, m, v, gp, *, tr, cw, gw, goff=0):
    R, C = w.shape
    nc = C // cw
    nr = R // tr

    def body(w_ref, m_ref, v_ref, g_ref, g_out, d_out, m_out, v_out):
        g = g_ref[...][:, :cw]
        d, mn, vn = _adamw_math(w_ref[...], g, m_ref[...], v_ref[...])
        g_out[...] = g
        d_out[...] = d
        m_out[...] = mn
        v_out[...] = vn

    wspec = pl.BlockSpec((tr, cw), lambda i, j: (i, j))
    gspec = pl.BlockSpec((tr, gw), lambda i, j: (i, goff + j))
    return pl.pallas_call(
        body, name=name, grid=(nr, nc), in_specs=[wspec, wspec, wspec, gspec], out_specs=[wspec] * 4,
        out_shape=[jax.ShapeDtypeStruct((R, C), F32)] * 4, compiler_params=_cp(("parallel", "parallel")),
    )(w, m, v, gp)


def _adamw_small(name, w, m, v, g):
    R, C = w.shape

    def body(w_ref, m_ref, v_ref, g_ref, d_out, m_out, v_out):
        d, mn, vn = _adamw_math(w_ref[...], g_ref[...], m_ref[...], v_ref[...])
        d_out[...] = d
        m_out[...] = mn
        v_out[...] = vn

    tr = _tile(R, 512)
    spec = pl.BlockSpec((tr, C), lambda i: (i, 0))
    return pl.pallas_call(
        body, name=name, grid=(R // tr,), in_specs=[spec] * 4, out_specs=[spec] * 3,
        out_shape=[jax.ShapeDtypeStruct((R, C), F32)] * 3, compiler_params=_cp(("parallel",)),
    )(w, m, v, g)


def _prep(name, parts, rows_p, cols_p):
    R, C = parts[0].shape
    n = len(parts)

    def body(*refs):
        o_ref = refs[n]
        if (R, C) != (rows_p, cols_p):
            o_ref[...] = jnp.zeros_like(o_ref)
        for i in range(n):
            o_ref[0:R, i * cols_p:i * cols_p + C] = refs[i][...].astype(BF16)

    return pl.pallas_call(
        body, name=name, out_shape=jax.ShapeDtypeStruct((rows_p, n * cols_p), BF16), compiler_params=_cp(),
    )(*parts)


def _attn_masks():
    lane = lax.broadcasted_iota(jnp.int32, (1, LANE), 1)
    return [(lane < HEAD_DIM), (lane >= HEAD_DIM)]


def _band_valid(base):
    qi = lax.broadcasted_iota(jnp.int32, (QB, 2 * QB), 0)
    ki = lax.broadcasted_iota(jnp.int32, (QB, 2 * QB), 1)
    dist = qi + QB - ki
    return (dist >= 0) & (dist <= QB) & (base + ki - QB >= 0)


def _attn_specs(d, nb, width, offs):
    per = width // LANE
    cur = pl.BlockSpec((nb * QB, LANE), lambda hp, r, b: (b, r * per + offs + hp))
    prev = pl.BlockSpec((QB, LANE), lambda hp, r, b: (jnp.maximum(b * nb - 1, 0), r * per + offs + hp))
    return cur, prev


def _attn_fwd(z, d):
    S, ZW = z.shape
    L = S // d
    nb = min(4, L // QB)
    assert L % (nb * QB) == 0
    zv = z.reshape(L, d * ZW)
    scale = HEAD_DIM ** -0.5

    def body(q_ref, kc_ref, kp_ref, vc_ref, vp_ref, o_ref, m_ref, l_ref, kcat, vcat):
        b = pl.program_id(2)
        kcat[0:QB, :] = kp_ref[...].astype(BF16)
        kcat[QB:, :] = kc_ref[...].astype(BF16)
        vcat[0:QB, :] = vp_ref[...].astype(BF16)
        vcat[QB:, :] = vc_ref[...].astype(BF16)
        masks = _attn_masks()
        for i in range(nb):
            q = q_ref[i * QB:(i + 1) * QB, :]
            kk = kcat[i * QB:(i + 2) * QB, :]
            vv = vcat[i * QB:(i + 2) * QB, :]
            valid = _band_valid((b * nb + i) * QB)
            o_acc = m_acc = l_acc = None
            for hm in masks:
                qh = jnp.where(hm, q, 0.0).astype(BF16)
                s = lax.dot_general(qh, kk, (((1,), (1,)), ((), ())), preferred_element_type=F32) * scale
                s = jnp.where(valid, s, MASK_VALUE)
                m = jnp.max(s, axis=-1, keepdims=True)
                p = jnp.exp(s - m)
                l = jnp.sum(p, axis=-1, keepdims=True)
                o = jnp.dot(p.astype(BF16), vv, preferred_element_type=F32)
                if o_acc is None:
                    o_acc, m_acc, l_acc = o, jnp.broadcast_to(m, (QB, LANE)), jnp.broadcast_to(l, (QB, LANE))
                else:
                    o_acc = jnp.where(hm, o, o_acc)
                    m_acc = jnp.where(hm, m, m_acc)
                    l_acc = jnp.where(hm, l, l_acc)
            o_ref[i * QB:(i + 1) * QB, :] = o_acc
            m_ref[i * QB:(i + 1) * QB, :] = m_acc
            l_ref[i * QB:(i + 1) * QB, :] = l_acc

    qc, _ = _attn_specs(d, nb, ZW, 0)
    kc, kp = _attn_specs(d, nb, ZW, ATTN_W // LANE)
    vc, vp = _attn_specs(d, nb, ZW, 2 * ATTN_W // LANE)
    oc, _ = _attn_specs(d, nb, ATTN_W, 0)
    shp = jax.ShapeDtypeStruct((L, d * ATTN_W), F32)
    o, m, l = pl.pallas_call(
        body, name=f"attn_fwd_d{d}", grid=(ATTN_W // LANE, d, L // (nb * QB)),
        in_specs=[qc, kc, kp, vc, vp], out_specs=[oc, oc, oc], out_shape=[shp, shp, shp],
        scratch_shapes=[pltpu.VMEM(((nb + 1) * QB, LANE), BF16), pltpu.VMEM(((nb + 1) * QB, LANE), BF16)],
        compiler_params=_cp(("parallel", "parallel", "parallel")),
    )(zv, zv, zv, zv, zv)
    return o.reshape(S, ATTN_W), m.reshape(S, ATTN_W), l.reshape(S, ATTN_W)


def _attn_merge_math(o1, m1, l1, o2, m2, l2, o3, m3, l3):
    mg = jnp.maximum(jnp.maximum(m1, m2), m3)
    w1, w2, w3 = jnp.exp(m1 - mg), jnp.exp(m2 - mg), jnp.exp(m3 - mg)
    den = w1 * l1 + w2 * l2 + w3 * l3
    return (w1 * o1 + w2 * o2 + w3 * o3) / den, mg, den


def _attn_bwd(z, dya, ya, mg, den, d):
    S, ZW = z.shape
    L = S // d
    nb = min(4, L // QB)
    nsteps = L // (nb * QB)
    zv = z.reshape(L, d * ZW)
    view = lambda a: a.reshape(L, d * ATTN_W)
    scale = HEAD_DIM ** -0.5

    def body(q_ref, kc_ref, kp_ref, vc_ref, vp_ref, dy_ref, y_ref, m_ref, n_ref, dq_ref, dk_ref, dv_ref,
             kcat, vcat, dkcat, dvcat):
        b = pl.program_id(2)

        @pl.when(b == 0)
        def _():
            dk_ref[...] = jnp.zeros_like(dk_ref)
            dv_ref[...] = jnp.zeros_like(dv_ref)

        kcat[0:QB, :] = kp_ref[...].astype(BF16)
        kcat[QB:, :] = kc_ref[...].astype(BF16)
        vcat[0:QB, :] = vp_ref[...].astype(BF16)
        vcat[QB:, :] = vc_ref[...].astype(BF16)
        dkcat[...] = jnp.zeros_like(dkcat)
        dvcat[...] = jnp.zeros_like(dvcat)
        masks = _attn_masks()
        for i in range(nb):
            rows = slice(i * QB, (i + 1) * QB)
            q, dy, y = q_ref[rows, :], dy_ref[rows, :], y_ref[rows, :]
            mrow, nrow = m_ref[rows, :], n_ref[rows, :]
            kk = kcat[i * QB:(i + 2) * QB, :]
            vv = vcat[i * QB:(i + 2) * QB, :]
            valid = _band_valid((b * nb + i) * QB)
            dq_acc = jnp.zeros((QB, LANE), F32)
            dk_acc = jnp.zeros((2 * QB, LANE), F32)
            dv_acc = jnp.zeros((2 * QB, LANE), F32)
            for hm in masks:
                qh = jnp.where(hm, q, 0.0).astype(BF16)
                dyh = jnp.where(hm, dy, 0.0)
                dyb = dyh.astype(BF16)
                dsum = jnp.sum(dyh * y, axis=-1, keepdims=True)
                mh = jnp.max(jnp.where(hm, mrow, MASK_VALUE), axis=-1, keepdims=True)
                nh = jnp.max(jnp.where(hm, nrow, 0.0), axis=-1, keepdims=True)
                s = lax.dot_general(qh, kk, (((1,), (1,)), ((), ())), preferred_element_type=F32) * scale
                p = jnp.where(valid, jnp.exp(s - mh), 0.0) / nh
                pb = p.astype(BF16)
                dv_h = lax.dot_general(pb, dyb, (((0,), (0,)), ((), ())), preferred_element_type=F32)
                dp = lax.dot_general(dyb, vv, (((1,), (1,)), ((), ())), preferred_element_type=F32)
                ds = (p * (dp - dsum) * scale).astype(BF16)
                dq_h = jnp.dot(ds, kk, preferred_element_type=F32)
                dk_h = lax.dot_general(ds, qh, (((0,), (0,)), ((), ())), preferred_element_type=F32)
                dq_acc += jnp.where(hm, dq_h, 0.0)
                dk_acc += dk_h
                dv_acc += dv_h
            dq_ref[rows, :] = dq_acc
            dkcat[i * QB:(i + 2) * QB, :] += dk_acc
            dvcat[i * QB:(i + 2) * QB, :] += dv_acc

        base = pl.multiple_of(b * (nb * QB), QB)
        dk_ref[pl.ds(base, nb * QB), :] += dkcat[QB:, :]
        dv_ref[pl.ds(base, nb * QB), :] += dvcat[QB:, :]

        @pl.when(b > 0)
        def _():
            prev = pl.multiple_of(b * (nb * QB) - QB, QB)
            dk_ref[pl.ds(prev, QB), :] += dkcat[0:QB, :]
            dv_ref[pl.ds(prev, QB), :] += dvcat[0:QB, :]

    qc, _ = _attn_specs(d, nb, ZW, 0)
    kc, kp = _attn_specs(d, nb, ZW, ATTN_W // LANE)
    vc, vp = _attn_specs(d, nb, ZW, 2 * ATTN_W // LANE)
    oc, _ = _attn_specs(d, nb, ATTN_W, 0)
    per = ATTN_W // LANE
    whole = pl.BlockSpec((L, LANE), lambda hp, r, b: (0, r * per + hp))
    shp = jax.ShapeDtypeStruct((L, d * ATTN_W), F32)
    dq, dk, dv = pl.pallas_call(
        body, name=f"attn_bwd_d{d}", grid=(ATTN_W // LANE, d, nsteps),
        in_specs=[qc, kc, kp, vc, vp, oc, oc, oc, oc], out_specs=[oc, whole, whole], out_shape=[shp, shp, shp],
        scratch_shapes=[pltpu.VMEM(((nb + 1) * QB, LANE), BF16), pltpu.VMEM(((nb + 1) * QB, LANE), BF16),
                        pltpu.VMEM(((nb + 1) * QB, LANE), F32), pltpu.VMEM(((nb + 1) * QB, LANE), F32)],
        compiler_params=_cp(("parallel", "parallel", "arbitrary")),
    )(zv, zv, zv, zv, zv, view(dya), view(ya), view(mg), view(den))
    return dq.reshape(S, ATTN_W), dk.reshape(S, ATTN_W), dv.reshape(S, ATTN_W)


def _ssm_disc(lr, li, logdt, br, bi):
    dt = jnp.exp(logdt)
    mag = jnp.exp(lr * dt)
    ar = mag * jnp.cos(li * dt)
    ai = mag * jnp.sin(li * dt)
    nr, ni = ar - 1.0, ai
    den = lr * lr + li * li
    cr = (nr * lr + ni * li) / den
    ci = (ni * lr - nr * li) / den
    return ar, ai, cr * br - ci * bi, cr * bi + ci * br


def _ssm_prep(lr, li, logdt, br, bi):
    n, c = br.shape
    outs, _ = _rowwise("ssm_prep", lambda *a: (list(_ssm_disc(*a)), []), n, _tile(n, 512),
                       [_full(a) for a in (lr, li, logdt, br, bi)], [],
                       [(1, 1, _c0, F32), (1, 1, _c0, F32), (c, c, _c0, F32), (c, c, _c0, F32)])
    return outs


def _ssm_prep_bwd(lr, li, logdt, br, bi, dar, dai, dbbr, dbbi):
    n, c = br.shape

    def f(lrb, lib, dtb, brb, bib, *cts):
        _, vjp = jax.vjp(_ssm_disc, lrb, lib, dtb, brb, bib)
        return list(vjp(cts)), []

    outs, _ = _rowwise("ssm_prep_bwd", f, n, _tile(n, 512),
                       [_full(a) for a in (lr, li, logdt, br, bi, dar, dai, dbbr, dbbi)], [],
                       [(1, 1, _c0, F32)] * 3 + [(c, c, _c0, F32)] * 2)
    return outs


def _cmul(ar, ai, br, bi):
    return ar * br - ai * bi, ar * bi + ai * br


def _scan_consts(ar, ai, reverse):
    w = ar.shape[-1]
    a1 = (jnp.broadcast_to(ar, (8, w)), jnp.broadcast_to(ai, (8, w)))
    a2 = _cmul(*a1, *a1)
    a4 = _cmul(*a2, *a2)
    a8 = _cmul(*a4, *a4)
    row = lax.broadcasted_iota(jnp.int32, (8, w), 0)
    e = (8 - row) if reverse else (row + 1)
    one, zero = jnp.ones((8, w), F32), jnp.zeros((8, w), F32)
    pw = (one, zero)
    for bit, ap in ((1, a1), (2, a2), (4, a4), (8, a8)):
        sel = (e & bit) != 0
        nxt = _cmul(*pw, *ap)
        pw = (jnp.where(sel, nxt[0], pw[0]), jnp.where(sel, nxt[1], pw[1]))
    return (a1, a2, a4), pw, row


def _scan_group(xr, xi, cr, ci, consts, reverse):
    steps, pw, row = consts
    for sh, (pr, pi) in zip((1, 2, 4), steps):
        if reverse:
            sr, si = pltpu.roll(xr, 8 - sh, 0), pltpu.roll(xi, 8 - sh, 0)
            keep = row < 8 - sh
        else:
            sr, si = pltpu.roll(xr, sh, 0), pltpu.roll(xi, sh, 0)
            keep = row >= sh
        tr_, ti_ = _cmul(pr, pi, sr, si)
        xr = xr + jnp.where(keep, tr_, 0.0)
        xi = xi + jnp.where(keep, ti_, 0.0)
    tr_, ti_ = _cmul(pw[0], pw[1], cr, ci)
    return xr + tr_, xi + ti_


def _ssm_fwd(z, a_r, a_i, bdr, bdi, cmr, cmi, dskip, ts):
    S, ZW = z.shape
    NS = SSM_G * SSM_P
    PW = PACK * SSM_P
    uoff = (ZW - SSM_W) // LANE
    nsteps = S // ts

    def body(u_ref, ar_ref, ai_ref, bdr_ref, bdi_ref, cmr_ref, cmi_ref, d_ref, hr_ref, hi_ref, y_ref, car_r, car_i):
        s = pl.program_id(1)

        @pl.when(s == 0)
        def _():
            car_r[...] = jnp.zeros_like(car_r)
            car_i[...] = jnp.zeros_like(car_i)

        u = u_ref[...]
        ub = u.astype(BF16)
        nt = (((1,), (1,)), ((), ()))
        hr_ref[...] = lax.dot_general(ub, bdr_ref[...], nt, preferred_element_type=F32)
        hi_ref[...] = lax.dot_general(ub, bdi_ref[...], nt, preferred_element_type=F32)
        consts = _scan_consts(ar_ref[...], ai_ref[...], False)

        def step(j, carry):
            rows = pl.ds(pl.multiple_of(j * 8, 8), 8)
            hr, hi = _scan_group(hr_ref[rows, :], hi_ref[rows, :], carry[0], carry[1], consts, False)
            hr_ref[rows, :] = hr
            hi_ref[rows, :] = hi
            return jnp.broadcast_to(hr[7:8, :], (8, PW)), jnp.broadcast_to(hi[7:8, :], (8, PW))

        cr, ci = lax.fori_loop(0, ts // 8, step, (car_r[...], car_i[...]))
        car_r[...] = cr
        car_i[...] = ci
        y = lax.dot_general(hr_ref[...].astype(BF16), cmr_ref[...], nt, preferred_element_type=F32)
        y -= lax.dot_general(hi_ref[...].astype(BF16), cmi_ref[...], nt, preferred_element_type=F32)
        y_ref[...] = y + d_ref[...] * u

    row_a = pl.BlockSpec((1, PW), lambda i, s: (0, i))
    return pl.pallas_call(
        body, name="ssm_fwd", grid=(SSM_G // PACK, nsteps),
        in_specs=[pl.BlockSpec((ts, LANE), lambda i, s: (s, uoff + i)), row_a, row_a,
                  pl.BlockSpec((None, PW, LANE), lambda i, s: (i, 0, 0)), pl.BlockSpec((None, PW, LANE), lambda i, s: (i, 0, 0)),
                  pl.BlockSpec((None, LANE, PW), lambda i, s: (i, 0, 0)), pl.BlockSpec((None, LANE, PW), lambda i, s: (i, 0, 0)),
                  pl.BlockSpec((1, LANE), lambda i, s: (0, i))],
        out_specs=[pl.BlockSpec((ts, PW), lambda i, s: (s, i)), pl.BlockSpec((ts, PW), lambda i, s: (s, i)),
                   pl.BlockSpec((ts, LANE), lambda i, s: (s, i))],
        out_shape=[jax.ShapeDtypeStruct((S, NS), F32), jax.ShapeDtypeStruct((S, NS), F32),
                   jax.ShapeDtypeStruct((S, SSM_W), F32)],
        scratch_shapes=[pltpu.VMEM((8, PW), F32), pltpu.VMEM((8, PW), F32)],
        compiler_params=_cp(("parallel", "arbitrary")),
    )(z, a_r, a_i, bdr, bdi, cmr, cmi, dskip)


def _ssm_bwd(z, dyp, hr, hi, a_r, a_i, bdr, bdi, cmr, cmi, dskip, ts):
    S, ZW = z.shape
    NS = SSM_G * SSM_P
    PW = PACK * SSM_P
    uoff = (ZW - SSM_W) // LANE
    nsteps = S // ts
    npk = SSM_G // PACK

    def body(u_ref, dy_ref, hr_ref, hi_ref, hpr_ref, hpi_ref, ar_ref, ai_ref, bdr_ref, bdi_ref, cmr_ref, cmi_ref,
             d_ref, du_ref, dbdr_ref, dbdi_ref, dcmr_ref, dcmi_ref, dar_ref, dai_ref, dd_ref,
             lr_s, li_s, hcr, hci, car_r, car_i):
        s = pl.program_id(1)
        first_tile = s == nsteps - 1

        @pl.when(s == 0)
        def _():
            car_r[...] = jnp.zeros_like(car_r)
            car_i[...] = jnp.zeros_like(car_i)
            for r in (dbdr_ref, dbdi_ref, dcmr_ref, dcmi_ref, dar_ref, dai_ref, dd_ref):
                r[...] = jnp.zeros_like(r)

        u, dy = u_ref[...], dy_ref[...]
        ub, dyb = u.astype(BF16), dy.astype(BF16)
        lr_s[...] = jnp.dot(dyb, cmr_ref[...], preferred_element_type=F32)
        li_s[...] = -jnp.dot(dyb, cmi_ref[...], preferred_element_type=F32)
        keep_prev = jnp.where(first_tile, 0.0, 1.0)
        hcr[0:8, :] = hpr_ref[...] * keep_prev
        hci[0:8, :] = hpi_ref[...] * keep_prev
        hcr[8:, :] = hr_ref[...]
        hci[8:, :] = hi_ref[...]
        consts = _scan_consts(ar_ref[...], -ai_ref[...], True)
        row = consts[2]
        ngrp = ts // 8

        def step(jj, carry):
            cr, ci, accr, acci = carry
            j = ngrp - 1 - jj
            rows = pl.ds(pl.multiple_of(j * 8, 8), 8)
            nxt = pl.ds(pl.multiple_of(j * 8 + 8, 8), 8)
            lr, li = _scan_group(lr_s[rows, :], li_s[rows, :], cr, ci, consts, True)
            lr_s[rows, :] = lr
            li_s[rows, :] = li
            pr, pi = hcr[rows, :], hci[rows, :]
            hsr = jnp.where(row == 0, jnp.broadcast_to(pr[7:8, :], (8, PW)), pltpu.roll(hcr[nxt, :], 1, 0))
            hsi = jnp.where(row == 0, jnp.broadcast_to(pi[7:8, :], (8, PW)), pltpu.roll(hci[nxt, :], 1, 0))
            accr = accr + lr * hsr + li * hsi
            acci = acci + li * hsr - lr * hsi
            return jnp.broadcast_to(lr[0:1, :], (8, PW)), jnp.broadcast_to(li[0:1, :], (8, PW)), accr, acci

        zero = jnp.zeros((8, PW), F32)
        cr, ci, accr, acci = lax.fori_loop(0, ngrp, step, (car_r[...], car_i[...], zero, zero))
        car_r[...] = cr
        car_i[...] = ci
        dar_ref[...] += jnp.sum(accr, axis=0, keepdims=True)
        dai_ref[...] += jnp.sum(acci, axis=0, keepdims=True)
        lrb, lib = lr_s[...].astype(BF16), li_s[...].astype(BF16)
        du = jnp.dot(lrb, bdr_ref[...], preferred_element_type=F32)
        du += jnp.dot(lib, bdi_ref[...], preferred_element_type=F32)
        du_ref[...] = du + dy * d_ref[...]
        tn = (((0,), (0,)), ((), ()))
        dbdr_ref[...] += lax.dot_general(lrb, ub, tn, preferred_element_type=F32)
        dbdi_ref[...] += lax.dot_general(lib, ub, tn, preferred_element_type=F32)
        dcmr_ref[...] += lax.dot_general(dyb, hr_ref[...].astype(BF16), tn, preferred_element_type=F32)
        dcmi_ref[...] -= lax.dot_general(dyb, hi_ref[...].astype(BF16), tn, preferred_element_type=F32)
        dd_ref[...] += jnp.sum(dy * u, axis=0, keepdims=True)

    rev = lambda s: nsteps - 1 - s
    row_a = pl.BlockSpec((1, PW), lambda i, s: (0, i))
    tile = pl.BlockSpec((ts, PW), lambda i, s: (rev(s), i))
    prev8 = pl.BlockSpec((8, PW), lambda i, s: (jnp.maximum(rev(s) * (ts // 8) - 1, 0), i))
    cols = pl.BlockSpec((ts, LANE), lambda i, s: (rev(s), i))
    bd = pl.BlockSpec((None, PW, LANE), lambda i, s: (i, 0, 0))
    cm = pl.BlockSpec((None, LANE, PW), lambda i, s: (i, 0, 0))
    return pl.pallas_call(
        body, name="ssm_bwd", grid=(npk, nsteps),
        in_specs=[pl.BlockSpec((ts, LANE), lambda i, s: (rev(s), uoff + i)), cols, tile, tile, prev8, prev8,
                  row_a, row_a, bd, bd, cm, cm, pl.BlockSpec((1, LANE), lambda i, s: (0, i))],
        out_specs=[cols, bd, bd, cm, cm, row_a, row_a, pl.BlockSpec((1, LANE), lambda i, s: (0, i))],
        out_shape=[jax.ShapeDtypeStruct((S, SSM_W), F32),
                   jax.ShapeDtypeStruct((npk, PW, LANE), F32), jax.ShapeDtypeStruct((npk, PW, LANE), F32),
                   jax.ShapeDtypeStruct((npk, LANE, PW), F32), jax.ShapeDtypeStruct((npk, LANE, PW), F32),
                   jax.ShapeDtypeStruct((1, NS), F32), jax.ShapeDtypeStruct((1, NS), F32),
                   jax.ShapeDtypeStruct((1, SSM_W), F32)],
        scratch_shapes=[pltpu.VMEM((ts, PW), F32), pltpu.VMEM((ts, PW), F32),
                        pltpu.VMEM((ts + 8, PW), F32), pltpu.VMEM((ts + 8, PW), F32),
                        pltpu.VMEM((8, PW), F32), pltpu.VMEM((8, PW), F32)],
        compiler_params=_cp(("parallel", "arbitrary")),
    )(z, dyp, hr, hi, hr, hi, a_r, a_i, bdr, bdi, cmr, cmi, dskip)


def _block_diag(m4):
    npk, g, a, b = m4.shape
    eye = jnp.eye(g, dtype=m4.dtype)
    return (m4[:, :, :, None, :] * eye[None, :, None, :, None]).reshape(npk, g * a, g * b)


def _block_diag_take(m, a, b):
    npk = m.shape[0]
    m5 = m.reshape(npk, PACK, a, PACK, b)
    return jnp.stack([m5[:, g, :, g, :] for g in range(PACK)], axis=1)


def _mix_out(ya, ypre, gl, ga, gb, bglu):
    yg = jax.nn.gelu(ypre)
    yb = yg * jax.nn.sigmoid(gl + bglu)
    return jnp.concatenate([_rms(ya, ga), _rms(yb, gb)], axis=-1)


def _tail_loss(h3, gl, pe, gf, tgt):
    h4 = h3 + jax.nn.sigmoid(gl) * pe
    err = jnp.square(_rms(h4, gf) - tgt)
    return 0.5 * jnp.mean(err, axis=-1, keepdims=True)


def kernel(x, p, ffn1_norm, ffn1_w_gate, ffn1_w_up, ffn1_w_down, mix_norm, w_in, attn_out_norm, ssm_lambda_re, ssm_lambda_im, ssm_log_dt, ssm_b_re, ssm_b_im, ssm_c_re, ssm_c_im, ssm_d, ssm_w_glu, ssm_b_glu, ssm_out_norm, w_out, ffn2_norm, ffn2_w_gate, ffn2_w_up, ffn2_w_down, ple_norm, ple_w_gate, ple_w_proj, final_norm, loss_target, m_ffn1_norm, m_ffn1_w_gate, m_ffn1_w_up, m_ffn1_w_down, m_mix_norm, m_w_in, m_attn_out_norm, m_ssm_lambda_re, m_ssm_lambda_im, m_ssm_log_dt, m_ssm_b_re, m_ssm_b_im, m_ssm_c_re, m_ssm_c_im, m_ssm_d, m_ssm_w_glu, m_ssm_b_glu, m_ssm_out_norm, m_w_out, m_ffn2_norm, m_ffn2_w_gate, m_ffn2_w_up, m_ffn2_w_down, m_ple_norm, m_ple_w_gate, m_ple_w_proj, m_final_norm, v_ffn1_norm, v_ffn1_w_gate, v_ffn1_w_up, v_ffn1_w_down, v_mix_norm, v_w_in, v_attn_out_norm, v_ssm_lambda_re, v_ssm_lambda_im, v_ssm_log_dt, v_ssm_b_re, v_ssm_b_im, v_ssm_c_re, v_ssm_c_im, v_ssm_d, v_ssm_w_glu, v_ssm_b_glu, v_ssm_out_norm, v_w_out, v_ffn2_norm, v_ffn2_w_gate, v_ffn2_w_up, v_ffn2_w_down, v_ple_norm, v_ple_w_gate, v_ple_w_proj, v_final_norm):
    A = dict(locals())
    xs = x[0]
    ps = p[0, 0]
    tgt = loss_target[0]
    S, D = xs.shape
    FSH = ffn1_w_gate.shape[-1]
    FSP = -(-FSH // LANE) * LANE
    TR = _tile(S, 256)
    ZW = 3 * ATTN_W + SSM_W

    wgu1 = _prep("prep_gu1", [ffn1_w_gate[0], ffn1_w_up[0]], D, FSP)
    wgu2 = _prep("prep_gu2", [ffn2_w_gate[0], ffn2_w_up[0]], D, FSP)
    wd1 = _prep("prep_d1", [ffn1_w_down[0]], FSP, D)
    wd2 = _prep("prep_d2", [ffn2_w_down[0]], FSP, D)
    win = _prep("prep_in", [w_in[0]], D, w_in.shape[-1])
    wglu = _prep("prep_glu", [ssm_w_glu[0]], ssm_w_glu.shape[1], SSM_W)
    wout = _prep("prep_out", [w_out[0]], w_out.shape[1], D)
    wpg = _prep("prep_pg", [ple_w_gate[0]], ple_w_gate.shape[1], D)
    wpp = _prep("prep_pp", [ple_w_proj[0]], ple_w_proj.shape[1], ple_w_proj.shape[2])
    Wgu1, Wd1, Win, Wglu, Wout, Wgu2, Wd2, Wpg, Wpp = _all_gather(
        "ag_weights", [wgu1, wd1, win, wglu, wout, wgu2, wd2, wpg, wpp])
    rowstack = lambda w: w.reshape(1, w.shape[0] * w.shape[1], w.shape[2])
    Wd1, Wd2, Wglu, Wout, Wpg = (rowstack(w) for w in (Wd1, Wd2, Wglu, Wout, Wpg))

    def ffn_fwd(tag, h, gain, Wgu, Wd):
        (xn,), _ = _rowwise(f"{tag}_norm", lambda a, g: ([_rms(a, g)], []), S, TR, [_full(h)], [gain], [(D, D, _c0, BF16)])
        gu, hid = _ffn_up(f"{tag}_up", xn, Wgu)
        out = _mm_nn(f"{tag}_down", hid, Wd, tn=D, tk=2 * FSP, res=h, scale=0.5)
        return out, (xn, gu, hid)

    h1, (xn1, gu1, hid1) = ffn_fwd("ffn1", xs, ffn1_norm, Wgu1, Wd1)

    (un,), _ = _rowwise("mix_norm", lambda a, g: ([_rms(a, g)], []), S, TR, [_full(h1)], [mix_norm], [(D, D, _c0, BF16)])
    z = _mm_nn("mix_in", un, Win, tn=512, tk=D)
    pats = [_attn_fwd(z, d) for d in DILATIONS]
    (ya, mg, den), _ = _rowwise("attn_merge", lambda *a: (list(_attn_merge_math(*a)), []), S, TR,
                                [_full(t) for pat in pats for t in pat], [], [(ATTN_W, ATTN_W, _c0, F32)] * 3)

    col = lambda a: a.reshape(-1, 1)
    lr_c, li_c = col(ssm_lambda_re), col(ssm_lambda_im)
    dt_c = col(jnp.broadcast_to(ssm_log_dt.reshape(SSM_G, 1), (SSM_G, SSM_P)))
    b_re2, b_im2 = ssm_b_re.reshape(-1, SSM_C), ssm_b_im.reshape(-1, SSM_C)
    ar_c, ai_c, bbr, bbi = _ssm_prep(lr_c, li_c, dt_c, b_re2, b_im2)
    a_r, a_i = ar_c.reshape(1, -1), ai_c.reshape(1, -1)
    npk = SSM_G // PACK
    bdr = _block_diag(bbr.reshape(npk, PACK, SSM_P, SSM_C)).astype(BF16)
    bdi = _block_diag(bbi.reshape(npk, PACK, SSM_P, SSM_C)).astype(BF16)
    cmr = _block_diag(ssm_c_re.reshape(npk, PACK, SSM_C, SSM_P)).astype(BF16)
    cmi = _block_diag(ssm_c_im.reshape(npk, PACK, SSM_C, SSM_P)).astype(BF16)
    TS = _tile(S, 512)
    hr, hi, ypre = _ssm_fwd(z, a_r, a_i, bdr, bdi, cmr, cmi, ssm_d, TS)
    (yg,), _ = _rowwise("ssm_gelu", lambda a: ([jax.nn.gelu(a)], []), S, TR, [_full(ypre)], [], [(SSM_W, SSM_W, _c0, BF16)])
    gl = _mm_nn("ssm_glu", yg, Wglu, tn=SSM_W, tk=SSM_W)
    (ycat,), _ = _rowwise("mix_out", lambda *a: ([_mix_out(*a)], []), S, TR, [_full(ya), _full(ypre), _full(gl)],
                          [attn_out_norm, ssm_out_norm, ssm_b_glu], [(MIX_W, MIX_W, _c0, BF16)])
    h2 = _mm_nn("mix_proj", ycat, Wout, tn=D // 2, tk=D, res=h1, scale=1.0)

    h3, (xn2, gu2, hid2) = ffn_fwd("ffn2", h2, ffn2_norm, Wgu2, Wd2)

    (hn, pb), _ = _rowwise("ple_norm", lambda a, q, g: ([_rms(a, g), q], []), S, TR, [_full(h3), _full(ps)], [ple_norm],
                           [(D, D, _c0, BF16), (ps.shape[1], ps.shape[1], _c0, BF16)])
    pgl = _mm_nn("ple_gate", hn, Wpg, tn=D // 2, tk=D)
    pe = _mm_nn("ple_proj", pb, Wpp, tn=Wpp.shape[2], tk=Wpp.shape[1])

    def tail(h3b, glb, peb, tb, gf):
        rows, vjp = jax.vjp(lambda a, b, c, g: _tail_loss(a, b, c, g, tb), h3b, glb, peb, gf)
        dh, dgl, dpe, dgf = vjp(jnp.ones_like(rows))
        return [dh, dgl, dpe], [jnp.broadcast_to(jnp.sum(rows, axis=0, keepdims=True), (1, LANE)), dgf]

    (dh3_dir, dpgl, dpe), (loss_row, g_final) = _rowwise(
        "tail", tail, S, TR, [_full(h3), _full(pgl), _full(pe), _full(tgt)], [final_norm.reshape(1, D)],
        [(D, D, _c0, F32), (D, D, _c0, BF16), (D, D, _c0, BF16)], [(LANE, LANE, _c0), (D, D, _c0)])
    loss = lax.psum(loss_row[0, 0], AXES)

    def norm_bwd(tag, h, gain, dn, dres):
        def f(hb, dnb, drb, g):
            _, vjp = jax.vjp(_rms, hb, g)
            dh, dg = vjp(dnb)
            dh = dh + drb
            return [dh, dh], [dg]
        (dh, dhb), (dg,) = _rowwise(f"{tag}_norm_bwd", f, S, TR, [_full(h), _full(dn), _full(dres)], [gain],
                                    [(D, D, _c0, F32), (D, D, _c0, BF16)], [(D, D, _c0)])
        return dh, dhb, dg

    dhn = _mm_nt("ple_gate_dx", dpgl, Wpg, tn=D, tk=D)
    g_wpg = _mm_tn("ple_gate_dw", hn, dpgl, 1)
    g_wpp = _mm_tn("ple_proj_dw", pb, dpe, NDEV)
    dh3, dh3b, g_ple_norm = norm_bwd("ple", h3, ple_norm, dhn, dh3_dir)

    def ffn_bwd(tag, h, gain, Wgu, Wd, saved, dout, doutb):
        xn, gu, hid = saved
        dgu = _ffn_down_dx(f"{tag}_down_dx", doutb, Wd, gu, NDEV, scale=0.5)
        g_wd = _mm_tn(f"{tag}_down_dw", hid, doutb, 1, tm=2048, tko=FSP, tn=D // 2, scale=0.5)
        dxn = _mm_nt(f"{tag}_up_dx", dgu, Wgu, tm=1024, tn=D, tk=2 * FSP)
        g_wgu = _mm_tn(f"{tag}_up_dw", xn, dgu, NDEV, tm=2048, tn=FSP)
        dh, dhb, g_norm = norm_bwd(tag, h, gain, dxn, dout)
        return dh, dhb, g_norm, g_wgu, g_wd

    dh2, dh2b, g_ffn2_norm, g_wgu2, g_wd2 = ffn_bwd("ffn2", h2, ffn2_norm, Wgu2, Wd2, (xn2, gu2, hid2), dh3, dh3b)

    dycat = _mm_nt("mix_proj_dx", dh2b, Wout, tn=D, tk=D)
    g_wout = _mm_tn("mix_proj_dw", ycat, dh2b, 1)

    def mix_out_bwd(yab, ypb, glb, dyc, ga, gb, bglu):
        _, vjp = jax.vjp(_mix_out, yab, ypb, glb, ga, gb, bglu)
        dya_, dyp_, dgl_, dga, dgb, dbg = vjp(dyc)
        return [dya_, dyp_, dgl_], [dga, dgb, dbg]
    (dya, dyp_dir, dglb), (g_attn_norm, g_ssm_norm, g_bglu) = _rowwise(
        "mix_out_bwd", mix_out_bwd, S, TR, [_full(ya), _full(ypre), _full(gl), _full(dycat)],
        [attn_out_norm, ssm_out_norm, ssm_b_glu],
        [(ATTN_W, ATTN_W, _c0, F32), (SSM_W, SSM_W, _c0, F32), (SSM_W, SSM_W, _c0, BF16)],
        [(ATTN_W, ATTN_W, _c0), (SSM_W, SSM_W, _c0), (SSM_W, SSM_W, _c0)])
    dyg = _mm_nt("ssm_glu_dx", dglb, Wglu, tn=SSM_W, tk=SSM_W)
    g_wglu = _mm_tn("ssm_glu_dw", yg, dglb, 1)

    def gelu_bwd(ypb, dygb, ddir):
        _, vjp = jax.vjp(jax.nn.gelu, ypb)
        return [ddir + vjp(dygb)[0]], []
    (dyp,), _ = _rowwise("ssm_gelu_bwd", gelu_bwd, S, TR, [_full(ypre), _full(dyg), _full(dyp_dir)], [],
                         [(SSM_W, SSM_W, _c0, F32)])
    du, dbdr, dbdi, dcmr, dcmi, da_r, da_i, g_ssm_d = _ssm_bwd(z, dyp, hr, hi, a_r, a_i, bdr, bdi, cmr, cmi, ssm_d, TS)
    dbbr = _block_diag_take(dbdr, SSM_P, SSM_C).reshape(-1, SSM_C)
    dbbi = _block_diag_take(dbdi, SSM_P, SSM_C).reshape(-1, SSM_C)
    g_c_re = _block_diag_take(dcmr, SSM_C, SSM_P).reshape(ssm_c_re.shape)
    g_c_im = _block_diag_take(dcmi, SSM_C, SSM_P).reshape(ssm_c_im.shape)
    dlr, dli, ddt, g_b_re, g_b_im = _ssm_prep_bwd(lr_c, li_c, dt_c, b_re2, b_im2, col(da_r), col(da_i), dbbr, dbbi)
    g_lam_re, g_lam_im = dlr.reshape(ssm_lambda_re.shape), dli.reshape(ssm_lambda_im.shape)
    g_log_dt = jnp.sum(ddt.reshape(SSM_G, SSM_P), axis=1).reshape(ssm_log_dt.shape)
    g_b_re, g_b_im = g_b_re.reshape(ssm_b_re.shape), g_b_im.reshape(ssm_b_im.shape)

    grads_qkv = [_attn_bwd(z, dya, ya, mg, den, d) for d in DILATIONS]

    def dz_cat(q1, k1, v1, q2, k2, v2, q3, k3, v3, dub):
        return [jnp.concatenate([q1 + q2 + q3, k1 + k2 + k3, v1 + v2 + v3, dub], axis=-1)], []
    (dz,), _ = _rowwise("mix_dz", dz_cat, S, TR, [_full(t) for g3 in grads_qkv for t in g3] + [_full(du)], [],
                        [(ZW, ZW, _c0, BF16)])
    dun = _mm_nt("mix_in_dx", dz, Win, tn=D, tk=512)
    g_win = _mm_tn("mix_in_dw", un, dz, NDEV, tn=512)
    dh1, dh1b, g_mix_norm = norm_bwd("mix", h1, mix_norm, dun, dh2)

    dx, _dxb, g_ffn1_norm, g_wgu1, g_wd1 = ffn_bwd("ffn1", xs, ffn1_norm, Wgu1, Wd1, (xn1, gu1, hid1), dh1, dh1b)

    restack = lambda g: g.reshape((NDEV, g.shape[1] // NDEV) + g.shape[2:])
    big = [g_wgu1, restack(g_wd1), g_win, restack(g_wglu), restack(g_wout), g_wgu2, restack(g_wd2), restack(g_wpg), g_wpp]
    mine = _reduce_scatter("rs", big)
    out = {}

    def upd(name, idx, *, tr, cw, gw, goff=0):
        w, m, v = A[name][0], A["m_" + name][0], A["v_" + name][0]
        g, dlt, mn, vn = _adamw("adamw_" + name, w, m, v, mine[idx], tr=tr, cw=cw, gw=gw, goff=goff)
        for k, val in (("grad_", g), ("delta_", dlt), ("new_m_", mn), ("new_v_", vn)):
            out[k + name] = val[None]

    DT = _tile(D, 256)
    FT = _tile(FSH, 512)
    DC = _tile(D, 1024, LANE)
    upd("ffn1_w_gate", 0, tr=DT, cw=FSH, gw=FSP, goff=0)
    upd("ffn1_w_up", 0, tr=DT, cw=FSH, gw=FSP, goff=1)
    upd("ffn1_w_down", 1, tr=FT, cw=DC, gw=DC)
    upd("w_in", 2, tr=DT, cw=w_in.shape[-1], gw=w_in.shape[-1])
    upd("ssm_w_glu", 3, tr=ssm_w_glu.shape[1], cw=SSM_W, gw=SSM_W)
    upd("w_out", 4, tr=w_out.shape[1], cw=DC, gw=DC)
    upd("ffn2_w_gate", 5, tr=DT, cw=FSH, gw=FSP, goff=0)
    upd("ffn2_w_up", 5, tr=DT, cw=FSH, gw=FSP, goff=1)
    upd("ffn2_w_down", 6, tr=FT, cw=DC, gw=DC)
    upd("ple_w_gate", 7, tr=ple_w_gate.shape[1], cw=DC, gw=DC)
    upd("ple_w_proj", 8, tr=ple_w_proj.shape[1], cw=ple_w_proj.shape[2], gw=ple_w_proj.shape[2])

    small = [("ffn1_norm", g_ffn1_norm), ("mix_norm", g_mix_norm), ("attn_out_norm", g_attn_norm),
             ("ssm_lambda_re", g_lam_re), ("ssm_lambda_im", g_lam_im), ("ssm_log_dt", g_log_dt),
             ("ssm_b_re", g_b_re), ("ssm_b_im", g_b_im), ("ssm_c_re", g_c_re), ("ssm_c_im", g_c_im),
             ("ssm_d", g_ssm_d), ("ssm_b_glu", g_bglu), ("ssm_out_norm", g_ssm_norm), ("ffn2_norm", g_ffn2_norm),
             ("ple_norm", g_ple_norm), ("final_norm", g_final)]
    chunk = 8 * LANE

    def pack(arrs):
        parts = []
        for a in arrs:
            flat = a.reshape(-1)
            padn = -(-flat.shape[0] // chunk) * chunk
            parts.append(jnp.pad(flat, (0, padn - flat.shape[0])).reshape(-1, LANE))
        return jnp.concatenate(parts, axis=0)

    g_pack = pack([g for _, g in small])
    (g_all,) = _all_gather("ag_small", [g_pack])
    g_sum = _sum8("small_sum", g_all)
    w_pack = pack([A[n] for n, _ in small])
    m_pack = pack([A["m_" + n] for n, _ in small])
    v_pack = pack([A["v_" + n] for n, _ in small])
    d_pack, mn_pack, vn_pack = _adamw_small("adamw_small", w_pack, m_pack, v_pack, g_sum)
    off = 0
    for n, _ in small:
        shape = A[n].shape
        size = math.prod(shape)
        rows = -(-size // chunk) * 8
        for k, buf in (("grad_", g_sum), ("delta_", d_pack), ("new_m_", mn_pack), ("new_v_", vn_pack)):
            out[k + n] = buf[off:off + rows].reshape(-1)[:size].reshape(shape)
        off += rows

    names = ['ffn1_norm', 'ffn1_w_gate', 'ffn1_w_up', 'ffn1_w_down', 'mix_norm', 'w_in', 'attn_out_norm',
             'ssm_lambda_re', 'ssm_lambda_im', 'ssm_log_dt', 'ssm_b_re', 'ssm_b_im', 'ssm_c_re', 'ssm_c_im', 'ssm_d',
             'ssm_w_glu', 'ssm_b_glu', 'ssm_out_norm', 'w_out', 'ffn2_norm', 'ffn2_w_gate', 'ffn2_w_up', 'ffn2_w_down',
             'ple_norm', 'ple_w_gate', 'ple_w_proj', 'final_norm']
    return (loss, dx[None], *[out[k + n] for k in ("grad_", "delta_", "new_m_", "new_v_") for n in names])
```

```python
import functools
import math

import jax
import jax.numpy as jnp
from jax import lax
from jax.experimental import pallas as pl
from jax.experimental.pallas import tpu as pltpu

F32, BF16 = jnp.float32, jnp.bfloat16
MESH = pl.DeviceIdType.MESH
NDEV = 8
AXES = ("x", "y", "c")
LANE = 128
VMEM_LIMIT = 56 * 1024 * 1024

ATTN_W = 1024
HEAD_DIM = 64
SSM_W = 1024
MIX_W = ATTN_W + SSM_W
SSM_G, SSM_P, SSM_C = 64, 64, 16
PACK = 8
DILATIONS = (1, 4, 16)
QB = 128
NORM_EPS = 1e-6
MASK_VALUE = -1e30
LR, B1, B2, EPS, WD, STEP = 0.001, 0.9, 0.999, 1e-08, 0.01, 10


def _cp(sem=None):
    return pltpu.CompilerParams(dimension_semantics=sem, vmem_limit_bytes=VMEM_LIMIT)


def _tile(n, target, mult=8):
    if n <= target:
        return n
    for t in range(target - target % mult, 0, -mult):
        if n % t == 0:
            return t
    return n


def _rms(x, g):
    return x * lax.rsqrt(jnp.mean(x * x, axis=-1, keepdims=True) + NORM_EPS) * g


def _rowwise(name, fn, S, tr, rows, fulls, outs, accs=(), ncol=1):
    nr, nf, no, na = len(rows), len(fulls), len(outs), len(accs)

    def body(*refs):
        ins = [r[...] for r in refs[:nr + nf]]
        o_refs = refs[nr + nf:nr + nf + no]
        a_refs = refs[nr + nf + no:]
        o_vals, a_vals = fn(*ins)
        for r, v in zip(o_refs, o_vals):
            r[...] = v.astype(r.dtype)
        if na:
            @pl.when(pl.program_id(1) == 0)
            def _():
                for r in a_refs:
                    r[...] = jnp.zeros_like(r)
            for r, v in zip(a_refs, a_vals):
                r[...] += v

    in_specs = [pl.BlockSpec((tr, w), functools.partial(lambda j, i, cm: (i, cm(j)), cm=cm)) for _, w, cm in rows]
    in_specs += [pl.BlockSpec(f.shape, functools.partial(lambda j, i, nd: (0,) * nd, nd=f.ndim)) for f in fulls]
    out_specs = [pl.BlockSpec((tr, w), functools.partial(lambda j, i, cm: (i, cm(j)), cm=cm)) for _, w, cm, _ in outs]
    out_specs += [pl.BlockSpec((1, w), functools.partial(lambda j, i, cm: (0, cm(j)), cm=cm)) for _, w, cm in accs]
    out_shape = [jax.ShapeDtypeStruct((S, c), dt) for c, _, _, dt in outs]
    out_shape += [jax.ShapeDtypeStruct((1, c), F32) for c, _, _ in accs]
    res = pl.pallas_call(
        body, name=name, grid=(ncol, S // tr), in_specs=in_specs, out_specs=out_specs, out_shape=out_shape,
        compiler_params=_cp(("parallel", "arbitrary" if na else "parallel")),
    )(*[a for a, _, _ in rows], *fulls)
    return res[:no], res[no:]


def _c0(j):
    return 0


def _full(a):
    return (a, a.shape[1], _c0)


def _mm_nn(name, a, w, *, out_dtype=F32, tm=512, tn=768, tk=2048, res=None, scale=1.0):
    M, K = a.shape
    J, K2, Np = w.shape
    assert K == K2
    tm, tn, tk = _tile(M, tm), _tile(Np, tn, LANE), _tile(K, tk, LANE)
    npj = Np // tn
    nk = K // tk

    def body(*refs):
        if res is None:
            a_ref, w_ref, o_ref, acc = refs
        else:
            a_ref, w_ref, r_ref, o_ref, acc = refs
        k = pl.program_id(2)
        part = jnp.dot(a_ref[...].astype(BF16), w_ref[...], preferred_element_type=F32)

        def finish(v):
            if res is not None:
                v = r_ref[...] + scale * v
            o_ref[...] = v.astype(o_ref.dtype)

        if nk == 1:
            finish(part)
            return

        @pl.when(k == 0)
        def _():
            acc[...] = part

        @pl.when(k > 0)
        def _():
            acc[...] += part

        @pl.when(k == nk - 1)
        def _():
            finish(acc[...])

    in_specs = [pl.BlockSpec((tm, tk), lambda i, n, k: (i, k)),
                pl.BlockSpec((None, tk, tn), lambda i, n, k: (n // npj, k, n % npj))]
    args = [a, w]
    if res is not None:
        in_specs.append(pl.BlockSpec((tm, tn), lambda i, n, k: (i, n)))
        args.append(res)
    return pl.pallas_call(
        body, name=name, grid=(M // tm, J * npj, nk), in_specs=in_specs,
        out_specs=pl.BlockSpec((tm, tn), lambda i, n, k: (i, n)),
        out_shape=jax.ShapeDtypeStruct((M, J * Np), out_dtype),
        scratch_shapes=[pltpu.VMEM((tm, tn), F32)],
        compiler_params=_cp(("parallel", "parallel", "arbitrary")),
    )(*args)


def _mm_nt(name, dy, w, *, out_dtype=F32, tm=512, tn=2048, tk=768, scale=1.0):
    M, N = dy.shape
    J, K, Np = w.shape
    assert N == J * Np
    tm, tn, tk = _tile(M, tm), _tile(K, tn, LANE), _tile(Np, tk, LANE)
    npj = Np // tk
    nc = J * npj

    def body(a_ref, w_ref, o_ref, acc):
        c = pl.program_id(2)
        part = lax.dot_general(a_ref[...].astype(BF16), w_ref[...], (((1,), (1,)), ((), ())),
                               preferred_element_type=F32)
        if nc == 1:
            o_ref[...] = (scale * part).astype(o_ref.dtype)
            return

        @pl.when(c == 0)
        def _():
            acc[...] = part

        @pl.when(c > 0)
        def _():
            acc[...] += part

        @pl.when(c == nc - 1)
        def _():
            o_ref[...] = (scale * acc[...]).astype(o_ref.dtype)

    return pl.pallas_call(
        body, name=name, grid=(M // tm, K // tn, nc),
        in_specs=[pl.BlockSpec((tm, tk), lambda i, n, c: (i, c)),
                  pl.BlockSpec((None, tn, tk), lambda i, n, c: (c // npj, n, c % npj))],
        out_specs=pl.BlockSpec((tm, tn), lambda i, n, c: (i, n)),
        out_shape=jax.ShapeDtypeStruct((M, K), out_dtype),
        scratch_shapes=[pltpu.VMEM((tm, tn), F32)],
        compiler_params=_cp(("parallel", "parallel", "arbitrary")),
    )(dy, w)


def _mm_tn(name, x, dy, J, *, tm=1024, tko=1024, tn=768, scale=1.0):
    M, K = x.shape
    M2, N = dy.shape
    assert M == M2 and N % J == 0
    Np = N // J
    tm, tko, tn = _tile(M, tm, LANE), _tile(K, tko, LANE), _tile(Np, tn, LANE)
    npj = Np // tn
    nm = M // tm

    def body(x_ref, d_ref, o_ref, acc):
        m = pl.program_id(2)
        part = lax.dot_general(x_ref[...].astype(BF16), d_ref[...].astype(BF16), (((0,), (0,)), ((), ())),
                               preferred_element_type=F32)
        if nm == 1:
            o_ref[...] = scale * part
            return

        @pl.when(m == 0)
        def _():
            acc[...] = part

        @pl.when(m > 0)
        def _():
            acc[...] += part

        @pl.when(m == nm - 1)
        def _():
            o_ref[...] = scale * acc[...]

    return pl.pallas_call(
        body, name=name, grid=(K // tko, J * npj, nm),
        in_specs=[pl.BlockSpec((tm, tko), lambda k, n, m: (m, k)),
                  pl.BlockSpec((tm, tn), lambda k, n, m: (m, n))],
        out_specs=pl.BlockSpec((None, tko, tn), lambda k, n, m: (n // npj, k, n % npj)),
        out_shape=jax.ShapeDtypeStruct((J, K, Np), F32),
        scratch_shapes=[pltpu.VMEM((tko, tn), F32)],
        compiler_params=_cp(("parallel", "parallel", "arbitrary")),
    )(x, dy)


def _swiglu_act(g, u):
    return jax.nn.silu(g) * u


def _ffn_up(name, xn, wgu, *, tm=1024):
    M, K = xn.shape
    J, _, F2 = wgu.shape
    F = F2 // 2
    tm = _tile(M, tm)

    def body(a_ref, w_ref, gu_ref, h_ref):
        r = jnp.dot(a_ref[...], w_ref[...], preferred_element_type=F32)
        gu_ref[...] = r.astype(gu_ref.dtype)
        h_ref[...] = _swiglu_act(r[:, :F], r[:, F:]).astype(h_ref.dtype)

    return pl.pallas_call(
        body, name=name, grid=(M // tm, J),
        in_specs=[pl.BlockSpec((tm, K), lambda i, j: (i, 0)), pl.BlockSpec((None, K, F2), lambda i, j: (j, 0, 0))],
        out_specs=[pl.BlockSpec((tm, F2), lambda i, j: (i, j)), pl.BlockSpec((tm, F), lambda i, j: (i, j))],
        out_shape=[jax.ShapeDtypeStruct((M, J * F2), BF16), jax.ShapeDtypeStruct((M, J * F), BF16)],
        compiler_params=_cp(("parallel", "parallel")),
    )(xn, wgu)


def _ffn_down_dx(name, dout, wd, gu, J, *, scale, tm=512):
    M, D = dout.shape
    F = wd.shape[1] // J
    tm = _tile(M, tm)

    def body(d_ref, w_ref, gu_ref, o_ref):
        dh = scale * lax.dot_general(d_ref[...], w_ref[...], (((1,), (1,)), ((), ())), preferred_element_type=F32)
        gu = gu_ref[...].astype(F32)
        _, vjp = jax.vjp(_swiglu_act, gu[:, :F], gu[:, F:])
        o_ref[...] = jnp.concatenate(vjp(dh), axis=-1).astype(o_ref.dtype)

    return pl.pallas_call(
        body, name=name, grid=(M // tm, J),
        in_specs=[pl.BlockSpec((tm, D), lambda i, j: (i, 0)), pl.BlockSpec((None, F, D), lambda i, j: (0, j, 0)),
                  pl.BlockSpec((tm, 2 * F), lambda i, j: (i, j))],
        out_specs=pl.BlockSpec((tm, 2 * F), lambda i, j: (i, j)),
        out_shape=jax.ShapeDtypeStruct((M, J * 2 * F), BF16),
        compiler_params=_cp(("parallel", "parallel")),
    )(dout, wd, gu)


def _all_gather(name, shards):
    n = len(shards)

    def body(*refs):
        ins, outs = refs[:n], refs[n:2 * n]
        send_sems, recv_sems, local_sems = refs[2 * n:]
        x, y, c = lax.axis_index("x"), lax.axis_index("y"), lax.axis_index("c")
        me, sibling = (x, y, c), (x, y, 1 - c)
        chips = [(1 - x, y), (x, 1 - y), (1 - x, 1 - y)]

        def blk(i, px, py, pc):
            return outs[i].at[4 * px + 2 * py + pc]

        def copy(i, k, block, to, src=None):
            return pltpu.make_async_remote_copy(
                src_ref=blk(i, *block) if src is None else src, dst_ref=blk(i, *block),
                send_sem=send_sems.at[i, k], recv_sem=recv_sems.at[i, k], device_id=to, device_id_type=MESH)

        mine = [pltpu.make_async_copy(ins[i], blk(i, *me), local_sems.at[i]) for i in range(n)]
        for cp in mine:
            cp.start()
        first = []
        for i in range(n):
            first.append(copy(i, 0, me, sibling, src=ins[i]))
            first += [copy(i, 1 + j, me, (*chip, c), src=ins[i]) for j, chip in enumerate(chips)]
        for cp in first:
            cp.start()
        passed = []
        for i in range(n):
            for j, chip in enumerate(chips):
                copy(i, 1 + j, (*chip, c), me).wait_recv()
                fwd = copy(i, 4 + j, (*chip, c), sibling)
                fwd.start()
                passed.append(fwd)
        for i in range(n):
            copy(i, 0, sibling, me).wait_recv()
            for j, chip in enumerate(chips):
                copy(i, 4 + j, (*chip, 1 - c), me).wait_recv()
        for cp in first + passed:
            cp.wait_send()
        for cp in mine:
            cp.wait()

    any_spec = pl.BlockSpec(memory_space=pl.ANY)
    return pl.pallas_call(
        body, name=name, in_specs=[any_spec] * n, out_specs=[any_spec] * n,
        out_shape=[jax.ShapeDtypeStruct((NDEV,) + s.shape, s.dtype) for s in shards],
        scratch_shapes=[pltpu.SemaphoreType.DMA((n, 7)), pltpu.SemaphoreType.DMA((n, 7)), pltpu.SemaphoreType.DMA((n,))],
    )(*shards)


def _swap_add(name, arr, streams, grid, tile, out_shape, out_spec):
    ns = len(streams)
    tr, C = tile
    n = grid[0] * grid[1]
    pos = {a: lax.axis_index(a) for a in AXES}
    sel = jnp.stack([v for a in AXES for v in (pos[a], 1 - pos[a])]).astype(jnp.int32)
    lead = (None,) * (arr.ndim - 2)

    def body(sel_ref, *refs):
        keeps, sends, o_ref = refs[0:2 * ns:2], refs[1:2 * ns:2], refs[2 * ns]
        scratch = refs[2 * ns + 1:]
        lands, stages = scratch[:ns], scratch[ns:2 * ns]
        send_sems, recv_sems, credits = scratch[2 * ns:]
        t = pl.program_id(0) * grid[1] + pl.program_id(1)
        slot = t % 2
        peers = []
        for axis, _, _, _ in streams:
            here = {a: lax.axis_index(a) for a in AXES}
            peers.append(tuple(1 - here[a] if a == axis else here[a] for a in AXES))

        @pl.when(t >= 2)
        def _():
            for s in range(ns):
                pl.semaphore_wait(credits.at[s], 1)

        rdmas = []
        for s in range(ns):
            stages[s][slot] = sends[s][...].astype(stages[s].dtype)
            rdma = pltpu.make_async_remote_copy(
                src_ref=stages[s].at[slot], dst_ref=lands[s].at[slot], send_sem=send_sems.at[s, slot],
                recv_sem=recv_sems.at[s, slot], device_id=peers[s], device_id_type=MESH)
            rdma.start()
            rdmas.append(rdma)
        for s in range(ns):
            rdmas[s].wait_recv()
            total = keeps[s][...] + lands[s][slot].astype(F32)
            if ns == 1:
                o_ref[...] = total
            else:
                o_ref[s] = total
        for rdma in rdmas:
            rdma.wait_send()

        @pl.when(t + 2 < n)
        def _():
            for s in range(ns):
                pl.semaphore_signal(credits.at[s], inc=1, device_id=peers[s], device_id_type=MESH)

    in_specs = []
    for _, _, keep_map, send_map in streams:
        in_specs += [pl.BlockSpec(lead + (tr, C), keep_map), pl.BlockSpec(lead + (tr, C), send_map)]
    wires = [w for _, w, _, _ in streams]
    grid_spec = pltpu.PrefetchScalarGridSpec(
        num_scalar_prefetch=1, grid=grid, in_specs=in_specs, out_specs=out_spec,
        scratch_shapes=[pltpu.VMEM((2, tr, C), w) for w in wires] + [pltpu.VMEM((2, tr, C), w) for w in wires]
        + [pltpu.SemaphoreType.DMA((ns, 2)), pltpu.SemaphoreType.DMA((ns, 2)), pltpu.SemaphoreType.REGULAR((ns,))])
    return pl.pallas_call(
        body, name=name, grid_spec=grid_spec, out_shape=jax.ShapeDtypeStruct(out_shape, F32),
        compiler_params=_cp(("arbitrary", "arbitrary")),
    )(sel, *([arr] * (2 * ns)))


def _reduce_scatter(name, grads):
    outs = []
    for i, g in enumerate(grads):
        _, R, C = g.shape
        R2 = R // 2
        tr = _tile(R2, 256, 16)
        nrh = R2 // tr
        g1 = _swap_add(
            f"{name}{i}_c", g.reshape(4, 2, R, C),
            [("c", F32, lambda b, i, s: (b, s[4], i, 0), lambda b, i, s: (b, s[5], i, 0))],
            (4, 2 * nrh), (tr, C), (2, 4, R2, C),
            pl.BlockSpec((None, None, tr, C), lambda b, i, s: (i // nrh, b, i % nrh, 0)))
        g2 = _swap_add(
            f"{name}{i}_yx", g1.reshape(2, 2, 2, R2, C),
            [("y", BF16, lambda b, i, s: (0, b, s[2], i, 0), lambda b, i, s: (0, b, s[3], i, 0)),
             ("x", BF16, lambda b, i, s: (1, s[0], b, i, 0), lambda b, i, s: (1, s[1], b, i, 0))],
            (2, nrh), (tr, C), (2, 2, R2, C), pl.BlockSpec((2, None, tr, C), lambda b, i, s: (0, b, i, 0)))
        g3 = _swap_add(
            f"{name}{i}_xy", g2,
            [("x", BF16, lambda b, i, s: (0, s[0], i, 0), lambda b, i, s: (0, s[1], i, 0)),
             ("y", BF16, lambda b, i, s: (1, s[2], i, 0), lambda b, i, s: (1, s[3], i, 0))],
            (1, nrh), (tr, C), (2, R2, C), pl.BlockSpec((2, tr, C), lambda b, i, s: (0, i, 0)))
        outs.append(g3.reshape(R, C))
    return outs


def _sum8(name, g):
    _, R, C = g.shape
    tr = _tile(R, 512)

    def body(g_ref, o_ref):
        acc = g_ref[0]
        for d in range(1, NDEV):
            acc = acc + g_ref[d]
        o_ref[...] = acc

    return pl.pallas_call(
        body, name=name, grid=(R // tr,), in_specs=[pl.BlockSpec((NDEV, tr, C), lambda i: (0, i, 0))],
        out_specs=pl.BlockSpec((tr, C), lambda i: (i, 0)), out_shape=jax.ShapeDtypeStruct((R, C), F32),
        compiler_params=_cp(("parallel",)),
    )(g)


def _adamw_math(w, g, m, v):
    m = B1 * m + (1.0 - B1) * g
    v = B2 * v + (1.0 - B2) * jnp.square(g)
    m_hat = m / (1.0 - B1 ** STEP)
    v_hat = v / (1.0 - B2 ** STEP)
    delta = -LR * (m_hat / (jnp.sqrt(v_hat) + EPS) + WD * w)
    return delta, m, v


def _adamw(name, w, m, v, gp, *, tr, cw, gw, goff=0):
    R, C = w.shape
    nc = C // cw
    nr = R // tr

    def body(w_ref, m_ref, v_ref, g_ref, g_out, d_out, m_out, v_out):
        g = g_ref[...][:, :cw]
        d, mn, vn = _adamw_math(w_ref[...], g, m_ref[...], v_ref[...])
        g_out[...] = g
        d_out[...] = d
        m_out[...] = mn
        v_out[...] = vn

    wspec = pl.BlockSpec((tr, cw), lambda i, j: (i, j))
    gspec = pl.BlockSpec((tr, gw), lambda i, j: (i, goff + j))
    return pl.pallas_call(
        body, name=name, grid=(nr, nc), in_specs=[wspec, wspec, wspec, gspec], out_specs=[wspec] * 4,
        out_shape=[jax.ShapeDtypeStruct((R, C), F32)] * 4, compiler_params=_cp(("parallel", "parallel")),
    )(w, m, v, gp)


def _adamw_small(name, w, m, v, g):
    R, C = w.shape

    def body(w_ref, m_ref, v_ref, g_ref, d_out, m_out, v_out):
        d, mn, vn = _adamw_math(w_ref[...], g_ref[...], m_ref[...], v_ref[...])
        d_out[...] = d
        m_out[...] = mn
        v_out[...] = vn

    tr = _tile(R, 512)
    spec = pl.BlockSpec((tr, C), lambda i: (i, 0))
    return pl.pallas_call(
        body, name=name, grid=(R // tr,), in_specs=[spec] * 4, out_specs=[spec] * 3,
        out_shape=[jax.ShapeDtypeStruct((R, C), F32)] * 3, compiler_params=_cp(("parallel",)),
    )(w, m, v, g)


def _prep(name, parts, rows_p, cols_p):
    R, C = parts[0].shape
    n = len(parts)

    def body(*refs):
        o_ref = refs[n]
        if (R, C) != (rows_p, cols_p):
            o_ref[...] = jnp.zeros_like(o_ref)
        for i in range(n):
            o_ref[0:R, i * cols_p:i * cols_p + C] = refs[i][...].astype(BF16)

    return pl.pallas_call(
        body, name=name, out_shape=jax.ShapeDtypeStruct((rows_p, n * cols_p), BF16), compiler_params=_cp(),
    )(*parts)


def _attn_masks():
    lane = lax.broadcasted_iota(jnp.int32, (1, LANE), 1)
    return [(lane < HEAD_DIM), (lane >= HEAD_DIM)]


def _band_valid(base):
    qi = lax.broadcasted_iota(jnp.int32, (QB, 2 * QB), 0)
    ki = lax.broadcasted_iota(jnp.int32, (QB, 2 * QB), 1)
    dist = qi + QB - ki
    return (dist >= 0) & (dist <= QB) & (base + ki - QB >= 0)


ATTN_T = max(DILATIONS) * QB


def _attn_groups(T):
    out = []
    for d in DILATIONS:
        for r in range(d):
            for i in range(T // (d * QB)):
                qrows = pl.ds(r + d * i * QB, QB, stride=d) if d > 1 else pl.ds(i * QB, QB)
                k0 = T + r + d * (i - 1) * QB
                krows = pl.ds(k0, 2 * QB, stride=d) if d > 1 else pl.ds(k0, 2 * QB)
                out.append((d, qrows, krows, i * QB))
    return out


def _attn_specs(T, width_off):
    cur = pl.BlockSpec((T, LANE), lambda hp, b: (b, width_off + hp))
    prev = pl.BlockSpec((T, LANE), lambda hp, b: (jnp.maximum(b - 1, 0), width_off + hp))
    return cur, prev


def _attn_fwd(z):
    S, ZW = z.shape
    T = min(ATTN_T, S)
    scale = HEAD_DIM ** -0.5
    groups = _attn_groups(T)

    def body(q_ref, kc_ref, kp_ref, vc_ref, vp_ref, y_ref, m_ref, l_ref, kcat, vcat):
        b = pl.program_id(1)
        kcat[0:T, :] = kp_ref[...]
        kcat[T:, :] = kc_ref[...]
        vcat[0:T, :] = vp_ref[...]
        vcat[T:, :] = vc_ref[...]
        masks = _attn_masks()
        for d, qrows, krows, l0 in groups:
            q = q_ref[qrows, :]
            kk = kcat[krows, :].astype(BF16)
            vv = vcat[krows, :].astype(BF16)
            valid = _band_valid(b * (T // d) + l0)
            o_new = m_new = l_new = None
            for hm in masks:
                qh = jnp.where(hm, q, 0.0).astype(BF16)
                s = lax.dot_general(qh, kk, (((1,), (1,)), ((), ())), preferred_element_type=F32) * scale
                s = jnp.where(valid, s, MASK_VALUE)
                m = jnp.max(s, axis=-1, keepdims=True)
                p = jnp.exp(s - m)
                l = jnp.sum(p, axis=-1, keepdims=True)
                o = jnp.dot(p.astype(BF16), vv, preferred_element_type=F32)
                if o_new is None:
                    o_new, m_new, l_new = o, jnp.broadcast_to(m, (QB, LANE)), jnp.broadcast_to(l, (QB, LANE))
                else:
                    o_new = jnp.where(hm, o, o_new)
                    m_new = jnp.where(hm, m, m_new)
                    l_new = jnp.where(hm, l, l_new)
            if d == DILATIONS[0]:
                y_ref[qrows, :] = o_new
                m_ref[qrows, :] = m_new
                l_ref[qrows, :] = l_new
            else:
                m_old = m_ref[qrows, :]
                m_all = jnp.maximum(m_old, m_new)
                w_old, w_new = jnp.exp(m_old - m_all), jnp.exp(m_new - m_all)
                y_ref[qrows, :] = w_old * y_ref[qrows, :] + w_new * o_new
                l_ref[qrows, :] = w_old * l_ref[qrows, :] + w_new * l_new
                m_ref[qrows, :] = m_all
        y_ref[...] = y_ref[...] / l_ref[...]

    qc, _ = _attn_specs(T, 0)
    kc, kp = _attn_specs(T, ATTN_W // LANE)
    vc, vp = _attn_specs(T, 2 * ATTN_W // LANE)
    shp = jax.ShapeDtypeStruct((S, ATTN_W), F32)
    return pl.pallas_call(
        body, name="attn_fwd", grid=(ATTN_W // LANE, S // T),
        in_specs=[qc, kc, kp, vc, vp], out_specs=[qc, qc, qc], out_shape=[shp, shp, shp],
        scratch_shapes=[pltpu.VMEM((2 * T, LANE), F32), pltpu.VMEM((2 * T, LANE), F32)],
        compiler_params=_cp(("parallel", "parallel")),
    )(z, z, z, z, z)


def _attn_bwd(z, dya, ya, mg, den):
    S, ZW = z.shape
    T = min(ATTN_T, S)
    scale = HEAD_DIM ** -0.5
    groups = _attn_groups(T)

    def body(q_ref, kc_ref, kp_ref, vc_ref, vp_ref, dy_ref, y_ref, m_ref, n_ref, dq_ref, dk_ref, dv_ref,
             kcat, vcat, dkcat, dvcat):
        b = pl.program_id(1)

        @pl.when(b == 0)
        def _():
            dk_ref[...] = jnp.zeros_like(dk_ref)
            dv_ref[...] = jnp.zeros_like(dv_ref)

        kcat[0:T, :] = kp_ref[...]
        kcat[T:, :] = kc_ref[...]
        vcat[0:T, :] = vp_ref[...]
        vcat[T:, :] = vc_ref[...]
        dkcat[...] = jnp.zeros_like(dkcat)
        dvcat[...] = jnp.zeros_like(dvcat)
        dq_ref[...] = jnp.zeros_like(dq_ref)
        masks = _attn_masks()
        for d, rows, krows, l0 in groups:
            q, dy, y = q_ref[rows, :], dy_ref[rows, :], y_ref[rows, :]
            mrow, nrow = m_ref[rows, :], n_ref[rows, :]
            kk = kcat[krows, :].astype(BF16)
            vv = vcat[krows, :].astype(BF16)
            valid = _band_valid(b * (T // d) + l0)
            dq_acc = jnp.zeros((QB, LANE), F32)
            dk_acc = jnp.zeros((2 * QB, LANE), F32)
            dv_acc = jnp.zeros((2 * QB, LANE), F32)
            for hm in masks:
                qh = jnp.where(hm, q, 0.0).astype(BF16)
                dyh = jnp.where(hm, dy, 0.0)
                dyb = dyh.astype(BF16)
                dsum = jnp.sum(dyh * y, axis=-1, keepdims=True)
                mh = jnp.max(jnp.where(hm, mrow, MASK_VALUE), axis=-1, keepdims=True)
                nh = jnp.max(jnp.where(hm, nrow, 0.0), axis=-1, keepdims=True)
                s = lax.dot_general(qh, kk, (((1,), (1,)), ((), ())), preferred_element_type=F32) * scale
                p = jnp.where(valid, jnp.exp(s - mh), 0.0) / nh
                pb = p.astype(BF16)
                dv_h = lax.dot_general(pb, dyb, (((0,), (0,)), ((), ())), preferred_element_type=F32)
                dp = lax.dot_general(dyb, vv, (((1,), (1,)), ((), ())), preferred_element_type=F32)
                ds = (p * (dp - dsum) * scale).astype(BF16)
                dq_h = jnp.dot(ds, kk, preferred_element_type=F32)
                dk_h = lax.dot_general(ds, qh, (((0,), (0,)), ((), ())), preferred_element_type=F32)
                dq_acc += jnp.where(hm, dq_h, 0.0)
                dk_acc += dk_h
                dv_acc += dv_h
            dq_ref[rows, :] += dq_acc
            dkcat[krows, :] += dk_acc
            dvcat[krows, :] += dv_acc

        base = pl.multiple_of(b * T, T)
        dk_ref[pl.ds(base, T), :] += dkcat[T:, :]
        dv_ref[pl.ds(base, T), :] += dvcat[T:, :]

        @pl.when(b > 0)
        def _():
            prev = pl.multiple_of(b * T - T, T)
            dk_ref[pl.ds(prev, T), :] += dkcat[0:T, :]
            dv_ref[pl.ds(prev, T), :] += dvcat[0:T, :]

    qc, _ = _attn_specs(T, 0)
    kc, kp = _attn_specs(T, ATTN_W // LANE)
    vc, vp = _attn_specs(T, 2 * ATTN_W // LANE)
    whole = pl.BlockSpec((S, LANE), lambda hp, b: (0, hp))
    shp = jax.ShapeDtypeStruct((S, ATTN_W), F32)
    return pl.pallas_call(
        body, name="attn_bwd", grid=(ATTN_W // LANE, S // T),
        in_specs=[qc, kc, kp, vc, vp, qc, qc, qc, qc], out_specs=[qc, whole, whole], out_shape=[shp, shp, shp],
        scratch_shapes=[pltpu.VMEM((2 * T, LANE), F32)] * 4,
        compiler_params=_cp(("parallel", "arbitrary")),
    )(z, z, z, z, z, dya, ya, mg, den)


def _ssm_disc(lr, li, logdt, br, bi):
    dt = jnp.exp(logdt)
    mag = jnp.exp(lr * dt)
    ar = mag * jnp.cos(li * dt)
    ai = mag * jnp.sin(li * dt)
    nr, ni = ar - 1.0, ai
    den = lr * lr + li * li
    cr = (nr * lr + ni * li) / den
    ci = (ni * lr - nr * li) / den
    return ar, ai, cr * br - ci * bi, cr * bi + ci * br


def _ssm_prep(lr, li, logdt, br, bi):
    n, c = br.shape
    outs, _ = _rowwise("ssm_prep", lambda *a: (list(_ssm_disc(*a)), []), n, _tile(n, 512),
                       [_full(a) for a in (lr, li, logdt, br, bi)], [],
                       [(1, 1, _c0, F32), (1, 1, _c0, F32), (c, c, _c0, F32), (c, c, _c0, F32)])
    return outs


def _ssm_prep_bwd(lr, li, logdt, br, bi, dar, dai, dbbr, dbbi):
    n, c = br.shape

    def f(lrb, lib, dtb, brb, bib, *cts):
        _, vjp = jax.vjp(_ssm_disc, lrb, lib, dtb, brb, bib)
        return list(vjp(cts)), []

    outs, _ = _rowwise("ssm_prep_bwd", f, n, _tile(n, 512),
                       [_full(a) for a in (lr, li, logdt, br, bi, dar, dai, dbbr, dbbi)], [],
                       [(1, 1, _c0, F32)] * 3 + [(c, c, _c0, F32)] * 2)
    return outs


def _cmul(ar, ai, br, bi):
    return ar * br - ai * bi, ar * bi + ai * br


def _scan_consts(ar, ai, reverse):
    w = ar.shape[-1]
    a1 = (jnp.broadcast_to(ar, (8, w)), jnp.broadcast_to(ai, (8, w)))
    a2 = _cmul(*a1, *a1)
    a4 = _cmul(*a2, *a2)
    a8 = _cmul(*a4, *a4)
    row = lax.broadcasted_iota(jnp.int32, (8, w), 0)
    e = (8 - row) if reverse else (row + 1)
    one, zero = jnp.ones((8, w), F32), jnp.zeros((8, w), F32)
    pw = (one, zero)
    for bit, ap in ((1, a1), (2, a2), (4, a4), (8, a8)):
        sel = (e & bit) != 0
        nxt = _cmul(*pw, *ap)
        pw = (jnp.where(sel, nxt[0], pw[0]), jnp.where(sel, nxt[1], pw[1]))
    return (a1, a2, a4), pw, row


def _scan_group(xr, xi, cr, ci, consts, reverse):
    steps, pw, row = consts
    for sh, (pr, pi) in zip((1, 2, 4), steps):
        if reverse:
            sr, si = pltpu.roll(xr, 8 - sh, 0), pltpu.roll(xi, 8 - sh, 0)
            keep = row < 8 - sh
        else:
            sr, si = pltpu.roll(xr, sh, 0), pltpu.roll(xi, sh, 0)
            keep = row >= sh
        tr_, ti_ = _cmul(pr, pi, sr, si)
        xr = xr + jnp.where(keep, tr_, 0.0)
        xi = xi + jnp.where(keep, ti_, 0.0)
    tr_, ti_ = _cmul(pw[0], pw[1], cr, ci)
    return xr + tr_, xi + ti_


def _ssm_fwd(z, a_r, a_i, bdr, bdi, cmr, cmi, dskip, ts):
    S, ZW = z.shape
    NS = SSM_G * SSM_P
    PW = PACK * SSM_P
    uoff = (ZW - SSM_W) // LANE
    nsteps = S // ts

    def body(u_ref, ar_ref, ai_ref, bdr_ref, bdi_ref, cmr_ref, cmi_ref, d_ref, hr_ref, hi_ref, y_ref, car_r, car_i):
        s = pl.program_id(1)

        @pl.when(s == 0)
        def _():
            car_r[...] = jnp.zeros_like(car_r)
            car_i[...] = jnp.zeros_like(car_i)

        u = u_ref[...]
        ub = u.astype(BF16)
        nt = (((1,), (1,)), ((), ()))
        hr_ref[...] = lax.dot_general(ub, bdr_ref[...], nt, preferred_element_type=F32)
        hi_ref[...] = lax.dot_general(ub, bdi_ref[...], nt, preferred_element_type=F32)
        consts = _scan_consts(ar_ref[...], ai_ref[...], False)

        def step(j, carry):
            rows = pl.ds(pl.multiple_of(j * 8, 8), 8)
            hr, hi = _scan_group(hr_ref[rows, :], hi_ref[rows, :], carry[0], carry[1], consts, False)
            hr_ref[rows, :] = hr
            hi_ref[rows, :] = hi
            return jnp.broadcast_to(hr[7:8, :], (8, PW)), jnp.broadcast_to(hi[7:8, :], (8, PW))

        cr, ci = lax.fori_loop(0, ts // 8, step, (car_r[...], car_i[...]))
        car_r[...] = cr
        car_i[...] = ci
        y = lax.dot_general(hr_ref[...].astype(BF16), cmr_ref[...], nt, preferred_element_type=F32)
        y -= lax.dot_general(hi_ref[...].astype(BF16), cmi_ref[...], nt, preferred_element_type=F32)
        y_ref[...] = y + d_ref[...] * u

    row_a = pl.BlockSpec((1, PW), lambda i, s: (0, i))
    return pl.pallas_call(
        body, name="ssm_fwd", grid=(SSM_G // PACK, nsteps),
        in_specs=[pl.BlockSpec((ts, LANE), lambda i, s: (s, uoff + i)), row_a, row_a,
                  pl.BlockSpec((None, PW, LANE), lambda i, s: (i, 0, 0)), pl.BlockSpec((None, PW, LANE), lambda i, s: (i, 0, 0)),
                  pl.BlockSpec((None, LANE, PW), lambda i, s: (i, 0, 0)), pl.BlockSpec((None, LANE, PW), lambda i, s: (i, 0, 0)),
                  pl.BlockSpec((1, LANE), lambda i, s: (0, i))],
        out_specs=[pl.BlockSpec((ts, PW), lambda i, s: (s, i)), pl.BlockSpec((ts, PW), lambda i, s: (s, i)),
                   pl.BlockSpec((ts, LANE), lambda i, s: (s, i))],
        out_shape=[jax.ShapeDtypeStruct((S, NS), F32), jax.ShapeDtypeStruct((S, NS), F32),
                   jax.ShapeDtypeStruct((S, SSM_W), F32)],
        scratch_shapes=[pltpu.VMEM((8, PW), F32), pltpu.VMEM((8, PW), F32)],
        compiler_params=_cp(("parallel", "arbitrary")),
    )(z, a_r, a_i, bdr, bdi, cmr, cmi, dskip)


def _ssm_bwd(z, dyp, hr, hi, a_r, a_i, bdr, bdi, cmr, cmi, dskip, ts):
    S, ZW = z.shape
    NS = SSM_G * SSM_P
    PW = PACK * SSM_P
    uoff = (ZW - SSM_W) // LANE
    nsteps = S // ts
    npk = SSM_G // PACK

    def body(u_ref, dy_ref, hr_ref, hi_ref, hpr_ref, hpi_ref, ar_ref, ai_ref, bdr_ref, bdi_ref, cmr_ref, cmi_ref,
             d_ref, du_ref, dbdr_ref, dbdi_ref, dcmr_ref, dcmi_ref, dar_ref, dai_ref, dd_ref,
             lr_s, li_s, hcr, hci, car_r, car_i):
        s = pl.program_id(1)
        first_tile = s == nsteps - 1

        @pl.when(s == 0)
        def _():
            car_r[...] = jnp.zeros_like(car_r)
            car_i[...] = jnp.zeros_like(car_i)
            for r in (dbdr_ref, dbdi_ref, dcmr_ref, dcmi_ref, dar_ref, dai_ref, dd_ref):
                r[...] = jnp.zeros_like(r)

        u, dy = u_ref[...], dy_ref[...]
        ub, dyb = u.astype(BF16), dy.astype(BF16)
        lr_s[...] = jnp.dot(dyb, cmr_ref[...], preferred_element_type=F32)
        li_s[...] = -jnp.dot(dyb, cmi_ref[...], preferred_element_type=F32)
        keep_prev = jnp.where(first_tile, 0.0, 1.0)
        hcr[0:8, :] = hpr_ref[...] * keep_prev
        hci[0:8, :] = hpi_ref[...] * keep_prev
        hcr[8:, :] = hr_ref[...]
        hci[8:, :] = hi_ref[...]
        consts = _scan_consts(ar_ref[...], -ai_ref[...], True)
        row = consts[2]
        ngrp = ts // 8

        def step(jj, carry):
            cr, ci, accr, acci = carry
            j = ngrp - 1 - jj
            rows = pl.ds(pl.multiple_of(j * 8, 8), 8)
            nxt = pl.ds(pl.multiple_of(j * 8 + 8, 8), 8)
            lr, li = _scan_group(lr_s[rows, :], li_s[rows, :], cr, ci, consts, True)
            lr_s[rows, :] = lr
            li_s[rows, :] = li
            pr, pi = hcr[rows, :], hci[rows, :]
            hsr = jnp.where(row == 0, jnp.broadcast_to(pr[7:8, :], (8, PW)), pltpu.roll(hcr[nxt, :], 1, 0))
            hsi = jnp.where(row == 0, jnp.broadcast_to(pi[7:8, :], (8, PW)), pltpu.roll(hci[nxt, :], 1, 0))
            accr = accr + lr * hsr + li * hsi
            acci = acci + li * hsr - lr * hsi
            return jnp.broadcast_to(lr[0:1, :], (8, PW)), jnp.broadcast_to(li[0:1, :], (8, PW)), accr, acci

        zero = jnp.zeros((8, PW), F32)
        cr, ci, accr, acci = lax.fori_loop(0, ngrp, step, (car_r[...], car_i[...], zero, zero))
        car_r[...] = cr
        car_i[...] = ci
        dar_ref[...] += jnp.sum(accr, axis=0, keepdims=True)
        dai_ref[...] += jnp.sum(acci, axis=0, keepdims=True)
        lrb, lib = lr_s[...].astype(BF16), li_s[...].astype(BF16)
        du = jnp.dot(lrb, bdr_ref[...], preferred_element_type=F32)
        du += jnp.dot(lib, bdi_ref[...], preferred_element_type=F32)
        du_ref[...] = du + dy * d_ref[...]
        tn = (((0,), (0,)), ((), ()))
        dbdr_ref[...] += lax.dot_general(lrb, ub, tn, preferred_element_type=F32)
        dbdi_ref[...] += lax.dot_general(lib, ub, tn, preferred_element_type=F32)
        dcmr_ref[...] += lax.dot_general(dyb, hr_ref[...].astype(BF16), tn, preferred_element_type=F32)
        dcmi_ref[...] -= lax.dot_general(dyb, hi_ref[...].astype(BF16), tn, preferred_element_type=F32)
        dd_ref[...] += jnp.sum(dy * u, axis=0, keepdims=True)

    rev = lambda s: nsteps - 1 - s
    row_a = pl.BlockSpec((1, PW), lambda i, s: (0, i))
    tile = pl.BlockSpec((ts, PW), lambda i, s: (rev(s), i))
    prev8 = pl.BlockSpec((8, PW), lambda i, s: (jnp.maximum(rev(s) * (ts // 8) - 1, 0), i))
    cols = pl.BlockSpec((ts, LANE), lambda i, s: (rev(s), i))
    bd = pl.BlockSpec((None, PW, LANE), lambda i, s: (i, 0, 0))
    cm = pl.BlockSpec((None, LANE, PW), lambda i, s: (i, 0, 0))
    return pl.pallas_call(
        body, name="ssm_bwd", grid=(npk, nsteps),
        in_specs=[pl.BlockSpec((ts, LANE), lambda i, s: (rev(s), uoff + i)), cols, tile, tile, prev8, prev8,
                  row_a, row_a, bd, bd, cm, cm, pl.BlockSpec((1, LANE), lambda i, s: (0, i))],
        out_specs=[cols, bd, bd, cm, cm, row_a, row_a, pl.BlockSpec((1, LANE), lambda i, s: (0, i))],
        out_shape=[jax.ShapeDtypeStruct((S, SSM_W), F32),
                   jax.ShapeDtypeStruct((npk, PW, LANE), F32), jax.ShapeDtypeStruct((npk, PW, LANE), F32),
                   jax.ShapeDtypeStruct((npk, LANE, PW), F32), jax.ShapeDtypeStruct((npk, LANE, PW), F32),
                   jax.ShapeDtypeStruct((1, NS), F32), jax.ShapeDtypeStruct((1, NS), F32),
                   jax.ShapeDtypeStruct((1, SSM_W), F32)],
        scratch_shapes=[pltpu.VMEM((ts, PW), F32), pltpu.VMEM((ts, PW), F32),
                        pltpu.VMEM((ts + 8, PW), F32), pltpu.VMEM((ts + 8, PW), F32),
                        pltpu.VMEM((8, PW), F32), pltpu.VMEM((8, PW), F32)],
        compiler_params=_cp(("parallel", "arbitrary")),
    )(z, dyp, hr, hi, hr, hi, a_r, a_i, bdr, bdi, cmr, cmi, dskip)


def _block_diag(m4):
    npk, g, a, b = m4.shape
    eye = jnp.eye(g, dtype=m4.dtype)
    return (m4[:, :, :, None, :] * eye[None, :, None, :, None]).reshape(npk, g * a, g * b)


def _block_diag_take(m, a, b):
    npk = m.shape[0]
    m5 = m.reshape(npk, PACK, a, PACK, b)
    return jnp.stack([m5[:, g, :, g, :] for g in range(PACK)], axis=1)


def _mix_out(ya, ypre, gl, ga, gb, bglu):
    yg = jax.nn.gelu(ypre)
    yb = yg * jax.nn.sigmoid(gl + bglu)
    return jnp.concatenate([_rms(ya, ga), _rms(yb, gb)], axis=-1)


def _tail_loss(h3, gl, pe, gf, tgt):
    h4 = h3 + jax.nn.sigmoid(gl) * pe
    err = jnp.square(_rms(h4, gf) - tgt)
    return 0.5 * jnp.mean(err, axis=-1, keepdims=True)


def kernel(x, p, ffn1_norm, ffn1_w_gate, ffn1_w_up, ffn1_w_down, mix_norm, w_in, attn_out_norm, ssm_lambda_re, ssm_lambda_im, ssm_log_dt, ssm_b_re, ssm_b_im, ssm_c_re, ssm_c_im, ssm_d, ssm_w_glu, ssm_b_glu, ssm_out_norm, w_out, ffn2_norm, ffn2_w_gate, ffn2_w_up, ffn2_w_down, ple_norm, ple_w_gate, ple_w_proj, final_norm, loss_target, m_ffn1_norm, m_ffn1_w_gate, m_ffn1_w_up, m_ffn1_w_down, m_mix_norm, m_w_in, m_attn_out_norm, m_ssm_lambda_re, m_ssm_lambda_im, m_ssm_log_dt, m_ssm_b_re, m_ssm_b_im, m_ssm_c_re, m_ssm_c_im, m_ssm_d, m_ssm_w_glu, m_ssm_b_glu, m_ssm_out_norm, m_w_out, m_ffn2_norm, m_ffn2_w_gate, m_ffn2_w_up, m_ffn2_w_down, m_ple_norm, m_ple_w_gate, m_ple_w_proj, m_final_norm, v_ffn1_norm, v_ffn1_w_gate, v_ffn1_w_up, v_ffn1_w_down, v_mix_norm, v_w_in, v_attn_out_norm, v_ssm_lambda_re, v_ssm_lambda_im, v_ssm_log_dt, v_ssm_b_re, v_ssm_b_im, v_ssm_c_re, v_ssm_c_im, v_ssm_d, v_ssm_w_glu, v_ssm_b_glu, v_ssm_out_norm, v_w_out, v_ffn2_norm, v_ffn2_w_gate, v_ffn2_w_up, v_ffn2_w_down, v_ple_norm, v_ple_w_gate, v_ple_w_proj, v_final_norm):
    A = dict(locals())
    xs = x[0]
    ps = p[0, 0]
    tgt = loss_target[0]
    S, D = xs.shape
    FSH = ffn1_w_gate.shape[-1]
    FSP = -(-FSH // LANE) * LANE
    TR = _tile(S, 256)
    ZW = 3 * ATTN_W + SSM_W

    wgu1 = _prep("prep_gu1", [ffn1_w_gate[0], ffn1_w_up[0]], D, FSP)
    wgu2 = _prep("prep_gu2", [ffn2_w_gate[0], ffn2_w_up[0]], D, FSP)
    wd1 = _prep("prep_d1", [ffn1_w_down[0]], FSP, D)
    wd2 = _prep("prep_d2", [ffn2_w_down[0]], FSP, D)
    win = _prep("prep_in", [w_in[0]], D, w_in.shape[-1])
    wglu = _prep("prep_glu", [ssm_w_glu[0]], ssm_w_glu.shape[1], SSM_W)
    wout = _prep("prep_out", [w_out[0]], w_out.shape[1], D)
    wpg = _prep("prep_pg", [ple_w_gate[0]], ple_w_gate.shape[1], D)
    wpp = _prep("prep_pp", [ple_w_proj[0]], ple_w_proj.shape[1], ple_w_proj.shape[2])
    Wgu1, Wd1, Win, Wglu, Wout, Wgu2, Wd2, Wpg, Wpp = _all_gather(
        "ag_weights", [wgu1, wd1, win, wglu, wout, wgu2, wd2, wpg, wpp])
    rowstack = lambda w: w.reshape(1, w.shape[0] * w.shape[1], w.shape[2])
    Wd1, Wd2, Wglu, Wout, Wpg = (rowstack(w) for w in (Wd1, Wd2, Wglu, Wout, Wpg))

    def ffn_fwd(tag, h, gain, Wgu, Wd):
        (xn,), _ = _rowwise(f"{tag}_norm", lambda a, g: ([_rms(a, g)], []), S, TR, [_full(h)], [gain], [(D, D, _c0, BF16)])
        gu, hid = _ffn_up(f"{tag}_up", xn, Wgu)
        out = _mm_nn(f"{tag}_down", hid, Wd, tn=D, tk=2 * FSP, res=h, scale=0.5)
        return out, (xn, gu, hid)

    h1, (xn1, gu1, hid1) = ffn_fwd("ffn1", xs, ffn1_norm, Wgu1, Wd1)

    (un,), _ = _rowwise("mix_norm", lambda a, g: ([_rms(a, g)], []), S, TR, [_full(h1)], [mix_norm], [(D, D, _c0, BF16)])
    z = _mm_nn("mix_in", un, Win, tn=512, tk=D)
    ya, mg, den = _attn_fwd(z)

    col = lambda a: a.reshape(-1, 1)
    lr_c, li_c = col(ssm_lambda_re), col(ssm_lambda_im)
    dt_c = col(jnp.broadcast_to(ssm_log_dt.reshape(SSM_G, 1), (SSM_G, SSM_P)))
    b_re2, b_im2 = ssm_b_re.reshape(-1, SSM_C), ssm_b_im.reshape(-1, SSM_C)
    ar_c, ai_c, bbr, bbi = _ssm_prep(lr_c, li_c, dt_c, b_re2, b_im2)
    a_r, a_i = ar_c.reshape(1, -1), ai_c.reshape(1, -1)
    npk = SSM_G // PACK
    bdr = _block_diag(bbr.reshape(npk, PACK, SSM_P, SSM_C)).astype(BF16)
    bdi = _block_diag(bbi.reshape(npk, PACK, SSM_P, SSM_C)).astype(BF16)
    cmr = _block_diag(ssm_c_re.reshape(npk, PACK, SSM_C, SSM_P)).astype(BF16)
    cmi = _block_diag(ssm_c_im.reshape(npk, PACK, SSM_C, SSM_P)).astype(BF16)
    TS = _tile(S, 512)
    hr, hi, ypre = _ssm_fwd(z, a_r, a_i, bdr, bdi, cmr, cmi, ssm_d, TS)
    (yg,), _ = _rowwise("ssm_gelu", lambda a: ([jax.nn.gelu(a)], []), S, TR, [_full(ypre)], [], [(SSM_W, SSM_W, _c0, BF16)])
    gl = _mm_nn("ssm_glu", yg, Wglu, tn=SSM_W, tk=SSM_W)
    (ycat,), _ = _rowwise("mix_out", lambda *a: ([_mix_out(*a)], []), S, TR, [_full(ya), _full(ypre), _full(gl)],
                          [attn_out_norm, ssm_out_norm, ssm_b_glu], [(MIX_W, MIX_W, _c0, BF16)])
    h2 = _mm_nn("mix_proj", ycat, Wout, tn=D // 2, tk=D, res=h1, scale=1.0)

    h3, (xn2, gu2, hid2) = ffn_fwd("ffn2", h2, ffn2_norm, Wgu2, Wd2)

    (hn, pb), _ = _rowwise("ple_norm", lambda a, q, g: ([_rms(a, g), q], []), S, TR, [_full(h3), _full(ps)], [ple_norm],
                           [(D, D, _c0, BF16), (ps.shape[1], ps.shape[1], _c0, BF16)])
    pgl = _mm_nn("ple_gate", hn, Wpg, tn=D // 2, tk=D)
    pe = _mm_nn("ple_proj", pb, Wpp, tn=Wpp.shape[2], tk=Wpp.shape[1])

    def tail(h3b, glb, peb, tb, gf):
        rows, vjp = jax.vjp(lambda a, b, c, g: _tail_loss(a, b, c, g, tb), h3b, glb, peb, gf)
        dh, dgl, dpe, dgf = vjp(jnp.ones_like(rows))
        return [dh, dgl, dpe], [jnp.broadcast_to(jnp.sum(rows, axis=0, keepdims=True), (1, LANE)), dgf]

    (dh3_dir, dpgl, dpe), (loss_row, g_final) = _rowwise(
        "tail", tail, S, TR, [_full(h3), _full(pgl), _full(pe), _full(tgt)], [final_norm.reshape(1, D)],
        [(D, D, _c0, F32), (D, D, _c0, BF16), (D, D, _c0, BF16)], [(LANE, LANE, _c0), (D, D, _c0)])
    loss = lax.psum(loss_row[0, 0], AXES)

    def norm_bwd(tag, h, gain, dn, dres):
        def f(hb, dnb, drb, g):
            _, vjp = jax.vjp(_rms, hb, g)
            dh, dg = vjp(dnb)
            dh = dh + drb
            return [dh, dh], [dg]
        (dh, dhb), (dg,) = _rowwise(f"{tag}_norm_bwd", f, S, TR, [_full(h), _full(dn), _full(dres)], [gain],
                                    [(D, D, _c0, F32), (D, D, _c0, BF16)], [(D, D, _c0)])
        return dh, dhb, dg

    dhn = _mm_nt("ple_gate_dx", dpgl, Wpg, tn=D, tk=D)
    g_wpg = _mm_tn("ple_gate_dw", hn, dpgl, 1)
    g_wpp = _mm_tn("ple_proj_dw", pb, dpe, NDEV)
    dh3, dh3b, g_ple_norm = norm_bwd("ple", h3, ple_norm, dhn, dh3_dir)

    def ffn_bwd(tag, h, gain, Wgu, Wd, saved, dout, doutb):
        xn, gu, hid = saved
        dgu = _ffn_down_dx(f"{tag}_down_dx", doutb, Wd, gu, NDEV, scale=0.5)
        g_wd = _mm_tn(f"{tag}_down_dw", hid, doutb, 1, tm=2048, tko=FSP, tn=D // 2, scale=0.5)
        dxn = _mm_nt(f"{tag}_up_dx", dgu, Wgu, tm=1024, tn=D, tk=2 * FSP)
        g_wgu = _mm_tn(f"{tag}_up_dw", xn, dgu, NDEV, tm=2048, tn=FSP)
        dh, dhb, g_norm = norm_bwd(tag, h, gain, dxn, dout)
        return dh, dhb, g_norm, g_wgu, g_wd

    dh2, dh2b, g_ffn2_norm, g_wgu2, g_wd2 = ffn_bwd("ffn2", h2, ffn2_norm, Wgu2, Wd2, (xn2, gu2, hid2), dh3, dh3b)

    dycat = _mm_nt("mix_proj_dx", dh2b, Wout, tn=D, tk=D)
    g_wout = _mm_tn("mix_proj_dw", ycat, dh2b, 1)

    def mix_out_bwd(yab, ypb, glb, dyc, ga, gb, bglu):
        _, vjp = jax.vjp(_mix_out, yab, ypb, glb, ga, gb, bglu)
        dya_, dyp_, dgl_, dga, dgb, dbg = vjp(dyc)
        return [dya_, dyp_, dgl_], [dga, dgb, dbg]
    (dya, dyp_dir, dglb), (g_attn_norm, g_ssm_norm, g_bglu) = _rowwise(
        "mix_out_bwd", mix_out_bwd, S, TR, [_full(ya), _full(ypre), _full(gl), _full(dycat)],
        [attn_out_norm, ssm_out_norm, ssm_b_glu],
        [(ATTN_W, ATTN_W, _c0, F32), (SSM_W, SSM_W, _c0, F32), (SSM_W, SSM_W, _c0, BF16)],
        [(ATTN_W, ATTN_W, _c0), (SSM_W, SSM_W, _c0), (SSM_W, SSM_W, _c0)])
    dyg = _mm_nt("ssm_glu_dx", dglb, Wglu, tn=SSM_W, tk=SSM_W)
    g_wglu = _mm_tn("ssm_glu_dw", yg, dglb, 1)

    def gelu_bwd(ypb, dygb, ddir):
        _, vjp = jax.vjp(jax.nn.gelu, ypb)
        return [ddir + vjp(dygb)[0]], []
    (dyp,), _ = _rowwise("ssm_gelu_bwd", gelu_bwd, S, TR, [_full(ypre), _full(dyg), _full(dyp_dir)], [],
                         [(SSM_W, SSM_W, _c0, F32)])
    du, dbdr, dbdi, dcmr, dcmi, da_r, da_i, g_ssm_d = _ssm_bwd(z, dyp, hr, hi, a_r, a_i, bdr, bdi, cmr, cmi, ssm_d, TS)
    dbbr = _block_diag_take(dbdr, SSM_P, SSM_C).reshape(-1, SSM_C)
    dbbi = _block_diag_take(dbdi, SSM_P, SSM_C).reshape(-1, SSM_C)
    g_c_re = _block_diag_take(dcmr, SSM_C, SSM_P).reshape(ssm_c_re.shape)
    g_c_im = _block_diag_take(dcmi, SSM_C, SSM_P).reshape(ssm_c_im.shape)
    dlr, dli, ddt, g_b_re, g_b_im = _ssm_prep_bwd(lr_c, li_c, dt_c, b_re2, b_im2, col(da_r), col(da_i), dbbr, dbbi)
    g_lam_re, g_lam_im = dlr.reshape(ssm_lambda_re.shape), dli.reshape(ssm_lambda_im.shape)
    g_log_dt = jnp.sum(ddt.reshape(SSM_G, SSM_P), axis=1).reshape(ssm_log_dt.shape)
    g_b_re, g_b_im = g_b_re.reshape(ssm_b_re.shape), g_b_im.reshape(ssm_b_im.shape)

    dq, dk, dv = _attn_bwd(z, dya, ya, mg, den)
    (dz,), _ = _rowwise("mix_dz", lambda *a: ([jnp.concatenate(a, axis=-1)], []), S, TR,
                        [_full(dq), _full(dk), _full(dv), _full(du)], [], [(ZW, ZW, _c0, BF16)])
    dun = _mm_nt("mix_in_dx", dz, Win, tn=D, tk=512)
    g_win = _mm_tn("mix_in_dw", un, dz, NDEV, tn=512)
    dh1, dh1b, g_mix_norm = norm_bwd("mix", h1, mix_norm, dun, dh2)

    dx, _dxb, g_ffn1_norm, g_wgu1, g_wd1 = ffn_bwd("ffn1", xs, ffn1_norm, Wgu1, Wd1, (xn1, gu1, hid1), dh1, dh1b)

    restack = lambda g: g.reshape((NDEV, g.shape[1] // NDEV) + g.shape[2:])
    big = [g_wgu1, restack(g_wd1), g_win, restack(g_wglu), restack(g_wout), g_wgu2, restack(g_wd2), restack(g_wpg), g_wpp]
    mine = _reduce_scatter("rs", big)
    out = {}

    def upd(name, idx, *, tr, cw, gw, goff=0):
        w, m, v = A[name][0], A["m_" + name][0], A["v_" + name][0]
        g, dlt, mn, vn = _adamw("adamw_" + name, w, m, v, mine[idx], tr=tr, cw=cw, gw=gw, goff=goff)
        for k, val in (("grad_", g), ("delta_", dlt), ("new_m_", mn), ("new_v_", vn)):
            out[k + name] = val[None]

    DT = _tile(D, 256)
    FT = _tile(FSH, 512)
    DC = _tile(D, 1024, LANE)
    upd("ffn1_w_gate", 0, tr=DT, cw=FSH, gw=FSP, goff=0)
    upd("ffn1_w_up", 0, tr=DT, cw=FSH, gw=FSP, goff=1)
    upd("ffn1_w_down", 1, tr=FT, cw=DC, gw=DC)
    upd("w_in", 2, tr=DT, cw=w_in.shape[-1], gw=w_in.shape[-1])
    upd("ssm_w_glu", 3, tr=ssm_w_glu.shape[1], cw=SSM_W, gw=SSM_W)
    upd("w_out", 4, tr=w_out.shape[1], cw=DC, gw=DC)
    upd("ffn2_w_gate", 5, tr=DT, cw=FSH, gw=FSP, goff=0)
    upd("ffn2_w_up", 5, tr=DT, cw=FSH, gw=FSP, goff=1)
    upd("ffn2_w_down", 6, tr=FT, cw=DC, gw=DC)
    upd("ple_w_gate", 7, tr=ple_w_gate.shape[1], cw=DC, gw=DC)
    upd("ple_w_proj", 8, tr=ple_w_proj.shape[1], cw=ple_w_proj.shape[2], gw=ple_w_proj.shape[2])

    small = [("ffn1_norm", g_ffn1_norm), ("mix_norm", g_mix_norm), ("attn_out_norm", g_attn_norm),
             ("ssm_lambda_re", g_lam_re), ("ssm_lambda_im", g_lam_im), ("ssm_log_dt", g_log_dt),
             ("ssm_b_re", g_b_re), ("ssm_b_im", g_b_im), ("ssm_c_re", g_c_re), ("ssm_c_im", g_c_im),
             ("ssm_d", g_ssm_d), ("ssm_b_glu", g_bglu), ("ssm_out_norm", g_ssm_norm), ("ffn2_norm", g_ffn2_norm),
             ("ple_norm", g_ple_norm), ("final_norm", g_final)]
    chunk = 8 * LANE

    def pack(arrs):
        parts = []
        for a in arrs:
            flat = a.reshape(-1)
            padn = -(-flat.shape[0] // chunk) * chunk
            parts.append(jnp.pad(flat, (0, padn - flat.shape[0])).reshape(-1, LANE))
        return jnp.concatenate(parts, axis=0)

    g_pack = pack([g for _, g in small])
    (g_all,) = _all_gather("ag_small", [g_pack])
    g_sum = _sum8("small_sum", g_all)
    w_pack = pack([A[n] for n, _ in small])
    m_pack = pack([A["m_" + n] for n, _ in small])
    v_pack = pack([A["v_" + n] for n, _ in small])
    d_pack, mn_pack, vn_pack = _adamw_small("adamw_small", w_pack, m_pack, v_pack, g_sum)
    off = 0
    for n, _ in small:
        shape = A[n].shape
        size = math.prod(shape)
        rows = -(-size // chunk) * 8
        for k, buf in (("grad_", g_sum), ("delta_", d_pack), ("new_m_", mn_pack), ("new_v_", vn_pack)):
            out[k + n] = buf[off:off + rows].reshape(-1)[:size].reshape(shape)
        off += rows

    names = ['ffn1_norm', 'ffn1_w_gate', 'ffn1_w_up', 'ffn1_w_down', 'mix_norm', 'w_in', 'attn_out_norm',
             'ssm_lambda_re', 'ssm_lambda_im', 'ssm_log_dt', 'ssm_b_re', 'ssm_b_im', 'ssm_c_re', 'ssm_c_im', 'ssm_d',
             'ssm_w_glu', 'ssm_b_glu', 'ssm_out_norm', 'w_out', 'ffn2_norm', 'ffn2_w_gate', 'ffn2_w_up', 'ffn2_w_down',
             'ple_norm', 'ple_w_gate', 'ple_w_proj', 'final_norm']
    return (loss, dx[None], *[out[k + n] for k in ("grad_", "delta_", "new_m_", "new_v_") for n in names])
```

```python
import functools
import math

import jax
import jax.numpy as jnp
from jax import lax
from jax.experimental import pallas as pl
from jax.experimental.pallas import tpu as pltpu

F32, BF16 = jnp.float32, jnp.bfloat16
MESH = pl.DeviceIdType.MESH
NDEV = 8
AXES = ("x", "y", "c")
LANE = 128
VMEM_LIMIT = 56 * 1024 * 1024

ATTN_W = 1024
HEAD_DIM = 64
SSM_W = 1024
MIX_W = ATTN_W + SSM_W
SSM_G, SSM_P, SSM_C = 64, 64, 16
PACK = 8
DILATIONS = (1, 4, 16)
QB = 128
NORM_EPS = 1e-6
MASK_VALUE = -1e30
LR, B1, B2, EPS, WD, STEP = 0.001, 0.9, 0.999, 1e-08, 0.01, 10


def _cp(sem=None):
    return pltpu.CompilerParams(dimension_semantics=sem, vmem_limit_bytes=VMEM_LIMIT)


def _tile(n, target, mult=8):
    if n <= target:
        return n
    for t in range(target - target % mult, 0, -mult):
        if n % t == 0:
            return t
    return n


def _rms(x, g):
    return x * lax.rsqrt(jnp.mean(x * x, axis=-1, keepdims=True) + NORM_EPS) * g


def _rowwise(name, fn, S, tr, rows, fulls, outs, accs=(), ncol=1):
    nr, nf, no, na = len(rows), len(fulls), len(outs), len(accs)

    def body(*refs):
        ins = [r[...] for r in refs[:nr + nf]]
        o_refs = refs[nr + nf:nr + nf + no]
        a_refs = refs[nr + nf + no:]
        o_vals, a_vals = fn(*ins)
        for r, v in zip(o_refs, o_vals):
            r[...] = v.astype(r.dtype)
        if na:
            @pl.when(pl.program_id(1) == 0)
            def _():
                for r in a_refs:
                    r[...] = jnp.zeros_like(r)
            for r, v in zip(a_refs, a_vals):
                r[...] += v

    in_specs = [pl.BlockSpec((tr, w), functools.partial(lambda j, i, cm: (i, cm(j)), cm=cm)) for _, w, cm in rows]
    in_specs += [pl.BlockSpec(f.shape, functools.partial(lambda j, i, nd: (0,) * nd, nd=f.ndim)) for f in fulls]
    out_specs = [pl.BlockSpec((tr, w), functools.partial(lambda j, i, cm: (i, cm(j)), cm=cm)) for _, w, cm, _ in outs]
    out_specs += [pl.BlockSpec((1, w), functools.partial(lambda j, i, cm: (0, cm(j)), cm=cm)) for _, w, cm in accs]
    out_shape = [jax.ShapeDtypeStruct((S, c), dt) for c, _, _, dt in outs]
    out_shape += [jax.ShapeDtypeStruct((1, c), F32) for c, _, _ in accs]
    res = pl.pallas_call(
        body, name=name, grid=(ncol, S // tr), in_specs=in_specs, out_specs=out_specs, out_shape=out_shape,
        compiler_params=_cp(("parallel", "arbitrary" if na else "parallel")),
    )(*[a for a, _, _ in rows], *fulls)
    return res[:no], res[no:]


def _c0(j):
    return 0


def _full(a):
    return (a, a.shape[1], _c0)


def _mm_nn(name, a, w, *, out_dtype=F32, tm=512, tn=768, tk=2048, res=None, scale=1.0, rider=None):
    M, K = a.shape
    J, K2, Np = w.shape
    assert K == K2
    tm, tn, tk = _tile(M, tm), _tile(Np, tn, LANE), _tile(K, tk, LANE)
    npj = Np // tn
    nk = K // tk
    split = _with_rider(rider, 2 if res is None else 3, 1, 1)
    grid = (M // tm, J * npj, nk)

    def body(*refs):
        refs, r_in, r_out, r_sems = split(refs)
        if rider:
            step = (pl.program_id(0) * grid[1] + pl.program_id(1)) * nk + pl.program_id(2)
            rider.run(r_in, r_out, r_sems, step, grid[0] * grid[1] * nk)
        if res is None:
            a_ref, w_ref, o_ref, acc = refs
        else:
            a_ref, w_ref, r_ref, o_ref, acc = refs
        k = pl.program_id(2)
        part = jnp.dot(a_ref[...].astype(BF16), w_ref[...], preferred_element_type=F32)

        def finish(v):
            if res is not None:
                v = r_ref[...] + scale * v
            o_ref[...] = v.astype(o_ref.dtype)

        if nk == 1:
            finish(part)
            return

        @pl.when(k == 0)
        def _():
            acc[...] = part

        @pl.when(k > 0)
        def _():
            acc[...] += part

        @pl.when(k == nk - 1)
        def _():
            finish(acc[...])

    in_specs = [pl.BlockSpec((tm, tk), lambda i, n, k: (i, k)),
                pl.BlockSpec((None, tk, tn), lambda i, n, k: (n // npj, k, n % npj))]
    args = [a, w]
    if res is not None:
        in_specs.append(pl.BlockSpec((tm, tn), lambda i, n, k: (i, n)))
        args.append(res)
    out_spec = pl.BlockSpec((tm, tn), lambda i, n, k: (i, n))
    out_shape = jax.ShapeDtypeStruct((M, J * Np), out_dtype)
    if rider is None:
        return pl.pallas_call(
            body, name=name, grid=grid, in_specs=in_specs, out_specs=out_spec, out_shape=out_shape,
            scratch_shapes=[pltpu.VMEM((tm, tn), F32)],
            compiler_params=_cp(("parallel", "parallel", "arbitrary")),
        )(*args)
    out = pl.pallas_call(
        body, name=name, grid=grid, in_specs=in_specs + rider.in_specs, out_specs=[out_spec] + rider.out_specs,
        out_shape=[out_shape] + rider.out_shape, scratch_shapes=[pltpu.VMEM((tm, tn), F32)] + rider.scratch,
        compiler_params=_cp(("arbitrary", "arbitrary", "arbitrary")),
    )(*args, *rider.shards)
    return out[0], list(out[1:])


def _mm_nt(name, dy, w, *, out_dtype=F32, tm=512, tn=2048, tk=768, scale=1.0):
    M, N = dy.shape
    J, K, Np = w.shape
    assert N == J * Np
    tm, tn, tk = _tile(M, tm), _tile(K, tn, LANE), _tile(Np, tk, LANE)
    npj = Np // tk
    nc = J * npj

    def body(a_ref, w_ref, o_ref, acc):
        c = pl.program_id(2)
        part = lax.dot_general(a_ref[...].astype(BF16), w_ref[...], (((1,), (1,)), ((), ())),
                               preferred_element_type=F32)
        if nc == 1:
            o_ref[...] = (scale * part).astype(o_ref.dtype)
            return

        @pl.when(c == 0)
        def _():
            acc[...] = part

        @pl.when(c > 0)
        def _():
            acc[...] += part

        @pl.when(c == nc - 1)
        def _():
            o_ref[...] = (scale * acc[...]).astype(o_ref.dtype)

    return pl.pallas_call(
        body, name=name, grid=(M // tm, K // tn, nc),
        in_specs=[pl.BlockSpec((tm, tk), lambda i, n, c: (i, c)),
                  pl.BlockSpec((None, tn, tk), lambda i, n, c: (c // npj, n, c % npj))],
        out_specs=pl.BlockSpec((tm, tn), lambda i, n, c: (i, n)),
        out_shape=jax.ShapeDtypeStruct((M, K), out_dtype),
        scratch_shapes=[pltpu.VMEM((tm, tn), F32)],
        compiler_params=_cp(("parallel", "parallel", "arbitrary")),
    )(dy, w)


def _mm_tn(name, x, dy, J, *, tm=1024, tko=1024, tn=768, scale=1.0):
    M, K = x.shape
    M2, N = dy.shape
    assert M == M2 and N % J == 0
    Np = N // J
    tm, tko, tn = _tile(M, tm, LANE), _tile(K, tko, LANE), _tile(Np, tn, LANE)
    npj = Np // tn
    nm = M // tm

    def body(x_ref, d_ref, o_ref, acc):
        m = pl.program_id(2)
        part = lax.dot_general(x_ref[...].astype(BF16), d_ref[...].astype(BF16), (((0,), (0,)), ((), ())),
                               preferred_element_type=F32)
        if nm == 1:
            o_ref[...] = scale * part
            return

        @pl.when(m == 0)
        def _():
            acc[...] = part

        @pl.when(m > 0)
        def _():
            acc[...] += part

        @pl.when(m == nm - 1)
        def _():
            o_ref[...] = scale * acc[...]

    return pl.pallas_call(
        body, name=name, grid=(K // tko, J * npj, nm),
        in_specs=[pl.BlockSpec((tm, tko), lambda k, n, m: (m, k)),
                  pl.BlockSpec((tm, tn), lambda k, n, m: (m, n))],
        out_specs=pl.BlockSpec((None, tko, tn), lambda k, n, m: (n // npj, k, n % npj)),
        out_shape=jax.ShapeDtypeStruct((J, K, Np), F32),
        scratch_shapes=[pltpu.VMEM((tko, tn), F32)],
        compiler_params=_cp(("parallel", "parallel", "arbitrary")),
    )(x, dy)


def _swiglu_act(g, u):
    return jax.nn.silu(g) * u


def _with_rider(rider, n_in, n_out, n_scratch):
    nr = rider.n if rider else 0

    def split(refs):
        a, b = n_in, n_in + nr
        c, d = b + n_out, b + n_out + nr
        host = refs[:a] + refs[b:c] + refs[d:d + n_scratch]
        return host, refs[a:b], refs[c:d], refs[d + n_scratch:]

    return split


def _ffn_up(name, xn, wgu, *, tm=1024, rider=None):
    M, K = xn.shape
    J, _, F2 = wgu.shape
    F = F2 // 2
    tm = _tile(M, tm)
    split = _with_rider(rider, 2, 2, 0)

    def body(*refs):
        (a_ref, w_ref, gu_ref, h_ref), r_in, r_out, r_sems = split(refs)
        if rider:
            rider.run(r_in, r_out, r_sems, pl.program_id(0) * J + pl.program_id(1), (M // tm) * J)
        r = jnp.dot(a_ref[...], w_ref[...], preferred_element_type=F32)
        gu_ref[...] = r.astype(gu_ref.dtype)
        h_ref[...] = _swiglu_act(r[:, :F], r[:, F:]).astype(h_ref.dtype)

    res = pl.pallas_call(
        body, name=name, grid=(M // tm, J),
        in_specs=[pl.BlockSpec((tm, K), lambda i, j: (i, 0)), pl.BlockSpec((None, K, F2), lambda i, j: (j, 0, 0))]
        + (rider.in_specs if rider else []),
        out_specs=[pl.BlockSpec((tm, F2), lambda i, j: (i, j)), pl.BlockSpec((tm, F), lambda i, j: (i, j))]
        + (rider.out_specs if rider else []),
        out_shape=[jax.ShapeDtypeStruct((M, J * F2), BF16), jax.ShapeDtypeStruct((M, J * F), BF16)]
        + (rider.out_shape if rider else []),
        scratch_shapes=rider.scratch if rider else [],
        compiler_params=_cp(("arbitrary", "arbitrary") if rider else ("parallel", "parallel")),
    )(xn, wgu, *(rider.shards if rider else []))
    return res[0], res[1], list(res[2:])


def _ffn_down_dx(name, dout, wd, gu, J, *, scale, tm=512):
    M, D = dout.shape
    F = wd.shape[1] // J
    tm = _tile(M, tm)

    def body(d_ref, w_ref, gu_ref, o_ref):
        dh = scale * lax.dot_general(d_ref[...], w_ref[...], (((1,), (1,)), ((), ())), preferred_element_type=F32)
        gu = gu_ref[...].astype(F32)
        _, vjp = jax.vjp(_swiglu_act, gu[:, :F], gu[:, F:])
        o_ref[...] = jnp.concatenate(vjp(dh), axis=-1).astype(o_ref.dtype)

    return pl.pallas_call(
        body, name=name, grid=(M // tm, J),
        in_specs=[pl.BlockSpec((tm, D), lambda i, j: (i, 0)), pl.BlockSpec((None, F, D), lambda i, j: (0, j, 0)),
                  pl.BlockSpec((tm, 2 * F), lambda i, j: (i, j))],
        out_specs=pl.BlockSpec((tm, 2 * F), lambda i, j: (i, j)),
        out_shape=jax.ShapeDtypeStruct((M, J * 2 * F), BF16),
        compiler_params=_cp(("parallel", "parallel")),
    )(dout, wd, gu)


def _all_gather(name, shards):
    n = len(shards)

    def body(*refs):
        start, forward, finish = _gather_phases(refs[:n], refs[n:2 * n], *refs[2 * n:])
        start()
        forward()
        finish()

    any_spec = pl.BlockSpec(memory_space=pl.ANY)
    return pl.pallas_call(
        body, name=name, in_specs=[any_spec] * n, out_specs=[any_spec] * n,
        out_shape=[jax.ShapeDtypeStruct((NDEV,) + s.shape, s.dtype) for s in shards],
        scratch_shapes=_gather_sems(n),
    )(*shards)


def _gather_sems(n):
    return [pltpu.SemaphoreType.DMA((n, 7)), pltpu.SemaphoreType.DMA((n, 7)), pltpu.SemaphoreType.DMA((n,))]


def _gather_phases(ins, outs, send_sems, recv_sems, local_sems):
    n = len(ins)
    x, y, c = lax.axis_index("x"), lax.axis_index("y"), lax.axis_index("c")
    me, sibling = (x, y, c), (x, y, 1 - c)
    chips = [(1 - x, y), (x, 1 - y), (1 - x, 1 - y)]

    def blk(i, px, py, pc):
        return outs[i].at[4 * px + 2 * py + pc]

    def copy(i, k, block, to, src=None):
        return pltpu.make_async_remote_copy(
            src_ref=blk(i, *block) if src is None else src, dst_ref=blk(i, *block),
            send_sem=send_sems.at[i, k], recv_sem=recv_sems.at[i, k], device_id=to, device_id_type=MESH)

    def local(i):
        return pltpu.make_async_copy(ins[i], blk(i, *me), local_sems.at[i])

    def firsts(i):
        return [copy(i, 0, me, sibling, src=ins[i])] + [copy(i, 1 + j, me, (*chip, c), src=ins[i])
                                                        for j, chip in enumerate(chips)]

    def start():
        for i in range(n):
            local(i).start()
        for i in range(n):
            for cp in firsts(i):
                cp.start()

    def forward():
        for i in range(n):
            for j, chip in enumerate(chips):
                copy(i, 1 + j, (*chip, c), me).wait_recv()
                copy(i, 4 + j, (*chip, c), sibling).start()

    def finish():
        for i in range(n):
            copy(i, 0, sibling, me).wait_recv()
            for j, chip in enumerate(chips):
                copy(i, 4 + j, (*chip, 1 - c), me).wait_recv()
        for i in range(n):
            for cp in firsts(i):
                cp.wait_send()
            for j, chip in enumerate(chips):
                copy(i, 4 + j, (*chip, c), sibling).wait_send()
            local(i).wait()

    return start, forward, finish


class _GatherRider:
    def __init__(self, shards):
        self.shards = list(shards)
        self.n = len(self.shards)
        any_spec = pl.BlockSpec(memory_space=pl.ANY)
        self.in_specs = [any_spec] * self.n
        self.out_specs = [any_spec] * self.n
        self.out_shape = [jax.ShapeDtypeStruct((NDEV,) + s.shape, s.dtype) for s in self.shards]
        self.scratch = _gather_sems(self.n)

    def run(self, ins, outs, sems, step, nsteps):
        start, forward, finish = _gather_phases(ins, outs, *sems)
        pl.when(step == 0)(start)
        pl.when(step == nsteps // 2)(forward)
        pl.when(step == nsteps - 1)(finish)


def _swap_add(name, arr, streams, grid, tile, out_shape, out_spec):
    ns = len(streams)
    tr, C = tile
    n = grid[0] * grid[1]
    pos = {a: lax.axis_index(a) for a in AXES}
    sel = jnp.stack([v for a in AXES for v in (pos[a], 1 - pos[a])]).astype(jnp.int32)
    lead = (None,) * (arr.ndim - 2)

    def body(sel_ref, *refs):
        keeps, sends, o_ref = refs[0:2 * ns:2], refs[1:2 * ns:2], refs[2 * ns]
        scratch = refs[2 * ns + 1:]
        lands, stages = scratch[:ns], scratch[ns:2 * ns]
        send_sems, recv_sems, credits = scratch[2 * ns:]
        t = pl.program_id(0) * grid[1] + pl.program_id(1)
        slot = t % 2
        peers = []
        for axis, _, _, _ in streams:
            here = {a: lax.axis_index(a) for a in AXES}
            peers.append(tuple(1 - here[a] if a == axis else here[a] for a in AXES))

        @pl.when(t >= 2)
        def _():
            for s in range(ns):
                pl.semaphore_wait(credits.at[s], 1)

        rdmas = []
        for s in range(ns):
            stages[s][slot] = sends[s][...].astype(stages[s].dtype)
            rdma = pltpu.make_async_remote_copy(
                src_ref=stages[s].at[slot], dst_ref=lands[s].at[slot], send_sem=send_sems.at[s, slot],
                recv_sem=recv_sems.at[s, slot], device_id=peers[s], device_id_type=MESH)
            rdma.start()
            rdmas.append(rdma)
        for s in range(ns):
            rdmas[s].wait_recv()
            total = keeps[s][...] + lands[s][slot].astype(F32)
            if ns == 1:
                o_ref[...] = total
            else:
                o_ref[s] = total
        for rdma in rdmas:
            rdma.wait_send()

        @pl.when(t + 2 < n)
        def _():
            for s in range(ns):
                pl.semaphore_signal(credits.at[s], inc=1, device_id=peers[s], device_id_type=MESH)

    in_specs = []
    for _, _, keep_map, send_map in streams:
        in_specs += [pl.BlockSpec(lead + (tr, C), keep_map), pl.BlockSpec(lead + (tr, C), send_map)]
    wires = [w for _, w, _, _ in streams]
    grid_spec = pltpu.PrefetchScalarGridSpec(
        num_scalar_prefetch=1, grid=grid, in_specs=in_specs, out_specs=out_spec,
        scratch_shapes=[pltpu.VMEM((2, tr, C), w) for w in wires] + [pltpu.VMEM((2, tr, C), w) for w in wires]
        + [pltpu.SemaphoreType.DMA((ns, 2)), pltpu.SemaphoreType.DMA((ns, 2)), pltpu.SemaphoreType.REGULAR((ns,))])
    return pl.pallas_call(
        body, name=name, grid_spec=grid_spec, out_shape=jax.ShapeDtypeStruct(out_shape, F32),
        compiler_params=_cp(("arbitrary", "arbitrary")),
    )(sel, *([arr] * (2 * ns)))


def _reduce_scatter(name, grads):
    outs = []
    for i, g in enumerate(grads):
        _, R, C = g.shape
        R2 = R // 2
        tr = _tile(R2, 256, 16)
        nrh = R2 // tr
        g1 = _swap_add(
            f"{name}{i}_c", g.reshape(4, 2, R, C),
            [("c", BF16, lambda b, i, s: (b, s[4], i, 0), lambda b, i, s: (b, s[5], i, 0))],
            (4, 2 * nrh), (tr, C), (2, 4, R2, C),
            pl.BlockSpec((None, None, tr, C), lambda b, i, s: (i // nrh, b, i % nrh, 0)))
        g2 = _swap_add(
            f"{name}{i}_yx", g1.reshape(2, 2, 2, R2, C),
            [("y", BF16, lambda b, i, s: (0, b, s[2], i, 0), lambda b, i, s: (0, b, s[3], i, 0)),
             ("x", BF16, lambda b, i, s: (1, s[0], b, i, 0), lambda b, i, s: (1, s[1], b, i, 0))],
            (2, nrh), (tr, C), (2, 2, R2, C), pl.BlockSpec((2, None, tr, C), lambda b, i, s: (0, b, i, 0)))
        g3 = _swap_add(
            f"{name}{i}_xy", g2,
            [("x", BF16, lambda b, i, s: (0, s[0], i, 0), lambda b, i, s: (0, s[1], i, 0)),
             ("y", BF16, lambda b, i, s: (1, s[2], i, 0), lambda b, i, s: (1, s[3], i, 0))],
            (1, nrh), (tr, C), (2, R2, C), pl.BlockSpec((2, tr, C), lambda b, i, s: (0, i, 0)))
        outs.append(g3.reshape(R, C))
    return outs


def _sum8(name, g):
    _, R, C = g.shape
    tr = _tile(R, 512)

    def body(g_ref, o_ref):
        acc = g_ref[0]
        for d in range(1, NDEV):
            acc = acc + g_ref[d]
        o_ref[...] = acc

    return pl.pallas_call(
        body, name=name, grid=(R // tr,), in_specs=[pl.BlockSpec((NDEV, tr, C), lambda i: (0, i, 0))],
        out_specs=pl.BlockSpec((tr, C), lambda i: (i, 0)), out_shape=jax.ShapeDtypeStruct((R, C), F32),
        compiler_params=_cp(("parallel",)),
    )(g)


def _adamw_math(w, g, m, v):
    m = B1 * m + (1.0 - B1) * g
    v = B2 * v + (1.0 - B2) * jnp.square(g)
    m_hat = m / (1.0 - B1 ** STEP)
    v_hat = v / (1.0 - B2 ** STEP)
    delta = -LR * (m_hat / (jnp.sqrt(v_hat) + EPS) + WD * w)
    return delta, m, v


def _adamw(name, w, m, v, gp, *, tr, cw, gw, goff=0):
    R, C = w.shape
    nc = C // cw
    nr = R // tr

    def body(w_ref, m_ref, v_ref, g_ref, g_out, d_out, m_out, v_out):
        g = g_ref[...][:, :cw]
        d, mn, vn = _adamw_math(w_ref[...], g, m_ref[...], v_ref[...])
        g_out[...] = g
        d_out[...] = d
        m_out[...] = mn
        v_out[...] = vn

    wspec = pl.BlockSpec((tr, cw), lambda i, j: (i, j))
    gspec = pl.BlockSpec((tr, gw), lambda i, j: (i, goff + j))
    return pl.pallas_call(
        body, name=name, grid=(nr, nc), in_specs=[wspec, wspec, wspec, gspec], out_specs=[wspec] * 4,
        out_shape=[jax.ShapeDtypeStruct((R, C), F32)] * 4, compiler_params=_cp(("parallel", "parallel")),
    )(w, m, v, gp)


def _adamw_small(name, w, m, v, g):
    R, C = w.shape

    def body(w_ref, m_ref, v_ref, g_ref, d_out, m_out, v_out):
        d, mn, vn = _adamw_math(w_ref[...], g_ref[...], m_ref[...], v_ref[...])
        d_out[...] = d
        m_out[...] = mn
        v_out[...] = vn

    tr = _tile(R, 512)
    spec = pl.BlockSpec((tr, C), lambda i: (i, 0))
    return pl.pallas_call(
        body, name=name, grid=(R // tr,), in_specs=[spec] * 4, out_specs=[spec] * 3,
        out_shape=[jax.ShapeDtypeStruct((R, C), F32)] * 3, compiler_params=_cp(("parallel",)),
    )(w, m, v, g)


def _prep(name, parts, rows_p, cols_p):
    R, C = parts[0].shape
    n = len(parts)

    def body(*refs):
        o_ref = refs[n]
        if (R, C) != (rows_p, cols_p):
            o_ref[...] = jnp.zeros_like(o_ref)
        for i in range(n):
            o_ref[0:R, i * cols_p:i * cols_p + C] = refs[i][...].astype(BF16)

    return pl.pallas_call(
        body, name=name, out_shape=jax.ShapeDtypeStruct((rows_p, n * cols_p), BF16), compiler_params=_cp(),
    )(*parts)


def _attn_masks():
    lane = lax.broadcasted_iota(jnp.int32, (1, LANE), 1)
    return [(lane < HEAD_DIM), (lane >= HEAD_DIM)]


def _band_valid(base):
    qi = lax.broadcasted_iota(jnp.int32, (QB, 2 * QB), 0)
    ki = lax.broadcasted_iota(jnp.int32, (QB, 2 * QB), 1)
    dist = qi + QB - ki
    return (dist >= 0) & (dist <= QB) & (base + ki - QB >= 0)


ATTN_T = max(DILATIONS) * QB


def _attn_groups(T):
    out = []
    for d in DILATIONS:
        for r in range(d):
            for i in range(T // (d * QB)):
                qrows = pl.ds(r + d * i * QB, QB, stride=d) if d > 1 else pl.ds(i * QB, QB)
                k0 = T + r + d * (i - 1) * QB
                krows = pl.ds(k0, 2 * QB, stride=d) if d > 1 else pl.ds(k0, 2 * QB)
                out.append((d, qrows, krows, i * QB))
    return out


def _attn_specs(T, width_off):
    cur = pl.BlockSpec((T, LANE), lambda hp, b: (b, width_off + hp))
    prev = pl.BlockSpec((T, LANE), lambda hp, b: (jnp.maximum(b - 1, 0), width_off + hp))
    return cur, prev


def _attn_fwd(z):
    S, ZW = z.shape
    T = min(ATTN_T, S)
    scale = HEAD_DIM ** -0.5
    groups = _attn_groups(T)

    def body(q_ref, kc_ref, kp_ref, vc_ref, vp_ref, y_ref, m_ref, l_ref, kcat, vcat):
        b = pl.program_id(1)
        kcat[0:T, :] = kp_ref[...]
        kcat[T:, :] = kc_ref[...]
        vcat[0:T, :] = vp_ref[...]
        vcat[T:, :] = vc_ref[...]
        masks = _attn_masks()
        for d, qrows, krows, l0 in groups:
            q = q_ref[qrows, :]
            kk = kcat[krows, :].astype(BF16)
            vv = vcat[krows, :].astype(BF16)
            valid = _band_valid(b * (T // d) + l0)
            o_new = m_new = l_new = None
            for hm in masks:
                qh = jnp.where(hm, q, 0.0).astype(BF16)
                s = lax.dot_general(qh, kk, (((1,), (1,)), ((), ())), preferred_element_type=F32) * scale
                s = jnp.where(valid, s, MASK_VALUE)
                m = jnp.max(s, axis=-1, keepdims=True)
                p = jnp.exp(s - m)
                l = jnp.sum(p, axis=-1, keepdims=True)
                o = jnp.dot(p.astype(BF16), vv, preferred_element_type=F32)
                if o_new is None:
                    o_new, m_new, l_new = o, jnp.broadcast_to(m, (QB, LANE)), jnp.broadcast_to(l, (QB, LANE))
                else:
                    o_new = jnp.where(hm, o, o_new)
                    m_new = jnp.where(hm, m, m_new)
                    l_new = jnp.where(hm, l, l_new)
            if d == DILATIONS[0]:
                y_ref[qrows, :] = o_new
                m_ref[qrows, :] = m_new
                l_ref[qrows, :] = l_new
            else:
                m_old = m_ref[qrows, :]
                m_all = jnp.maximum(m_old, m_new)
                w_old, w_new = jnp.exp(m_old - m_all), jnp.exp(m_new - m_all)
                y_ref[qrows, :] = w_old * y_ref[qrows, :] + w_new * o_new
                l_ref[qrows, :] = w_old * l_ref[qrows, :] + w_new * l_new
                m_ref[qrows, :] = m_all
        y_ref[...] = y_ref[...] / l_ref[...]

    qc, _ = _attn_specs(T, 0)
    kc, kp = _attn_specs(T, ATTN_W // LANE)
    vc, vp = _attn_specs(T, 2 * ATTN_W // LANE)
    shp = jax.ShapeDtypeStruct((S, ATTN_W), F32)
    return pl.pallas_call(
        body, name="attn_fwd", grid=(ATTN_W // LANE, S // T),
        in_specs=[qc, kc, kp, vc, vp], out_specs=[qc, qc, qc], out_shape=[shp, shp, shp],
        scratch_shapes=[pltpu.VMEM((2 * T, LANE), F32), pltpu.VMEM((2 * T, LANE), F32)],
        compiler_params=_cp(("parallel", "parallel")),
    )(z, z, z, z, z)


def _attn_bwd(z, dya, ya, mg, den):
    S, ZW = z.shape
    T = min(ATTN_T, S)
    scale = HEAD_DIM ** -0.5
    groups = _attn_groups(T)

    def body(q_ref, kc_ref, kp_ref, vc_ref, vp_ref, dy_ref, y_ref, m_ref, n_ref, dq_ref, dk_ref, dv_ref,
             kcat, vcat, dkcat, dvcat):
        b = pl.program_id(1)

        @pl.when(b == 0)
        def _():
            dk_ref[...] = jnp.zeros_like(dk_ref)
            dv_ref[...] = jnp.zeros_like(dv_ref)

        kcat[0:T, :] = kp_ref[...]
        kcat[T:, :] = kc_ref[...]
        vcat[0:T, :] = vp_ref[...]
        vcat[T:, :] = vc_ref[...]
        dkcat[...] = jnp.zeros_like(dkcat)
        dvcat[...] = jnp.zeros_like(dvcat)
        dq_ref[...] = jnp.zeros_like(dq_ref)
        masks = _attn_masks()
        for d, rows, krows, l0 in groups:
            q, dy, y = q_ref[rows, :], dy_ref[rows, :], y_ref[rows, :]
            mrow, nrow = m_ref[rows, :], n_ref[rows, :]
            kk = kcat[krows, :].astype(BF16)
            vv = vcat[krows, :].astype(BF16)
            valid = _band_valid(b * (T // d) + l0)
            dq_acc = jnp.zeros((QB, LANE), F32)
            dk_acc = jnp.zeros((2 * QB, LANE), F32)
            dv_acc = jnp.zeros((2 * QB, LANE), F32)
            for hm in masks:
                qh = jnp.where(hm, q, 0.0).astype(BF16)
                dyh = jnp.where(hm, dy, 0.0)
                dyb = dyh.astype(BF16)
                dsum = jnp.sum(dyh * y, axis=-1, keepdims=True)
                mh = jnp.max(jnp.where(hm, mrow, MASK_VALUE), axis=-1, keepdims=True)
                nh = jnp.max(jnp.where(hm, nrow, 0.0), axis=-1, keepdims=True)
                s = lax.dot_general(qh, kk, (((1,), (1,)), ((), ())), preferred_element_type=F32) * scale
                p = jnp.where(valid, jnp.exp(s - mh), 0.0) / nh
                pb = p.astype(BF16)
                dv_h = lax.dot_general(pb, dyb, (((0,), (0,)), ((), ())), preferred_element_type=F32)
                dp = lax.dot_general(dyb, vv, (((1,), (1,)), ((), ())), preferred_element_type=F32)
                ds = (p * (dp - dsum) * scale).astype(BF16)
                dq_h = jnp.dot(ds, kk, preferred_element_type=F32)
                dk_h = lax.dot_general(ds, qh, (((0,), (0,)), ((), ())), preferred_element_type=F32)
                dq_acc += jnp.where(hm, dq_h, 0.0)
                dk_acc += dk_h
                dv_acc += dv_h
            dq_ref[rows, :] += dq_acc
            dkcat[krows, :] += dk_acc
            dvcat[krows, :] += dv_acc

        base = pl.multiple_of(b * T, T)
        dk_ref[pl.ds(base, T), :] += dkcat[T:, :]
        dv_ref[pl.ds(base, T), :] += dvcat[T:, :]

        @pl.when(b > 0)
        def _():
            prev = pl.multiple_of(b * T - T, T)
            dk_ref[pl.ds(prev, T), :] += dkcat[0:T, :]
            dv_ref[pl.ds(prev, T), :] += dvcat[0:T, :]

    qc, _ = _attn_specs(T, 0)
    kc, kp = _attn_specs(T, ATTN_W // LANE)
    vc, vp = _attn_specs(T, 2 * ATTN_W // LANE)
    whole = pl.BlockSpec((S, LANE), lambda hp, b: (0, hp))
    shp = jax.ShapeDtypeStruct((S, ATTN_W), F32)
    return pl.pallas_call(
        body, name="attn_bwd", grid=(ATTN_W // LANE, S // T),
        in_specs=[qc, kc, kp, vc, vp, qc, qc, qc, qc], out_specs=[qc, whole, whole], out_shape=[shp, shp, shp],
        scratch_shapes=[pltpu.VMEM((2 * T, LANE), F32)] * 4,
        compiler_params=_cp(("parallel", "arbitrary")),
    )(z, z, z, z, z, dya, ya, mg, den)


def _ssm_disc(lr, li, logdt, br, bi):
    dt = jnp.exp(logdt)
    mag = jnp.exp(lr * dt)
    ar = mag * jnp.cos(li * dt)
    ai = mag * jnp.sin(li * dt)
    nr, ni = ar - 1.0, ai
    den = lr * lr + li * li
    cr = (nr * lr + ni * li) / den
    ci = (ni * lr - nr * li) / den
    return ar, ai, cr * br - ci * bi, cr * bi + ci * br


def _ssm_prep(lr, li, logdt, br, bi):
    n, c = br.shape
    outs, _ = _rowwise("ssm_prep", lambda *a: (list(_ssm_disc(*a)), []), n, _tile(n, 512),
                       [_full(a) for a in (lr, li, logdt, br, bi)], [],
                       [(1, 1, _c0, F32), (1, 1, _c0, F32), (c, c, _c0, F32), (c, c, _c0, F32)])
    return outs


def _ssm_prep_bwd(lr, li, logdt, br, bi, dar, dai, dbbr, dbbi):
    n, c = br.shape

    def f(lrb, lib, dtb, brb, bib, *cts):
        _, vjp = jax.vjp(_ssm_disc, lrb, lib, dtb, brb, bib)
        return list(vjp(cts)), []

    outs, _ = _rowwise("ssm_prep_bwd", f, n, _tile(n, 512),
                       [_full(a) for a in (lr, li, logdt, br, bi, dar, dai, dbbr, dbbi)], [],
                       [(1, 1, _c0, F32)] * 3 + [(c, c, _c0, F32)] * 2)
    return outs


def _cmul(ar, ai, br, bi):
    return ar * br - ai * bi, ar * bi + ai * br


def _scan_consts(ar, ai, reverse):
    w = ar.shape[-1]
    a1 = (jnp.broadcast_to(ar, (8, w)), jnp.broadcast_to(ai, (8, w)))
    a2 = _cmul(*a1, *a1)
    a4 = _cmul(*a2, *a2)
    a8 = _cmul(*a4, *a4)
    row = lax.broadcasted_iota(jnp.int32, (8, w), 0)
    e = (8 - row) if reverse else (row + 1)
    one, zero = jnp.ones((8, w), F32), jnp.zeros((8, w), F32)
    pw = (one, zero)
    for bit, ap in ((1, a1), (2, a2), (4, a4), (8, a8)):
        sel = (e & bit) != 0
        nxt = _cmul(*pw, *ap)
        pw = (jnp.where(sel, nxt[0], pw[0]), jnp.where(sel, nxt[1], pw[1]))
    steps = []
    for sh, (pr, pi) in zip((1, 2, 4), (a1, a2, a4)):
        keep = (row < 8 - sh) if reverse else (row >= sh)
        steps.append((jnp.where(keep, pr, 0.0), jnp.where(keep, pi, 0.0)))
    return steps, pw, row


def _scan_group(xr, xi, cr, ci, consts, reverse):
    steps, pw, _ = consts
    for sh, (pr, pi) in zip((1, 2, 4), steps):
        by = 8 - sh if reverse else sh
        tr_, ti_ = _cmul(pr, pi, pltpu.roll(xr, by, 0), pltpu.roll(xi, by, 0))
        xr = xr + tr_
        xi = xi + ti_
    tr_, ti_ = _cmul(pw[0], pw[1], cr, ci)
    return xr + tr_, xi + ti_


def _ssm_fwd(z, a_r, a_i, bdr, bdi, cmr, cmi, dskip, ts, rider=None):
    S, ZW = z.shape
    NS = SSM_G * SSM_P
    PW = PACK * SSM_P
    uoff = (ZW - SSM_W) // LANE
    nsteps = S // ts
    split = _with_rider(rider, 8, 3, 2)

    def body(*refs):
        refs, r_in, r_out, r_sems = split(refs)
        u_ref, ar_ref, ai_ref, bdr_ref, bdi_ref, cmr_ref, cmi_ref, d_ref, hr_ref, hi_ref, y_ref, car_r, car_i = refs
        s = pl.program_id(1)
        if rider:
            rider.run(r_in, r_out, r_sems, pl.program_id(0) * nsteps + s, (SSM_G // PACK) * nsteps)

        @pl.when(s == 0)
        def _():
            car_r[...] = jnp.zeros_like(car_r)
            car_i[...] = jnp.zeros_like(car_i)

        u = u_ref[...]
        ub = u.astype(BF16)
        nt = (((1,), (1,)), ((), ()))
        hr_ref[...] = lax.dot_general(ub, bdr_ref[...], nt, preferred_element_type=F32)
        hi_ref[...] = lax.dot_general(ub, bdi_ref[...], nt, preferred_element_type=F32)
        consts = _scan_consts(ar_ref[...], ai_ref[...], False)

        def step(j, carry):
            rows = pl.ds(pl.multiple_of(j * 8, 8), 8)
            hr, hi = _scan_group(hr_ref[rows, :], hi_ref[rows, :], carry[0], carry[1], consts, False)
            hr_ref[rows, :] = hr
            hi_ref[rows, :] = hi
            return jnp.broadcast_to(hr[7:8, :], (8, PW)), jnp.broadcast_to(hi[7:8, :], (8, PW))

        cr, ci = lax.fori_loop(0, ts // 8, step, (car_r[...], car_i[...]))
        car_r[...] = cr
        car_i[...] = ci
        y = lax.dot_general(hr_ref[...].astype(BF16), cmr_ref[...], nt, preferred_element_type=F32)
        y -= lax.dot_general(hi_ref[...].astype(BF16), cmi_ref[...], nt, preferred_element_type=F32)
        y_ref[...] = y + d_ref[...] * u

    row_a = pl.BlockSpec((1, PW), lambda i, s: (0, i))
    res = pl.pallas_call(
        body, name="ssm_fwd", grid=(SSM_G // PACK, nsteps),
        in_specs=[pl.BlockSpec((ts, LANE), lambda i, s: (s, uoff + i)), row_a, row_a,
                  pl.BlockSpec((None, PW, LANE), lambda i, s: (i, 0, 0)), pl.BlockSpec((None, PW, LANE), lambda i, s: (i, 0, 0)),
                  pl.BlockSpec((None, LANE, PW), lambda i, s: (i, 0, 0)), pl.BlockSpec((None, LANE, PW), lambda i, s: (i, 0, 0)),
                  pl.BlockSpec((1, LANE), lambda i, s: (0, i))] + (rider.in_specs if rider else []),
        out_specs=[pl.BlockSpec((ts, PW), lambda i, s: (s, i)), pl.BlockSpec((ts, PW), lambda i, s: (s, i)),
                   pl.BlockSpec((ts, LANE), lambda i, s: (s, i))] + (rider.out_specs if rider else []),
        out_shape=[jax.ShapeDtypeStruct((S, NS), F32), jax.ShapeDtypeStruct((S, NS), F32),
                   jax.ShapeDtypeStruct((S, SSM_W), F32)] + (rider.out_shape if rider else []),
        scratch_shapes=[pltpu.VMEM((8, PW), F32), pltpu.VMEM((8, PW), F32)] + (rider.scratch if rider else []),
        compiler_params=_cp(("arbitrary" if rider else "parallel", "arbitrary")),
    )(z, a_r, a_i, bdr, bdi, cmr, cmi, dskip, *(rider.shards if rider else []))
    return res[0], res[1], res[2], list(res[3:])


def _ssm_bwd(z, dyp, hr, hi, a_r, a_i, bdr, bdi, cmr, cmi, dskip, ts):
    S, ZW = z.shape
    NS = SSM_G * SSM_P
    PW = PACK * SSM_P
    uoff = (ZW - SSM_W) // LANE
    nsteps = S // ts
    npk = SSM_G // PACK

    def body(u_ref, dy_ref, hr_ref, hi_ref, hpr_ref, hpi_ref, ar_ref, ai_ref, bdr_ref, bdi_ref, cmr_ref, cmi_ref,
             d_ref, du_ref, dbdr_ref, dbdi_ref, dcmr_ref, dcmi_ref, dar_ref, dai_ref, dd_ref,
             lr_s, li_s, hcr, hci, car_r, car_i):
        s = pl.program_id(1)
        first_tile = s == nsteps - 1

        @pl.when(s == 0)
        def _():
            car_r[...] = jnp.zeros_like(car_r)
            car_i[...] = jnp.zeros_like(car_i)
            for r in (dbdr_ref, dbdi_ref, dcmr_ref, dcmi_ref, dar_ref, dai_ref, dd_ref):
                r[...] = jnp.zeros_like(r)

        u, dy = u_ref[...], dy_ref[...]
        ub, dyb = u.astype(BF16), dy.astype(BF16)
        lr_s[...] = jnp.dot(dyb, cmr_ref[...], preferred_element_type=F32)
        li_s[...] = -jnp.dot(dyb, cmi_ref[...], preferred_element_type=F32)
        keep_prev = jnp.where(first_tile, 0.0, 1.0)
        hcr[0:8, :] = hpr_ref[...] * keep_prev
        hci[0:8, :] = hpi_ref[...] * keep_prev
        hcr[8:, :] = hr_ref[...]
        hci[8:, :] = hi_ref[...]
        consts = _scan_consts(ar_ref[...], -ai_ref[...], True)
        row = consts[2]
        ngrp = ts // 8

        def step(jj, carry):
            cr, ci, accr, acci = carry
            j = ngrp - 1 - jj
            rows = pl.ds(pl.multiple_of(j * 8, 8), 8)
            nxt = pl.ds(pl.multiple_of(j * 8 + 8, 8), 8)
            lr, li = _scan_group(lr_s[rows, :], li_s[rows, :], cr, ci, consts, True)
            lr_s[rows, :] = lr
            li_s[rows, :] = li
            pr, pi = hcr[rows, :], hci[rows, :]
            hsr = jnp.where(row == 0, jnp.broadcast_to(pr[7:8, :], (8, PW)), pltpu.roll(hcr[nxt, :], 1, 0))
            hsi = jnp.where(row == 0, jnp.broadcast_to(pi[7:8, :], (8, PW)), pltpu.roll(hci[nxt, :], 1, 0))
            accr = accr + lr * hsr + li * hsi
            acci = acci + li * hsr - lr * hsi
            return jnp.broadcast_to(lr[0:1, :], (8, PW)), jnp.broadcast_to(li[0:1, :], (8, PW)), accr, acci

        zero = jnp.zeros((8, PW), F32)
        cr, ci, accr, acci = lax.fori_loop(0, ngrp, step, (car_r[...], car_i[...], zero, zero))
        car_r[...] = cr
        car_i[...] = ci
        dar_ref[...] += jnp.sum(accr, axis=0, keepdims=True)
        dai_ref[...] += jnp.sum(acci, axis=0, keepdims=True)
        lrb, lib = lr_s[...].astype(BF16), li_s[...].astype(BF16)
        du = jnp.dot(lrb, bdr_ref[...], preferred_element_type=F32)
        du += jnp.dot(lib, bdi_ref[...], preferred_element_type=F32)
        du_ref[...] = du + dy * d_ref[...]
        tn = (((0,), (0,)), ((), ()))
        dbdr_ref[...] += lax.dot_general(lrb, ub, tn, preferred_element_type=F32)
        dbdi_ref[...] += lax.dot_general(lib, ub, tn, preferred_element_type=F32)
        dcmr_ref[...] += lax.dot_general(dyb, hr_ref[...].astype(BF16), tn, preferred_element_type=F32)
        dcmi_ref[...] -= lax.dot_general(dyb, hi_ref[...].astype(BF16), tn, preferred_element_type=F32)
        dd_ref[...] += jnp.sum(dy * u, axis=0, keepdims=True)

    rev = lambda s: nsteps - 1 - s
    row_a = pl.BlockSpec((1, PW), lambda i, s: (0, i))
    tile = pl.BlockSpec((ts, PW), lambda i, s: (rev(s), i))
    prev8 = pl.BlockSpec((8, PW), lambda i, s: (jnp.maximum(rev(s) * (ts // 8) - 1, 0), i))
    cols = pl.BlockSpec((ts, LANE), lambda i, s: (rev(s), i))
    bd = pl.BlockSpec((None, PW, LANE), lambda i, s: (i, 0, 0))
    cm = pl.BlockSpec((None, LANE, PW), lambda i, s: (i, 0, 0))
    return pl.pallas_call(
        body, name="ssm_bwd", grid=(npk, nsteps),
        in_specs=[pl.BlockSpec((ts, LANE), lambda i, s: (rev(s), uoff + i)), cols, tile, tile, prev8, prev8,
                  row_a, row_a, bd, bd, cm, cm, pl.BlockSpec((1, LANE), lambda i, s: (0, i))],
        out_specs=[cols, bd, bd, cm, cm, row_a, row_a, pl.BlockSpec((1, LANE), lambda i, s: (0, i))],
        out_shape=[jax.ShapeDtypeStruct((S, SSM_W), F32),
                   jax.ShapeDtypeStruct((npk, PW, LANE), F32), jax.ShapeDtypeStruct((npk, PW, LANE), F32),
                   jax.ShapeDtypeStruct((npk, LANE, PW), F32), jax.ShapeDtypeStruct((npk, LANE, PW), F32),
                   jax.ShapeDtypeStruct((1, NS), F32), jax.ShapeDtypeStruct((1, NS), F32),
                   jax.ShapeDtypeStruct((1, SSM_W), F32)],
        scratch_shapes=[pltpu.VMEM((ts, PW), F32), pltpu.VMEM((ts, PW), F32),
                        pltpu.VMEM((ts + 8, PW), F32), pltpu.VMEM((ts + 8, PW), F32),
                        pltpu.VMEM((8, PW), F32), pltpu.VMEM((8, PW), F32)],
        compiler_params=_cp(("parallel", "arbitrary")),
    )(z, dyp, hr, hi, hr, hi, a_r, a_i, bdr, bdi, cmr, cmi, dskip)


def _block_diag(m4):
    npk, g, a, b = m4.shape
    eye = jnp.eye(g, dtype=m4.dtype)
    return (m4[:, :, :, None, :] * eye[None, :, None, :, None]).reshape(npk, g * a, g * b)


def _block_diag_take(m, a, b):
    npk = m.shape[0]
    m5 = m.reshape(npk, PACK, a, PACK, b)
    return jnp.stack([m5[:, g, :, g, :] for g in range(PACK)], axis=1)


def _mix_out(ya, ypre, gl, ga, gb, bglu):
    yg = jax.nn.gelu(ypre)
    yb = yg * jax.nn.sigmoid(gl + bglu)
    return jnp.concatenate([_rms(ya, ga), _rms(yb, gb)], axis=-1)


def _tail_loss(h3, gl, pe, gf, tgt):
    h4 = h3 + jax.nn.sigmoid(gl) * pe
    err = jnp.square(_rms(h4, gf) - tgt)
    return 0.5 * jnp.mean(err, axis=-1, keepdims=True)


def kernel(x, p, ffn1_norm, ffn1_w_gate, ffn1_w_up, ffn1_w_down, mix_norm, w_in, attn_out_norm, ssm_lambda_re, ssm_lambda_im, ssm_log_dt, ssm_b_re, ssm_b_im, ssm_c_re, ssm_c_im, ssm_d, ssm_w_glu, ssm_b_glu, ssm_out_norm, w_out, ffn2_norm, ffn2_w_gate, ffn2_w_up, ffn2_w_down, ple_norm, ple_w_gate, ple_w_proj, final_norm, loss_target, m_ffn1_norm, m_ffn1_w_gate, m_ffn1_w_up, m_ffn1_w_down, m_mix_norm, m_w_in, m_attn_out_norm, m_ssm_lambda_re, m_ssm_lambda_im, m_ssm_log_dt, m_ssm_b_re, m_ssm_b_im, m_ssm_c_re, m_ssm_c_im, m_ssm_d, m_ssm_w_glu, m_ssm_b_glu, m_ssm_out_norm, m_w_out, m_ffn2_norm, m_ffn2_w_gate, m_ffn2_w_up, m_ffn2_w_down, m_ple_norm, m_ple_w_gate, m_ple_w_proj, m_final_norm, v_ffn1_norm, v_ffn1_w_gate, v_ffn1_w_up, v_ffn1_w_down, v_mix_norm, v_w_in, v_attn_out_norm, v_ssm_lambda_re, v_ssm_lambda_im, v_ssm_log_dt, v_ssm_b_re, v_ssm_b_im, v_ssm_c_re, v_ssm_c_im, v_ssm_d, v_ssm_w_glu, v_ssm_b_glu, v_ssm_out_norm, v_w_out, v_ffn2_norm, v_ffn2_w_gate, v_ffn2_w_up, v_ffn2_w_down, v_ple_norm, v_ple_w_gate, v_ple_w_proj, v_final_norm):
    A = dict(locals())
    xs = x[0]
    ps = p[0, 0]
    tgt = loss_target[0]
    S, D = xs.shape
    FSH = ffn1_w_gate.shape[-1]
    FSP = -(-FSH // LANE) * LANE
    TR = _tile(S, 256)
    ZW = 3 * ATTN_W + SSM_W

    wgu1 = _prep("prep_gu1", [ffn1_w_gate[0], ffn1_w_up[0]], D, FSP)
    wgu2 = _prep("prep_gu2", [ffn2_w_gate[0], ffn2_w_up[0]], D, FSP)
    wd1 = _prep("prep_d1", [ffn1_w_down[0]], FSP, D)
    wd2 = _prep("prep_d2", [ffn2_w_down[0]], FSP, D)
    win = _prep("prep_in", [w_in[0]], D, w_in.shape[-1])
    wglu = _prep("prep_glu", [ssm_w_glu[0]], ssm_w_glu.shape[1], SSM_W)
    wout = _prep("prep_out", [w_out[0]], w_out.shape[1], D)
    wpg = _prep("prep_pg", [ple_w_gate[0]], ple_w_gate.shape[1], D)
    wpp = _prep("prep_pp", [ple_w_proj[0]], ple_w_proj.shape[1], ple_w_proj.shape[2])
    (Wgu1,) = _all_gather("ag_weights", [wgu1])
    rowstack = lambda w: w.reshape(1, w.shape[0] * w.shape[1], w.shape[2])

    def ffn_norm(tag, h, gain):
        return _rowwise(f"{tag}_norm", lambda a, g: ([_rms(a, g)], []), S, TR, [_full(h)], [gain], [(D, D, _c0, BF16)])[0][0]

    xn1 = ffn_norm("ffn1", xs, ffn1_norm)
    gu1, hid1, (Wd1, Win, Wglu, Wout) = _ffn_up("ffn1_up", xn1, Wgu1, rider=_GatherRider([wd1, win, wglu, wout]))
    Wd1, Wglu, Wout = rowstack(Wd1), rowstack(Wglu), rowstack(Wout)
    h1 = _mm_nn("ffn1_down", hid1, Wd1, tn=D, tk=2 * FSP, res=xs, scale=0.5)

    (un,), _ = _rowwise("mix_norm", lambda a, g: ([_rms(a, g)], []), S, TR, [_full(h1)], [mix_norm], [(D, D, _c0, BF16)])
    z, (Wgu2,) = _mm_nn("mix_in", un, Win, tn=512, tk=D, rider=_GatherRider([wgu2]))
    ya, mg, den = _attn_fwd(z)

    col = lambda a: a.reshape(-1, 1)
    lr_c, li_c = col(ssm_lambda_re), col(ssm_lambda_im)
    dt_c = col(jnp.broadcast_to(ssm_log_dt.reshape(SSM_G, 1), (SSM_G, SSM_P)))
    b_re2, b_im2 = ssm_b_re.reshape(-1, SSM_C), ssm_b_im.reshape(-1, SSM_C)
    ar_c, ai_c, bbr, bbi = _ssm_prep(lr_c, li_c, dt_c, b_re2, b_im2)
    a_r, a_i = ar_c.reshape(1, -1), ai_c.reshape(1, -1)
    npk = SSM_G // PACK
    bdr = _block_diag(bbr.reshape(npk, PACK, SSM_P, SSM_C)).astype(BF16)
    bdi = _block_diag(bbi.reshape(npk, PACK, SSM_P, SSM_C)).astype(BF16)
    cmr = _block_diag(ssm_c_re.reshape(npk, PACK, SSM_C, SSM_P)).astype(BF16)
    cmi = _block_diag(ssm_c_im.reshape(npk, PACK, SSM_C, SSM_P)).astype(BF16)
    TS = _tile(S, 512)
    hr, hi, ypre, (Wd2, Wpg, Wpp) = _ssm_fwd(z, a_r, a_i, bdr, bdi, cmr, cmi, ssm_d, TS,
                                             rider=_GatherRider([wd2, wpg, wpp]))
    Wd2, Wpg = rowstack(Wd2), rowstack(Wpg)
    (yg,), _ = _rowwise("ssm_gelu", lambda a: ([jax.nn.gelu(a)], []), S, TR, [_full(ypre)], [], [(SSM_W, SSM_W, _c0, BF16)])
    gl = _mm_nn("ssm_glu", yg, Wglu, tn=SSM_W, tk=SSM_W)
    (ycat,), _ = _rowwise("mix_out", lambda *a: ([_mix_out(*a)], []), S, TR, [_full(ya), _full(ypre), _full(gl)],
                          [attn_out_norm, ssm_out_norm, ssm_b_glu], [(MIX_W, MIX_W, _c0, BF16)])
    h2 = _mm_nn("mix_proj", ycat, Wout, tn=D // 2, tk=D, res=h1, scale=1.0)

    xn2 = ffn_norm("ffn2", h2, ffn2_norm)
    gu2, hid2, _ = _ffn_up("ffn2_up", xn2, Wgu2)
    h3 = _mm_nn("ffn2_down", hid2, Wd2, tn=D, tk=2 * FSP, res=h2, scale=0.5)

    (hn, pb), _ = _rowwise("ple_norm", lambda a, q, g: ([_rms(a, g), q], []), S, TR, [_full(h3), _full(ps)], [ple_norm],
                           [(D, D, _c0, BF16), (ps.shape[1], ps.shape[1], _c0, BF16)])
    pgl = _mm_nn("ple_gate", hn, Wpg, tn=D // 2, tk=D)
    pe = _mm_nn("ple_proj", pb, Wpp, tn=Wpp.shape[2], tk=Wpp.shape[1])

    def tail(h3b, glb, peb, tb, gf):
        rows, vjp = jax.vjp(lambda a, b, c, g: _tail_loss(a, b, c, g, tb), h3b, glb, peb, gf)
        dh, dgl, dpe, dgf = vjp(jnp.ones_like(rows))
        return [dh, dgl, dpe], [jnp.broadcast_to(jnp.sum(rows, axis=0, keepdims=True), (1, LANE)), dgf]

    (dh3_dir, dpgl, dpe), (loss_row, g_final) = _rowwise(
        "tail", tail, S, TR, [_full(h3), _full(pgl), _full(pe), _full(tgt)], [final_norm.reshape(1, D)],
        [(D, D, _c0, F32), (D, D, _c0, BF16), (D, D, _c0, BF16)], [(LANE, LANE, _c0), (D, D, _c0)])
    loss = lax.psum(loss_row[0, 0], AXES)

    def norm_bwd(tag, h, gain, dn, dres):
        def f(hb, dnb, drb, g):
            _, vjp = jax.vjp(_rms, hb, g)
            dh, dg = vjp(dnb)
            dh = dh + drb
            return [dh, dh], [dg]
        (dh, dhb), (dg,) = _rowwise(f"{tag}_norm_bwd", f, S, TR, [_full(h), _full(dn), _full(dres)], [gain],
                                    [(D, D, _c0, F32), (D, D, _c0, BF16)], [(D, D, _c0)])
        return dh, dhb, dg

    dhn = _mm_nt("ple_gate_dx", dpgl, Wpg, tn=D, tk=D)
    g_wpg = _mm_tn("ple_gate_dw", hn, dpgl, 1)
    g_wpp = _mm_tn("ple_proj_dw", pb, dpe, NDEV)
    dh3, dh3b, g_ple_norm = norm_bwd("ple", h3, ple_norm, dhn, dh3_dir)

    def ffn_bwd(tag, h, gain, Wgu, Wd, saved, dout, doutb):
        xn, gu, hid = saved
        dgu = _ffn_down_dx(f"{tag}_down_dx", doutb, Wd, gu, NDEV, scale=0.5)
        g_wd = _mm_tn(f"{tag}_down_dw", hid, doutb, 1, tm=2048, tko=FSP, tn=D // 2, scale=0.5)
        dxn = _mm_nt(f"{tag}_up_dx", dgu, Wgu, tm=1024, tn=D, tk=2 * FSP)
        g_wgu = _mm_tn(f"{tag}_up_dw", xn, dgu, NDEV, tm=2048, tn=FSP)
        dh, dhb, g_norm = norm_bwd(tag, h, gain, dxn, dout)
        return dh, dhb, g_norm, g_wgu, g_wd

    dh2, dh2b, g_ffn2_norm, g_wgu2, g_wd2 = ffn_bwd("ffn2", h2, ffn2_norm, Wgu2, Wd2, (xn2, gu2, hid2), dh3, dh3b)

    dycat = _mm_nt("mix_proj_dx", dh2b, Wout, tn=D, tk=D)
    g_wout = _mm_tn("mix_proj_dw", ycat, dh2b, 1)

    def mix_out_bwd(yab, ypb, glb, dyc, ga, gb, bglu):
        _, vjp = jax.vjp(_mix_out, yab, ypb, glb, ga, gb, bglu)
        dya_, dyp_, dgl_, dga, dgb, dbg = vjp(dyc)
        return [dya_, dyp_, dgl_], [dga, dgb, dbg]
    (dya, dyp_dir, dglb), (g_attn_norm, g_ssm_norm, g_bglu) = _rowwise(
        "mix_out_bwd", mix_out_bwd, S, TR, [_full(ya), _full(ypre), _full(gl), _full(dycat)],
        [attn_out_norm, ssm_out_norm, ssm_b_glu],
        [(ATTN_W, ATTN_W, _c0, F32), (SSM_W, SSM_W, _c0, F32), (SSM_W, SSM_W, _c0, BF16)],
        [(ATTN_W, ATTN_W, _c0), (SSM_W, SSM_W, _c0), (SSM_W, SSM_W, _c0)])
    dyg = _mm_nt("ssm_glu_dx", dglb, Wglu, tn=SSM_W, tk=SSM_W)
    g_wglu = _mm_tn("ssm_glu_dw", yg, dglb, 1)

    def gelu_bwd(ypb, dygb, ddir):
        _, vjp = jax.vjp(jax.nn.gelu, ypb)
        return [ddir + vjp(dygb)[0]], []
    (dyp,), _ = _rowwise("ssm_gelu_bwd", gelu_bwd, S, TR, [_full(ypre), _full(dyg), _full(dyp_dir)], [],
                         [(SSM_W, SSM_W, _c0, F32)])
    du, dbdr, dbdi, dcmr, dcmi, da_r, da_i, g_ssm_d = _ssm_bwd(z, dyp, hr, hi, a_r, a_i, bdr, bdi, cmr, cmi, ssm_d, TS)
    dbbr = _block_diag_take(dbdr, SSM_P, SSM_C).reshape(-1, SSM_C)
    dbbi = _block_diag_take(dbdi, SSM_P, SSM_C).reshape(-1, SSM_C)
    g_c_re = _block_diag_take(dcmr, SSM_C, SSM_P).reshape(ssm_c_re.shape)
    g_c_im = _block_diag_take(dcmi, SSM_C, SSM_P).reshape(ssm_c_im.shape)
    dlr, dli, ddt, g_b_re, g_b_im = _ssm_prep_bwd(lr_c, li_c, dt_c, b_re2, b_im2, col(da_r), col(da_i), dbbr, dbbi)
    g_lam_re, g_lam_im = dlr.reshape(ssm_lambda_re.shape), dli.reshape(ssm_lambda_im.shape)
    g_log_dt = jnp.sum(ddt.reshape(SSM_G, SSM_P), axis=1).reshape(ssm_log_dt.shape)
    g_b_re, g_b_im = g_b_re.reshape(ssm_b_re.shape), g_b_im.reshape(ssm_b_im.shape)

    dq, dk, dv = _attn_bwd(z, dya, ya, mg, den)
    (dz,), _ = _rowwise("mix_dz", lambda *a: ([jnp.concatenate(a, axis=-1)], []), S, TR,
                        [_full(dq), _full(dk), _full(dv), _full(du)], [], [(ZW, ZW, _c0, BF16)])
    dun = _mm_nt("mix_in_dx", dz, Win, tn=D, tk=512)
    g_win = _mm_tn("mix_in_dw", un, dz, NDEV, tn=512)
    dh1, dh1b, g_mix_norm = norm_bwd("mix", h1, mix_norm, dun, dh2)

    dx, _dxb, g_ffn1_norm, g_wgu1, g_wd1 = ffn_bwd("ffn1", xs, ffn1_norm, Wgu1, Wd1, (xn1, gu1, hid1), dh1, dh1b)

    restack = lambda g: g.reshape((NDEV, g.shape[1] // NDEV) + g.shape[2:])
    big = [g_wgu1, restack(g_wd1), g_win, restack(g_wglu), restack(g_wout), g_wgu2, restack(g_wd2), restack(g_wpg), g_wpp]
    mine = _reduce_scatter("rs", big)
    out = {}

    def upd(name, idx, *, tr, cw, gw, goff=0):
        w, m, v = A[name][0], A["m_" + name][0], A["v_" + name][0]
        g, dlt, mn, vn = _adamw("adamw_" + name, w, m, v, mine[idx], tr=tr, cw=cw, gw=gw, goff=goff)
        for k, val in (("grad_", g), ("delta_", dlt), ("new_m_", mn), ("new_v_", vn)):
            out[k + name] = val[None]

    DT = _tile(D, 256)
    FT = _tile(FSH, 512)
    DC = _tile(D, 1024, LANE)
    upd("ffn1_w_gate", 0, tr=DT, cw=FSH, gw=FSP, goff=0)
    upd("ffn1_w_up", 0, tr=DT, cw=FSH, gw=FSP, goff=1)
    upd("ffn1_w_down", 1, tr=FT, cw=DC, gw=DC)
    upd("w_in", 2, tr=DT, cw=w_in.shape[-1], gw=w_in.shape[-1])
    upd("ssm_w_glu", 3, tr=ssm_w_glu.shape[1], cw=SSM_W, gw=SSM_W)
    upd("w_out", 4, tr=w_out.shape[1], cw=DC, gw=DC)
    upd("ffn2_w_gate", 5, tr=DT, cw=FSH, gw=FSP, goff=0)
    upd("ffn2_w_up", 5, tr=DT, cw=FSH, gw=FSP, goff=1)
    upd("ffn2_w_down", 6, tr=FT, cw=DC, gw=DC)
    upd("ple_w_gate", 7, tr=ple_w_gate.shape[1], cw=DC, gw=DC)
    upd("ple_w_proj", 8, tr=ple_w_proj.shape[1], cw=ple_w_proj.shape[2], gw=ple_w_proj.shape[2])

    small = [("ffn1_norm", g_ffn1_norm), ("mix_norm", g_mix_norm), ("attn_out_norm", g_attn_norm),
             ("ssm_lambda_re", g_lam_re), ("ssm_lambda_im", g_lam_im), ("ssm_log_dt", g_log_dt),
             ("ssm_b_re", g_b_re), ("ssm_b_im", g_b_im), ("ssm_c_re", g_c_re), ("ssm_c_im", g_c_im),
             ("ssm_d", g_ssm_d), ("ssm_b_glu", g_bglu), ("ssm_out_norm", g_ssm_norm), ("ffn2_norm", g_ffn2_norm),
             ("ple_norm", g_ple_norm), ("final_norm", g_final)]
    chunk = 8 * LANE

    def pack(arrs):
        parts = []
        for a in arrs:
            flat = a.reshape(-1)
            padn = -(-flat.shape[0] // chunk) * chunk
            parts.append(jnp.pad(flat, (0, padn - flat.shape[0])).reshape(-1, LANE))
        return jnp.concatenate(parts, axis=0)

    g_pack = pack([g for _, g in small])
    (g_all,) = _all_gather("ag_small", [g_pack])
    g_sum = _sum8("small_sum", g_all)
    w_pack = pack([A[n] for n, _ in small])
    m_pack = pack([A["m_" + n] for n, _ in small])
    v_pack = pack([A["v_" + n] for n, _ in small])
    d_pack, mn_pack, vn_pack = _adamw_small("adamw_small", w_pack, m_pack, v_pack, g_sum)
    off = 0
    for n, _ in small:
        shape = A[n].shape
        size = math.prod(shape)
        rows = -(-size // chunk) * 8
        for k, buf in (("grad_", g_sum), ("delta_", d_pack), ("new_m_", mn_pack), ("new_v_", vn_pack)):
            out[k + n] = buf[off:off + rows].reshape(-1)[:size].reshape(shape)
        off += rows

    names = ['ffn1_norm', 'ffn1_w_gate', 'ffn1_w_up', 'ffn1_w_down', 'mix_norm', 'w_in', 'attn_out_norm',
             'ssm_lambda_re', 'ssm_lambda_im', 'ssm_log_dt', 'ssm_b_re', 'ssm_b_im', 'ssm_c_re', 'ssm_c_im', 'ssm_d',
             'ssm_w_glu', 'ssm_b_glu', 'ssm_out_norm', 'w_out', 'ffn2_norm', 'ffn2_w_gate', 'ffn2_w_up', 'ffn2_w_down',
             'ple_norm', 'ple_w_gate', 'ple_w_proj', 'final_norm']
    return (loss, dx[None], *[out[k + n] for k in ("grad_", "delta_", "new_m_", "new_v_") for n in names])
```

```python
import functools
import math

import jax
import jax.numpy as jnp
from jax import lax
from jax.experimental import pallas as pl
from jax.experimental.pallas import tpu as pltpu

F32, BF16 = jnp.float32, jnp.bfloat16
MESH = pl.DeviceIdType.MESH
NDEV = 8
AXES = ("x", "y", "c")
LANE = 128
VMEM_LIMIT = 56 * 1024 * 1024

ATTN_W = 1024
HEAD_DIM = 64
SSM_W = 1024
MIX_W = ATTN_W + SSM_W
SSM_G, SSM_P, SSM_C = 64, 64, 16
PACK = 8
DILATIONS = (1, 4, 16)
QB = 128
NORM_EPS = 1e-6
MASK_VALUE = -1e30
LR, B1, B2, EPS, WD, STEP = 0.001, 0.9, 0.999, 1e-08, 0.01, 10


def _cp(sem=None):
    return pltpu.CompilerParams(dimension_semantics=sem, vmem_limit_bytes=VMEM_LIMIT)


def _tile(n, target, mult=8):
    if n <= target:
        return n
    for t in range(target - target % mult, 0, -mult):
        if n % t == 0:
            return t
    return n


def _rms(x, g):
    return x * lax.rsqrt(jnp.mean(x * x, axis=-1, keepdims=True) + NORM_EPS) * g


def _rowwise(name, fn, S, tr, rows, fulls, outs, accs=(), ncol=1):
    nr, nf, no, na = len(rows), len(fulls), len(outs), len(accs)

    def body(*refs):
        ins = [r[...] for r in refs[:nr + nf]]
        o_refs = refs[nr + nf:nr + nf + no]
        a_refs = refs[nr + nf + no:]
        o_vals, a_vals = fn(*ins)
        for r, v in zip(o_refs, o_vals):
            r[...] = v.astype(r.dtype)
        if na:
            @pl.when(pl.program_id(1) == 0)
            def _():
                for r in a_refs:
                    r[...] = jnp.zeros_like(r)
            for r, v in zip(a_refs, a_vals):
                r[...] += v

    in_specs = [pl.BlockSpec((tr, w), functools.partial(lambda j, i, cm: (i, cm(j)), cm=cm)) for _, w, cm in rows]
    in_specs += [pl.BlockSpec(f.shape, functools.partial(lambda j, i, nd: (0,) * nd, nd=f.ndim)) for f in fulls]
    out_specs = [pl.BlockSpec((tr, w), functools.partial(lambda j, i, cm: (i, cm(j)), cm=cm)) for _, w, cm, _ in outs]
    out_specs += [pl.BlockSpec((1, w), functools.partial(lambda j, i, cm: (0, cm(j)), cm=cm)) for _, w, cm in accs]
    out_shape = [jax.ShapeDtypeStruct((S, c), dt) for c, _, _, dt in outs]
    out_shape += [jax.ShapeDtypeStruct((1, c), F32) for c, _, _ in accs]
    res = pl.pallas_call(
        body, name=name, grid=(ncol, S // tr), in_specs=in_specs, out_specs=out_specs, out_shape=out_shape,
        compiler_params=_cp(("parallel", "arbitrary" if na else "parallel")),
    )(*[a for a, _, _ in rows], *fulls)
    return res[:no], res[no:]


def _c0(j):
    return 0


def _full(a):
    return (a, a.shape[1], _c0)


def _hosted(name, body, grid, in_specs, out_specs, out_shape, scratch, args, sem, rider=None, jobs=None):
    nsteps = math.prod(grid)

    def step_of(*g):
        t = 0
        for gi, n in zip(g, grid):
            t = t * n + gi
        return t

    job = None
    if rider is None:
        job, rider = _pick(jobs, nsteps)
    if rider is None:
        outs = pl.pallas_call(body, name=name, grid=grid, in_specs=in_specs, out_specs=out_specs, out_shape=out_shape,
                              scratch_shapes=scratch, compiler_params=_cp(sem))(*args)
        return outs, None
    rider.bind(step_of, nsteps)
    n_in, n_out, n_scr = len(in_specs), len(out_specs), len(scratch)

    def full(*refs):
        a, b = n_in, n_in + rider.n_in
        c, d = b + n_out, b + n_out + rider.n_out
        rider.run(refs[a:b], refs[c:d], refs[d + n_scr:], step_of(*[pl.program_id(i) for i in range(len(grid))]), nsteps)
        body(*(refs[:a] + refs[b:c] + refs[d:d + n_scr]))

    outs = pl.pallas_call(
        full, name=name, grid=grid, in_specs=in_specs + rider.in_specs, out_specs=out_specs + rider.out_specs,
        out_shape=out_shape + rider.out_shape, scratch_shapes=scratch + rider.scratch,
        compiler_params=_cp(("arbitrary",) * len(grid)))(*args, *rider.operands)
    extra = rider.take(outs[n_out:])
    if job is not None:
        job.advance(extra)
        extra = None
    return outs[:n_out], extra


def _mm_nn(name, a, w, *, out_dtype=F32, tm=512, tn=768, tk=2048, res=None, scale=1.0, rider=None, jobs=None):
    M, K = a.shape
    J, K2, Np = w.shape
    assert K == K2
    tm, tn, tk = _tile(M, tm), _tile(Np, tn, LANE), _tile(K, tk, LANE)
    npj = Np // tn
    nk = K // tk
    grid = (M // tm, J * npj, nk)

    def body(*refs):
        if res is None:
            a_ref, w_ref, o_ref, acc = refs
        else:
            a_ref, w_ref, r_ref, o_ref, acc = refs
        k = pl.program_id(2)
        part = jnp.dot(a_ref[...].astype(BF16), w_ref[...], preferred_element_type=F32)

        def finish(v):
            if res is not None:
                v = r_ref[...] + scale * v
            o_ref[...] = v.astype(o_ref.dtype)

        if nk == 1:
            finish(part)
            return

        @pl.when(k == 0)
        def _():
            acc[...] = part

        @pl.when(k > 0)
        def _():
            acc[...] += part

        @pl.when(k == nk - 1)
        def _():
            finish(acc[...])

    in_specs = [pl.BlockSpec((tm, tk), lambda i, n, k: (i, k)),
                pl.BlockSpec((None, tk, tn), lambda i, n, k: (n // npj, k, n % npj))]
    args = [a, w]
    if res is not None:
        in_specs.append(pl.BlockSpec((tm, tn), lambda i, n, k: (i, n)))
        args.append(res)
    (out,), extra = _hosted(
        name, body, grid, in_specs, [pl.BlockSpec((tm, tn), lambda i, n, k: (i, n))],
        [jax.ShapeDtypeStruct((M, J * Np), out_dtype)], [pltpu.VMEM((tm, tn), F32)], args,
        ("parallel", "parallel", "arbitrary"), rider, jobs)
    return out if rider is None else (out, extra)


def _mm_nt(name, dy, w, *, out_dtype=F32, tm=512, tn=2048, tk=768, scale=1.0, jobs=None):
    M, N = dy.shape
    J, K, Np = w.shape
    assert N == J * Np
    tm, tn, tk = _tile(M, tm), _tile(K, tn, LANE), _tile(Np, tk, LANE)
    npj = Np // tk
    nc = J * npj

    def body(a_ref, w_ref, o_ref, acc):
        c = pl.program_id(2)
        part = lax.dot_general(a_ref[...].astype(BF16), w_ref[...], (((1,), (1,)), ((), ())),
                               preferred_element_type=F32)
        if nc == 1:
            o_ref[...] = (scale * part).astype(o_ref.dtype)
            return

        @pl.when(c == 0)
        def _():
            acc[...] = part

        @pl.when(c > 0)
        def _():
            acc[...] += part

        @pl.when(c == nc - 1)
        def _():
            o_ref[...] = (scale * acc[...]).astype(o_ref.dtype)

    (out,), _ = _hosted(
        name, body, (M // tm, K // tn, nc),
        [pl.BlockSpec((tm, tk), lambda i, n, c: (i, c)),
         pl.BlockSpec((None, tn, tk), lambda i, n, c: (c // npj, n, c % npj))],
        [pl.BlockSpec((tm, tn), lambda i, n, c: (i, n))], [jax.ShapeDtypeStruct((M, K), out_dtype)],
        [pltpu.VMEM((tm, tn), F32)], (dy, w), ("parallel", "parallel", "arbitrary"), None, jobs)
    return out


def _mm_tn(name, x, dy, J, *, tm=1024, tko=1024, tn=768, scale=1.0, jobs=None):
    M, K = x.shape
    M2, N = dy.shape
    assert M == M2 and N % J == 0
    Np = N // J
    tm, tko, tn = _tile(M, tm, LANE), _tile(K, tko, LANE), _tile(Np, tn, LANE)
    npj = Np // tn
    nm = M // tm

    def body(x_ref, d_ref, o_ref, acc):
        m = pl.program_id(2)
        part = lax.dot_general(x_ref[...].astype(BF16), d_ref[...].astype(BF16), (((0,), (0,)), ((), ())),
                               preferred_element_type=F32)
        if nm == 1:
            o_ref[...] = scale * part
            return

        @pl.when(m == 0)
        def _():
            acc[...] = part

        @pl.when(m > 0)
        def _():
            acc[...] += part

        @pl.when(m == nm - 1)
        def _():
            o_ref[...] = scale * acc[...]

    (out,), _ = _hosted(
        name, body, (K // tko, J * npj, nm),
        [pl.BlockSpec((tm, tko), lambda k, n, m: (m, k)), pl.BlockSpec((tm, tn), lambda k, n, m: (m, n))],
        [pl.BlockSpec((None, tko, tn), lambda k, n, m: (n // npj, k, n % npj))],
        [jax.ShapeDtypeStruct((J, K, Np), F32)], [pltpu.VMEM((tko, tn), F32)], (x, dy),
        ("parallel", "parallel", "arbitrary"), None, jobs)
    return out


def _swiglu_act(g, u):
    return jax.nn.silu(g) * u


def _ffn_up(name, xn, wgu, *, tm=1024, rider=None):
    M, K = xn.shape
    J, _, F2 = wgu.shape
    F = F2 // 2
    tm = _tile(M, tm)

    def body(a_ref, w_ref, gu_ref, h_ref):
        r = jnp.dot(a_ref[...], w_ref[...], preferred_element_type=F32)
        gu_ref[...] = r.astype(gu_ref.dtype)
        h_ref[...] = _swiglu_act(r[:, :F], r[:, F:]).astype(h_ref.dtype)

    (gu, hid), extra = _hosted(
        name, body, (M // tm, J),
        [pl.BlockSpec((tm, K), lambda i, j: (i, 0)), pl.BlockSpec((None, K, F2), lambda i, j: (j, 0, 0))],
        [pl.BlockSpec((tm, F2), lambda i, j: (i, j)), pl.BlockSpec((tm, F), lambda i, j: (i, j))],
        [jax.ShapeDtypeStruct((M, J * F2), BF16), jax.ShapeDtypeStruct((M, J * F), BF16)], [], (xn, wgu),
        ("parallel", "parallel"), rider)
    return gu, hid, extra


def _ffn_down_dx(name, dout, wd, gu, J, *, scale, tm=512, jobs=None):
    M, D = dout.shape
    F = wd.shape[1] // J
    tm = _tile(M, tm)

    def body(d_ref, w_ref, gu_ref, o_ref):
        dh = scale * lax.dot_general(d_ref[...], w_ref[...], (((1,), (1,)), ((), ())), preferred_element_type=F32)
        gu = gu_ref[...].astype(F32)
        _, vjp = jax.vjp(_swiglu_act, gu[:, :F], gu[:, F:])
        o_ref[...] = jnp.concatenate(vjp(dh), axis=-1).astype(o_ref.dtype)

    (out,), _ = _hosted(
        name, body, (M // tm, J),
        [pl.BlockSpec((tm, D), lambda i, j: (i, 0)), pl.BlockSpec((None, F, D), lambda i, j: (0, j, 0)),
         pl.BlockSpec((tm, 2 * F), lambda i, j: (i, j))],
        [pl.BlockSpec((tm, 2 * F), lambda i, j: (i, j))], [jax.ShapeDtypeStruct((M, J * 2 * F), BF16)], [],
        (dout, wd, gu), ("parallel", "parallel"), None, jobs)
    return out


def _all_gather(name, shards):
    n = len(shards)

    def body(*refs):
        start, forward, finish = _gather_phases(refs[:n], refs[n:2 * n], *refs[2 * n:])
        start()
        forward()
        finish()

    any_spec = pl.BlockSpec(memory_space=pl.ANY)
    return pl.pallas_call(
        body, name=name, in_specs=[any_spec] * n, out_specs=[any_spec] * n,
        out_shape=[jax.ShapeDtypeStruct((NDEV,) + s.shape, s.dtype) for s in shards],
        scratch_shapes=_gather_sems(n),
    )(*shards)


def _gather_sems(n):
    return [pltpu.SemaphoreType.DMA((n, 7)), pltpu.SemaphoreType.DMA((n, 7)), pltpu.SemaphoreType.DMA((n,))]


def _gather_phases(ins, outs, send_sems, recv_sems, local_sems):
    n = len(ins)
    x, y, c = lax.axis_index("x"), lax.axis_index("y"), lax.axis_index("c")
    me, sibling = (x, y, c), (x, y, 1 - c)
    chips = [(1 - x, y), (x, 1 - y), (1 - x, 1 - y)]

    def blk(i, px, py, pc):
        return outs[i].at[4 * px + 2 * py + pc]

    def copy(i, k, block, to, src=None):
        return pltpu.make_async_remote_copy(
            src_ref=blk(i, *block) if src is None else src, dst_ref=blk(i, *block),
            send_sem=send_sems.at[i, k], recv_sem=recv_sems.at[i, k], device_id=to, device_id_type=MESH)

    def local(i):
        return pltpu.make_async_copy(ins[i], blk(i, *me), local_sems.at[i])

    def firsts(i):
        return [copy(i, 0, me, sibling, src=ins[i])] + [copy(i, 1 + j, me, (*chip, c), src=ins[i])
                                                        for j, chip in enumerate(chips)]

    def start():
        for i in range(n):
            local(i).start()
        for i in range(n):
            for cp in firsts(i):
                cp.start()

    def forward():
        for i in range(n):
            for j, chip in enumerate(chips):
                copy(i, 1 + j, (*chip, c), me).wait_recv()
                copy(i, 4 + j, (*chip, c), sibling).start()

    def finish():
        for i in range(n):
            copy(i, 0, sibling, me).wait_recv()
            for j, chip in enumerate(chips):
                copy(i, 4 + j, (*chip, 1 - c), me).wait_recv()
        for i in range(n):
            for cp in firsts(i):
                cp.wait_send()
            for j, chip in enumerate(chips):
                copy(i, 4 + j, (*chip, c), sibling).wait_send()
            local(i).wait()

    return start, forward, finish


class _GatherRider:
    def __init__(self, shards):
        self.operands = list(shards)
        n = len(self.operands)
        self.n_in = self.n_out = n
        any_spec = pl.BlockSpec(memory_space=pl.ANY)
        self.in_specs = [any_spec] * n
        self.out_specs = [any_spec] * n
        self.out_shape = [jax.ShapeDtypeStruct((NDEV,) + s.shape, s.dtype) for s in self.operands]
        self.scratch = _gather_sems(n)

    def bind(self, step_of, nsteps):
        return self

    def take(self, outs):
        return list(outs)

    def run(self, ins, outs, sems, step, nsteps):
        start, forward, finish = _gather_phases(ins, outs, *sems)
        pl.when(step == 0)(start)
        pl.when(step == nsteps // 2)(forward)
        pl.when(step == nsteps - 1)(finish)


class _SwapRider:
    def __init__(self, arr, streams, grid, tile, out_shape, out_block, out_map):
        self.arr, self.streams, self.grid, self.tile = arr, streams, grid, tile
        self.ns, self.n = len(streams), grid[0] * grid[1]
        self.operands = [arr] * (2 * self.ns)
        self.n_in, self.n_out = 2 * self.ns, 1
        self.out_shape = [jax.ShapeDtypeStruct(out_shape, F32)]
        self.out_block, self.out_map = out_block, out_map
        tr, C = tile
        slots = [pltpu.VMEM((2, tr, C), w) for _, w, _, _ in streams]
        self.scratch = slots + slots + [pltpu.SemaphoreType.DMA((self.ns, 2)), pltpu.SemaphoreType.DMA((self.ns, 2)),
                                        pltpu.SemaphoreType.REGULAR((self.ns,))]

    def bind(self, step_of, nsteps):
        assert nsteps >= self.n
        self.period = period = nsteps // self.n
        n, nr = self.n, self.grid[1]

        def ids(*g):
            k = jnp.minimum(step_of(*g) // period, n - 1)
            pos = {a: lax.axis_index(a) for a in AXES}
            return k // nr, k % nr, [v for a in AXES for v in (pos[a], 1 - pos[a])]

        block = (None,) * (self.arr.ndim - 2) + tuple(self.tile)
        self.in_specs = []
        for _, _, keep_map, send_map in self.streams:
            for m in (keep_map, send_map):
                self.in_specs.append(pl.BlockSpec(block, functools.partial(lambda *g, m: m(*ids(*g)), m=m)))
        self.out_specs = [pl.BlockSpec(self.out_block, lambda *g: self.out_map(*ids(*g)))]
        return self

    def take(self, outs):
        return outs[0]

    def run(self, ins, outs, scratch, step, nsteps):
        ns, n, period = self.ns, self.n, self.period
        keeps, sends, o_ref = ins[0::2], ins[1::2], outs[0]
        lands, stages = scratch[:ns], scratch[ns:2 * ns]
        send_sems, recv_sems, credits = scratch[2 * ns:]
        k = step // period
        slot = k % 2
        here = {a: lax.axis_index(a) for a in AXES}
        peers = [tuple(1 - here[a] if a == axis else here[a] for a in AXES) for axis, _, _, _ in self.streams]

        def rdma(s):
            return pltpu.make_async_remote_copy(
                src_ref=stages[s].at[slot], dst_ref=lands[s].at[slot], send_sem=send_sems.at[s, slot],
                recv_sem=recv_sems.at[s, slot], device_id=peers[s], device_id_type=MESH)

        @pl.when((k < n) & (step % period == 0))
        def _():
            @pl.when(k >= 2)
            def _():
                for s in range(ns):
                    pl.semaphore_wait(credits.at[s], 1)

            for s in range(ns):
                stages[s][slot] = sends[s][...].astype(stages[s].dtype)
                rdma(s).start()

        @pl.when((k < n) & (step % period == period - 1))
        def _():
            for s in range(ns):
                rdma(s).wait_recv()
                total = keeps[s][...] + lands[s][slot].astype(F32)
                if ns == 1:
                    o_ref[...] = total
                else:
                    o_ref[s] = total
            for s in range(ns):
                rdma(s).wait_send()

            @pl.when(k + 2 < n)
            def _():
                for s in range(ns):
                    pl.semaphore_signal(credits.at[s], inc=1, device_id=peers[s], device_id_type=MESH)


def _run_alone(name, rider):
    rider.bind(lambda t: t, rider.n)

    def body(*refs):
        a, b = rider.n_in, rider.n_in + rider.n_out
        rider.run(refs[:a], refs[a:b], refs[b:], pl.program_id(0), rider.n)

    outs = pl.pallas_call(
        body, name=name, grid=(rider.n,), in_specs=rider.in_specs, out_specs=rider.out_specs,
        out_shape=rider.out_shape, scratch_shapes=rider.scratch, compiler_params=_cp(("arbitrary",)),
    )(*rider.operands)
    return rider.take(outs)


class _Scatter:
    def __init__(self, name, g):
        self.name, self.cur, self.stage = name, g, 0
        _, self.R, self.C = g.shape

    def done(self):
        return self.stage == 3

    def rider(self, rows):
        R, C = self.R, self.C
        R2 = R // 2
        tr = _tile(R2, rows, 16)
        nrh = R2 // tr
        if self.stage == 0:
            return _SwapRider(
                self.cur.reshape(4, 2, R, C),
                [("c", BF16, lambda b, i, s: (b, s[4], i, 0), lambda b, i, s: (b, s[5], i, 0))],
                (4, 2 * nrh), (tr, C), (2, 4, R2, C), (None, None, tr, C), lambda b, i, s: (i // nrh, b, i % nrh, 0))
        if self.stage == 1:
            return _SwapRider(
                self.cur.reshape(2, 2, 2, R2, C),
                [("y", BF16, lambda b, i, s: (0, b, s[2], i, 0), lambda b, i, s: (0, b, s[3], i, 0)),
                 ("x", BF16, lambda b, i, s: (1, s[0], b, i, 0), lambda b, i, s: (1, s[1], b, i, 0))],
                (2, nrh), (tr, C), (2, 2, R2, C), (2, None, tr, C), lambda b, i, s: (0, b, i, 0))
        return _SwapRider(
            self.cur,
            [("x", BF16, lambda b, i, s: (0, s[0], i, 0), lambda b, i, s: (0, s[1], i, 0)),
             ("y", BF16, lambda b, i, s: (1, s[2], i, 0), lambda b, i, s: (1, s[3], i, 0))],
            (1, nrh), (tr, C), (2, R2, C), (2, tr, C), lambda b, i, s: (0, i, 0))

    def advance(self, out):
        self.cur, self.stage = out, self.stage + 1

    def finish(self):
        while not self.done():
            self.advance(_run_alone(f"{self.name}_s{self.stage}", self.rider(256)))
        return self.cur.reshape(self.R, self.C)


def _pick(jobs, nsteps, rows=128):
    for job in jobs or ():
        if not job.done():
            rider = job.rider(rows)
            if rider.n <= nsteps:
                return job, rider
    return None, None


def _sum8(name, g):
    _, R, C = g.shape
    tr = _tile(R, 512)

    def body(g_ref, o_ref):
        acc = g_ref[0]
        for d in range(1, NDEV):
            acc = acc + g_ref[d]
        o_ref[...] = acc

    return pl.pallas_call(
        body, name=name, grid=(R // tr,), in_specs=[pl.BlockSpec((NDEV, tr, C), lambda i: (0, i, 0))],
        out_specs=pl.BlockSpec((tr, C), lambda i: (i, 0)), out_shape=jax.ShapeDtypeStruct((R, C), F32),
        compiler_params=_cp(("parallel",)),
    )(g)


def _adamw_math(w, g, m, v):
    m = B1 * m + (1.0 - B1) * g
    v = B2 * v + (1.0 - B2) * jnp.square(g)
    m_hat = m / (1.0 - B1 ** STEP)
    v_hat = v / (1.0 - B2 ** STEP)
    delta = -LR * (m_hat / (jnp.sqrt(v_hat) + EPS) + WD * w)
    return delta, m, v


def _adamw(name, w, m, v, gp, *, tr, cw, gw, goff=0):
    R, C = w.shape
    nc = C // cw
    nr = R // tr

    def body(w_ref, m_ref, v_ref, g_ref, g_out, d_out, m_out, v_out):
        g = g_ref[...][:, :cw]
        d, mn, vn = _adamw_math(w_ref[...], g, m_ref[...], v_ref[...])
        g_out[...] = g
        d_out[...] = d
        m_out[...] = mn
        v_out[...] = vn

    wspec = pl.BlockSpec((tr, cw), lambda i, j: (i, j))
    gspec = pl.BlockSpec((tr, gw), lambda i, j: (i, goff + j))
    return pl.pallas_call(
        body, name=name, grid=(nr, nc), in_specs=[wspec, wspec, wspec, gspec], out_specs=[wspec] * 4,
        out_shape=[jax.ShapeDtypeStruct((R, C), F32)] * 4, compiler_params=_cp(("parallel", "parallel")),
    )(w, m, v, gp)


def _adamw_small(name, w, m, v, g):
    R, C = w.shape

    def body(w_ref, m_ref, v_ref, g_ref, d_out, m_out, v_out):
        d, mn, vn = _adamw_math(w_ref[...], g_ref[...], m_ref[...], v_ref[...])
        d_out[...] = d
        m_out[...] = mn
        v_out[...] = vn

    tr = _tile(R, 512)
    spec = pl.BlockSpec((tr, C), lambda i: (i, 0))
    return pl.pallas_call(
        body, name=name, grid=(R // tr,), in_specs=[spec] * 4, out_specs=[spec] * 3,
        out_shape=[jax.ShapeDtypeStruct((R, C), F32)] * 3, compiler_params=_cp(("parallel",)),
    )(w, m, v, g)


def _prep(name, parts, rows_p, cols_p):
    R, C = parts[0].shape
    n = len(parts)

    def body(*refs):
        o_ref = refs[n]
        if (R, C) != (rows_p, cols_p):
            o_ref[...] = jnp.zeros_like(o_ref)
        for i in range(n):
            o_ref[0:R, i * cols_p:i * cols_p + C] = refs[i][...].astype(BF16)

    return pl.pallas_call(
        body, name=name, out_shape=jax.ShapeDtypeStruct((rows_p, n * cols_p), BF16), compiler_params=_cp(),
    )(*parts)


def _attn_masks():
    lane = lax.broadcasted_iota(jnp.int32, (1, LANE), 1)
    return [(lane < HEAD_DIM), (lane >= HEAD_DIM)]


def _band_valid(base):
    qi = lax.broadcasted_iota(jnp.int32, (QB, 2 * QB), 0)
    ki = lax.broadcasted_iota(jnp.int32, (QB, 2 * QB), 1)
    dist = qi + QB - ki
    return (dist >= 0) & (dist <= QB) & (base + ki - QB >= 0)


ATTN_T = max(DILATIONS) * QB


def _attn_groups(T):
    out = []
    for d in DILATIONS:
        for r in range(d):
            for i in range(T // (d * QB)):
                qrows = pl.ds(r + d * i * QB, QB, stride=d) if d > 1 else pl.ds(i * QB, QB)
                k0 = T + r + d * (i - 1) * QB
                krows = pl.ds(k0, 2 * QB, stride=d) if d > 1 else pl.ds(k0, 2 * QB)
                out.append((d, qrows, krows, i * QB))
    return out


def _attn_specs(T, width_off):
    cur = pl.BlockSpec((T, LANE), lambda hp, b: (b, width_off + hp))
    prev = pl.BlockSpec((T, LANE), lambda hp, b: (jnp.maximum(b - 1, 0), width_off + hp))
    return cur, prev


def _attn_fwd(z):
    S, ZW = z.shape
    T = min(ATTN_T, S)
    scale = HEAD_DIM ** -0.5
    groups = _attn_groups(T)

    def body(q_ref, kc_ref, kp_ref, vc_ref, vp_ref, y_ref, m_ref, l_ref, kcat, vcat):
        b = pl.program_id(1)
        kcat[0:T, :] = kp_ref[...]
        kcat[T:, :] = kc_ref[...]
        vcat[0:T, :] = vp_ref[...]
        vcat[T:, :] = vc_ref[...]
        masks = _attn_masks()
        for d, qrows, krows, l0 in groups:
            q = q_ref[qrows, :]
            kk = kcat[krows, :].astype(BF16)
            vv = vcat[krows, :].astype(BF16)
            valid = _band_valid(b * (T // d) + l0)
            o_new = m_new = l_new = None
            for hm in masks:
                qh = jnp.where(hm, q, 0.0).astype(BF16)
                s = lax.dot_general(qh, kk, (((1,), (1,)), ((), ())), preferred_element_type=F32) * scale
                s = jnp.where(valid, s, MASK_VALUE)
                m = jnp.max(s, axis=-1, keepdims=True)
                p = jnp.exp(s - m)
                l = jnp.sum(p, axis=-1, keepdims=True)
                o = jnp.dot(p.astype(BF16), vv, preferred_element_type=F32)
                if o_new is None:
                    o_new, m_new, l_new = o, jnp.broadcast_to(m, (QB, LANE)), jnp.broadcast_to(l, (QB, LANE))
                else:
                    o_new = jnp.where(hm, o, o_new)
                    m_new = jnp.where(hm, m, m_new)
                    l_new = jnp.where(hm, l, l_new)
            if d == DILATIONS[0]:
                y_ref[qrows, :] = o_new
                m_ref[qrows, :] = m_new
                l_ref[qrows, :] = l_new
            else:
                m_old = m_ref[qrows, :]
                m_all = jnp.maximum(m_old, m_new)
                w_old, w_new = jnp.exp(m_old - m_all), jnp.exp(m_new - m_all)
                y_ref[qrows, :] = w_old * y_ref[qrows, :] + w_new * o_new
                l_ref[qrows, :] = w_old * l_ref[qrows, :] + w_new * l_new
                m_ref[qrows, :] = m_all
        y_ref[...] = y_ref[...] / l_ref[...]

    qc, _ = _attn_specs(T, 0)
    kc, kp = _attn_specs(T, ATTN_W // LANE)
    vc, vp = _attn_specs(T, 2 * ATTN_W // LANE)
    shp = jax.ShapeDtypeStruct((S, ATTN_W), F32)
    return pl.pallas_call(
        body, name="attn_fwd", grid=(ATTN_W // LANE, S // T),
        in_specs=[qc, kc, kp, vc, vp], out_specs=[qc, qc, qc], out_shape=[shp, shp, shp],
        scratch_shapes=[pltpu.VMEM((2 * T, LANE), F32), pltpu.VMEM((2 * T, LANE), F32)],
        compiler_params=_cp(("parallel", "parallel")),
    )(z, z, z, z, z)


def _attn_bwd(z, dya, ya, mg, den):
    S, ZW = z.shape
    T = min(ATTN_T, S)
    scale = HEAD_DIM ** -0.5
    groups = _attn_groups(T)

    def body(q_ref, kc_ref, kp_ref, vc_ref, vp_ref, dy_ref, y_ref, m_ref, n_ref, dq_ref, dk_ref, dv_ref,
             kcat, vcat, dkcat, dvcat):
        b = pl.program_id(1)

        @pl.when(b == 0)
        def _():
            dk_ref[...] = jnp.zeros_like(dk_ref)
            dv_ref[...] = jnp.zeros_like(dv_ref)

        kcat[0:T, :] = kp_ref[...]
        kcat[T:, :] = kc_ref[...]
        vcat[0:T, :] = vp_ref[...]
        vcat[T:, :] = vc_ref[...]
        dkcat[...] = jnp.zeros_like(dkcat)
        dvcat[...] = jnp.zeros_like(dvcat)
        dq_ref[...] = jnp.zeros_like(dq_ref)
        masks = _attn_masks()
        for d, rows, krows, l0 in groups:
            q, dy, y = q_ref[rows, :], dy_ref[rows, :], y_ref[rows, :]
            mrow, nrow = m_ref[rows, :], n_ref[rows, :]
            kk = kcat[krows, :].astype(BF16)
            vv = vcat[krows, :].astype(BF16)
            valid = _band_valid(b * (T // d) + l0)
            dq_acc = jnp.zeros((QB, LANE), F32)
            dk_acc = jnp.zeros((2 * QB, LANE), F32)
            dv_acc = jnp.zeros((2 * QB, LANE), F32)
            for hm in masks:
                qh = jnp.where(hm, q, 0.0).astype(BF16)
                dyh = jnp.where(hm, dy, 0.0)
                dyb = dyh.astype(BF16)
                dsum = jnp.sum(dyh * y, axis=-1, keepdims=True)
                mh = jnp.max(jnp.where(hm, mrow, MASK_VALUE), axis=-1, keepdims=True)
                nh = jnp.max(jnp.where(hm, nrow, 0.0), axis=-1, keepdims=True)
                s = lax.dot_general(qh, kk, (((1,), (1,)), ((), ())), preferred_element_type=F32) * scale
                p = jnp.where(valid, jnp.exp(s - mh), 0.0) / nh
                pb = p.astype(BF16)
                dv_h = lax.dot_general(pb, dyb, (((0,), (0,)), ((), ())), preferred_element_type=F32)
                dp = lax.dot_general(dyb, vv, (((1,), (1,)), ((), ())), preferred_element_type=F32)
                ds = (p * (dp - dsum) * scale).astype(BF16)
                dq_h = jnp.dot(ds, kk, preferred_element_type=F32)
                dk_h = lax.dot_general(ds, qh, (((0,), (0,)), ((), ())), preferred_element_type=F32)
                dq_acc += jnp.where(hm, dq_h, 0.0)
                dk_acc += dk_h
                dv_acc += dv_h
            dq_ref[rows, :] += dq_acc
            dkcat[krows, :] += dk_acc
            dvcat[krows, :] += dv_acc

        base = pl.multiple_of(b * T, T)
        dk_ref[pl.ds(base, T), :] += dkcat[T:, :]
        dv_ref[pl.ds(base, T), :] += dvcat[T:, :]

        @pl.when(b > 0)
        def _():
            prev = pl.multiple_of(b * T - T, T)
            dk_ref[pl.ds(prev, T), :] += dkcat[0:T, :]
            dv_ref[pl.ds(prev, T), :] += dvcat[0:T, :]

    qc, _ = _attn_specs(T, 0)
    kc, kp = _attn_specs(T, ATTN_W // LANE)
    vc, vp = _attn_specs(T, 2 * ATTN_W // LANE)
    whole = pl.BlockSpec((S, LANE), lambda hp, b: (0, hp))
    shp = jax.ShapeDtypeStruct((S, ATTN_W), F32)
    return pl.pallas_call(
        body, name="attn_bwd", grid=(ATTN_W // LANE, S // T),
        in_specs=[qc, kc, kp, vc, vp, qc, qc, qc, qc], out_specs=[qc, whole, whole], out_shape=[shp, shp, shp],
        scratch_shapes=[pltpu.VMEM((2 * T, LANE), F32)] * 4,
        compiler_params=_cp(("parallel", "arbitrary")),
    )(z, z, z, z, z, dya, ya, mg, den)


def _ssm_disc(lr, li, logdt, br, bi):
    dt = jnp.exp(logdt)
    mag = jnp.exp(lr * dt)
    ar = mag * jnp.cos(li * dt)
    ai = mag * jnp.sin(li * dt)
    nr, ni = ar - 1.0, ai
    den = lr * lr + li * li
    cr = (nr * lr + ni * li) / den
    ci = (ni * lr - nr * li) / den
    return ar, ai, cr * br - ci * bi, cr * bi + ci * br


def _ssm_prep(lr, li, logdt, br, bi):
    n, c = br.shape
    outs, _ = _rowwise("ssm_prep", lambda *a: (list(_ssm_disc(*a)), []), n, _tile(n, 512),
                       [_full(a) for a in (lr, li, logdt, br, bi)], [],
                       [(1, 1, _c0, F32), (1, 1, _c0, F32), (c, c, _c0, F32), (c, c, _c0, F32)])
    return outs


def _ssm_prep_bwd(lr, li, logdt, br, bi, dar, dai, dbbr, dbbi):
    n, c = br.shape

    def f(lrb, lib, dtb, brb, bib, *cts):
        _, vjp = jax.vjp(_ssm_disc, lrb, lib, dtb, brb, bib)
        return list(vjp(cts)), []

    outs, _ = _rowwise("ssm_prep_bwd", f, n, _tile(n, 512),
                       [_full(a) for a in (lr, li, logdt, br, bi, dar, dai, dbbr, dbbi)], [],
                       [(1, 1, _c0, F32)] * 3 + [(c, c, _c0, F32)] * 2)
    return outs


def _cmul(ar, ai, br, bi):
    return ar * br - ai * bi, ar * bi + ai * br


def _scan_consts(ar, ai, reverse):
    w = ar.shape[-1]
    a1 = (jnp.broadcast_to(ar, (8, w)), jnp.broadcast_to(ai, (8, w)))
    a2 = _cmul(*a1, *a1)
    a4 = _cmul(*a2, *a2)
    a8 = _cmul(*a4, *a4)
    row = lax.broadcasted_iota(jnp.int32, (8, w), 0)
    e = (8 - row) if reverse else (row + 1)
    one, zero = jnp.ones((8, w), F32), jnp.zeros((8, w), F32)
    pw = (one, zero)
    for bit, ap in ((1, a1), (2, a2), (4, a4), (8, a8)):
        sel = (e & bit) != 0
        nxt = _cmul(*pw, *ap)
        pw = (jnp.where(sel, nxt[0], pw[0]), jnp.where(sel, nxt[1], pw[1]))
    steps = []
    for sh, (pr, pi) in zip((1, 2, 4), (a1, a2, a4)):
        keep = (row < 8 - sh) if reverse else (row >= sh)
        steps.append((jnp.where(keep, pr, 0.0), jnp.where(keep, pi, 0.0)))
    return steps, pw, row


def _scan_group(xr, xi, cr, ci, consts, reverse):
    steps, pw, _ = consts
    for sh, (pr, pi) in zip((1, 2, 4), steps):
        by = 8 - sh if reverse else sh
        tr_, ti_ = _cmul(pr, pi, pltpu.roll(xr, by, 0), pltpu.roll(xi, by, 0))
        xr = xr + tr_
        xi = xi + ti_
    tr_, ti_ = _cmul(pw[0], pw[1], cr, ci)
    return xr + tr_, xi + ti_


def _ssm_fwd(z, a_r, a_i, bdr, bdi, cmr, cmi, dskip, ts, rider=None):
    S, ZW = z.shape
    NS = SSM_G * SSM_P
    PW = PACK * SSM_P
    uoff = (ZW - SSM_W) // LANE
    nsteps = S // ts

    def body(u_ref, ar_ref, ai_ref, bdr_ref, bdi_ref, cmr_ref, cmi_ref, d_ref, hr_ref, hi_ref, y_ref, car_r, car_i):
        s = pl.program_id(1)

        @pl.when(s == 0)
        def _():
            car_r[...] = jnp.zeros_like(car_r)
            car_i[...] = jnp.zeros_like(car_i)

        u = u_ref[...]
        ub = u.astype(BF16)
        nt = (((1,), (1,)), ((), ()))
        hr_ref[...] = lax.dot_general(ub, bdr_ref[...], nt, preferred_element_type=F32)
        hi_ref[...] = lax.dot_general(ub, bdi_ref[...], nt, preferred_element_type=F32)
        consts = _scan_consts(ar_ref[...], ai_ref[...], False)

        def step(j, carry):
            rows = pl.ds(pl.multiple_of(j * 8, 8), 8)
            hr, hi = _scan_group(hr_ref[rows, :], hi_ref[rows, :], carry[0], carry[1], consts, False)
            hr_ref[rows, :] = hr
            hi_ref[rows, :] = hi
            return jnp.broadcast_to(hr[7:8, :], (8, PW)), jnp.broadcast_to(hi[7:8, :], (8, PW))

        cr, ci = lax.fori_loop(0, ts // 8, step, (car_r[...], car_i[...]))
        car_r[...] = cr
        car_i[...] = ci
        y = lax.dot_general(hr_ref[...].astype(BF16), cmr_ref[...], nt, preferred_element_type=F32)
        y -= lax.dot_general(hi_ref[...].astype(BF16), cmi_ref[...], nt, preferred_element_type=F32)
        y_ref[...] = y + d_ref[...] * u

    row_a = pl.BlockSpec((1, PW), lambda i, s: (0, i))
    (hr, hi, y), extra = _hosted(
        "ssm_fwd", body, (SSM_G // PACK, nsteps),
        [pl.BlockSpec((ts, LANE), lambda i, s: (s, uoff + i)), row_a, row_a,
         pl.BlockSpec((None, PW, LANE), lambda i, s: (i, 0, 0)), pl.BlockSpec((None, PW, LANE), lambda i, s: (i, 0, 0)),
         pl.BlockSpec((None, LANE, PW), lambda i, s: (i, 0, 0)), pl.BlockSpec((None, LANE, PW), lambda i, s: (i, 0, 0)),
         pl.BlockSpec((1, LANE), lambda i, s: (0, i))],
        [pl.BlockSpec((ts, PW), lambda i, s: (s, i)), pl.BlockSpec((ts, PW), lambda i, s: (s, i)),
         pl.BlockSpec((ts, LANE), lambda i, s: (s, i))],
        [jax.ShapeDtypeStruct((S, NS), F32), jax.ShapeDtypeStruct((S, NS), F32), jax.ShapeDtypeStruct((S, SSM_W), F32)],
        [pltpu.VMEM((8, PW), F32), pltpu.VMEM((8, PW), F32)], (z, a_r, a_i, bdr, bdi, cmr, cmi, dskip),
        ("parallel", "arbitrary"), rider)
    return hr, hi, y, extra


def _ssm_bwd(z, dyp, hr, hi, a_r, a_i, bdr, bdi, cmr, cmi, dskip, ts, jobs=None):
    S, ZW = z.shape
    NS = SSM_G * SSM_P
    PW = PACK * SSM_P
    uoff = (ZW - SSM_W) // LANE
    nsteps = S // ts
    npk = SSM_G // PACK

    def body(u_ref, dy_ref, hr_ref, hi_ref, hpr_ref, hpi_ref, ar_ref, ai_ref, bdr_ref, bdi_ref, cmr_ref, cmi_ref,
             d_ref, du_ref, dbdr_ref, dbdi_ref, dcmr_ref, dcmi_ref, dar_ref, dai_ref, dd_ref,
             lr_s, li_s, hcr, hci, car_r, car_i):
        s = pl.program_id(1)
        first_tile = s == nsteps - 1

        @pl.when(s == 0)
        def _():
            car_r[...] = jnp.zeros_like(car_r)
            car_i[...] = jnp.zeros_like(car_i)
            for r in (dbdr_ref, dbdi_ref, dcmr_ref, dcmi_ref, dar_ref, dai_ref, dd_ref):
                r[...] = jnp.zeros_like(r)

        u, dy = u_ref[...], dy_ref[...]
        ub, dyb = u.astype(BF16), dy.astype(BF16)
        lr_s[...] = jnp.dot(dyb, cmr_ref[...], preferred_element_type=F32)
        li_s[...] = -jnp.dot(dyb, cmi_ref[...], preferred_element_type=F32)
        keep_prev = jnp.where(first_tile, 0.0, 1.0)
        hcr[0:8, :] = hpr_ref[...] * keep_prev
        hci[0:8, :] = hpi_ref[...] * keep_prev
        hcr[8:, :] = hr_ref[...]
        hci[8:, :] = hi_ref[...]
        consts = _scan_consts(ar_ref[...], -ai_ref[...], True)
        row = consts[2]
        ngrp = ts // 8

        def step(jj, carry):
            cr, ci, accr, acci = carry
            j = ngrp - 1 - jj
            rows = pl.ds(pl.multiple_of(j * 8, 8), 8)
            nxt = pl.ds(pl.multiple_of(j * 8 + 8, 8), 8)
            lr, li = _scan_group(lr_s[rows, :], li_s[rows, :], cr, ci, consts, True)
            lr_s[rows, :] = lr
            li_s[rows, :] = li
            pr, pi = hcr[rows, :], hci[rows, :]
            hsr = jnp.where(row == 0, jnp.broadcast_to(pr[7:8, :], (8, PW)), pltpu.roll(hcr[nxt, :], 1, 0))
            hsi = jnp.where(row == 0, jnp.broadcast_to(pi[7:8, :], (8, PW)), pltpu.roll(hci[nxt, :], 1, 0))
            accr = accr + lr * hsr + li * hsi
            acci = acci + li * hsr - lr * hsi
            return jnp.broadcast_to(lr[0:1, :], (8, PW)), jnp.broadcast_to(li[0:1, :], (8, PW)), accr, acci

        zero = jnp.zeros((8, PW), F32)
        cr, ci, accr, acci = lax.fori_loop(0, ngrp, step, (car_r[...], car_i[...], zero, zero))
        car_r[...] = cr
        car_i[...] = ci
        dar_ref[...] += jnp.sum(accr, axis=0, keepdims=True)
        dai_ref[...] += jnp.sum(acci, axis=0, keepdims=True)
        lrb, lib = lr_s[...].astype(BF16), li_s[...].astype(BF16)
        du = jnp.dot(lrb, bdr_ref[...], preferred_element_type=F32)
        du += jnp.dot(lib, bdi_ref[...], preferred_element_type=F32)
        du_ref[...] = du + dy * d_ref[...]
        tn = (((0,), (0,)), ((), ()))
        dbdr_ref[...] += lax.dot_general(lrb, ub, tn, preferred_element_type=F32)
        dbdi_ref[...] += lax.dot_general(lib, ub, tn, preferred_element_type=F32)
        dcmr_ref[...] += lax.dot_general(dyb, hr_ref[...].astype(BF16), tn, preferred_element_type=F32)
        dcmi_ref[...] -= lax.dot_general(dyb, hi_ref[...].astype(BF16), tn, preferred_element_type=F32)
        dd_ref[...] += jnp.sum(dy * u, axis=0, keepdims=True)

    rev = lambda s: nsteps - 1 - s
    row_a = pl.BlockSpec((1, PW), lambda i, s: (0, i))
    tile = pl.BlockSpec((ts, PW), lambda i, s: (rev(s), i))
    prev8 = pl.BlockSpec((8, PW), lambda i, s: (jnp.maximum(rev(s) * (ts // 8) - 1, 0), i))
    cols = pl.BlockSpec((ts, LANE), lambda i, s: (rev(s), i))
    bd = pl.BlockSpec((None, PW, LANE), lambda i, s: (i, 0, 0))
    cm = pl.BlockSpec((None, LANE, PW), lambda i, s: (i, 0, 0))
    outs, _ = _hosted(
        "ssm_bwd", body, (npk, nsteps),
        [pl.BlockSpec((ts, LANE), lambda i, s: (rev(s), uoff + i)), cols, tile, tile, prev8, prev8,
         row_a, row_a, bd, bd, cm, cm, pl.BlockSpec((1, LANE), lambda i, s: (0, i))],
        [cols, bd, bd, cm, cm, row_a, row_a, pl.BlockSpec((1, LANE), lambda i, s: (0, i))],
        [jax.ShapeDtypeStruct((S, SSM_W), F32),
         jax.ShapeDtypeStruct((npk, PW, LANE), F32), jax.ShapeDtypeStruct((npk, PW, LANE), F32),
         jax.ShapeDtypeStruct((npk, LANE, PW), F32), jax.ShapeDtypeStruct((npk, LANE, PW), F32),
         jax.ShapeDtypeStruct((1, NS), F32), jax.ShapeDtypeStruct((1, NS), F32), jax.ShapeDtypeStruct((1, SSM_W), F32)],
        [pltpu.VMEM((ts, PW), F32), pltpu.VMEM((ts, PW), F32), pltpu.VMEM((ts + 8, PW), F32),
         pltpu.VMEM((ts + 8, PW), F32), pltpu.VMEM((8, PW), F32), pltpu.VMEM((8, PW), F32)],
        (z, dyp, hr, hi, hr, hi, a_r, a_i, bdr, bdi, cmr, cmi, dskip), ("parallel", "arbitrary"), None, jobs)
    return outs


def _block_diag(m4):
    npk, g, a, b = m4.shape
    eye = jnp.eye(g, dtype=m4.dtype)
    return (m4[:, :, :, None, :] * eye[None, :, None, :, None]).reshape(npk, g * a, g * b)


def _block_diag_take(m, a, b):
    npk = m.shape[0]
    m5 = m.reshape(npk, PACK, a, PACK, b)
    return jnp.stack([m5[:, g, :, g, :] for g in range(PACK)], axis=1)


def _mix_out(ya, ypre, gl, ga, gb, bglu):
    yg = jax.nn.gelu(ypre)
    yb = yg * jax.nn.sigmoid(gl + bglu)
    return jnp.concatenate([_rms(ya, ga), _rms(yb, gb)], axis=-1)


def _tail_loss(h3, gl, pe, gf, tgt):
    h4 = h3 + jax.nn.sigmoid(gl) * pe
    err = jnp.square(_rms(h4, gf) - tgt)
    return 0.5 * jnp.mean(err, axis=-1, keepdims=True)


def kernel(x, p, ffn1_norm, ffn1_w_gate, ffn1_w_up, ffn1_w_down, mix_norm, w_in, attn_out_norm, ssm_lambda_re, ssm_lambda_im, ssm_log_dt, ssm_b_re, ssm_b_im, ssm_c_re, ssm_c_im, ssm_d, ssm_w_glu, ssm_b_glu, ssm_out_norm, w_out, ffn2_norm, ffn2_w_gate, ffn2_w_up, ffn2_w_down, ple_norm, ple_w_gate, ple_w_proj, final_norm, loss_target, m_ffn1_norm, m_ffn1_w_gate, m_ffn1_w_up, m_ffn1_w_down, m_mix_norm, m_w_in, m_attn_out_norm, m_ssm_lambda_re, m_ssm_lambda_im, m_ssm_log_dt, m_ssm_b_re, m_ssm_b_im, m_ssm_c_re, m_ssm_c_im, m_ssm_d, m_ssm_w_glu, m_ssm_b_glu, m_ssm_out_norm, m_w_out, m_ffn2_norm, m_ffn2_w_gate, m_ffn2_w_up, m_ffn2_w_down, m_ple_norm, m_ple_w_gate, m_ple_w_proj, m_final_norm, v_ffn1_norm, v_ffn1_w_gate, v_ffn1_w_up, v_ffn1_w_down, v_mix_norm, v_w_in, v_attn_out_norm, v_ssm_lambda_re, v_ssm_lambda_im, v_ssm_log_dt, v_ssm_b_re, v_ssm_b_im, v_ssm_c_re, v_ssm_c_im, v_ssm_d, v_ssm_w_glu, v_ssm_b_glu, v_ssm_out_norm, v_w_out, v_ffn2_norm, v_ffn2_w_gate, v_ffn2_w_up, v_ffn2_w_down, v_ple_norm, v_ple_w_gate, v_ple_w_proj, v_final_norm):
    A = dict(locals())
    xs = x[0]
    ps = p[0, 0]
    tgt = loss_target[0]
    S, D = xs.shape
    FSH = ffn1_w_gate.shape[-1]
    FSP = -(-FSH // LANE) * LANE
    TR = _tile(S, 256)
    ZW = 3 * ATTN_W + SSM_W

    wgu1 = _prep("prep_gu1", [ffn1_w_gate[0], ffn1_w_up[0]], D, FSP)
    wgu2 = _prep("prep_gu2", [ffn2_w_gate[0], ffn2_w_up[0]], D, FSP)
    wd1 = _prep("prep_d1", [ffn1_w_down[0]], FSP, D)
    wd2 = _prep("prep_d2", [ffn2_w_down[0]], FSP, D)
    win = _prep("prep_in", [w_in[0]], D, w_in.shape[-1])
    wglu = _prep("prep_glu", [ssm_w_glu[0]], ssm_w_glu.shape[1], SSM_W)
    wout = _prep("prep_out", [w_out[0]], w_out.shape[1], D)
    wpg = _prep("prep_pg", [ple_w_gate[0]], ple_w_gate.shape[1], D)
    wpp = _prep("prep_pp", [ple_w_proj[0]], ple_w_proj.shape[1], ple_w_proj.shape[2])
    (Wgu1,) = _all_gather("ag_weights", [wgu1])
    rowstack = lambda w: w.reshape(1, w.shape[0] * w.shape[1], w.shape[2])

    def ffn_norm(tag, h, gain):
        return _rowwise(f"{tag}_norm", lambda a, g: ([_rms(a, g)], []), S, TR, [_full(h)], [gain], [(D, D, _c0, BF16)])[0][0]

    xn1 = ffn_norm("ffn1", xs, ffn1_norm)
    gu1, hid1, (Wd1, Win) = _ffn_up("ffn1_up", xn1, Wgu1, rider=_GatherRider([wd1, win]))
    Wd1 = rowstack(Wd1)
    h1, (Wgu2,) = _mm_nn("ffn1_down", hid1, Wd1, tn=D, tk=2 * FSP, res=xs, scale=0.5, rider=_GatherRider([wgu2]))

    (un,), _ = _rowwise("mix_norm", lambda a, g: ([_rms(a, g)], []), S, TR, [_full(h1)], [mix_norm], [(D, D, _c0, BF16)])
    z, (Wglu, Wout, Wpg, Wpp) = _mm_nn("mix_in", un, Win, tn=512, tk=D,
                                       rider=_GatherRider([wglu, wout, wpg, wpp]))
    Wglu, Wout, Wpg = rowstack(Wglu), rowstack(Wout), rowstack(Wpg)
    ya, mg, den = _attn_fwd(z)

    col = lambda a: a.reshape(-1, 1)
    lr_c, li_c = col(ssm_lambda_re), col(ssm_lambda_im)
    dt_c = col(jnp.broadcast_to(ssm_log_dt.reshape(SSM_G, 1), (SSM_G, SSM_P)))
    b_re2, b_im2 = ssm_b_re.reshape(-1, SSM_C), ssm_b_im.reshape(-1, SSM_C)
    ar_c, ai_c, bbr, bbi = _ssm_prep(lr_c, li_c, dt_c, b_re2, b_im2)
    a_r, a_i = ar_c.reshape(1, -1), ai_c.reshape(1, -1)
    npk = SSM_G // PACK
    bdr = _block_diag(bbr.reshape(npk, PACK, SSM_P, SSM_C)).astype(BF16)
    bdi = _block_diag(bbi.reshape(npk, PACK, SSM_P, SSM_C)).astype(BF16)
    cmr = _block_diag(ssm_c_re.reshape(npk, PACK, SSM_C, SSM_P)).astype(BF16)
    cmi = _block_diag(ssm_c_im.reshape(npk, PACK, SSM_C, SSM_P)).astype(BF16)
    TS = _tile(S, 512)
    hr, hi, ypre, (Wd2,) = _ssm_fwd(z, a_r, a_i, bdr, bdi, cmr, cmi, ssm_d, TS, rider=_GatherRider([wd2]))
    Wd2 = rowstack(Wd2)
    (yg,), _ = _rowwise("ssm_gelu", lambda a: ([jax.nn.gelu(a)], []), S, TR, [_full(ypre)], [], [(SSM_W, SSM_W, _c0, BF16)])
    gl = _mm_nn("ssm_glu", yg, Wglu, tn=SSM_W, tk=SSM_W)
    (ycat,), _ = _rowwise("mix_out", lambda *a: ([_mix_out(*a)], []), S, TR, [_full(ya), _full(ypre), _full(gl)],
                          [attn_out_norm, ssm_out_norm, ssm_b_glu], [(MIX_W, MIX_W, _c0, BF16)])
    h2 = _mm_nn("mix_proj", ycat, Wout, tn=D // 2, tk=D, res=h1, scale=1.0)

    xn2 = ffn_norm("ffn2", h2, ffn2_norm)
    gu2, hid2, _ = _ffn_up("ffn2_up", xn2, Wgu2)
    h3 = _mm_nn("ffn2_down", hid2, Wd2, tn=D, tk=2 * FSP, res=h2, scale=0.5)

    (hn, pb), _ = _rowwise("ple_norm", lambda a, q, g: ([_rms(a, g), q], []), S, TR, [_full(h3), _full(ps)], [ple_norm],
                           [(D, D, _c0, BF16), (ps.shape[1], ps.shape[1], _c0, BF16)])
    pgl = _mm_nn("ple_gate", hn, Wpg, tn=D // 2, tk=D)
    pe = _mm_nn("ple_proj", pb, Wpp, tn=Wpp.shape[2], tk=Wpp.shape[1])

    def tail(h3b, glb, peb, tb, gf):
        rows, vjp = jax.vjp(lambda a, b, c, g: _tail_loss(a, b, c, g, tb), h3b, glb, peb, gf)
        dh, dgl, dpe, dgf = vjp(jnp.ones_like(rows))
        return [dh, dgl, dpe], [jnp.broadcast_to(jnp.sum(rows, axis=0, keepdims=True), (1, LANE)), dgf]

    (dh3_dir, dpgl, dpe), (loss_row, g_final) = _rowwise(
        "tail", tail, S, TR, [_full(h3), _full(pgl), _full(pe), _full(tgt)], [final_norm.reshape(1, D)],
        [(D, D, _c0, F32), (D, D, _c0, BF16), (D, D, _c0, BF16)], [(LANE, LANE, _c0), (D, D, _c0)])
    loss = lax.psum(loss_row[0, 0], AXES)

    def norm_bwd(tag, h, gain, dn, dres):
        def f(hb, dnb, drb, g):
            _, vjp = jax.vjp(_rms, hb, g)
            dh, dg = vjp(dnb)
            dh = dh + drb
            return [dh, dh], [dg]
        (dh, dhb), (dg,) = _rowwise(f"{tag}_norm_bwd", f, S, TR, [_full(h), _full(dn), _full(dres)], [gain],
                                    [(D, D, _c0, F32), (D, D, _c0, BF16)], [(D, D, _c0)])
        return dh, dhb, dg

    restack = lambda g: g.reshape((NDEV, g.shape[1] // NDEV) + g.shape[2:])
    jobs, scat = [], {}

    def scatter(key, g):
        scat[key] = _Scatter("rs_" + key, g)
        jobs.append(scat[key])

    dhn = _mm_nt("ple_gate_dx", dpgl, Wpg, tn=D, tk=D)
    scatter("pg", restack(_mm_tn("ple_gate_dw", hn, dpgl, 1)))
    scatter("pp", _mm_tn("ple_proj_dw", pb, dpe, NDEV, jobs=jobs))
    dh3, dh3b, g_ple_norm = norm_bwd("ple", h3, ple_norm, dhn, dh3_dir)

    def ffn_bwd(tag, h, gain, Wgu, Wd, saved, dout, doutb):
        xn, gu, hid = saved
        dgu = _ffn_down_dx(f"{tag}_down_dx", doutb, Wd, gu, NDEV, scale=0.5, jobs=jobs)
        scatter(tag + "d", restack(_mm_tn(f"{tag}_down_dw", hid, doutb, 1, tm=2048, tko=FSP, tn=D // 2, scale=0.5,
                                          jobs=jobs)))
        dxn = _mm_nt(f"{tag}_up_dx", dgu, Wgu, tm=1024, tn=D // 2, tk=2 * FSP, jobs=jobs)
        scatter(tag + "gu", _mm_tn(f"{tag}_up_dw", xn, dgu, NDEV, tm=2048, tn=FSP, jobs=jobs))
        dh, dhb, g_norm = norm_bwd(tag, h, gain, dxn, dout)
        return dh, dhb, g_norm

    dh2, dh2b, g_ffn2_norm = ffn_bwd("ffn2", h2, ffn2_norm, Wgu2, Wd2, (xn2, gu2, hid2), dh3, dh3b)

    dycat = _mm_nt("mix_proj_dx", dh2b, Wout, tn=D, tk=D, jobs=jobs)
    scatter("out", restack(_mm_tn("mix_proj_dw", ycat, dh2b, 1, jobs=jobs)))

    def mix_out_bwd(yab, ypb, glb, dyc, ga, gb, bglu):
        _, vjp = jax.vjp(_mix_out, yab, ypb, glb, ga, gb, bglu)
        dya_, dyp_, dgl_, dga, dgb, dbg = vjp(dyc)
        return [dya_, dyp_, dgl_], [dga, dgb, dbg]
    (dya, dyp_dir, dglb), (g_attn_norm, g_ssm_norm, g_bglu) = _rowwise(
        "mix_out_bwd", mix_out_bwd, S, TR, [_full(ya), _full(ypre), _full(gl), _full(dycat)],
        [attn_out_norm, ssm_out_norm, ssm_b_glu],
        [(ATTN_W, ATTN_W, _c0, F32), (SSM_W, SSM_W, _c0, F32), (SSM_W, SSM_W, _c0, BF16)],
        [(ATTN_W, ATTN_W, _c0), (SSM_W, SSM_W, _c0), (SSM_W, SSM_W, _c0)])
    dyg = _mm_nt("ssm_glu_dx", dglb, Wglu, tn=SSM_W, tk=SSM_W, jobs=jobs)
    scatter("glu", restack(_mm_tn("ssm_glu_dw", yg, dglb, 1, jobs=jobs)))

    def gelu_bwd(ypb, dygb, ddir):
        _, vjp = jax.vjp(jax.nn.gelu, ypb)
        return [ddir + vjp(dygb)[0]], []
    (dyp,), _ = _rowwise("ssm_gelu_bwd", gelu_bwd, S, TR, [_full(ypre), _full(dyg), _full(dyp_dir)], [],
                         [(SSM_W, SSM_W, _c0, F32)])
    du, dbdr, dbdi, dcmr, dcmi, da_r, da_i, g_ssm_d = _ssm_bwd(z, dyp, hr, hi, a_r, a_i, bdr, bdi, cmr, cmi, ssm_d, TS,
                                                              jobs=jobs)
    dbbr = _block_diag_take(dbdr, SSM_P, SSM_C).reshape(-1, SSM_C)
    dbbi = _block_diag_take(dbdi, SSM_P, SSM_C).reshape(-1, SSM_C)
    g_c_re = _block_diag_take(dcmr, SSM_C, SSM_P).reshape(ssm_c_re.shape)
    g_c_im = _block_diag_take(dcmi, SSM_C, SSM_P).reshape(ssm_c_im.shape)
    dlr, dli, ddt, g_b_re, g_b_im = _ssm_prep_bwd(lr_c, li_c, dt_c, b_re2, b_im2, col(da_r), col(da_i), dbbr, dbbi)
    g_lam_re, g_lam_im = dlr.reshape(ssm_lambda_re.shape), dli.reshape(ssm_lambda_im.shape)
    g_log_dt = jnp.sum(ddt.reshape(SSM_G, SSM_P), axis=1).reshape(ssm_log_dt.shape)
    g_b_re, g_b_im = g_b_re.reshape(ssm_b_re.shape), g_b_im.reshape(ssm_b_im.shape)

    dq, dk, dv = _attn_bwd(z, dya, ya, mg, den)
    (dz,), _ = _rowwise("mix_dz", lambda *a: ([jnp.concatenate(a, axis=-1)], []), S, TR,
                        [_full(dq), _full(dk), _full(dv), _full(du)], [], [(ZW, ZW, _c0, BF16)])
    dun = _mm_nt("mix_in_dx", dz, Win, tn=D, tk=512, jobs=jobs)
    scatter("in", _mm_tn("mix_in_dw", un, dz, NDEV, tn=512, jobs=jobs))
    dh1, dh1b, g_mix_norm = norm_bwd("mix", h1, mix_norm, dun, dh2)

    dx, _dxb, g_ffn1_norm = ffn_bwd("ffn1", xs, ffn1_norm, Wgu1, Wd1, (xn1, gu1, hid1), dh1, dh1b)

    mine = {key: job.finish() for key, job in scat.items()}
    out = {}

    def upd(name, idx, *, tr, cw, gw, goff=0):
        w, m, v = A[name][0], A["m_" + name][0], A["v_" + name][0]
        g, dlt, mn, vn = _adamw("adamw_" + name, w, m, v, mine[idx], tr=tr, cw=cw, gw=gw, goff=goff)
        for k, val in (("grad_", g), ("delta_", dlt), ("new_m_", mn), ("new_v_", vn)):
            out[k + name] = val[None]

    DT = _tile(D, 256)
    FT = _tile(FSH, 512)
    DC = _tile(D, 1024, LANE)
    upd("ffn1_w_gate", "ffn1gu", tr=DT, cw=FSH, gw=FSP, goff=0)
    upd("ffn1_w_up", "ffn1gu", tr=DT, cw=FSH, gw=FSP, goff=1)
    upd("ffn1_w_down", "ffn1d", tr=FT, cw=DC, gw=DC)
    upd("w_in", "in", tr=DT, cw=w_in.shape[-1], gw=w_in.shape[-1])
    upd("ssm_w_glu", "glu", tr=ssm_w_glu.shape[1], cw=SSM_W, gw=SSM_W)
    upd("w_out", "out", tr=w_out.shape[1], cw=DC, gw=DC)
    upd("ffn2_w_gate", "ffn2gu", tr=DT, cw=FSH, gw=FSP, goff=0)
    upd("ffn2_w_up", "ffn2gu", tr=DT, cw=FSH, gw=FSP, goff=1)
    upd("ffn2_w_down", "ffn2d", tr=FT, cw=DC, gw=DC)
    upd("ple_w_gate", "pg", tr=ple_w_gate.shape[1], cw=DC, gw=DC)
    upd("ple_w_proj", "pp", tr=ple_w_proj.shape[1], cw=ple_w_proj.shape[2], gw=ple_w_proj.shape[2])

    small = [("ffn1_norm", g_ffn1_norm), ("mix_norm", g_mix_norm), ("attn_out_norm", g_attn_norm),
             ("ssm_lambda_re", g_lam_re), ("ssm_lambda_im", g_lam_im), ("ssm_log_dt", g_log_dt),
             ("ssm_b_re", g_b_re), ("ssm_b_im", g_b_im), ("ssm_c_re", g_c_re), ("ssm_c_im", g_c_im),
             ("ssm_d", g_ssm_d), ("ssm_b_glu", g_bglu), ("ssm_out_norm", g_ssm_norm), ("ffn2_norm", g_ffn2_norm),
             ("ple_norm", g_ple_norm), ("final_norm", g_final)]
    chunk = 8 * LANE

    def pack(arrs):
        parts = []
        for a in arrs:
            flat = a.reshape(-1)
            padn = -(-flat.shape[0] // chunk) * chunk
            parts.append(jnp.pad(flat, (0, padn - flat.shape[0])).reshape(-1, LANE))
        return jnp.concatenate(parts, axis=0)

    g_pack = pack([g for _, g in small])
    (g_all,) = _all_gather("ag_small", [g_pack])
    g_sum = _sum8("small_sum", g_all)
    w_pack = pack([A[n] for n, _ in small])
    m_pack = pack([A["m_" + n] for n, _ in small])
    v_pack = pack([A["v_" + n] for n, _ in small])
    d_pack, mn_pack, vn_pack = _adamw_small("adamw_small", w_pack, m_pack, v_pack, g_sum)
    off = 0
    for n, _ in small:
        shape = A[n].shape
        size = math.prod(shape)
        rows = -(-size // chunk) * 8
        for k, buf in (("grad_", g_sum), ("delta_", d_pack), ("new_m_", mn_pack), ("new_v_", vn_pack)):
            out[k + n] = buf[off:off + rows].reshape(-1)[:size].reshape(shape)
        off += rows

    names = ['ffn1_norm', 'ffn1_w_gate', 'ffn1_w_up', 'ffn1_w_down', 'mix_norm', 'w_in', 'attn_out_norm',
             'ssm_lambda_re', 'ssm_lambda_im', 'ssm_log_dt', 'ssm_b_re', 'ssm_b_im', 'ssm_c_re', 'ssm_c_im', 'ssm_d',
             'ssm_w_glu', 'ssm_b_glu', 'ssm_out_norm', 'w_out', 'ffn2_norm', 'ffn2_w_gate', 'ffn2_w_up', 'ffn2_w_down',
             'ple_norm', 'ple_w_gate', 'ple_w_proj', 'final_norm']
    return (loss, dx[None], *[out[k + n] for k in ("grad_", "delta_", "new_m_", "new_v_") for n in names])
```

```python
import functools
import math

import jax
import jax.numpy as jnp
from jax import lax
from jax.experimental import pallas as pl
from jax.experimental.pallas import tpu as pltpu

F32, BF16 = jnp.float32, jnp.bfloat16
MESH = pl.DeviceIdType.MESH
NDEV = 8
AXES = ("x", "y", "c")
LANE = 128
VMEM_LIMIT = 56 * 1024 * 1024

ATTN_W = 1024
HEAD_DIM = 64
SSM_W = 1024
MIX_W = ATTN_W + SSM_W
SSM_G, SSM_P, SSM_C = 64, 64, 16
PACK = 8
DILATIONS = (1, 4, 16)
QB = 128
NORM_EPS = 1e-6
MASK_VALUE = -1e30
LR, B1, B2, EPS, WD, STEP = 0.001, 0.9, 0.999, 1e-08, 0.01, 10


def _cp(sem=None):
    return pltpu.CompilerParams(dimension_semantics=sem, vmem_limit_bytes=VMEM_LIMIT)


def _tile(n, target, mult=8):
    if n <= target:
        return n
    for t in range(target - target % mult, 0, -mult):
        if n % t == 0:
            return t
    return n


def _rms(x, g):
    return x * lax.rsqrt(jnp.mean(x * x, axis=-1, keepdims=True) + NORM_EPS) * g


def _rowwise(name, fn, S, tr, rows, fulls, outs, accs=(), ncol=1):
    nr, nf, no, na = len(rows), len(fulls), len(outs), len(accs)

    def body(*refs):
        ins = [r[...] for r in refs[:nr + nf]]
        o_refs = refs[nr + nf:nr + nf + no]
        a_refs = refs[nr + nf + no:]
        o_vals, a_vals = fn(*ins)
        for r, v in zip(o_refs, o_vals):
            r[...] = v.astype(r.dtype)
        if na:
            @pl.when(pl.program_id(1) == 0)
            def _():
                for r in a_refs:
                    r[...] = jnp.zeros_like(r)
            for r, v in zip(a_refs, a_vals):
                r[...] += v

    in_specs = [pl.BlockSpec((tr, w), functools.partial(lambda j, i, cm: (i, cm(j)), cm=cm)) for _, w, cm in rows]
    in_specs += [pl.BlockSpec(f.shape, functools.partial(lambda j, i, nd: (0,) * nd, nd=f.ndim)) for f in fulls]
    out_specs = [pl.BlockSpec((tr, w), functools.partial(lambda j, i, cm: (i, cm(j)), cm=cm)) for _, w, cm, _ in outs]
    out_specs += [pl.BlockSpec((1, w), functools.partial(lambda j, i, cm: (0, cm(j)), cm=cm)) for _, w, cm in accs]
    out_shape = [jax.ShapeDtypeStruct((S, c), dt) for c, _, _, dt in outs]
    out_shape += [jax.ShapeDtypeStruct((1, c), F32) for c, _, _ in accs]
    res = pl.pallas_call(
        body, name=name, grid=(ncol, S // tr), in_specs=in_specs, out_specs=out_specs, out_shape=out_shape,
        compiler_params=_cp(("parallel", "arbitrary" if na else "parallel")),
    )(*[a for a, _, _ in rows], *fulls)
    return res[:no], res[no:]


def _c0(j):
    return 0


def _full(a):
    return (a, a.shape[1], _c0)


def _hosted(name, body, grid, in_specs, out_specs, out_shape, scratch, args, sem, rider=None, jobs=None):
    nsteps = math.prod(grid)

    def step_of(*g):
        t = 0
        for gi, n in zip(g, grid):
            t = t * n + gi
        return t

    job = None
    if rider is None:
        job, rider = _pick(jobs, nsteps)
    if rider is None:
        outs = pl.pallas_call(body, name=name, grid=grid, in_specs=in_specs, out_specs=out_specs, out_shape=out_shape,
                              scratch_shapes=scratch, compiler_params=_cp(sem))(*args)
        return outs, None
    rider.bind(step_of, nsteps)
    n_in, n_out, n_scr = len(in_specs), len(out_specs), len(scratch)

    def full(*refs):
        a, b = n_in, n_in + rider.n_in
        c, d = b + n_out, b + n_out + rider.n_out
        rider.run(refs[a:b], refs[c:d], refs[d + n_scr:], step_of(*[pl.program_id(i) for i in range(len(grid))]), nsteps)
        body(*(refs[:a] + refs[b:c] + refs[d:d + n_scr]))

    outs = pl.pallas_call(
        full, name=name, grid=grid, in_specs=in_specs + rider.in_specs, out_specs=out_specs + rider.out_specs,
        out_shape=out_shape + rider.out_shape, scratch_shapes=scratch + rider.scratch,
        compiler_params=_cp(("arbitrary",) * len(grid)))(*args, *rider.operands)
    extra = rider.take(outs[n_out:])
    if job is not None:
        job.advance(extra)
        extra = None
    return outs[:n_out], extra


def _mm_nn(name, a, w, *, out_dtype=F32, tm=512, tn=768, tk=2048, res=None, scale=1.0, rider=None, jobs=None):
    M, K = a.shape
    J, K2, Np = w.shape
    assert K == K2
    tm, tn, tk = _tile(M, tm), _tile(Np, tn, LANE), _tile(K, tk, LANE)
    npj = Np // tn
    nk = K // tk
    grid = (M // tm, J * npj, nk)

    def body(*refs):
        if res is None:
            a_ref, w_ref, o_ref, acc = refs
        else:
            a_ref, w_ref, r_ref, o_ref, acc = refs
        k = pl.program_id(2)
        part = jnp.dot(a_ref[...].astype(BF16), w_ref[...], preferred_element_type=F32)

        def finish(v):
            if res is not None:
                v = r_ref[...] + scale * v
            o_ref[...] = v.astype(o_ref.dtype)

        if nk == 1:
            finish(part)
            return

        @pl.when(k == 0)
        def _():
            acc[...] = part

        @pl.when(k > 0)
        def _():
            acc[...] += part

        @pl.when(k == nk - 1)
        def _():
            finish(acc[...])

    in_specs = [pl.BlockSpec((tm, tk), lambda i, n, k: (i, k)),
                pl.BlockSpec((None, tk, tn), lambda i, n, k: (n // npj, k, n % npj))]
    args = [a, w]
    if res is not None:
        in_specs.append(pl.BlockSpec((tm, tn), lambda i, n, k: (i, n)))
        args.append(res)
    (out,), extra = _hosted(
        name, body, grid, in_specs, [pl.BlockSpec((tm, tn), lambda i, n, k: (i, n))],
        [jax.ShapeDtypeStruct((M, J * Np), out_dtype)], [pltpu.VMEM((tm, tn), F32)], args,
        ("parallel", "parallel", "arbitrary"), rider, jobs)
    return out if rider is None else (out, extra)


def _mm_nt(name, dy, w, *, out_dtype=F32, tm=512, tn=2048, tk=768, scale=1.0, jobs=None):
    M, N = dy.shape
    J, K, Np = w.shape
    assert N == J * Np
    tm, tn, tk = _tile(M, tm), _tile(K, tn, LANE), _tile(Np, tk, LANE)
    npj = Np // tk
    nc = J * npj

    def body(a_ref, w_ref, o_ref, acc):
        c = pl.program_id(2)
        part = lax.dot_general(a_ref[...].astype(BF16), w_ref[...], (((1,), (1,)), ((), ())),
                               preferred_element_type=F32)
        if nc == 1:
            o_ref[...] = (scale * part).astype(o_ref.dtype)
            return

        @pl.when(c == 0)
        def _():
            acc[...] = part

        @pl.when(c > 0)
        def _():
            acc[...] += part

        @pl.when(c == nc - 1)
        def _():
            o_ref[...] = (scale * acc[...]).astype(o_ref.dtype)

    (out,), _ = _hosted(
        name, body, (M // tm, K // tn, nc),
        [pl.BlockSpec((tm, tk), lambda i, n, c: (i, c)),
         pl.BlockSpec((None, tn, tk), lambda i, n, c: (c // npj, n, c % npj))],
        [pl.BlockSpec((tm, tn), lambda i, n, c: (i, n))], [jax.ShapeDtypeStruct((M, K), out_dtype)],
        [pltpu.VMEM((tm, tn), F32)], (dy, w), ("parallel", "parallel", "arbitrary"), None, jobs)
    return out


def _mm_tn(name, x, dy, J, *, tm=1024, tko=1024, tn=768, scale=1.0, jobs=None):
    M, K = x.shape
    M2, N = dy.shape
    assert M == M2 and N % J == 0
    Np = N // J
    tm, tko, tn = _tile(M, tm, LANE), _tile(K, tko, LANE), _tile(Np, tn, LANE)
    npj = Np // tn
    nm = M // tm

    def body(x_ref, d_ref, o_ref, acc):
        m = pl.program_id(2)
        part = lax.dot_general(x_ref[...].astype(BF16), d_ref[...].astype(BF16), (((0,), (0,)), ((), ())),
                               preferred_element_type=F32)
        if nm == 1:
            o_ref[...] = scale * part
            return

        @pl.when(m == 0)
        def _():
            acc[...] = part

        @pl.when(m > 0)
        def _():
            acc[...] += part

        @pl.when(m == nm - 1)
        def _():
            o_ref[...] = scale * acc[...]

    (out,), _ = _hosted(
        name, body, (K // tko, J * npj, nm),
        [pl.BlockSpec((tm, tko), lambda k, n, m: (m, k)), pl.BlockSpec((tm, tn), lambda k, n, m: (m, n))],
        [pl.BlockSpec((None, tko, tn), lambda k, n, m: (n // npj, k, n % npj))],
        [jax.ShapeDtypeStruct((J, K, Np), F32)], [pltpu.VMEM((tko, tn), F32)], (x, dy),
        ("parallel", "parallel", "arbitrary"), None, jobs)
    return out


def _swiglu_act(g, u):
    return jax.nn.silu(g) * u


def _ffn_up(name, xn, wgu, *, tm=1024, rider=None):
    M, K = xn.shape
    J, _, F2 = wgu.shape
    F = F2 // 2
    tm = _tile(M, tm)

    def body(a_ref, w_ref, gu_ref, h_ref):
        r = jnp.dot(a_ref[...], w_ref[...], preferred_element_type=F32)
        gu_ref[...] = r.astype(gu_ref.dtype)
        h_ref[...] = _swiglu_act(r[:, :F], r[:, F:]).astype(h_ref.dtype)

    (gu, hid), extra = _hosted(
        name, body, (M // tm, J),
        [pl.BlockSpec((tm, K), lambda i, j: (i, 0)), pl.BlockSpec((None, K, F2), lambda i, j: (j, 0, 0))],
        [pl.BlockSpec((tm, F2), lambda i, j: (i, j)), pl.BlockSpec((tm, F), lambda i, j: (i, j))],
        [jax.ShapeDtypeStruct((M, J * F2), BF16), jax.ShapeDtypeStruct((M, J * F), BF16)], [], (xn, wgu),
        ("parallel", "parallel"), rider)
    return gu, hid, extra


def _ffn_down_dx(name, dout, wd, gu, J, *, scale, tm=512, jobs=None):
    M, D = dout.shape
    F = wd.shape[1] // J
    tm = _tile(M, tm)

    def body(d_ref, w_ref, gu_ref, o_ref):
        dh = scale * lax.dot_general(d_ref[...], w_ref[...], (((1,), (1,)), ((), ())), preferred_element_type=F32)
        gu = gu_ref[...].astype(F32)
        _, vjp = jax.vjp(_swiglu_act, gu[:, :F], gu[:, F:])
        o_ref[...] = jnp.concatenate(vjp(dh), axis=-1).astype(o_ref.dtype)

    (out,), _ = _hosted(
        name, body, (M // tm, J),
        [pl.BlockSpec((tm, D), lambda i, j: (i, 0)), pl.BlockSpec((None, F, D), lambda i, j: (0, j, 0)),
         pl.BlockSpec((tm, 2 * F), lambda i, j: (i, j))],
        [pl.BlockSpec((tm, 2 * F), lambda i, j: (i, j))], [jax.ShapeDtypeStruct((M, J * 2 * F), BF16)], [],
        (dout, wd, gu), ("parallel", "parallel"), None, jobs)
    return out


def _all_gather(name, shards):
    n = len(shards)

    def body(*refs):
        start, forward, finish = _gather_phases(refs[:n], refs[n:2 * n], *refs[2 * n:])
        start()
        forward()
        finish()

    any_spec = pl.BlockSpec(memory_space=pl.ANY)
    return pl.pallas_call(
        body, name=name, in_specs=[any_spec] * n, out_specs=[any_spec] * n,
        out_shape=[jax.ShapeDtypeStruct((NDEV,) + s.shape, s.dtype) for s in shards],
        scratch_shapes=_gather_sems(n),
    )(*shards)


def _gather_sems(n):
    return [pltpu.SemaphoreType.DMA((n, 7)), pltpu.SemaphoreType.DMA((n, 7)), pltpu.SemaphoreType.DMA((n,))]


def _gather_phases(ins, outs, send_sems, recv_sems, local_sems):
    n = len(ins)
    x, y, c = lax.axis_index("x"), lax.axis_index("y"), lax.axis_index("c")
    me, sibling = (x, y, c), (x, y, 1 - c)
    chips = [(1 - x, y), (x, 1 - y), (1 - x, 1 - y)]

    def blk(i, px, py, pc):
        return outs[i].at[4 * px + 2 * py + pc]

    def copy(i, k, block, to, src=None):
        return pltpu.make_async_remote_copy(
            src_ref=blk(i, *block) if src is None else src, dst_ref=blk(i, *block),
            send_sem=send_sems.at[i, k], recv_sem=recv_sems.at[i, k], device_id=to, device_id_type=MESH)

    def local(i):
        return pltpu.make_async_copy(ins[i], blk(i, *me), local_sems.at[i])

    def firsts(i):
        return [copy(i, 0, me, sibling, src=ins[i])] + [copy(i, 1 + j, me, (*chip, c), src=ins[i])
                                                        for j, chip in enumerate(chips)]

    def start():
        for i in range(n):
            local(i).start()
        for i in range(n):
            for cp in firsts(i):
                cp.start()

    def forward():
        for i in range(n):
            for j, chip in enumerate(chips):
                copy(i, 1 + j, (*chip, c), me).wait_recv()
                copy(i, 4 + j, (*chip, c), sibling).start()

    def finish():
        for i in range(n):
            copy(i, 0, sibling, me).wait_recv()
            for j, chip in enumerate(chips):
                copy(i, 4 + j, (*chip, 1 - c), me).wait_recv()
        for i in range(n):
            for cp in firsts(i):
                cp.wait_send()
            for j, chip in enumerate(chips):
                copy(i, 4 + j, (*chip, c), sibling).wait_send()
            local(i).wait()

    return start, forward, finish


class _GatherRider:
    def __init__(self, shards):
        self.operands = list(shards)
        n = len(self.operands)
        self.n_in = self.n_out = n
        any_spec = pl.BlockSpec(memory_space=pl.ANY)
        self.in_specs = [any_spec] * n
        self.out_specs = [any_spec] * n
        self.out_shape = [jax.ShapeDtypeStruct((NDEV,) + s.shape, s.dtype) for s in self.operands]
        self.scratch = _gather_sems(n)

    def bind(self, step_of, nsteps):
        return self

    def take(self, outs):
        return list(outs)

    def run(self, ins, outs, sems, step, nsteps):
        start, forward, finish = _gather_phases(ins, outs, *sems)
        pl.when(step == 0)(start)
        pl.when(step == nsteps // 2)(forward)
        pl.when(step == nsteps - 1)(finish)


class _SwapRider:
    def __init__(self, arr, streams, grid, tile, out_shape, out_block, out_map):
        self.arr, self.streams, self.grid, self.tile = arr, streams, grid, tile
        self.ns, self.n = len(streams), grid[0] * grid[1]
        self.operands = [arr] * (2 * self.ns)
        self.n_in, self.n_out = 2 * self.ns, 1
        self.out_shape = [jax.ShapeDtypeStruct(out_shape, F32)]
        self.out_block, self.out_map = out_block, out_map
        tr, C = tile
        slots = [pltpu.VMEM((2, tr, C), w) for _, w, _, _ in streams]
        self.scratch = slots + slots + [pltpu.SemaphoreType.DMA((self.ns, 2)), pltpu.SemaphoreType.DMA((self.ns, 2)),
                                        pltpu.SemaphoreType.REGULAR((self.ns,))]

    def bind(self, step_of, nsteps):
        assert nsteps >= self.n
        self.period = period = nsteps // self.n
        n, nr = self.n, self.grid[1]

        def ids(*g):
            k = jnp.minimum(step_of(*g) // period, n - 1)
            pos = {a: lax.axis_index(a) for a in AXES}
            return k // nr, k % nr, [v for a in AXES for v in (pos[a], 1 - pos[a])]

        block = (None,) * (self.arr.ndim - 2) + tuple(self.tile)
        self.in_specs = []
        for _, _, keep_map, send_map in self.streams:
            for m in (keep_map, send_map):
                self.in_specs.append(pl.BlockSpec(block, functools.partial(lambda *g, m: m(*ids(*g)), m=m)))
        self.out_specs = [pl.BlockSpec(self.out_block, lambda *g: self.out_map(*ids(*g)))]
        return self

    def take(self, outs):
        return outs[0]

    def run(self, ins, outs, scratch, step, nsteps):
        ns, n, period = self.ns, self.n, self.period
        keeps, sends, o_ref = ins[0::2], ins[1::2], outs[0]
        lands, stages = scratch[:ns], scratch[ns:2 * ns]
        send_sems, recv_sems, credits = scratch[2 * ns:]
        k = step // period
        slot = k % 2
        here = {a: lax.axis_index(a) for a in AXES}
        peers = [tuple(1 - here[a] if a == axis else here[a] for a in AXES) for axis, _, _, _ in self.streams]

        def rdma(s):
            return pltpu.make_async_remote_copy(
                src_ref=stages[s].at[slot], dst_ref=lands[s].at[slot], send_sem=send_sems.at[s, slot],
                recv_sem=recv_sems.at[s, slot], device_id=peers[s], device_id_type=MESH)

        @pl.when((k < n) & (step % period == 0))
        def _():
            @pl.when(k >= 2)
            def _():
                for s in range(ns):
                    pl.semaphore_wait(credits.at[s], 1)

            for s in range(ns):
                stages[s][slot] = sends[s][...].astype(stages[s].dtype)
                rdma(s).start()

        @pl.when((k < n) & (step % period == period - 1))
        def _():
            for s in range(ns):
                rdma(s).wait_recv()
                total = keeps[s][...] + lands[s][slot].astype(F32)
                if ns == 1:
                    o_ref[...] = total
                else:
                    o_ref[s] = total
            for s in range(ns):
                rdma(s).wait_send()

            @pl.when(k + 2 < n)
            def _():
                for s in range(ns):
                    pl.semaphore_signal(credits.at[s], inc=1, device_id=peers[s], device_id_type=MESH)


def _run_alone(name, rider):
    rider.bind(lambda t: t, rider.n)

    def body(*refs):
        a, b = rider.n_in, rider.n_in + rider.n_out
        rider.run(refs[:a], refs[a:b], refs[b:], pl.program_id(0), rider.n)

    outs = pl.pallas_call(
        body, name=name, grid=(rider.n,), in_specs=rider.in_specs, out_specs=rider.out_specs,
        out_shape=rider.out_shape, scratch_shapes=rider.scratch, compiler_params=_cp(("arbitrary",)),
    )(*rider.operands)
    return rider.take(outs)


class _Scatter:
    def __init__(self, name, g):
        self.name, self.cur, self.stage = name, g, 0
        _, self.R, self.C = g.shape

    def done(self):
        return self.stage == 3

    def rider(self, rows):
        R, C = self.R, self.C
        R2 = R // 2
        tr = _tile(R2, rows, 16)
        nrh = R2 // tr
        if self.stage == 0:
            return _SwapRider(
                self.cur.reshape(4, 2, R, C),
                [("c", BF16, lambda b, i, s: (b, s[4], i, 0), lambda b, i, s: (b, s[5], i, 0))],
                (4, 2 * nrh), (tr, C), (2, 4, R2, C), (None, None, tr, C), lambda b, i, s: (i // nrh, b, i % nrh, 0))
        if self.stage == 1:
            return _SwapRider(
                self.cur.reshape(2, 2, 2, R2, C),
                [("y", BF16, lambda b, i, s: (0, b, s[2], i, 0), lambda b, i, s: (0, b, s[3], i, 0)),
                 ("x", BF16, lambda b, i, s: (1, s[0], b, i, 0), lambda b, i, s: (1, s[1], b, i, 0))],
                (2, nrh), (tr, C), (2, 2, R2, C), (2, None, tr, C), lambda b, i, s: (0, b, i, 0))
        return _SwapRider(
            self.cur,
            [("x", BF16, lambda b, i, s: (0, s[0], i, 0), lambda b, i, s: (0, s[1], i, 0)),
             ("y", BF16, lambda b, i, s: (1, s[2], i, 0), lambda b, i, s: (1, s[3], i, 0))],
            (1, nrh), (tr, C), (2, R2, C), (2, tr, C), lambda b, i, s: (0, i, 0))

    def advance(self, out):
        self.cur, self.stage = out, self.stage + 1

    def finish(self):
        while not self.done():
            self.advance(_run_alone(f"{self.name}_s{self.stage}", self.rider(256)))
        return self.cur.reshape(self.R, self.C)


def _pick(jobs, nsteps, rows=128):
    for job in sorted(jobs or (), key=lambda j: -j.R * j.C):
        if not job.done():
            rider = job.rider(rows)
            if rider.n <= nsteps:
                return job, rider
    return None, None


def _sum8(name, g):
    _, R, C = g.shape
    tr = _tile(R, 512)

    def body(g_ref, o_ref):
        acc = g_ref[0]
        for d in range(1, NDEV):
            acc = acc + g_ref[d]
        o_ref[...] = acc

    return pl.pallas_call(
        body, name=name, grid=(R // tr,), in_specs=[pl.BlockSpec((NDEV, tr, C), lambda i: (0, i, 0))],
        out_specs=pl.BlockSpec((tr, C), lambda i: (i, 0)), out_shape=jax.ShapeDtypeStruct((R, C), F32),
        compiler_params=_cp(("parallel",)),
    )(g)


def _adamw_math(w, g, m, v):
    m = B1 * m + (1.0 - B1) * g
    v = B2 * v + (1.0 - B2) * jnp.square(g)
    m_hat = m / (1.0 - B1 ** STEP)
    v_hat = v / (1.0 - B2 ** STEP)
    delta = -LR * (m_hat / (jnp.sqrt(v_hat) + EPS) + WD * w)
    return delta, m, v


def _adamw(name, w, m, v, gp, *, tr, cw, gw, goff=0):
    R, C = w.shape
    nc = C // cw
    nr = R // tr

    def body(w_ref, m_ref, v_ref, g_ref, g_out, d_out, m_out, v_out):
        g = g_ref[...][:, :cw]
        d, mn, vn = _adamw_math(w_ref[...], g, m_ref[...], v_ref[...])
        g_out[...] = g
        d_out[...] = d
        m_out[...] = mn
        v_out[...] = vn

    wspec = pl.BlockSpec((tr, cw), lambda i, j: (i, j))
    gspec = pl.BlockSpec((tr, gw), lambda i, j: (i, goff + j))
    return pl.pallas_call(
        body, name=name, grid=(nr, nc), in_specs=[wspec, wspec, wspec, gspec], out_specs=[wspec] * 4,
        out_shape=[jax.ShapeDtypeStruct((R, C), F32)] * 4, compiler_params=_cp(("parallel", "parallel")),
    )(w, m, v, gp)


def _adamw_small(name, w, m, v, g):
    R, C = w.shape

    def body(w_ref, m_ref, v_ref, g_ref, d_out, m_out, v_out):
        d, mn, vn = _adamw_math(w_ref[...], g_ref[...], m_ref[...], v_ref[...])
        d_out[...] = d
        m_out[...] = mn
        v_out[...] = vn

    tr = _tile(R, 512)
    spec = pl.BlockSpec((tr, C), lambda i: (i, 0))
    return pl.pallas_call(
        body, name=name, grid=(R // tr,), in_specs=[spec] * 4, out_specs=[spec] * 3,
        out_shape=[jax.ShapeDtypeStruct((R, C), F32)] * 3, compiler_params=_cp(("parallel",)),
    )(w, m, v, g)


def _prep(name, parts, rows_p, cols_p):
    R, C = parts[0].shape
    n = len(parts)

    def body(*refs):
        o_ref = refs[n]
        if (R, C) != (rows_p, cols_p):
            o_ref[...] = jnp.zeros_like(o_ref)
        for i in range(n):
            o_ref[0:R, i * cols_p:i * cols_p + C] = refs[i][...].astype(BF16)

    return pl.pallas_call(
        body, name=name, out_shape=jax.ShapeDtypeStruct((rows_p, n * cols_p), BF16), compiler_params=_cp(),
    )(*parts)


def _attn_masks():
    lane = lax.broadcasted_iota(jnp.int32, (1, LANE), 1)
    return [(lane < HEAD_DIM), (lane >= HEAD_DIM)]


def _band_valid(base):
    qi = lax.broadcasted_iota(jnp.int32, (QB, 2 * QB), 0)
    ki = lax.broadcasted_iota(jnp.int32, (QB, 2 * QB), 1)
    dist = qi + QB - ki
    return (dist >= 0) & (dist <= QB) & (base + ki - QB >= 0)


ATTN_T = max(DILATIONS) * QB


def _attn_groups(T):
    out = []
    for d in DILATIONS:
        for r in range(d):
            for i in range(T // (d * QB)):
                qrows = pl.ds(r + d * i * QB, QB, stride=d) if d > 1 else pl.ds(i * QB, QB)
                k0 = T + r + d * (i - 1) * QB
                krows = pl.ds(k0, 2 * QB, stride=d) if d > 1 else pl.ds(k0, 2 * QB)
                out.append((d, qrows, krows, i * QB))
    return out


def _attn_specs(T, width_off):
    cur = pl.BlockSpec((T, LANE), lambda hp, b: (b, width_off + hp))
    prev = pl.BlockSpec((T, LANE), lambda hp, b: (jnp.maximum(b - 1, 0), width_off + hp))
    return cur, prev


def _attn_fwd(z):
    S, ZW = z.shape
    T = min(ATTN_T, S)
    scale = HEAD_DIM ** -0.5
    groups = _attn_groups(T)

    def body(q_ref, kc_ref, kp_ref, vc_ref, vp_ref, y_ref, m_ref, l_ref, kcat, vcat):
        b = pl.program_id(1)
        kcat[0:T, :] = kp_ref[...]
        kcat[T:, :] = kc_ref[...]
        vcat[0:T, :] = vp_ref[...]
        vcat[T:, :] = vc_ref[...]
        masks = _attn_masks()
        for d, qrows, krows, l0 in groups:
            q = q_ref[qrows, :]
            kk = kcat[krows, :].astype(BF16)
            vv = vcat[krows, :].astype(BF16)
            valid = _band_valid(b * (T // d) + l0)
            o_new = m_new = l_new = None
            for hm in masks:
                qh = jnp.where(hm, q, 0.0).astype(BF16)
                s = lax.dot_general(qh, kk, (((1,), (1,)), ((), ())), preferred_element_type=F32) * scale
                s = jnp.where(valid, s, MASK_VALUE)
                m = jnp.max(s, axis=-1, keepdims=True)
                p = jnp.exp(s - m)
                l = jnp.sum(p, axis=-1, keepdims=True)
                o = jnp.dot(p.astype(BF16), vv, preferred_element_type=F32)
                if o_new is None:
                    o_new, m_new, l_new = o, jnp.broadcast_to(m, (QB, LANE)), jnp.broadcast_to(l, (QB, LANE))
                else:
                    o_new = jnp.where(hm, o, o_new)
                    m_new = jnp.where(hm, m, m_new)
                    l_new = jnp.where(hm, l, l_new)
            if d == DILATIONS[0]:
                y_ref[qrows, :] = o_new
                m_ref[qrows, :] = m_new
                l_ref[qrows, :] = l_new
            else:
                m_old = m_ref[qrows, :]
                m_all = jnp.maximum(m_old, m_new)
                w_old, w_new = jnp.exp(m_old - m_all), jnp.exp(m_new - m_all)
                y_ref[qrows, :] = w_old * y_ref[qrows, :] + w_new * o_new
                l_ref[qrows, :] = w_old * l_ref[qrows, :] + w_new * l_new
                m_ref[qrows, :] = m_all
        y_ref[...] = y_ref[...] / l_ref[...]

    qc, _ = _attn_specs(T, 0)
    kc, kp = _attn_specs(T, ATTN_W // LANE)
    vc, vp = _attn_specs(T, 2 * ATTN_W // LANE)
    shp = jax.ShapeDtypeStruct((S, ATTN_W), F32)
    return pl.pallas_call(
        body, name="attn_fwd", grid=(ATTN_W // LANE, S // T),
        in_specs=[qc, kc, kp, vc, vp], out_specs=[qc, qc, qc], out_shape=[shp, shp, shp],
        scratch_shapes=[pltpu.VMEM((2 * T, LANE), F32), pltpu.VMEM((2 * T, LANE), F32)],
        compiler_params=_cp(("parallel", "parallel")),
    )(z, z, z, z, z)


def _attn_bwd(z, dya, ya, mg, den):
    S, ZW = z.shape
    T = min(ATTN_T, S)
    scale = HEAD_DIM ** -0.5
    groups = _attn_groups(T)

    def body(q_ref, kc_ref, kp_ref, vc_ref, vp_ref, dy_ref, y_ref, m_ref, n_ref, dq_ref, dk_ref, dv_ref,
             kcat, vcat, dkcat, dvcat):
        b = pl.program_id(1)

        @pl.when(b == 0)
        def _():
            dk_ref[...] = jnp.zeros_like(dk_ref)
            dv_ref[...] = jnp.zeros_like(dv_ref)

        kcat[0:T, :] = kp_ref[...]
        kcat[T:, :] = kc_ref[...]
        vcat[0:T, :] = vp_ref[...]
        vcat[T:, :] = vc_ref[...]
        dkcat[...] = jnp.zeros_like(dkcat)
        dvcat[...] = jnp.zeros_like(dvcat)
        dq_ref[...] = jnp.zeros_like(dq_ref)
        masks = _attn_masks()
        for d, rows, krows, l0 in groups:
            q, dy, y = q_ref[rows, :], dy_ref[rows, :], y_ref[rows, :]
            mrow, nrow = m_ref[rows, :], n_ref[rows, :]
            kk = kcat[krows, :].astype(BF16)
            vv = vcat[krows, :].astype(BF16)
            valid = _band_valid(b * (T // d) + l0)
            dq_acc = jnp.zeros((QB, LANE), F32)
            dk_acc = jnp.zeros((2 * QB, LANE), F32)
            dv_acc = jnp.zeros((2 * QB, LANE), F32)
            for hm in masks:
                qh = jnp.where(hm, q, 0.0).astype(BF16)
                dyh = jnp.where(hm, dy, 0.0)
                dyb = dyh.astype(BF16)
                dsum = jnp.sum(dyh * y, axis=-1, keepdims=True)
                mh = jnp.max(jnp.where(hm, mrow, MASK_VALUE), axis=-1, keepdims=True)
                nh = jnp.max(jnp.where(hm, nrow, 0.0), axis=-1, keepdims=True)
                s = lax.dot_general(qh, kk, (((1,), (1,)), ((), ())), preferred_element_type=F32) * scale
                p = jnp.where(valid, jnp.exp(s - mh), 0.0) / nh
                pb = p.astype(BF16)
                dv_h = lax.dot_general(pb, dyb, (((0,), (0,)), ((), ())), preferred_element_type=F32)
                dp = lax.dot_general(dyb, vv, (((1,), (1,)), ((), ())), preferred_element_type=F32)
                ds = (p * (dp - dsum) * scale).astype(BF16)
                dq_h = jnp.dot(ds, kk, preferred_element_type=F32)
                dk_h = lax.dot_general(ds, qh, (((0,), (0,)), ((), ())), preferred_element_type=F32)
                dq_acc += jnp.where(hm, dq_h, 0.0)
                dk_acc += dk_h
                dv_acc += dv_h
            dq_ref[rows, :] += dq_acc
            dkcat[krows, :] += dk_acc
            dvcat[krows, :] += dv_acc

        base = pl.multiple_of(b * T, T)
        dk_ref[pl.ds(base, T), :] += dkcat[T:, :]
        dv_ref[pl.ds(base, T), :] += dvcat[T:, :]

        @pl.when(b > 0)
        def _():
            prev = pl.multiple_of(b * T - T, T)
            dk_ref[pl.ds(prev, T), :] += dkcat[0:T, :]
            dv_ref[pl.ds(prev, T), :] += dvcat[0:T, :]

    qc, _ = _attn_specs(T, 0)
    kc, kp = _attn_specs(T, ATTN_W // LANE)
    vc, vp = _attn_specs(T, 2 * ATTN_W // LANE)
    whole = pl.BlockSpec((S, LANE), lambda hp, b: (0, hp))
    shp = jax.ShapeDtypeStruct((S, ATTN_W), F32)
    return pl.pallas_call(
        body, name="attn_bwd", grid=(ATTN_W // LANE, S // T),
        in_specs=[qc, kc, kp, vc, vp, qc, qc, qc, qc], out_specs=[qc, whole, whole], out_shape=[shp, shp, shp],
        scratch_shapes=[pltpu.VMEM((2 * T, LANE), F32)] * 4,
        compiler_params=_cp(("parallel", "arbitrary")),
    )(z, z, z, z, z, dya, ya, mg, den)


def _ssm_disc(lr, li, logdt, br, bi):
    dt = jnp.exp(logdt)
    mag = jnp.exp(lr * dt)
    ar = mag * jnp.cos(li * dt)
    ai = mag * jnp.sin(li * dt)
    nr, ni = ar - 1.0, ai
    den = lr * lr + li * li
    cr = (nr * lr + ni * li) / den
    ci = (ni * lr - nr * li) / den
    return ar, ai, cr * br - ci * bi, cr * bi + ci * br


def _ssm_prep(lr, li, logdt, br, bi):
    n, c = br.shape
    outs, _ = _rowwise("ssm_prep", lambda *a: (list(_ssm_disc(*a)), []), n, _tile(n, 512),
                       [_full(a) for a in (lr, li, logdt, br, bi)], [],
                       [(1, 1, _c0, F32), (1, 1, _c0, F32), (c, c, _c0, F32), (c, c, _c0, F32)])
    return outs


def _ssm_prep_bwd(lr, li, logdt, br, bi, dar, dai, dbbr, dbbi):
    n, c = br.shape

    def f(lrb, lib, dtb, brb, bib, *cts):
        _, vjp = jax.vjp(_ssm_disc, lrb, lib, dtb, brb, bib)
        return list(vjp(cts)), []

    outs, _ = _rowwise("ssm_prep_bwd", f, n, _tile(n, 512),
                       [_full(a) for a in (lr, li, logdt, br, bi, dar, dai, dbbr, dbbi)], [],
                       [(1, 1, _c0, F32)] * 3 + [(c, c, _c0, F32)] * 2)
    return outs


def _cmul(ar, ai, br, bi):
    return ar * br - ai * bi, ar * bi + ai * br


def _scan_consts(ar, ai, reverse):
    w = ar.shape[-1]
    a1 = (jnp.broadcast_to(ar, (8, w)), jnp.broadcast_to(ai, (8, w)))
    a2 = _cmul(*a1, *a1)
    a4 = _cmul(*a2, *a2)
    a8 = _cmul(*a4, *a4)
    row = lax.broadcasted_iota(jnp.int32, (8, w), 0)
    e = (8 - row) if reverse else (row + 1)
    one, zero = jnp.ones((8, w), F32), jnp.zeros((8, w), F32)
    pw = (one, zero)
    for bit, ap in ((1, a1), (2, a2), (4, a4), (8, a8)):
        sel = (e & bit) != 0
        nxt = _cmul(*pw, *ap)
        pw = (jnp.where(sel, nxt[0], pw[0]), jnp.where(sel, nxt[1], pw[1]))
    steps = []
    for sh, (pr, pi) in zip((1, 2, 4), (a1, a2, a4)):
        keep = (row < 8 - sh) if reverse else (row >= sh)
        steps.append((jnp.where(keep, pr, 0.0), jnp.where(keep, pi, 0.0)))
    return steps, pw, row


def _scan_group(xr, xi, cr, ci, consts, reverse):
    steps, pw, _ = consts
    for sh, (pr, pi) in zip((1, 2, 4), steps):
        by = 8 - sh if reverse else sh
        tr_, ti_ = _cmul(pr, pi, pltpu.roll(xr, by, 0), pltpu.roll(xi, by, 0))
        xr = xr + tr_
        xi = xi + ti_
    tr_, ti_ = _cmul(pw[0], pw[1], cr, ci)
    return xr + tr_, xi + ti_


def _ssm_fwd(z, a_r, a_i, bdr, bdi, cmr, cmi, dskip, ts, rider=None):
    S, ZW = z.shape
    NS = SSM_G * SSM_P
    PW = PACK * SSM_P
    uoff = (ZW - SSM_W) // LANE
    nsteps = S // ts

    def body(u_ref, ar_ref, ai_ref, bdr_ref, bdi_ref, cmr_ref, cmi_ref, d_ref, hr_ref, hi_ref, y_ref, car_r, car_i):
        s = pl.program_id(1)

        @pl.when(s == 0)
        def _():
            car_r[...] = jnp.zeros_like(car_r)
            car_i[...] = jnp.zeros_like(car_i)

        u = u_ref[...]
        ub = u.astype(BF16)
        nt = (((1,), (1,)), ((), ()))
        hr_ref[...] = lax.dot_general(ub, bdr_ref[...], nt, preferred_element_type=F32)
        hi_ref[...] = lax.dot_general(ub, bdi_ref[...], nt, preferred_element_type=F32)
        consts = _scan_consts(ar_ref[...], ai_ref[...], False)

        def step(j, carry):
            rows = pl.ds(pl.multiple_of(j * 8, 8), 8)
            hr, hi = _scan_group(hr_ref[rows, :], hi_ref[rows, :], carry[0], carry[1], consts, False)
            hr_ref[rows, :] = hr
            hi_ref[rows, :] = hi
            return jnp.broadcast_to(hr[7:8, :], (8, PW)), jnp.broadcast_to(hi[7:8, :], (8, PW))

        cr, ci = lax.fori_loop(0, ts // 8, step, (car_r[...], car_i[...]))
        car_r[...] = cr
        car_i[...] = ci
        y = lax.dot_general(hr_ref[...].astype(BF16), cmr_ref[...], nt, preferred_element_type=F32)
        y -= lax.dot_general(hi_ref[...].astype(BF16), cmi_ref[...], nt, preferred_element_type=F32)
        y_ref[...] = y + d_ref[...] * u

    row_a = pl.BlockSpec((1, PW), lambda i, s: (0, i))
    (hr, hi, y), extra = _hosted(
        "ssm_fwd", body, (SSM_G // PACK, nsteps),
        [pl.BlockSpec((ts, LANE), lambda i, s: (s, uoff + i)), row_a, row_a,
         pl.BlockSpec((None, PW, LANE), lambda i, s: (i, 0, 0)), pl.BlockSpec((None, PW, LANE), lambda i, s: (i, 0, 0)),
         pl.BlockSpec((None, LANE, PW), lambda i, s: (i, 0, 0)), pl.BlockSpec((None, LANE, PW), lambda i, s: (i, 0, 0)),
         pl.BlockSpec((1, LANE), lambda i, s: (0, i))],
        [pl.BlockSpec((ts, PW), lambda i, s: (s, i)), pl.BlockSpec((ts, PW), lambda i, s: (s, i)),
         pl.BlockSpec((ts, LANE), lambda i, s: (s, i))],
        [jax.ShapeDtypeStruct((S, NS), F32), jax.ShapeDtypeStruct((S, NS), F32), jax.ShapeDtypeStruct((S, SSM_W), F32)],
        [pltpu.VMEM((8, PW), F32), pltpu.VMEM((8, PW), F32)], (z, a_r, a_i, bdr, bdi, cmr, cmi, dskip),
        ("parallel", "arbitrary"), rider)
    return hr, hi, y, extra


def _ssm_bwd(z, dyp, hr, hi, a_r, a_i, bdr, bdi, cmr, cmi, dskip, ts, jobs=None):
    S, ZW = z.shape
    NS = SSM_G * SSM_P
    PW = PACK * SSM_P
    uoff = (ZW - SSM_W) // LANE
    nsteps = S // ts
    npk = SSM_G // PACK

    def body(u_ref, dy_ref, hr_ref, hi_ref, hpr_ref, hpi_ref, ar_ref, ai_ref, bdr_ref, bdi_ref, cmr_ref, cmi_ref,
             d_ref, du_ref, dbdr_ref, dbdi_ref, dcmr_ref, dcmi_ref, dar_ref, dai_ref, dd_ref,
             lr_s, li_s, hcr, hci, car_r, car_i):
        s = pl.program_id(1)
        first_tile = s == nsteps - 1

        @pl.when(s == 0)
        def _():
            car_r[...] = jnp.zeros_like(car_r)
            car_i[...] = jnp.zeros_like(car_i)
            for r in (dbdr_ref, dbdi_ref, dcmr_ref, dcmi_ref, dar_ref, dai_ref, dd_ref):
                r[...] = jnp.zeros_like(r)

        u, dy = u_ref[...], dy_ref[...]
        ub, dyb = u.astype(BF16), dy.astype(BF16)
        lr_s[...] = jnp.dot(dyb, cmr_ref[...], preferred_element_type=F32)
        li_s[...] = -jnp.dot(dyb, cmi_ref[...], preferred_element_type=F32)
        keep_prev = jnp.where(first_tile, 0.0, 1.0)
        hcr[0:8, :] = hpr_ref[...] * keep_prev
        hci[0:8, :] = hpi_ref[...] * keep_prev
        hcr[8:, :] = hr_ref[...]
        hci[8:, :] = hi_ref[...]
        consts = _scan_consts(ar_ref[...], -ai_ref[...], True)
        row = consts[2]
        ngrp = ts // 8

        def step(jj, carry):
            cr, ci, accr, acci = carry
            j = ngrp - 1 - jj
            rows = pl.ds(pl.multiple_of(j * 8, 8), 8)
            nxt = pl.ds(pl.multiple_of(j * 8 + 8, 8), 8)
            lr, li = _scan_group(lr_s[rows, :], li_s[rows, :], cr, ci, consts, True)
            lr_s[rows, :] = lr
            li_s[rows, :] = li
            pr, pi = hcr[rows, :], hci[rows, :]
            hsr = jnp.where(row == 0, jnp.broadcast_to(pr[7:8, :], (8, PW)), pltpu.roll(hcr[nxt, :], 1, 0))
            hsi = jnp.where(row == 0, jnp.broadcast_to(pi[7:8, :], (8, PW)), pltpu.roll(hci[nxt, :], 1, 0))
            accr = accr + lr * hsr + li * hsi
            acci = acci + li * hsr - lr * hsi
            return jnp.broadcast_to(lr[0:1, :], (8, PW)), jnp.broadcast_to(li[0:1, :], (8, PW)), accr, acci

        zero = jnp.zeros((8, PW), F32)
        cr, ci, accr, acci = lax.fori_loop(0, ngrp, step, (car_r[...], car_i[...], zero, zero))
        car_r[...] = cr
        car_i[...] = ci
        dar_ref[...] += jnp.sum(accr, axis=0, keepdims=True)
        dai_ref[...] += jnp.sum(acci, axis=0, keepdims=True)
        lrb, lib = lr_s[...].astype(BF16), li_s[...].astype(BF16)
        du = jnp.dot(lrb, bdr_ref[...], preferred_element_type=F32)
        du += jnp.dot(lib, bdi_ref[...], preferred_element_type=F32)
        du_ref[...] = du + dy * d_ref[...]
        tn = (((0,), (0,)), ((), ()))
        dbdr_ref[...] += lax.dot_general(lrb, ub, tn, preferred_element_type=F32)
        dbdi_ref[...] += lax.dot_general(lib, ub, tn, preferred_element_type=F32)
        dcmr_ref[...] += lax.dot_general(dyb, hr_ref[...].astype(BF16), tn, preferred_element_type=F32)
        dcmi_ref[...] -= lax.dot_general(dyb, hi_ref[...].astype(BF16), tn, preferred_element_type=F32)
        dd_ref[...] += jnp.sum(dy * u, axis=0, keepdims=True)

    rev = lambda s: nsteps - 1 - s
    row_a = pl.BlockSpec((1, PW), lambda i, s: (0, i))
    tile = pl.BlockSpec((ts, PW), lambda i, s: (rev(s), i))
    prev8 = pl.BlockSpec((8, PW), lambda i, s: (jnp.maximum(rev(s) * (ts // 8) - 1, 0), i))
    cols = pl.BlockSpec((ts, LANE), lambda i, s: (rev(s), i))
    bd = pl.BlockSpec((None, PW, LANE), lambda i, s: (i, 0, 0))
    cm = pl.BlockSpec((None, LANE, PW), lambda i, s: (i, 0, 0))
    outs, _ = _hosted(
        "ssm_bwd", body, (npk, nsteps),
        [pl.BlockSpec((ts, LANE), lambda i, s: (rev(s), uoff + i)), cols, tile, tile, prev8, prev8,
         row_a, row_a, bd, bd, cm, cm, pl.BlockSpec((1, LANE), lambda i, s: (0, i))],
        [cols, bd, bd, cm, cm, row_a, row_a, pl.BlockSpec((1, LANE), lambda i, s: (0, i))],
        [jax.ShapeDtypeStruct((S, SSM_W), F32),
         jax.ShapeDtypeStruct((npk, PW, LANE), F32), jax.ShapeDtypeStruct((npk, PW, LANE), F32),
         jax.ShapeDtypeStruct((npk, LANE, PW), F32), jax.ShapeDtypeStruct((npk, LANE, PW), F32),
         jax.ShapeDtypeStruct((1, NS), F32), jax.ShapeDtypeStruct((1, NS), F32), jax.ShapeDtypeStruct((1, SSM_W), F32)],
        [pltpu.VMEM((ts, PW), F32), pltpu.VMEM((ts, PW), F32), pltpu.VMEM((ts + 8, PW), F32),
         pltpu.VMEM((ts + 8, PW), F32), pltpu.VMEM((8, PW), F32), pltpu.VMEM((8, PW), F32)],
        (z, dyp, hr, hi, hr, hi, a_r, a_i, bdr, bdi, cmr, cmi, dskip), ("parallel", "arbitrary"), None, jobs)
    return outs


def _block_diag(m4):
    npk, g, a, b = m4.shape
    eye = jnp.eye(g, dtype=m4.dtype)
    return (m4[:, :, :, None, :] * eye[None, :, None, :, None]).reshape(npk, g * a, g * b)


def _block_diag_take(m, a, b):
    npk = m.shape[0]
    m5 = m.reshape(npk, PACK, a, PACK, b)
    return jnp.stack([m5[:, g, :, g, :] for g in range(PACK)], axis=1)


def _mix_out(ya, ypre, gl, ga, gb, bglu):
    yg = jax.nn.gelu(ypre)
    yb = yg * jax.nn.sigmoid(gl + bglu)
    return jnp.concatenate([_rms(ya, ga), _rms(yb, gb)], axis=-1)


def _tail_loss(h3, gl, pe, gf, tgt):
    h4 = h3 + jax.nn.sigmoid(gl) * pe
    err = jnp.square(_rms(h4, gf) - tgt)
    return 0.5 * jnp.mean(err, axis=-1, keepdims=True)


def kernel(x, p, ffn1_norm, ffn1_w_gate, ffn1_w_up, ffn1_w_down, mix_norm, w_in, attn_out_norm, ssm_lambda_re, ssm_lambda_im, ssm_log_dt, ssm_b_re, ssm_b_im, ssm_c_re, ssm_c_im, ssm_d, ssm_w_glu, ssm_b_glu, ssm_out_norm, w_out, ffn2_norm, ffn2_w_gate, ffn2_w_up, ffn2_w_down, ple_norm, ple_w_gate, ple_w_proj, final_norm, loss_target, m_ffn1_norm, m_ffn1_w_gate, m_ffn1_w_up, m_ffn1_w_down, m_mix_norm, m_w_in, m_attn_out_norm, m_ssm_lambda_re, m_ssm_lambda_im, m_ssm_log_dt, m_ssm_b_re, m_ssm_b_im, m_ssm_c_re, m_ssm_c_im, m_ssm_d, m_ssm_w_glu, m_ssm_b_glu, m_ssm_out_norm, m_w_out, m_ffn2_norm, m_ffn2_w_gate, m_ffn2_w_up, m_ffn2_w_down, m_ple_norm, m_ple_w_gate, m_ple_w_proj, m_final_norm, v_ffn1_norm, v_ffn1_w_gate, v_ffn1_w_up, v_ffn1_w_down, v_mix_norm, v_w_in, v_attn_out_norm, v_ssm_lambda_re, v_ssm_lambda_im, v_ssm_log_dt, v_ssm_b_re, v_ssm_b_im, v_ssm_c_re, v_ssm_c_im, v_ssm_d, v_ssm_w_glu, v_ssm_b_glu, v_ssm_out_norm, v_w_out, v_ffn2_norm, v_ffn2_w_gate, v_ffn2_w_up, v_ffn2_w_down, v_ple_norm, v_ple_w_gate, v_ple_w_proj, v_final_norm):
    A = dict(locals())
    xs = x[0]
    ps = p[0, 0]
    tgt = loss_target[0]
    S, D = xs.shape
    FSH = ffn1_w_gate.shape[-1]
    FSP = -(-FSH // LANE) * LANE
    TR = _tile(S, 256)
    ZW = 3 * ATTN_W + SSM_W

    wgu1 = _prep("prep_gu1", [ffn1_w_gate[0], ffn1_w_up[0]], D, FSP)
    wgu2 = _prep("prep_gu2", [ffn2_w_gate[0], ffn2_w_up[0]], D, FSP)
    wd1 = _prep("prep_d1", [ffn1_w_down[0]], FSP, D)
    wd2 = _prep("prep_d2", [ffn2_w_down[0]], FSP, D)
    win = _prep("prep_in", [w_in[0]], D, w_in.shape[-1])
    wglu = _prep("prep_glu", [ssm_w_glu[0]], ssm_w_glu.shape[1], SSM_W)
    wout = _prep("prep_out", [w_out[0]], w_out.shape[1], D)
    wpg = _prep("prep_pg", [ple_w_gate[0]], ple_w_gate.shape[1], D)
    wpp = _prep("prep_pp", [ple_w_proj[0]], ple_w_proj.shape[1], ple_w_proj.shape[2])
    (Wgu1,) = _all_gather("ag_weights", [wgu1])
    rowstack = lambda w: w.reshape(1, w.shape[0] * w.shape[1], w.shape[2])

    def ffn_norm(tag, h, gain):
        return _rowwise(f"{tag}_norm", lambda a, g: ([_rms(a, g)], []), S, TR, [_full(h)], [gain], [(D, D, _c0, BF16)])[0][0]

    xn1 = ffn_norm("ffn1", xs, ffn1_norm)
    gu1, hid1, (Wd1, Win) = _ffn_up("ffn1_up", xn1, Wgu1, rider=_GatherRider([wd1, win]))
    Wd1 = rowstack(Wd1)
    h1, (Wd2,) = _mm_nn("ffn1_down", hid1, Wd1, tn=D, tk=2 * FSP, res=xs, scale=0.5, rider=_GatherRider([wd2]))
    Wd2 = rowstack(Wd2)

    (un,), _ = _rowwise("mix_norm", lambda a, g: ([_rms(a, g)], []), S, TR, [_full(h1)], [mix_norm], [(D, D, _c0, BF16)])
    z, (Wglu, Wout, Wpg, Wpp) = _mm_nn("mix_in", un, Win, tn=512, tk=D,
                                       rider=_GatherRider([wglu, wout, wpg, wpp]))
    Wglu, Wout, Wpg = rowstack(Wglu), rowstack(Wout), rowstack(Wpg)
    ya, mg, den = _attn_fwd(z)

    col = lambda a: a.reshape(-1, 1)
    lr_c, li_c = col(ssm_lambda_re), col(ssm_lambda_im)
    dt_c = col(jnp.broadcast_to(ssm_log_dt.reshape(SSM_G, 1), (SSM_G, SSM_P)))
    b_re2, b_im2 = ssm_b_re.reshape(-1, SSM_C), ssm_b_im.reshape(-1, SSM_C)
    ar_c, ai_c, bbr, bbi = _ssm_prep(lr_c, li_c, dt_c, b_re2, b_im2)
    a_r, a_i = ar_c.reshape(1, -1), ai_c.reshape(1, -1)
    npk = SSM_G // PACK
    bdr = _block_diag(bbr.reshape(npk, PACK, SSM_P, SSM_C)).astype(BF16)
    bdi = _block_diag(bbi.reshape(npk, PACK, SSM_P, SSM_C)).astype(BF16)
    cmr = _block_diag(ssm_c_re.reshape(npk, PACK, SSM_C, SSM_P)).astype(BF16)
    cmi = _block_diag(ssm_c_im.reshape(npk, PACK, SSM_C, SSM_P)).astype(BF16)
    TS = _tile(S, 512)
    hr, hi, ypre, (Wgu2,) = _ssm_fwd(z, a_r, a_i, bdr, bdi, cmr, cmi, ssm_d, TS, rider=_GatherRider([wgu2]))
    (yg,), _ = _rowwise("ssm_gelu", lambda a: ([jax.nn.gelu(a)], []), S, TR, [_full(ypre)], [], [(SSM_W, SSM_W, _c0, BF16)])
    gl = _mm_nn("ssm_glu", yg, Wglu, tn=SSM_W, tk=SSM_W)
    (ycat,), _ = _rowwise("mix_out", lambda *a: ([_mix_out(*a)], []), S, TR, [_full(ya), _full(ypre), _full(gl)],
                          [attn_out_norm, ssm_out_norm, ssm_b_glu], [(MIX_W, MIX_W, _c0, BF16)])
    h2 = _mm_nn("mix_proj", ycat, Wout, tn=D // 2, tk=D, res=h1, scale=1.0)

    xn2 = ffn_norm("ffn2", h2, ffn2_norm)
    gu2, hid2, _ = _ffn_up("ffn2_up", xn2, Wgu2)
    h3 = _mm_nn("ffn2_down", hid2, Wd2, tn=D, tk=2 * FSP, res=h2, scale=0.5)

    (hn, pb), _ = _rowwise("ple_norm", lambda a, q, g: ([_rms(a, g), q], []), S, TR, [_full(h3), _full(ps)], [ple_norm],
                           [(D, D, _c0, BF16), (ps.shape[1], ps.shape[1], _c0, BF16)])
    pgl = _mm_nn("ple_gate", hn, Wpg, tn=D // 2, tk=D)
    pe = _mm_nn("ple_proj", pb, Wpp, tn=Wpp.shape[2], tk=Wpp.shape[1])

    def tail(h3b, glb, peb, tb, gf):
        rows, vjp = jax.vjp(lambda a, b, c, g: _tail_loss(a, b, c, g, tb), h3b, glb, peb, gf)
        dh, dgl, dpe, dgf = vjp(jnp.ones_like(rows))
        return [dh, dgl, dpe], [jnp.broadcast_to(jnp.sum(rows, axis=0, keepdims=True), (1, LANE)), dgf]

    (dh3_dir, dpgl, dpe), (loss_row, g_final) = _rowwise(
        "tail", tail, S, TR, [_full(h3), _full(pgl), _full(pe), _full(tgt)], [final_norm.reshape(1, D)],
        [(D, D, _c0, F32), (D, D, _c0, BF16), (D, D, _c0, BF16)], [(LANE, LANE, _c0), (D, D, _c0)])
    loss = lax.psum(loss_row[0, 0], AXES)

    def norm_bwd(tag, h, gain, dn, dres):
        def f(hb, dnb, drb, g):
            _, vjp = jax.vjp(_rms, hb, g)
            dh, dg = vjp(dnb)
            dh = dh + drb
            return [dh, dh], [dg]
        (dh, dhb), (dg,) = _rowwise(f"{tag}_norm_bwd", f, S, TR, [_full(h), _full(dn), _full(dres)], [gain],
                                    [(D, D, _c0, F32), (D, D, _c0, BF16)], [(D, D, _c0)])
        return dh, dhb, dg

    restack = lambda g: g.reshape((NDEV, g.shape[1] // NDEV) + g.shape[2:])
    jobs, scat = [], {}

    def scatter(key, g):
        scat[key] = _Scatter("rs_" + key, g)
        jobs.append(scat[key])

    late = []
    dhn = _mm_nt("ple_gate_dx", dpgl, Wpg, tn=D, tk=D)
    late.append(lambda: scatter("pg", restack(_mm_tn("ple_gate_dw", hn, dpgl, 1, jobs=jobs))))
    late.append(lambda: scatter("pp", _mm_tn("ple_proj_dw", pb, dpe, NDEV, jobs=jobs)))
    dh3, dh3b, g_ple_norm = norm_bwd("ple", h3, ple_norm, dhn, dh3_dir)

    def ffn_bwd(tag, h, gain, Wgu, Wd, saved, dout, doutb):
        xn, gu, hid = saved
        dgu = _ffn_down_dx(f"{tag}_down_dx", doutb, Wd, gu, NDEV, scale=0.5, jobs=jobs)
        scatter(tag + "gu", _mm_tn(f"{tag}_up_dw", xn, dgu, NDEV, tm=2048, tn=FSP, jobs=jobs))
        scatter(tag + "d", restack(_mm_tn(f"{tag}_down_dw", hid, doutb, 1, tm=2048, tko=FSP, tn=D // 2, scale=0.5,
                                          jobs=jobs)))
        dxn = _mm_nt(f"{tag}_up_dx", dgu, Wgu, tm=1024, tn=D // 2, tk=2 * FSP, jobs=jobs)
        dh, dhb, g_norm = norm_bwd(tag, h, gain, dxn, dout)
        return dh, dhb, g_norm

    dh2, dh2b, g_ffn2_norm = ffn_bwd("ffn2", h2, ffn2_norm, Wgu2, Wd2, (xn2, gu2, hid2), dh3, dh3b)

    dycat = _mm_nt("mix_proj_dx", dh2b, Wout, tn=D, tk=D, jobs=jobs)
    late.append(lambda: scatter("out", restack(_mm_tn("mix_proj_dw", ycat, dh2b, 1, jobs=jobs))))

    def mix_out_bwd(yab, ypb, glb, dyc, ga, gb, bglu):
        _, vjp = jax.vjp(_mix_out, yab, ypb, glb, ga, gb, bglu)
        dya_, dyp_, dgl_, dga, dgb, dbg = vjp(dyc)
        return [dya_, dyp_, dgl_], [dga, dgb, dbg]
    (dya, dyp_dir, dglb), (g_attn_norm, g_ssm_norm, g_bglu) = _rowwise(
        "mix_out_bwd", mix_out_bwd, S, TR, [_full(ya), _full(ypre), _full(gl), _full(dycat)],
        [attn_out_norm, ssm_out_norm, ssm_b_glu],
        [(ATTN_W, ATTN_W, _c0, F32), (SSM_W, SSM_W, _c0, F32), (SSM_W, SSM_W, _c0, BF16)],
        [(ATTN_W, ATTN_W, _c0), (SSM_W, SSM_W, _c0), (SSM_W, SSM_W, _c0)])
    dyg = _mm_nt("ssm_glu_dx", dglb, Wglu, tn=SSM_W, tk=SSM_W, jobs=jobs)
    late.append(lambda: scatter("glu", restack(_mm_tn("ssm_glu_dw", yg, dglb, 1, jobs=jobs))))

    def gelu_bwd(ypb, dygb, ddir):
        _, vjp = jax.vjp(jax.nn.gelu, ypb)
        return [ddir + vjp(dygb)[0]], []
    (dyp,), _ = _rowwise("ssm_gelu_bwd", gelu_bwd, S, TR, [_full(ypre), _full(dyg), _full(dyp_dir)], [],
                         [(SSM_W, SSM_W, _c0, F32)])
    du, dbdr, dbdi, dcmr, dcmi, da_r, da_i, g_ssm_d = _ssm_bwd(z, dyp, hr, hi, a_r, a_i, bdr, bdi, cmr, cmi, ssm_d, TS,
                                                              jobs=jobs)
    dbbr = _block_diag_take(dbdr, SSM_P, SSM_C).reshape(-1, SSM_C)
    dbbi = _block_diag_take(dbdi, SSM_P, SSM_C).reshape(-1, SSM_C)
    g_c_re = _block_diag_take(dcmr, SSM_C, SSM_P).reshape(ssm_c_re.shape)
    g_c_im = _block_diag_take(dcmi, SSM_C, SSM_P).reshape(ssm_c_im.shape)
    dlr, dli, ddt, g_b_re, g_b_im = _ssm_prep_bwd(lr_c, li_c, dt_c, b_re2, b_im2, col(da_r), col(da_i), dbbr, dbbi)
    g_lam_re, g_lam_im = dlr.reshape(ssm_lambda_re.shape), dli.reshape(ssm_lambda_im.shape)
    g_log_dt = jnp.sum(ddt.reshape(SSM_G, SSM_P), axis=1).reshape(ssm_log_dt.shape)
    g_b_re, g_b_im = g_b_re.reshape(ssm_b_re.shape), g_b_im.reshape(ssm_b_im.shape)

    dq, dk, dv = _attn_bwd(z, dya, ya, mg, den)
    (dz,), _ = _rowwise("mix_dz", lambda *a: ([jnp.concatenate(a, axis=-1)], []), S, TR,
                        [_full(dq), _full(dk), _full(dv), _full(du)], [], [(ZW, ZW, _c0, BF16)])
    dun = _mm_nt("mix_in_dx", dz, Win, tn=D, tk=512, jobs=jobs)
    scatter("in", _mm_tn("mix_in_dw", un, dz, NDEV, tn=512, jobs=jobs))
    dh1, dh1b, g_mix_norm = norm_bwd("mix", h1, mix_norm, dun, dh2)

    dx, _dxb, g_ffn1_norm = ffn_bwd("ffn1", xs, ffn1_norm, Wgu1, Wd1, (xn1, gu1, hid1), dh1, dh1b)
    for run in (late[2], late[0], late[3], late[1]):
        run()

    mine = {key: job.finish() for key, job in scat.items()}
    out = {}

    def upd(name, idx, *, tr, cw, gw, goff=0):
        w, m, v = A[name][0], A["m_" + name][0], A["v_" + name][0]
        g, dlt, mn, vn = _adamw("adamw_" + name, w, m, v, mine[idx], tr=tr, cw=cw, gw=gw, goff=goff)
        for k, val in (("grad_", g), ("delta_", dlt), ("new_m_", mn), ("new_v_", vn)):
            out[k + name] = val[None]

    DT = _tile(D, 256)
    FT = _tile(FSH, 512)
    DC = _tile(D, 1024, LANE)
    upd("ffn1_w_gate", "ffn1gu", tr=DT, cw=FSH, gw=FSP, goff=0)
    upd("ffn1_w_up", "ffn1gu", tr=DT, cw=FSH, gw=FSP, goff=1)
    upd("ffn1_w_down", "ffn1d", tr=FT, cw=DC, gw=DC)
    upd("w_in", "in", tr=DT, cw=w_in.shape[-1], gw=w_in.shape[-1])
    upd("ssm_w_glu", "glu", tr=ssm_w_glu.shape[1], cw=SSM_W, gw=SSM_W)
    upd("w_out", "out", tr=w_out.shape[1], cw=DC, gw=DC)
    upd("ffn2_w_gate", "ffn2gu", tr=DT, cw=FSH, gw=FSP, goff=0)
    upd("ffn2_w_up", "ffn2gu", tr=DT, cw=FSH, gw=FSP, goff=1)
    upd("ffn2_w_down", "ffn2d", tr=FT, cw=DC, gw=DC)
    upd("ple_w_gate", "pg", tr=ple_w_gate.shape[1], cw=DC, gw=DC)
    upd("ple_w_proj", "pp", tr=ple_w_proj.shape[1], cw=ple_w_proj.shape[2], gw=ple_w_proj.shape[2])

    small = [("ffn1_norm", g_ffn1_norm), ("mix_norm", g_mix_norm), ("attn_out_norm", g_attn_norm),
             ("ssm_lambda_re", g_lam_re), ("ssm_lambda_im", g_lam_im), ("ssm_log_dt", g_log_dt),
             ("ssm_b_re", g_b_re), ("ssm_b_im", g_b_im), ("ssm_c_re", g_c_re), ("ssm_c_im", g_c_im),
             ("ssm_d", g_ssm_d), ("ssm_b_glu", g_bglu), ("ssm_out_norm", g_ssm_norm), ("ffn2_norm", g_ffn2_norm),
             ("ple_norm", g_ple_norm), ("final_norm", g_final)]
    chunk = 8 * LANE

    def pack(arrs):
        parts = []
        for a in arrs:
            flat = a.reshape(-1)
            padn = -(-flat.shape[0] // chunk) * chunk
            parts.append(jnp.pad(flat, (0, padn - flat.shape[0])).reshape(-1, LANE))
        return jnp.concatenate(parts, axis=0)

    g_pack = pack([g for _, g in small])
    (g_all,) = _all_gather("ag_small", [g_pack])
    g_sum = _sum8("small_sum", g_all)
    w_pack = pack([A[n] for n, _ in small])
    m_pack = pack([A["m_" + n] for n, _ in small])
    v_pack = pack([A["v_" + n] for n, _ in small])
    d_pack, mn_pack, vn_pack = _adamw_small("adamw_small", w_pack, m_pack, v_pack, g_sum)
    off = 0
    for n, _ in small:
        shape = A[n].shape
        size = math.prod(shape)
        rows = -(-size // chunk) * 8
        for k, buf in (("grad_", g_sum), ("delta_", d_pack), ("new_m_", mn_pack), ("new_v_", vn_pack)):
            out[k + n] = buf[off:off + rows].reshape(-1)[:size].reshape(shape)
        off += rows

    names = ['ffn1_norm', 'ffn1_w_gate', 'ffn1_w_up', 'ffn1_w_down', 'mix_norm', 'w_in', 'attn_out_norm',
             'ssm_lambda_re', 'ssm_lambda_im', 'ssm_log_dt', 'ssm_b_re', 'ssm_b_im', 'ssm_c_re', 'ssm_c_im', 'ssm_d',
             'ssm_w_glu', 'ssm_b_glu', 'ssm_out_norm', 'w_out', 'ffn2_norm', 'ffn2_w_gate', 'ffn2_w_up', 'ffn2_w_down',
             'ple_norm', 'ple_w_gate', 'ple_w_proj', 'final_norm']
    return (loss, dx[None], *[out[k + n] for k in ("grad_", "delta_", "new_m_", "new_v_") for n in names])
```

```python
import functools
import math

import jax
import jax.numpy as jnp
from jax import lax
from jax.experimental import pallas as pl
from jax.experimental.pallas import tpu as pltpu

F32, BF16 = jnp.float32, jnp.bfloat16
MESH = pl.DeviceIdType.MESH
NDEV = 8
AXES = ("x", "y", "c")
LANE = 128
VMEM_LIMIT = 56 * 1024 * 1024

ATTN_W = 1024
HEAD_DIM = 64
SSM_W = 1024
MIX_W = ATTN_W + SSM_W
SSM_G, SSM_P, SSM_C = 64, 64, 16
PACK = 8
DILATIONS = (1, 4, 16)
QB = 128
NORM_EPS = 1e-6
MASK_VALUE = -1e30
LR, B1, B2, EPS, WD, STEP = 0.001, 0.9, 0.999, 1e-08, 0.01, 10


def _cp(sem=None):
    return pltpu.CompilerParams(dimension_semantics=sem, vmem_limit_bytes=VMEM_LIMIT)


def _tile(n, target, mult=8):
    if n <= target:
        return n
    for t in range(target - target % mult, 0, -mult):
        if n % t == 0:
            return t
    return n


def _rms(x, g):
    return x * lax.rsqrt(jnp.mean(x * x, axis=-1, keepdims=True) + NORM_EPS) * g


def _rowwise(name, fn, S, tr, rows, fulls, outs, accs=(), ncol=1):
    nr, nf, no, na = len(rows), len(fulls), len(outs), len(accs)

    def body(*refs):
        ins = [r[...] for r in refs[:nr + nf]]
        o_refs = refs[nr + nf:nr + nf + no]
        a_refs = refs[nr + nf + no:]
        o_vals, a_vals = fn(*ins)
        for r, v in zip(o_refs, o_vals):
            r[...] = v.astype(r.dtype)
        if na:
            @pl.when(pl.program_id(1) == 0)
            def _():
                for r in a_refs:
                    r[...] = jnp.zeros_like(r)
            for r, v in zip(a_refs, a_vals):
                r[...] += v

    in_specs = [pl.BlockSpec((tr, w), functools.partial(lambda j, i, cm: (i, cm(j)), cm=cm)) for _, w, cm in rows]
    in_specs += [pl.BlockSpec(f.shape, functools.partial(lambda j, i, nd: (0,) * nd, nd=f.ndim)) for f in fulls]
    out_specs = [pl.BlockSpec((tr, w), functools.partial(lambda j, i, cm: (i, cm(j)), cm=cm)) for _, w, cm, _ in outs]
    out_specs += [pl.BlockSpec((1, w), functools.partial(lambda j, i, cm: (0, cm(j)), cm=cm)) for _, w, cm in accs]
    out_shape = [jax.ShapeDtypeStruct((S, c), dt) for c, _, _, dt in outs]
    out_shape += [jax.ShapeDtypeStruct((1, c), F32) for c, _, _ in accs]
    res = pl.pallas_call(
        body, name=name, grid=(ncol, S // tr), in_specs=in_specs, out_specs=out_specs, out_shape=out_shape,
        compiler_params=_cp(("parallel", "arbitrary" if na else "parallel")),
    )(*[a for a, _, _ in rows], *fulls)
    return res[:no], res[no:]


def _c0(j):
    return 0


def _full(a):
    return (a, a.shape[1], _c0)


def _hosted(name, body, grid, in_specs, out_specs, out_shape, scratch, args, sem, rider=None, jobs=None):
    nsteps = math.prod(grid)

    def step_of(*g):
        t = 0
        for gi, n in zip(g, grid):
            t = t * n + gi
        return t

    job = None
    if rider is None:
        job, rider = _pick(jobs, nsteps)
    if rider is None:
        outs = pl.pallas_call(body, name=name, grid=grid, in_specs=in_specs, out_specs=out_specs, out_shape=out_shape,
                              scratch_shapes=scratch, compiler_params=_cp(sem))(*args)
        return outs, None
    rider.bind(step_of, nsteps)
    n_in, n_out, n_scr = len(in_specs), len(out_specs), len(scratch)

    def full(*refs):
        a, b = n_in, n_in + rider.n_in
        c, d = b + n_out, b + n_out + rider.n_out
        rider.run(refs[a:b], refs[c:d], refs[d + n_scr:], step_of(*[pl.program_id(i) for i in range(len(grid))]), nsteps)
        body(*(refs[:a] + refs[b:c] + refs[d:d + n_scr]))

    outs = pl.pallas_call(
        full, name=name, grid=grid, in_specs=in_specs + rider.in_specs, out_specs=out_specs + rider.out_specs,
        out_shape=out_shape + rider.out_shape, scratch_shapes=scratch + rider.scratch,
        compiler_params=_cp(("arbitrary",) * len(grid)))(*args, *rider.operands)
    extra = rider.take(outs[n_out:])
    if job is not None:
        job.advance(extra)
        extra = None
    return outs[:n_out], extra


def _mm_nn(name, a, w, *, out_dtype=F32, tm=512, tn=768, tk=2048, res=None, scale=1.0, rider=None, jobs=None):
    M, K = a.shape
    J, K2, Np = w.shape
    assert K == K2
    tm, tn, tk = _tile(M, tm), _tile(Np, tn, LANE), _tile(K, tk, LANE)
    npj = Np // tn
    nk = K // tk
    grid = (M // tm, J * npj, nk)

    def body(*refs):
        if res is None:
            a_ref, w_ref, o_ref, acc = refs
        else:
            a_ref, w_ref, r_ref, o_ref, acc = refs
        k = pl.program_id(2)
        part = jnp.dot(a_ref[...].astype(BF16), w_ref[...], preferred_element_type=F32)

        def finish(v):
            if res is not None:
                v = r_ref[...] + scale * v
            o_ref[...] = v.astype(o_ref.dtype)

        if nk == 1:
            finish(part)
            return

        @pl.when(k == 0)
        def _():
            acc[...] = part

        @pl.when(k > 0)
        def _():
            acc[...] += part

        @pl.when(k == nk - 1)
        def _():
            finish(acc[...])

    in_specs = [pl.BlockSpec((tm, tk), lambda i, n, k: (i, k)),
                pl.BlockSpec((None, tk, tn), lambda i, n, k: (n // npj, k, n % npj))]
    args = [a, w]
    if res is not None:
        in_specs.append(pl.BlockSpec((tm, tn), lambda i, n, k: (i, n)))
        args.append(res)
    (out,), extra = _hosted(
        name, body, grid, in_specs, [pl.BlockSpec((tm, tn), lambda i, n, k: (i, n))],
        [jax.ShapeDtypeStruct((M, J * Np), out_dtype)], [pltpu.VMEM((tm, tn), F32)], args,
        ("parallel", "parallel", "arbitrary"), rider, jobs)
    return out if rider is None else (out, extra)


def _mm_nt(name, dy, w, *, out_dtype=F32, tm=512, tn=2048, tk=768, scale=1.0, jobs=None):
    M, N = dy.shape
    J, K, Np = w.shape
    assert N == J * Np
    tm, tn, tk = _tile(M, tm), _tile(K, tn, LANE), _tile(Np, tk, LANE)
    npj = Np // tk
    nc = J * npj

    def body(a_ref, w_ref, o_ref, acc):
        c = pl.program_id(2)
        part = lax.dot_general(a_ref[...].astype(BF16), w_ref[...], (((1,), (1,)), ((), ())),
                               preferred_element_type=F32)
        if nc == 1:
            o_ref[...] = (scale * part).astype(o_ref.dtype)
            return

        @pl.when(c == 0)
        def _():
            acc[...] = part

        @pl.when(c > 0)
        def _():
            acc[...] += part

        @pl.when(c == nc - 1)
        def _():
            o_ref[...] = (scale * acc[...]).astype(o_ref.dtype)

    (out,), _ = _hosted(
        name, body, (M // tm, K // tn, nc),
        [pl.BlockSpec((tm, tk), lambda i, n, c: (i, c)),
         pl.BlockSpec((None, tn, tk), lambda i, n, c: (c // npj, n, c % npj))],
        [pl.BlockSpec((tm, tn), lambda i, n, c: (i, n))], [jax.ShapeDtypeStruct((M, K), out_dtype)],
        [pltpu.VMEM((tm, tn), F32)], (dy, w), ("parallel", "parallel", "arbitrary"), None, jobs)
    return out


def _mm_tn(name, x, dy, J, *, tm=1024, tko=1024, tn=768, scale=1.0, jobs=None):
    M, K = x.shape
    M2, N = dy.shape
    assert M == M2 and N % J == 0
    Np = N // J
    tm, tko, tn = _tile(M, tm, LANE), _tile(K, tko, LANE), _tile(Np, tn, LANE)
    npj = Np // tn
    nm = M // tm

    def body(x_ref, d_ref, o_ref, acc):
        m = pl.program_id(2)
        part = lax.dot_general(x_ref[...].astype(BF16), d_ref[...].astype(BF16), (((0,), (0,)), ((), ())),
                               preferred_element_type=F32)
        if nm == 1:
            o_ref[...] = scale * part
            return

        @pl.when(m == 0)
        def _():
            acc[...] = part

        @pl.when(m > 0)
        def _():
            acc[...] += part

        @pl.when(m == nm - 1)
        def _():
            o_ref[...] = scale * acc[...]

    (out,), _ = _hosted(
        name, body, (K // tko, J * npj, nm),
        [pl.BlockSpec((tm, tko), lambda k, n, m: (m, k)), pl.BlockSpec((tm, tn), lambda k, n, m: (m, n))],
        [pl.BlockSpec((None, tko, tn), lambda k, n, m: (n // npj, k, n % npj))],
        [jax.ShapeDtypeStruct((J, K, Np), F32)], [pltpu.VMEM((tko, tn), F32)], (x, dy),
        ("parallel", "parallel", "arbitrary"), None, jobs)
    return out


def _swiglu_act(g, u):
    return jax.nn.silu(g) * u


def _ffn_up(name, xn, wgu, *, tm=1024, rider=None):
    M, K = xn.shape
    J, _, F2 = wgu.shape
    F = F2 // 2
    tm = _tile(M, tm)

    def body(a_ref, w_ref, gu_ref, h_ref):
        r = jnp.dot(a_ref[...], w_ref[...], preferred_element_type=F32)
        gu_ref[...] = r.astype(gu_ref.dtype)
        h_ref[...] = _swiglu_act(r[:, :F], r[:, F:]).astype(h_ref.dtype)

    (gu, hid), extra = _hosted(
        name, body, (M // tm, J),
        [pl.BlockSpec((tm, K), lambda i, j: (i, 0)), pl.BlockSpec((None, K, F2), lambda i, j: (j, 0, 0))],
        [pl.BlockSpec((tm, F2), lambda i, j: (i, j)), pl.BlockSpec((tm, F), lambda i, j: (i, j))],
        [jax.ShapeDtypeStruct((M, J * F2), BF16), jax.ShapeDtypeStruct((M, J * F), BF16)], [], (xn, wgu),
        ("parallel", "parallel"), rider)
    return gu, hid, extra


def _ffn_down_dx(name, dout, wd, gu, J, *, scale, tm=512, jobs=None):
    M, D = dout.shape
    F = wd.shape[1] // J
    tm = _tile(M, tm)

    def body(d_ref, w_ref, gu_ref, o_ref):
        dh = scale * lax.dot_general(d_ref[...], w_ref[...], (((1,), (1,)), ((), ())), preferred_element_type=F32)
        gu = gu_ref[...].astype(F32)
        _, vjp = jax.vjp(_swiglu_act, gu[:, :F], gu[:, F:])
        o_ref[...] = jnp.concatenate(vjp(dh), axis=-1).astype(o_ref.dtype)

    (out,), _ = _hosted(
        name, body, (M // tm, J),
        [pl.BlockSpec((tm, D), lambda i, j: (i, 0)), pl.BlockSpec((None, F, D), lambda i, j: (0, j, 0)),
         pl.BlockSpec((tm, 2 * F), lambda i, j: (i, j))],
        [pl.BlockSpec((tm, 2 * F), lambda i, j: (i, j))], [jax.ShapeDtypeStruct((M, J * 2 * F), BF16)], [],
        (dout, wd, gu), ("parallel", "parallel"), None, jobs)
    return out


def _all_gather(name, shards):
    n = len(shards)

    def body(*refs):
        start, forward, finish = _gather_phases(refs[:n], refs[n:2 * n], *refs[2 * n:])
        start()
        forward()
        finish()

    any_spec = pl.BlockSpec(memory_space=pl.ANY)
    return pl.pallas_call(
        body, name=name, in_specs=[any_spec] * n, out_specs=[any_spec] * n,
        out_shape=[jax.ShapeDtypeStruct((NDEV,) + s.shape, s.dtype) for s in shards],
        scratch_shapes=_gather_sems(n),
    )(*shards)


def _gather_sems(n):
    return [pltpu.SemaphoreType.DMA((n, 7)), pltpu.SemaphoreType.DMA((n, 7)), pltpu.SemaphoreType.DMA((n,))]


def _gather_phases(ins, outs, send_sems, recv_sems, local_sems):
    n = len(ins)
    x, y, c = lax.axis_index("x"), lax.axis_index("y"), lax.axis_index("c")
    me, sibling = (x, y, c), (x, y, 1 - c)
    chips = [(1 - x, y), (x, 1 - y), (1 - x, 1 - y)]

    def blk(i, px, py, pc):
        return outs[i].at[4 * px + 2 * py + pc]

    def copy(i, k, block, to, src=None):
        return pltpu.make_async_remote_copy(
            src_ref=blk(i, *block) if src is None else src, dst_ref=blk(i, *block),
            send_sem=send_sems.at[i, k], recv_sem=recv_sems.at[i, k], device_id=to, device_id_type=MESH)

    def local(i):
        return pltpu.make_async_copy(ins[i], blk(i, *me), local_sems.at[i])

    def firsts(i):
        return [copy(i, 0, me, sibling, src=ins[i])] + [copy(i, 1 + j, me, (*chip, c), src=ins[i])
                                                        for j, chip in enumerate(chips)]

    def start():
        for i in range(n):
            local(i).start()
        for i in range(n):
            for cp in firsts(i):
                cp.start()

    def forward():
        for i in range(n):
            for j, chip in enumerate(chips):
                copy(i, 1 + j, (*chip, c), me).wait_recv()
                copy(i, 4 + j, (*chip, c), sibling).start()

    def finish():
        for i in range(n):
            copy(i, 0, sibling, me).wait_recv()
            for j, chip in enumerate(chips):
                copy(i, 4 + j, (*chip, 1 - c), me).wait_recv()
        for i in range(n):
            for cp in firsts(i):
                cp.wait_send()
            for j, chip in enumerate(chips):
                copy(i, 4 + j, (*chip, c), sibling).wait_send()
            local(i).wait()

    return start, forward, finish


class _GatherRider:
    def __init__(self, shards):
        self.operands = list(shards)
        n = len(self.operands)
        self.n_in = self.n_out = n
        any_spec = pl.BlockSpec(memory_space=pl.ANY)
        self.in_specs = [any_spec] * n
        self.out_specs = [any_spec] * n
        self.out_shape = [jax.ShapeDtypeStruct((NDEV,) + s.shape, s.dtype) for s in self.operands]
        self.scratch = _gather_sems(n)

    def bind(self, step_of, nsteps):
        return self

    def take(self, outs):
        return list(outs)

    def run(self, ins, outs, sems, step, nsteps):
        start, forward, finish = _gather_phases(ins, outs, *sems)
        pl.when(step == 0)(start)
        pl.when(step == (4 * nsteps) // 5)(forward)
        pl.when(step == nsteps - 1)(finish)


class _SwapRider:
    def __init__(self, arr, streams, grid, tile, out_shape, out_block, out_map):
        self.arr, self.streams, self.grid, self.tile = arr, streams, grid, tile
        self.ns, self.n = len(streams), grid[0] * grid[1]
        self.operands = [arr] * (2 * self.ns)
        self.n_in, self.n_out = 2 * self.ns, 1
        self.out_shape = [jax.ShapeDtypeStruct(out_shape, F32)]
        self.out_block, self.out_map = out_block, out_map
        tr, C = tile
        slots = [pltpu.VMEM((2, tr, C), w) for _, w, _, _ in streams]
        self.scratch = slots + slots + [pltpu.SemaphoreType.DMA((self.ns, 2)), pltpu.SemaphoreType.DMA((self.ns, 2)),
                                        pltpu.SemaphoreType.REGULAR((self.ns,))]

    def bind(self, step_of, nsteps):
        assert nsteps >= self.n
        self.period = period = nsteps // self.n
        n, nr = self.n, self.grid[1]

        def ids(*g):
            k = jnp.minimum(step_of(*g) // period, n - 1)
            pos = {a: lax.axis_index(a) for a in AXES}
            return k // nr, k % nr, [v for a in AXES for v in (pos[a], 1 - pos[a])]

        block = (None,) * (self.arr.ndim - 2) + tuple(self.tile)
        self.in_specs = []
        for _, _, keep_map, send_map in self.streams:
            for m in (keep_map, send_map):
                self.in_specs.append(pl.BlockSpec(block, functools.partial(lambda *g, m: m(*ids(*g)), m=m)))
        self.out_specs = [pl.BlockSpec(self.out_block, lambda *g: self.out_map(*ids(*g)))]
        return self

    def take(self, outs):
        return outs[0]

    def run(self, ins, outs, scratch, step, nsteps):
        ns, n, period = self.ns, self.n, self.period
        keeps, sends, o_ref = ins[0::2], ins[1::2], outs[0]
        lands, stages = scratch[:ns], scratch[ns:2 * ns]
        send_sems, recv_sems, credits = scratch[2 * ns:]
        k = step // period
        slot = k % 2
        here = {a: lax.axis_index(a) for a in AXES}
        peers = [tuple(1 - here[a] if a == axis else here[a] for a in AXES) for axis, _, _, _ in self.streams]

        def rdma(s):
            return pltpu.make_async_remote_copy(
                src_ref=stages[s].at[slot], dst_ref=lands[s].at[slot], send_sem=send_sems.at[s, slot],
                recv_sem=recv_sems.at[s, slot], device_id=peers[s], device_id_type=MESH)

        @pl.when((k < n) & (step % period == 0))
        def _():
            @pl.when(k >= 2)
            def _():
                for s in range(ns):
                    pl.semaphore_wait(credits.at[s], 1)

            for s in range(ns):
                stages[s][slot] = sends[s][...].astype(stages[s].dtype)
                rdma(s).start()

        @pl.when((k < n) & (step % period == period - 1))
        def _():
            for s in range(ns):
                rdma(s).wait_recv()
                total = keeps[s][...] + lands[s][slot].astype(F32)
                if ns == 1:
                    o_ref[...] = total
                else:
                    o_ref[s] = total
            for s in range(ns):
                rdma(s).wait_send()

            @pl.when(k + 2 < n)
            def _():
                for s in range(ns):
                    pl.semaphore_signal(credits.at[s], inc=1, device_id=peers[s], device_id_type=MESH)


def _run_alone(name, rider):
    rider.bind(lambda t: t, rider.n)

    def body(*refs):
        a, b = rider.n_in, rider.n_in + rider.n_out
        rider.run(refs[:a], refs[a:b], refs[b:], pl.program_id(0), rider.n)

    outs = pl.pallas_call(
        body, name=name, grid=(rider.n,), in_specs=rider.in_specs, out_specs=rider.out_specs,
        out_shape=rider.out_shape, scratch_shapes=rider.scratch, compiler_params=_cp(("arbitrary",)),
    )(*rider.operands)
    return rider.take(outs)


class _Scatter:
    def __init__(self, name, g):
        self.name, self.cur, self.stage = name, g, 0
        _, self.R, self.C = g.shape

    def done(self):
        return self.stage == 3

    def rider(self, rows):
        R, C = self.R, self.C
        R2 = R // 2
        tr = _tile(R2, rows, 16)
        nrh = R2 // tr
        if self.stage == 0:
            return _SwapRider(
                self.cur.reshape(4, 2, R, C),
                [("c", BF16, lambda b, i, s: (b, s[4], i, 0), lambda b, i, s: (b, s[5], i, 0))],
                (4, 2 * nrh), (tr, C), (2, 4, R2, C), (None, None, tr, C), lambda b, i, s: (i // nrh, b, i % nrh, 0))
        if self.stage == 1:
            return _SwapRider(
                self.cur.reshape(2, 2, 2, R2, C),
                [("y", BF16, lambda b, i, s: (0, b, s[2], i, 0), lambda b, i, s: (0, b, s[3], i, 0)),
                 ("x", BF16, lambda b, i, s: (1, s[0], b, i, 0), lambda b, i, s: (1, s[1], b, i, 0))],
                (2, nrh), (tr, C), (2, 2, R2, C), (2, None, tr, C), lambda b, i, s: (0, b, i, 0))
        return _SwapRider(
            self.cur,
            [("x", BF16, lambda b, i, s: (0, s[0], i, 0), lambda b, i, s: (0, s[1], i, 0)),
             ("y", BF16, lambda b, i, s: (1, s[2], i, 0), lambda b, i, s: (1, s[3], i, 0))],
            (1, nrh), (tr, C), (2, R2, C), (2, tr, C), lambda b, i, s: (0, i, 0))

    def advance(self, out):
        self.cur, self.stage = out, self.stage + 1

    def finish(self):
        while not self.done():
            self.advance(_run_alone(f"{self.name}_s{self.stage}", self.rider(256)))
        return self.cur.reshape(self.R, self.C)


RIDER_TILE_BYTES = 3 * 512 * 1024


def _pick(jobs, nsteps):
    for job in sorted(jobs or (), key=lambda j: -j.R * j.C):
        if job.done():
            continue
        streams = 1 if job.stage == 0 else 2
        riders = [job.rider(rows) for rows in (512, 256, 128, 64)
                  if rows * job.C * 4 * streams <= RIDER_TILE_BYTES or rows == 64]
        for rider in riders:
            if 2 * rider.n <= nsteps:
                return job, rider
        if riders[-1].n <= nsteps:
            return job, riders[-1]
    return None, None


def _sum8(name, g):
    _, R, C = g.shape
    tr = _tile(R, 512)

    def body(g_ref, o_ref):
        acc = g_ref[0]
        for d in range(1, NDEV):
            acc = acc + g_ref[d]
        o_ref[...] = acc

    return pl.pallas_call(
        body, name=name, grid=(R // tr,), in_specs=[pl.BlockSpec((NDEV, tr, C), lambda i: (0, i, 0))],
        out_specs=pl.BlockSpec((tr, C), lambda i: (i, 0)), out_shape=jax.ShapeDtypeStruct((R, C), F32),
        compiler_params=_cp(("parallel",)),
    )(g)


def _adamw_math(w, g, m, v):
    m = B1 * m + (1.0 - B1) * g
    v = B2 * v + (1.0 - B2) * jnp.square(g)
    m_hat = m / (1.0 - B1 ** STEP)
    v_hat = v / (1.0 - B2 ** STEP)
    delta = -LR * (m_hat / (jnp.sqrt(v_hat) + EPS) + WD * w)
    return delta, m, v


def _adamw(name, w, m, v, gp, *, tr, cw, gw, goff=0):
    R, C = w.shape
    nc = C // cw
    nr = R // tr

    def body(w_ref, m_ref, v_ref, g_ref, g_out, d_out, m_out, v_out):
        g = g_ref[...][:, :cw]
        d, mn, vn = _adamw_math(w_ref[...], g, m_ref[...], v_ref[...])
        g_out[...] = g
        d_out[...] = d
        m_out[...] = mn
        v_out[...] = vn

    wspec = pl.BlockSpec((tr, cw), lambda i, j: (i, j))
    gspec = pl.BlockSpec((tr, gw), lambda i, j: (i, goff + j))
    return pl.pallas_call(
        body, name=name, grid=(nr, nc), in_specs=[wspec, wspec, wspec, gspec], out_specs=[wspec] * 4,
        out_shape=[jax.ShapeDtypeStruct((R, C), F32)] * 4, compiler_params=_cp(("parallel", "parallel")),
    )(w, m, v, gp)


def _adamw_small(name, w, m, v, g):
    R, C = w.shape

    def body(w_ref, m_ref, v_ref, g_ref, d_out, m_out, v_out):
        d, mn, vn = _adamw_math(w_ref[...], g_ref[...], m_ref[...], v_ref[...])
        d_out[...] = d
        m_out[...] = mn
        v_out[...] = vn

    tr = _tile(R, 512)
    spec = pl.BlockSpec((tr, C), lambda i: (i, 0))
    return pl.pallas_call(
        body, name=name, grid=(R // tr,), in_specs=[spec] * 4, out_specs=[spec] * 3,
        out_shape=[jax.ShapeDtypeStruct((R, C), F32)] * 3, compiler_params=_cp(("parallel",)),
    )(w, m, v, g)


def _prep(name, parts, rows_p, cols_p):
    R, C = parts[0].shape
    n = len(parts)

    def body(*refs):
        o_ref = refs[n]
        if (R, C) != (rows_p, cols_p):
            o_ref[...] = jnp.zeros_like(o_ref)
        for i in range(n):
            o_ref[0:R, i * cols_p:i * cols_p + C] = refs[i][...].astype(BF16)

    return pl.pallas_call(
        body, name=name, out_shape=jax.ShapeDtypeStruct((rows_p, n * cols_p), BF16), compiler_params=_cp(),
    )(*parts)


def _attn_masks():
    lane = lax.broadcasted_iota(jnp.int32, (1, LANE), 1)
    return [(lane < HEAD_DIM), (lane >= HEAD_DIM)]


def _band_valid(base):
    qi = lax.broadcasted_iota(jnp.int32, (QB, 2 * QB), 0)
    ki = lax.broadcasted_iota(jnp.int32, (QB, 2 * QB), 1)
    dist = qi + QB - ki
    return (dist >= 0) & (dist <= QB) & (base + ki - QB >= 0)


ATTN_T = max(DILATIONS) * QB


def _attn_groups(T):
    out = []
    for d in DILATIONS:
        for r in range(d):
            for i in range(T // (d * QB)):
                qrows = pl.ds(r + d * i * QB, QB, stride=d) if d > 1 else pl.ds(i * QB, QB)
                k0 = T + r + d * (i - 1) * QB
                krows = pl.ds(k0, 2 * QB, stride=d) if d > 1 else pl.ds(k0, 2 * QB)
                out.append((d, qrows, krows, i * QB))
    return out


def _attn_specs(T, width_off):
    cur = pl.BlockSpec((T, LANE), lambda hp, b: (b, width_off + hp))
    prev = pl.BlockSpec((T, LANE), lambda hp, b: (jnp.maximum(b - 1, 0), width_off + hp))
    return cur, prev


def _attn_fwd(z):
    S, ZW = z.shape
    T = min(ATTN_T, S)
    scale = HEAD_DIM ** -0.5
    groups = _attn_groups(T)

    def body(q_ref, kc_ref, kp_ref, vc_ref, vp_ref, y_ref, m_ref, l_ref, kcat, vcat):
        b = pl.program_id(1)
        kcat[0:T, :] = kp_ref[...]
        kcat[T:, :] = kc_ref[...]
        vcat[0:T, :] = vp_ref[...]
        vcat[T:, :] = vc_ref[...]
        masks = _attn_masks()
        for d, qrows, krows, l0 in groups:
            q = q_ref[qrows, :]
            kk = kcat[krows, :].astype(BF16)
            vv = vcat[krows, :].astype(BF16)
            valid = _band_valid(b * (T // d) + l0)
            o_new = m_new = l_new = None
            for hm in masks:
                qh = jnp.where(hm, q, 0.0).astype(BF16)
                s = lax.dot_general(qh, kk, (((1,), (1,)), ((), ())), preferred_element_type=F32) * scale
                s = jnp.where(valid, s, MASK_VALUE)
                m = jnp.max(s, axis=-1, keepdims=True)
                p = jnp.exp(s - m)
                l = jnp.sum(p, axis=-1, keepdims=True)
                o = jnp.dot(p.astype(BF16), vv, preferred_element_type=F32)
                if o_new is None:
                    o_new, m_new, l_new = o, jnp.broadcast_to(m, (QB, LANE)), jnp.broadcast_to(l, (QB, LANE))
                else:
                    o_new = jnp.where(hm, o, o_new)
                    m_new = jnp.where(hm, m, m_new)
                    l_new = jnp.where(hm, l, l_new)
            if d == DILATIONS[0]:
                y_ref[qrows, :] = o_new
                m_ref[qrows, :] = m_new
                l_ref[qrows, :] = l_new
            else:
                m_old = m_ref[qrows, :]
                m_all = jnp.maximum(m_old, m_new)
                w_old, w_new = jnp.exp(m_old - m_all), jnp.exp(m_new - m_all)
                y_ref[qrows, :] = w_old * y_ref[qrows, :] + w_new * o_new
                l_ref[qrows, :] = w_old * l_ref[qrows, :] + w_new * l_new
                m_ref[qrows, :] = m_all
        y_ref[...] = y_ref[...] / l_ref[...]

    qc, _ = _attn_specs(T, 0)
    kc, kp = _attn_specs(T, ATTN_W // LANE)
    vc, vp = _attn_specs(T, 2 * ATTN_W // LANE)
    shp = jax.ShapeDtypeStruct((S, ATTN_W), F32)
    return pl.pallas_call(
        body, name="attn_fwd", grid=(ATTN_W // LANE, S // T),
        in_specs=[qc, kc, kp, vc, vp], out_specs=[qc, qc, qc], out_shape=[shp, shp, shp],
        scratch_shapes=[pltpu.VMEM((2 * T, LANE), F32), pltpu.VMEM((2 * T, LANE), F32)],
        compiler_params=_cp(("parallel", "parallel")),
    )(z, z, z, z, z)


def _attn_bwd(z, dya, ya, mg, den):
    S, ZW = z.shape
    T = min(ATTN_T, S)
    scale = HEAD_DIM ** -0.5
    groups = _attn_groups(T)

    def body(q_ref, kc_ref, kp_ref, vc_ref, vp_ref, dy_ref, y_ref, m_ref, n_ref, dq_ref, dk_ref, dv_ref,
             kcat, vcat, dkcat, dvcat):
        b = pl.program_id(1)

        @pl.when(b == 0)
        def _():
            dk_ref[...] = jnp.zeros_like(dk_ref)
            dv_ref[...] = jnp.zeros_like(dv_ref)

        kcat[0:T, :] = kp_ref[...]
        kcat[T:, :] = kc_ref[...]
        vcat[0:T, :] = vp_ref[...]
        vcat[T:, :] = vc_ref[...]
        dkcat[...] = jnp.zeros_like(dkcat)
        dvcat[...] = jnp.zeros_like(dvcat)
        dq_ref[...] = jnp.zeros_like(dq_ref)
        masks = _attn_masks()
        for d, rows, krows, l0 in groups:
            q, dy, y = q_ref[rows, :], dy_ref[rows, :], y_ref[rows, :]
            mrow, nrow = m_ref[rows, :], n_ref[rows, :]
            kk = kcat[krows, :].astype(BF16)
            vv = vcat[krows, :].astype(BF16)
            valid = _band_valid(b * (T // d) + l0)
            dq_acc = jnp.zeros((QB, LANE), F32)
            dk_acc = jnp.zeros((2 * QB, LANE), F32)
            dv_acc = jnp.zeros((2 * QB, LANE), F32)
            for hm in masks:
                qh = jnp.where(hm, q, 0.0).astype(BF16)
                dyh = jnp.where(hm, dy, 0.0)
                dyb = dyh.astype(BF16)
                dsum = jnp.sum(dyh * y, axis=-1, keepdims=True)
                mh = jnp.max(jnp.where(hm, mrow, MASK_VALUE), axis=-1, keepdims=True)
                nh = jnp.max(jnp.where(hm, nrow, 0.0), axis=-1, keepdims=True)
                s = lax.dot_general(qh, kk, (((1,), (1,)), ((), ())), preferred_element_type=F32) * scale
                p = jnp.where(valid, jnp.exp(s - mh), 0.0) / nh
                pb = p.astype(BF16)
                dv_h = lax.dot_general(pb, dyb, (((0,), (0,)), ((), ())), preferred_element_type=F32)
                dp = lax.dot_general(dyb, vv, (((1,), (1,)), ((), ())), preferred_element_type=F32)
                ds = (p * (dp - dsum) * scale).astype(BF16)
                dq_h = jnp.dot(ds, kk, preferred_element_type=F32)
                dk_h = lax.dot_general(ds, qh, (((0,), (0,)), ((), ())), preferred_element_type=F32)
                dq_acc += jnp.where(hm, dq_h, 0.0)
                dk_acc += dk_h
                dv_acc += dv_h
            dq_ref[rows, :] += dq_acc
            dkcat[krows, :] += dk_acc
            dvcat[krows, :] += dv_acc

        base = pl.multiple_of(b * T, T)
        dk_ref[pl.ds(base, T), :] += dkcat[T:, :]
        dv_ref[pl.ds(base, T), :] += dvcat[T:, :]

        @pl.when(b > 0)
        def _():
            prev = pl.multiple_of(b * T - T, T)
            dk_ref[pl.ds(prev, T), :] += dkcat[0:T, :]
            dv_ref[pl.ds(prev, T), :] += dvcat[0:T, :]

    qc, _ = _attn_specs(T, 0)
    kc, kp = _attn_specs(T, ATTN_W // LANE)
    vc, vp = _attn_specs(T, 2 * ATTN_W // LANE)
    whole = pl.BlockSpec((S, LANE), lambda hp, b: (0, hp))
    shp = jax.ShapeDtypeStruct((S, ATTN_W), F32)
    return pl.pallas_call(
        body, name="attn_bwd", grid=(ATTN_W // LANE, S // T),
        in_specs=[qc, kc, kp, vc, vp, qc, qc, qc, qc], out_specs=[qc, whole, whole], out_shape=[shp, shp, shp],
        scratch_shapes=[pltpu.VMEM((2 * T, LANE), F32)] * 4,
        compiler_params=_cp(("parallel", "arbitrary")),
    )(z, z, z, z, z, dya, ya, mg, den)


def _ssm_disc(lr, li, logdt, br, bi):
    dt = jnp.exp(logdt)
    mag = jnp.exp(lr * dt)
    ar = mag * jnp.cos(li * dt)
    ai = mag * jnp.sin(li * dt)
    nr, ni = ar - 1.0, ai
    den = lr * lr + li * li
    cr = (nr * lr + ni * li) / den
    ci = (ni * lr - nr * li) / den
    return ar, ai, cr * br - ci * bi, cr * bi + ci * br


def _ssm_prep(lr, li, logdt, br, bi):
    n, c = br.shape
    outs, _ = _rowwise("ssm_prep", lambda *a: (list(_ssm_disc(*a)), []), n, _tile(n, 512),
                       [_full(a) for a in (lr, li, logdt, br, bi)], [],
                       [(1, 1, _c0, F32), (1, 1, _c0, F32), (c, c, _c0, F32), (c, c, _c0, F32)])
    return outs


def _ssm_prep_bwd(lr, li, logdt, br, bi, dar, dai, dbbr, dbbi):
    n, c = br.shape

    def f(lrb, lib, dtb, brb, bib, *cts):
        _, vjp = jax.vjp(_ssm_disc, lrb, lib, dtb, brb, bib)
        return list(vjp(cts)), []

    outs, _ = _rowwise("ssm_prep_bwd", f, n, _tile(n, 512),
                       [_full(a) for a in (lr, li, logdt, br, bi, dar, dai, dbbr, dbbi)], [],
                       [(1, 1, _c0, F32)] * 3 + [(c, c, _c0, F32)] * 2)
    return outs


def _cmul(ar, ai, br, bi):
    return ar * br - ai * bi, ar * bi + ai * br


def _scan_consts(ar, ai, reverse):
    w = ar.shape[-1]
    a1 = (jnp.broadcast_to(ar, (8, w)), jnp.broadcast_to(ai, (8, w)))
    a2 = _cmul(*a1, *a1)
    a4 = _cmul(*a2, *a2)
    a8 = _cmul(*a4, *a4)
    row = lax.broadcasted_iota(jnp.int32, (8, w), 0)
    e = (8 - row) if reverse else (row + 1)
    one, zero = jnp.ones((8, w), F32), jnp.zeros((8, w), F32)
    pw = (one, zero)
    for bit, ap in ((1, a1), (2, a2), (4, a4), (8, a8)):
        sel = (e & bit) != 0
        nxt = _cmul(*pw, *ap)
        pw = (jnp.where(sel, nxt[0], pw[0]), jnp.where(sel, nxt[1], pw[1]))
    steps = []
    for sh, (pr, pi) in zip((1, 2, 4), (a1, a2, a4)):
        keep = (row < 8 - sh) if reverse else (row >= sh)
        steps.append((jnp.where(keep, pr, 0.0), jnp.where(keep, pi, 0.0)))
    return steps, pw, row


def _scan_group(xr, xi, cr, ci, consts, reverse):
    steps, pw, _ = consts
    for sh, (pr, pi) in zip((1, 2, 4), steps):
        by = 8 - sh if reverse else sh
        tr_, ti_ = _cmul(pr, pi, pltpu.roll(xr, by, 0), pltpu.roll(xi, by, 0))
        xr = xr + tr_
        xi = xi + ti_
    tr_, ti_ = _cmul(pw[0], pw[1], cr, ci)
    return xr + tr_, xi + ti_


def _ssm_fwd(z, a_r, a_i, bdr, bdi, cmr, cmi, dskip, ts, rider=None):
    S, ZW = z.shape
    NS = SSM_G * SSM_P
    PW = PACK * SSM_P
    uoff = (ZW - SSM_W) // LANE
    nsteps = S // ts

    def body(u_ref, ar_ref, ai_ref, bdr_ref, bdi_ref, cmr_ref, cmi_ref, d_ref, hr_ref, hi_ref, y_ref, car_r, car_i):
        s = pl.program_id(1)

        @pl.when(s == 0)
        def _():
            car_r[...] = jnp.zeros_like(car_r)
            car_i[...] = jnp.zeros_like(car_i)

        u = u_ref[...]
        ub = u.astype(BF16)
        nt = (((1,), (1,)), ((), ()))
        hr_ref[...] = lax.dot_general(ub, bdr_ref[...], nt, preferred_element_type=F32)
        hi_ref[...] = lax.dot_general(ub, bdi_ref[...], nt, preferred_element_type=F32)
        consts = _scan_consts(ar_ref[...], ai_ref[...], False)

        def step(j, carry):
            rows = pl.ds(pl.multiple_of(j * 8, 8), 8)
            hr, hi = _scan_group(hr_ref[rows, :], hi_ref[rows, :], carry[0], carry[1], consts, False)
            hr_ref[rows, :] = hr
            hi_ref[rows, :] = hi
            return jnp.broadcast_to(hr[7:8, :], (8, PW)), jnp.broadcast_to(hi[7:8, :], (8, PW))

        cr, ci = lax.fori_loop(0, ts // 8, step, (car_r[...], car_i[...]))
        car_r[...] = cr
        car_i[...] = ci
        y = lax.dot_general(hr_ref[...].astype(BF16), cmr_ref[...], nt, preferred_element_type=F32)
        y -= lax.dot_general(hi_ref[...].astype(BF16), cmi_ref[...], nt, preferred_element_type=F32)
        y_ref[...] = y + d_ref[...] * u

    row_a = pl.BlockSpec((1, PW), lambda i, s: (0, i))
    (hr, hi, y), extra = _hosted(
        "ssm_fwd", body, (SSM_G // PACK, nsteps),
        [pl.BlockSpec((ts, LANE), lambda i, s: (s, uoff + i)), row_a, row_a,
         pl.BlockSpec((None, PW, LANE), lambda i, s: (i, 0, 0)), pl.BlockSpec((None, PW, LANE), lambda i, s: (i, 0, 0)),
         pl.BlockSpec((None, LANE, PW), lambda i, s: (i, 0, 0)), pl.BlockSpec((None, LANE, PW), lambda i, s: (i, 0, 0)),
         pl.BlockSpec((1, LANE), lambda i, s: (0, i))],
        [pl.BlockSpec((ts, PW), lambda i, s: (s, i)), pl.BlockSpec((ts, PW), lambda i, s: (s, i)),
         pl.BlockSpec((ts, LANE), lambda i, s: (s, i))],
        [jax.ShapeDtypeStruct((S, NS), F32), jax.ShapeDtypeStruct((S, NS), F32), jax.ShapeDtypeStruct((S, SSM_W), F32)],
        [pltpu.VMEM((8, PW), F32), pltpu.VMEM((8, PW), F32)], (z, a_r, a_i, bdr, bdi, cmr, cmi, dskip),
        ("parallel", "arbitrary"), rider)
    return hr, hi, y, extra


def _ssm_bwd(z, dyp, hr, hi, a_r, a_i, bdr, bdi, cmr, cmi, dskip, ts, jobs=None):
    S, ZW = z.shape
    NS = SSM_G * SSM_P
    PW = PACK * SSM_P
    uoff = (ZW - SSM_W) // LANE
    nsteps = S // ts
    npk = SSM_G // PACK

    def body(u_ref, dy_ref, hr_ref, hi_ref, hpr_ref, hpi_ref, ar_ref, ai_ref, bdr_ref, bdi_ref, cmr_ref, cmi_ref,
             d_ref, du_ref, dbdr_ref, dbdi_ref, dcmr_ref, dcmi_ref, dar_ref, dai_ref, dd_ref,
             lr_s, li_s, hcr, hci, car_r, car_i):
        s = pl.program_id(1)
        first_tile = s == nsteps - 1

        @pl.when(s == 0)
        def _():
            car_r[...] = jnp.zeros_like(car_r)
            car_i[...] = jnp.zeros_like(car_i)
            for r in (dbdr_ref, dbdi_ref, dcmr_ref, dcmi_ref, dar_ref, dai_ref, dd_ref):
                r[...] = jnp.zeros_like(r)

        u, dy = u_ref[...], dy_ref[...]
        ub, dyb = u.astype(BF16), dy.astype(BF16)
        lr_s[...] = jnp.dot(dyb, cmr_ref[...], preferred_element_type=F32)
        li_s[...] = -jnp.dot(dyb, cmi_ref[...], preferred_element_type=F32)
        keep_prev = jnp.where(first_tile, 0.0, 1.0)
        hcr[0:8, :] = hpr_ref[...] * keep_prev
        hci[0:8, :] = hpi_ref[...] * keep_prev
        hcr[8:, :] = hr_ref[...]
        hci[8:, :] = hi_ref[...]
        consts = _scan_consts(ar_ref[...], -ai_ref[...], True)
        row = consts[2]
        ngrp = ts // 8

        def step(jj, carry):
            cr, ci, accr, acci = carry
            j = ngrp - 1 - jj
            rows = pl.ds(pl.multiple_of(j * 8, 8), 8)
            nxt = pl.ds(pl.multiple_of(j * 8 + 8, 8), 8)
            lr, li = _scan_group(lr_s[rows, :], li_s[rows, :], cr, ci, consts, True)
            lr_s[rows, :] = lr
            li_s[rows, :] = li
            pr, pi = hcr[rows, :], hci[rows, :]
            hsr = jnp.where(row == 0, jnp.broadcast_to(pr[7:8, :], (8, PW)), pltpu.roll(hcr[nxt, :], 1, 0))
            hsi = jnp.where(row == 0, jnp.broadcast_to(pi[7:8, :], (8, PW)), pltpu.roll(hci[nxt, :], 1, 0))
            accr = accr + lr * hsr + li * hsi
            acci = acci + li * hsr - lr * hsi
            return jnp.broadcast_to(lr[0:1, :], (8, PW)), jnp.broadcast_to(li[0:1, :], (8, PW)), accr, acci

        zero = jnp.zeros((8, PW), F32)
        cr, ci, accr, acci = lax.fori_loop(0, ngrp, step, (car_r[...], car_i[...], zero, zero))
        car_r[...] = cr
        car_i[...] = ci
        dar_ref[...] += jnp.sum(accr, axis=0, keepdims=True)
        dai_ref[...] += jnp.sum(acci, axis=0, keepdims=True)
        lrb, lib = lr_s[...].astype(BF16), li_s[...].astype(BF16)
        du = jnp.dot(lrb, bdr_ref[...], preferred_element_type=F32)
        du += jnp.dot(lib, bdi_ref[...], preferred_element_type=F32)
        du_ref[...] = du + dy * d_ref[...]
        tn = (((0,), (0,)), ((), ()))
        dbdr_ref[...] += lax.dot_general(lrb, ub, tn, preferred_element_type=F32)
        dbdi_ref[...] += lax.dot_general(lib, ub, tn, preferred_element_type=F32)
        dcmr_ref[...] += lax.dot_general(dyb, hr_ref[...].astype(BF16), tn, preferred_element_type=F32)
        dcmi_ref[...] -= lax.dot_general(dyb, hi_ref[...].astype(BF16), tn, preferred_element_type=F32)
        dd_ref[...] += jnp.sum(dy * u, axis=0, keepdims=True)

    rev = lambda s: nsteps - 1 - s
    row_a = pl.BlockSpec((1, PW), lambda i, s: (0, i))
    tile = pl.BlockSpec((ts, PW), lambda i, s: (rev(s), i))
    prev8 = pl.BlockSpec((8, PW), lambda i, s: (jnp.maximum(rev(s) * (ts // 8) - 1, 0), i))
    cols = pl.BlockSpec((ts, LANE), lambda i, s: (rev(s), i))
    bd = pl.BlockSpec((None, PW, LANE), lambda i, s: (i, 0, 0))
    cm = pl.BlockSpec((None, LANE, PW), lambda i, s: (i, 0, 0))
    outs, _ = _hosted(
        "ssm_bwd", body, (npk, nsteps),
        [pl.BlockSpec((ts, LANE), lambda i, s: (rev(s), uoff + i)), cols, tile, tile, prev8, prev8,
         row_a, row_a, bd, bd, cm, cm, pl.BlockSpec((1, LANE), lambda i, s: (0, i))],
        [cols, bd, bd, cm, cm, row_a, row_a, pl.BlockSpec((1, LANE), lambda i, s: (0, i))],
        [jax.ShapeDtypeStruct((S, SSM_W), F32),
         jax.ShapeDtypeStruct((npk, PW, LANE), F32), jax.ShapeDtypeStruct((npk, PW, LANE), F32),
         jax.ShapeDtypeStruct((npk, LANE, PW), F32), jax.ShapeDtypeStruct((npk, LANE, PW), F32),
         jax.ShapeDtypeStruct((1, NS), F32), jax.ShapeDtypeStruct((1, NS), F32), jax.ShapeDtypeStruct((1, SSM_W), F32)],
        [pltpu.VMEM((ts, PW), F32), pltpu.VMEM((ts, PW), F32), pltpu.VMEM((ts + 8, PW), F32),
         pltpu.VMEM((ts + 8, PW), F32), pltpu.VMEM((8, PW), F32), pltpu.VMEM((8, PW), F32)],
        (z, dyp, hr, hi, hr, hi, a_r, a_i, bdr, bdi, cmr, cmi, dskip), ("parallel", "arbitrary"), None, jobs)
    return outs


def _block_diag(m4):
    npk, g, a, b = m4.shape
    eye = jnp.eye(g, dtype=m4.dtype)
    return (m4[:, :, :, None, :] * eye[None, :, None, :, None]).reshape(npk, g * a, g * b)


def _block_diag_take(m, a, b):
    npk = m.shape[0]
    m5 = m.reshape(npk, PACK, a, PACK, b)
    return jnp.stack([m5[:, g, :, g, :] for g in range(PACK)], axis=1)


def _mix_out(ya, ypre, gl, ga, gb, bglu):
    yg = jax.nn.gelu(ypre)
    yb = yg * jax.nn.sigmoid(gl + bglu)
    return jnp.concatenate([_rms(ya, ga), _rms(yb, gb)], axis=-1)


def _tail_loss(h3, gl, pe, gf, tgt):
    h4 = h3 + jax.nn.sigmoid(gl) * pe
    err = jnp.square(_rms(h4, gf) - tgt)
    return 0.5 * jnp.mean(err, axis=-1, keepdims=True)


def kernel(x, p, ffn1_norm, ffn1_w_gate, ffn1_w_up, ffn1_w_down, mix_norm, w_in, attn_out_norm, ssm_lambda_re, ssm_lambda_im, ssm_log_dt, ssm_b_re, ssm_b_im, ssm_c_re, ssm_c_im, ssm_d, ssm_w_glu, ssm_b_glu, ssm_out_norm, w_out, ffn2_norm, ffn2_w_gate, ffn2_w_up, ffn2_w_down, ple_norm, ple_w_gate, ple_w_proj, final_norm, loss_target, m_ffn1_norm, m_ffn1_w_gate, m_ffn1_w_up, m_ffn1_w_down, m_mix_norm, m_w_in, m_attn_out_norm, m_ssm_lambda_re, m_ssm_lambda_im, m_ssm_log_dt, m_ssm_b_re, m_ssm_b_im, m_ssm_c_re, m_ssm_c_im, m_ssm_d, m_ssm_w_glu, m_ssm_b_glu, m_ssm_out_norm, m_w_out, m_ffn2_norm, m_ffn2_w_gate, m_ffn2_w_up, m_ffn2_w_down, m_ple_norm, m_ple_w_gate, m_ple_w_proj, m_final_norm, v_ffn1_norm, v_ffn1_w_gate, v_ffn1_w_up, v_ffn1_w_down, v_mix_norm, v_w_in, v_attn_out_norm, v_ssm_lambda_re, v_ssm_lambda_im, v_ssm_log_dt, v_ssm_b_re, v_ssm_b_im, v_ssm_c_re, v_ssm_c_im, v_ssm_d, v_ssm_w_glu, v_ssm_b_glu, v_ssm_out_norm, v_w_out, v_ffn2_norm, v_ffn2_w_gate, v_ffn2_w_up, v_ffn2_w_down, v_ple_norm, v_ple_w_gate, v_ple_w_proj, v_final_norm):
    A = dict(locals())
    xs = x[0]
    ps = p[0, 0]
    tgt = loss_target[0]
    S, D = xs.shape
    FSH = ffn1_w_gate.shape[-1]
    FSP = -(-FSH // LANE) * LANE
    TR = _tile(S, 256)
    ZW = 3 * ATTN_W + SSM_W

    wgu1 = _prep("prep_gu1", [ffn1_w_gate[0], ffn1_w_up[0]], D, FSP)
    wgu2 = _prep("prep_gu2", [ffn2_w_gate[0], ffn2_w_up[0]], D, FSP)
    wd1 = _prep("prep_d1", [ffn1_w_down[0]], FSP, D)
    wd2 = _prep("prep_d2", [ffn2_w_down[0]], FSP, D)
    win = _prep("prep_in", [w_in[0]], D, w_in.shape[-1])
    wglu = _prep("prep_glu", [ssm_w_glu[0]], ssm_w_glu.shape[1], SSM_W)
    wout = _prep("prep_out", [w_out[0]], w_out.shape[1], D)
    wpg = _prep("prep_pg", [ple_w_gate[0]], ple_w_gate.shape[1], D)
    wpp = _prep("prep_pp", [ple_w_proj[0]], ple_w_proj.shape[1], ple_w_proj.shape[2])
    (Wgu1,) = _all_gather("ag_weights", [wgu1])
    rowstack = lambda w: w.reshape(1, w.shape[0] * w.shape[1], w.shape[2])

    def ffn_norm(tag, h, gain):
        return _rowwise(f"{tag}_norm", lambda a, g: ([_rms(a, g)], []), S, TR, [_full(h)], [gain], [(D, D, _c0, BF16)])[0][0]

    xn1 = ffn_norm("ffn1", xs, ffn1_norm)
    gu1, hid1, (Wd1, Win) = _ffn_up("ffn1_up", xn1, Wgu1, rider=_GatherRider([wd1, win]))
    Wd1 = rowstack(Wd1)
    h1, (Wd2,) = _mm_nn("ffn1_down", hid1, Wd1, tn=D, tk=2 * FSP, res=xs, scale=0.5, rider=_GatherRider([wd2]))
    Wd2 = rowstack(Wd2)

    (un,), _ = _rowwise("mix_norm", lambda a, g: ([_rms(a, g)], []), S, TR, [_full(h1)], [mix_norm], [(D, D, _c0, BF16)])
    z, (Wglu, Wout, Wpg, Wpp) = _mm_nn("mix_in", un, Win, tn=512, tk=D,
                                       rider=_GatherRider([wglu, wout, wpg, wpp]))
    Wglu, Wout, Wpg = rowstack(Wglu), rowstack(Wout), rowstack(Wpg)
    ya, mg, den = _attn_fwd(z)

    col = lambda a: a.reshape(-1, 1)
    lr_c, li_c = col(ssm_lambda_re), col(ssm_lambda_im)
    dt_c = col(jnp.broadcast_to(ssm_log_dt.reshape(SSM_G, 1), (SSM_G, SSM_P)))
    b_re2, b_im2 = ssm_b_re.reshape(-1, SSM_C), ssm_b_im.reshape(-1, SSM_C)
    ar_c, ai_c, bbr, bbi = _ssm_prep(lr_c, li_c, dt_c, b_re2, b_im2)
    a_r, a_i = ar_c.reshape(1, -1), ai_c.reshape(1, -1)
    npk = SSM_G // PACK
    bdr = _block_diag(bbr.reshape(npk, PACK, SSM_P, SSM_C)).astype(BF16)
    bdi = _block_diag(bbi.reshape(npk, PACK, SSM_P, SSM_C)).astype(BF16)
    cmr = _block_diag(ssm_c_re.reshape(npk, PACK, SSM_C, SSM_P)).astype(BF16)
    cmi = _block_diag(ssm_c_im.reshape(npk, PACK, SSM_C, SSM_P)).astype(BF16)
    TS = _tile(S, 512)
    hr, hi, ypre, (Wgu2,) = _ssm_fwd(z, a_r, a_i, bdr, bdi, cmr, cmi, ssm_d, TS, rider=_GatherRider([wgu2]))
    (yg,), _ = _rowwise("ssm_gelu", lambda a: ([jax.nn.gelu(a)], []), S, TR, [_full(ypre)], [], [(SSM_W, SSM_W, _c0, BF16)])
    gl = _mm_nn("ssm_glu", yg, Wglu, tn=SSM_W, tk=SSM_W)
    (ycat,), _ = _rowwise("mix_out", lambda *a: ([_mix_out(*a)], []), S, TR, [_full(ya), _full(ypre), _full(gl)],
                          [attn_out_norm, ssm_out_norm, ssm_b_glu], [(MIX_W, MIX_W, _c0, BF16)])
    h2 = _mm_nn("mix_proj", ycat, Wout, tn=D // 2, tk=D, res=h1, scale=1.0)

    xn2 = ffn_norm("ffn2", h2, ffn2_norm)
    gu2, hid2, _ = _ffn_up("ffn2_up", xn2, Wgu2)
    h3 = _mm_nn("ffn2_down", hid2, Wd2, tn=D, tk=2 * FSP, res=h2, scale=0.5)

    (hn, pb), _ = _rowwise("ple_norm", lambda a, q, g: ([_rms(a, g), q], []), S, TR, [_full(h3), _full(ps)], [ple_norm],
                           [(D, D, _c0, BF16), (ps.shape[1], ps.shape[1], _c0, BF16)])
    pgl = _mm_nn("ple_gate", hn, Wpg, tn=D // 2, tk=D)
    pe = _mm_nn("ple_proj", pb, Wpp, tn=Wpp.shape[2], tk=Wpp.shape[1])

    def tail(h3b, glb, peb, tb, gf):
        rows, vjp = jax.vjp(lambda a, b, c, g: _tail_loss(a, b, c, g, tb), h3b, glb, peb, gf)
        dh, dgl, dpe, dgf = vjp(jnp.ones_like(rows))
        return [dh, dgl, dpe], [jnp.broadcast_to(jnp.sum(rows, axis=0, keepdims=True), (1, LANE)), dgf]

    (dh3_dir, dpgl, dpe), (loss_row, g_final) = _rowwise(
        "tail", tail, S, TR, [_full(h3), _full(pgl), _full(pe), _full(tgt)], [final_norm.reshape(1, D)],
        [(D, D, _c0, F32), (D, D, _c0, BF16), (D, D, _c0, BF16)], [(LANE, LANE, _c0), (D, D, _c0)])
    loss = lax.psum(loss_row[0, 0], AXES)

    def norm_bwd(tag, h, gain, dn, dres):
        def f(hb, dnb, drb, g):
            _, vjp = jax.vjp(_rms, hb, g)
            dh, dg = vjp(dnb)
            dh = dh + drb
            return [dh, dh], [dg]
        (dh, dhb), (dg,) = _rowwise(f"{tag}_norm_bwd", f, S, TR, [_full(h), _full(dn), _full(dres)], [gain],
                                    [(D, D, _c0, F32), (D, D, _c0, BF16)], [(D, D, _c0)])
        return dh, dhb, dg

    restack = lambda g: g.reshape((NDEV, g.shape[1] // NDEV) + g.shape[2:])
    jobs, scat = [], {}

    def scatter(key, g):
        scat[key] = _Scatter("rs_" + key, g)
        jobs.append(scat[key])

    late = []
    dhn = _mm_nt("ple_gate_dx", dpgl, Wpg, tn=D, tk=D)
    late.append(lambda: scatter("pg", restack(_mm_tn("ple_gate_dw", hn, dpgl, 1, jobs=jobs))))
    late.append(lambda: scatter("pp", _mm_tn("ple_proj_dw", pb, dpe, NDEV, jobs=jobs)))
    dh3, dh3b, g_ple_norm = norm_bwd("ple", h3, ple_norm, dhn, dh3_dir)

    def ffn_bwd(tag, h, gain, Wgu, Wd, saved, dout, doutb):
        xn, gu, hid = saved
        dgu = _ffn_down_dx(f"{tag}_down_dx", doutb, Wd, gu, NDEV, scale=0.5, jobs=jobs)
        scatter(tag + "gu", _mm_tn(f"{tag}_up_dw", xn, dgu, NDEV, tm=2048, tn=FSP, jobs=jobs))
        scatter(tag + "d", restack(_mm_tn(f"{tag}_down_dw", hid, doutb, 1, tm=2048, tko=FSP, tn=D // 2, scale=0.5,
                                          jobs=jobs)))
        dxn = _mm_nt(f"{tag}_up_dx", dgu, Wgu, tm=512, tn=D, tk=2 * FSP, jobs=jobs)
        dh, dhb, g_norm = norm_bwd(tag, h, gain, dxn, dout)
        return dh, dhb, g_norm

    dh2, dh2b, g_ffn2_norm = ffn_bwd("ffn2", h2, ffn2_norm, Wgu2, Wd2, (xn2, gu2, hid2), dh3, dh3b)

    dycat = _mm_nt("mix_proj_dx", dh2b, Wout, tn=D, tk=D, jobs=jobs)
    late.append(lambda: scatter("out", restack(_mm_tn("mix_proj_dw", ycat, dh2b, 1, jobs=jobs))))

    def mix_out_bwd(yab, ypb, glb, dyc, ga, gb, bglu):
        _, vjp = jax.vjp(_mix_out, yab, ypb, glb, ga, gb, bglu)
        dya_, dyp_, dgl_, dga, dgb, dbg = vjp(dyc)
        return [dya_, dyp_, dgl_], [dga, dgb, dbg]
    (dya, dyp_dir, dglb), (g_attn_norm, g_ssm_norm, g_bglu) = _rowwise(
        "mix_out_bwd", mix_out_bwd, S, TR, [_full(ya), _full(ypre), _full(gl), _full(dycat)],
        [attn_out_norm, ssm_out_norm, ssm_b_glu],
        [(ATTN_W, ATTN_W, _c0, F32), (SSM_W, SSM_W, _c0, F32), (SSM_W, SSM_W, _c0, BF16)],
        [(ATTN_W, ATTN_W, _c0), (SSM_W, SSM_W, _c0), (SSM_W, SSM_W, _c0)])
    dyg = _mm_nt("ssm_glu_dx", dglb, Wglu, tn=SSM_W, tk=SSM_W, jobs=jobs)
    late.append(lambda: scatter("glu", restack(_mm_tn("ssm_glu_dw", yg, dglb, 1, jobs=jobs))))

    def gelu_bwd(ypb, dygb, ddir):
        _, vjp = jax.vjp(jax.nn.gelu, ypb)
        return [ddir + vjp(dygb)[0]], []
    (dyp,), _ = _rowwise("ssm_gelu_bwd", gelu_bwd, S, TR, [_full(ypre), _full(dyg), _full(dyp_dir)], [],
                         [(SSM_W, SSM_W, _c0, F32)])
    du, dbdr, dbdi, dcmr, dcmi, da_r, da_i, g_ssm_d = _ssm_bwd(z, dyp, hr, hi, a_r, a_i, bdr, bdi, cmr, cmi, ssm_d, TS,
                                                              jobs=jobs)
    dbbr = _block_diag_take(dbdr, SSM_P, SSM_C).reshape(-1, SSM_C)
    dbbi = _block_diag_take(dbdi, SSM_P, SSM_C).reshape(-1, SSM_C)
    g_c_re = _block_diag_take(dcmr, SSM_C, SSM_P).reshape(ssm_c_re.shape)
    g_c_im = _block_diag_take(dcmi, SSM_C, SSM_P).reshape(ssm_c_im.shape)
    dlr, dli, ddt, g_b_re, g_b_im = _ssm_prep_bwd(lr_c, li_c, dt_c, b_re2, b_im2, col(da_r), col(da_i), dbbr, dbbi)
    g_lam_re, g_lam_im = dlr.reshape(ssm_lambda_re.shape), dli.reshape(ssm_lambda_im.shape)
    g_log_dt = jnp.sum(ddt.reshape(SSM_G, SSM_P), axis=1).reshape(ssm_log_dt.shape)
    g_b_re, g_b_im = g_b_re.reshape(ssm_b_re.shape), g_b_im.reshape(ssm_b_im.shape)

    dq, dk, dv = _attn_bwd(z, dya, ya, mg, den)
    (dz,), _ = _rowwise("mix_dz", lambda *a: ([jnp.concatenate(a, axis=-1)], []), S, TR,
                        [_full(dq), _full(dk), _full(dv), _full(du)], [], [(ZW, ZW, _c0, BF16)])
    dun = _mm_nt("mix_in_dx", dz, Win, tn=D, tk=512, jobs=jobs)
    scatter("in", _mm_tn("mix_in_dw", un, dz, NDEV, tn=512, jobs=jobs))
    dh1, dh1b, g_mix_norm = norm_bwd("mix", h1, mix_norm, dun, dh2)

    dx, _dxb, g_ffn1_norm = ffn_bwd("ffn1", xs, ffn1_norm, Wgu1, Wd1, (xn1, gu1, hid1), dh1, dh1b)
    for run in (late[2], late[0], late[3], late[1]):
        run()

    mine = {key: job.finish() for key, job in scat.items()}
    out = {}

    def upd(name, idx, *, tr, cw, gw, goff=0):
        w, m, v = A[name][0], A["m_" + name][0], A["v_" + name][0]
        g, dlt, mn, vn = _adamw("adamw_" + name, w, m, v, mine[idx], tr=tr, cw=cw, gw=gw, goff=goff)
        for k, val in (("grad_", g), ("delta_", dlt), ("new_m_", mn), ("new_v_", vn)):
            out[k + name] = val[None]

    DT = _tile(D, 256)
    FT = _tile(FSH, 512)
    DC = _tile(D, 1024, LANE)
    upd("ffn1_w_gate", "ffn1gu", tr=DT, cw=FSH, gw=FSP, goff=0)
    upd("ffn1_w_up", "ffn1gu", tr=DT, cw=FSH, gw=FSP, goff=1)
    upd("ffn1_w_down", "ffn1d", tr=FT, cw=DC, gw=DC)
    upd("w_in", "in", tr=DT, cw=w_in.shape[-1], gw=w_in.shape[-1])
    upd("ssm_w_glu", "glu", tr=ssm_w_glu.shape[1], cw=SSM_W, gw=SSM_W)
    upd("w_out", "out", tr=w_out.shape[1], cw=DC, gw=DC)
    upd("ffn2_w_gate", "ffn2gu", tr=DT, cw=FSH, gw=FSP, goff=0)
    upd("ffn2_w_up", "ffn2gu", tr=DT, cw=FSH, gw=FSP, goff=1)
    upd("ffn2_w_down", "ffn2d", tr=FT, cw=DC, gw=DC)
    upd("ple_w_gate", "pg", tr=ple_w_gate.shape[1], cw=DC, gw=DC)
    upd("ple_w_proj", "pp", tr=ple_w_proj.shape[1], cw=ple_w_proj.shape[2], gw=ple_w_proj.shape[2])

    small = [("ffn1_norm", g_ffn1_norm), ("mix_norm", g_mix_norm), ("attn_out_norm", g_attn_norm),
             ("ssm_lambda_re", g_lam_re), ("ssm_lambda_im", g_lam_im), ("ssm_log_dt", g_log_dt),
             ("ssm_b_re", g_b_re), ("ssm_b_im", g_b_im), ("ssm_c_re", g_c_re), ("ssm_c_im", g_c_im),
             ("ssm_d", g_ssm_d), ("ssm_b_glu", g_bglu), ("ssm_out_norm", g_ssm_norm), ("ffn2_norm", g_ffn2_norm),
             ("ple_norm", g_ple_norm), ("final_norm", g_final)]
    chunk = 8 * LANE

    def pack(arrs):
        parts = []
        for a in arrs:
            flat = a.reshape(-1)
            padn = -(-flat.shape[0] // chunk) * chunk
            parts.append(jnp.pad(flat, (0, padn - flat.shape[0])).reshape(-1, LANE))
        return jnp.concatenate(parts, axis=0)

    g_pack = pack([g for _, g in small])
    (g_all,) = _all_gather("ag_small", [g_pack])
    g_sum = _sum8("small_sum", g_all)
    w_pack = pack([A[n] for n, _ in small])
    m_pack = pack([A["m_" + n] for n, _ in small])
    v_pack = pack([A["v_" + n] for n, _ in small])
    d_pack, mn_pack, vn_pack = _adamw_small("adamw_small", w_pack, m_pack, v_pack, g_sum)
    off = 0
    for n, _ in small:
        shape = A[n].shape
        size = math.prod(shape)
        rows = -(-size // chunk) * 8
        for k, buf in (("grad_", g_sum), ("delta_", d_pack), ("new_m_", mn_pack), ("new_v_", vn_pack)):
            out[k + n] = buf[off:off + rows].reshape(-1)[:size].reshape(shape)
        off += rows

    names = ['ffn1_norm', 'ffn1_w_gate', 'ffn1_w_up', 'ffn1_w_down', 'mix_norm', 'w_in', 'attn_out_norm',
             'ssm_lambda_re', 'ssm_lambda_im', 'ssm_log_dt', 'ssm_b_re', 'ssm_b_im', 'ssm_c_re', 'ssm_c_im', 'ssm_d',
             'ssm_w_glu', 'ssm_b_glu', 'ssm_out_norm', 'w_out', 'ffn2_norm', 'ffn2_w_gate', 'ffn2_w_up', 'ffn2_w_down',
             'ple_norm', 'ple_w_gate', 'ple_w_proj', 'final_norm']
    return (loss, dx[None], *[out[k + n] for k in ("grad_", "delta_", "new_m_", "new_v_") for n in names])
```

```python
import functools
import math

import jax
import jax.numpy as jnp
from jax import lax
from jax.experimental import pallas as pl
from jax.experimental.pallas import tpu as pltpu

F32, BF16 = jnp.float32, jnp.bfloat16
MESH = pl.DeviceIdType.MESH
NDEV = 8
AXES = ("x", "y", "c")
LANE = 128
VMEM_LIMIT = 56 * 1024 * 1024

ATTN_W = 1024
HEAD_DIM = 64
SSM_W = 1024
MIX_W = ATTN_W + SSM_W
SSM_G, SSM_P, SSM_C = 64, 64, 16
PACK = 8
DILATIONS = (1, 4, 16)
QB = 128
NORM_EPS = 1e-6
MASK_VALUE = -1e30
LR, B1, B2, EPS, WD, STEP = 0.001, 0.9, 0.999, 1e-08, 0.01, 10


def _cp(sem=None):
    return pltpu.CompilerParams(dimension_semantics=sem, vmem_limit_bytes=VMEM_LIMIT)


def _tile(n, target, mult=8):
    if n <= target:
        return n
    for t in range(target - target % mult, 0, -mult):
        if n % t == 0:
            return t
    return n


def _rms(x, g):
    return x * lax.rsqrt(jnp.mean(x * x, axis=-1, keepdims=True) + NORM_EPS) * g


def _rowwise(name, fn, S, tr, rows, fulls, outs, accs=(), ncol=1):
    nr, nf, no, na = len(rows), len(fulls), len(outs), len(accs)

    def body(*refs):
        ins = [r[...] for r in refs[:nr + nf]]
        o_refs = refs[nr + nf:nr + nf + no]
        a_refs = refs[nr + nf + no:]
        o_vals, a_vals = fn(*ins)
        for r, v in zip(o_refs, o_vals):
            r[...] = v.astype(r.dtype)
        if na:
            @pl.when(pl.program_id(1) == 0)
            def _():
                for r in a_refs:
                    r[...] = jnp.zeros_like(r)
            for r, v in zip(a_refs, a_vals):
                r[...] += v

    in_specs = [pl.BlockSpec((tr, w), functools.partial(lambda j, i, cm: (i, cm(j)), cm=cm)) for _, w, cm in rows]
    in_specs += [pl.BlockSpec(f.shape, functools.partial(lambda j, i, nd: (0,) * nd, nd=f.ndim)) for f in fulls]
    out_specs = [pl.BlockSpec((tr, w), functools.partial(lambda j, i, cm: (i, cm(j)), cm=cm)) for _, w, cm, _ in outs]
    out_specs += [pl.BlockSpec((1, w), functools.partial(lambda j, i, cm: (0, cm(j)), cm=cm)) for _, w, cm in accs]
    out_shape = [jax.ShapeDtypeStruct((S, c), dt) for c, _, _, dt in outs]
    out_shape += [jax.ShapeDtypeStruct((1, c), F32) for c, _, _ in accs]
    res = pl.pallas_call(
        body, name=name, grid=(ncol, S // tr), in_specs=in_specs, out_specs=out_specs, out_shape=out_shape,
        compiler_params=_cp(("parallel", "arbitrary" if na else "parallel")),
    )(*[a for a, _, _ in rows], *fulls)
    return res[:no], res[no:]


def _c0(j):
    return 0


def _full(a):
    return (a, a.shape[1], _c0)


def _hosted(name, body, grid, in_specs, out_specs, out_shape, scratch, args, sem, rider=None, jobs=None):
    nsteps = math.prod(grid)

    def step_of(*g):
        t = 0
        for gi, n in zip(g, grid):
            t = t * n + gi
        return t

    job = None
    if rider is None:
        job, rider = _pick(jobs, nsteps)
    if rider is None:
        outs = pl.pallas_call(body, name=name, grid=grid, in_specs=in_specs, out_specs=out_specs, out_shape=out_shape,
                              scratch_shapes=scratch, compiler_params=_cp(sem))(*args)
        return outs, None
    rider.bind(step_of, nsteps)
    n_in, n_out, n_scr = len(in_specs), len(out_specs), len(scratch)

    def full(*refs):
        a, b = n_in, n_in + rider.n_in
        c, d = b + n_out, b + n_out + rider.n_out
        rider.run(refs[a:b], refs[c:d], refs[d + n_scr:], step_of(*[pl.program_id(i) for i in range(len(grid))]), nsteps)
        body(*(refs[:a] + refs[b:c] + refs[d:d + n_scr]))

    outs = pl.pallas_call(
        full, name=name, grid=grid, in_specs=in_specs + rider.in_specs, out_specs=out_specs + rider.out_specs,
        out_shape=out_shape + rider.out_shape, scratch_shapes=scratch + rider.scratch,
        compiler_params=_cp(("arbitrary",) * len(grid)))(*args, *rider.operands)
    extra = rider.take(outs[n_out:])
    if job is not None:
        job.advance(extra)
        extra = None
    return outs[:n_out], extra


def _mm_nn(name, a, w, *, out_dtype=F32, tm=512, tn=768, tk=2048, res=None, scale=1.0, gain=None, rider=None,
           jobs=None):
    M, K = a.shape
    J, K2, Np = w.shape
    assert K == K2
    tm, tn, tk = _tile(M, tm), _tile(Np, tn, LANE), _tile(K, tk, LANE)
    npj = Np // tn
    nk = K // tk
    grid = (M // tm, J * npj, nk)
    assert gain is None or (J * npj == 1 and res is not None)

    def body(*refs):
        refs = list(refs)
        a_ref, w_ref = refs[:2]
        r_ref = refs[2] if res is not None else None
        g_ref = refs[3] if gain is not None else None
        acc = refs[-1]
        o_ref = refs[-3] if gain is not None else refs[-2]
        k = pl.program_id(2)
        part = jnp.dot(a_ref[...].astype(BF16), w_ref[...], preferred_element_type=F32)

        def finish(v):
            if res is not None:
                v = r_ref[...] + scale * v
            o_ref[...] = v.astype(o_ref.dtype)
            if gain is not None:
                refs[-2][...] = _rms(v, g_ref[...]).astype(BF16)

        if nk == 1:
            finish(part)
            return

        @pl.when(k == 0)
        def _():
            acc[...] = part

        @pl.when(k > 0)
        def _():
            acc[...] += part

        @pl.when(k == nk - 1)
        def _():
            finish(acc[...])

    in_specs = [pl.BlockSpec((tm, tk), lambda i, n, k: (i, k)),
                pl.BlockSpec((None, tk, tn), lambda i, n, k: (n // npj, k, n % npj))]
    args = [a, w]
    if res is not None:
        in_specs.append(pl.BlockSpec((tm, tn), lambda i, n, k: (i, n)))
        args.append(res)
    out_specs = [pl.BlockSpec((tm, tn), lambda i, n, k: (i, n))]
    out_shape = [jax.ShapeDtypeStruct((M, J * Np), out_dtype)]
    if gain is not None:
        in_specs.append(pl.BlockSpec((1, tn), lambda i, n, k: (0, 0)))
        args.append(gain)
        out_specs.append(pl.BlockSpec((tm, tn), lambda i, n, k: (i, n)))
        out_shape.append(jax.ShapeDtypeStruct((M, J * Np), BF16))
    outs, extra = _hosted(name, body, grid, in_specs, out_specs, out_shape, [pltpu.VMEM((tm, tn), F32)], args,
                          ("parallel", "parallel", "arbitrary"), rider, jobs)
    outs = tuple(outs) + (() if rider is None else (extra,))
    return outs[0] if len(outs) == 1 else outs


def _mm_nt(name, dy, w, *, out_dtype=F32, tm=512, tn=2048, tk=768, scale=1.0, jobs=None):
    M, N = dy.shape
    J, K, Np = w.shape
    assert N == J * Np
    tm, tn, tk = _tile(M, tm), _tile(K, tn, LANE), _tile(Np, tk, LANE)
    npj = Np // tk
    nc = J * npj

    def body(a_ref, w_ref, o_ref, acc):
        c = pl.program_id(2)
        part = lax.dot_general(a_ref[...].astype(BF16), w_ref[...], (((1,), (1,)), ((), ())),
                               preferred_element_type=F32)
        if nc == 1:
            o_ref[...] = (scale * part).astype(o_ref.dtype)
            return

        @pl.when(c == 0)
        def _():
            acc[...] = part

        @pl.when(c > 0)
        def _():
            acc[...] += part

        @pl.when(c == nc - 1)
        def _():
            o_ref[...] = (scale * acc[...]).astype(o_ref.dtype)

    (out,), _ = _hosted(
        name, body, (M // tm, K // tn, nc),
        [pl.BlockSpec((tm, tk), lambda i, n, c: (i, c)),
         pl.BlockSpec((None, tn, tk), lambda i, n, c: (c // npj, n, c % npj))],
        [pl.BlockSpec((tm, tn), lambda i, n, c: (i, n))], [jax.ShapeDtypeStruct((M, K), out_dtype)],
        [pltpu.VMEM((tm, tn), F32)], (dy, w), ("parallel", "parallel", "arbitrary"), None, jobs)
    return out


def _mm_tn(name, x, dy, J, *, tm=1024, tko=1024, tn=768, scale=1.0, jobs=None):
    M, K = x.shape
    M2, N = dy.shape
    assert M == M2 and N % J == 0
    Np = N // J
    tm, tko, tn = _tile(M, tm, LANE), _tile(K, tko, LANE), _tile(Np, tn, LANE)
    npj = Np // tn
    nm = M // tm

    def body(x_ref, d_ref, o_ref, acc):
        m = pl.program_id(2)
        part = lax.dot_general(x_ref[...].astype(BF16), d_ref[...].astype(BF16), (((0,), (0,)), ((), ())),
                               preferred_element_type=F32)
        if nm == 1:
            o_ref[...] = scale * part
            return

        @pl.when(m == 0)
        def _():
            acc[...] = part

        @pl.when(m > 0)
        def _():
            acc[...] += part

        @pl.when(m == nm - 1)
        def _():
            o_ref[...] = scale * acc[...]

    (out,), _ = _hosted(
        name, body, (K // tko, J * npj, nm),
        [pl.BlockSpec((tm, tko), lambda k, n, m: (m, k)), pl.BlockSpec((tm, tn), lambda k, n, m: (m, n))],
        [pl.BlockSpec((None, tko, tn), lambda k, n, m: (n // npj, k, n % npj))],
        [jax.ShapeDtypeStruct((J, K, Np), F32)], [pltpu.VMEM((tko, tn), F32)], (x, dy),
        ("parallel", "parallel", "arbitrary"), None, jobs)
    return out


def _swiglu_act(g, u):
    return jax.nn.silu(g) * u


def _ffn_up(name, xn, wgu, *, tm=1024, rider=None):
    M, K = xn.shape
    J, _, F2 = wgu.shape
    F = F2 // 2
    tm = _tile(M, tm)

    def body(a_ref, w_ref, gu_ref, h_ref):
        r = jnp.dot(a_ref[...], w_ref[...], preferred_element_type=F32)
        gu_ref[...] = r.astype(gu_ref.dtype)
        h_ref[...] = _swiglu_act(r[:, :F], r[:, F:]).astype(h_ref.dtype)

    (gu, hid), extra = _hosted(
        name, body, (M // tm, J),
        [pl.BlockSpec((tm, K), lambda i, j: (i, 0)), pl.BlockSpec((None, K, F2), lambda i, j: (j, 0, 0))],
        [pl.BlockSpec((tm, F2), lambda i, j: (i, j)), pl.BlockSpec((tm, F), lambda i, j: (i, j))],
        [jax.ShapeDtypeStruct((M, J * F2), BF16), jax.ShapeDtypeStruct((M, J * F), BF16)], [], (xn, wgu),
        ("parallel", "parallel"), rider)
    return gu, hid, extra


def _ffn_down_dx(name, dout, wd, gu, J, *, scale, tm=512, jobs=None):
    M, D = dout.shape
    F = wd.shape[1] // J
    tm = _tile(M, tm)

    def body(d_ref, w_ref, gu_ref, o_ref):
        dh = scale * lax.dot_general(d_ref[...], w_ref[...], (((1,), (1,)), ((), ())), preferred_element_type=F32)
        gu = gu_ref[...].astype(F32)
        _, vjp = jax.vjp(_swiglu_act, gu[:, :F], gu[:, F:])
        o_ref[...] = jnp.concatenate(vjp(dh), axis=-1).astype(o_ref.dtype)

    (out,), _ = _hosted(
        name, body, (M // tm, J),
        [pl.BlockSpec((tm, D), lambda i, j: (i, 0)), pl.BlockSpec((None, F, D), lambda i, j: (0, j, 0)),
         pl.BlockSpec((tm, 2 * F), lambda i, j: (i, j))],
        [pl.BlockSpec((tm, 2 * F), lambda i, j: (i, j))], [jax.ShapeDtypeStruct((M, J * 2 * F), BF16)], [],
        (dout, wd, gu), ("parallel", "parallel"), None, jobs)
    return out


def _all_gather(name, shards):
    n = len(shards)

    def body(*refs):
        start, forward, finish = _gather_phases(refs[:n], refs[n:2 * n], *refs[2 * n:])
        start()
        forward()
        finish()

    any_spec = pl.BlockSpec(memory_space=pl.ANY)
    return pl.pallas_call(
        body, name=name, in_specs=[any_spec] * n, out_specs=[any_spec] * n,
        out_shape=[jax.ShapeDtypeStruct((NDEV,) + s.shape, s.dtype) for s in shards],
        scratch_shapes=_gather_sems(n),
    )(*shards)


def _gather_sems(n):
    return [pltpu.SemaphoreType.DMA((n, 7)), pltpu.SemaphoreType.DMA((n, 7)), pltpu.SemaphoreType.DMA((n,))]


def _gather_phases(ins, outs, send_sems, recv_sems, local_sems):
    n = len(ins)
    x, y, c = lax.axis_index("x"), lax.axis_index("y"), lax.axis_index("c")
    me, sibling = (x, y, c), (x, y, 1 - c)
    chips = [(1 - x, y), (x, 1 - y), (1 - x, 1 - y)]

    def blk(i, px, py, pc):
        return outs[i].at[4 * px + 2 * py + pc]

    def copy(i, k, block, to, src=None):
        return pltpu.make_async_remote_copy(
            src_ref=blk(i, *block) if src is None else src, dst_ref=blk(i, *block),
            send_sem=send_sems.at[i, k], recv_sem=recv_sems.at[i, k], device_id=to, device_id_type=MESH)

    def local(i):
        return pltpu.make_async_copy(ins[i], blk(i, *me), local_sems.at[i])

    def firsts(i):
        return [copy(i, 0, me, sibling, src=ins[i])] + [copy(i, 1 + j, me, (*chip, c), src=ins[i])
                                                        for j, chip in enumerate(chips)]

    def start():
        for i in range(n):
            local(i).start()
        for i in range(n):
            for cp in firsts(i):
                cp.start()

    def forward():
        for i in range(n):
            for j, chip in enumerate(chips):
                copy(i, 1 + j, (*chip, c), me).wait_recv()
                copy(i, 4 + j, (*chip, c), sibling).start()

    def finish():
        for i in range(n):
            copy(i, 0, sibling, me).wait_recv()
            for j, chip in enumerate(chips):
                copy(i, 4 + j, (*chip, 1 - c), me).wait_recv()
        for i in range(n):
            for cp in firsts(i):
                cp.wait_send()
            for j, chip in enumerate(chips):
                copy(i, 4 + j, (*chip, c), sibling).wait_send()
            local(i).wait()

    return start, forward, finish


class _GatherRider:
    def __init__(self, shards):
        self.operands = list(shards)
        n = len(self.operands)
        self.n_in = self.n_out = n
        any_spec = pl.BlockSpec(memory_space=pl.ANY)
        self.in_specs = [any_spec] * n
        self.out_specs = [any_spec] * n
        self.out_shape = [jax.ShapeDtypeStruct((NDEV,) + s.shape, s.dtype) for s in self.operands]
        self.scratch = _gather_sems(n)

    def bind(self, step_of, nsteps):
        return self

    def take(self, outs):
        return list(outs)

    def run(self, ins, outs, sems, step, nsteps):
        start, forward, finish = _gather_phases(ins, outs, *sems)
        pl.when(step == 0)(start)
        pl.when(step == (4 * nsteps) // 5)(forward)
        pl.when(step == nsteps - 1)(finish)


class _SwapRider:
    def __init__(self, arr, streams, grid, tile, out_shape, out_block, out_map):
        self.arr, self.streams, self.grid, self.tile = arr, streams, grid, tile
        self.ns, self.n = len(streams), grid[0] * grid[1]
        self.operands = [arr] * (2 * self.ns)
        self.n_in, self.n_out = 2 * self.ns, 1
        self.out_shape = [jax.ShapeDtypeStruct(out_shape, F32)]
        self.out_block, self.out_map = out_block, out_map
        tr, C = tile
        slots = [pltpu.VMEM((2, tr, C), w) for _, w, _, _ in streams]
        self.scratch = slots + slots + [pltpu.SemaphoreType.DMA((self.ns, 2)), pltpu.SemaphoreType.DMA((self.ns, 2)),
                                        pltpu.SemaphoreType.REGULAR((self.ns,))]

    def bind(self, step_of, nsteps):
        assert nsteps >= self.n
        self.period = period = nsteps // self.n
        n, nr = self.n, self.grid[1]

        def ids(*g):
            k = jnp.minimum(step_of(*g) // period, n - 1)
            pos = {a: lax.axis_index(a) for a in AXES}
            return k // nr, k % nr, [v for a in AXES for v in (pos[a], 1 - pos[a])]

        block = (None,) * (self.arr.ndim - 2) + tuple(self.tile)
        self.in_specs = []
        for _, _, keep_map, send_map in self.streams:
            for m in (keep_map, send_map):
                self.in_specs.append(pl.BlockSpec(block, functools.partial(lambda *g, m: m(*ids(*g)), m=m)))
        self.out_specs = [pl.BlockSpec(self.out_block, lambda *g: self.out_map(*ids(*g)))]
        return self

    def take(self, outs):
        return outs[0]

    def run(self, ins, outs, scratch, step, nsteps):
        ns, n, period = self.ns, self.n, self.period
        keeps, sends, o_ref = ins[0::2], ins[1::2], outs[0]
        lands, stages = scratch[:ns], scratch[ns:2 * ns]
        send_sems, recv_sems, credits = scratch[2 * ns:]
        k = step // period
        slot = k % 2
        here = {a: lax.axis_index(a) for a in AXES}
        peers = [tuple(1 - here[a] if a == axis else here[a] for a in AXES) for axis, _, _, _ in self.streams]

        def rdma(s):
            return pltpu.make_async_remote_copy(
                src_ref=stages[s].at[slot], dst_ref=lands[s].at[slot], send_sem=send_sems.at[s, slot],
                recv_sem=recv_sems.at[s, slot], device_id=peers[s], device_id_type=MESH)

        @pl.when((k < n) & (step % period == 0))
        def _():
            @pl.when(k >= 2)
            def _():
                for s in range(ns):
                    pl.semaphore_wait(credits.at[s], 1)

            for s in range(ns):
                stages[s][slot] = sends[s][...].astype(stages[s].dtype)
                rdma(s).start()

        @pl.when((k < n) & (step % period == period - 1))
        def _():
            for s in range(ns):
                rdma(s).wait_recv()
                total = keeps[s][...] + lands[s][slot].astype(F32)
                if ns == 1:
                    o_ref[...] = total
                else:
                    o_ref[s] = total
            for s in range(ns):
                rdma(s).wait_send()

            @pl.when(k + 2 < n)
            def _():
                for s in range(ns):
                    pl.semaphore_signal(credits.at[s], inc=1, device_id=peers[s], device_id_type=MESH)


def _run_alone(name, rider):
    rider.bind(lambda t: t, rider.n)

    def body(*refs):
        a, b = rider.n_in, rider.n_in + rider.n_out
        rider.run(refs[:a], refs[a:b], refs[b:], pl.program_id(0), rider.n)

    outs = pl.pallas_call(
        body, name=name, grid=(rider.n,), in_specs=rider.in_specs, out_specs=rider.out_specs,
        out_shape=rider.out_shape, scratch_shapes=rider.scratch, compiler_params=_cp(("arbitrary",)),
    )(*rider.operands)
    return rider.take(outs)


class _Scatter:
    def __init__(self, name, g):
        self.name, self.cur, self.stage = name, g, 0
        _, self.R, self.C = g.shape

    def done(self):
        return self.stage == 3

    def rider(self, rows):
        R, C = self.R, self.C
        R2 = R // 2
        tr = _tile(R2, rows, 16)
        nrh = R2 // tr
        if self.stage == 0:
            return _SwapRider(
                self.cur.reshape(4, 2, R, C),
                [("c", BF16, lambda b, i, s: (b, s[4], i, 0), lambda b, i, s: (b, s[5], i, 0))],
                (4, 2 * nrh), (tr, C), (2, 4, R2, C), (None, None, tr, C), lambda b, i, s: (i // nrh, b, i % nrh, 0))
        if self.stage == 1:
            return _SwapRider(
                self.cur.reshape(2, 2, 2, R2, C),
                [("y", BF16, lambda b, i, s: (0, b, s[2], i, 0), lambda b, i, s: (0, b, s[3], i, 0)),
                 ("x", BF16, lambda b, i, s: (1, s[0], b, i, 0), lambda b, i, s: (1, s[1], b, i, 0))],
                (2, nrh), (tr, C), (2, 2, R2, C), (2, None, tr, C), lambda b, i, s: (0, b, i, 0))
        return _SwapRider(
            self.cur,
            [("x", BF16, lambda b, i, s: (0, s[0], i, 0), lambda b, i, s: (0, s[1], i, 0)),
             ("y", BF16, lambda b, i, s: (1, s[2], i, 0), lambda b, i, s: (1, s[3], i, 0))],
            (1, nrh), (tr, C), (2, R2, C), (2, tr, C), lambda b, i, s: (0, i, 0))

    def advance(self, out):
        self.cur, self.stage = out, self.stage + 1

    def finish(self):
        while not self.done():
            self.advance(_run_alone(f"{self.name}_s{self.stage}", self.rider(256)))
        return self.cur.reshape(self.R, self.C)


RIDER_TILE_BYTES = 3 * 512 * 1024


def _pick(jobs, nsteps):
    for job in sorted(jobs or (), key=lambda j: -j.R * j.C):
        if job.done():
            continue
        streams = 1 if job.stage == 0 else 2
        riders = [job.rider(rows) for rows in (512, 256, 128, 64)
                  if rows * job.C * 4 * streams <= RIDER_TILE_BYTES or rows == 64]
        for rider in riders:
            if 2 * rider.n <= nsteps:
                return job, rider
        if riders[-1].n <= nsteps:
            return job, riders[-1]
    return None, None


def _sum8(name, g):
    _, R, C = g.shape
    tr = _tile(R, 512)

    def body(g_ref, o_ref):
        acc = g_ref[0]
        for d in range(1, NDEV):
            acc = acc + g_ref[d]
        o_ref[...] = acc

    return pl.pallas_call(
        body, name=name, grid=(R // tr,), in_specs=[pl.BlockSpec((NDEV, tr, C), lambda i: (0, i, 0))],
        out_specs=pl.BlockSpec((tr, C), lambda i: (i, 0)), out_shape=jax.ShapeDtypeStruct((R, C), F32),
        compiler_params=_cp(("parallel",)),
    )(g)


def _adamw_math(w, g, m, v):
    m = B1 * m + (1.0 - B1) * g
    v = B2 * v + (1.0 - B2) * jnp.square(g)
    m_hat = m / (1.0 - B1 ** STEP)
    v_hat = v / (1.0 - B2 ** STEP)
    delta = -LR * (m_hat / (jnp.sqrt(v_hat) + EPS) + WD * w)
    return delta, m, v


def _adamw(name, w, m, v, gp, *, tr, cw, gw, goff=0):
    R, C = w.shape
    nc = C // cw
    nr = R // tr

    def body(w_ref, m_ref, v_ref, g_ref, g_out, d_out, m_out, v_out):
        g = g_ref[...][:, :cw]
        d, mn, vn = _adamw_math(w_ref[...], g, m_ref[...], v_ref[...])
        g_out[...] = g
        d_out[...] = d
        m_out[...] = mn
        v_out[...] = vn

    wspec = pl.BlockSpec((tr, cw), lambda i, j: (i, j))
    gspec = pl.BlockSpec((tr, gw), lambda i, j: (i, goff + j))
    return pl.pallas_call(
        body, name=name, grid=(nr, nc), in_specs=[wspec, wspec, wspec, gspec], out_specs=[wspec] * 4,
        out_shape=[jax.ShapeDtypeStruct((R, C), F32)] * 4, compiler_params=_cp(("parallel", "parallel")),
    )(w, m, v, gp)


def _adamw_small(name, w, m, v, g):
    R, C = w.shape

    def body(w_ref, m_ref, v_ref, g_ref, d_out, m_out, v_out):
        d, mn, vn = _adamw_math(w_ref[...], g_ref[...], m_ref[...], v_ref[...])
        d_out[...] = d
        m_out[...] = mn
        v_out[...] = vn

    tr = _tile(R, 512)
    spec = pl.BlockSpec((tr, C), lambda i: (i, 0))
    return pl.pallas_call(
        body, name=name, grid=(R // tr,), in_specs=[spec] * 4, out_specs=[spec] * 3,
        out_shape=[jax.ShapeDtypeStruct((R, C), F32)] * 3, compiler_params=_cp(("parallel",)),
    )(w, m, v, g)


def _prep(name, parts, rows_p, cols_p):
    R, C = parts[0].shape
    n = len(parts)

    def body(*refs):
        o_ref = refs[n]
        if (R, C) != (rows_p, cols_p):
            o_ref[...] = jnp.zeros_like(o_ref)
        for i in range(n):
            o_ref[0:R, i * cols_p:i * cols_p + C] = refs[i][...].astype(BF16)

    return pl.pallas_call(
        body, name=name, out_shape=jax.ShapeDtypeStruct((rows_p, n * cols_p), BF16), compiler_params=_cp(),
    )(*parts)


def _attn_masks():
    lane = lax.broadcasted_iota(jnp.int32, (1, LANE), 1)
    return [(lane < HEAD_DIM), (lane >= HEAD_DIM)]


QH = 64
KW = QB + QH


def _band_valid(base):
    qi = lax.broadcasted_iota(jnp.int32, (QH, KW), 0)
    ki = lax.broadcasted_iota(jnp.int32, (QH, KW), 1)
    dist = qi + QB - ki
    return (dist >= 0) & (dist <= QB) & (base + ki - QB >= 0)


ATTN_T = max(DILATIONS) * QB


def _attn_groups(T):
    out = []
    for d in DILATIONS:
        for r in range(d):
            for l0 in range(0, T // d, QH):
                qrows = pl.ds(r + d * l0, QH, stride=d) if d > 1 else pl.ds(l0, QH)
                k0 = T + r + d * (l0 - QB)
                krows = pl.ds(k0, KW, stride=d) if d > 1 else pl.ds(k0, KW)
                out.append((d, qrows, krows, l0))
    return out


def _attn_specs(T, width_off):
    cur = pl.BlockSpec((T, LANE), lambda hp, b: (b, width_off + hp))
    prev = pl.BlockSpec((T, LANE), lambda hp, b: (jnp.maximum(b - 1, 0), width_off + hp))
    return cur, prev


def _attn_fwd(z):
    S, ZW = z.shape
    T = min(ATTN_T, S)
    scale = HEAD_DIM ** -0.5
    groups = _attn_groups(T)

    def body(q_ref, kc_ref, kp_ref, vc_ref, vp_ref, y_ref, m_ref, l_ref, kcat, vcat):
        b = pl.program_id(1)
        kcat[0:T, :] = kp_ref[...]
        kcat[T:, :] = kc_ref[...]
        vcat[0:T, :] = vp_ref[...]
        vcat[T:, :] = vc_ref[...]
        masks = _attn_masks()
        for d, qrows, krows, l0 in groups:
            q = q_ref[qrows, :]
            kk = kcat[krows, :].astype(BF16)
            vv = vcat[krows, :].astype(BF16)
            valid = _band_valid(b * (T // d) + l0)
            o_new = m_new = l_new = None
            for hm in masks:
                qh = jnp.where(hm, q, 0.0).astype(BF16)
                s = lax.dot_general(qh, kk, (((1,), (1,)), ((), ())), preferred_element_type=F32) * scale
                s = jnp.where(valid, s, MASK_VALUE)
                m = jnp.max(s, axis=-1, keepdims=True)
                p = jnp.exp(s - m)
                l = jnp.sum(p, axis=-1, keepdims=True)
                o = jnp.dot(p.astype(BF16), vv, preferred_element_type=F32)
                if o_new is None:
                    o_new, m_new, l_new = o, jnp.broadcast_to(m, (QH, LANE)), jnp.broadcast_to(l, (QH, LANE))
                else:
                    o_new = jnp.where(hm, o, o_new)
                    m_new = jnp.where(hm, m, m_new)
                    l_new = jnp.where(hm, l, l_new)
            if d == DILATIONS[0]:
                y_ref[qrows, :] = o_new
                m_ref[qrows, :] = m_new
                l_ref[qrows, :] = l_new
            else:
                m_old = m_ref[qrows, :]
                m_all = jnp.maximum(m_old, m_new)
                w_old, w_new = jnp.exp(m_old - m_all), jnp.exp(m_new - m_all)
                y_ref[qrows, :] = w_old * y_ref[qrows, :] + w_new * o_new
                l_ref[qrows, :] = w_old * l_ref[qrows, :] + w_new * l_new
                m_ref[qrows, :] = m_all
        y_ref[...] = y_ref[...] / l_ref[...]

    qc, _ = _attn_specs(T, 0)
    kc, kp = _attn_specs(T, ATTN_W // LANE)
    vc, vp = _attn_specs(T, 2 * ATTN_W // LANE)
    shp = jax.ShapeDtypeStruct((S, ATTN_W), F32)
    return pl.pallas_call(
        body, name="attn_fwd", grid=(ATTN_W // LANE, S // T),
        in_specs=[qc, kc, kp, vc, vp], out_specs=[qc, qc, qc], out_shape=[shp, shp, shp],
        scratch_shapes=[pltpu.VMEM((2 * T, LANE), F32), pltpu.VMEM((2 * T, LANE), F32)],
        compiler_params=_cp(("parallel", "parallel")),
    )(z, z, z, z, z)


def _attn_bwd(z, dya, ya, mg, den):
    S, ZW = z.shape
    T = min(ATTN_T, S)
    scale = HEAD_DIM ** -0.5
    groups = _attn_groups(T)

    def body(q_ref, kc_ref, kp_ref, vc_ref, vp_ref, dy_ref, y_ref, m_ref, n_ref, dq_ref, dk_ref, dv_ref,
             kcat, vcat, dkcat, dvcat):
        b = pl.program_id(1)

        @pl.when(b == 0)
        def _():
            dk_ref[...] = jnp.zeros_like(dk_ref)
            dv_ref[...] = jnp.zeros_like(dv_ref)

        kcat[0:T, :] = kp_ref[...]
        kcat[T:, :] = kc_ref[...]
        vcat[0:T, :] = vp_ref[...]
        vcat[T:, :] = vc_ref[...]
        dkcat[...] = jnp.zeros_like(dkcat)
        dvcat[...] = jnp.zeros_like(dvcat)
        dq_ref[...] = jnp.zeros_like(dq_ref)
        masks = _attn_masks()
        for d, rows, krows, l0 in groups:
            q, dy, y = q_ref[rows, :], dy_ref[rows, :], y_ref[rows, :]
            mrow, nrow = m_ref[rows, :], n_ref[rows, :]
            kk = kcat[krows, :].astype(BF16)
            vv = vcat[krows, :].astype(BF16)
            valid = _band_valid(b * (T // d) + l0)
            dq_acc = jnp.zeros((QH, LANE), F32)
            dk_acc = jnp.zeros((KW, LANE), F32)
            dv_acc = jnp.zeros((KW, LANE), F32)
            for hm in masks:
                qh = jnp.where(hm, q, 0.0).astype(BF16)
                dyh = jnp.where(hm, dy, 0.0)
                dyb = dyh.astype(BF16)
                dsum = jnp.sum(dyh * y, axis=-1, keepdims=True)
                mh = jnp.max(jnp.where(hm, mrow, MASK_VALUE), axis=-1, keepdims=True)
                nh = jnp.max(jnp.where(hm, nrow, 0.0), axis=-1, keepdims=True)
                s = lax.dot_general(qh, kk, (((1,), (1,)), ((), ())), preferred_element_type=F32) * scale
                p = jnp.where(valid, jnp.exp(s - mh), 0.0) / nh
                pb = p.astype(BF16)
                dv_h = lax.dot_general(pb, dyb, (((0,), (0,)), ((), ())), preferred_element_type=F32)
                dp = lax.dot_general(dyb, vv, (((1,), (1,)), ((), ())), preferred_element_type=F32)
                ds = (p * (dp - dsum) * scale).astype(BF16)
                dq_h = jnp.dot(ds, kk, preferred_element_type=F32)
                dk_h = lax.dot_general(ds, qh, (((0,), (0,)), ((), ())), preferred_element_type=F32)
                dq_acc += jnp.where(hm, dq_h, 0.0)
                dk_acc += dk_h
                dv_acc += dv_h
            dq_ref[rows, :] += dq_acc
            dkcat[krows, :] += dk_acc
            dvcat[krows, :] += dv_acc

        base = pl.multiple_of(b * T, T)
        dk_ref[pl.ds(base, T), :] += dkcat[T:, :]
        dv_ref[pl.ds(base, T), :] += dvcat[T:, :]

        @pl.when(b > 0)
        def _():
            prev = pl.multiple_of(b * T - T, T)
            dk_ref[pl.ds(prev, T), :] += dkcat[0:T, :]
            dv_ref[pl.ds(prev, T), :] += dvcat[0:T, :]

    qc, _ = _attn_specs(T, 0)
    kc, kp = _attn_specs(T, ATTN_W // LANE)
    vc, vp = _attn_specs(T, 2 * ATTN_W // LANE)
    whole = pl.BlockSpec((S, LANE), lambda hp, b: (0, hp))
    shp = jax.ShapeDtypeStruct((S, ATTN_W), F32)
    return pl.pallas_call(
        body, name="attn_bwd", grid=(ATTN_W // LANE, S // T),
        in_specs=[qc, kc, kp, vc, vp, qc, qc, qc, qc], out_specs=[qc, whole, whole], out_shape=[shp, shp, shp],
        scratch_shapes=[pltpu.VMEM((2 * T, LANE), F32)] * 4,
        compiler_params=_cp(("parallel", "arbitrary")),
    )(z, z, z, z, z, dya, ya, mg, den)


def _ssm_disc(lr, li, logdt, br, bi):
    dt = jnp.exp(logdt)
    mag = jnp.exp(lr * dt)
    ar = mag * jnp.cos(li * dt)
    ai = mag * jnp.sin(li * dt)
    nr, ni = ar - 1.0, ai
    den = lr * lr + li * li
    cr = (nr * lr + ni * li) / den
    ci = (ni * lr - nr * li) / den
    return ar, ai, cr * br - ci * bi, cr * bi + ci * br


def _ssm_prep(lr, li, logdt, br, bi):
    n, c = br.shape
    outs, _ = _rowwise("ssm_prep", lambda *a: (list(_ssm_disc(*a)), []), n, _tile(n, 512),
                       [_full(a) for a in (lr, li, logdt, br, bi)], [],
                       [(1, 1, _c0, F32), (1, 1, _c0, F32), (c, c, _c0, F32), (c, c, _c0, F32)])
    return outs


def _ssm_prep_bwd(lr, li, logdt, br, bi, dar, dai, dbbr, dbbi):
    n, c = br.shape

    def f(lrb, lib, dtb, brb, bib, *cts):
        _, vjp = jax.vjp(_ssm_disc, lrb, lib, dtb, brb, bib)
        return list(vjp(cts)), []

    outs, _ = _rowwise("ssm_prep_bwd", f, n, _tile(n, 512),
                       [_full(a) for a in (lr, li, logdt, br, bi, dar, dai, dbbr, dbbi)], [],
                       [(1, 1, _c0, F32)] * 3 + [(c, c, _c0, F32)] * 2)
    return outs


def _cmul(ar, ai, br, bi):
    return ar * br - ai * bi, ar * bi + ai * br


def _scan_consts(ar, ai, reverse):
    w = ar.shape[-1]
    a1 = (jnp.broadcast_to(ar, (8, w)), jnp.broadcast_to(ai, (8, w)))
    a2 = _cmul(*a1, *a1)
    a4 = _cmul(*a2, *a2)
    a8 = _cmul(*a4, *a4)
    row = lax.broadcasted_iota(jnp.int32, (8, w), 0)
    e = (8 - row) if reverse else (row + 1)
    one, zero = jnp.ones((8, w), F32), jnp.zeros((8, w), F32)
    pw = (one, zero)
    for bit, ap in ((1, a1), (2, a2), (4, a4), (8, a8)):
        sel = (e & bit) != 0
        nxt = _cmul(*pw, *ap)
        pw = (jnp.where(sel, nxt[0], pw[0]), jnp.where(sel, nxt[1], pw[1]))
    steps = []
    for sh, (pr, pi) in zip((1, 2, 4), (a1, a2, a4)):
        keep = (row < 8 - sh) if reverse else (row >= sh)
        steps.append((jnp.where(keep, pr, 0.0), jnp.where(keep, pi, 0.0)))
    return steps, pw, row


def _scan_group(xr, xi, cr, ci, consts, reverse):
    steps, pw, _ = consts
    for sh, (pr, pi) in zip((1, 2, 4), steps):
        by = 8 - sh if reverse else sh
        tr_, ti_ = _cmul(pr, pi, pltpu.roll(xr, by, 0), pltpu.roll(xi, by, 0))
        xr = xr + tr_
        xi = xi + ti_
    tr_, ti_ = _cmul(pw[0], pw[1], cr, ci)
    return xr + tr_, xi + ti_


def _ssm_fwd(z, a_r, a_i, bdr, bdi, cmr, cmi, dskip, ts, rider=None):
    S, ZW = z.shape
    NS = SSM_G * SSM_P
    PW = PACK * SSM_P
    uoff = (ZW - SSM_W) // LANE
    nsteps = S // ts

    def body(u_ref, ar_ref, ai_ref, bdr_ref, bdi_ref, cmr_ref, cmi_ref, d_ref, hr_ref, hi_ref, y_ref, car_r, car_i):
        s = pl.program_id(1)

        @pl.when(s == 0)
        def _():
            car_r[...] = jnp.zeros_like(car_r)
            car_i[...] = jnp.zeros_like(car_i)

        u = u_ref[...]
        ub = u.astype(BF16)
        nt = (((1,), (1,)), ((), ()))
        hr_ref[...] = lax.dot_general(ub, bdr_ref[...], nt, preferred_element_type=F32)
        hi_ref[...] = lax.dot_general(ub, bdi_ref[...], nt, preferred_element_type=F32)
        consts = _scan_consts(ar_ref[...], ai_ref[...], False)

        def step(j, carry):
            rows = pl.ds(pl.multiple_of(j * 8, 8), 8)
            hr, hi = _scan_group(hr_ref[rows, :], hi_ref[rows, :], carry[0], carry[1], consts, False)
            hr_ref[rows, :] = hr
            hi_ref[rows, :] = hi
            return jnp.broadcast_to(hr[7:8, :], (8, PW)), jnp.broadcast_to(hi[7:8, :], (8, PW))

        cr, ci = lax.fori_loop(0, ts // 8, step, (car_r[...], car_i[...]))
        car_r[...] = cr
        car_i[...] = ci
        y = lax.dot_general(hr_ref[...].astype(BF16), cmr_ref[...], nt, preferred_element_type=F32)
        y -= lax.dot_general(hi_ref[...].astype(BF16), cmi_ref[...], nt, preferred_element_type=F32)
        y_ref[...] = y + d_ref[...] * u

    row_a = pl.BlockSpec((1, PW), lambda i, s: (0, i))
    (hr, hi, y), extra = _hosted(
        "ssm_fwd", body, (SSM_G // PACK, nsteps),
        [pl.BlockSpec((ts, LANE), lambda i, s: (s, uoff + i)), row_a, row_a,
         pl.BlockSpec((None, PW, LANE), lambda i, s: (i, 0, 0)), pl.BlockSpec((None, PW, LANE), lambda i, s: (i, 0, 0)),
         pl.BlockSpec((None, LANE, PW), lambda i, s: (i, 0, 0)), pl.BlockSpec((None, LANE, PW), lambda i, s: (i, 0, 0)),
         pl.BlockSpec((1, LANE), lambda i, s: (0, i))],
        [pl.BlockSpec((ts, PW), lambda i, s: (s, i)), pl.BlockSpec((ts, PW), lambda i, s: (s, i)),
         pl.BlockSpec((ts, LANE), lambda i, s: (s, i))],
        [jax.ShapeDtypeStruct((S, NS), F32), jax.ShapeDtypeStruct((S, NS), F32), jax.ShapeDtypeStruct((S, SSM_W), F32)],
        [pltpu.VMEM((8, PW), F32), pltpu.VMEM((8, PW), F32)], (z, a_r, a_i, bdr, bdi, cmr, cmi, dskip),
        ("parallel", "arbitrary"), rider)
    return hr, hi, y, extra


def _ssm_bwd(z, dyp, hr, hi, a_r, a_i, bdr, bdi, cmr, cmi, dskip, ts, jobs=None):
    S, ZW = z.shape
    NS = SSM_G * SSM_P
    PW = PACK * SSM_P
    uoff = (ZW - SSM_W) // LANE
    nsteps = S // ts
    npk = SSM_G // PACK

    def body(u_ref, dy_ref, hr_ref, hi_ref, hpr_ref, hpi_ref, ar_ref, ai_ref, bdr_ref, bdi_ref, cmr_ref, cmi_ref,
             d_ref, du_ref, dbdr_ref, dbdi_ref, dcmr_ref, dcmi_ref, dar_ref, dai_ref, dd_ref,
             lr_s, li_s, hcr, hci, car_r, car_i):
        s = pl.program_id(1)
        first_tile = s == nsteps - 1

        @pl.when(s == 0)
        def _():
            car_r[...] = jnp.zeros_like(car_r)
            car_i[...] = jnp.zeros_like(car_i)
            for r in (dbdr_ref, dbdi_ref, dcmr_ref, dcmi_ref, dar_ref, dai_ref, dd_ref):
                r[...] = jnp.zeros_like(r)

        u, dy = u_ref[...], dy_ref[...]
        ub, dyb = u.astype(BF16), dy.astype(BF16)
        lr_s[...] = jnp.dot(dyb, cmr_ref[...], preferred_element_type=F32)
        li_s[...] = -jnp.dot(dyb, cmi_ref[...], preferred_element_type=F32)
        keep_prev = jnp.where(first_tile, 0.0, 1.0)
        hcr[0:8, :] = hpr_ref[...] * keep_prev
        hci[0:8, :] = hpi_ref[...] * keep_prev
        hcr[8:, :] = hr_ref[...]
        hci[8:, :] = hi_ref[...]
        consts = _scan_consts(ar_ref[...], -ai_ref[...], True)
        row = consts[2]
        ngrp = ts // 8

        def step(jj, carry):
            cr, ci, accr, acci = carry
            j = ngrp - 1 - jj
            rows = pl.ds(pl.multiple_of(j * 8, 8), 8)
            nxt = pl.ds(pl.multiple_of(j * 8 + 8, 8), 8)
            lr, li = _scan_group(lr_s[rows, :], li_s[rows, :], cr, ci, consts, True)
            lr_s[rows, :] = lr
            li_s[rows, :] = li
            pr, pi = hcr[rows, :], hci[rows, :]
            hsr = jnp.where(row == 0, jnp.broadcast_to(pr[7:8, :], (8, PW)), pltpu.roll(hcr[nxt, :], 1, 0))
            hsi = jnp.where(row == 0, jnp.broadcast_to(pi[7:8, :], (8, PW)), pltpu.roll(hci[nxt, :], 1, 0))
            accr = accr + lr * hsr + li * hsi
            acci = acci + li * hsr - lr * hsi
            return jnp.broadcast_to(lr[0:1, :], (8, PW)), jnp.broadcast_to(li[0:1, :], (8, PW)), accr, acci

        zero = jnp.zeros((8, PW), F32)
        cr, ci, accr, acci = lax.fori_loop(0, ngrp, step, (car_r[...], car_i[...], zero, zero))
        car_r[...] = cr
        car_i[...] = ci
        dar_ref[...] += jnp.sum(accr, axis=0, keepdims=True)
        dai_ref[...] += jnp.sum(acci, axis=0, keepdims=True)
        lrb, lib = lr_s[...].astype(BF16), li_s[...].astype(BF16)
        du = jnp.dot(lrb, bdr_ref[...], preferred_element_type=F32)
        du += jnp.dot(lib, bdi_ref[...], preferred_element_type=F32)
        du_ref[...] = du + dy * d_ref[...]
        tn = (((0,), (0,)), ((), ()))
        dbdr_ref[...] += lax.dot_general(lrb, ub, tn, preferred_element_type=F32)
        dbdi_ref[...] += lax.dot_general(lib, ub, tn, preferred_element_type=F32)
        dcmr_ref[...] += lax.dot_general(dyb, hr_ref[...].astype(BF16), tn, preferred_element_type=F32)
        dcmi_ref[...] -= lax.dot_general(dyb, hi_ref[...].astype(BF16), tn, preferred_element_type=F32)
        dd_ref[...] += jnp.sum(dy * u, axis=0, keepdims=True)

    rev = lambda s: nsteps - 1 - s
    row_a = pl.BlockSpec((1, PW), lambda i, s: (0, i))
    tile = pl.BlockSpec((ts, PW), lambda i, s: (rev(s), i))
    prev8 = pl.BlockSpec((8, PW), lambda i, s: (jnp.maximum(rev(s) * (ts // 8) - 1, 0), i))
    cols = pl.BlockSpec((ts, LANE), lambda i, s: (rev(s), i))
    bd = pl.BlockSpec((None, PW, LANE), lambda i, s: (i, 0, 0))
    cm = pl.BlockSpec((None, LANE, PW), lambda i, s: (i, 0, 0))
    outs, _ = _hosted(
        "ssm_bwd", body, (npk, nsteps),
        [pl.BlockSpec((ts, LANE), lambda i, s: (rev(s), uoff + i)), cols, tile, tile, prev8, prev8,
         row_a, row_a, bd, bd, cm, cm, pl.BlockSpec((1, LANE), lambda i, s: (0, i))],
        [cols, bd, bd, cm, cm, row_a, row_a, pl.BlockSpec((1, LANE), lambda i, s: (0, i))],
        [jax.ShapeDtypeStruct((S, SSM_W), F32),
         jax.ShapeDtypeStruct((npk, PW, LANE), F32), jax.ShapeDtypeStruct((npk, PW, LANE), F32),
         jax.ShapeDtypeStruct((npk, LANE, PW), F32), jax.ShapeDtypeStruct((npk, LANE, PW), F32),
         jax.ShapeDtypeStruct((1, NS), F32), jax.ShapeDtypeStruct((1, NS), F32), jax.ShapeDtypeStruct((1, SSM_W), F32)],
        [pltpu.VMEM((ts, PW), F32), pltpu.VMEM((ts, PW), F32), pltpu.VMEM((ts + 8, PW), F32),
         pltpu.VMEM((ts + 8, PW), F32), pltpu.VMEM((8, PW), F32), pltpu.VMEM((8, PW), F32)],
        (z, dyp, hr, hi, hr, hi, a_r, a_i, bdr, bdi, cmr, cmi, dskip), ("parallel", "arbitrary"), None, jobs)
    return outs


def _block_diag(m4):
    npk, g, a, b = m4.shape
    eye = jnp.eye(g, dtype=m4.dtype)
    return (m4[:, :, :, None, :] * eye[None, :, None, :, None]).reshape(npk, g * a, g * b)


def _block_diag_take(m, a, b):
    npk = m.shape[0]
    m5 = m.reshape(npk, PACK, a, PACK, b)
    return jnp.stack([m5[:, g, :, g, :] for g in range(PACK)], axis=1)


def _mix_out(ya, ypre, gl, ga, gb, bglu):
    yg = jax.nn.gelu(ypre)
    yb = yg * jax.nn.sigmoid(gl + bglu)
    return jnp.concatenate([_rms(ya, ga), _rms(yb, gb)], axis=-1)


def _tail_loss(h3, gl, pe, gf, tgt):
    h4 = h3 + jax.nn.sigmoid(gl) * pe
    err = jnp.square(_rms(h4, gf) - tgt)
    return 0.5 * jnp.mean(err, axis=-1, keepdims=True)


def kernel(x, p, ffn1_norm, ffn1_w_gate, ffn1_w_up, ffn1_w_down, mix_norm, w_in, attn_out_norm, ssm_lambda_re, ssm_lambda_im, ssm_log_dt, ssm_b_re, ssm_b_im, ssm_c_re, ssm_c_im, ssm_d, ssm_w_glu, ssm_b_glu, ssm_out_norm, w_out, ffn2_norm, ffn2_w_gate, ffn2_w_up, ffn2_w_down, ple_norm, ple_w_gate, ple_w_proj, final_norm, loss_target, m_ffn1_norm, m_ffn1_w_gate, m_ffn1_w_up, m_ffn1_w_down, m_mix_norm, m_w_in, m_attn_out_norm, m_ssm_lambda_re, m_ssm_lambda_im, m_ssm_log_dt, m_ssm_b_re, m_ssm_b_im, m_ssm_c_re, m_ssm_c_im, m_ssm_d, m_ssm_w_glu, m_ssm_b_glu, m_ssm_out_norm, m_w_out, m_ffn2_norm, m_ffn2_w_gate, m_ffn2_w_up, m_ffn2_w_down, m_ple_norm, m_ple_w_gate, m_ple_w_proj, m_final_norm, v_ffn1_norm, v_ffn1_w_gate, v_ffn1_w_up, v_ffn1_w_down, v_mix_norm, v_w_in, v_attn_out_norm, v_ssm_lambda_re, v_ssm_lambda_im, v_ssm_log_dt, v_ssm_b_re, v_ssm_b_im, v_ssm_c_re, v_ssm_c_im, v_ssm_d, v_ssm_w_glu, v_ssm_b_glu, v_ssm_out_norm, v_w_out, v_ffn2_norm, v_ffn2_w_gate, v_ffn2_w_up, v_ffn2_w_down, v_ple_norm, v_ple_w_gate, v_ple_w_proj, v_final_norm):
    A = dict(locals())
    xs = x[0]
    ps = p[0, 0]
    tgt = loss_target[0]
    S, D = xs.shape
    FSH = ffn1_w_gate.shape[-1]
    FSP = -(-FSH // LANE) * LANE
    TR = _tile(S, 256)
    ZW = 3 * ATTN_W + SSM_W

    wgu1 = _prep("prep_gu1", [ffn1_w_gate[0], ffn1_w_up[0]], D, FSP)
    wgu2 = _prep("prep_gu2", [ffn2_w_gate[0], ffn2_w_up[0]], D, FSP)
    wd1 = _prep("prep_d1", [ffn1_w_down[0]], FSP, D)
    wd2 = _prep("prep_d2", [ffn2_w_down[0]], FSP, D)
    win = _prep("prep_in", [w_in[0]], D, w_in.shape[-1])
    wglu = _prep("prep_glu", [ssm_w_glu[0]], ssm_w_glu.shape[1], SSM_W)
    wout = _prep("prep_out", [w_out[0]], w_out.shape[1], D)
    wpg = _prep("prep_pg", [ple_w_gate[0]], ple_w_gate.shape[1], D)
    wpp = _prep("prep_pp", [ple_w_proj[0]], ple_w_proj.shape[1], ple_w_proj.shape[2])
    (Wgu1,) = _all_gather("ag_weights", [wgu1])
    rowstack = lambda w: w.reshape(1, w.shape[0] * w.shape[1], w.shape[2])

    def ffn_norm(tag, h, gain):
        return _rowwise(f"{tag}_norm", lambda a, g: ([_rms(a, g)], []), S, TR, [_full(h)], [gain], [(D, D, _c0, BF16)])[0][0]

    xn1 = ffn_norm("ffn1", xs, ffn1_norm)
    gu1, hid1, (Wd1, Win) = _ffn_up("ffn1_up", xn1, Wgu1, rider=_GatherRider([wd1, win]))
    Wd1 = rowstack(Wd1)
    h1, un, (Wd2,) = _mm_nn("ffn1_down", hid1, Wd1, tn=D, tk=2 * FSP, res=xs, scale=0.5, gain=mix_norm,
                            rider=_GatherRider([wd2]))
    Wd2 = rowstack(Wd2)
    z, (Wglu, Wout, Wpg, Wpp) = _mm_nn("mix_in", un, Win, tn=512, tk=D,
                                       rider=_GatherRider([wglu, wout, wpg, wpp]))
    Wglu, Wout, Wpg = rowstack(Wglu), rowstack(Wout), rowstack(Wpg)
    ya, mg, den = _attn_fwd(z)

    col = lambda a: a.reshape(-1, 1)
    lr_c, li_c = col(ssm_lambda_re), col(ssm_lambda_im)
    dt_c = col(jnp.broadcast_to(ssm_log_dt.reshape(SSM_G, 1), (SSM_G, SSM_P)))
    b_re2, b_im2 = ssm_b_re.reshape(-1, SSM_C), ssm_b_im.reshape(-1, SSM_C)
    ar_c, ai_c, bbr, bbi = _ssm_prep(lr_c, li_c, dt_c, b_re2, b_im2)
    a_r, a_i = ar_c.reshape(1, -1), ai_c.reshape(1, -1)
    npk = SSM_G // PACK
    bdr = _block_diag(bbr.reshape(npk, PACK, SSM_P, SSM_C)).astype(BF16)
    bdi = _block_diag(bbi.reshape(npk, PACK, SSM_P, SSM_C)).astype(BF16)
    cmr = _block_diag(ssm_c_re.reshape(npk, PACK, SSM_C, SSM_P)).astype(BF16)
    cmi = _block_diag(ssm_c_im.reshape(npk, PACK, SSM_C, SSM_P)).astype(BF16)
    TS = _tile(S, 512)
    hr, hi, ypre, (Wgu2,) = _ssm_fwd(z, a_r, a_i, bdr, bdi, cmr, cmi, ssm_d, TS, rider=_GatherRider([wgu2]))
    (yg,), _ = _rowwise("ssm_gelu", lambda a: ([jax.nn.gelu(a)], []), S, TR, [_full(ypre)], [], [(SSM_W, SSM_W, _c0, BF16)])
    gl = _mm_nn("ssm_glu", yg, Wglu, tn=SSM_W, tk=SSM_W)
    (ycat,), _ = _rowwise("mix_out", lambda *a: ([_mix_out(*a)], []), S, TR, [_full(ya), _full(ypre), _full(gl)],
                          [attn_out_norm, ssm_out_norm, ssm_b_glu], [(MIX_W, MIX_W, _c0, BF16)])
    h2, xn2 = _mm_nn("mix_proj", ycat, Wout, tn=D, tk=D, res=h1, scale=1.0, gain=ffn2_norm)
    gu2, hid2, _ = _ffn_up("ffn2_up", xn2, Wgu2)
    h3, hn = _mm_nn("ffn2_down", hid2, Wd2, tn=D, tk=2 * FSP, res=h2, scale=0.5, gain=ple_norm)
    pgl = _mm_nn("ple_gate", hn, Wpg, tn=D // 2, tk=D)
    pb = ps
    pe = _mm_nn("ple_proj", pb, Wpp, tn=Wpp.shape[2], tk=Wpp.shape[1])

    def tail(h3b, glb, peb, tb, gf):
        rows, vjp = jax.vjp(lambda a, b, c, g: _tail_loss(a, b, c, g, tb), h3b, glb, peb, gf)
        dh, dgl, dpe, dgf = vjp(jnp.ones_like(rows))
        return [dh, dgl, dpe], [jnp.broadcast_to(jnp.sum(rows, axis=0, keepdims=True), (1, LANE)), dgf]

    (dh3_dir, dpgl, dpe), (loss_row, g_final) = _rowwise(
        "tail", tail, S, TR, [_full(h3), _full(pgl), _full(pe), _full(tgt)], [final_norm.reshape(1, D)],
        [(D, D, _c0, F32), (D, D, _c0, BF16), (D, D, _c0, BF16)], [(LANE, LANE, _c0), (D, D, _c0)])
    loss = lax.psum(loss_row[0, 0], AXES)

    def norm_bwd(tag, h, gain, dn, dres):
        def f(hb, dnb, drb, g):
            _, vjp = jax.vjp(_rms, hb, g)
            dh, dg = vjp(dnb)
            dh = dh + drb
            return [dh, dh], [dg]
        (dh, dhb), (dg,) = _rowwise(f"{tag}_norm_bwd", f, S, TR, [_full(h), _full(dn), _full(dres)], [gain],
                                    [(D, D, _c0, F32), (D, D, _c0, BF16)], [(D, D, _c0)])
        return dh, dhb, dg

    restack = lambda g: g.reshape((NDEV, g.shape[1] // NDEV) + g.shape[2:])
    jobs, scat = [], {}

    def scatter(key, g):
        scat[key] = _Scatter("rs_" + key, g)
        jobs.append(scat[key])

    late = []
    dhn = _mm_nt("ple_gate_dx", dpgl, Wpg, tn=D, tk=D)
    late.append(lambda: scatter("pg", restack(_mm_tn("ple_gate_dw", hn, dpgl, 1, jobs=jobs))))
    late.append(lambda: scatter("pp", _mm_tn("ple_proj_dw", pb, dpe, NDEV, jobs=jobs)))
    dh3, dh3b, g_ple_norm = norm_bwd("ple", h3, ple_norm, dhn, dh3_dir)

    def ffn_bwd(tag, h, gain, Wgu, Wd, saved, dout, doutb):
        xn, gu, hid = saved
        dgu = _ffn_down_dx(f"{tag}_down_dx", doutb, Wd, gu, NDEV, scale=0.5, jobs=jobs)
        scatter(tag + "gu", _mm_tn(f"{tag}_up_dw", xn, dgu, NDEV, tm=2048, tn=FSP, jobs=jobs))
        scatter(tag + "d", restack(_mm_tn(f"{tag}_down_dw", hid, doutb, 1, tm=2048, tko=FSP, tn=D // 2, scale=0.5,
                                          jobs=jobs)))
        dxn = _mm_nt(f"{tag}_up_dx", dgu, Wgu, tm=512, tn=D, tk=2 * FSP, jobs=jobs)
        dh, dhb, g_norm = norm_bwd(tag, h, gain, dxn, dout)
        return dh, dhb, g_norm

    dh2, dh2b, g_ffn2_norm = ffn_bwd("ffn2", h2, ffn2_norm, Wgu2, Wd2, (xn2, gu2, hid2), dh3, dh3b)

    dycat = _mm_nt("mix_proj_dx", dh2b, Wout, tn=D, tk=D, jobs=jobs)
    late.append(lambda: scatter("out", restack(_mm_tn("mix_proj_dw", ycat, dh2b, 1, jobs=jobs))))

    def mix_out_bwd(yab, ypb, glb, dyc, ga, gb, bglu):
        _, vjp = jax.vjp(_mix_out, yab, ypb, glb, ga, gb, bglu)
        dya_, dyp_, dgl_, dga, dgb, dbg = vjp(dyc)
        return [dya_, dyp_, dgl_], [dga, dgb, dbg]
    (dya, dyp_dir, dglb), (g_attn_norm, g_ssm_norm, g_bglu) = _rowwise(
        "mix_out_bwd", mix_out_bwd, S, TR, [_full(ya), _full(ypre), _full(gl), _full(dycat)],
        [attn_out_norm, ssm_out_norm, ssm_b_glu],
        [(ATTN_W, ATTN_W, _c0, F32), (SSM_W, SSM_W, _c0, F32), (SSM_W, SSM_W, _c0, BF16)],
        [(ATTN_W, ATTN_W, _c0), (SSM_W, SSM_W, _c0), (SSM_W, SSM_W, _c0)])
    dyg = _mm_nt("ssm_glu_dx", dglb, Wglu, tn=SSM_W, tk=SSM_W, jobs=jobs)
    late.append(lambda: scatter("glu", restack(_mm_tn("ssm_glu_dw", yg, dglb, 1, jobs=jobs))))

    def gelu_bwd(ypb, dygb, ddir):
        _, vjp = jax.vjp(jax.nn.gelu, ypb)
        return [ddir + vjp(dygb)[0]], []
    (dyp,), _ = _rowwise("ssm_gelu_bwd", gelu_bwd, S, TR, [_full(ypre), _full(dyg), _full(dyp_dir)], [],
                         [(SSM_W, SSM_W, _c0, F32)])
    du, dbdr, dbdi, dcmr, dcmi, da_r, da_i, g_ssm_d = _ssm_bwd(z, dyp, hr, hi, a_r, a_i, bdr, bdi, cmr, cmi, ssm_d, TS,
                                                              jobs=jobs)
    dbbr = _block_diag_take(dbdr, SSM_P, SSM_C).reshape(-1, SSM_C)
    dbbi = _block_diag_take(dbdi, SSM_P, SSM_C).reshape(-1, SSM_C)
    g_c_re = _block_diag_take(dcmr, SSM_C, SSM_P).reshape(ssm_c_re.shape)
    g_c_im = _block_diag_take(dcmi, SSM_C, SSM_P).reshape(ssm_c_im.shape)
    dlr, dli, ddt, g_b_re, g_b_im = _ssm_prep_bwd(lr_c, li_c, dt_c, b_re2, b_im2, col(da_r), col(da_i), dbbr, dbbi)
    g_lam_re, g_lam_im = dlr.reshape(ssm_lambda_re.shape), dli.reshape(ssm_lambda_im.shape)
    g_log_dt = jnp.sum(ddt.reshape(SSM_G, SSM_P), axis=1).reshape(ssm_log_dt.shape)
    g_b_re, g_b_im = g_b_re.reshape(ssm_b_re.shape), g_b_im.reshape(ssm_b_im.shape)

    dq, dk, dv = _attn_bwd(z, dya, ya, mg, den)
    (dz,), _ = _rowwise("mix_dz", lambda *a: ([jnp.concatenate(a, axis=-1)], []), S, TR,
                        [_full(dq), _full(dk), _full(dv), _full(du)], [], [(ZW, ZW, _c0, BF16)])
    dun = _mm_nt("mix_in_dx", dz, Win, tn=D, tk=512, jobs=jobs)
    scatter("in", _mm_tn("mix_in_dw", un, dz, NDEV, tn=512, jobs=jobs))
    dh1, dh1b, g_mix_norm = norm_bwd("mix", h1, mix_norm, dun, dh2)

    dx, _dxb, g_ffn1_norm = ffn_bwd("ffn1", xs, ffn1_norm, Wgu1, Wd1, (xn1, gu1, hid1), dh1, dh1b)
    for run in (late[2], late[0], late[3], late[1]):
        run()

    mine = {key: job.finish() for key, job in scat.items()}
    out = {}

    def upd(name, idx, *, tr, cw, gw, goff=0):
        w, m, v = A[name][0], A["m_" + name][0], A["v_" + name][0]
        g, dlt, mn, vn = _adamw("adamw_" + name, w, m, v, mine[idx], tr=tr, cw=cw, gw=gw, goff=goff)
        for k, val in (("grad_", g), ("delta_", dlt), ("new_m_", mn), ("new_v_", vn)):
            out[k + name] = val[None]

    DT = _tile(D, 256)
    FT = _tile(FSH, 512)
    DC = _tile(D, 1024, LANE)
    upd("ffn1_w_gate", "ffn1gu", tr=DT, cw=FSH, gw=FSP, goff=0)
    upd("ffn1_w_up", "ffn1gu", tr=DT, cw=FSH, gw=FSP, goff=1)
    upd("ffn1_w_down", "ffn1d", tr=FT, cw=DC, gw=DC)
    upd("w_in", "in", tr=DT, cw=w_in.shape[-1], gw=w_in.shape[-1])
    upd("ssm_w_glu", "glu", tr=ssm_w_glu.shape[1], cw=SSM_W, gw=SSM_W)
    upd("w_out", "out", tr=w_out.shape[1], cw=DC, gw=DC)
    upd("ffn2_w_gate", "ffn2gu", tr=DT, cw=FSH, gw=FSP, goff=0)
    upd("ffn2_w_up", "ffn2gu", tr=DT, cw=FSH, gw=FSP, goff=1)
    upd("ffn2_w_down", "ffn2d", tr=FT, cw=DC, gw=DC)
    upd("ple_w_gate", "pg", tr=ple_w_gate.shape[1], cw=DC, gw=DC)
    upd("ple_w_proj", "pp", tr=ple_w_proj.shape[1], cw=ple_w_proj.shape[2], gw=ple_w_proj.shape[2])

    small = [("ffn1_norm", g_ffn1_norm), ("mix_norm", g_mix_norm), ("attn_out_norm", g_attn_norm),
             ("ssm_lambda_re", g_lam_re), ("ssm_lambda_im", g_lam_im), ("ssm_log_dt", g_log_dt),
             ("ssm_b_re", g_b_re), ("ssm_b_im", g_b_im), ("ssm_c_re", g_c_re), ("ssm_c_im", g_c_im),
             ("ssm_d", g_ssm_d), ("ssm_b_glu", g_bglu), ("ssm_out_norm", g_ssm_norm), ("ffn2_norm", g_ffn2_norm),
             ("ple_norm", g_ple_norm), ("final_norm", g_final)]
    chunk = 8 * LANE

    def pack(arrs):
        parts = []
        for a in arrs:
            flat = a.reshape(-1)
            padn = -(-flat.shape[0] // chunk) * chunk
            parts.append(jnp.pad(flat, (0, padn - flat.shape[0])).reshape(-1, LANE))
        return jnp.concatenate(parts, axis=0)

    g_pack = pack([g for _, g in small])
    (g_all,) = _all_gather("ag_small", [g_pack])
    g_sum = _sum8("small_sum", g_all)
    w_pack = pack([A[n] for n, _ in small])
    m_pack = pack([A["m_" + n] for n, _ in small])
    v_pack = pack([A["v_" + n] for n, _ in small])
    d_pack, mn_pack, vn_pack = _adamw_small("adamw_small", w_pack, m_pack, v_pack, g_sum)
    off = 0
    for n, _ in small:
        shape = A[n].shape
        size = math.prod(shape)
        rows = -(-size // chunk) * 8
        for k, buf in (("grad_", g_sum), ("delta_", d_pack), ("new_m_", mn_pack), ("new_v_", vn_pack)):
            out[k + n] = buf[off:off + rows].reshape(-1)[:size].reshape(shape)
        off += rows

    names = ['ffn1_norm', 'ffn1_w_gate', 'ffn1_w_up', 'ffn1_w_down', 'mix_norm', 'w_in', 'attn_out_norm',
             'ssm_lambda_re', 'ssm_lambda_im', 'ssm_log_dt', 'ssm_b_re', 'ssm_b_im', 'ssm_c_re', 'ssm_c_im', 'ssm_d',
             'ssm_w_glu', 'ssm_b_glu', 'ssm_out_norm', 'w_out', 'ffn2_norm', 'ffn2_w_gate', 'ffn2_w_up', 'ffn2_w_down',
             'ple_norm', 'ple_w_gate', 'ple_w_proj', 'final_norm']
    return (loss, dx[None], *[out[k + n] for k in ("grad_", "delta_", "new_m_", "new_v_") for n in names])
```

```python
import functools
import math

import jax
import jax.numpy as jnp
from jax import lax
from jax.experimental import pallas as pl
from jax.experimental.pallas import tpu as pltpu

F32, BF16 = jnp.float32, jnp.bfloat16
MESH = pl.DeviceIdType.MESH
NDEV = 8
AXES = ("x", "y", "c")
LANE = 128
VMEM_LIMIT = 56 * 1024 * 1024

ATTN_W = 1024
HEAD_DIM = 64
SSM_W = 1024
MIX_W = ATTN_W + SSM_W
SSM_G, SSM_P, SSM_C = 64, 64, 16
PACK = 8
DILATIONS = (1, 4, 16)
QB = 128
NORM_EPS = 1e-6
MASK_VALUE = -1e30
LR, B1, B2, EPS, WD, STEP = 0.001, 0.9, 0.999, 1e-08, 0.01, 10


def _cp(sem=None):
    return pltpu.CompilerParams(dimension_semantics=sem, vmem_limit_bytes=VMEM_LIMIT)


def _tile(n, target, mult=8):
    if n <= target:
        return n
    for t in range(target - target % mult, 0, -mult):
        if n % t == 0:
            return t
    return n


def _rms(x, g):
    return x * lax.rsqrt(jnp.mean(x * x, axis=-1, keepdims=True) + NORM_EPS) * g


def _rowwise(name, fn, S, tr, rows, fulls, outs, accs=(), ncol=1):
    nr, nf, no, na = len(rows), len(fulls), len(outs), len(accs)

    def body(*refs):
        ins = [r[...] for r in refs[:nr + nf]]
        o_refs = refs[nr + nf:nr + nf + no]
        a_refs = refs[nr + nf + no:]
        o_vals, a_vals = fn(*ins)
        for r, v in zip(o_refs, o_vals):
            r[...] = v.astype(r.dtype)
        if na:
            @pl.when(pl.program_id(1) == 0)
            def _():
                for r in a_refs:
                    r[...] = jnp.zeros_like(r)
            for r, v in zip(a_refs, a_vals):
                r[...] += v

    in_specs = [pl.BlockSpec((tr, w), functools.partial(lambda j, i, cm: (i, cm(j)), cm=cm)) for _, w, cm in rows]
    in_specs += [pl.BlockSpec(f.shape, functools.partial(lambda j, i, nd: (0,) * nd, nd=f.ndim)) for f in fulls]
    out_specs = [pl.BlockSpec((tr, w), functools.partial(lambda j, i, cm: (i, cm(j)), cm=cm)) for _, w, cm, _ in outs]
    out_specs += [pl.BlockSpec((1, w), functools.partial(lambda j, i, cm: (0, cm(j)), cm=cm)) for _, w, cm in accs]
    out_shape = [jax.ShapeDtypeStruct((S, c), dt) for c, _, _, dt in outs]
    out_shape += [jax.ShapeDtypeStruct((1, c), F32) for c, _, _ in accs]
    res = pl.pallas_call(
        body, name=name, grid=(ncol, S // tr), in_specs=in_specs, out_specs=out_specs, out_shape=out_shape,
        compiler_params=_cp(("parallel", "arbitrary" if na else "parallel")),
    )(*[a for a, _, _ in rows], *fulls)
    return res[:no], res[no:]


def _c0(j):
    return 0


def _full(a):
    return (a, a.shape[1], _c0)


def _hosted(name, body, grid, in_specs, out_specs, out_shape, scratch, args, sem, rider=None, jobs=None):
    nsteps = math.prod(grid)

    def step_of(*g):
        t = 0
        for gi, n in zip(g, grid):
            t = t * n + gi
        return t

    job = None
    if rider is None:
        job, rider = _pick(jobs, nsteps)
    if rider is None:
        outs = pl.pallas_call(body, name=name, grid=grid, in_specs=in_specs, out_specs=out_specs, out_shape=out_shape,
                              scratch_shapes=scratch, compiler_params=_cp(sem))(*args)
        return outs, None
    rider.bind(step_of, nsteps)
    n_in, n_out, n_scr = len(in_specs), len(out_specs), len(scratch)

    def full(*refs):
        a, b = n_in, n_in + rider.n_in
        c, d = b + n_out, b + n_out + rider.n_out
        rider.run(refs[a:b], refs[c:d], refs[d + n_scr:], step_of(*[pl.program_id(i) for i in range(len(grid))]), nsteps)
        body(*(refs[:a] + refs[b:c] + refs[d:d + n_scr]))

    outs = pl.pallas_call(
        full, name=name, grid=grid, in_specs=in_specs + rider.in_specs, out_specs=out_specs + rider.out_specs,
        out_shape=out_shape + rider.out_shape, scratch_shapes=scratch + rider.scratch,
        compiler_params=_cp(("arbitrary",) * len(grid)))(*args, *rider.operands)
    extra = rider.take(outs[n_out:])
    if job is not None:
        job.advance(extra)
        extra = None
    return outs[:n_out], extra


def _mm_nn(name, a, w, *, out_dtype=F32, tm=512, tn=768, tk=2048, res=None, scale=1.0, gain=None, rider=None,
           jobs=None):
    M, K = a.shape
    J, K2, Np = w.shape
    assert K == K2
    tm, tn, tk = _tile(M, tm), _tile(Np, tn, LANE), _tile(K, tk, LANE)
    npj = Np // tn
    nk = K // tk
    grid = (M // tm, J * npj, nk)
    assert gain is None or (J * npj == 1 and res is not None)

    def body(*refs):
        refs = list(refs)
        a_ref, w_ref = refs[:2]
        r_ref = refs[2] if res is not None else None
        g_ref = refs[3] if gain is not None else None
        acc = refs[-1]
        o_ref = refs[-3] if gain is not None else refs[-2]
        k = pl.program_id(2)
        part = jnp.dot(a_ref[...].astype(BF16), w_ref[...], preferred_element_type=F32)

        def finish(v):
            if res is not None:
                v = r_ref[...] + scale * v
            o_ref[...] = v.astype(o_ref.dtype)
            if gain is not None:
                refs[-2][...] = _rms(v, g_ref[...]).astype(BF16)

        if nk == 1:
            finish(part)
            return

        @pl.when(k == 0)
        def _():
            acc[...] = part

        @pl.when(k > 0)
        def _():
            acc[...] += part

        @pl.when(k == nk - 1)
        def _():
            finish(acc[...])

    in_specs = [pl.BlockSpec((tm, tk), lambda i, n, k: (i, k)),
                pl.BlockSpec((None, tk, tn), lambda i, n, k: (n // npj, k, n % npj))]
    args = [a, w]
    if res is not None:
        in_specs.append(pl.BlockSpec((tm, tn), lambda i, n, k: (i, n)))
        args.append(res)
    out_specs = [pl.BlockSpec((tm, tn), lambda i, n, k: (i, n))]
    out_shape = [jax.ShapeDtypeStruct((M, J * Np), out_dtype)]
    if gain is not None:
        in_specs.append(pl.BlockSpec((1, tn), lambda i, n, k: (0, 0)))
        args.append(gain)
        out_specs.append(pl.BlockSpec((tm, tn), lambda i, n, k: (i, n)))
        out_shape.append(jax.ShapeDtypeStruct((M, J * Np), BF16))
    outs, extra = _hosted(name, body, grid, in_specs, out_specs, out_shape, [pltpu.VMEM((tm, tn), F32)], args,
                          ("parallel", "parallel", "arbitrary"), rider, jobs)
    outs = tuple(outs) + (() if rider is None else (extra,))
    return outs[0] if len(outs) == 1 else outs


def _mm_nt(name, dy, w, *, out_dtype=F32, tm=512, tn=2048, tk=768, scale=1.0, jobs=None):
    M, N = dy.shape
    J, K, Np = w.shape
    assert N == J * Np
    tm, tn, tk = _tile(M, tm), _tile(K, tn, LANE), _tile(Np, tk, LANE)
    npj = Np // tk
    nc = J * npj

    def body(a_ref, w_ref, o_ref, acc):
        c = pl.program_id(2)
        part = lax.dot_general(a_ref[...].astype(BF16), w_ref[...], (((1,), (1,)), ((), ())),
                               preferred_element_type=F32)
        if nc == 1:
            o_ref[...] = (scale * part).astype(o_ref.dtype)
            return

        @pl.when(c == 0)
        def _():
            acc[...] = part

        @pl.when(c > 0)
        def _():
            acc[...] += part

        @pl.when(c == nc - 1)
        def _():
            o_ref[...] = (scale * acc[...]).astype(o_ref.dtype)

    (out,), _ = _hosted(
        name, body, (M // tm, K // tn, nc),
        [pl.BlockSpec((tm, tk), lambda i, n, c: (i, c)),
         pl.BlockSpec((None, tn, tk), lambda i, n, c: (c // npj, n, c % npj))],
        [pl.BlockSpec((tm, tn), lambda i, n, c: (i, n))], [jax.ShapeDtypeStruct((M, K), out_dtype)],
        [pltpu.VMEM((tm, tn), F32)], (dy, w), ("parallel", "parallel", "arbitrary"), None, jobs)
    return out


def _mm_tn(name, x, dy, J, *, tm=1024, tko=1024, tn=768, scale=1.0, jobs=None):
    M, K = x.shape
    M2, N = dy.shape
    assert M == M2 and N % J == 0
    Np = N // J
    tm, tko, tn = _tile(M, tm, LANE), _tile(K, tko, LANE), _tile(Np, tn, LANE)
    npj = Np // tn
    nm = M // tm

    def body(x_ref, d_ref, o_ref, acc):
        m = pl.program_id(2)
        part = lax.dot_general(x_ref[...].astype(BF16), d_ref[...].astype(BF16), (((0,), (0,)), ((), ())),
                               preferred_element_type=F32)
        if nm == 1:
            o_ref[...] = scale * part
            return

        @pl.when(m == 0)
        def _():
            acc[...] = part

        @pl.when(m > 0)
        def _():
            acc[...] += part

        @pl.when(m == nm - 1)
        def _():
            o_ref[...] = scale * acc[...]

    (out,), _ = _hosted(
        name, body, (K // tko, J * npj, nm),
        [pl.BlockSpec((tm, tko), lambda k, n, m: (m, k)), pl.BlockSpec((tm, tn), lambda k, n, m: (m, n))],
        [pl.BlockSpec((None, tko, tn), lambda k, n, m: (n // npj, k, n % npj))],
        [jax.ShapeDtypeStruct((J, K, Np), F32)], [pltpu.VMEM((tko, tn), F32)], (x, dy),
        ("parallel", "parallel", "arbitrary"), None, jobs)
    return out


def _swiglu_act(g, u):
    return jax.nn.silu(g) * u


def _ffn_up(name, xn, wgu, *, tm=1024, rider=None):
    M, K = xn.shape
    J, _, F2 = wgu.shape
    F = F2 // 2
    tm = _tile(M, tm)

    def body(a_ref, w_ref, gu_ref, h_ref):
        r = jnp.dot(a_ref[...], w_ref[...], preferred_element_type=F32)
        gu_ref[...] = r.astype(gu_ref.dtype)
        h_ref[...] = _swiglu_act(r[:, :F], r[:, F:]).astype(h_ref.dtype)

    (gu, hid), extra = _hosted(
        name, body, (M // tm, J),
        [pl.BlockSpec((tm, K), lambda i, j: (i, 0)), pl.BlockSpec((None, K, F2), lambda i, j: (j, 0, 0))],
        [pl.BlockSpec((tm, F2), lambda i, j: (i, j)), pl.BlockSpec((tm, F), lambda i, j: (i, j))],
        [jax.ShapeDtypeStruct((M, J * F2), BF16), jax.ShapeDtypeStruct((M, J * F), BF16)], [], (xn, wgu),
        ("parallel", "parallel"), rider)
    return gu, hid, extra


def _ffn_down_dx(name, dout, wd, gu, J, *, scale, tm=512, jobs=None):
    M, D = dout.shape
    F = wd.shape[1] // J
    tm = _tile(M, tm)

    def body(d_ref, w_ref, gu_ref, o_ref):
        dh = scale * lax.dot_general(d_ref[...], w_ref[...], (((1,), (1,)), ((), ())), preferred_element_type=F32)
        gu = gu_ref[...].astype(F32)
        _, vjp = jax.vjp(_swiglu_act, gu[:, :F], gu[:, F:])
        o_ref[...] = jnp.concatenate(vjp(dh), axis=-1).astype(o_ref.dtype)

    (out,), _ = _hosted(
        name, body, (M // tm, J),
        [pl.BlockSpec((tm, D), lambda i, j: (i, 0)), pl.BlockSpec((None, F, D), lambda i, j: (0, j, 0)),
         pl.BlockSpec((tm, 2 * F), lambda i, j: (i, j))],
        [pl.BlockSpec((tm, 2 * F), lambda i, j: (i, j))], [jax.ShapeDtypeStruct((M, J * 2 * F), BF16)], [],
        (dout, wd, gu), ("parallel", "parallel"), None, jobs)
    return out


def _all_gather(name, shards):
    n = len(shards)

    def body(*refs):
        start, forward, finish = _gather_phases(refs[:n], refs[n:2 * n], *refs[2 * n:])
        start()
        forward()
        finish()

    any_spec = pl.BlockSpec(memory_space=pl.ANY)
    return pl.pallas_call(
        body, name=name, in_specs=[any_spec] * n, out_specs=[any_spec] * n,
        out_shape=[jax.ShapeDtypeStruct((NDEV,) + s.shape, s.dtype) for s in shards],
        scratch_shapes=_gather_sems(n),
    )(*shards)


def _gather_sems(n):
    return [pltpu.SemaphoreType.DMA((n, 7)), pltpu.SemaphoreType.DMA((n, 7)), pltpu.SemaphoreType.DMA((n,))]


def _gather_phases(ins, outs, send_sems, recv_sems, local_sems):
    n = len(ins)
    x, y, c = lax.axis_index("x"), lax.axis_index("y"), lax.axis_index("c")
    me, sibling = (x, y, c), (x, y, 1 - c)
    chips = [(1 - x, y), (x, 1 - y), (1 - x, 1 - y)]

    def blk(i, px, py, pc):
        return outs[i].at[4 * px + 2 * py + pc]

    def copy(i, k, block, to, src=None):
        return pltpu.make_async_remote_copy(
            src_ref=blk(i, *block) if src is None else src, dst_ref=blk(i, *block),
            send_sem=send_sems.at[i, k], recv_sem=recv_sems.at[i, k], device_id=to, device_id_type=MESH)

    def local(i):
        return pltpu.make_async_copy(ins[i], blk(i, *me), local_sems.at[i])

    def firsts(i):
        return [copy(i, 0, me, sibling, src=ins[i])] + [copy(i, 1 + j, me, (*chip, c), src=ins[i])
                                                        for j, chip in enumerate(chips)]

    def start():
        for i in range(n):
            local(i).start()
        for i in range(n):
            for cp in firsts(i):
                cp.start()

    def forward():
        for i in range(n):
            for j, chip in enumerate(chips):
                copy(i, 1 + j, (*chip, c), me).wait_recv()
                copy(i, 4 + j, (*chip, c), sibling).start()

    def finish():
        for i in range(n):
            copy(i, 0, sibling, me).wait_recv()
            for j, chip in enumerate(chips):
                copy(i, 4 + j, (*chip, 1 - c), me).wait_recv()
        for i in range(n):
            for cp in firsts(i):
                cp.wait_send()
            for j, chip in enumerate(chips):
                copy(i, 4 + j, (*chip, c), sibling).wait_send()
            local(i).wait()

    return start, forward, finish


class _GatherRider:
    def __init__(self, shards):
        self.operands = list(shards)
        n = len(self.operands)
        self.n_in = self.n_out = n
        any_spec = pl.BlockSpec(memory_space=pl.ANY)
        self.in_specs = [any_spec] * n
        self.out_specs = [any_spec] * n
        self.out_shape = [jax.ShapeDtypeStruct((NDEV,) + s.shape, s.dtype) for s in self.operands]
        self.scratch = _gather_sems(n)

    def bind(self, step_of, nsteps):
        return self

    def take(self, outs):
        return list(outs)

    def run(self, ins, outs, sems, step, nsteps):
        start, forward, finish = _gather_phases(ins, outs, *sems)
        pl.when(step == 0)(start)
        pl.when(step == (4 * nsteps) // 5)(forward)
        pl.when(step == nsteps - 1)(finish)


class _SwapRider:
    def __init__(self, arr, streams, grid, tile, out_shape, out_block, out_map):
        self.arr, self.streams, self.grid, self.tile = arr, streams, grid, tile
        self.ns, self.n = len(streams), grid[0] * grid[1]
        self.operands = [arr] * (2 * self.ns)
        self.n_in, self.n_out = 2 * self.ns, 1
        self.out_shape = [jax.ShapeDtypeStruct(out_shape, F32)]
        self.out_block, self.out_map = out_block, out_map
        tr, C = tile
        slots = [pltpu.VMEM((2, tr, C), w) for _, w, _, _ in streams]
        self.scratch = slots + slots + [pltpu.SemaphoreType.DMA((self.ns, 2)), pltpu.SemaphoreType.DMA((self.ns, 2)),
                                        pltpu.SemaphoreType.REGULAR((self.ns,))]

    def bind(self, step_of, nsteps):
        assert nsteps >= self.n
        self.period = period = nsteps // self.n
        n, nr = self.n, self.grid[1]

        def ids(*g):
            k = jnp.minimum(step_of(*g) // period, n - 1)
            pos = {a: lax.axis_index(a) for a in AXES}
            return k // nr, k % nr, [v for a in AXES for v in (pos[a], 1 - pos[a])]

        block = (None,) * (self.arr.ndim - 2) + tuple(self.tile)
        self.in_specs = []
        for _, _, keep_map, send_map in self.streams:
            for m in (keep_map, send_map):
                self.in_specs.append(pl.BlockSpec(block, functools.partial(lambda *g, m: m(*ids(*g)), m=m)))
        self.out_specs = [pl.BlockSpec(self.out_block, lambda *g: self.out_map(*ids(*g)))]
        return self

    def take(self, outs):
        return outs[0]

    def run(self, ins, outs, scratch, step, nsteps):
        ns, n, period = self.ns, self.n, self.period
        keeps, sends, o_ref = ins[0::2], ins[1::2], outs[0]
        lands, stages = scratch[:ns], scratch[ns:2 * ns]
        send_sems, recv_sems, credits = scratch[2 * ns:]
        k = step // period
        slot = k % 2
        here = {a: lax.axis_index(a) for a in AXES}
        peers = [tuple(1 - here[a] if a == axis else here[a] for a in AXES) for axis, _, _, _ in self.streams]

        def rdma(s):
            return pltpu.make_async_remote_copy(
                src_ref=stages[s].at[slot], dst_ref=lands[s].at[slot], send_sem=send_sems.at[s, slot],
                recv_sem=recv_sems.at[s, slot], device_id=peers[s], device_id_type=MESH)

        @pl.when((k < n) & (step % period == 0))
        def _():
            @pl.when(k >= 2)
            def _():
                for s in range(ns):
                    pl.semaphore_wait(credits.at[s], 1)

            for s in range(ns):
                stages[s][slot] = sends[s][...].astype(stages[s].dtype)
                rdma(s).start()

        @pl.when((k < n) & (step % period == period - 1))
        def _():
            for s in range(ns):
                rdma(s).wait_recv()
                total = keeps[s][...] + lands[s][slot].astype(F32)
                if ns == 1:
                    o_ref[...] = total
                else:
                    o_ref[s] = total
            for s in range(ns):
                rdma(s).wait_send()

            @pl.when(k + 2 < n)
            def _():
                for s in range(ns):
                    pl.semaphore_signal(credits.at[s], inc=1, device_id=peers[s], device_id_type=MESH)


def _run_alone(name, rider):
    rider.bind(lambda t: t, rider.n)

    def body(*refs):
        a, b = rider.n_in, rider.n_in + rider.n_out
        rider.run(refs[:a], refs[a:b], refs[b:], pl.program_id(0), rider.n)

    outs = pl.pallas_call(
        body, name=name, grid=(rider.n,), in_specs=rider.in_specs, out_specs=rider.out_specs,
        out_shape=rider.out_shape, scratch_shapes=rider.scratch, compiler_params=_cp(("arbitrary",)),
    )(*rider.operands)
    return rider.take(outs)


class _Scatter:
    def __init__(self, name, g):
        self.name, self.cur, self.stage = name, g, 0
        _, self.R, self.C = g.shape

    def done(self):
        return self.stage == 3

    def rider(self, rows):
        R, C = self.R, self.C
        R2 = R // 2
        tr = _tile(R2, rows, 16)
        nrh = R2 // tr
        if self.stage == 0:
            return _SwapRider(
                self.cur.reshape(4, 2, R, C),
                [("c", BF16, lambda b, i, s: (b, s[4], i, 0), lambda b, i, s: (b, s[5], i, 0))],
                (4, 2 * nrh), (tr, C), (2, 4, R2, C), (None, None, tr, C), lambda b, i, s: (i // nrh, b, i % nrh, 0))
        if self.stage == 1:
            return _SwapRider(
                self.cur.reshape(2, 2, 2, R2, C),
                [("y", BF16, lambda b, i, s: (0, b, s[2], i, 0), lambda b, i, s: (0, b, s[3], i, 0)),
                 ("x", BF16, lambda b, i, s: (1, s[0], b, i, 0), lambda b, i, s: (1, s[1], b, i, 0))],
                (2, nrh), (tr, C), (2, 2, R2, C), (2, None, tr, C), lambda b, i, s: (0, b, i, 0))
        return _SwapRider(
            self.cur,
            [("x", BF16, lambda b, i, s: (0, s[0], i, 0), lambda b, i, s: (0, s[1], i, 0)),
             ("y", BF16, lambda b, i, s: (1, s[2], i, 0), lambda b, i, s: (1, s[3], i, 0))],
            (1, nrh), (tr, C), (2, R2, C), (2, tr, C), lambda b, i, s: (0, i, 0))

    def advance(self, out):
        self.cur, self.stage = out, self.stage + 1

    def finish(self):
        while not self.done():
            self.advance(_run_alone(f"{self.name}_s{self.stage}", self.rider(256)))
        return self.cur.reshape(self.R, self.C)


RIDER_TILE_BYTES = 3 * 512 * 1024


def _pick(jobs, nsteps):
    for job in sorted(jobs or (), key=lambda j: -j.R * j.C):
        if job.done():
            continue
        streams = 1 if job.stage == 0 else 2
        riders = [job.rider(rows) for rows in (512, 256, 128, 64)
                  if rows * job.C * 4 * streams <= RIDER_TILE_BYTES or rows == 64]
        for rider in riders:
            if 2 * rider.n <= nsteps:
                return job, rider
        if riders[-1].n <= nsteps:
            return job, riders[-1]
    return None, None


def _sum8(name, g):
    _, R, C = g.shape
    tr = _tile(R, 512)

    def body(g_ref, o_ref):
        acc = g_ref[0]
        for d in range(1, NDEV):
            acc = acc + g_ref[d]
        o_ref[...] = acc

    return pl.pallas_call(
        body, name=name, grid=(R // tr,), in_specs=[pl.BlockSpec((NDEV, tr, C), lambda i: (0, i, 0))],
        out_specs=pl.BlockSpec((tr, C), lambda i: (i, 0)), out_shape=jax.ShapeDtypeStruct((R, C), F32),
        compiler_params=_cp(("parallel",)),
    )(g)


def _adamw_math(w, g, m, v):
    m = B1 * m + (1.0 - B1) * g
    v = B2 * v + (1.0 - B2) * jnp.square(g)
    m_hat = m / (1.0 - B1 ** STEP)
    v_hat = v / (1.0 - B2 ** STEP)
    delta = -LR * (m_hat / (jnp.sqrt(v_hat) + EPS) + WD * w)
    return delta, m, v


def _adamw(name, w, m, v, gp, *, tr, cw, gw, goff=0, jobs=None):
    R, C = w.shape
    nc = C // cw
    nr = R // tr

    def body(w_ref, m_ref, v_ref, g_ref, g_out, d_out, m_out, v_out):
        g = g_ref[...][:, :cw]
        d, mn, vn = _adamw_math(w_ref[...], g, m_ref[...], v_ref[...])
        g_out[...] = g
        d_out[...] = d
        m_out[...] = mn
        v_out[...] = vn

    wspec = pl.BlockSpec((tr, cw), lambda i, j: (i, j))
    gspec = pl.BlockSpec((tr, gw), lambda i, j: (i, goff + j))
    outs, _ = _hosted(name, body, (nr, nc), [wspec, wspec, wspec, gspec], [wspec] * 4,
                      [jax.ShapeDtypeStruct((R, C), F32)] * 4, [], (w, m, v, gp), ("parallel", "parallel"), None, jobs)
    return outs


def _adamw_small(name, w, m, v, g):
    R, C = w.shape

    def body(w_ref, m_ref, v_ref, g_ref, d_out, m_out, v_out):
        d, mn, vn = _adamw_math(w_ref[...], g_ref[...], m_ref[...], v_ref[...])
        d_out[...] = d
        m_out[...] = mn
        v_out[...] = vn

    tr = _tile(R, 512)
    spec = pl.BlockSpec((tr, C), lambda i: (i, 0))
    return pl.pallas_call(
        body, name=name, grid=(R // tr,), in_specs=[spec] * 4, out_specs=[spec] * 3,
        out_shape=[jax.ShapeDtypeStruct((R, C), F32)] * 3, compiler_params=_cp(("parallel",)),
    )(w, m, v, g)


def _prep(name, parts, rows_p, cols_p):
    R, C = parts[0].shape
    n = len(parts)

    def body(*refs):
        o_ref = refs[n]
        if (R, C) != (rows_p, cols_p):
            o_ref[...] = jnp.zeros_like(o_ref)
        for i in range(n):
            o_ref[0:R, i * cols_p:i * cols_p + C] = refs[i][...].astype(BF16)

    return pl.pallas_call(
        body, name=name, out_shape=jax.ShapeDtypeStruct((rows_p, n * cols_p), BF16), compiler_params=_cp(),
    )(*parts)


def _attn_masks():
    lane = lax.broadcasted_iota(jnp.int32, (1, LANE), 1)
    return [(lane < HEAD_DIM), (lane >= HEAD_DIM)]


QH = QB
KW = QB + QH


def _band_valid(base):
    qi = lax.broadcasted_iota(jnp.int32, (QH, KW), 0)
    ki = lax.broadcasted_iota(jnp.int32, (QH, KW), 1)
    dist = qi + QB - ki
    return (dist >= 0) & (dist <= QB) & (base + ki - QB >= 0)


ATTN_T = max(DILATIONS) * QB


def _attn_groups(T):
    out = []
    for d in DILATIONS:
        for r in range(d):
            for l0 in range(0, T // d, QH):
                qrows = pl.ds(r + d * l0, QH, stride=d) if d > 1 else pl.ds(l0, QH)
                k0 = T + r + d * (l0 - QB)
                krows = pl.ds(k0, KW, stride=d) if d > 1 else pl.ds(k0, KW)
                out.append((d, qrows, krows, l0))
    return out


def _attn_specs(T, width_off):
    cur = pl.BlockSpec((T, LANE), lambda hp, b: (b, width_off + hp))
    prev = pl.BlockSpec((T, LANE), lambda hp, b: (jnp.maximum(b - 1, 0), width_off + hp))
    return cur, prev


def _attn_fwd(z):
    S, ZW = z.shape
    T = min(ATTN_T, S)
    scale = HEAD_DIM ** -0.5
    groups = _attn_groups(T)

    def body(q_ref, kc_ref, kp_ref, vc_ref, vp_ref, y_ref, m_ref, l_ref, kcat, vcat):
        b = pl.program_id(1)
        kcat[0:T, :] = kp_ref[...]
        kcat[T:, :] = kc_ref[...]
        vcat[0:T, :] = vp_ref[...]
        vcat[T:, :] = vc_ref[...]
        masks = _attn_masks()
        for d, qrows, krows, l0 in groups:
            q = q_ref[qrows, :]
            kk = kcat[krows, :].astype(BF16)
            vv = vcat[krows, :].astype(BF16)
            valid = _band_valid(b * (T // d) + l0)
            o_new = m_new = l_new = None
            for hm in masks:
                qh = jnp.where(hm, q, 0.0).astype(BF16)
                s = lax.dot_general(qh, kk, (((1,), (1,)), ((), ())), preferred_element_type=F32) * scale
                s = jnp.where(valid, s, MASK_VALUE)
                m = jnp.max(s, axis=-1, keepdims=True)
                p = jnp.exp(s - m)
                l = jnp.sum(p, axis=-1, keepdims=True)
                o = jnp.dot(p.astype(BF16), vv, preferred_element_type=F32)
                if o_new is None:
                    o_new, m_new, l_new = o, jnp.broadcast_to(m, (QH, LANE)), jnp.broadcast_to(l, (QH, LANE))
                else:
                    o_new = jnp.where(hm, o, o_new)
                    m_new = jnp.where(hm, m, m_new)
                    l_new = jnp.where(hm, l, l_new)
            if d == DILATIONS[0]:
                y_ref[qrows, :] = o_new
                m_ref[qrows, :] = m_new
                l_ref[qrows, :] = l_new
            else:
                m_old = m_ref[qrows, :]
                m_all = jnp.maximum(m_old, m_new)
                w_old, w_new = jnp.exp(m_old - m_all), jnp.exp(m_new - m_all)
                y_ref[qrows, :] = w_old * y_ref[qrows, :] + w_new * o_new
                l_ref[qrows, :] = w_old * l_ref[qrows, :] + w_new * l_new
                m_ref[qrows, :] = m_all
        y_ref[...] = y_ref[...] / l_ref[...]

    qc, _ = _attn_specs(T, 0)
    kc, kp = _attn_specs(T, ATTN_W // LANE)
    vc, vp = _attn_specs(T, 2 * ATTN_W // LANE)
    shp = jax.ShapeDtypeStruct((S, ATTN_W), F32)
    return pl.pallas_call(
        body, name="attn_fwd", grid=(ATTN_W // LANE, S // T),
        in_specs=[qc, kc, kp, vc, vp], out_specs=[qc, qc, qc], out_shape=[shp, shp, shp],
        scratch_shapes=[pltpu.VMEM((2 * T, LANE), F32), pltpu.VMEM((2 * T, LANE), F32)],
        compiler_params=_cp(("parallel", "parallel")),
    )(z, z, z, z, z)


def _attn_bwd(z, dya, ya, mg, den):
    S, ZW = z.shape
    T = min(ATTN_T, S)
    scale = HEAD_DIM ** -0.5
    groups = _attn_groups(T)

    def body(q_ref, kc_ref, kp_ref, vc_ref, vp_ref, dy_ref, y_ref, m_ref, n_ref, dq_ref, dk_ref, dv_ref,
             kcat, vcat, dkcat, dvcat):
        b = pl.program_id(1)

        @pl.when(b == 0)
        def _():
            dk_ref[...] = jnp.zeros_like(dk_ref)
            dv_ref[...] = jnp.zeros_like(dv_ref)

        kcat[0:T, :] = kp_ref[...]
        kcat[T:, :] = kc_ref[...]
        vcat[0:T, :] = vp_ref[...]
        vcat[T:, :] = vc_ref[...]
        dkcat[...] = jnp.zeros_like(dkcat)
        dvcat[...] = jnp.zeros_like(dvcat)
        dq_ref[...] = jnp.zeros_like(dq_ref)
        masks = _attn_masks()
        for d, rows, krows, l0 in groups:
            q, dy, y = q_ref[rows, :], dy_ref[rows, :], y_ref[rows, :]
            mrow, nrow = m_ref[rows, :], n_ref[rows, :]
            kk = kcat[krows, :].astype(BF16)
            vv = vcat[krows, :].astype(BF16)
            valid = _band_valid(b * (T // d) + l0)
            dq_acc = jnp.zeros((QH, LANE), F32)
            dk_acc = jnp.zeros((KW, LANE), F32)
            dv_acc = jnp.zeros((KW, LANE), F32)
            for hm in masks:
                qh = jnp.where(hm, q, 0.0).astype(BF16)
                dyh = jnp.where(hm, dy, 0.0)
                dyb = dyh.astype(BF16)
                dsum = jnp.sum(dyh * y, axis=-1, keepdims=True)
                mh = jnp.max(jnp.where(hm, mrow, MASK_VALUE), axis=-1, keepdims=True)
                nh = jnp.max(jnp.where(hm, nrow, 0.0), axis=-1, keepdims=True)
                s = lax.dot_general(qh, kk, (((1,), (1,)), ((), ())), preferred_element_type=F32) * scale
                p = jnp.where(valid, jnp.exp(s - mh), 0.0) / nh
                pb = p.astype(BF16)
                dv_h = lax.dot_general(pb, dyb, (((0,), (0,)), ((), ())), preferred_element_type=F32)
                dp = lax.dot_general(dyb, vv, (((1,), (1,)), ((), ())), preferred_element_type=F32)
                ds = (p * (dp - dsum) * scale).astype(BF16)
                dq_h = jnp.dot(ds, kk, preferred_element_type=F32)
                dk_h = lax.dot_general(ds, qh, (((0,), (0,)), ((), ())), preferred_element_type=F32)
                dq_acc += jnp.where(hm, dq_h, 0.0)
                dk_acc += dk_h
                dv_acc += dv_h
            dq_ref[rows, :] += dq_acc
            dkcat[krows, :] += dk_acc
            dvcat[krows, :] += dv_acc

        base = pl.multiple_of(b * T, T)
        dk_ref[pl.ds(base, T), :] += dkcat[T:, :]
        dv_ref[pl.ds(base, T), :] += dvcat[T:, :]

        @pl.when(b > 0)
        def _():
            prev = pl.multiple_of(b * T - T, T)
            dk_ref[pl.ds(prev, T), :] += dkcat[0:T, :]
            dv_ref[pl.ds(prev, T), :] += dvcat[0:T, :]

    qc, _ = _attn_specs(T, 0)
    kc, kp = _attn_specs(T, ATTN_W // LANE)
    vc, vp = _attn_specs(T, 2 * ATTN_W // LANE)
    whole = pl.BlockSpec((S, LANE), lambda hp, b: (0, hp))
    shp = jax.ShapeDtypeStruct((S, ATTN_W), F32)
    return pl.pallas_call(
        body, name="attn_bwd", grid=(ATTN_W // LANE, S // T),
        in_specs=[qc, kc, kp, vc, vp, qc, qc, qc, qc], out_specs=[qc, whole, whole], out_shape=[shp, shp, shp],
        scratch_shapes=[pltpu.VMEM((2 * T, LANE), F32)] * 4,
        compiler_params=_cp(("parallel", "arbitrary")),
    )(z, z, z, z, z, dya, ya, mg, den)


def _ssm_disc(lr, li, logdt, br, bi):
    dt = jnp.exp(logdt)
    mag = jnp.exp(lr * dt)
    ar = mag * jnp.cos(li * dt)
    ai = mag * jnp.sin(li * dt)
    nr, ni = ar - 1.0, ai
    den = lr * lr + li * li
    cr = (nr * lr + ni * li) / den
    ci = (ni * lr - nr * li) / den
    return ar, ai, cr * br - ci * bi, cr * bi + ci * br


def _ssm_prep(lr, li, logdt, br, bi):
    n, c = br.shape
    outs, _ = _rowwise("ssm_prep", lambda *a: (list(_ssm_disc(*a)), []), n, _tile(n, 512),
                       [_full(a) for a in (lr, li, logdt, br, bi)], [],
                       [(1, 1, _c0, F32), (1, 1, _c0, F32), (c, c, _c0, F32), (c, c, _c0, F32)])
    return outs


def _ssm_prep_bwd(lr, li, logdt, br, bi, dar, dai, dbbr, dbbi):
    n, c = br.shape

    def f(lrb, lib, dtb, brb, bib, *cts):
        _, vjp = jax.vjp(_ssm_disc, lrb, lib, dtb, brb, bib)
        return list(vjp(cts)), []

    outs, _ = _rowwise("ssm_prep_bwd", f, n, _tile(n, 512),
                       [_full(a) for a in (lr, li, logdt, br, bi, dar, dai, dbbr, dbbi)], [],
                       [(1, 1, _c0, F32)] * 3 + [(c, c, _c0, F32)] * 2)
    return outs


def _cmul(ar, ai, br, bi):
    return ar * br - ai * bi, ar * bi + ai * br


def _scan_consts(ar, ai, reverse):
    w = ar.shape[-1]
    a1 = (jnp.broadcast_to(ar, (8, w)), jnp.broadcast_to(ai, (8, w)))
    a2 = _cmul(*a1, *a1)
    a4 = _cmul(*a2, *a2)
    a8 = _cmul(*a4, *a4)
    row = lax.broadcasted_iota(jnp.int32, (8, w), 0)
    e = (8 - row) if reverse else (row + 1)
    one, zero = jnp.ones((8, w), F32), jnp.zeros((8, w), F32)
    pw = (one, zero)
    for bit, ap in ((1, a1), (2, a2), (4, a4), (8, a8)):
        sel = (e & bit) != 0
        nxt = _cmul(*pw, *ap)
        pw = (jnp.where(sel, nxt[0], pw[0]), jnp.where(sel, nxt[1], pw[1]))
    steps = []
    for sh, (pr, pi) in zip((1, 2, 4), (a1, a2, a4)):
        keep = (row < 8 - sh) if reverse else (row >= sh)
        steps.append((jnp.where(keep, pr, 0.0), jnp.where(keep, pi, 0.0)))
    return steps, pw, row


def _scan_group(xr, xi, cr, ci, consts, reverse):
    steps, pw, _ = consts
    for sh, (pr, pi) in zip((1, 2, 4), steps):
        by = 8 - sh if reverse else sh
        tr_, ti_ = _cmul(pr, pi, pltpu.roll(xr, by, 0), pltpu.roll(xi, by, 0))
        xr = xr + tr_
        xi = xi + ti_
    tr_, ti_ = _cmul(pw[0], pw[1], cr, ci)
    return xr + tr_, xi + ti_


def _ssm_fwd(z, a_r, a_i, bdr, bdi, cmr, cmi, dskip, ts, rider=None):
    S, ZW = z.shape
    NS = SSM_G * SSM_P
    PW = PACK * SSM_P
    uoff = (ZW - SSM_W) // LANE
    nsteps = S // ts

    def body(u_ref, ar_ref, ai_ref, bdr_ref, bdi_ref, cmr_ref, cmi_ref, d_ref, hr_ref, hi_ref, y_ref, car_r, car_i):
        s = pl.program_id(1)

        @pl.when(s == 0)
        def _():
            car_r[...] = jnp.zeros_like(car_r)
            car_i[...] = jnp.zeros_like(car_i)

        u = u_ref[...]
        ub = u.astype(BF16)
        nt = (((1,), (1,)), ((), ()))
        hr_ref[...] = lax.dot_general(ub, bdr_ref[...], nt, preferred_element_type=F32)
        hi_ref[...] = lax.dot_general(ub, bdi_ref[...], nt, preferred_element_type=F32)
        consts = _scan_consts(ar_ref[...], ai_ref[...], False)

        def step(j, carry):
            rows = pl.ds(pl.multiple_of(j * 8, 8), 8)
            hr, hi = _scan_group(hr_ref[rows, :], hi_ref[rows, :], carry[0], carry[1], consts, False)
            hr_ref[rows, :] = hr
            hi_ref[rows, :] = hi
            return jnp.broadcast_to(hr[7:8, :], (8, PW)), jnp.broadcast_to(hi[7:8, :], (8, PW))

        cr, ci = lax.fori_loop(0, ts // 8, step, (car_r[...], car_i[...]))
        car_r[...] = cr
        car_i[...] = ci
        y = lax.dot_general(hr_ref[...].astype(BF16), cmr_ref[...], nt, preferred_element_type=F32)
        y -= lax.dot_general(hi_ref[...].astype(BF16), cmi_ref[...], nt, preferred_element_type=F32)
        y_ref[...] = y + d_ref[...] * u

    row_a = pl.BlockSpec((1, PW), lambda i, s: (0, i))
    (hr, hi, y), extra = _hosted(
        "ssm_fwd", body, (SSM_G // PACK, nsteps),
        [pl.BlockSpec((ts, LANE), lambda i, s: (s, uoff + i)), row_a, row_a,
         pl.BlockSpec((None, PW, LANE), lambda i, s: (i, 0, 0)), pl.BlockSpec((None, PW, LANE), lambda i, s: (i, 0, 0)),
         pl.BlockSpec((None, LANE, PW), lambda i, s: (i, 0, 0)), pl.BlockSpec((None, LANE, PW), lambda i, s: (i, 0, 0)),
         pl.BlockSpec((1, LANE), lambda i, s: (0, i))],
        [pl.BlockSpec((ts, PW), lambda i, s: (s, i)), pl.BlockSpec((ts, PW), lambda i, s: (s, i)),
         pl.BlockSpec((ts, LANE), lambda i, s: (s, i))],
        [jax.ShapeDtypeStruct((S, NS), F32), jax.ShapeDtypeStruct((S, NS), F32), jax.ShapeDtypeStruct((S, SSM_W), F32)],
        [pltpu.VMEM((8, PW), F32), pltpu.VMEM((8, PW), F32)], (z, a_r, a_i, bdr, bdi, cmr, cmi, dskip),
        ("parallel", "arbitrary"), rider)
    return hr, hi, y, extra


def _ssm_bwd(z, dyp, hr, hi, a_r, a_i, bdr, bdi, cmr, cmi, dskip, ts, jobs=None):
    S, ZW = z.shape
    NS = SSM_G * SSM_P
    PW = PACK * SSM_P
    uoff = (ZW - SSM_W) // LANE
    nsteps = S // ts
    npk = SSM_G // PACK

    def body(u_ref, dy_ref, hr_ref, hi_ref, hpr_ref, hpi_ref, ar_ref, ai_ref, bdr_ref, bdi_ref, cmr_ref, cmi_ref,
             d_ref, du_ref, dbdr_ref, dbdi_ref, dcmr_ref, dcmi_ref, dar_ref, dai_ref, dd_ref,
             lr_s, li_s, hcr, hci, car_r, car_i):
        s = pl.program_id(1)
        first_tile = s == nsteps - 1

        @pl.when(s == 0)
        def _():
            car_r[...] = jnp.zeros_like(car_r)
            car_i[...] = jnp.zeros_like(car_i)
            for r in (dbdr_ref, dbdi_ref, dcmr_ref, dcmi_ref, dar_ref, dai_ref, dd_ref):
                r[...] = jnp.zeros_like(r)

        u, dy = u_ref[...], dy_ref[...]
        ub, dyb = u.astype(BF16), dy.astype(BF16)
        lr_s[...] = jnp.dot(dyb, cmr_ref[...], preferred_element_type=F32)
        li_s[...] = -jnp.dot(dyb, cmi_ref[...], preferred_element_type=F32)
        keep_prev = jnp.where(first_tile, 0.0, 1.0)
        hcr[0:8, :] = hpr_ref[...] * keep_prev
        hci[0:8, :] = hpi_ref[...] * keep_prev
        hcr[8:, :] = hr_ref[...]
        hci[8:, :] = hi_ref[...]
        consts = _scan_consts(ar_ref[...], -ai_ref[...], True)
        row = consts[2]
        ngrp = ts // 8

        def step(jj, carry):
            cr, ci, accr, acci = carry
            j = ngrp - 1 - jj
            rows = pl.ds(pl.multiple_of(j * 8, 8), 8)
            nxt = pl.ds(pl.multiple_of(j * 8 + 8, 8), 8)
            lr, li = _scan_group(lr_s[rows, :], li_s[rows, :], cr, ci, consts, True)
            lr_s[rows, :] = lr
            li_s[rows, :] = li
            pr, pi = hcr[rows, :], hci[rows, :]
            hsr = jnp.where(row == 0, jnp.broadcast_to(pr[7:8, :], (8, PW)), pltpu.roll(hcr[nxt, :], 1, 0))
            hsi = jnp.where(row == 0, jnp.broadcast_to(pi[7:8, :], (8, PW)), pltpu.roll(hci[nxt, :], 1, 0))
            accr = accr + lr * hsr + li * hsi
            acci = acci + li * hsr - lr * hsi
            return jnp.broadcast_to(lr[0:1, :], (8, PW)), jnp.broadcast_to(li[0:1, :], (8, PW)), accr, acci

        zero = jnp.zeros((8, PW), F32)
        cr, ci, accr, acci = lax.fori_loop(0, ngrp, step, (car_r[...], car_i[...], zero, zero))
        car_r[...] = cr
        car_i[...] = ci
        dar_ref[...] += jnp.sum(accr, axis=0, keepdims=True)
        dai_ref[...] += jnp.sum(acci, axis=0, keepdims=True)
        lrb, lib = lr_s[...].astype(BF16), li_s[...].astype(BF16)
        du = jnp.dot(lrb, bdr_ref[...], preferred_element_type=F32)
        du += jnp.dot(lib, bdi_ref[...], preferred_element_type=F32)
        du_ref[...] = du + dy * d_ref[...]
        tn = (((0,), (0,)), ((), ()))
        dbdr_ref[...] += lax.dot_general(lrb, ub, tn, preferred_element_type=F32)
        dbdi_ref[...] += lax.dot_general(lib, ub, tn, preferred_element_type=F32)
        dcmr_ref[...] += lax.dot_general(dyb, hr_ref[...].astype(BF16), tn, preferred_element_type=F32)
        dcmi_ref[...] -= lax.dot_general(dyb, hi_ref[...].astype(BF16), tn, preferred_element_type=F32)
        dd_ref[...] += jnp.sum(dy * u, axis=0, keepdims=True)

    rev = lambda s: nsteps - 1 - s
    row_a = pl.BlockSpec((1, PW), lambda i, s: (0, i))
    tile = pl.BlockSpec((ts, PW), lambda i, s: (rev(s), i))
    prev8 = pl.BlockSpec((8, PW), lambda i, s: (jnp.maximum(rev(s) * (ts // 8) - 1, 0), i))
    cols = pl.BlockSpec((ts, LANE), lambda i, s: (rev(s), i))
    bd = pl.BlockSpec((None, PW, LANE), lambda i, s: (i, 0, 0))
    cm = pl.BlockSpec((None, LANE, PW), lambda i, s: (i, 0, 0))
    outs, _ = _hosted(
        "ssm_bwd", body, (npk, nsteps),
        [pl.BlockSpec((ts, LANE), lambda i, s: (rev(s), uoff + i)), cols, tile, tile, prev8, prev8,
         row_a, row_a, bd, bd, cm, cm, pl.BlockSpec((1, LANE), lambda i, s: (0, i))],
        [cols, bd, bd, cm, cm, row_a, row_a, pl.BlockSpec((1, LANE), lambda i, s: (0, i))],
        [jax.ShapeDtypeStruct((S, SSM_W), F32),
         jax.ShapeDtypeStruct((npk, PW, LANE), F32), jax.ShapeDtypeStruct((npk, PW, LANE), F32),
         jax.ShapeDtypeStruct((npk, LANE, PW), F32), jax.ShapeDtypeStruct((npk, LANE, PW), F32),
         jax.ShapeDtypeStruct((1, NS), F32), jax.ShapeDtypeStruct((1, NS), F32), jax.ShapeDtypeStruct((1, SSM_W), F32)],
        [pltpu.VMEM((ts, PW), F32), pltpu.VMEM((ts, PW), F32), pltpu.VMEM((ts + 8, PW), F32),
         pltpu.VMEM((ts + 8, PW), F32), pltpu.VMEM((8, PW), F32), pltpu.VMEM((8, PW), F32)],
        (z, dyp, hr, hi, hr, hi, a_r, a_i, bdr, bdi, cmr, cmi, dskip), ("parallel", "arbitrary"), None, jobs)
    return outs


def _block_diag(m4):
    npk, g, a, b = m4.shape
    eye = jnp.eye(g, dtype=m4.dtype)
    return (m4[:, :, :, None, :] * eye[None, :, None, :, None]).reshape(npk, g * a, g * b)


def _block_diag_take(m, a, b):
    npk = m.shape[0]
    m5 = m.reshape(npk, PACK, a, PACK, b)
    return jnp.stack([m5[:, g, :, g, :] for g in range(PACK)], axis=1)


def _mix_out(ya, ypre, gl, ga, gb, bglu):
    yg = jax.nn.gelu(ypre)
    yb = yg * jax.nn.sigmoid(gl + bglu)
    return jnp.concatenate([_rms(ya, ga), _rms(yb, gb)], axis=-1)


def _tail_loss(h3, gl, pe, gf, tgt):
    h4 = h3 + jax.nn.sigmoid(gl) * pe
    err = jnp.square(_rms(h4, gf) - tgt)
    return 0.5 * jnp.mean(err, axis=-1, keepdims=True)


def kernel(x, p, ffn1_norm, ffn1_w_gate, ffn1_w_up, ffn1_w_down, mix_norm, w_in, attn_out_norm, ssm_lambda_re, ssm_lambda_im, ssm_log_dt, ssm_b_re, ssm_b_im, ssm_c_re, ssm_c_im, ssm_d, ssm_w_glu, ssm_b_glu, ssm_out_norm, w_out, ffn2_norm, ffn2_w_gate, ffn2_w_up, ffn2_w_down, ple_norm, ple_w_gate, ple_w_proj, final_norm, loss_target, m_ffn1_norm, m_ffn1_w_gate, m_ffn1_w_up, m_ffn1_w_down, m_mix_norm, m_w_in, m_attn_out_norm, m_ssm_lambda_re, m_ssm_lambda_im, m_ssm_log_dt, m_ssm_b_re, m_ssm_b_im, m_ssm_c_re, m_ssm_c_im, m_ssm_d, m_ssm_w_glu, m_ssm_b_glu, m_ssm_out_norm, m_w_out, m_ffn2_norm, m_ffn2_w_gate, m_ffn2_w_up, m_ffn2_w_down, m_ple_norm, m_ple_w_gate, m_ple_w_proj, m_final_norm, v_ffn1_norm, v_ffn1_w_gate, v_ffn1_w_up, v_ffn1_w_down, v_mix_norm, v_w_in, v_attn_out_norm, v_ssm_lambda_re, v_ssm_lambda_im, v_ssm_log_dt, v_ssm_b_re, v_ssm_b_im, v_ssm_c_re, v_ssm_c_im, v_ssm_d, v_ssm_w_glu, v_ssm_b_glu, v_ssm_out_norm, v_w_out, v_ffn2_norm, v_ffn2_w_gate, v_ffn2_w_up, v_ffn2_w_down, v_ple_norm, v_ple_w_gate, v_ple_w_proj, v_final_norm):
    A = dict(locals())
    xs = x[0]
    ps = p[0, 0]
    tgt = loss_target[0]
    S, D = xs.shape
    FSH = ffn1_w_gate.shape[-1]
    FSP = -(-FSH // LANE) * LANE
    TR = _tile(S, 256)
    ZW = 3 * ATTN_W + SSM_W

    wgu1 = _prep("prep_gu1", [ffn1_w_gate[0], ffn1_w_up[0]], D, FSP)
    wgu2 = _prep("prep_gu2", [ffn2_w_gate[0], ffn2_w_up[0]], D, FSP)
    wd1 = _prep("prep_d1", [ffn1_w_down[0]], FSP, D)
    wd2 = _prep("prep_d2", [ffn2_w_down[0]], FSP, D)
    win = _prep("prep_in", [w_in[0]], D, w_in.shape[-1])
    wglu = _prep("prep_glu", [ssm_w_glu[0]], ssm_w_glu.shape[1], SSM_W)
    wout = _prep("prep_out", [w_out[0]], w_out.shape[1], D)
    wpg = _prep("prep_pg", [ple_w_gate[0]], ple_w_gate.shape[1], D)
    wpp = _prep("prep_pp", [ple_w_proj[0]], ple_w_proj.shape[1], ple_w_proj.shape[2])
    (Wgu1,) = _all_gather("ag_weights", [wgu1])
    rowstack = lambda w: w.reshape(1, w.shape[0] * w.shape[1], w.shape[2])

    def ffn_norm(tag, h, gain):
        return _rowwise(f"{tag}_norm", lambda a, g: ([_rms(a, g)], []), S, TR, [_full(h)], [gain], [(D, D, _c0, BF16)])[0][0]

    xn1 = ffn_norm("ffn1", xs, ffn1_norm)
    gu1, hid1, (Wd1, Win) = _ffn_up("ffn1_up", xn1, Wgu1, rider=_GatherRider([wd1, win]))
    Wd1 = rowstack(Wd1)
    h1, un, (Wd2,) = _mm_nn("ffn1_down", hid1, Wd1, tn=D, tk=2 * FSP, res=xs, scale=0.5, gain=mix_norm,
                            rider=_GatherRider([wd2]))
    Wd2 = rowstack(Wd2)
    z, (Wglu, Wout, Wpg, Wpp) = _mm_nn("mix_in", un, Win, tn=512, tk=D,
                                       rider=_GatherRider([wglu, wout, wpg, wpp]))
    Wglu, Wout, Wpg = rowstack(Wglu), rowstack(Wout), rowstack(Wpg)
    ya, mg, den = _attn_fwd(z)

    col = lambda a: a.reshape(-1, 1)
    lr_c, li_c = col(ssm_lambda_re), col(ssm_lambda_im)
    dt_c = col(jnp.broadcast_to(ssm_log_dt.reshape(SSM_G, 1), (SSM_G, SSM_P)))
    b_re2, b_im2 = ssm_b_re.reshape(-1, SSM_C), ssm_b_im.reshape(-1, SSM_C)
    ar_c, ai_c, bbr, bbi = _ssm_prep(lr_c, li_c, dt_c, b_re2, b_im2)
    a_r, a_i = ar_c.reshape(1, -1), ai_c.reshape(1, -1)
    npk = SSM_G // PACK
    bdr = _block_diag(bbr.reshape(npk, PACK, SSM_P, SSM_C)).astype(BF16)
    bdi = _block_diag(bbi.reshape(npk, PACK, SSM_P, SSM_C)).astype(BF16)
    cmr = _block_diag(ssm_c_re.reshape(npk, PACK, SSM_C, SSM_P)).astype(BF16)
    cmi = _block_diag(ssm_c_im.reshape(npk, PACK, SSM_C, SSM_P)).astype(BF16)
    TS = _tile(S, 512)
    hr, hi, ypre, (Wgu2,) = _ssm_fwd(z, a_r, a_i, bdr, bdi, cmr, cmi, ssm_d, TS, rider=_GatherRider([wgu2]))
    (yg,), _ = _rowwise("ssm_gelu", lambda a: ([jax.nn.gelu(a)], []), S, TR, [_full(ypre)], [], [(SSM_W, SSM_W, _c0, BF16)])
    gl = _mm_nn("ssm_glu", yg, Wglu, tn=SSM_W, tk=SSM_W)
    (ycat,), _ = _rowwise("mix_out", lambda *a: ([_mix_out(*a)], []), S, TR, [_full(ya), _full(ypre), _full(gl)],
                          [attn_out_norm, ssm_out_norm, ssm_b_glu], [(MIX_W, MIX_W, _c0, BF16)])
    h2, xn2 = _mm_nn("mix_proj", ycat, Wout, tn=D, tk=D, res=h1, scale=1.0, gain=ffn2_norm)
    gu2, hid2, _ = _ffn_up("ffn2_up", xn2, Wgu2)
    h3, hn = _mm_nn("ffn2_down", hid2, Wd2, tn=D, tk=2 * FSP, res=h2, scale=0.5, gain=ple_norm)
    pgl = _mm_nn("ple_gate", hn, Wpg, tn=D // 2, tk=D)
    pb = ps
    pe = _mm_nn("ple_proj", pb, Wpp, tn=Wpp.shape[2], tk=Wpp.shape[1])

    def tail(h3b, glb, peb, tb, gf):
        rows, vjp = jax.vjp(lambda a, b, c, g: _tail_loss(a, b, c, g, tb), h3b, glb, peb, gf)
        dh, dgl, dpe, dgf = vjp(jnp.ones_like(rows))
        return [dh, dgl, dpe], [jnp.broadcast_to(jnp.sum(rows, axis=0, keepdims=True), (1, LANE)), dgf]

    (dh3_dir, dpgl, dpe), (loss_row, g_final) = _rowwise(
        "tail", tail, S, TR, [_full(h3), _full(pgl), _full(pe), _full(tgt)], [final_norm.reshape(1, D)],
        [(D, D, _c0, F32), (D, D, _c0, BF16), (D, D, _c0, BF16)], [(LANE, LANE, _c0), (D, D, _c0)])
    loss = lax.psum(loss_row[0, 0], AXES)

    def norm_bwd(tag, h, gain, dn, dres):
        def f(hb, dnb, drb, g):
            _, vjp = jax.vjp(_rms, hb, g)
            dh, dg = vjp(dnb)
            dh = dh + drb
            return [dh, dh], [dg]
        (dh, dhb), (dg,) = _rowwise(f"{tag}_norm_bwd", f, S, TR, [_full(h), _full(dn), _full(dres)], [gain],
                                    [(D, D, _c0, F32), (D, D, _c0, BF16)], [(D, D, _c0)])
        return dh, dhb, dg

    restack = lambda g: g.reshape((NDEV, g.shape[1] // NDEV) + g.shape[2:])
    jobs, scat = [], {}

    def scatter(key, g):
        scat[key] = _Scatter("rs_" + key, g)
        jobs.append(scat[key])

    late = []
    dhn = _mm_nt("ple_gate_dx", dpgl, Wpg, tn=D, tk=D)
    late.append(lambda: scatter("pg", restack(_mm_tn("ple_gate_dw", hn, dpgl, 1, jobs=jobs))))
    late.append(lambda: scatter("pp", _mm_tn("ple_proj_dw", pb, dpe, NDEV, jobs=jobs)))
    dh3, dh3b, g_ple_norm = norm_bwd("ple", h3, ple_norm, dhn, dh3_dir)

    def ffn_bwd(tag, h, gain, Wgu, Wd, saved, dout, doutb):
        xn, gu, hid = saved
        dgu = _ffn_down_dx(f"{tag}_down_dx", doutb, Wd, gu, NDEV, scale=0.5, jobs=jobs)
        scatter(tag + "gu", _mm_tn(f"{tag}_up_dw", xn, dgu, NDEV, tm=2048, tn=FSP, jobs=jobs))
        scatter(tag + "d", restack(_mm_tn(f"{tag}_down_dw", hid, doutb, 1, tm=2048, tko=FSP, tn=D // 2, scale=0.5,
                                          jobs=jobs)))
        dxn = _mm_nt(f"{tag}_up_dx", dgu, Wgu, tm=512, tn=D, tk=2 * FSP, jobs=jobs)
        dh, dhb, g_norm = norm_bwd(tag, h, gain, dxn, dout)
        return dh, dhb, g_norm

    dh2, dh2b, g_ffn2_norm = ffn_bwd("ffn2", h2, ffn2_norm, Wgu2, Wd2, (xn2, gu2, hid2), dh3, dh3b)

    dycat = _mm_nt("mix_proj_dx", dh2b, Wout, tn=D, tk=D, jobs=jobs)
    late.append(lambda: scatter("out", restack(_mm_tn("mix_proj_dw", ycat, dh2b, 1, jobs=jobs))))

    def mix_out_bwd(yab, ypb, glb, dyc, ga, gb, bglu):
        _, vjp = jax.vjp(_mix_out, yab, ypb, glb, ga, gb, bglu)
        dya_, dyp_, dgl_, dga, dgb, dbg = vjp(dyc)
        return [dya_, dyp_, dgl_], [dga, dgb, dbg]
    (dya, dyp_dir, dglb), (g_attn_norm, g_ssm_norm, g_bglu) = _rowwise(
        "mix_out_bwd", mix_out_bwd, S, TR, [_full(ya), _full(ypre), _full(gl), _full(dycat)],
        [attn_out_norm, ssm_out_norm, ssm_b_glu],
        [(ATTN_W, ATTN_W, _c0, F32), (SSM_W, SSM_W, _c0, F32), (SSM_W, SSM_W, _c0, BF16)],
        [(ATTN_W, ATTN_W, _c0), (SSM_W, SSM_W, _c0), (SSM_W, SSM_W, _c0)])
    dyg = _mm_nt("ssm_glu_dx", dglb, Wglu, tn=SSM_W, tk=SSM_W, jobs=jobs)
    late.append(lambda: scatter("glu", restack(_mm_tn("ssm_glu_dw", yg, dglb, 1, jobs=jobs))))

    def gelu_bwd(ypb, dygb, ddir):
        _, vjp = jax.vjp(jax.nn.gelu, ypb)
        return [ddir + vjp(dygb)[0]], []
    (dyp,), _ = _rowwise("ssm_gelu_bwd", gelu_bwd, S, TR, [_full(ypre), _full(dyg), _full(dyp_dir)], [],
                         [(SSM_W, SSM_W, _c0, F32)])
    du, dbdr, dbdi, dcmr, dcmi, da_r, da_i, g_ssm_d = _ssm_bwd(z, dyp, hr, hi, a_r, a_i, bdr, bdi, cmr, cmi, ssm_d, TS,
                                                              jobs=jobs)
    dbbr = _block_diag_take(dbdr, SSM_P, SSM_C).reshape(-1, SSM_C)
    dbbi = _block_diag_take(dbdi, SSM_P, SSM_C).reshape(-1, SSM_C)
    g_c_re = _block_diag_take(dcmr, SSM_C, SSM_P).reshape(ssm_c_re.shape)
    g_c_im = _block_diag_take(dcmi, SSM_C, SSM_P).reshape(ssm_c_im.shape)
    dlr, dli, ddt, g_b_re, g_b_im = _ssm_prep_bwd(lr_c, li_c, dt_c, b_re2, b_im2, col(da_r), col(da_i), dbbr, dbbi)
    g_lam_re, g_lam_im = dlr.reshape(ssm_lambda_re.shape), dli.reshape(ssm_lambda_im.shape)
    g_log_dt = jnp.sum(ddt.reshape(SSM_G, SSM_P), axis=1).reshape(ssm_log_dt.shape)
    g_b_re, g_b_im = g_b_re.reshape(ssm_b_re.shape), g_b_im.reshape(ssm_b_im.shape)

    dq, dk, dv = _attn_bwd(z, dya, ya, mg, den)
    (dz,), _ = _rowwise("mix_dz", lambda *a: ([jnp.concatenate(a, axis=-1)], []), S, TR,
                        [_full(dq), _full(dk), _full(dv), _full(du)], [], [(ZW, ZW, _c0, BF16)])
    dun = _mm_nt("mix_in_dx", dz, Win, tn=D, tk=512, jobs=jobs)
    scatter("in", _mm_tn("mix_in_dw", un, dz, NDEV, tn=512, jobs=jobs))
    dh1, dh1b, g_mix_norm = norm_bwd("mix", h1, mix_norm, dun, dh2)

    dx, _dxb, g_ffn1_norm = ffn_bwd("ffn1", xs, ffn1_norm, Wgu1, Wd1, (xn1, gu1, hid1), dh1, dh1b)
    for run in (late[2], late[0], late[3], late[1]):
        run()

    out = {}

    def upd(name, key, *, tr, cw, gw, goff=0):
        w, m, v = A[name][0], A["m_" + name][0], A["v_" + name][0]
        g, dlt, mn, vn = _adamw("adamw_" + name, w, m, v, scat[key].finish(), tr=tr, cw=cw, gw=gw, goff=goff, jobs=jobs)
        for k, val in (("grad_", g), ("delta_", dlt), ("new_m_", mn), ("new_v_", vn)):
            out[k + name] = val[None]

    DT = _tile(D, 256)
    FT = _tile(FSH, 512)
    DC = _tile(D, 1024, LANE)
    upd("ffn2_w_gate", "ffn2gu", tr=DT, cw=FSH, gw=FSP, goff=0)
    upd("ffn2_w_up", "ffn2gu", tr=DT, cw=FSH, gw=FSP, goff=1)
    upd("ffn2_w_down", "ffn2d", tr=FT, cw=DC, gw=DC)
    upd("w_in", "in", tr=DT, cw=w_in.shape[-1], gw=w_in.shape[-1])
    upd("ffn1_w_gate", "ffn1gu", tr=DT, cw=FSH, gw=FSP, goff=0)
    upd("ffn1_w_up", "ffn1gu", tr=DT, cw=FSH, gw=FSP, goff=1)
    upd("ffn1_w_down", "ffn1d", tr=FT, cw=DC, gw=DC)
    upd("w_out", "out", tr=w_out.shape[1], cw=DC, gw=DC)
    upd("ple_w_gate", "pg", tr=ple_w_gate.shape[1], cw=DC, gw=DC)
    upd("ssm_w_glu", "glu", tr=ssm_w_glu.shape[1], cw=SSM_W, gw=SSM_W)
    upd("ple_w_proj", "pp", tr=ple_w_proj.shape[1], cw=ple_w_proj.shape[2], gw=ple_w_proj.shape[2])

    small = [("ffn1_norm", g_ffn1_norm), ("mix_norm", g_mix_norm), ("attn_out_norm", g_attn_norm),
             ("ssm_lambda_re", g_lam_re), ("ssm_lambda_im", g_lam_im), ("ssm_log_dt", g_log_dt),
             ("ssm_b_re", g_b_re), ("ssm_b_im", g_b_im), ("ssm_c_re", g_c_re), ("ssm_c_im", g_c_im),
             ("ssm_d", g_ssm_d), ("ssm_b_glu", g_bglu), ("ssm_out_norm", g_ssm_norm), ("ffn2_norm", g_ffn2_norm),
             ("ple_norm", g_ple_norm), ("final_norm", g_final)]
    chunk = 8 * LANE

    def pack(arrs):
        parts = []
        for a in arrs:
            flat = a.reshape(-1)
            padn = -(-flat.shape[0] // chunk) * chunk
            parts.append(jnp.pad(flat, (0, padn - flat.shape[0])).reshape(-1, LANE))
        return jnp.concatenate(parts, axis=0)

    g_pack = pack([g for _, g in small])
    (g_all,) = _all_gather("ag_small", [g_pack])
    g_sum = _sum8("small_sum", g_all)
    w_pack = pack([A[n] for n, _ in small])
    m_pack = pack([A["m_" + n] for n, _ in small])
    v_pack = pack([A["v_" + n] for n, _ in small])
    d_pack, mn_pack, vn_pack = _adamw_small("adamw_small", w_pack, m_pack, v_pack, g_sum)
    off = 0
    for n, _ in small:
        shape = A[n].shape
        size = math.prod(shape)
        rows = -(-size // chunk) * 8
        for k, buf in (("grad_", g_sum), ("delta_", d_pack), ("new_m_", mn_pack), ("new_v_", vn_pack)):
            out[k + n] = buf[off:off + rows].reshape(-1)[:size].reshape(shape)
        off += rows

    names = ['ffn1_norm', 'ffn1_w_gate', 'ffn1_w_up', 'ffn1_w_down', 'mix_norm', 'w_in', 'attn_out_norm',
             'ssm_lambda_re', 'ssm_lambda_im', 'ssm_log_dt', 'ssm_b_re', 'ssm_b_im', 'ssm_c_re', 'ssm_c_im', 'ssm_d',
             'ssm_w_glu', 'ssm_b_glu', 'ssm_out_norm', 'w_out', 'ffn2_norm', 'ffn2_w_gate', 'ffn2_w_up', 'ffn2_w_down',
             'ple_norm', 'ple_w_gate', 'ple_w_proj', 'final_norm']
    return (loss, dx[None], *[out[k + n] for k in ("grad_", "delta_", "new_m_", "new_v_") for n in names])
```

```python
import functools
import math

import jax
import jax.numpy as jnp
from jax import lax
from jax.experimental import pallas as pl
from jax.experimental.pallas import tpu as pltpu

F32, BF16 = jnp.float32, jnp.bfloat16
MESH = pl.DeviceIdType.MESH
NDEV = 8
AXES = ("x", "y", "c")
LANE = 128
VMEM_LIMIT = 56 * 1024 * 1024

ATTN_W = 1024
HEAD_DIM = 64
SSM_W = 1024
MIX_W = ATTN_W + SSM_W
SSM_G, SSM_P, SSM_C = 64, 64, 16
PACK = 8
DILATIONS = (1, 4, 16)
QB = 128
NORM_EPS = 1e-6
MASK_VALUE = -1e30
LR, B1, B2, EPS, WD, STEP = 0.001, 0.9, 0.999, 1e-08, 0.01, 10


def _cp(sem=None):
    return pltpu.CompilerParams(dimension_semantics=sem, vmem_limit_bytes=VMEM_LIMIT)


def _tile(n, target, mult=8):
    if n <= target:
        return n
    for t in range(target - target % mult, 0, -mult):
        if n % t == 0:
            return t
    return n


def _rms(x, g):
    return x * lax.rsqrt(jnp.mean(x * x, axis=-1, keepdims=True) + NORM_EPS) * g


def _rowwise(name, fn, S, tr, rows, fulls, outs, accs=(), ncol=1):
    nr, nf, no, na = len(rows), len(fulls), len(outs), len(accs)

    def body(*refs):
        ins = [r[...] for r in refs[:nr + nf]]
        o_refs = refs[nr + nf:nr + nf + no]
        a_refs = refs[nr + nf + no:]
        o_vals, a_vals = fn(*ins)
        for r, v in zip(o_refs, o_vals):
            r[...] = v.astype(r.dtype)
        if na:
            @pl.when(pl.program_id(1) == 0)
            def _():
                for r in a_refs:
                    r[...] = jnp.zeros_like(r)
            for r, v in zip(a_refs, a_vals):
                r[...] += v

    in_specs = [pl.BlockSpec((tr, w), functools.partial(lambda j, i, cm: (i, cm(j)), cm=cm)) for _, w, cm in rows]
    in_specs += [pl.BlockSpec(f.shape, functools.partial(lambda j, i, nd: (0,) * nd, nd=f.ndim)) for f in fulls]
    out_specs = [pl.BlockSpec((tr, w), functools.partial(lambda j, i, cm: (i, cm(j)), cm=cm)) for _, w, cm, _ in outs]
    out_specs += [pl.BlockSpec((1, w), functools.partial(lambda j, i, cm: (0, cm(j)), cm=cm)) for _, w, cm in accs]
    out_shape = [jax.ShapeDtypeStruct((S, c), dt) for c, _, _, dt in outs]
    out_shape += [jax.ShapeDtypeStruct((1, c), F32) for c, _, _ in accs]
    res = pl.pallas_call(
        body, name=name, grid=(ncol, S // tr), in_specs=in_specs, out_specs=out_specs, out_shape=out_shape,
        compiler_params=_cp(("parallel", "arbitrary" if na else "parallel")),
    )(*[a for a, _, _ in rows], *fulls)
    return res[:no], res[no:]


def _c0(j):
    return 0


def _full(a):
    return (a, a.shape[1], _c0)


def _hosted(name, body, grid, in_specs, out_specs, out_shape, scratch, args, sem, rider=None, jobs=None):
    nsteps = math.prod(grid)

    def step_of(*g):
        t = 0
        for gi, n in zip(g, grid):
            t = t * n + gi
        return t

    job = None
    if rider is None:
        job, rider = _pick(jobs, nsteps)
    if rider is None:
        outs = pl.pallas_call(body, name=name, grid=grid, in_specs=in_specs, out_specs=out_specs, out_shape=out_shape,
                              scratch_shapes=scratch, compiler_params=_cp(sem))(*args)
        return outs, None
    rider.bind(step_of, nsteps)
    n_in, n_out, n_scr = len(in_specs), len(out_specs), len(scratch)

    def full(*refs):
        a, b = n_in, n_in + rider.n_in
        c, d = b + n_out, b + n_out + rider.n_out
        rider.run(refs[a:b], refs[c:d], refs[d + n_scr:], step_of(*[pl.program_id(i) for i in range(len(grid))]), nsteps)
        body(*(refs[:a] + refs[b:c] + refs[d:d + n_scr]))

    outs = pl.pallas_call(
        full, name=name, grid=grid, in_specs=in_specs + rider.in_specs, out_specs=out_specs + rider.out_specs,
        out_shape=out_shape + rider.out_shape, scratch_shapes=scratch + rider.scratch,
        compiler_params=_cp(("arbitrary",) * len(grid)))(*args, *rider.operands)
    extra = rider.take(outs[n_out:])
    if job is not None:
        job.advance(extra)
        extra = None
    return outs[:n_out], extra


def _mm_nn(name, a, w, *, out_dtype=F32, tm=512, tn=768, tk=2048, res=None, scale=1.0, gain=None, rider=None,
           jobs=None):
    M, K = a.shape
    J, K2, Np = w.shape
    assert K == K2
    tm, tn, tk = _tile(M, tm), _tile(Np, tn, LANE), _tile(K, tk, LANE)
    npj = Np // tn
    nk = K // tk
    grid = (M // tm, J * npj, nk)
    assert gain is None or (J * npj == 1 and res is not None)

    def body(*refs):
        refs = list(refs)
        a_ref, w_ref = refs[:2]
        r_ref = refs[2] if res is not None else None
        g_ref = refs[3] if gain is not None else None
        acc = refs[-1]
        o_ref = refs[-3] if gain is not None else refs[-2]
        k = pl.program_id(2)
        part = jnp.dot(a_ref[...].astype(BF16), w_ref[...], preferred_element_type=F32)

        def finish(v):
            if res is not None:
                v = r_ref[...] + scale * v
            o_ref[...] = v.astype(o_ref.dtype)
            if gain is not None:
                refs[-2][...] = _rms(v, g_ref[...]).astype(BF16)

        if nk == 1:
            finish(part)
            return

        @pl.when(k == 0)
        def _():
            acc[...] = part

        @pl.when(k > 0)
        def _():
            acc[...] += part

        @pl.when(k == nk - 1)
        def _():
            finish(acc[...])

    in_specs = [pl.BlockSpec((tm, tk), lambda i, n, k: (i, k)),
                pl.BlockSpec((None, tk, tn), lambda i, n, k: (n // npj, k, n % npj))]
    args = [a, w]
    if res is not None:
        in_specs.append(pl.BlockSpec((tm, tn), lambda i, n, k: (i, n)))
        args.append(res)
    out_specs = [pl.BlockSpec((tm, tn), lambda i, n, k: (i, n))]
    out_shape = [jax.ShapeDtypeStruct((M, J * Np), out_dtype)]
    if gain is not None:
        in_specs.append(pl.BlockSpec((1, tn), lambda i, n, k: (0, 0)))
        args.append(gain)
        out_specs.append(pl.BlockSpec((tm, tn), lambda i, n, k: (i, n)))
        out_shape.append(jax.ShapeDtypeStruct((M, J * Np), BF16))
    outs, extra = _hosted(name, body, grid, in_specs, out_specs, out_shape, [pltpu.VMEM((tm, tn), F32)], args,
                          ("parallel", "parallel", "arbitrary"), rider, jobs)
    outs = tuple(outs) + (() if rider is None else (extra,))
    return outs[0] if len(outs) == 1 else outs


def _mm_nt(name, dy, w, *, out_dtype=F32, tm=512, tn=2048, tk=768, scale=1.0, jobs=None):
    M, N = dy.shape
    J, K, Np = w.shape
    assert N == J * Np
    tm, tn, tk = _tile(M, tm), _tile(K, tn, LANE), _tile(Np, tk, LANE)
    npj = Np // tk
    nc = J * npj

    def body(a_ref, w_ref, o_ref, acc):
        c = pl.program_id(2)
        part = lax.dot_general(a_ref[...].astype(BF16), w_ref[...], (((1,), (1,)), ((), ())),
                               preferred_element_type=F32)
        if nc == 1:
            o_ref[...] = (scale * part).astype(o_ref.dtype)
            return

        @pl.when(c == 0)
        def _():
            acc[...] = part

        @pl.when(c > 0)
        def _():
            acc[...] += part

        @pl.when(c == nc - 1)
        def _():
            o_ref[...] = (scale * acc[...]).astype(o_ref.dtype)

    (out,), _ = _hosted(
        name, body, (M // tm, K // tn, nc),
        [pl.BlockSpec((tm, tk), lambda i, n, c: (i, c)),
         pl.BlockSpec((None, tn, tk), lambda i, n, c: (c // npj, n, c % npj))],
        [pl.BlockSpec((tm, tn), lambda i, n, c: (i, n))], [jax.ShapeDtypeStruct((M, K), out_dtype)],
        [pltpu.VMEM((tm, tn), F32)], (dy, w), ("parallel", "parallel", "arbitrary"), None, jobs)
    return out


def _mm_tn(name, x, dy, J, *, tm=8192, tko=256, tn=512, scale=1.0, jobs=None):
    M, K = x.shape
    M2, N = dy.shape
    assert M == M2 and N % J == 0
    Np = N // J
    tm, tko, tn = _tile(M, tm, LANE), _tile(K, tko, LANE), _tile(Np, tn, LANE)
    npj = Np // tn
    nm = M // tm

    def body(x_ref, d_ref, o_ref, acc):
        m = pl.program_id(2)
        part = lax.dot_general(x_ref[...].astype(BF16), d_ref[...].astype(BF16), (((0,), (0,)), ((), ())),
                               preferred_element_type=F32)
        if nm == 1:
            o_ref[...] = scale * part
            return

        @pl.when(m == 0)
        def _():
            acc[...] = part

        @pl.when(m > 0)
        def _():
            acc[...] += part

        @pl.when(m == nm - 1)
        def _():
            o_ref[...] = scale * acc[...]

    nk, nn = K // tko, J * npj
    k_outer = x.size + nk * dy.size <= dy.size + nn * x.size
    kn = (lambda a, b: (a, b)) if k_outer else (lambda a, b: (b, a))
    (out,), _ = _hosted(
        name, body, (nk, nn, nm) if k_outer else (nn, nk, nm),
        [pl.BlockSpec((tm, tko), lambda a, b, m: (m, kn(a, b)[0])),
         pl.BlockSpec((tm, tn), lambda a, b, m: (m, kn(a, b)[1]))],
        [pl.BlockSpec((None, tko, tn), lambda a, b, m: (kn(a, b)[1] // npj, kn(a, b)[0], kn(a, b)[1] % npj))],
        [jax.ShapeDtypeStruct((J, K, Np), F32)], [pltpu.VMEM((tko, tn) if nm > 1 else (8, LANE), F32)], (x, dy),
        ("parallel", "parallel", "arbitrary"), None, jobs)
    return out


def _swiglu_act(g, u):
    return jax.nn.silu(g) * u


def _ffn_up(name, xn, wgu, *, tm=1024, rider=None):
    M, K = xn.shape
    J, _, F2 = wgu.shape
    F = F2 // 2
    tm = _tile(M, tm)

    def body(a_ref, w_ref, gu_ref, h_ref):
        r = jnp.dot(a_ref[...], w_ref[...], preferred_element_type=F32)
        gu_ref[...] = r.astype(gu_ref.dtype)
        h_ref[...] = _swiglu_act(r[:, :F], r[:, F:]).astype(h_ref.dtype)

    (gu, hid), extra = _hosted(
        name, body, (M // tm, J),
        [pl.BlockSpec((tm, K), lambda i, j: (i, 0)), pl.BlockSpec((None, K, F2), lambda i, j: (j, 0, 0))],
        [pl.BlockSpec((tm, F2), lambda i, j: (i, j)), pl.BlockSpec((tm, F), lambda i, j: (i, j))],
        [jax.ShapeDtypeStruct((M, J * F2), BF16), jax.ShapeDtypeStruct((M, J * F), BF16)], [], (xn, wgu),
        ("parallel", "parallel"), rider)
    return gu, hid, extra


def _ffn_down_dx(name, dout, wd, gu, J, *, scale, tm=512, jobs=None):
    M, D = dout.shape
    F = wd.shape[1] // J
    tm = _tile(M, tm)

    def body(d_ref, w_ref, gu_ref, o_ref):
        dh = scale * lax.dot_general(d_ref[...], w_ref[...], (((1,), (1,)), ((), ())), preferred_element_type=F32)
        gu = gu_ref[...].astype(F32)
        _, vjp = jax.vjp(_swiglu_act, gu[:, :F], gu[:, F:])
        o_ref[...] = jnp.concatenate(vjp(dh), axis=-1).astype(o_ref.dtype)

    (out,), _ = _hosted(
        name, body, (M // tm, J),
        [pl.BlockSpec((tm, D), lambda i, j: (i, 0)), pl.BlockSpec((None, F, D), lambda i, j: (0, j, 0)),
         pl.BlockSpec((tm, 2 * F), lambda i, j: (i, j))],
        [pl.BlockSpec((tm, 2 * F), lambda i, j: (i, j))], [jax.ShapeDtypeStruct((M, J * 2 * F), BF16)], [],
        (dout, wd, gu), ("parallel", "parallel"), None, jobs)
    return out


def _all_gather(name, shards):
    n = len(shards)

    def body(*refs):
        start, forward, finish = _gather_phases(refs[:n], refs[n:2 * n], *refs[2 * n:])
        start()
        forward()
        finish()

    any_spec = pl.BlockSpec(memory_space=pl.ANY)
    return pl.pallas_call(
        body, name=name, in_specs=[any_spec] * n, out_specs=[any_spec] * n,
        out_shape=[jax.ShapeDtypeStruct((NDEV,) + s.shape, s.dtype) for s in shards],
        scratch_shapes=_gather_sems(n),
    )(*shards)


def _gather_sems(n):
    return [pltpu.SemaphoreType.DMA((n, 7)), pltpu.SemaphoreType.DMA((n, 7)), pltpu.SemaphoreType.DMA((n,))]


def _gather_phases(ins, outs, send_sems, recv_sems, local_sems):
    n = len(ins)
    x, y, c = lax.axis_index("x"), lax.axis_index("y"), lax.axis_index("c")
    me, sibling = (x, y, c), (x, y, 1 - c)
    chips = [(1 - x, y), (x, 1 - y), (1 - x, 1 - y)]

    def blk(i, px, py, pc):
        return outs[i].at[4 * px + 2 * py + pc]

    def copy(i, k, block, to, src=None):
        return pltpu.make_async_remote_copy(
            src_ref=blk(i, *block) if src is None else src, dst_ref=blk(i, *block),
            send_sem=send_sems.at[i, k], recv_sem=recv_sems.at[i, k], device_id=to, device_id_type=MESH)

    def local(i):
        return pltpu.make_async_copy(ins[i], blk(i, *me), local_sems.at[i])

    def firsts(i):
        return [copy(i, 0, me, sibling, src=ins[i])] + [copy(i, 1 + j, me, (*chip, c), src=ins[i])
                                                        for j, chip in enumerate(chips)]

    def start():
        for i in range(n):
            local(i).start()
        for i in range(n):
            for cp in firsts(i):
                cp.start()

    def forward():
        for i in range(n):
            for j, chip in enumerate(chips):
                copy(i, 1 + j, (*chip, c), me).wait_recv()
                copy(i, 4 + j, (*chip, c), sibling).start()

    def finish():
        for i in range(n):
            copy(i, 0, sibling, me).wait_recv()
            for j, chip in enumerate(chips):
                copy(i, 4 + j, (*chip, 1 - c), me).wait_recv()
        for i in range(n):
            for cp in firsts(i):
                cp.wait_send()
            for j, chip in enumerate(chips):
                copy(i, 4 + j, (*chip, c), sibling).wait_send()
            local(i).wait()

    return start, forward, finish


class _GatherRider:
    def __init__(self, shards):
        self.operands = list(shards)
        n = len(self.operands)
        self.n_in = self.n_out = n
        any_spec = pl.BlockSpec(memory_space=pl.ANY)
        self.in_specs = [any_spec] * n
        self.out_specs = [any_spec] * n
        self.out_shape = [jax.ShapeDtypeStruct((NDEV,) + s.shape, s.dtype) for s in self.operands]
        self.scratch = _gather_sems(n)

    def bind(self, step_of, nsteps):
        return self

    def take(self, outs):
        return list(outs)

    def run(self, ins, outs, sems, step, nsteps):
        start, forward, finish = _gather_phases(ins, outs, *sems)
        pl.when(step == 0)(start)
        pl.when(step == (4 * nsteps) // 5)(forward)
        pl.when(step == nsteps - 1)(finish)


class _SwapRider:
    def __init__(self, arr, streams, grid, tile, out_shape, out_block, out_map):
        self.arr, self.streams, self.grid, self.tile = arr, streams, grid, tile
        self.ns, self.n = len(streams), grid[0] * grid[1]
        self.operands = [arr] * (2 * self.ns)
        self.n_in, self.n_out = 2 * self.ns, 1
        self.out_shape = [jax.ShapeDtypeStruct(out_shape, F32)]
        self.out_block, self.out_map = out_block, out_map
        tr, C = tile
        slots = [pltpu.VMEM((2, tr, C), w) for _, w, _, _ in streams]
        self.scratch = slots + slots + [pltpu.SemaphoreType.DMA((self.ns, 2)), pltpu.SemaphoreType.DMA((self.ns, 2)),
                                        pltpu.SemaphoreType.REGULAR((self.ns,))]

    def bind(self, step_of, nsteps):
        assert nsteps >= self.n
        self.period = period = nsteps // self.n
        n, nr = self.n, self.grid[1]

        def ids(*g):
            k = jnp.minimum(step_of(*g) // period, n - 1)
            pos = {a: lax.axis_index(a) for a in AXES}
            return k // nr, k % nr, [v for a in AXES for v in (pos[a], 1 - pos[a])]

        block = (None,) * (self.arr.ndim - 2) + tuple(self.tile)
        self.in_specs = []
        for _, _, keep_map, send_map in self.streams:
            for m in (keep_map, send_map):
                self.in_specs.append(pl.BlockSpec(block, functools.partial(lambda *g, m: m(*ids(*g)), m=m)))
        self.out_specs = [pl.BlockSpec(self.out_block, lambda *g: self.out_map(*ids(*g)))]
        return self

    def take(self, outs):
        return outs[0]

    def run(self, ins, outs, scratch, step, nsteps):
        ns, n, period = self.ns, self.n, self.period
        keeps, sends, o_ref = ins[0::2], ins[1::2], outs[0]
        lands, stages = scratch[:ns], scratch[ns:2 * ns]
        send_sems, recv_sems, credits = scratch[2 * ns:]
        k = step // period
        slot = k % 2
        here = {a: lax.axis_index(a) for a in AXES}
        peers = [tuple(1 - here[a] if a == axis else here[a] for a in AXES) for axis, _, _, _ in self.streams]

        def rdma(s):
            return pltpu.make_async_remote_copy(
                src_ref=stages[s].at[slot], dst_ref=lands[s].at[slot], send_sem=send_sems.at[s, slot],
                recv_sem=recv_sems.at[s, slot], device_id=peers[s], device_id_type=MESH)

        @pl.when((k < n) & (step % period == 0))
        def _():
            @pl.when(k >= 2)
            def _():
                for s in range(ns):
                    pl.semaphore_wait(credits.at[s], 1)

            for s in range(ns):
                stages[s][slot] = sends[s][...].astype(stages[s].dtype)
                rdma(s).start()

        @pl.when((k < n) & (step % period == period - 1))
        def _():
            for s in range(ns):
                rdma(s).wait_recv()
                total = keeps[s][...] + lands[s][slot].astype(F32)
                if ns == 1:
                    o_ref[...] = total
                else:
                    o_ref[s] = total
            for s in range(ns):
                rdma(s).wait_send()

            @pl.when(k + 2 < n)
            def _():
                for s in range(ns):
                    pl.semaphore_signal(credits.at[s], inc=1, device_id=peers[s], device_id_type=MESH)


def _run_alone(name, rider):
    rider.bind(lambda t: t, rider.n)

    def body(*refs):
        a, b = rider.n_in, rider.n_in + rider.n_out
        rider.run(refs[:a], refs[a:b], refs[b:], pl.program_id(0), rider.n)

    outs = pl.pallas_call(
        body, name=name, grid=(rider.n,), in_specs=rider.in_specs, out_specs=rider.out_specs,
        out_shape=rider.out_shape, scratch_shapes=rider.scratch, compiler_params=_cp(("arbitrary",)),
    )(*rider.operands)
    return rider.take(outs)


class _Scatter:
    def __init__(self, name, g):
        self.name, self.cur, self.stage = name, g, 0
        _, self.R, self.C = g.shape

    def done(self):
        return self.stage == 3

    def rider(self, rows):
        R, C = self.R, self.C
        R2 = R // 2
        tr = _tile(R2, rows, 16)
        nrh = R2 // tr
        if self.stage == 0:
            return _SwapRider(
                self.cur.reshape(4, 2, R, C),
                [("c", BF16, lambda b, i, s: (b, s[4], i, 0), lambda b, i, s: (b, s[5], i, 0))],
                (4, 2 * nrh), (tr, C), (2, 4, R2, C), (None, None, tr, C), lambda b, i, s: (i // nrh, b, i % nrh, 0))
        if self.stage == 1:
            return _SwapRider(
                self.cur.reshape(2, 2, 2, R2, C),
                [("y", BF16, lambda b, i, s: (0, b, s[2], i, 0), lambda b, i, s: (0, b, s[3], i, 0)),
                 ("x", BF16, lambda b, i, s: (1, s[0], b, i, 0), lambda b, i, s: (1, s[1], b, i, 0))],
                (2, nrh), (tr, C), (2, 2, R2, C), (2, None, tr, C), lambda b, i, s: (0, b, i, 0))
        return _SwapRider(
            self.cur,
            [("x", BF16, lambda b, i, s: (0, s[0], i, 0), lambda b, i, s: (0, s[1], i, 0)),
             ("y", BF16, lambda b, i, s: (1, s[2], i, 0), lambda b, i, s: (1, s[3], i, 0))],
            (1, nrh), (tr, C), (2, R2, C), (2, tr, C), lambda b, i, s: (0, i, 0))

    def advance(self, out):
        self.cur, self.stage = out, self.stage + 1

    def finish(self):
        while not self.done():
            self.advance(_run_alone(f"{self.name}_s{self.stage}", self.rider(256)))
        return self.cur.reshape(self.R, self.C)


RIDER_TILE_BYTES = 3 * 512 * 1024


def _pick(jobs, nsteps):
    for job in sorted(jobs or (), key=lambda j: -j.R * j.C):
        if job.done():
            continue
        streams = 1 if job.stage == 0 else 2
        riders = [job.rider(rows) for rows in (512, 256, 128, 64)
                  if rows * job.C * 4 * streams <= RIDER_TILE_BYTES or rows == 64]
        for rider in riders:
            if 2 * rider.n <= nsteps:
                return job, rider
        if riders[-1].n <= nsteps:
            return job, riders[-1]
    return None, None


def _sum8(name, g):
    _, R, C = g.shape
    tr = _tile(R, 512)

    def body(g_ref, o_ref):
        acc = g_ref[0]
        for d in range(1, NDEV):
            acc = acc + g_ref[d]
        o_ref[...] = acc

    return pl.pallas_call(
        body, name=name, grid=(R // tr,), in_specs=[pl.BlockSpec((NDEV, tr, C), lambda i: (0, i, 0))],
        out_specs=pl.BlockSpec((tr, C), lambda i: (i, 0)), out_shape=jax.ShapeDtypeStruct((R, C), F32),
        compiler_params=_cp(("parallel",)),
    )(g)


def _adamw_math(w, g, m, v):
    m = B1 * m + (1.0 - B1) * g
    v = B2 * v + (1.0 - B2) * jnp.square(g)
    m_hat = m / (1.0 - B1 ** STEP)
    v_hat = v / (1.0 - B2 ** STEP)
    delta = -LR * (m_hat / (jnp.sqrt(v_hat) + EPS) + WD * w)
    return delta, m, v


def _adamw(name, w, m, v, gp, *, tr, cw, gw, goff=0, jobs=None):
    R, C = w.shape
    nc = C // cw
    nr = R // tr

    def body(w_ref, m_ref, v_ref, g_ref, g_out, d_out, m_out, v_out):
        g = g_ref[...][:, :cw]
        d, mn, vn = _adamw_math(w_ref[...], g, m_ref[...], v_ref[...])
        g_out[...] = g
        d_out[...] = d
        m_out[...] = mn
        v_out[...] = vn

    wspec = pl.BlockSpec((tr, cw), lambda i, j: (i, j))
    gspec = pl.BlockSpec((tr, gw), lambda i, j: (i, goff + j))
    outs, _ = _hosted(name, body, (nr, nc), [wspec, wspec, wspec, gspec], [wspec] * 4,
                      [jax.ShapeDtypeStruct((R, C), F32)] * 4, [], (w, m, v, gp), ("parallel", "parallel"), None, jobs)
    return outs


def _adamw_small(name, w, m, v, g):
    R, C = w.shape

    def body(w_ref, m_ref, v_ref, g_ref, d_out, m_out, v_out):
        d, mn, vn = _adamw_math(w_ref[...], g_ref[...], m_ref[...], v_ref[...])
        d_out[...] = d
        m_out[...] = mn
        v_out[...] = vn

    tr = _tile(R, 512)
    spec = pl.BlockSpec((tr, C), lambda i: (i, 0))
    return pl.pallas_call(
        body, name=name, grid=(R // tr,), in_specs=[spec] * 4, out_specs=[spec] * 3,
        out_shape=[jax.ShapeDtypeStruct((R, C), F32)] * 3, compiler_params=_cp(("parallel",)),
    )(w, m, v, g)


def _prep(name, parts, rows_p, cols_p):
    R, C = parts[0].shape
    n = len(parts)

    def body(*refs):
        o_ref = refs[n]
        if (R, C) != (rows_p, cols_p):
            o_ref[...] = jnp.zeros_like(o_ref)
        for i in range(n):
            o_ref[0:R, i * cols_p:i * cols_p + C] = refs[i][...].astype(BF16)

    return pl.pallas_call(
        body, name=name, out_shape=jax.ShapeDtypeStruct((rows_p, n * cols_p), BF16), compiler_params=_cp(),
    )(*parts)


def _attn_masks():
    lane = lax.broadcasted_iota(jnp.int32, (1, LANE), 1)
    return [(lane < HEAD_DIM), (lane >= HEAD_DIM)]


QH = QB
KW = QB + QH


def _band_valid(base):
    qi = lax.broadcasted_iota(jnp.int32, (QH, KW), 0)
    ki = lax.broadcasted_iota(jnp.int32, (QH, KW), 1)
    dist = qi + QB - ki
    return (dist >= 0) & (dist <= QB) & (base + ki - QB >= 0)


ATTN_T = max(DILATIONS) * QB


def _attn_groups(T):
    out = []
    for d in DILATIONS:
        for r in range(d):
            for l0 in range(0, T // d, QH):
                qrows = pl.ds(r + d * l0, QH, stride=d) if d > 1 else pl.ds(l0, QH)
                k0 = T + r + d * (l0 - QB)
                krows = pl.ds(k0, KW, stride=d) if d > 1 else pl.ds(k0, KW)
                out.append((d, qrows, krows, l0))
    return out


def _attn_specs(T, width_off):
    cur = pl.BlockSpec((T, LANE), lambda hp, b: (b, width_off + hp))
    prev = pl.BlockSpec((T, LANE), lambda hp, b: (jnp.maximum(b - 1, 0), width_off + hp))
    return cur, prev


def _attn_fwd(z):
    S, ZW = z.shape
    T = min(ATTN_T, S)
    scale = HEAD_DIM ** -0.5
    groups = _attn_groups(T)

    def body(q_ref, kc_ref, kp_ref, vc_ref, vp_ref, y_ref, m_ref, l_ref, kcat, vcat):
        b = pl.program_id(1)
        kcat[0:T, :] = kp_ref[...]
        kcat[T:, :] = kc_ref[...]
        vcat[0:T, :] = vp_ref[...]
        vcat[T:, :] = vc_ref[...]
        masks = _attn_masks()
        for d, qrows, krows, l0 in groups:
            q = q_ref[qrows, :]
            kk = kcat[krows, :].astype(BF16)
            vv = vcat[krows, :].astype(BF16)
            valid = _band_valid(b * (T // d) + l0)
            o_new = m_new = l_new = None
            for hm in masks:
                qh = jnp.where(hm, q, 0.0).astype(BF16)
                s = lax.dot_general(qh, kk, (((1,), (1,)), ((), ())), preferred_element_type=F32) * scale
                s = jnp.where(valid, s, MASK_VALUE)
                m = jnp.max(s, axis=-1, keepdims=True)
                p = jnp.exp(s - m)
                l = jnp.sum(p, axis=-1, keepdims=True)
                o = jnp.dot(p.astype(BF16), vv, preferred_element_type=F32)
                if o_new is None:
                    o_new, m_new, l_new = o, jnp.broadcast_to(m, (QH, LANE)), jnp.broadcast_to(l, (QH, LANE))
                else:
                    o_new = jnp.where(hm, o, o_new)
                    m_new = jnp.where(hm, m, m_new)
                    l_new = jnp.where(hm, l, l_new)
            if d == DILATIONS[0]:
                y_ref[qrows, :] = o_new
                m_ref[qrows, :] = m_new
                l_ref[qrows, :] = l_new
            else:
                m_old = m_ref[qrows, :]
                m_all = jnp.maximum(m_old, m_new)
                w_old, w_new = jnp.exp(m_old - m_all), jnp.exp(m_new - m_all)
                y_ref[qrows, :] = w_old * y_ref[qrows, :] + w_new * o_new
                l_ref[qrows, :] = w_old * l_ref[qrows, :] + w_new * l_new
                m_ref[qrows, :] = m_all
        y_ref[...] = y_ref[...] / l_ref[...]

    qc, _ = _attn_specs(T, 0)
    kc, kp = _attn_specs(T, ATTN_W // LANE)
    vc, vp = _attn_specs(T, 2 * ATTN_W // LANE)
    shp = jax.ShapeDtypeStruct((S, ATTN_W), F32)
    return pl.pallas_call(
        body, name="attn_fwd", grid=(ATTN_W // LANE, S // T),
        in_specs=[qc, kc, kp, vc, vp], out_specs=[qc, qc, qc], out_shape=[shp, shp, shp],
        scratch_shapes=[pltpu.VMEM((2 * T, LANE), F32), pltpu.VMEM((2 * T, LANE), F32)],
        compiler_params=_cp(("parallel", "parallel")),
    )(z, z, z, z, z)


def _attn_bwd(z, dya, ya, mg, den):
    S, ZW = z.shape
    T = min(ATTN_T, S)
    scale = HEAD_DIM ** -0.5
    groups = _attn_groups(T)

    def body(q_ref, kc_ref, kp_ref, vc_ref, vp_ref, dy_ref, y_ref, m_ref, n_ref, dq_ref, dk_ref, dv_ref,
             kcat, vcat, dkcat, dvcat):
        b = pl.program_id(1)

        @pl.when(b == 0)
        def _():
            dk_ref[...] = jnp.zeros_like(dk_ref)
            dv_ref[...] = jnp.zeros_like(dv_ref)

        kcat[0:T, :] = kp_ref[...]
        kcat[T:, :] = kc_ref[...]
        vcat[0:T, :] = vp_ref[...]
        vcat[T:, :] = vc_ref[...]
        dkcat[...] = jnp.zeros_like(dkcat)
        dvcat[...] = jnp.zeros_like(dvcat)
        dq_ref[...] = jnp.zeros_like(dq_ref)
        masks = _attn_masks()
        for d, rows, krows, l0 in groups:
            q, dy, y = q_ref[rows, :], dy_ref[rows, :], y_ref[rows, :]
            mrow, nrow = m_ref[rows, :], n_ref[rows, :]
            kk = kcat[krows, :].astype(BF16)
            vv = vcat[krows, :].astype(BF16)
            valid = _band_valid(b * (T // d) + l0)
            dq_acc = jnp.zeros((QH, LANE), F32)
            dk_acc = jnp.zeros((KW, LANE), F32)
            dv_acc = jnp.zeros((KW, LANE), F32)
            for hm in masks:
                qh = jnp.where(hm, q, 0.0).astype(BF16)
                dyh = jnp.where(hm, dy, 0.0)
                dyb = dyh.astype(BF16)
                dsum = jnp.sum(dyh * y, axis=-1, keepdims=True)
                mh = jnp.max(jnp.where(hm, mrow, MASK_VALUE), axis=-1, keepdims=True)
                nh = jnp.max(jnp.where(hm, nrow, 0.0), axis=-1, keepdims=True)
                s = lax.dot_general(qh, kk, (((1,), (1,)), ((), ())), preferred_element_type=F32) * scale
                p = jnp.where(valid, jnp.exp(s - mh), 0.0) / nh
                pb = p.astype(BF16)
                dv_h = lax.dot_general(pb, dyb, (((0,), (0,)), ((), ())), preferred_element_type=F32)
                dp = lax.dot_general(dyb, vv, (((1,), (1,)), ((), ())), preferred_element_type=F32)
                ds = (p * (dp - dsum) * scale).astype(BF16)
                dq_h = jnp.dot(ds, kk, preferred_element_type=F32)
                dk_h = lax.dot_general(ds, qh, (((0,), (0,)), ((), ())), preferred_element_type=F32)
                dq_acc += jnp.where(hm, dq_h, 0.0)
                dk_acc += dk_h
                dv_acc += dv_h
            dq_ref[rows, :] += dq_acc
            dkcat[krows, :] += dk_acc
            dvcat[krows, :] += dv_acc

        base = pl.multiple_of(b * T, T)
        dk_ref[pl.ds(base, T), :] += dkcat[T:, :]
        dv_ref[pl.ds(base, T), :] += dvcat[T:, :]

        @pl.when(b > 0)
        def _():
            prev = pl.multiple_of(b * T - T, T)
            dk_ref[pl.ds(prev, T), :] += dkcat[0:T, :]
            dv_ref[pl.ds(prev, T), :] += dvcat[0:T, :]

    qc, _ = _attn_specs(T, 0)
    kc, kp = _attn_specs(T, ATTN_W // LANE)
    vc, vp = _attn_specs(T, 2 * ATTN_W // LANE)
    whole = pl.BlockSpec((S, LANE), lambda hp, b: (0, hp))
    shp = jax.ShapeDtypeStruct((S, ATTN_W), F32)
    return pl.pallas_call(
        body, name="attn_bwd", grid=(ATTN_W // LANE, S // T),
        in_specs=[qc, kc, kp, vc, vp, qc, qc, qc, qc], out_specs=[qc, whole, whole], out_shape=[shp, shp, shp],
        scratch_shapes=[pltpu.VMEM((2 * T, LANE), F32)] * 4,
        compiler_params=_cp(("parallel", "arbitrary")),
    )(z, z, z, z, z, dya, ya, mg, den)


def _ssm_disc(lr, li, logdt, br, bi):
    dt = jnp.exp(logdt)
    mag = jnp.exp(lr * dt)
    ar = mag * jnp.cos(li * dt)
    ai = mag * jnp.sin(li * dt)
    nr, ni = ar - 1.0, ai
    den = lr * lr + li * li
    cr = (nr * lr + ni * li) / den
    ci = (ni * lr - nr * li) / den
    return ar, ai, cr * br - ci * bi, cr * bi + ci * br


def _ssm_prep(lr, li, logdt, br, bi):
    n, c = br.shape
    outs, _ = _rowwise("ssm_prep", lambda *a: (list(_ssm_disc(*a)), []), n, _tile(n, 512),
                       [_full(a) for a in (lr, li, logdt, br, bi)], [],
                       [(1, 1, _c0, F32), (1, 1, _c0, F32), (c, c, _c0, F32), (c, c, _c0, F32)])
    return outs


def _ssm_prep_bwd(lr, li, logdt, br, bi, dar, dai, dbbr, dbbi):
    n, c = br.shape

    def f(lrb, lib, dtb, brb, bib, *cts):
        _, vjp = jax.vjp(_ssm_disc, lrb, lib, dtb, brb, bib)
        return list(vjp(cts)), []

    outs, _ = _rowwise("ssm_prep_bwd", f, n, _tile(n, 512),
                       [_full(a) for a in (lr, li, logdt, br, bi, dar, dai, dbbr, dbbi)], [],
                       [(1, 1, _c0, F32)] * 3 + [(c, c, _c0, F32)] * 2)
    return outs


def _cmul(ar, ai, br, bi):
    return ar * br - ai * bi, ar * bi + ai * br


def _scan_consts(ar, ai, reverse):
    w = ar.shape[-1]
    a1 = (jnp.broadcast_to(ar, (8, w)), jnp.broadcast_to(ai, (8, w)))
    a2 = _cmul(*a1, *a1)
    a4 = _cmul(*a2, *a2)
    a8 = _cmul(*a4, *a4)
    row = lax.broadcasted_iota(jnp.int32, (8, w), 0)
    e = (8 - row) if reverse else (row + 1)
    one, zero = jnp.ones((8, w), F32), jnp.zeros((8, w), F32)
    pw = (one, zero)
    for bit, ap in ((1, a1), (2, a2), (4, a4), (8, a8)):
        sel = (e & bit) != 0
        nxt = _cmul(*pw, *ap)
        pw = (jnp.where(sel, nxt[0], pw[0]), jnp.where(sel, nxt[1], pw[1]))
    steps = []
    for sh, (pr, pi) in zip((1, 2, 4), (a1, a2, a4)):
        keep = (row < 8 - sh) if reverse else (row >= sh)
        steps.append((jnp.where(keep, pr, 0.0), jnp.where(keep, pi, 0.0)))
    return steps, pw, row


def _scan_group(xr, xi, cr, ci, consts, reverse):
    steps, pw, _ = consts
    for sh, (pr, pi) in zip((1, 2, 4), steps):
        by = 8 - sh if reverse else sh
        tr_, ti_ = _cmul(pr, pi, pltpu.roll(xr, by, 0), pltpu.roll(xi, by, 0))
        xr = xr + tr_
        xi = xi + ti_
    tr_, ti_ = _cmul(pw[0], pw[1], cr, ci)
    return xr + tr_, xi + ti_


def _ssm_fwd(z, a_r, a_i, bdr, bdi, cmr, cmi, dskip, ts, rider=None):
    S, ZW = z.shape
    NS = SSM_G * SSM_P
    PW = PACK * SSM_P
    uoff = (ZW - SSM_W) // LANE
    nsteps = S // ts

    def body(u_ref, ar_ref, ai_ref, bdr_ref, bdi_ref, cmr_ref, cmi_ref, d_ref, hr_ref, hi_ref, y_ref, car_r, car_i):
        s = pl.program_id(1)

        @pl.when(s == 0)
        def _():
            car_r[...] = jnp.zeros_like(car_r)
            car_i[...] = jnp.zeros_like(car_i)

        u = u_ref[...]
        ub = u.astype(BF16)
        nt = (((1,), (1,)), ((), ()))
        hr_ref[...] = lax.dot_general(ub, bdr_ref[...], nt, preferred_element_type=F32)
        hi_ref[...] = lax.dot_general(ub, bdi_ref[...], nt, preferred_element_type=F32)
        consts = _scan_consts(ar_ref[...], ai_ref[...], False)

        def step(j, carry):
            rows = pl.ds(pl.multiple_of(j * 8, 8), 8)
            hr, hi = _scan_group(hr_ref[rows, :], hi_ref[rows, :], carry[0], carry[1], consts, False)
            hr_ref[rows, :] = hr
            hi_ref[rows, :] = hi
            return jnp.broadcast_to(hr[7:8, :], (8, PW)), jnp.broadcast_to(hi[7:8, :], (8, PW))

        cr, ci = lax.fori_loop(0, ts // 8, step, (car_r[...], car_i[...]))
        car_r[...] = cr
        car_i[...] = ci
        y = lax.dot_general(hr_ref[...].astype(BF16), cmr_ref[...], nt, preferred_element_type=F32)
        y -= lax.dot_general(hi_ref[...].astype(BF16), cmi_ref[...], nt, preferred_element_type=F32)
        y_ref[...] = y + d_ref[...] * u

    row_a = pl.BlockSpec((1, PW), lambda i, s: (0, i))
    (hr, hi, y), extra = _hosted(
        "ssm_fwd", body, (SSM_G // PACK, nsteps),
        [pl.BlockSpec((ts, LANE), lambda i, s: (s, uoff + i)), row_a, row_a,
         pl.BlockSpec((None, PW, LANE), lambda i, s: (i, 0, 0)), pl.BlockSpec((None, PW, LANE), lambda i, s: (i, 0, 0)),
         pl.BlockSpec((None, LANE, PW), lambda i, s: (i, 0, 0)), pl.BlockSpec((None, LANE, PW), lambda i, s: (i, 0, 0)),
         pl.BlockSpec((1, LANE), lambda i, s: (0, i))],
        [pl.BlockSpec((ts, PW), lambda i, s: (s, i)), pl.BlockSpec((ts, PW), lambda i, s: (s, i)),
         pl.BlockSpec((ts, LANE), lambda i, s: (s, i))],
        [jax.ShapeDtypeStruct((S, NS), F32), jax.ShapeDtypeStruct((S, NS), F32), jax.ShapeDtypeStruct((S, SSM_W), F32)],
        [pltpu.VMEM((8, PW), F32), pltpu.VMEM((8, PW), F32)], (z, a_r, a_i, bdr, bdi, cmr, cmi, dskip),
        ("parallel", "arbitrary"), rider)
    return hr, hi, y, extra


def _ssm_bwd(z, dyp, hr, hi, a_r, a_i, bdr, bdi, cmr, cmi, dskip, ts, jobs=None):
    S, ZW = z.shape
    NS = SSM_G * SSM_P
    PW = PACK * SSM_P
    uoff = (ZW - SSM_W) // LANE
    nsteps = S // ts
    npk = SSM_G // PACK

    def body(u_ref, dy_ref, hr_ref, hi_ref, hpr_ref, hpi_ref, ar_ref, ai_ref, bdr_ref, bdi_ref, cmr_ref, cmi_ref,
             d_ref, du_ref, dbdr_ref, dbdi_ref, dcmr_ref, dcmi_ref, dar_ref, dai_ref, dd_ref,
             lr_s, li_s, hcr, hci, car_r, car_i):
        s = pl.program_id(1)
        first_tile = s == nsteps - 1

        @pl.when(s == 0)
        def _():
            car_r[...] = jnp.zeros_like(car_r)
            car_i[...] = jnp.zeros_like(car_i)
            for r in (dbdr_ref, dbdi_ref, dcmr_ref, dcmi_ref, dar_ref, dai_ref, dd_ref):
                r[...] = jnp.zeros_like(r)

        u, dy = u_ref[...], dy_ref[...]
        ub, dyb = u.astype(BF16), dy.astype(BF16)
        lr_s[...] = jnp.dot(dyb, cmr_ref[...], preferred_element_type=F32)
        li_s[...] = -jnp.dot(dyb, cmi_ref[...], preferred_element_type=F32)
        keep_prev = jnp.where(first_tile, 0.0, 1.0)
        hcr[0:8, :] = hpr_ref[...] * keep_prev
        hci[0:8, :] = hpi_ref[...] * keep_prev
        hcr[8:, :] = hr_ref[...]
        hci[8:, :] = hi_ref[...]
        consts = _scan_consts(ar_ref[...], -ai_ref[...], True)
        row = consts[2]
        ngrp = ts // 8

        def step(jj, carry):
            cr, ci, accr, acci = carry
            j = ngrp - 1 - jj
            rows = pl.ds(pl.multiple_of(j * 8, 8), 8)
            nxt = pl.ds(pl.multiple_of(j * 8 + 8, 8), 8)
            lr, li = _scan_group(lr_s[rows, :], li_s[rows, :], cr, ci, consts, True)
            lr_s[rows, :] = lr
            li_s[rows, :] = li
            pr, pi = hcr[rows, :], hci[rows, :]
            hsr = jnp.where(row == 0, jnp.broadcast_to(pr[7:8, :], (8, PW)), pltpu.roll(hcr[nxt, :], 1, 0))
            hsi = jnp.where(row == 0, jnp.broadcast_to(pi[7:8, :], (8, PW)), pltpu.roll(hci[nxt, :], 1, 0))
            accr = accr + lr * hsr + li * hsi
            acci = acci + li * hsr - lr * hsi
            return jnp.broadcast_to(lr[0:1, :], (8, PW)), jnp.broadcast_to(li[0:1, :], (8, PW)), accr, acci

        zero = jnp.zeros((8, PW), F32)
        cr, ci, accr, acci = lax.fori_loop(0, ngrp, step, (car_r[...], car_i[...], zero, zero))
        car_r[...] = cr
        car_i[...] = ci
        dar_ref[...] += jnp.sum(accr, axis=0, keepdims=True)
        dai_ref[...] += jnp.sum(acci, axis=0, keepdims=True)
        lrb, lib = lr_s[...].astype(BF16), li_s[...].astype(BF16)
        du = jnp.dot(lrb, bdr_ref[...], preferred_element_type=F32)
        du += jnp.dot(lib, bdi_ref[...], preferred_element_type=F32)
        du_ref[...] = du + dy * d_ref[...]
        tn = (((0,), (0,)), ((), ()))
        dbdr_ref[...] += lax.dot_general(lrb, ub, tn, preferred_element_type=F32)
        dbdi_ref[...] += lax.dot_general(lib, ub, tn, preferred_element_type=F32)
        dcmr_ref[...] += lax.dot_general(dyb, hr_ref[...].astype(BF16), tn, preferred_element_type=F32)
        dcmi_ref[...] -= lax.dot_general(dyb, hi_ref[...].astype(BF16), tn, preferred_element_type=F32)
        dd_ref[...] += jnp.sum(dy * u, axis=0, keepdims=True)

    rev = lambda s: nsteps - 1 - s
    row_a = pl.BlockSpec((1, PW), lambda i, s: (0, i))
    tile = pl.BlockSpec((ts, PW), lambda i, s: (rev(s), i))
    prev8 = pl.BlockSpec((8, PW), lambda i, s: (jnp.maximum(rev(s) * (ts // 8) - 1, 0), i))
    cols = pl.BlockSpec((ts, LANE), lambda i, s: (rev(s), i))
    bd = pl.BlockSpec((None, PW, LANE), lambda i, s: (i, 0, 0))
    cm = pl.BlockSpec((None, LANE, PW), lambda i, s: (i, 0, 0))
    outs, _ = _hosted(
        "ssm_bwd", body, (npk, nsteps),
        [pl.BlockSpec((ts, LANE), lambda i, s: (rev(s), uoff + i)), cols, tile, tile, prev8, prev8,
         row_a, row_a, bd, bd, cm, cm, pl.BlockSpec((1, LANE), lambda i, s: (0, i))],
        [cols, bd, bd, cm, cm, row_a, row_a, pl.BlockSpec((1, LANE), lambda i, s: (0, i))],
        [jax.ShapeDtypeStruct((S, SSM_W), F32),
         jax.ShapeDtypeStruct((npk, PW, LANE), F32), jax.ShapeDtypeStruct((npk, PW, LANE), F32),
         jax.ShapeDtypeStruct((npk, LANE, PW), F32), jax.ShapeDtypeStruct((npk, LANE, PW), F32),
         jax.ShapeDtypeStruct((1, NS), F32), jax.ShapeDtypeStruct((1, NS), F32), jax.ShapeDtypeStruct((1, SSM_W), F32)],
        [pltpu.VMEM((ts, PW), F32), pltpu.VMEM((ts, PW), F32), pltpu.VMEM((ts + 8, PW), F32),
         pltpu.VMEM((ts + 8, PW), F32), pltpu.VMEM((8, PW), F32), pltpu.VMEM((8, PW), F32)],
        (z, dyp, hr, hi, hr, hi, a_r, a_i, bdr, bdi, cmr, cmi, dskip), ("parallel", "arbitrary"), None, jobs)
    return outs


def _block_diag(m4):
    npk, g, a, b = m4.shape
    eye = jnp.eye(g, dtype=m4.dtype)
    return (m4[:, :, :, None, :] * eye[None, :, None, :, None]).reshape(npk, g * a, g * b)


def _block_diag_take(m, a, b):
    npk = m.shape[0]
    m5 = m.reshape(npk, PACK, a, PACK, b)
    return jnp.stack([m5[:, g, :, g, :] for g in range(PACK)], axis=1)


def _mix_out(ya, ypre, gl, ga, gb, bglu):
    yg = jax.nn.gelu(ypre)
    yb = yg * jax.nn.sigmoid(gl + bglu)
    return jnp.concatenate([_rms(ya, ga), _rms(yb, gb)], axis=-1)


def _tail_loss(h3, gl, pe, gf, tgt):
    h4 = h3 + jax.nn.sigmoid(gl) * pe
    err = jnp.square(_rms(h4, gf) - tgt)
    return 0.5 * jnp.mean(err, axis=-1, keepdims=True)


def kernel(x, p, ffn1_norm, ffn1_w_gate, ffn1_w_up, ffn1_w_down, mix_norm, w_in, attn_out_norm, ssm_lambda_re, ssm_lambda_im, ssm_log_dt, ssm_b_re, ssm_b_im, ssm_c_re, ssm_c_im, ssm_d, ssm_w_glu, ssm_b_glu, ssm_out_norm, w_out, ffn2_norm, ffn2_w_gate, ffn2_w_up, ffn2_w_down, ple_norm, ple_w_gate, ple_w_proj, final_norm, loss_target, m_ffn1_norm, m_ffn1_w_gate, m_ffn1_w_up, m_ffn1_w_down, m_mix_norm, m_w_in, m_attn_out_norm, m_ssm_lambda_re, m_ssm_lambda_im, m_ssm_log_dt, m_ssm_b_re, m_ssm_b_im, m_ssm_c_re, m_ssm_c_im, m_ssm_d, m_ssm_w_glu, m_ssm_b_glu, m_ssm_out_norm, m_w_out, m_ffn2_norm, m_ffn2_w_gate, m_ffn2_w_up, m_ffn2_w_down, m_ple_norm, m_ple_w_gate, m_ple_w_proj, m_final_norm, v_ffn1_norm, v_ffn1_w_gate, v_ffn1_w_up, v_ffn1_w_down, v_mix_norm, v_w_in, v_attn_out_norm, v_ssm_lambda_re, v_ssm_lambda_im, v_ssm_log_dt, v_ssm_b_re, v_ssm_b_im, v_ssm_c_re, v_ssm_c_im, v_ssm_d, v_ssm_w_glu, v_ssm_b_glu, v_ssm_out_norm, v_w_out, v_ffn2_norm, v_ffn2_w_gate, v_ffn2_w_up, v_ffn2_w_down, v_ple_norm, v_ple_w_gate, v_ple_w_proj, v_final_norm):
    A = dict(locals())
    xs = x[0]
    ps = p[0, 0]
    tgt = loss_target[0]
    S, D = xs.shape
    FSH = ffn1_w_gate.shape[-1]
    FSP = -(-FSH // LANE) * LANE
    TR = _tile(S, 256)
    ZW = 3 * ATTN_W + SSM_W

    wgu1 = _prep("prep_gu1", [ffn1_w_gate[0], ffn1_w_up[0]], D, FSP)
    wgu2 = _prep("prep_gu2", [ffn2_w_gate[0], ffn2_w_up[0]], D, FSP)
    wd1 = _prep("prep_d1", [ffn1_w_down[0]], FSP, D)
    wd2 = _prep("prep_d2", [ffn2_w_down[0]], FSP, D)
    win = _prep("prep_in", [w_in[0]], D, w_in.shape[-1])
    wglu = _prep("prep_glu", [ssm_w_glu[0]], ssm_w_glu.shape[1], SSM_W)
    wout = _prep("prep_out", [w_out[0]], w_out.shape[1], D)
    wpg = _prep("prep_pg", [ple_w_gate[0]], ple_w_gate.shape[1], D)
    wpp = _prep("prep_pp", [ple_w_proj[0]], ple_w_proj.shape[1], ple_w_proj.shape[2])
    (Wgu1,) = _all_gather("ag_weights", [wgu1])
    rowstack = lambda w: w.reshape(1, w.shape[0] * w.shape[1], w.shape[2])

    def ffn_norm(tag, h, gain):
        return _rowwise(f"{tag}_norm", lambda a, g: ([_rms(a, g)], []), S, TR, [_full(h)], [gain], [(D, D, _c0, BF16)])[0][0]

    xn1 = ffn_norm("ffn1", xs, ffn1_norm)
    gu1, hid1, (Wd1, Win) = _ffn_up("ffn1_up", xn1, Wgu1, rider=_GatherRider([wd1, win]))
    Wd1 = rowstack(Wd1)
    h1, (Wd2,) = _mm_nn("ffn1_down", hid1, Wd1, tn=D // 2, tk=NDEV * FSP, res=xs, scale=0.5,
                        rider=_GatherRider([wd2]))
    Wd2 = rowstack(Wd2)
    un = ffn_norm("mix", h1, mix_norm)
    z, (Wglu, Wout, Wpg, Wpp) = _mm_nn("mix_in", un, Win, tn=512, tk=D,
                                       rider=_GatherRider([wglu, wout, wpg, wpp]))
    Wglu, Wout, Wpg = rowstack(Wglu), rowstack(Wout), rowstack(Wpg)
    ya, mg, den = _attn_fwd(z)

    col = lambda a: a.reshape(-1, 1)
    lr_c, li_c = col(ssm_lambda_re), col(ssm_lambda_im)
    dt_c = col(jnp.broadcast_to(ssm_log_dt.reshape(SSM_G, 1), (SSM_G, SSM_P)))
    b_re2, b_im2 = ssm_b_re.reshape(-1, SSM_C), ssm_b_im.reshape(-1, SSM_C)
    ar_c, ai_c, bbr, bbi = _ssm_prep(lr_c, li_c, dt_c, b_re2, b_im2)
    a_r, a_i = ar_c.reshape(1, -1), ai_c.reshape(1, -1)
    npk = SSM_G // PACK
    bdr = _block_diag(bbr.reshape(npk, PACK, SSM_P, SSM_C)).astype(BF16)
    bdi = _block_diag(bbi.reshape(npk, PACK, SSM_P, SSM_C)).astype(BF16)
    cmr = _block_diag(ssm_c_re.reshape(npk, PACK, SSM_C, SSM_P)).astype(BF16)
    cmi = _block_diag(ssm_c_im.reshape(npk, PACK, SSM_C, SSM_P)).astype(BF16)
    TS = _tile(S, 512)
    hr, hi, ypre, (Wgu2,) = _ssm_fwd(z, a_r, a_i, bdr, bdi, cmr, cmi, ssm_d, TS, rider=_GatherRider([wgu2]))
    (yg,), _ = _rowwise("ssm_gelu", lambda a: ([jax.nn.gelu(a)], []), S, TR, [_full(ypre)], [], [(SSM_W, SSM_W, _c0, BF16)])
    gl = _mm_nn("ssm_glu", yg, Wglu, tn=SSM_W, tk=SSM_W)
    (ycat,), _ = _rowwise("mix_out", lambda *a: ([_mix_out(*a)], []), S, TR, [_full(ya), _full(ypre), _full(gl)],
                          [attn_out_norm, ssm_out_norm, ssm_b_glu], [(MIX_W, MIX_W, _c0, BF16)])
    h2, xn2 = _mm_nn("mix_proj", ycat, Wout, tn=D, tk=D, res=h1, scale=1.0, gain=ffn2_norm)
    gu2, hid2, _ = _ffn_up("ffn2_up", xn2, Wgu2)
    h3 = _mm_nn("ffn2_down", hid2, Wd2, tn=D // 2, tk=NDEV * FSP, res=h2, scale=0.5)
    hn = ffn_norm("ple", h3, ple_norm)
    pgl = _mm_nn("ple_gate", hn, Wpg, tn=D // 2, tk=D)
    pb = ps
    pe = _mm_nn("ple_proj", pb, Wpp, tn=Wpp.shape[2], tk=Wpp.shape[1])

    def tail(h3b, glb, peb, tb, gf):
        rows, vjp = jax.vjp(lambda a, b, c, g: _tail_loss(a, b, c, g, tb), h3b, glb, peb, gf)
        dh, dgl, dpe, dgf = vjp(jnp.ones_like(rows))
        return [dh, dgl, dpe], [jnp.broadcast_to(jnp.sum(rows, axis=0, keepdims=True), (1, LANE)), dgf]

    (dh3_dir, dpgl, dpe), (loss_row, g_final) = _rowwise(
        "tail", tail, S, TR, [_full(h3), _full(pgl), _full(pe), _full(tgt)], [final_norm.reshape(1, D)],
        [(D, D, _c0, F32), (D, D, _c0, BF16), (D, D, _c0, BF16)], [(LANE, LANE, _c0), (D, D, _c0)])
    loss = lax.psum(loss_row[0, 0], AXES)

    def norm_bwd(tag, h, gain, dn, dres):
        def f(hb, dnb, drb, g):
            _, vjp = jax.vjp(_rms, hb, g)
            dh, dg = vjp(dnb)
            dh = dh + drb
            return [dh, dh], [dg]
        (dh, dhb), (dg,) = _rowwise(f"{tag}_norm_bwd", f, S, TR, [_full(h), _full(dn), _full(dres)], [gain],
                                    [(D, D, _c0, F32), (D, D, _c0, BF16)], [(D, D, _c0)])
        return dh, dhb, dg

    restack = lambda g: g.reshape((NDEV, g.shape[1] // NDEV) + g.shape[2:])
    jobs, scat = [], {}

    def scatter(key, g):
        scat[key] = _Scatter("rs_" + key, g)
        jobs.append(scat[key])

    late = []
    dhn = _mm_nt("ple_gate_dx", dpgl, Wpg, tn=D, tk=D)
    late.append(lambda: scatter("pg", restack(_mm_tn("ple_gate_dw", hn, dpgl, 1, jobs=jobs))))
    late.append(lambda: scatter("pp", _mm_tn("ple_proj_dw", pb, dpe, NDEV, jobs=jobs)))
    dh3, dh3b, g_ple_norm = norm_bwd("ple", h3, ple_norm, dhn, dh3_dir)

    def ffn_bwd(tag, h, gain, Wgu, Wd, saved, dout, doutb):
        xn, gu, hid = saved
        dgu = _ffn_down_dx(f"{tag}_down_dx", doutb, Wd, gu, NDEV, scale=0.5, jobs=jobs)
        scatter(tag + "gu", _mm_tn(f"{tag}_up_dw", xn, dgu, NDEV, tn=FSP, jobs=jobs))
        scatter(tag + "d", restack(_mm_tn(f"{tag}_down_dw", hid, doutb, 1, tko=FSP, tn=256, scale=0.5, jobs=jobs)))
        dxn = _mm_nt(f"{tag}_up_dx", dgu, Wgu, tm=512, tn=D, tk=2 * FSP, jobs=jobs)
        dh, dhb, g_norm = norm_bwd(tag, h, gain, dxn, dout)
        return dh, dhb, g_norm

    dh2, dh2b, g_ffn2_norm = ffn_bwd("ffn2", h2, ffn2_norm, Wgu2, Wd2, (xn2, gu2, hid2), dh3, dh3b)

    dycat = _mm_nt("mix_proj_dx", dh2b, Wout, tn=D, tk=D, jobs=jobs)
    late.append(lambda: scatter("out", restack(_mm_tn("mix_proj_dw", ycat, dh2b, 1, jobs=jobs))))

    def mix_out_bwd(yab, ypb, glb, dyc, ga, gb, bglu):
        _, vjp = jax.vjp(_mix_out, yab, ypb, glb, ga, gb, bglu)
        dya_, dyp_, dgl_, dga, dgb, dbg = vjp(dyc)
        return [dya_, dyp_, dgl_], [dga, dgb, dbg]
    (dya, dyp_dir, dglb), (g_attn_norm, g_ssm_norm, g_bglu) = _rowwise(
        "mix_out_bwd", mix_out_bwd, S, TR, [_full(ya), _full(ypre), _full(gl), _full(dycat)],
        [attn_out_norm, ssm_out_norm, ssm_b_glu],
        [(ATTN_W, ATTN_W, _c0, F32), (SSM_W, SSM_W, _c0, F32), (SSM_W, SSM_W, _c0, BF16)],
        [(ATTN_W, ATTN_W, _c0), (SSM_W, SSM_W, _c0), (SSM_W, SSM_W, _c0)])
    dyg = _mm_nt("ssm_glu_dx", dglb, Wglu, tn=SSM_W, tk=SSM_W, jobs=jobs)
    late.append(lambda: scatter("glu", restack(_mm_tn("ssm_glu_dw", yg, dglb, 1, jobs=jobs))))

    def gelu_bwd(ypb, dygb, ddir):
        _, vjp = jax.vjp(jax.nn.gelu, ypb)
        return [ddir + vjp(dygb)[0]], []
    (dyp,), _ = _rowwise("ssm_gelu_bwd", gelu_bwd, S, TR, [_full(ypre), _full(dyg), _full(dyp_dir)], [],
                         [(SSM_W, SSM_W, _c0, F32)])
    du, dbdr, dbdi, dcmr, dcmi, da_r, da_i, g_ssm_d = _ssm_bwd(z, dyp, hr, hi, a_r, a_i, bdr, bdi, cmr, cmi, ssm_d, TS,
                                                              jobs=jobs)
    dbbr = _block_diag_take(dbdr, SSM_P, SSM_C).reshape(-1, SSM_C)
    dbbi = _block_diag_take(dbdi, SSM_P, SSM_C).reshape(-1, SSM_C)
    g_c_re = _block_diag_take(dcmr, SSM_C, SSM_P).reshape(ssm_c_re.shape)
    g_c_im = _block_diag_take(dcmi, SSM_C, SSM_P).reshape(ssm_c_im.shape)
    dlr, dli, ddt, g_b_re, g_b_im = _ssm_prep_bwd(lr_c, li_c, dt_c, b_re2, b_im2, col(da_r), col(da_i), dbbr, dbbi)
    g_lam_re, g_lam_im = dlr.reshape(ssm_lambda_re.shape), dli.reshape(ssm_lambda_im.shape)
    g_log_dt = jnp.sum(ddt.reshape(SSM_G, SSM_P), axis=1).reshape(ssm_log_dt.shape)
    g_b_re, g_b_im = g_b_re.reshape(ssm_b_re.shape), g_b_im.reshape(ssm_b_im.shape)

    dq, dk, dv = _attn_bwd(z, dya, ya, mg, den)
    (dz,), _ = _rowwise("mix_dz", lambda *a: ([jnp.concatenate(a, axis=-1)], []), S, TR,
                        [_full(dq), _full(dk), _full(dv), _full(du)], [], [(ZW, ZW, _c0, BF16)])
    dun = _mm_nt("mix_in_dx", dz, Win, tn=D, tk=512, jobs=jobs)
    scatter("in", _mm_tn("mix_in_dw", un, dz, NDEV, tn=512, jobs=jobs))
    dh1, dh1b, g_mix_norm = norm_bwd("mix", h1, mix_norm, dun, dh2)

    dx, _dxb, g_ffn1_norm = ffn_bwd("ffn1", xs, ffn1_norm, Wgu1, Wd1, (xn1, gu1, hid1), dh1, dh1b)
    for run in (late[2], late[0], late[3], late[1]):
        run()

    out = {}

    def upd(name, key, *, tr, cw, gw, goff=0):
        w, m, v = A[name][0], A["m_" + name][0], A["v_" + name][0]
        g, dlt, mn, vn = _adamw("adamw_" + name, w, m, v, scat[key].finish(), tr=tr, cw=cw, gw=gw, goff=goff, jobs=jobs)
        for k, val in (("grad_", g), ("delta_", dlt), ("new_m_", mn), ("new_v_", vn)):
            out[k + name] = val[None]

    DT = _tile(D, 256)
    FT = _tile(FSH, 512)
    DC = _tile(D, 1024, LANE)
    upd("ffn2_w_gate", "ffn2gu", tr=DT, cw=FSH, gw=FSP, goff=0)
    upd("ffn2_w_up", "ffn2gu", tr=DT, cw=FSH, gw=FSP, goff=1)
    upd("ffn2_w_down", "ffn2d", tr=FT, cw=DC, gw=DC)
    upd("w_in", "in", tr=DT, cw=w_in.shape[-1], gw=w_in.shape[-1])
    upd("ffn1_w_gate", "ffn1gu", tr=DT, cw=FSH, gw=FSP, goff=0)
    upd("ffn1_w_up", "ffn1gu", tr=DT, cw=FSH, gw=FSP, goff=1)
    upd("ffn1_w_down", "ffn1d", tr=FT, cw=DC, gw=DC)
    upd("w_out", "out", tr=w_out.shape[1], cw=DC, gw=DC)
    upd("ple_w_gate", "pg", tr=ple_w_gate.shape[1], cw=DC, gw=DC)
    upd("ssm_w_glu", "glu", tr=ssm_w_glu.shape[1], cw=SSM_W, gw=SSM_W)
    upd("ple_w_proj", "pp", tr=ple_w_proj.shape[1], cw=ple_w_proj.shape[2], gw=ple_w_proj.shape[2])

    small = [("ffn1_norm", g_ffn1_norm), ("mix_norm", g_mix_norm), ("attn_out_norm", g_attn_norm),
             ("ssm_lambda_re", g_lam_re), ("ssm_lambda_im", g_lam_im), ("ssm_log_dt", g_log_dt),
             ("ssm_b_re", g_b_re), ("ssm_b_im", g_b_im), ("ssm_c_re", g_c_re), ("ssm_c_im", g_c_im),
             ("ssm_d", g_ssm_d), ("ssm_b_glu", g_bglu), ("ssm_out_norm", g_ssm_norm), ("ffn2_norm", g_ffn2_norm),
             ("ple_norm", g_ple_norm), ("final_norm", g_final)]
    chunk = 8 * LANE

    def pack(arrs):
        parts = []
        for a in arrs:
            flat = a.reshape(-1)
            padn = -(-flat.shape[0] // chunk) * chunk
            parts.append(jnp.pad(flat, (0, padn - flat.shape[0])).reshape(-1, LANE))
        return jnp.concatenate(parts, axis=0)

    g_pack = pack([g for _, g in small])
    (g_all,) = _all_gather("ag_small", [g_pack])
    g_sum = _sum8("small_sum", g_all)
    w_pack = pack([A[n] for n, _ in small])
    m_pack = pack([A["m_" + n] for n, _ in small])
    v_pack = pack([A["v_" + n] for n, _ in small])
    d_pack, mn_pack, vn_pack = _adamw_small("adamw_small", w_pack, m_pack, v_pack, g_sum)
    off = 0
    for n, _ in small:
        shape = A[n].shape
        size = math.prod(shape)
        rows = -(-size // chunk) * 8
        for k, buf in (("grad_", g_sum), ("delta_", d_pack), ("new_m_", mn_pack), ("new_v_", vn_pack)):
            out[k + n] = buf[off:off + rows].reshape(-1)[:size].reshape(shape)
        off += rows

    names = ['ffn1_norm', 'ffn1_w_gate', 'ffn1_w_up', 'ffn1_w_down', 'mix_norm', 'w_in', 'attn_out_norm',
             'ssm_lambda_re', 'ssm_lambda_im', 'ssm_log_dt', 'ssm_b_re', 'ssm_b_im', 'ssm_c_re', 'ssm_c_im', 'ssm_d',
             'ssm_w_glu', 'ssm_b_glu', 'ssm_out_norm', 'w_out', 'ffn2_norm', 'ffn2_w_gate', 'ffn2_w_up', 'ffn2_w_down',
             'ple_norm', 'ple_w_gate', 'ple_w_proj', 'final_norm']
    return (loss, dx[None], *[out[k + n] for k in ("grad_", "delta_", "new_m_", "new_v_") for n in names])
```

```python
import functools
import math

import jax
import jax.numpy as jnp
from jax import lax
from jax.experimental import pallas as pl
from jax.experimental.pallas import tpu as pltpu

F32, BF16 = jnp.float32, jnp.bfloat16
MESH = pl.DeviceIdType.MESH
NDEV = 8
AXES = ("x", "y", "c")
LANE = 128
VMEM_LIMIT = 56 * 1024 * 1024

ATTN_W = 1024
HEAD_DIM = 64
SSM_W = 1024
MIX_W = ATTN_W + SSM_W
SSM_G, SSM_P, SSM_C = 64, 64, 16
PACK = 8
DILATIONS = (1, 4, 16)
QB = 128
NORM_EPS = 1e-6
MASK_VALUE = -1e30
LR, B1, B2, EPS, WD, STEP = 0.001, 0.9, 0.999, 1e-08, 0.01, 10


def _cp(sem=None):
    return pltpu.CompilerParams(dimension_semantics=sem, vmem_limit_bytes=VMEM_LIMIT)


def _tile(n, target, mult=8):
    if n <= target:
        return n
    for t in range(target - target % mult, 0, -mult):
        if n % t == 0:
            return t
    return n


def _rms(x, g):
    return x * lax.rsqrt(jnp.mean(x * x, axis=-1, keepdims=True) + NORM_EPS) * g


def _rowwise(name, fn, S, tr, rows, fulls, outs, accs=(), ncol=1):
    nr, nf, no, na = len(rows), len(fulls), len(outs), len(accs)

    def body(*refs):
        ins = [r[...] for r in refs[:nr + nf]]
        o_refs = refs[nr + nf:nr + nf + no]
        a_refs = refs[nr + nf + no:]
        o_vals, a_vals = fn(*ins)
        for r, v in zip(o_refs, o_vals):
            r[...] = v.astype(r.dtype)
        if na:
            @pl.when(pl.program_id(1) == 0)
            def _():
                for r in a_refs:
                    r[...] = jnp.zeros_like(r)
            for r, v in zip(a_refs, a_vals):
                r[...] += v

    in_specs = [pl.BlockSpec((tr, w), functools.partial(lambda j, i, cm: (i, cm(j)), cm=cm)) for _, w, cm in rows]
    in_specs += [pl.BlockSpec(f.shape, functools.partial(lambda j, i, nd: (0,) * nd, nd=f.ndim)) for f in fulls]
    out_specs = [pl.BlockSpec((tr, w), functools.partial(lambda j, i, cm: (i, cm(j)), cm=cm)) for _, w, cm, _ in outs]
    out_specs += [pl.BlockSpec((1, w), functools.partial(lambda j, i, cm: (0, cm(j)), cm=cm)) for _, w, cm in accs]
    out_shape = [jax.ShapeDtypeStruct((S, c), dt) for c, _, _, dt in outs]
    out_shape += [jax.ShapeDtypeStruct((1, c), F32) for c, _, _ in accs]
    res = pl.pallas_call(
        body, name=name, grid=(ncol, S // tr), in_specs=in_specs, out_specs=out_specs, out_shape=out_shape,
        compiler_params=_cp(("parallel", "arbitrary" if na else "parallel")),
    )(*[a for a, _, _ in rows], *fulls)
    return res[:no], res[no:]


def _c0(j):
    return 0


def _full(a):
    return (a, a.shape[1], _c0)


def _hosted(name, body, grid, in_specs, out_specs, out_shape, scratch, args, sem, rider=None, jobs=None):
    nsteps = math.prod(grid)

    def step_of(*g):
        t = 0
        for gi, n in zip(g, grid):
            t = t * n + gi
        return t

    job = None
    if rider is None:
        job, rider = _pick(jobs, nsteps)
    if rider is None:
        outs = pl.pallas_call(body, name=name, grid=grid, in_specs=in_specs, out_specs=out_specs, out_shape=out_shape,
                              scratch_shapes=scratch, compiler_params=_cp(sem))(*args)
        return outs, None
    rider.bind(step_of, nsteps)
    n_in, n_out, n_scr = len(in_specs), len(out_specs), len(scratch)

    def full(*refs):
        a, b = n_in, n_in + rider.n_in
        c, d = b + n_out, b + n_out + rider.n_out
        rider.run(refs[a:b], refs[c:d], refs[d + n_scr:], step_of(*[pl.program_id(i) for i in range(len(grid))]), nsteps)
        body(*(refs[:a] + refs[b:c] + refs[d:d + n_scr]))

    outs = pl.pallas_call(
        full, name=name, grid=grid, in_specs=in_specs + rider.in_specs, out_specs=out_specs + rider.out_specs,
        out_shape=out_shape + rider.out_shape, scratch_shapes=scratch + rider.scratch,
        compiler_params=_cp(("arbitrary",) * len(grid)))(*args, *rider.operands)
    extra = rider.take(outs[n_out:])
    if job is not None:
        job.advance(extra)
        extra = None
    return outs[:n_out], extra


def _mm_nn(name, a, w, *, out_dtype=F32, tm=512, tn=768, tk=2048, res=None, scale=1.0, gain=None, rider=None,
           jobs=None):
    M, K = a.shape
    J, K2, Np = w.shape
    assert K == K2
    tm, tn, tk = _tile(M, tm), _tile(Np, tn, LANE), _tile(K, tk, LANE)
    npj = Np // tn
    nk = K // tk
    grid = (M // tm, J * npj, nk)
    assert gain is None or (J * npj == 1 and res is not None)

    def body(*refs):
        refs = list(refs)
        a_ref, w_ref = refs[:2]
        r_ref = refs[2] if res is not None else None
        g_ref = refs[3] if gain is not None else None
        acc = refs[-1]
        o_ref = refs[-3] if gain is not None else refs[-2]
        k = pl.program_id(2)
        part = jnp.dot(a_ref[...].astype(BF16), w_ref[...], preferred_element_type=F32)

        def finish(v):
            if res is not None:
                v = r_ref[...] + scale * v
            o_ref[...] = v.astype(o_ref.dtype)
            if gain is not None:
                refs[-2][...] = _rms(v, g_ref[...]).astype(BF16)

        if nk == 1:
            finish(part)
            return

        @pl.when(k == 0)
        def _():
            acc[...] = part

        @pl.when(k > 0)
        def _():
            acc[...] += part

        @pl.when(k == nk - 1)
        def _():
            finish(acc[...])

    in_specs = [pl.BlockSpec((tm, tk), lambda i, n, k: (i, k)),
                pl.BlockSpec((None, tk, tn), lambda i, n, k: (n // npj, k, n % npj))]
    args = [a, w]
    if res is not None:
        in_specs.append(pl.BlockSpec((tm, tn), lambda i, n, k: (i, n)))
        args.append(res)
    out_specs = [pl.BlockSpec((tm, tn), lambda i, n, k: (i, n))]
    out_shape = [jax.ShapeDtypeStruct((M, J * Np), out_dtype)]
    if gain is not None:
        in_specs.append(pl.BlockSpec((1, tn), lambda i, n, k: (0, 0)))
        args.append(gain)
        out_specs.append(pl.BlockSpec((tm, tn), lambda i, n, k: (i, n)))
        out_shape.append(jax.ShapeDtypeStruct((M, J * Np), BF16))
    outs, extra = _hosted(name, body, grid, in_specs, out_specs, out_shape, [pltpu.VMEM((tm, tn), F32)], args,
                          ("parallel", "parallel", "arbitrary"), rider, jobs)
    outs = tuple(outs) + (() if rider is None else (extra,))
    return outs[0] if len(outs) == 1 else outs


def _mm_nt(name, dy, w, *, out_dtype=F32, tm=512, tn=2048, tk=768, jb=1, scale=1.0, jobs=None):
    M, N = dy.shape
    J, K, Np = w.shape
    assert N == J * Np
    tm, tn, tk = _tile(M, tm), _tile(K, tn, LANE), (Np if jb > 1 else _tile(Np, tk, LANE))
    npj = Np // tk
    nc = J * npj // jb
    nt = (((1,), (1,)), ((), ()))

    def body(a_ref, w_ref, o_ref, acc):
        c = pl.program_id(2)
        if jb == 1:
            part = lax.dot_general(a_ref[...].astype(BF16), w_ref[...], nt, preferred_element_type=F32)
        else:
            part = sum(lax.dot_general(a_ref[:, j * Np:(j + 1) * Np].astype(BF16), w_ref[j], nt,
                                       preferred_element_type=F32) for j in range(jb))
        if nc == 1:
            o_ref[...] = (scale * part).astype(o_ref.dtype)
            return

        @pl.when(c == 0)
        def _():
            acc[...] = part

        @pl.when(c > 0)
        def _():
            acc[...] += part

        @pl.when(c == nc - 1)
        def _():
            o_ref[...] = (scale * acc[...]).astype(o_ref.dtype)

    w_spec = (pl.BlockSpec((None, tn, tk), lambda i, n, c: (c // npj, n, c % npj)) if jb == 1 else
              pl.BlockSpec((jb, tn, Np), lambda i, n, c: (c, n, 0)))
    (out,), _ = _hosted(
        name, body, (M // tm, K // tn, nc), [pl.BlockSpec((tm, jb * tk), lambda i, n, c: (i, c)), w_spec],
        [pl.BlockSpec((tm, tn), lambda i, n, c: (i, n))], [jax.ShapeDtypeStruct((M, K), out_dtype)],
        [pltpu.VMEM((tm, tn) if nc > 1 else (8, LANE), F32)], (dy, w), ("parallel", "parallel", "arbitrary"), None, jobs)
    return out


def _mm_tn(name, x, dy, J, *, tm=8192, tko=256, tn=512, scale=1.0, jobs=None):
    M, K = x.shape
    M2, N = dy.shape
    assert M == M2 and N % J == 0
    Np = N // J
    tm, tko, tn = _tile(M, tm, LANE), _tile(K, tko, LANE), _tile(Np, tn, LANE)
    npj = Np // tn
    nm = M // tm

    def body(x_ref, d_ref, o_ref, acc):
        m = pl.program_id(2)
        part = lax.dot_general(x_ref[...].astype(BF16), d_ref[...].astype(BF16), (((0,), (0,)), ((), ())),
                               preferred_element_type=F32)
        if nm == 1:
            o_ref[...] = scale * part
            return

        @pl.when(m == 0)
        def _():
            acc[...] = part

        @pl.when(m > 0)
        def _():
            acc[...] += part

        @pl.when(m == nm - 1)
        def _():
            o_ref[...] = scale * acc[...]

    nk, nn = K // tko, J * npj
    k_outer = x.size + nk * dy.size <= dy.size + nn * x.size
    kn = (lambda a, b: (a, b)) if k_outer else (lambda a, b: (b, a))
    (out,), _ = _hosted(
        name, body, (nk, nn, nm) if k_outer else (nn, nk, nm),
        [pl.BlockSpec((tm, tko), lambda a, b, m: (m, kn(a, b)[0])),
         pl.BlockSpec((tm, tn), lambda a, b, m: (m, kn(a, b)[1]))],
        [pl.BlockSpec((None, tko, tn), lambda a, b, m: (kn(a, b)[1] // npj, kn(a, b)[0], kn(a, b)[1] % npj))],
        [jax.ShapeDtypeStruct((J, K, Np), F32)], [pltpu.VMEM((tko, tn) if nm > 1 else (8, LANE), F32)], (x, dy),
        ("parallel", "parallel", "arbitrary"), None, jobs)
    return out


def _swiglu_act(g, u):
    return jax.nn.silu(g) * u


def _ffn_up(name, xn, wgu, *, tm=1024, rider=None):
    M, K = xn.shape
    J, _, F2 = wgu.shape
    F = F2 // 2
    tm = _tile(M, tm)

    def body(a_ref, w_ref, gu_ref, h_ref):
        r = jnp.dot(a_ref[...], w_ref[...], preferred_element_type=F32)
        gu_ref[...] = r.astype(gu_ref.dtype)
        h_ref[...] = _swiglu_act(r[:, :F], r[:, F:]).astype(h_ref.dtype)

    (gu, hid), extra = _hosted(
        name, body, (M // tm, J),
        [pl.BlockSpec((tm, K), lambda i, j: (i, 0)), pl.BlockSpec((None, K, F2), lambda i, j: (j, 0, 0))],
        [pl.BlockSpec((tm, F2), lambda i, j: (i, j)), pl.BlockSpec((tm, F), lambda i, j: (i, j))],
        [jax.ShapeDtypeStruct((M, J * F2), BF16), jax.ShapeDtypeStruct((M, J * F), BF16)], [], (xn, wgu),
        ("parallel", "parallel"), rider)
    return gu, hid, extra


def _ffn_down_dx(name, dout, wd, gu, J, *, scale, tm=512, jobs=None):
    M, D = dout.shape
    F = wd.shape[1] // J
    tm = _tile(M, tm)

    def body(d_ref, w_ref, gu_ref, o_ref):
        dh = scale * lax.dot_general(d_ref[...], w_ref[...], (((1,), (1,)), ((), ())), preferred_element_type=F32)
        gu = gu_ref[...].astype(F32)
        _, vjp = jax.vjp(_swiglu_act, gu[:, :F], gu[:, F:])
        o_ref[...] = jnp.concatenate(vjp(dh), axis=-1).astype(o_ref.dtype)

    (out,), _ = _hosted(
        name, body, (M // tm, J),
        [pl.BlockSpec((tm, D), lambda i, j: (i, 0)), pl.BlockSpec((None, F, D), lambda i, j: (0, j, 0)),
         pl.BlockSpec((tm, 2 * F), lambda i, j: (i, j))],
        [pl.BlockSpec((tm, 2 * F), lambda i, j: (i, j))], [jax.ShapeDtypeStruct((M, J * 2 * F), BF16)], [],
        (dout, wd, gu), ("parallel", "parallel"), None, jobs)
    return out


def _all_gather(name, shards):
    n = len(shards)

    def body(*refs):
        start, forward, finish = _gather_phases(refs[:n], refs[n:2 * n], *refs[2 * n:])
        start()
        forward()
        finish()

    any_spec = pl.BlockSpec(memory_space=pl.ANY)
    return pl.pallas_call(
        body, name=name, in_specs=[any_spec] * n, out_specs=[any_spec] * n,
        out_shape=[jax.ShapeDtypeStruct((NDEV,) + s.shape, s.dtype) for s in shards],
        scratch_shapes=_gather_sems(n),
    )(*shards)


def _gather_sems(n):
    return [pltpu.SemaphoreType.DMA((n, 7)), pltpu.SemaphoreType.DMA((n, 7)), pltpu.SemaphoreType.DMA((n,))]


def _gather_phases(ins, outs, send_sems, recv_sems, local_sems):
    n = len(ins)
    x, y, c = lax.axis_index("x"), lax.axis_index("y"), lax.axis_index("c")
    me, sibling = (x, y, c), (x, y, 1 - c)
    chips = [(1 - x, y), (x, 1 - y), (1 - x, 1 - y)]

    def blk(i, px, py, pc):
        return outs[i].at[4 * px + 2 * py + pc]

    def copy(i, k, block, to, src=None):
        return pltpu.make_async_remote_copy(
            src_ref=blk(i, *block) if src is None else src, dst_ref=blk(i, *block),
            send_sem=send_sems.at[i, k], recv_sem=recv_sems.at[i, k], device_id=to, device_id_type=MESH)

    def local(i):
        return pltpu.make_async_copy(ins[i], blk(i, *me), local_sems.at[i])

    def firsts(i):
        return [copy(i, 0, me, sibling, src=ins[i])] + [copy(i, 1 + j, me, (*chip, c), src=ins[i])
                                                        for j, chip in enumerate(chips)]

    def start():
        for i in range(n):
            local(i).start()
        for i in range(n):
            for cp in firsts(i):
                cp.start()

    def forward():
        for i in range(n):
            for j, chip in enumerate(chips):
                copy(i, 1 + j, (*chip, c), me).wait_recv()
                copy(i, 4 + j, (*chip, c), sibling).start()

    def finish():
        for i in range(n):
            copy(i, 0, sibling, me).wait_recv()
            for j, chip in enumerate(chips):
                copy(i, 4 + j, (*chip, 1 - c), me).wait_recv()
        for i in range(n):
            for cp in firsts(i):
                cp.wait_send()
            for j, chip in enumerate(chips):
                copy(i, 4 + j, (*chip, c), sibling).wait_send()
            local(i).wait()

    return start, forward, finish


class _GatherRider:
    def __init__(self, shards):
        self.operands = list(shards)
        n = len(self.operands)
        self.n_in = self.n_out = n
        any_spec = pl.BlockSpec(memory_space=pl.ANY)
        self.in_specs = [any_spec] * n
        self.out_specs = [any_spec] * n
        self.out_shape = [jax.ShapeDtypeStruct((NDEV,) + s.shape, s.dtype) for s in self.operands]
        self.scratch = _gather_sems(n)

    def bind(self, step_of, nsteps):
        return self

    def take(self, outs):
        return list(outs)

    def run(self, ins, outs, sems, step, nsteps):
        start, forward, finish = _gather_phases(ins, outs, *sems)
        pl.when(step == 0)(start)
        pl.when(step == (4 * nsteps) // 5)(forward)
        pl.when(step == nsteps - 1)(finish)


class _SwapRider:
    def __init__(self, arr, streams, grid, tile, out_shape, out_block, out_map):
        self.arr, self.streams, self.grid, self.tile = arr, streams, grid, tile
        self.ns, self.n = len(streams), grid[0] * grid[1]
        self.operands = [arr] * (2 * self.ns)
        self.n_in, self.n_out = 2 * self.ns, 1
        self.out_shape = [jax.ShapeDtypeStruct(out_shape, F32)]
        self.out_block, self.out_map = out_block, out_map
        tr, C = tile
        slots = [pltpu.VMEM((2, tr, C), w) for _, w, _, _ in streams]
        self.scratch = slots + slots + [pltpu.SemaphoreType.DMA((self.ns, 2)), pltpu.SemaphoreType.DMA((self.ns, 2)),
                                        pltpu.SemaphoreType.REGULAR((self.ns,))]

    def bind(self, step_of, nsteps):
        assert nsteps >= self.n
        self.period = period = nsteps // self.n
        n, nr = self.n, self.grid[1]

        def ids(*g):
            k = jnp.minimum(step_of(*g) // period, n - 1)
            pos = {a: lax.axis_index(a) for a in AXES}
            return k // nr, k % nr, [v for a in AXES for v in (pos[a], 1 - pos[a])]

        block = (None,) * (self.arr.ndim - 2) + tuple(self.tile)
        self.in_specs = []
        for _, _, keep_map, send_map in self.streams:
            for m in (keep_map, send_map):
                self.in_specs.append(pl.BlockSpec(block, functools.partial(lambda *g, m: m(*ids(*g)), m=m)))
        self.out_specs = [pl.BlockSpec(self.out_block, lambda *g: self.out_map(*ids(*g)))]
        return self

    def take(self, outs):
        return outs[0]

    def run(self, ins, outs, scratch, step, nsteps):
        ns, n, period = self.ns, self.n, self.period
        keeps, sends, o_ref = ins[0::2], ins[1::2], outs[0]
        lands, stages = scratch[:ns], scratch[ns:2 * ns]
        send_sems, recv_sems, credits = scratch[2 * ns:]
        k = step // period
        slot = k % 2
        here = {a: lax.axis_index(a) for a in AXES}
        peers = [tuple(1 - here[a] if a == axis else here[a] for a in AXES) for axis, _, _, _ in self.streams]

        def rdma(s):
            return pltpu.make_async_remote_copy(
                src_ref=stages[s].at[slot], dst_ref=lands[s].at[slot], send_sem=send_sems.at[s, slot],
                recv_sem=recv_sems.at[s, slot], device_id=peers[s], device_id_type=MESH)

        @pl.when((k < n) & (step % period == 0))
        def _():
            @pl.when(k >= 2)
            def _():
                for s in range(ns):
                    pl.semaphore_wait(credits.at[s], 1)

            for s in range(ns):
                stages[s][slot] = sends[s][...].astype(stages[s].dtype)
                rdma(s).start()

        @pl.when((k < n) & (step % period == period - 1))
        def _():
            for s in range(ns):
                rdma(s).wait_recv()
                total = keeps[s][...] + lands[s][slot].astype(F32)
                if ns == 1:
                    o_ref[...] = total
                else:
                    o_ref[s] = total
            for s in range(ns):
                rdma(s).wait_send()

            @pl.when(k + 2 < n)
            def _():
                for s in range(ns):
                    pl.semaphore_signal(credits.at[s], inc=1, device_id=peers[s], device_id_type=MESH)


def _run_alone(name, rider):
    rider.bind(lambda t: t, rider.n)

    def body(*refs):
        a, b = rider.n_in, rider.n_in + rider.n_out
        rider.run(refs[:a], refs[a:b], refs[b:], pl.program_id(0), rider.n)

    outs = pl.pallas_call(
        body, name=name, grid=(rider.n,), in_specs=rider.in_specs, out_specs=rider.out_specs,
        out_shape=rider.out_shape, scratch_shapes=rider.scratch, compiler_params=_cp(("arbitrary",)),
    )(*rider.operands)
    return rider.take(outs)


class _Scatter:
    def __init__(self, name, g):
        self.name, self.cur, self.stage = name, g, 0
        _, self.R, self.C = g.shape

    def done(self):
        return self.stage == 3

    def rider(self, rows):
        R, C = self.R, self.C
        R2 = R // 2
        tr = _tile(R2, rows, 16)
        nrh = R2 // tr
        if self.stage == 0:
            return _SwapRider(
                self.cur.reshape(4, 2, R, C),
                [("c", BF16, lambda b, i, s: (b, s[4], i, 0), lambda b, i, s: (b, s[5], i, 0))],
                (4, 2 * nrh), (tr, C), (2, 4, R2, C), (None, None, tr, C), lambda b, i, s: (i // nrh, b, i % nrh, 0))
        if self.stage == 1:
            return _SwapRider(
                self.cur.reshape(2, 2, 2, R2, C),
                [("y", BF16, lambda b, i, s: (0, b, s[2], i, 0), lambda b, i, s: (0, b, s[3], i, 0)),
                 ("x", BF16, lambda b, i, s: (1, s[0], b, i, 0), lambda b, i, s: (1, s[1], b, i, 0))],
                (2, nrh), (tr, C), (2, 2, R2, C), (2, None, tr, C), lambda b, i, s: (0, b, i, 0))
        return _SwapRider(
            self.cur,
            [("x", BF16, lambda b, i, s: (0, s[0], i, 0), lambda b, i, s: (0, s[1], i, 0)),
             ("y", BF16, lambda b, i, s: (1, s[2], i, 0), lambda b, i, s: (1, s[3], i, 0))],
            (1, nrh), (tr, C), (2, R2, C), (2, tr, C), lambda b, i, s: (0, i, 0))

    def advance(self, out):
        self.cur, self.stage = out, self.stage + 1

    def finish(self):
        while not self.done():
            self.advance(_run_alone(f"{self.name}_s{self.stage}", self.rider(256)))
        return self.cur.reshape(self.R, self.C)


RIDER_TILE_BYTES = 3 * 512 * 1024


def _pick(jobs, nsteps):
    for job in sorted(jobs or (), key=lambda j: -j.R * j.C):
        if job.done():
            continue
        streams = 1 if job.stage == 0 else 2
        riders = [job.rider(rows) for rows in (512, 256, 128, 64)
                  if rows * job.C * 4 * streams <= RIDER_TILE_BYTES or rows == 64]
        for rider in riders:
            if 2 * rider.n <= nsteps:
                return job, rider
        if riders[-1].n <= nsteps:
            return job, riders[-1]
    return None, None


def _sum8(name, g):
    _, R, C = g.shape
    tr = _tile(R, 512)

    def body(g_ref, o_ref):
        acc = g_ref[0]
        for d in range(1, NDEV):
            acc = acc + g_ref[d]
        o_ref[...] = acc

    return pl.pallas_call(
        body, name=name, grid=(R // tr,), in_specs=[pl.BlockSpec((NDEV, tr, C), lambda i: (0, i, 0))],
        out_specs=pl.BlockSpec((tr, C), lambda i: (i, 0)), out_shape=jax.ShapeDtypeStruct((R, C), F32),
        compiler_params=_cp(("parallel",)),
    )(g)


def _adamw_math(w, g, m, v):
    m = B1 * m + (1.0 - B1) * g
    v = B2 * v + (1.0 - B2) * jnp.square(g)
    m_hat = m / (1.0 - B1 ** STEP)
    v_hat = v / (1.0 - B2 ** STEP)
    delta = -LR * (m_hat / (jnp.sqrt(v_hat) + EPS) + WD * w)
    return delta, m, v


def _adamw(name, w, m, v, gp, *, tr, cw, gw, goff=0, jobs=None):
    R, C = w.shape
    nc = C // cw
    nr = R // tr

    def body(w_ref, m_ref, v_ref, g_ref, g_out, d_out, m_out, v_out):
        g = g_ref[...][:, :cw]
        d, mn, vn = _adamw_math(w_ref[...], g, m_ref[...], v_ref[...])
        g_out[...] = g
        d_out[...] = d
        m_out[...] = mn
        v_out[...] = vn

    wspec = pl.BlockSpec((tr, cw), lambda i, j: (i, j))
    gspec = pl.BlockSpec((tr, gw), lambda i, j: (i, goff + j))
    outs, _ = _hosted(name, body, (nr, nc), [wspec, wspec, wspec, gspec], [wspec] * 4,
                      [jax.ShapeDtypeStruct((R, C), F32)] * 4, [], (w, m, v, gp), ("parallel", "parallel"), None, jobs)
    return outs


def _adamw_small(name, w, m, v, g):
    R, C = w.shape

    def body(w_ref, m_ref, v_ref, g_ref, d_out, m_out, v_out):
        d, mn, vn = _adamw_math(w_ref[...], g_ref[...], m_ref[...], v_ref[...])
        d_out[...] = d
        m_out[...] = mn
        v_out[...] = vn

    tr = _tile(R, 512)
    spec = pl.BlockSpec((tr, C), lambda i: (i, 0))
    return pl.pallas_call(
        body, name=name, grid=(R // tr,), in_specs=[spec] * 4, out_specs=[spec] * 3,
        out_shape=[jax.ShapeDtypeStruct((R, C), F32)] * 3, compiler_params=_cp(("parallel",)),
    )(w, m, v, g)


def _prep(name, parts, rows_p, cols_p):
    R, C = parts[0].shape
    n = len(parts)

    def body(*refs):
        o_ref = refs[n]
        if (R, C) != (rows_p, cols_p):
            o_ref[...] = jnp.zeros_like(o_ref)
        for i in range(n):
            o_ref[0:R, i * cols_p:i * cols_p + C] = refs[i][...].astype(BF16)

    return pl.pallas_call(
        body, name=name, out_shape=jax.ShapeDtypeStruct((rows_p, n * cols_p), BF16), compiler_params=_cp(),
    )(*parts)


def _attn_masks():
    lane = lax.broadcasted_iota(jnp.int32, (1, LANE), 1)
    return [(lane < HEAD_DIM), (lane >= HEAD_DIM)]


QH = QB
KW = QB + QH


def _band_valid(base):
    qi = lax.broadcasted_iota(jnp.int32, (QH, KW), 0)
    ki = lax.broadcasted_iota(jnp.int32, (QH, KW), 1)
    dist = qi + QB - ki
    return (dist >= 0) & (dist <= QB) & (base + ki - QB >= 0)


ATTN_T = max(DILATIONS) * QB


def _attn_groups(T):
    out = []
    for d in DILATIONS:
        for r in range(d):
            for l0 in range(0, T // d, QH):
                qrows = pl.ds(r + d * l0, QH, stride=d) if d > 1 else pl.ds(l0, QH)
                k0 = T + r + d * (l0 - QB)
                krows = pl.ds(k0, KW, stride=d) if d > 1 else pl.ds(k0, KW)
                out.append((d, qrows, krows, l0))
    return out


def _attn_specs(T, width_off):
    cur = pl.BlockSpec((T, LANE), lambda hp, b: (b, width_off + hp))
    prev = pl.BlockSpec((T, LANE), lambda hp, b: (jnp.maximum(b - 1, 0), width_off + hp))
    return cur, prev


def _attn_fwd(z):
    S, ZW = z.shape
    T = min(ATTN_T, S)
    scale = HEAD_DIM ** -0.5
    groups = _attn_groups(T)

    def body(q_ref, kc_ref, kp_ref, vc_ref, vp_ref, y_ref, m_ref, l_ref, kcat, vcat):
        b = pl.program_id(1)
        kcat[0:T, :] = kp_ref[...]
        kcat[T:, :] = kc_ref[...]
        vcat[0:T, :] = vp_ref[...]
        vcat[T:, :] = vc_ref[...]
        masks = _attn_masks()
        for d, qrows, krows, l0 in groups:
            q = q_ref[qrows, :]
            kk = kcat[krows, :].astype(BF16)
            vv = vcat[krows, :].astype(BF16)
            valid = _band_valid(b * (T // d) + l0)
            o_new = m_new = l_new = None
            for hm in masks:
                qh = jnp.where(hm, q, 0.0).astype(BF16)
                s = lax.dot_general(qh, kk, (((1,), (1,)), ((), ())), preferred_element_type=F32) * scale
                s = jnp.where(valid, s, MASK_VALUE)
                m = jnp.max(s, axis=-1, keepdims=True)
                p = jnp.exp(s - m)
                l = jnp.sum(p, axis=-1, keepdims=True)
                o = jnp.dot(p.astype(BF16), vv, preferred_element_type=F32)
                if o_new is None:
                    o_new, m_new, l_new = o, jnp.broadcast_to(m, (QH, LANE)), jnp.broadcast_to(l, (QH, LANE))
                else:
                    o_new = jnp.where(hm, o, o_new)
                    m_new = jnp.where(hm, m, m_new)
                    l_new = jnp.where(hm, l, l_new)
            if d == DILATIONS[0]:
                y_ref[qrows, :] = o_new
                m_ref[qrows, :] = m_new
                l_ref[qrows, :] = l_new
            else:
                m_old = m_ref[qrows, :]
                m_all = jnp.maximum(m_old, m_new)
                w_old, w_new = jnp.exp(m_old - m_all), jnp.exp(m_new - m_all)
                y_ref[qrows, :] = w_old * y_ref[qrows, :] + w_new * o_new
                l_ref[qrows, :] = w_old * l_ref[qrows, :] + w_new * l_new
                m_ref[qrows, :] = m_all
        y_ref[...] = y_ref[...] / l_ref[...]

    qc, _ = _attn_specs(T, 0)
    kc, kp = _attn_specs(T, ATTN_W // LANE)
    vc, vp = _attn_specs(T, 2 * ATTN_W // LANE)
    shp = jax.ShapeDtypeStruct((S, ATTN_W), F32)
    return pl.pallas_call(
        body, name="attn_fwd", grid=(ATTN_W // LANE, S // T),
        in_specs=[qc, kc, kp, vc, vp], out_specs=[qc, qc, qc], out_shape=[shp, shp, shp],
        scratch_shapes=[pltpu.VMEM((2 * T, LANE), F32), pltpu.VMEM((2 * T, LANE), F32)],
        compiler_params=_cp(("parallel", "parallel")),
    )(z, z, z, z, z)


def _attn_bwd(z, dya, ya, mg, den):
    S, ZW = z.shape
    T = min(ATTN_T, S)
    scale = HEAD_DIM ** -0.5
    groups = _attn_groups(T)

    def body(q_ref, kc_ref, kp_ref, vc_ref, vp_ref, dy_ref, y_ref, m_ref, n_ref, dq_ref, dk_ref, dv_ref,
             kcat, vcat, dkcat, dvcat):
        b = pl.program_id(1)

        @pl.when(b == 0)
        def _():
            dk_ref[...] = jnp.zeros_like(dk_ref)
            dv_ref[...] = jnp.zeros_like(dv_ref)

        kcat[0:T, :] = kp_ref[...]
        kcat[T:, :] = kc_ref[...]
        vcat[0:T, :] = vp_ref[...]
        vcat[T:, :] = vc_ref[...]
        dkcat[...] = jnp.zeros_like(dkcat)
        dvcat[...] = jnp.zeros_like(dvcat)
        dq_ref[...] = jnp.zeros_like(dq_ref)
        masks = _attn_masks()
        for d, rows, krows, l0 in groups:
            q, dy, y = q_ref[rows, :], dy_ref[rows, :], y_ref[rows, :]
            mrow, nrow = m_ref[rows, :], n_ref[rows, :]
            kk = kcat[krows, :].astype(BF16)
            vv = vcat[krows, :].astype(BF16)
            valid = _band_valid(b * (T // d) + l0)
            dq_acc = jnp.zeros((QH, LANE), F32)
            dk_acc = jnp.zeros((KW, LANE), F32)
            dv_acc = jnp.zeros((KW, LANE), F32)
            for hm in masks:
                qh = jnp.where(hm, q, 0.0).astype(BF16)
                dyh = jnp.where(hm, dy, 0.0)
                dyb = dyh.astype(BF16)
                dsum = jnp.sum(dyh * y, axis=-1, keepdims=True)
                mh = jnp.max(jnp.where(hm, mrow, MASK_VALUE), axis=-1, keepdims=True)
                nh = jnp.max(jnp.where(hm, nrow, 0.0), axis=-1, keepdims=True)
                s = lax.dot_general(qh, kk, (((1,), (1,)), ((), ())), preferred_element_type=F32) * scale
                p = jnp.where(valid, jnp.exp(s - mh), 0.0) * (1.0 / nh)
                pb = p.astype(BF16)
                dv_h = lax.dot_general(pb, dyb, (((0,), (0,)), ((), ())), preferred_element_type=F32)
                dp = lax.dot_general(dyb, vv, (((1,), (1,)), ((), ())), preferred_element_type=F32)
                ds = (p * (dp - dsum) * scale).astype(BF16)
                dq_h = jnp.dot(ds, kk, preferred_element_type=F32)
                dk_h = lax.dot_general(ds, qh, (((0,), (0,)), ((), ())), preferred_element_type=F32)
                dq_acc += jnp.where(hm, dq_h, 0.0)
                dk_acc += dk_h
                dv_acc += dv_h
            dq_ref[rows, :] += dq_acc
            dkcat[krows, :] += dk_acc
            dvcat[krows, :] += dv_acc

        base = pl.multiple_of(b * T, T)
        dk_ref[pl.ds(base, T), :] += dkcat[T:, :]
        dv_ref[pl.ds(base, T), :] += dvcat[T:, :]

        @pl.when(b > 0)
        def _():
            prev = pl.multiple_of(b * T - T, T)
            dk_ref[pl.ds(prev, T), :] += dkcat[0:T, :]
            dv_ref[pl.ds(prev, T), :] += dvcat[0:T, :]

    qc, _ = _attn_specs(T, 0)
    kc, kp = _attn_specs(T, ATTN_W // LANE)
    vc, vp = _attn_specs(T, 2 * ATTN_W // LANE)
    whole = pl.BlockSpec((S, LANE), lambda hp, b: (0, hp))
    shp = jax.ShapeDtypeStruct((S, ATTN_W), F32)
    return pl.pallas_call(
        body, name="attn_bwd", grid=(ATTN_W // LANE, S // T),
        in_specs=[qc, kc, kp, vc, vp, qc, qc, qc, qc], out_specs=[qc, whole, whole], out_shape=[shp, shp, shp],
        scratch_shapes=[pltpu.VMEM((2 * T, LANE), F32)] * 4,
        compiler_params=_cp(("parallel", "arbitrary")),
    )(z, z, z, z, z, dya, ya, mg, den)


def _ssm_disc(lr, li, logdt, br, bi):
    dt = jnp.exp(logdt)
    mag = jnp.exp(lr * dt)
    ar = mag * jnp.cos(li * dt)
    ai = mag * jnp.sin(li * dt)
    nr, ni = ar - 1.0, ai
    den = lr * lr + li * li
    cr = (nr * lr + ni * li) / den
    ci = (ni * lr - nr * li) / den
    return ar, ai, cr * br - ci * bi, cr * bi + ci * br


def _ssm_prep(lr, li, logdt, br, bi):
    n, c = br.shape
    outs, _ = _rowwise("ssm_prep", lambda *a: (list(_ssm_disc(*a)), []), n, _tile(n, 512),
                       [_full(a) for a in (lr, li, logdt, br, bi)], [],
                       [(1, 1, _c0, F32), (1, 1, _c0, F32), (c, c, _c0, F32), (c, c, _c0, F32)])
    return outs


def _ssm_prep_bwd(lr, li, logdt, br, bi, dar, dai, dbbr, dbbi):
    n, c = br.shape

    def f(lrb, lib, dtb, brb, bib, *cts):
        _, vjp = jax.vjp(_ssm_disc, lrb, lib, dtb, brb, bib)
        return list(vjp(cts)), []

    outs, _ = _rowwise("ssm_prep_bwd", f, n, _tile(n, 512),
                       [_full(a) for a in (lr, li, logdt, br, bi, dar, dai, dbbr, dbbi)], [],
                       [(1, 1, _c0, F32)] * 3 + [(c, c, _c0, F32)] * 2)
    return outs


def _cmul(ar, ai, br, bi):
    return ar * br - ai * bi, ar * bi + ai * br


def _scan_consts(ar, ai, reverse):
    w = ar.shape[-1]
    a1 = (jnp.broadcast_to(ar, (8, w)), jnp.broadcast_to(ai, (8, w)))
    a2 = _cmul(*a1, *a1)
    a4 = _cmul(*a2, *a2)
    a8 = _cmul(*a4, *a4)
    row = lax.broadcasted_iota(jnp.int32, (8, w), 0)
    e = (8 - row) if reverse else (row + 1)
    one, zero = jnp.ones((8, w), F32), jnp.zeros((8, w), F32)
    pw = (one, zero)
    for bit, ap in ((1, a1), (2, a2), (4, a4), (8, a8)):
        sel = (e & bit) != 0
        nxt = _cmul(*pw, *ap)
        pw = (jnp.where(sel, nxt[0], pw[0]), jnp.where(sel, nxt[1], pw[1]))
    steps = []
    for sh, (pr, pi) in zip((1, 2, 4), (a1, a2, a4)):
        keep = (row < 8 - sh) if reverse else (row >= sh)
        steps.append((jnp.where(keep, pr, 0.0), jnp.where(keep, pi, 0.0)))
    return steps, pw, row


def _scan_group(xr, xi, cr, ci, consts, reverse):
    steps, pw, _ = consts
    for sh, (pr, pi) in zip((1, 2, 4), steps):
        by = 8 - sh if reverse else sh
        tr_, ti_ = _cmul(pr, pi, pltpu.roll(xr, by, 0), pltpu.roll(xi, by, 0))
        xr = xr + tr_
        xi = xi + ti_
    tr_, ti_ = _cmul(pw[0], pw[1], cr, ci)
    return xr + tr_, xi + ti_


def _ssm_fwd(z, a_r, a_i, bdr, bdi, cmr, cmi, dskip, ts, rider=None):
    S, ZW = z.shape
    NS = SSM_G * SSM_P
    PW = PACK * SSM_P
    uoff = (ZW - SSM_W) // LANE
    nsteps = S // ts

    def body(u_ref, ar_ref, ai_ref, bdr_ref, bdi_ref, cmr_ref, cmi_ref, d_ref, hr_ref, hi_ref, y_ref, car_r, car_i):
        s = pl.program_id(1)

        @pl.when(s == 0)
        def _():
            car_r[...] = jnp.zeros_like(car_r)
            car_i[...] = jnp.zeros_like(car_i)

        u = u_ref[...]
        ub = u.astype(BF16)
        nt = (((1,), (1,)), ((), ()))
        hr_ref[...] = lax.dot_general(ub, bdr_ref[...], nt, preferred_element_type=F32)
        hi_ref[...] = lax.dot_general(ub, bdi_ref[...], nt, preferred_element_type=F32)
        consts = _scan_consts(ar_ref[...], ai_ref[...], False)

        def step(j, carry):
            rows = pl.ds(pl.multiple_of(j * 8, 8), 8)
            hr, hi = _scan_group(hr_ref[rows, :], hi_ref[rows, :], carry[0], carry[1], consts, False)
            hr_ref[rows, :] = hr
            hi_ref[rows, :] = hi
            return jnp.broadcast_to(hr[7:8, :], (8, PW)), jnp.broadcast_to(hi[7:8, :], (8, PW))

        cr, ci = lax.fori_loop(0, ts // 8, step, (car_r[...], car_i[...]))
        car_r[...] = cr
        car_i[...] = ci
        y = lax.dot_general(hr_ref[...].astype(BF16), cmr_ref[...], nt, preferred_element_type=F32)
        y -= lax.dot_general(hi_ref[...].astype(BF16), cmi_ref[...], nt, preferred_element_type=F32)
        y_ref[...] = y + d_ref[...] * u

    row_a = pl.BlockSpec((1, PW), lambda i, s: (0, i))
    (hr, hi, y), extra = _hosted(
        "ssm_fwd", body, (SSM_G // PACK, nsteps),
        [pl.BlockSpec((ts, LANE), lambda i, s: (s, uoff + i)), row_a, row_a,
         pl.BlockSpec((None, PW, LANE), lambda i, s: (i, 0, 0)), pl.BlockSpec((None, PW, LANE), lambda i, s: (i, 0, 0)),
         pl.BlockSpec((None, LANE, PW), lambda i, s: (i, 0, 0)), pl.BlockSpec((None, LANE, PW), lambda i, s: (i, 0, 0)),
         pl.BlockSpec((1, LANE), lambda i, s: (0, i))],
        [pl.BlockSpec((ts, PW), lambda i, s: (s, i)), pl.BlockSpec((ts, PW), lambda i, s: (s, i)),
         pl.BlockSpec((ts, LANE), lambda i, s: (s, i))],
        [jax.ShapeDtypeStruct((S, NS), F32), jax.ShapeDtypeStruct((S, NS), F32), jax.ShapeDtypeStruct((S, SSM_W), F32)],
        [pltpu.VMEM((8, PW), F32), pltpu.VMEM((8, PW), F32)], (z, a_r, a_i, bdr, bdi, cmr, cmi, dskip),
        ("parallel", "arbitrary"), rider)
    return hr, hi, y, extra


def _ssm_bwd(z, dyp, hr, hi, a_r, a_i, bdr, bdi, cmr, cmi, dskip, ts, jobs=None):
    S, ZW = z.shape
    NS = SSM_G * SSM_P
    PW = PACK * SSM_P
    uoff = (ZW - SSM_W) // LANE
    nsteps = S // ts
    npk = SSM_G // PACK

    def body(u_ref, dy_ref, hr_ref, hi_ref, hpr_ref, hpi_ref, ar_ref, ai_ref, bdr_ref, bdi_ref, cmr_ref, cmi_ref,
             d_ref, du_ref, dbdr_ref, dbdi_ref, dcmr_ref, dcmi_ref, dar_ref, dai_ref, dd_ref,
             lr_s, li_s, hcr, hci, car_r, car_i):
        s = pl.program_id(1)
        first_tile = s == nsteps - 1

        @pl.when(s == 0)
        def _():
            car_r[...] = jnp.zeros_like(car_r)
            car_i[...] = jnp.zeros_like(car_i)
            for r in (dbdr_ref, dbdi_ref, dcmr_ref, dcmi_ref, dar_ref, dai_ref, dd_ref):
                r[...] = jnp.zeros_like(r)

        u, dy = u_ref[...], dy_ref[...]
        ub, dyb = u.astype(BF16), dy.astype(BF16)
        lr_s[...] = jnp.dot(dyb, cmr_ref[...], preferred_element_type=F32)
        li_s[...] = -jnp.dot(dyb, cmi_ref[...], preferred_element_type=F32)
        keep_prev = jnp.where(first_tile, 0.0, 1.0)
        hcr[0:8, :] = hpr_ref[...] * keep_prev
        hci[0:8, :] = hpi_ref[...] * keep_prev
        hcr[8:, :] = hr_ref[...]
        hci[8:, :] = hi_ref[...]
        consts = _scan_consts(ar_ref[...], -ai_ref[...], True)
        row = consts[2]
        ngrp = ts // 8

        def step(jj, carry):
            cr, ci, accr, acci = carry
            j = ngrp - 1 - jj
            rows = pl.ds(pl.multiple_of(j * 8, 8), 8)
            nxt = pl.ds(pl.multiple_of(j * 8 + 8, 8), 8)
            lr, li = _scan_group(lr_s[rows, :], li_s[rows, :], cr, ci, consts, True)
            lr_s[rows, :] = lr
            li_s[rows, :] = li
            pr, pi = hcr[rows, :], hci[rows, :]
            hsr = jnp.where(row == 0, jnp.broadcast_to(pr[7:8, :], (8, PW)), pltpu.roll(hcr[nxt, :], 1, 0))
            hsi = jnp.where(row == 0, jnp.broadcast_to(pi[7:8, :], (8, PW)), pltpu.roll(hci[nxt, :], 1, 0))
            accr = accr + lr * hsr + li * hsi
            acci = acci + li * hsr - lr * hsi
            return jnp.broadcast_to(lr[0:1, :], (8, PW)), jnp.broadcast_to(li[0:1, :], (8, PW)), accr, acci

        zero = jnp.zeros((8, PW), F32)
        cr, ci, accr, acci = lax.fori_loop(0, ngrp, step, (car_r[...], car_i[...], zero, zero))
        car_r[...] = cr
        car_i[...] = ci
        dar_ref[...] += jnp.sum(accr, axis=0, keepdims=True)
        dai_ref[...] += jnp.sum(acci, axis=0, keepdims=True)
        lrb, lib = lr_s[...].astype(BF16), li_s[...].astype(BF16)
        du = jnp.dot(lrb, bdr_ref[...], preferred_element_type=F32)
        du += jnp.dot(lib, bdi_ref[...], preferred_element_type=F32)
        du_ref[...] = du + dy * d_ref[...]
        tn = (((0,), (0,)), ((), ()))
        dbdr_ref[...] += lax.dot_general(lrb, ub, tn, preferred_element_type=F32)
        dbdi_ref[...] += lax.dot_general(lib, ub, tn, preferred_element_type=F32)
        dcmr_ref[...] += lax.dot_general(dyb, hr_ref[...].astype(BF16), tn, preferred_element_type=F32)
        dcmi_ref[...] -= lax.dot_general(dyb, hi_ref[...].astype(BF16), tn, preferred_element_type=F32)
        dd_ref[...] += jnp.sum(dy * u, axis=0, keepdims=True)

    rev = lambda s: nsteps - 1 - s
    row_a = pl.BlockSpec((1, PW), lambda i, s: (0, i))
    tile = pl.BlockSpec((ts, PW), lambda i, s: (rev(s), i))
    prev8 = pl.BlockSpec((8, PW), lambda i, s: (jnp.maximum(rev(s) * (ts // 8) - 1, 0), i))
    cols = pl.BlockSpec((ts, LANE), lambda i, s: (rev(s), i))
    bd = pl.BlockSpec((None, PW, LANE), lambda i, s: (i, 0, 0))
    cm = pl.BlockSpec((None, LANE, PW), lambda i, s: (i, 0, 0))
    outs, _ = _hosted(
        "ssm_bwd", body, (npk, nsteps),
        [pl.BlockSpec((ts, LANE), lambda i, s: (rev(s), uoff + i)), cols, tile, tile, prev8, prev8,
         row_a, row_a, bd, bd, cm, cm, pl.BlockSpec((1, LANE), lambda i, s: (0, i))],
        [cols, bd, bd, cm, cm, row_a, row_a, pl.BlockSpec((1, LANE), lambda i, s: (0, i))],
        [jax.ShapeDtypeStruct((S, SSM_W), F32),
         jax.ShapeDtypeStruct((npk, PW, LANE), F32), jax.ShapeDtypeStruct((npk, PW, LANE), F32),
         jax.ShapeDtypeStruct((npk, LANE, PW), F32), jax.ShapeDtypeStruct((npk, LANE, PW), F32),
         jax.ShapeDtypeStruct((1, NS), F32), jax.ShapeDtypeStruct((1, NS), F32), jax.ShapeDtypeStruct((1, SSM_W), F32)],
        [pltpu.VMEM((ts, PW), F32), pltpu.VMEM((ts, PW), F32), pltpu.VMEM((ts + 8, PW), F32),
         pltpu.VMEM((ts + 8, PW), F32), pltpu.VMEM((8, PW), F32), pltpu.VMEM((8, PW), F32)],
        (z, dyp, hr, hi, hr, hi, a_r, a_i, bdr, bdi, cmr, cmi, dskip), ("parallel", "arbitrary"), None, jobs)
    return outs


def _block_diag(m4):
    npk, g, a, b = m4.shape
    eye = jnp.eye(g, dtype=m4.dtype)
    return (m4[:, :, :, None, :] * eye[None, :, None, :, None]).reshape(npk, g * a, g * b)


def _block_diag_take(m, a, b):
    npk = m.shape[0]
    m5 = m.reshape(npk, PACK, a, PACK, b)
    return jnp.stack([m5[:, g, :, g, :] for g in range(PACK)], axis=1)


def _mix_out(ya, ypre, gl, ga, gb, bglu):
    yg = jax.nn.gelu(ypre)
    yb = yg * jax.nn.sigmoid(gl + bglu)
    return jnp.concatenate([_rms(ya, ga), _rms(yb, gb)], axis=-1)


def _tail_loss(h3, gl, pe, gf, tgt):
    h4 = h3 + jax.nn.sigmoid(gl) * pe
    err = jnp.square(_rms(h4, gf) - tgt)
    return 0.5 * jnp.mean(err, axis=-1, keepdims=True)


def kernel(x, p, ffn1_norm, ffn1_w_gate, ffn1_w_up, ffn1_w_down, mix_norm, w_in, attn_out_norm, ssm_lambda_re, ssm_lambda_im, ssm_log_dt, ssm_b_re, ssm_b_im, ssm_c_re, ssm_c_im, ssm_d, ssm_w_glu, ssm_b_glu, ssm_out_norm, w_out, ffn2_norm, ffn2_w_gate, ffn2_w_up, ffn2_w_down, ple_norm, ple_w_gate, ple_w_proj, final_norm, loss_target, m_ffn1_norm, m_ffn1_w_gate, m_ffn1_w_up, m_ffn1_w_down, m_mix_norm, m_w_in, m_attn_out_norm, m_ssm_lambda_re, m_ssm_lambda_im, m_ssm_log_dt, m_ssm_b_re, m_ssm_b_im, m_ssm_c_re, m_ssm_c_im, m_ssm_d, m_ssm_w_glu, m_ssm_b_glu, m_ssm_out_norm, m_w_out, m_ffn2_norm, m_ffn2_w_gate, m_ffn2_w_up, m_ffn2_w_down, m_ple_norm, m_ple_w_gate, m_ple_w_proj, m_final_norm, v_ffn1_norm, v_ffn1_w_gate, v_ffn1_w_up, v_ffn1_w_down, v_mix_norm, v_w_in, v_attn_out_norm, v_ssm_lambda_re, v_ssm_lambda_im, v_ssm_log_dt, v_ssm_b_re, v_ssm_b_im, v_ssm_c_re, v_ssm_c_im, v_ssm_d, v_ssm_w_glu, v_ssm_b_glu, v_ssm_out_norm, v_w_out, v_ffn2_norm, v_ffn2_w_gate, v_ffn2_w_up, v_ffn2_w_down, v_ple_norm, v_ple_w_gate, v_ple_w_proj, v_final_norm):
    A = dict(locals())
    xs = x[0]
    ps = p[0, 0]
    tgt = loss_target[0]
    S, D = xs.shape
    FSH = ffn1_w_gate.shape[-1]
    FSP = -(-FSH // LANE) * LANE
    TR = _tile(S, 256)
    ZW = 3 * ATTN_W + SSM_W

    wgu1 = _prep("prep_gu1", [ffn1_w_gate[0], ffn1_w_up[0]], D, FSP)
    wgu2 = _prep("prep_gu2", [ffn2_w_gate[0], ffn2_w_up[0]], D, FSP)
    wd1 = _prep("prep_d1", [ffn1_w_down[0]], FSP, D)
    wd2 = _prep("prep_d2", [ffn2_w_down[0]], FSP, D)
    win = _prep("prep_in", [w_in[0]], D, w_in.shape[-1])
    wglu = _prep("prep_glu", [ssm_w_glu[0]], ssm_w_glu.shape[1], SSM_W)
    wout = _prep("prep_out", [w_out[0]], w_out.shape[1], D)
    wpg = _prep("prep_pg", [ple_w_gate[0]], ple_w_gate.shape[1], D)
    wpp = _prep("prep_pp", [ple_w_proj[0]], ple_w_proj.shape[1], ple_w_proj.shape[2])
    (Wgu1,) = _all_gather("ag_weights", [wgu1])
    rowstack = lambda w: w.reshape(1, w.shape[0] * w.shape[1], w.shape[2])

    def ffn_norm(tag, h, gain):
        return _rowwise(f"{tag}_norm", lambda a, g: ([_rms(a, g)], []), S, TR, [_full(h)], [gain], [(D, D, _c0, BF16)])[0][0]

    xn1 = ffn_norm("ffn1", xs, ffn1_norm)
    gu1, hid1, (Wd1, Win) = _ffn_up("ffn1_up", xn1, Wgu1, rider=_GatherRider([wd1, win]))
    Wd1 = rowstack(Wd1)
    h1, (Wd2,) = _mm_nn("ffn1_down", hid1, Wd1, tn=D // 2, tk=NDEV * FSP, res=xs, scale=0.5,
                        rider=_GatherRider([wd2]))
    Wd2 = rowstack(Wd2)
    un = ffn_norm("mix", h1, mix_norm)
    z, (Wglu, Wout, Wpg, Wpp) = _mm_nn("mix_in", un, Win, tn=512, tk=D,
                                       rider=_GatherRider([wglu, wout, wpg, wpp]))
    Wglu, Wout, Wpg = rowstack(Wglu), rowstack(Wout), rowstack(Wpg)
    ya, mg, den = _attn_fwd(z)

    col = lambda a: a.reshape(-1, 1)
    lr_c, li_c = col(ssm_lambda_re), col(ssm_lambda_im)
    dt_c = col(jnp.broadcast_to(ssm_log_dt.reshape(SSM_G, 1), (SSM_G, SSM_P)))
    b_re2, b_im2 = ssm_b_re.reshape(-1, SSM_C), ssm_b_im.reshape(-1, SSM_C)
    ar_c, ai_c, bbr, bbi = _ssm_prep(lr_c, li_c, dt_c, b_re2, b_im2)
    a_r, a_i = ar_c.reshape(1, -1), ai_c.reshape(1, -1)
    npk = SSM_G // PACK
    bdr = _block_diag(bbr.reshape(npk, PACK, SSM_P, SSM_C)).astype(BF16)
    bdi = _block_diag(bbi.reshape(npk, PACK, SSM_P, SSM_C)).astype(BF16)
    cmr = _block_diag(ssm_c_re.reshape(npk, PACK, SSM_C, SSM_P)).astype(BF16)
    cmi = _block_diag(ssm_c_im.reshape(npk, PACK, SSM_C, SSM_P)).astype(BF16)
    TS = _tile(S, 512)
    hr, hi, ypre, (Wgu2,) = _ssm_fwd(z, a_r, a_i, bdr, bdi, cmr, cmi, ssm_d, TS, rider=_GatherRider([wgu2]))
    (yg,), _ = _rowwise("ssm_gelu", lambda a: ([jax.nn.gelu(a)], []), S, TR, [_full(ypre)], [], [(SSM_W, SSM_W, _c0, BF16)])
    gl = _mm_nn("ssm_glu", yg, Wglu, tn=SSM_W, tk=SSM_W)
    (ycat,), _ = _rowwise("mix_out", lambda *a: ([_mix_out(*a)], []), S, TR, [_full(ya), _full(ypre), _full(gl)],
                          [attn_out_norm, ssm_out_norm, ssm_b_glu], [(MIX_W, MIX_W, _c0, BF16)])
    h2, xn2 = _mm_nn("mix_proj", ycat, Wout, tn=D, tk=D, res=h1, scale=1.0, gain=ffn2_norm)
    gu2, hid2, _ = _ffn_up("ffn2_up", xn2, Wgu2)
    h3 = _mm_nn("ffn2_down", hid2, Wd2, tn=D // 2, tk=NDEV * FSP, res=h2, scale=0.5)
    hn = ffn_norm("ple", h3, ple_norm)
    pgl = _mm_nn("ple_gate", hn, Wpg, tn=D // 2, tk=D)
    pb = ps
    pe = _mm_nn("ple_proj", pb, Wpp, tn=Wpp.shape[2], tk=Wpp.shape[1])

    def tail(h3b, glb, peb, tb, gf):
        rows, vjp = jax.vjp(lambda a, b, c, g: _tail_loss(a, b, c, g, tb), h3b, glb, peb, gf)
        dh, dgl, dpe, dgf = vjp(jnp.ones_like(rows))
        return [dh, dgl, dpe], [jnp.broadcast_to(jnp.sum(rows, axis=0, keepdims=True), (1, LANE)), dgf]

    (dh3_dir, dpgl, dpe), (loss_row, g_final) = _rowwise(
        "tail", tail, S, TR, [_full(h3), _full(pgl), _full(pe), _full(tgt)], [final_norm.reshape(1, D)],
        [(D, D, _c0, F32), (D, D, _c0, BF16), (D, D, _c0, BF16)], [(LANE, LANE, _c0), (D, D, _c0)])
    loss = lax.psum(loss_row[0, 0], AXES)

    def norm_bwd(tag, h, gain, dn, dres):
        def f(hb, dnb, drb, g):
            _, vjp = jax.vjp(_rms, hb, g)
            dh, dg = vjp(dnb)
            dh = dh + drb
            return [dh, dh], [dg]
        (dh, dhb), (dg,) = _rowwise(f"{tag}_norm_bwd", f, S, TR, [_full(h), _full(dn), _full(dres)], [gain],
                                    [(D, D, _c0, F32), (D, D, _c0, BF16)], [(D, D, _c0)])
        return dh, dhb, dg

    restack = lambda g: g.reshape((NDEV, g.shape[1] // NDEV) + g.shape[2:])
    jobs, scat = [], {}

    def scatter(key, g):
        scat[key] = _Scatter("rs_" + key, g)
        jobs.append(scat[key])

    late = []
    dhn = _mm_nt("ple_gate_dx", dpgl, Wpg, tn=D, tk=D)
    late.append(lambda: scatter("pg", restack(_mm_tn("ple_gate_dw", hn, dpgl, 1, jobs=jobs))))
    late.append(lambda: scatter("pp", _mm_tn("ple_proj_dw", pb, dpe, NDEV, jobs=jobs)))
    dh3, dh3b, g_ple_norm = norm_bwd("ple", h3, ple_norm, dhn, dh3_dir)

    def ffn_bwd(tag, h, gain, Wgu, Wd, saved, dout, doutb):
        xn, gu, hid = saved
        dgu = _ffn_down_dx(f"{tag}_down_dx", doutb, Wd, gu, NDEV, scale=0.5, jobs=jobs)
        scatter(tag + "gu", _mm_tn(f"{tag}_up_dw", xn, dgu, NDEV, tn=FSP, jobs=jobs))
        scatter(tag + "d", restack(_mm_tn(f"{tag}_down_dw", hid, doutb, 1, tko=FSP, tn=256, scale=0.5, jobs=jobs)))
        dxn = _mm_nt(f"{tag}_up_dx", dgu, Wgu, tm=512, tn=256, jb=NDEV, jobs=jobs)
        dh, dhb, g_norm = norm_bwd(tag, h, gain, dxn, dout)
        return dh, dhb, g_norm

    dh2, dh2b, g_ffn2_norm = ffn_bwd("ffn2", h2, ffn2_norm, Wgu2, Wd2, (xn2, gu2, hid2), dh3, dh3b)

    dycat = _mm_nt("mix_proj_dx", dh2b, Wout, tn=D, tk=D, jobs=jobs)
    late.append(lambda: scatter("out", restack(_mm_tn("mix_proj_dw", ycat, dh2b, 1, jobs=jobs))))

    def mix_out_bwd(yab, ypb, glb, dyc, ga, gb, bglu):
        _, vjp = jax.vjp(_mix_out, yab, ypb, glb, ga, gb, bglu)
        dya_, dyp_, dgl_, dga, dgb, dbg = vjp(dyc)
        return [dya_, dyp_, dgl_], [dga, dgb, dbg]
    (dya, dyp_dir, dglb), (g_attn_norm, g_ssm_norm, g_bglu) = _rowwise(
        "mix_out_bwd", mix_out_bwd, S, TR, [_full(ya), _full(ypre), _full(gl), _full(dycat)],
        [attn_out_norm, ssm_out_norm, ssm_b_glu],
        [(ATTN_W, ATTN_W, _c0, F32), (SSM_W, SSM_W, _c0, F32), (SSM_W, SSM_W, _c0, BF16)],
        [(ATTN_W, ATTN_W, _c0), (SSM_W, SSM_W, _c0), (SSM_W, SSM_W, _c0)])
    dyg = _mm_nt("ssm_glu_dx", dglb, Wglu, tn=SSM_W, tk=SSM_W, jobs=jobs)
    late.append(lambda: scatter("glu", restack(_mm_tn("ssm_glu_dw", yg, dglb, 1, jobs=jobs))))

    def gelu_bwd(ypb, dygb, ddir):
        _, vjp = jax.vjp(jax.nn.gelu, ypb)
        return [ddir + vjp(dygb)[0]], []
    (dyp,), _ = _rowwise("ssm_gelu_bwd", gelu_bwd, S, TR, [_full(ypre), _full(dyg), _full(dyp_dir)], [],
                         [(SSM_W, SSM_W, _c0, F32)])
    du, dbdr, dbdi, dcmr, dcmi, da_r, da_i, g_ssm_d = _ssm_bwd(z, dyp, hr, hi, a_r, a_i, bdr, bdi, cmr, cmi, ssm_d, TS,
                                                              jobs=jobs)
    dbbr = _block_diag_take(dbdr, SSM_P, SSM_C).reshape(-1, SSM_C)
    dbbi = _block_diag_take(dbdi, SSM_P, SSM_C).reshape(-1, SSM_C)
    g_c_re = _block_diag_take(dcmr, SSM_C, SSM_P).reshape(ssm_c_re.shape)
    g_c_im = _block_diag_take(dcmi, SSM_C, SSM_P).reshape(ssm_c_im.shape)
    dlr, dli, ddt, g_b_re, g_b_im = _ssm_prep_bwd(lr_c, li_c, dt_c, b_re2, b_im2, col(da_r), col(da_i), dbbr, dbbi)
    g_lam_re, g_lam_im = dlr.reshape(ssm_lambda_re.shape), dli.reshape(ssm_lambda_im.shape)
    g_log_dt = jnp.sum(ddt.reshape(SSM_G, SSM_P), axis=1).reshape(ssm_log_dt.shape)
    g_b_re, g_b_im = g_b_re.reshape(ssm_b_re.shape), g_b_im.reshape(ssm_b_im.shape)

    dq, dk, dv = _attn_bwd(z, dya, ya, mg, den)
    (dz,), _ = _rowwise("mix_dz", lambda *a: ([jnp.concatenate(a, axis=-1)], []), S, TR,
                        [_full(dq), _full(dk), _full(dv), _full(du)], [], [(ZW, ZW, _c0, BF16)])
    dun = _mm_nt("mix_in_dx", dz, Win, tm=512, tn=512, jb=NDEV, jobs=jobs)
    scatter("in", _mm_tn("mix_in_dw", un, dz, NDEV, tn=512, jobs=jobs))
    dh1, dh1b, g_mix_norm = norm_bwd("mix", h1, mix_norm, dun, dh2)

    dx, _dxb, g_ffn1_norm = ffn_bwd("ffn1", xs, ffn1_norm, Wgu1, Wd1, (xn1, gu1, hid1), dh1, dh1b)
    for run in (late[2], late[0], late[3], late[1]):
        run()

    out = {}

    def upd(name, key, *, tr, cw, gw, goff=0):
        w, m, v = A[name][0], A["m_" + name][0], A["v_" + name][0]
        g, dlt, mn, vn = _adamw("adamw_" + name, w, m, v, scat[key].finish(), tr=tr, cw=cw, gw=gw, goff=goff, jobs=jobs)
        for k, val in (("grad_", g), ("delta_", dlt), ("new_m_", mn), ("new_v_", vn)):
            out[k + name] = val[None]

    DT = _tile(D, 256)
    FT = _tile(FSH, 512)
    DC = _tile(D, 1024, LANE)
    upd("ffn2_w_gate", "ffn2gu", tr=DT, cw=FSH, gw=FSP, goff=0)
    upd("ffn2_w_up", "ffn2gu", tr=DT, cw=FSH, gw=FSP, goff=1)
    upd("ffn2_w_down", "ffn2d", tr=FT, cw=DC, gw=DC)
    upd("w_in", "in", tr=DT, cw=w_in.shape[-1], gw=w_in.shape[-1])
    upd("ffn1_w_gate", "ffn1gu", tr=DT, cw=FSH, gw=FSP, goff=0)
    upd("ffn1_w_up", "ffn1gu", tr=DT, cw=FSH, gw=FSP, goff=1)
    upd("ffn1_w_down", "ffn1d", tr=FT, cw=DC, gw=DC)
    upd("w_out", "out", tr=w_out.shape[1], cw=DC, gw=DC)
    upd("ple_w_gate", "pg", tr=ple_w_gate.shape[1], cw=DC, gw=DC)
    upd("ssm_w_glu", "glu", tr=ssm_w_glu.shape[1], cw=SSM_W, gw=SSM_W)
    upd("ple_w_proj", "pp", tr=ple_w_proj.shape[1], cw=ple_w_proj.shape[2], gw=ple_w_proj.shape[2])

    small = [("ffn1_norm", g_ffn1_norm), ("mix_norm", g_mix_norm), ("attn_out_norm", g_attn_norm),
             ("ssm_lambda_re", g_lam_re), ("ssm_lambda_im", g_lam_im), ("ssm_log_dt", g_log_dt),
             ("ssm_b_re", g_b_re), ("ssm_b_im", g_b_im), ("ssm_c_re", g_c_re), ("ssm_c_im", g_c_im),
             ("ssm_d", g_ssm_d), ("ssm_b_glu", g_bglu), ("ssm_out_norm", g_ssm_norm), ("ffn2_norm", g_ffn2_norm),
             ("ple_norm", g_ple_norm), ("final_norm", g_final)]
    chunk = 8 * LANE

    def pack(arrs):
        parts = []
        for a in arrs:
            flat = a.reshape(-1)
            padn = -(-flat.shape[0] // chunk) * chunk
            parts.append(jnp.pad(flat, (0, padn - flat.shape[0])).reshape(-1, LANE))
        return jnp.concatenate(parts, axis=0)

    g_pack = pack([g for _, g in small])
    (g_all,) = _all_gather("ag_small", [g_pack])
    g_sum = _sum8("small_sum", g_all)
    w_pack = pack([A[n] for n, _ in small])
    m_pack = pack([A["m_" + n] for n, _ in small])
    v_pack = pack([A["v_" + n] for n, _ in small])
    d_pack, mn_pack, vn_pack = _adamw_small("adamw_small", w_pack, m_pack, v_pack, g_sum)
    off = 0
    for n, _ in small:
        shape = A[n].shape
        size = math.prod(shape)
        rows = -(-size // chunk) * 8
        for k, buf in (("grad_", g_sum), ("delta_", d_pack), ("new_m_", mn_pack), ("new_v_", vn_pack)):
            out[k + n] = buf[off:off + rows].reshape(-1)[:size].reshape(shape)
        off += rows

    names = ['ffn1_norm', 'ffn1_w_gate', 'ffn1_w_up', 'ffn1_w_down', 'mix_norm', 'w_in', 'attn_out_norm',
             'ssm_lambda_re', 'ssm_lambda_im', 'ssm_log_dt', 'ssm_b_re', 'ssm_b_im', 'ssm_c_re', 'ssm_c_im', 'ssm_d',
             'ssm_w_glu', 'ssm_b_glu', 'ssm_out_norm', 'w_out', 'ffn2_norm', 'ffn2_w_gate', 'ffn2_w_up', 'ffn2_w_down',
             'ple_norm', 'ple_w_gate', 'ple_w_proj', 'final_norm']
    return (loss, dx[None], *[out[k + n] for k in ("grad_", "delta_", "new_m_", "new_v_") for n in names])
```

```python
import functools
import math

import jax
import jax.numpy as jnp
from jax import lax
from jax.experimental import pallas as pl
from jax.experimental.pallas import tpu as pltpu

F32, BF16 = jnp.float32, jnp.bfloat16
MESH = pl.DeviceIdType.MESH
NDEV = 8
AXES = ("x", "y", "c")
LANE = 128
VMEM_LIMIT = 60 * 1024 * 1024

ATTN_W = 1024
HEAD_DIM = 64
SSM_W = 1024
MIX_W = ATTN_W + SSM_W
SSM_G, SSM_P, SSM_C = 64, 64, 16
PACK = 8
DILATIONS = (1, 4, 16)
QB = 128
NORM_EPS = 1e-6
MASK_VALUE = -1e30
LR, B1, B2, EPS, WD, STEP = 0.001, 0.9, 0.999, 1e-08, 0.01, 10


def _cp(sem=None):
    return pltpu.CompilerParams(dimension_semantics=sem, vmem_limit_bytes=VMEM_LIMIT)


def _tile(n, target, mult=8):
    if n <= target:
        return n
    for t in range(target - target % mult, 0, -mult):
        if n % t == 0:
            return t
    return n


def _rms(x, g):
    return x * lax.rsqrt(jnp.mean(x * x, axis=-1, keepdims=True) + NORM_EPS) * g


def _rowwise(name, fn, S, tr, rows, fulls, outs, accs=(), ncol=1):
    nr, nf, no, na = len(rows), len(fulls), len(outs), len(accs)

    def body(*refs):
        ins = [r[...] for r in refs[:nr + nf]]
        o_refs = refs[nr + nf:nr + nf + no]
        a_refs = refs[nr + nf + no:]
        o_vals, a_vals = fn(*ins)
        for r, v in zip(o_refs, o_vals):
            r[...] = v.astype(r.dtype)
        if na:
            @pl.when(pl.program_id(1) == 0)
            def _():
                for r in a_refs:
                    r[...] = jnp.zeros_like(r)
            for r, v in zip(a_refs, a_vals):
                r[...] += v

    in_specs = [pl.BlockSpec((tr, w), functools.partial(lambda j, i, cm: (i, cm(j)), cm=cm)) for _, w, cm in rows]
    in_specs += [pl.BlockSpec(f.shape, functools.partial(lambda j, i, nd: (0,) * nd, nd=f.ndim)) for f in fulls]
    out_specs = [pl.BlockSpec((tr, w), functools.partial(lambda j, i, cm: (i, cm(j)), cm=cm)) for _, w, cm, _ in outs]
    out_specs += [pl.BlockSpec((1, w), functools.partial(lambda j, i, cm: (0, cm(j)), cm=cm)) for _, w, cm in accs]
    out_shape = [jax.ShapeDtypeStruct((S, c), dt) for c, _, _, dt in outs]
    out_shape += [jax.ShapeDtypeStruct((1, c), F32) for c, _, _ in accs]
    res = pl.pallas_call(
        body, name=name, grid=(ncol, S // tr), in_specs=in_specs, out_specs=out_specs, out_shape=out_shape,
        compiler_params=_cp(("parallel", "arbitrary" if na else "parallel")),
    )(*[a for a, _, _ in rows], *fulls)
    return res[:no], res[no:]


def _c0(j):
    return 0


def _full(a):
    return (a, a.shape[1], _c0)


def _hosted(name, body, grid, in_specs, out_specs, out_shape, scratch, args, sem, rider=None, jobs=None):
    nsteps = math.prod(grid)

    def step_of(*g):
        t = 0
        for gi, n in zip(g, grid):
            t = t * n + gi
        return t

    job = None
    if rider is None:
        job, rider = _pick(jobs, nsteps)
    if rider is None:
        outs = pl.pallas_call(body, name=name, grid=grid, in_specs=in_specs, out_specs=out_specs, out_shape=out_shape,
                              scratch_shapes=scratch, compiler_params=_cp(sem))(*args)
        return outs, None
    rider.bind(step_of, nsteps)
    n_in, n_out, n_scr = len(in_specs), len(out_specs), len(scratch)

    def full(*refs):
        a, b = n_in, n_in + rider.n_in
        c, d = b + n_out, b + n_out + rider.n_out
        rider.run(refs[a:b], refs[c:d], refs[d + n_scr:], step_of(*[pl.program_id(i) for i in range(len(grid))]), nsteps)
        body(*(refs[:a] + refs[b:c] + refs[d:d + n_scr]))

    outs = pl.pallas_call(
        full, name=name, grid=grid, in_specs=in_specs + rider.in_specs, out_specs=out_specs + rider.out_specs,
        out_shape=out_shape + rider.out_shape, scratch_shapes=scratch + rider.scratch,
        compiler_params=_cp(("arbitrary",) * len(grid)))(*args, *rider.operands)
    extra = rider.take(outs[n_out:])
    if job is not None:
        job.advance(extra)
        extra = None
    return outs[:n_out], extra


def _mm_nn(name, a, w, *, out_dtype=F32, tm=512, tn=768, tk=2048, res=None, scale=1.0, gain=None, rider=None,
           jobs=None):
    M, K = a.shape
    J, K2, Np = w.shape
    assert K == K2
    tm, tn, tk = _tile(M, tm), _tile(Np, tn, LANE), _tile(K, tk, LANE)
    npj = Np // tn
    nk = K // tk
    grid = (M // tm, J * npj, nk)
    assert gain is None or (J * npj == 1 and res is not None)

    def body(*refs):
        refs = list(refs)
        a_ref, w_ref = refs[:2]
        r_ref = refs[2] if res is not None else None
        g_ref = refs[3] if gain is not None else None
        acc = refs[-1]
        o_ref = refs[-3] if gain is not None else refs[-2]
        k = pl.program_id(2)
        part = jnp.dot(a_ref[...].astype(BF16), w_ref[...], preferred_element_type=F32)

        def finish(v):
            if res is not None:
                v = r_ref[...] + scale * v
            o_ref[...] = v.astype(o_ref.dtype)
            if gain is not None:
                refs[-2][...] = _rms(v, g_ref[...]).astype(BF16)

        if nk == 1:
            finish(part)
            return

        @pl.when(k == 0)
        def _():
            acc[...] = part

        @pl.when(k > 0)
        def _():
            acc[...] += part

        @pl.when(k == nk - 1)
        def _():
            finish(acc[...])

    in_specs = [pl.BlockSpec((tm, tk), lambda i, n, k: (i, k)),
                pl.BlockSpec((None, tk, tn), lambda i, n, k: (n // npj, k, n % npj))]
    args = [a, w]
    if res is not None:
        in_specs.append(pl.BlockSpec((tm, tn), lambda i, n, k: (i, n)))
        args.append(res)
    out_specs = [pl.BlockSpec((tm, tn), lambda i, n, k: (i, n))]
    out_shape = [jax.ShapeDtypeStruct((M, J * Np), out_dtype)]
    if gain is not None:
        in_specs.append(pl.BlockSpec((1, tn), lambda i, n, k: (0, 0)))
        args.append(gain)
        out_specs.append(pl.BlockSpec((tm, tn), lambda i, n, k: (i, n)))
        out_shape.append(jax.ShapeDtypeStruct((M, J * Np), BF16))
    outs, extra = _hosted(name, body, grid, in_specs, out_specs, out_shape, [pltpu.VMEM((tm, tn), F32)], args,
                          ("parallel", "parallel", "arbitrary"), rider, jobs)
    outs = tuple(outs) + (() if rider is None else (extra,))
    return outs[0] if len(outs) == 1 else outs


def _mm_nt(name, dy, w, *, out_dtype=F32, tm=512, tn=2048, tk=768, jb=1, scale=1.0, jobs=None):
    M, N = dy.shape
    J, K, Np = w.shape
    assert N == J * Np
    tm, tn, tk = _tile(M, tm), _tile(K, tn, LANE), (Np if jb > 1 else _tile(Np, tk, LANE))
    npj = Np // tk
    nc = J * npj // jb
    nt = (((1,), (1,)), ((), ()))

    def body(a_ref, w_ref, o_ref, acc):
        c = pl.program_id(2)
        if jb == 1:
            part = lax.dot_general(a_ref[...].astype(BF16), w_ref[...], nt, preferred_element_type=F32)
        else:
            part = sum(lax.dot_general(a_ref[:, j * Np:(j + 1) * Np].astype(BF16), w_ref[j], nt,
                                       preferred_element_type=F32) for j in range(jb))
        if nc == 1:
            o_ref[...] = (scale * part).astype(o_ref.dtype)
            return

        @pl.when(c == 0)
        def _():
            acc[...] = part

        @pl.when(c > 0)
        def _():
            acc[...] += part

        @pl.when(c == nc - 1)
        def _():
            o_ref[...] = (scale * acc[...]).astype(o_ref.dtype)

    w_spec = (pl.BlockSpec((None, tn, tk), lambda i, n, c: (c // npj, n, c % npj)) if jb == 1 else
              pl.BlockSpec((jb, tn, Np), lambda i, n, c: (c, n, 0)))
    (out,), _ = _hosted(
        name, body, (M // tm, K // tn, nc), [pl.BlockSpec((tm, jb * tk), lambda i, n, c: (i, c)), w_spec],
        [pl.BlockSpec((tm, tn), lambda i, n, c: (i, n))], [jax.ShapeDtypeStruct((M, K), out_dtype)],
        [pltpu.VMEM((tm, tn) if nc > 1 else (8, LANE), F32)], (dy, w), ("parallel", "parallel", "arbitrary"), None, jobs)
    return out


def _mm_tn(name, x, dy, J, *, tm=8192, tko=256, tn=512, scale=1.0, jobs=None):
    M, K = x.shape
    M2, N = dy.shape
    assert M == M2 and N % J == 0
    Np = N // J
    tm, tko, tn = _tile(M, tm, LANE), _tile(K, tko, LANE), _tile(Np, tn, LANE)
    npj = Np // tn
    nm = M // tm

    def body(x_ref, d_ref, o_ref, acc):
        m = pl.program_id(2)
        part = lax.dot_general(x_ref[...].astype(BF16), d_ref[...].astype(BF16), (((0,), (0,)), ((), ())),
                               preferred_element_type=F32)
        if nm == 1:
            o_ref[...] = scale * part
            return

        @pl.when(m == 0)
        def _():
            acc[...] = part

        @pl.when(m > 0)
        def _():
            acc[...] += part

        @pl.when(m == nm - 1)
        def _():
            o_ref[...] = scale * acc[...]

    nk, nn = K // tko, J * npj
    k_outer = x.size + nk * dy.size <= dy.size + nn * x.size
    kn = (lambda a, b: (a, b)) if k_outer else (lambda a, b: (b, a))
    (out,), _ = _hosted(
        name, body, (nk, nn, nm) if k_outer else (nn, nk, nm),
        [pl.BlockSpec((tm, tko), lambda a, b, m: (m, kn(a, b)[0])),
         pl.BlockSpec((tm, tn), lambda a, b, m: (m, kn(a, b)[1]))],
        [pl.BlockSpec((None, tko, tn), lambda a, b, m: (kn(a, b)[1] // npj, kn(a, b)[0], kn(a, b)[1] % npj))],
        [jax.ShapeDtypeStruct((J, K, Np), F32)], [pltpu.VMEM((tko, tn) if nm > 1 else (8, LANE), F32)], (x, dy),
        ("parallel", "parallel", "arbitrary"), None, jobs)
    return out


def _swiglu_act(g, u):
    return jax.nn.silu(g) * u


def _ffn_up(name, xn, wgu, *, tm=1024, rider=None):
    M, K = xn.shape
    J, _, F2 = wgu.shape
    F = F2 // 2
    tm = _tile(M, tm)

    def body(a_ref, w_ref, gu_ref, h_ref):
        r = jnp.dot(a_ref[...], w_ref[...], preferred_element_type=F32)
        gu_ref[...] = r.astype(gu_ref.dtype)
        h_ref[...] = _swiglu_act(r[:, :F], r[:, F:]).astype(h_ref.dtype)

    (gu, hid), extra = _hosted(
        name, body, (M // tm, J),
        [pl.BlockSpec((tm, K), lambda i, j: (i, 0)), pl.BlockSpec((None, K, F2), lambda i, j: (j, 0, 0))],
        [pl.BlockSpec((tm, F2), lambda i, j: (i, j)), pl.BlockSpec((tm, F), lambda i, j: (i, j))],
        [jax.ShapeDtypeStruct((M, J * F2), BF16), jax.ShapeDtypeStruct((M, J * F), BF16)], [], (xn, wgu),
        ("parallel", "parallel"), rider)
    return gu, hid, extra


def _ffn_down_dx(name, dout, wd, gu, J, *, scale, tm=512, jobs=None):
    M, D = dout.shape
    F = wd.shape[1] // J
    tm = _tile(M, tm)

    def body(d_ref, w_ref, gu_ref, o_ref):
        dh = scale * lax.dot_general(d_ref[...], w_ref[...], (((1,), (1,)), ((), ())), preferred_element_type=F32)
        gu = gu_ref[...].astype(F32)
        _, vjp = jax.vjp(_swiglu_act, gu[:, :F], gu[:, F:])
        o_ref[...] = jnp.concatenate(vjp(dh), axis=-1).astype(o_ref.dtype)

    (out,), _ = _hosted(
        name, body, (M // tm, J),
        [pl.BlockSpec((tm, D), lambda i, j: (i, 0)), pl.BlockSpec((None, F, D), lambda i, j: (0, j, 0)),
         pl.BlockSpec((tm, 2 * F), lambda i, j: (i, j))],
        [pl.BlockSpec((tm, 2 * F), lambda i, j: (i, j))], [jax.ShapeDtypeStruct((M, J * 2 * F), BF16)], [],
        (dout, wd, gu), ("parallel", "parallel"), None, jobs)
    return out


def _all_gather(name, shards):
    n = len(shards)

    def body(*refs):
        start, forward, finish = _gather_phases(refs[:n], refs[n:2 * n], *refs[2 * n:])
        start()
        forward()
        finish()

    any_spec = pl.BlockSpec(memory_space=pl.ANY)
    return pl.pallas_call(
        body, name=name, in_specs=[any_spec] * n, out_specs=[any_spec] * n,
        out_shape=[jax.ShapeDtypeStruct((NDEV,) + s.shape, s.dtype) for s in shards],
        scratch_shapes=_gather_sems(n),
    )(*shards)


def _gather_sems(n):
    return [pltpu.SemaphoreType.DMA((n, 7)), pltpu.SemaphoreType.DMA((n, 7)), pltpu.SemaphoreType.DMA((n,))]


def _gather_phases(ins, outs, send_sems, recv_sems, local_sems):
    n = len(ins)
    x, y, c = lax.axis_index("x"), lax.axis_index("y"), lax.axis_index("c")
    me, sibling = (x, y, c), (x, y, 1 - c)
    chips = [(1 - x, y), (x, 1 - y), (1 - x, 1 - y)]

    def blk(i, px, py, pc):
        return outs[i].at[4 * px + 2 * py + pc]

    def copy(i, k, block, to, src=None):
        return pltpu.make_async_remote_copy(
            src_ref=blk(i, *block) if src is None else src, dst_ref=blk(i, *block),
            send_sem=send_sems.at[i, k], recv_sem=recv_sems.at[i, k], device_id=to, device_id_type=MESH)

    def local(i):
        return pltpu.make_async_copy(ins[i], blk(i, *me), local_sems.at[i])

    def firsts(i):
        return [copy(i, 0, me, sibling, src=ins[i])] + [copy(i, 1 + j, me, (*chip, c), src=ins[i])
                                                        for j, chip in enumerate(chips)]

    def start():
        for i in range(n):
            local(i).start()
        for i in range(n):
            for cp in firsts(i):
                cp.start()

    def forward():
        for i in range(n):
            for j, chip in enumerate(chips):
                copy(i, 1 + j, (*chip, c), me).wait_recv()
                copy(i, 4 + j, (*chip, c), sibling).start()

    def finish():
        for i in range(n):
            copy(i, 0, sibling, me).wait_recv()
            for j, chip in enumerate(chips):
                copy(i, 4 + j, (*chip, 1 - c), me).wait_recv()
        for i in range(n):
            for cp in firsts(i):
                cp.wait_send()
            for j, chip in enumerate(chips):
                copy(i, 4 + j, (*chip, c), sibling).wait_send()
            local(i).wait()

    return start, forward, finish


class _GatherRider:
    def __init__(self, shards):
        self.operands = list(shards)
        n = len(self.operands)
        self.n_in = self.n_out = n
        any_spec = pl.BlockSpec(memory_space=pl.ANY)
        self.in_specs = [any_spec] * n
        self.out_specs = [any_spec] * n
        self.out_shape = [jax.ShapeDtypeStruct((NDEV,) + s.shape, s.dtype) for s in self.operands]
        self.scratch = _gather_sems(n)

    def bind(self, step_of, nsteps):
        return self

    def take(self, outs):
        return list(outs)

    def run(self, ins, outs, sems, step, nsteps):
        start, forward, finish = _gather_phases(ins, outs, *sems)
        pl.when(step == 0)(start)
        pl.when(step == (4 * nsteps) // 5)(forward)
        pl.when(step == nsteps - 1)(finish)


class _SwapRider:
    def __init__(self, arr, streams, grid, tile, out_shape, out_block, out_map):
        self.arr, self.streams, self.grid, self.tile = arr, streams, grid, tile
        self.ns, self.n = len(streams), grid[0] * grid[1]
        self.operands = [arr] * (2 * self.ns)
        self.n_in, self.n_out = 2 * self.ns, 1
        self.out_shape = [jax.ShapeDtypeStruct(out_shape, F32)]
        self.out_block, self.out_map = out_block, out_map
        tr, C = tile
        slots = [pltpu.VMEM((2, tr, C), w) for _, w, _, _ in streams]
        self.scratch = slots + slots + [pltpu.SemaphoreType.DMA((self.ns, 2)), pltpu.SemaphoreType.DMA((self.ns, 2)),
                                        pltpu.SemaphoreType.REGULAR((self.ns,))]

    def bind(self, step_of, nsteps):
        assert nsteps >= self.n
        self.period = period = nsteps // self.n
        n, nr = self.n, self.grid[1]

        def ids(*g):
            k = jnp.minimum(step_of(*g) // period, n - 1)
            pos = {a: lax.axis_index(a) for a in AXES}
            return k // nr, k % nr, [v for a in AXES for v in (pos[a], 1 - pos[a])]

        block = (None,) * (self.arr.ndim - 2) + tuple(self.tile)
        self.in_specs = []
        for _, _, keep_map, send_map in self.streams:
            for m in (keep_map, send_map):
                self.in_specs.append(pl.BlockSpec(block, functools.partial(lambda *g, m: m(*ids(*g)), m=m)))
        self.out_specs = [pl.BlockSpec(self.out_block, lambda *g: self.out_map(*ids(*g)))]
        return self

    def take(self, outs):
        return outs[0]

    def run(self, ins, outs, scratch, step, nsteps):
        ns, n, period = self.ns, self.n, self.period
        keeps, sends, o_ref = ins[0::2], ins[1::2], outs[0]
        lands, stages = scratch[:ns], scratch[ns:2 * ns]
        send_sems, recv_sems, credits = scratch[2 * ns:]
        k = step // period
        slot = k % 2
        here = {a: lax.axis_index(a) for a in AXES}
        peers = [tuple(1 - here[a] if a == axis else here[a] for a in AXES) for axis, _, _, _ in self.streams]

        def rdma(s):
            return pltpu.make_async_remote_copy(
                src_ref=stages[s].at[slot], dst_ref=lands[s].at[slot], send_sem=send_sems.at[s, slot],
                recv_sem=recv_sems.at[s, slot], device_id=peers[s], device_id_type=MESH)

        @pl.when((k < n) & (step % period == 0))
        def _():
            @pl.when(k >= 2)
            def _():
                for s in range(ns):
                    pl.semaphore_wait(credits.at[s], 1)

            for s in range(ns):
                stages[s][slot] = sends[s][...].astype(stages[s].dtype)
                rdma(s).start()

        @pl.when((k < n) & (step % period == period - 1))
        def _():
            for s in range(ns):
                rdma(s).wait_recv()
                total = keeps[s][...] + lands[s][slot].astype(F32)
                if ns == 1:
                    o_ref[...] = total
                else:
                    o_ref[s] = total
            for s in range(ns):
                rdma(s).wait_send()

            @pl.when(k + 2 < n)
            def _():
                for s in range(ns):
                    pl.semaphore_signal(credits.at[s], inc=1, device_id=peers[s], device_id_type=MESH)


def _run_alone(name, rider):
    rider.bind(lambda t: t, rider.n)

    def body(*refs):
        a, b = rider.n_in, rider.n_in + rider.n_out
        rider.run(refs[:a], refs[a:b], refs[b:], pl.program_id(0), rider.n)

    outs = pl.pallas_call(
        body, name=name, grid=(rider.n,), in_specs=rider.in_specs, out_specs=rider.out_specs,
        out_shape=rider.out_shape, scratch_shapes=rider.scratch, compiler_params=_cp(("arbitrary",)),
    )(*rider.operands)
    return rider.take(outs)


class _Scatter:
    def __init__(self, name, g):
        self.name, self.cur, self.stage = name, g, 0
        _, self.R, self.C = g.shape

    def done(self):
        return self.stage == 3

    def rider(self, rows):
        R, C = self.R, self.C
        R2 = R // 2
        tr = _tile(R2, rows, 16)
        nrh = R2 // tr
        if self.stage == 0:
            return _SwapRider(
                self.cur.reshape(4, 2, R, C),
                [("c", BF16, lambda b, i, s: (b, s[4], i, 0), lambda b, i, s: (b, s[5], i, 0))],
                (4, 2 * nrh), (tr, C), (2, 4, R2, C), (None, None, tr, C), lambda b, i, s: (i // nrh, b, i % nrh, 0))
        if self.stage == 1:
            return _SwapRider(
                self.cur.reshape(2, 2, 2, R2, C),
                [("y", BF16, lambda b, i, s: (0, b, s[2], i, 0), lambda b, i, s: (0, b, s[3], i, 0)),
                 ("x", BF16, lambda b, i, s: (1, s[0], b, i, 0), lambda b, i, s: (1, s[1], b, i, 0))],
                (2, nrh), (tr, C), (2, 2, R2, C), (2, None, tr, C), lambda b, i, s: (0, b, i, 0))
        return _SwapRider(
            self.cur,
            [("x", BF16, lambda b, i, s: (0, s[0], i, 0), lambda b, i, s: (0, s[1], i, 0)),
             ("y", BF16, lambda b, i, s: (1, s[2], i, 0), lambda b, i, s: (1, s[3], i, 0))],
            (1, nrh), (tr, C), (2, R2, C), (2, tr, C), lambda b, i, s: (0, i, 0))

    def advance(self, out):
        self.cur, self.stage = out, self.stage + 1

    def finish(self):
        while not self.done():
            self.advance(_run_alone(f"{self.name}_s{self.stage}", self.rider(256)))
        return self.cur.reshape(self.R, self.C)


RIDER_TILE_BYTES = 3 * 512 * 1024


def _pick(jobs, nsteps):
    for job in sorted(jobs or (), key=lambda j: -j.R * j.C):
        if job.done():
            continue
        streams = 1 if job.stage == 0 else 2
        riders = [job.rider(rows) for rows in (512, 256, 128, 64)
                  if rows * job.C * 4 * streams <= RIDER_TILE_BYTES or rows == 64]
        for rider in riders:
            if 2 * rider.n <= nsteps:
                return job, rider
        if riders[-1].n <= nsteps:
            return job, riders[-1]
    return None, None


def _sum8(name, g):
    _, R, C = g.shape
    tr = _tile(R, 512)

    def body(g_ref, o_ref):
        acc = g_ref[0]
        for d in range(1, NDEV):
            acc = acc + g_ref[d]
        o_ref[...] = acc

    return pl.pallas_call(
        body, name=name, grid=(R // tr,), in_specs=[pl.BlockSpec((NDEV, tr, C), lambda i: (0, i, 0))],
        out_specs=pl.BlockSpec((tr, C), lambda i: (i, 0)), out_shape=jax.ShapeDtypeStruct((R, C), F32),
        compiler_params=_cp(("parallel",)),
    )(g)


def _adamw_math(w, g, m, v):
    m = B1 * m + (1.0 - B1) * g
    v = B2 * v + (1.0 - B2) * jnp.square(g)
    m_hat = m / (1.0 - B1 ** STEP)
    v_hat = v / (1.0 - B2 ** STEP)
    delta = -LR * (m_hat / (jnp.sqrt(v_hat) + EPS) + WD * w)
    return delta, m, v


def _adamw(name, w, m, v, gp, *, tr, cw, gw, goff=0, jobs=None):
    R, C = w.shape
    nc = C // cw
    nr = R // tr

    def body(w_ref, m_ref, v_ref, g_ref, g_out, d_out, m_out, v_out):
        g = g_ref[...][:, :cw]
        d, mn, vn = _adamw_math(w_ref[...], g, m_ref[...], v_ref[...])
        g_out[...] = g
        d_out[...] = d
        m_out[...] = mn
        v_out[...] = vn

    wspec = pl.BlockSpec((tr, cw), lambda i, j: (i, j))
    gspec = pl.BlockSpec((tr, gw), lambda i, j: (i, goff + j))
    outs, _ = _hosted(name, body, (nr, nc), [wspec, wspec, wspec, gspec], [wspec] * 4,
                      [jax.ShapeDtypeStruct((R, C), F32)] * 4, [], (w, m, v, gp), ("parallel", "parallel"), None, jobs)
    return outs


def _adamw_small(name, w, m, v, g):
    R, C = w.shape

    def body(w_ref, m_ref, v_ref, g_ref, d_out, m_out, v_out):
        d, mn, vn = _adamw_math(w_ref[...], g_ref[...], m_ref[...], v_ref[...])
        d_out[...] = d
        m_out[...] = mn
        v_out[...] = vn

    tr = _tile(R, 512)
    spec = pl.BlockSpec((tr, C), lambda i: (i, 0))
    return pl.pallas_call(
        body, name=name, grid=(R // tr,), in_specs=[spec] * 4, out_specs=[spec] * 3,
        out_shape=[jax.ShapeDtypeStruct((R, C), F32)] * 3, compiler_params=_cp(("parallel",)),
    )(w, m, v, g)


def _prep(name, parts, rows_p, cols_p):
    R, C = parts[0].shape
    n = len(parts)

    def body(*refs):
        o_ref = refs[n]
        if (R, C) != (rows_p, cols_p):
            o_ref[...] = jnp.zeros_like(o_ref)
        for i in range(n):
            o_ref[0:R, i * cols_p:i * cols_p + C] = refs[i][...].astype(BF16)

    return pl.pallas_call(
        body, name=name, out_shape=jax.ShapeDtypeStruct((rows_p, n * cols_p), BF16), compiler_params=_cp(),
    )(*parts)


def _attn_masks():
    lane = lax.broadcasted_iota(jnp.int32, (1, LANE), 1)
    return [(lane < HEAD_DIM), (lane >= HEAD_DIM)]


QH = QB
KW = QB + QH


def _band_valid(base):
    qi = lax.broadcasted_iota(jnp.int32, (QH, KW), 0)
    ki = lax.broadcasted_iota(jnp.int32, (QH, KW), 1)
    dist = qi + QB - ki
    return (dist >= 0) & (dist <= QB) & (base + ki - QB >= 0)


ATTN_T = max(DILATIONS) * QB


def _attn_groups(T):
    out = []
    for d in DILATIONS:
        for r in range(d):
            for l0 in range(0, T // d, QH):
                qrows = pl.ds(r + d * l0, QH, stride=d) if d > 1 else pl.ds(l0, QH)
                k0 = T + r + d * (l0 - QB)
                krows = pl.ds(k0, KW, stride=d) if d > 1 else pl.ds(k0, KW)
                out.append((d, qrows, krows, l0))
    return out


def _attn_specs(T, width_off):
    cur = pl.BlockSpec((T, LANE), lambda hp, b: (b, width_off + hp))
    prev = pl.BlockSpec((T, LANE), lambda hp, b: (jnp.maximum(b - 1, 0), width_off + hp))
    return cur, prev


def _attn_fwd(z):
    S, ZW = z.shape
    T = min(ATTN_T, S)
    scale = HEAD_DIM ** -0.5
    groups = _attn_groups(T)

    def body(q_ref, kc_ref, kp_ref, vc_ref, vp_ref, y_ref, m_ref, l_ref, kcat, vcat):
        b = pl.program_id(1)
        kcat[0:T, :] = kp_ref[...]
        kcat[T:, :] = kc_ref[...]
        vcat[0:T, :] = vp_ref[...]
        vcat[T:, :] = vc_ref[...]
        masks = _attn_masks()
        for d, qrows, krows, l0 in groups:
            q = q_ref[qrows, :]
            kk = kcat[krows, :].astype(BF16)
            vv = vcat[krows, :].astype(BF16)
            valid = _band_valid(b * (T // d) + l0)
            o_new = m_new = l_new = None
            for hm in masks:
                qh = jnp.where(hm, q, 0.0).astype(BF16)
                s = lax.dot_general(qh, kk, (((1,), (1,)), ((), ())), preferred_element_type=F32) * scale
                s = jnp.where(valid, s, MASK_VALUE)
                m = jnp.max(s, axis=-1, keepdims=True)
                p = jnp.exp(s - m)
                l = jnp.sum(p, axis=-1, keepdims=True)
                o = jnp.dot(p.astype(BF16), vv, preferred_element_type=F32)
                if o_new is None:
                    o_new, m_new, l_new = o, jnp.broadcast_to(m, (QH, LANE)), jnp.broadcast_to(l, (QH, LANE))
                else:
                    o_new = jnp.where(hm, o, o_new)
                    m_new = jnp.where(hm, m, m_new)
                    l_new = jnp.where(hm, l, l_new)
            if d == DILATIONS[0]:
                y_ref[qrows, :] = o_new
                m_ref[qrows, :] = m_new
                l_ref[qrows, :] = l_new
            else:
                m_old = m_ref[qrows, :]
                m_all = jnp.maximum(m_old, m_new)
                w_old, w_new = jnp.exp(m_old - m_all), jnp.exp(m_new - m_all)
                y_ref[qrows, :] = w_old * y_ref[qrows, :] + w_new * o_new
                l_ref[qrows, :] = w_old * l_ref[qrows, :] + w_new * l_new
                m_ref[qrows, :] = m_all
        y_ref[...] = y_ref[...] / l_ref[...]

    qc, _ = _attn_specs(T, 0)
    kc, kp = _attn_specs(T, ATTN_W // LANE)
    vc, vp = _attn_specs(T, 2 * ATTN_W // LANE)
    shp = jax.ShapeDtypeStruct((S, ATTN_W), F32)
    return pl.pallas_call(
        body, name="attn_fwd", grid=(ATTN_W // LANE, S // T),
        in_specs=[qc, kc, kp, vc, vp], out_specs=[qc, qc, qc], out_shape=[shp, shp, shp],
        scratch_shapes=[pltpu.VMEM((2 * T, LANE), F32), pltpu.VMEM((2 * T, LANE), F32)],
        compiler_params=_cp(("parallel", "parallel")),
    )(z, z, z, z, z)


def _attn_bwd(z, dya, ya, mg, den, jobs=None):
    S, ZW = z.shape
    T = min(ATTN_T, S)
    scale = HEAD_DIM ** -0.5
    groups = _attn_groups(T)

    def body(q_ref, kc_ref, kp_ref, vc_ref, vp_ref, dy_ref, y_ref, m_ref, n_ref, dq_ref, dk_ref, dv_ref,
             kcat, vcat, dkcat, dvcat):
        b = pl.program_id(1)

        @pl.when(b == 0)
        def _():
            dk_ref[...] = jnp.zeros_like(dk_ref)
            dv_ref[...] = jnp.zeros_like(dv_ref)

        kcat[0:T, :] = kp_ref[...]
        kcat[T:, :] = kc_ref[...]
        vcat[0:T, :] = vp_ref[...]
        vcat[T:, :] = vc_ref[...]
        dkcat[...] = jnp.zeros_like(dkcat)
        dvcat[...] = jnp.zeros_like(dvcat)
        dq_ref[...] = jnp.zeros_like(dq_ref)
        masks = _attn_masks()
        for d, rows, krows, l0 in groups:
            q, dy, y = q_ref[rows, :], dy_ref[rows, :], y_ref[rows, :]
            mrow, nrow = m_ref[rows, :], n_ref[rows, :]
            kk = kcat[krows, :].astype(BF16)
            vv = vcat[krows, :].astype(BF16)
            valid = _band_valid(b * (T // d) + l0)
            dq_acc = jnp.zeros((QH, LANE), F32)
            dk_acc = jnp.zeros((KW, LANE), F32)
            dv_acc = jnp.zeros((KW, LANE), F32)
            for hm in masks:
                qh = jnp.where(hm, q, 0.0).astype(BF16)
                dyh = jnp.where(hm, dy, 0.0)
                dyb = dyh.astype(BF16)
                dsum = jnp.sum(dyh * y, axis=-1, keepdims=True)
                mh = jnp.max(jnp.where(hm, mrow, MASK_VALUE), axis=-1, keepdims=True)
                nh = jnp.max(jnp.where(hm, nrow, 0.0), axis=-1, keepdims=True)
                s = lax.dot_general(qh, kk, (((1,), (1,)), ((), ())), preferred_element_type=F32) * scale
                p = jnp.where(valid, jnp.exp(s - mh), 0.0) * (1.0 / nh)
                pb = p.astype(BF16)
                dv_h = lax.dot_general(pb, dyb, (((0,), (0,)), ((), ())), preferred_element_type=F32)
                dp = lax.dot_general(dyb, vv, (((1,), (1,)), ((), ())), preferred_element_type=F32)
                ds = (p * (dp - dsum) * scale).astype(BF16)
                dq_h = jnp.dot(ds, kk, preferred_element_type=F32)
                dk_h = lax.dot_general(ds, qh, (((0,), (0,)), ((), ())), preferred_element_type=F32)
                dq_acc += jnp.where(hm, dq_h, 0.0)
                dk_acc += dk_h
                dv_acc += dv_h
            dq_ref[rows, :] += dq_acc
            dkcat[krows, :] += dk_acc
            dvcat[krows, :] += dv_acc

        base = pl.multiple_of(b * T, T)
        dk_ref[pl.ds(base, T), :] += dkcat[T:, :]
        dv_ref[pl.ds(base, T), :] += dvcat[T:, :]

        @pl.when(b > 0)
        def _():
            prev = pl.multiple_of(b * T - T, T)
            dk_ref[pl.ds(prev, T), :] += dkcat[0:T, :]
            dv_ref[pl.ds(prev, T), :] += dvcat[0:T, :]

    qc, _ = _attn_specs(T, 0)
    kc, kp = _attn_specs(T, ATTN_W // LANE)
    vc, vp = _attn_specs(T, 2 * ATTN_W // LANE)
    whole = pl.BlockSpec((S, LANE), lambda hp, b: (0, hp))
    shp = jax.ShapeDtypeStruct((S, ATTN_W), F32)
    outs, _ = _hosted(
        "attn_bwd", body, (ATTN_W // LANE, S // T), [qc, kc, kp, vc, vp, qc, qc, qc, qc], [qc, whole, whole],
        [shp, shp, shp], [pltpu.VMEM((2 * T, LANE), F32)] * 4, (z, z, z, z, z, dya, ya, mg, den),
        ("parallel", "arbitrary"), None, jobs)
    return outs


def _ssm_disc(lr, li, logdt, br, bi):
    dt = jnp.exp(logdt)
    mag = jnp.exp(lr * dt)
    ar = mag * jnp.cos(li * dt)
    ai = mag * jnp.sin(li * dt)
    nr, ni = ar - 1.0, ai
    den = lr * lr + li * li
    cr = (nr * lr + ni * li) / den
    ci = (ni * lr - nr * li) / den
    return ar, ai, cr * br - ci * bi, cr * bi + ci * br


def _ssm_prep(lr, li, logdt, br, bi):
    n, c = br.shape
    outs, _ = _rowwise("ssm_prep", lambda *a: (list(_ssm_disc(*a)), []), n, _tile(n, 512),
                       [_full(a) for a in (lr, li, logdt, br, bi)], [],
                       [(1, 1, _c0, F32), (1, 1, _c0, F32), (c, c, _c0, F32), (c, c, _c0, F32)])
    return outs


def _ssm_prep_bwd(lr, li, logdt, br, bi, dar, dai, dbbr, dbbi):
    n, c = br.shape

    def f(lrb, lib, dtb, brb, bib, *cts):
        _, vjp = jax.vjp(_ssm_disc, lrb, lib, dtb, brb, bib)
        return list(vjp(cts)), []

    outs, _ = _rowwise("ssm_prep_bwd", f, n, _tile(n, 512),
                       [_full(a) for a in (lr, li, logdt, br, bi, dar, dai, dbbr, dbbi)], [],
                       [(1, 1, _c0, F32)] * 3 + [(c, c, _c0, F32)] * 2)
    return outs


def _cmul(ar, ai, br, bi):
    return ar * br - ai * bi, ar * bi + ai * br


def _scan_consts(ar, ai, reverse):
    w = ar.shape[-1]
    a1 = (jnp.broadcast_to(ar, (8, w)), jnp.broadcast_to(ai, (8, w)))
    a2 = _cmul(*a1, *a1)
    a4 = _cmul(*a2, *a2)
    a8 = _cmul(*a4, *a4)
    row = lax.broadcasted_iota(jnp.int32, (8, w), 0)
    e = (8 - row) if reverse else (row + 1)
    one, zero = jnp.ones((8, w), F32), jnp.zeros((8, w), F32)
    pw = (one, zero)
    for bit, ap in ((1, a1), (2, a2), (4, a4), (8, a8)):
        sel = (e & bit) != 0
        nxt = _cmul(*pw, *ap)
        pw = (jnp.where(sel, nxt[0], pw[0]), jnp.where(sel, nxt[1], pw[1]))
    steps = []
    for sh, (pr, pi) in zip((1, 2, 4), (a1, a2, a4)):
        keep = (row < 8 - sh) if reverse else (row >= sh)
        steps.append((jnp.where(keep, pr, 0.0), jnp.where(keep, pi, 0.0)))
    return steps, pw, row


def _scan_group(xr, xi, cr, ci, consts, reverse):
    steps, pw, _ = consts
    for sh, (pr, pi) in zip((1, 2, 4), steps):
        by = 8 - sh if reverse else sh
        tr_, ti_ = _cmul(pr, pi, pltpu.roll(xr, by, 0), pltpu.roll(xi, by, 0))
        xr = xr + tr_
        xi = xi + ti_
    tr_, ti_ = _cmul(pw[0], pw[1], cr, ci)
    return xr + tr_, xi + ti_


def _ssm_fwd(z, a_r, a_i, bdr, bdi, cmr, cmi, dskip, ts, rider=None):
    S, ZW = z.shape
    NS = SSM_G * SSM_P
    PW = PACK * SSM_P
    uoff = (ZW - SSM_W) // LANE
    nsteps = S // ts

    def body(u_ref, ar_ref, ai_ref, bdr_ref, bdi_ref, cmr_ref, cmi_ref, d_ref, hr_ref, hi_ref, y_ref, car_r, car_i):
        s = pl.program_id(1)

        @pl.when(s == 0)
        def _():
            car_r[...] = jnp.zeros_like(car_r)
            car_i[...] = jnp.zeros_like(car_i)

        u = u_ref[...]
        ub = u.astype(BF16)
        nt = (((1,), (1,)), ((), ()))
        hr_ref[...] = lax.dot_general(ub, bdr_ref[...], nt, preferred_element_type=F32)
        hi_ref[...] = lax.dot_general(ub, bdi_ref[...], nt, preferred_element_type=F32)
        consts = _scan_consts(ar_ref[...], ai_ref[...], False)

        def step(j, carry):
            rows = pl.ds(pl.multiple_of(j * 8, 8), 8)
            hr, hi = _scan_group(hr_ref[rows, :], hi_ref[rows, :], carry[0], carry[1], consts, False)
            hr_ref[rows, :] = hr
            hi_ref[rows, :] = hi
            return jnp.broadcast_to(hr[7:8, :], (8, PW)), jnp.broadcast_to(hi[7:8, :], (8, PW))

        cr, ci = lax.fori_loop(0, ts // 8, step, (car_r[...], car_i[...]))
        car_r[...] = cr
        car_i[...] = ci
        y = lax.dot_general(hr_ref[...].astype(BF16), cmr_ref[...], nt, preferred_element_type=F32)
        y -= lax.dot_general(hi_ref[...].astype(BF16), cmi_ref[...], nt, preferred_element_type=F32)
        y_ref[...] = y + d_ref[...] * u

    row_a = pl.BlockSpec((1, PW), lambda i, s: (0, i))
    (hr, hi, y), extra = _hosted(
        "ssm_fwd", body, (SSM_G // PACK, nsteps),
        [pl.BlockSpec((ts, LANE), lambda i, s: (s, uoff + i)), row_a, row_a,
         pl.BlockSpec((None, PW, LANE), lambda i, s: (i, 0, 0)), pl.BlockSpec((None, PW, LANE), lambda i, s: (i, 0, 0)),
         pl.BlockSpec((None, LANE, PW), lambda i, s: (i, 0, 0)), pl.BlockSpec((None, LANE, PW), lambda i, s: (i, 0, 0)),
         pl.BlockSpec((1, LANE), lambda i, s: (0, i))],
        [pl.BlockSpec((ts, PW), lambda i, s: (s, i)), pl.BlockSpec((ts, PW), lambda i, s: (s, i)),
         pl.BlockSpec((ts, LANE), lambda i, s: (s, i))],
        [jax.ShapeDtypeStruct((S, NS), F32), jax.ShapeDtypeStruct((S, NS), F32), jax.ShapeDtypeStruct((S, SSM_W), F32)],
        [pltpu.VMEM((8, PW), F32), pltpu.VMEM((8, PW), F32)], (z, a_r, a_i, bdr, bdi, cmr, cmi, dskip),
        ("parallel", "arbitrary"), rider)
    return hr, hi, y, extra


def _ssm_bwd(z, dyp, hr, hi, a_r, a_i, bdr, bdi, cmr, cmi, dskip, ts, jobs=None):
    S, ZW = z.shape
    NS = SSM_G * SSM_P
    PW = PACK * SSM_P
    uoff = (ZW - SSM_W) // LANE
    nsteps = S // ts
    npk = SSM_G // PACK

    def body(u_ref, dy_ref, hr_ref, hi_ref, hpr_ref, hpi_ref, ar_ref, ai_ref, bdr_ref, bdi_ref, cmr_ref, cmi_ref,
             d_ref, du_ref, dbdr_ref, dbdi_ref, dcmr_ref, dcmi_ref, dar_ref, dai_ref, dd_ref,
             lr_s, li_s, hcr, hci, car_r, car_i):
        s = pl.program_id(1)
        first_tile = s == nsteps - 1

        @pl.when(s == 0)
        def _():
            car_r[...] = jnp.zeros_like(car_r)
            car_i[...] = jnp.zeros_like(car_i)
            for r in (dbdr_ref, dbdi_ref, dcmr_ref, dcmi_ref, dar_ref, dai_ref, dd_ref):
                r[...] = jnp.zeros_like(r)

        u, dy = u_ref[...], dy_ref[...]
        ub, dyb = u.astype(BF16), dy.astype(BF16)
        lr_s[...] = jnp.dot(dyb, cmr_ref[...], preferred_element_type=F32)
        li_s[...] = -jnp.dot(dyb, cmi_ref[...], preferred_element_type=F32)
        keep_prev = jnp.where(first_tile, 0.0, 1.0)
        hcr[0:8, :] = hpr_ref[...] * keep_prev
        hci[0:8, :] = hpi_ref[...] * keep_prev
        hcr[8:, :] = hr_ref[...]
        hci[8:, :] = hi_ref[...]
        consts = _scan_consts(ar_ref[...], -ai_ref[...], True)
        row = consts[2]
        ngrp = ts // 8

        def step(jj, carry):
            cr, ci, accr, acci = carry
            j = ngrp - 1 - jj
            rows = pl.ds(pl.multiple_of(j * 8, 8), 8)
            nxt = pl.ds(pl.multiple_of(j * 8 + 8, 8), 8)
            lr, li = _scan_group(lr_s[rows, :], li_s[rows, :], cr, ci, consts, True)
            lr_s[rows, :] = lr
            li_s[rows, :] = li
            pr, pi = hcr[rows, :], hci[rows, :]
            hsr = jnp.where(row == 0, jnp.broadcast_to(pr[7:8, :], (8, PW)), pltpu.roll(hcr[nxt, :], 1, 0))
            hsi = jnp.where(row == 0, jnp.broadcast_to(pi[7:8, :], (8, PW)), pltpu.roll(hci[nxt, :], 1, 0))
            accr = accr + lr * hsr + li * hsi
            acci = acci + li * hsr - lr * hsi
            return jnp.broadcast_to(lr[0:1, :], (8, PW)), jnp.broadcast_to(li[0:1, :], (8, PW)), accr, acci

        zero = jnp.zeros((8, PW), F32)
        cr, ci, accr, acci = lax.fori_loop(0, ngrp, step, (car_r[...], car_i[...], zero, zero))
        car_r[...] = cr
        car_i[...] = ci
        dar_ref[...] += jnp.sum(accr, axis=0, keepdims=True)
        dai_ref[...] += jnp.sum(acci, axis=0, keepdims=True)
        lrb, lib = lr_s[...].astype(BF16), li_s[...].astype(BF16)
        du = jnp.dot(lrb, bdr_ref[...], preferred_element_type=F32)
        du += jnp.dot(lib, bdi_ref[...], preferred_element_type=F32)
        du_ref[...] = du + dy * d_ref[...]
        tn = (((0,), (0,)), ((), ()))
        dbdr_ref[...] += lax.dot_general(lrb, ub, tn, preferred_element_type=F32)
        dbdi_ref[...] += lax.dot_general(lib, ub, tn, preferred_element_type=F32)
        dcmr_ref[...] += lax.dot_general(dyb, hr_ref[...].astype(BF16), tn, preferred_element_type=F32)
        dcmi_ref[...] -= lax.dot_general(dyb, hi_ref[...].astype(BF16), tn, preferred_element_type=F32)
        dd_ref[...] += jnp.sum(dy * u, axis=0, keepdims=True)

    rev = lambda s: nsteps - 1 - s
    row_a = pl.BlockSpec((1, PW), lambda i, s: (0, i))
    tile = pl.BlockSpec((ts, PW), lambda i, s: (rev(s), i))
    prev8 = pl.BlockSpec((8, PW), lambda i, s: (jnp.maximum(rev(s) * (ts // 8) - 1, 0), i))
    cols = pl.BlockSpec((ts, LANE), lambda i, s: (rev(s), i))
    bd = pl.BlockSpec((None, PW, LANE), lambda i, s: (i, 0, 0))
    cm = pl.BlockSpec((None, LANE, PW), lambda i, s: (i, 0, 0))
    outs, _ = _hosted(
        "ssm_bwd", body, (npk, nsteps),
        [pl.BlockSpec((ts, LANE), lambda i, s: (rev(s), uoff + i)), cols, tile, tile, prev8, prev8,
         row_a, row_a, bd, bd, cm, cm, pl.BlockSpec((1, LANE), lambda i, s: (0, i))],
        [cols, bd, bd, cm, cm, row_a, row_a, pl.BlockSpec((1, LANE), lambda i, s: (0, i))],
        [jax.ShapeDtypeStruct((S, SSM_W), F32),
         jax.ShapeDtypeStruct((npk, PW, LANE), F32), jax.ShapeDtypeStruct((npk, PW, LANE), F32),
         jax.ShapeDtypeStruct((npk, LANE, PW), F32), jax.ShapeDtypeStruct((npk, LANE, PW), F32),
         jax.ShapeDtypeStruct((1, NS), F32), jax.ShapeDtypeStruct((1, NS), F32), jax.ShapeDtypeStruct((1, SSM_W), F32)],
        [pltpu.VMEM((ts, PW), F32), pltpu.VMEM((ts, PW), F32), pltpu.VMEM((ts + 8, PW), F32),
         pltpu.VMEM((ts + 8, PW), F32), pltpu.VMEM((8, PW), F32), pltpu.VMEM((8, PW), F32)],
        (z, dyp, hr, hi, hr, hi, a_r, a_i, bdr, bdi, cmr, cmi, dskip), ("parallel", "arbitrary"), None, jobs)
    return outs


def _block_diag(m4):
    npk, g, a, b = m4.shape
    eye = jnp.eye(g, dtype=m4.dtype)
    return (m4[:, :, :, None, :] * eye[None, :, None, :, None]).reshape(npk, g * a, g * b)


def _block_diag_take(m, a, b):
    npk = m.shape[0]
    m5 = m.reshape(npk, PACK, a, PACK, b)
    return jnp.stack([m5[:, g, :, g, :] for g in range(PACK)], axis=1)


def _mix_out(ya, ypre, gl, ga, gb, bglu):
    yg = jax.nn.gelu(ypre)
    yb = yg * jax.nn.sigmoid(gl + bglu)
    return jnp.concatenate([_rms(ya, ga), _rms(yb, gb)], axis=-1)


def _tail_loss(h3, gl, pe, gf, tgt):
    h4 = h3 + jax.nn.sigmoid(gl) * pe
    err = jnp.square(_rms(h4, gf) - tgt)
    return 0.5 * jnp.mean(err, axis=-1, keepdims=True)


def kernel(x, p, ffn1_norm, ffn1_w_gate, ffn1_w_up, ffn1_w_down, mix_norm, w_in, attn_out_norm, ssm_lambda_re, ssm_lambda_im, ssm_log_dt, ssm_b_re, ssm_b_im, ssm_c_re, ssm_c_im, ssm_d, ssm_w_glu, ssm_b_glu, ssm_out_norm, w_out, ffn2_norm, ffn2_w_gate, ffn2_w_up, ffn2_w_down, ple_norm, ple_w_gate, ple_w_proj, final_norm, loss_target, m_ffn1_norm, m_ffn1_w_gate, m_ffn1_w_up, m_ffn1_w_down, m_mix_norm, m_w_in, m_attn_out_norm, m_ssm_lambda_re, m_ssm_lambda_im, m_ssm_log_dt, m_ssm_b_re, m_ssm_b_im, m_ssm_c_re, m_ssm_c_im, m_ssm_d, m_ssm_w_glu, m_ssm_b_glu, m_ssm_out_norm, m_w_out, m_ffn2_norm, m_ffn2_w_gate, m_ffn2_w_up, m_ffn2_w_down, m_ple_norm, m_ple_w_gate, m_ple_w_proj, m_final_norm, v_ffn1_norm, v_ffn1_w_gate, v_ffn1_w_up, v_ffn1_w_down, v_mix_norm, v_w_in, v_attn_out_norm, v_ssm_lambda_re, v_ssm_lambda_im, v_ssm_log_dt, v_ssm_b_re, v_ssm_b_im, v_ssm_c_re, v_ssm_c_im, v_ssm_d, v_ssm_w_glu, v_ssm_b_glu, v_ssm_out_norm, v_w_out, v_ffn2_norm, v_ffn2_w_gate, v_ffn2_w_up, v_ffn2_w_down, v_ple_norm, v_ple_w_gate, v_ple_w_proj, v_final_norm):
    A = dict(locals())
    xs = x[0]
    ps = p[0, 0]
    tgt = loss_target[0]
    S, D = xs.shape
    FSH = ffn1_w_gate.shape[-1]
    FSP = -(-FSH // LANE) * LANE
    TR = _tile(S, 256)
    ZW = 3 * ATTN_W + SSM_W

    wgu1 = _prep("prep_gu1", [ffn1_w_gate[0], ffn1_w_up[0]], D, FSP)
    wgu2 = _prep("prep_gu2", [ffn2_w_gate[0], ffn2_w_up[0]], D, FSP)
    wd1 = _prep("prep_d1", [ffn1_w_down[0]], FSP, D)
    wd2 = _prep("prep_d2", [ffn2_w_down[0]], FSP, D)
    win = _prep("prep_in", [w_in[0]], D, w_in.shape[-1])
    wglu = _prep("prep_glu", [ssm_w_glu[0]], ssm_w_glu.shape[1], SSM_W)
    wout = _prep("prep_out", [w_out[0]], w_out.shape[1], D)
    wpg = _prep("prep_pg", [ple_w_gate[0]], ple_w_gate.shape[1], D)
    wpp = _prep("prep_pp", [ple_w_proj[0]], ple_w_proj.shape[1], ple_w_proj.shape[2])
    (Wgu1,) = _all_gather("ag_weights", [wgu1])
    rowstack = lambda w: w.reshape(1, w.shape[0] * w.shape[1], w.shape[2])

    def ffn_norm(tag, h, gain):
        return _rowwise(f"{tag}_norm", lambda a, g: ([_rms(a, g)], []), S, TR, [_full(h)], [gain], [(D, D, _c0, BF16)])[0][0]

    xn1 = ffn_norm("ffn1", xs, ffn1_norm)
    gu1, hid1, (Wd1, Win) = _ffn_up("ffn1_up", xn1, Wgu1, rider=_GatherRider([wd1, win]))
    Wd1 = rowstack(Wd1)
    h1, (Wd2,) = _mm_nn("ffn1_down", hid1, Wd1, tn=D // 2, tk=NDEV * FSP, res=xs, scale=0.5,
                        rider=_GatherRider([wd2]))
    Wd2 = rowstack(Wd2)
    un = ffn_norm("mix", h1, mix_norm)
    z, (Wglu, Wout, Wpg, Wpp) = _mm_nn("mix_in", un, Win, tn=512, tk=D,
                                       rider=_GatherRider([wglu, wout, wpg, wpp]))
    Wglu, Wout, Wpg = rowstack(Wglu), rowstack(Wout), rowstack(Wpg)
    ya, mg, den = _attn_fwd(z)

    col = lambda a: a.reshape(-1, 1)
    lr_c, li_c = col(ssm_lambda_re), col(ssm_lambda_im)
    dt_c = col(jnp.broadcast_to(ssm_log_dt.reshape(SSM_G, 1), (SSM_G, SSM_P)))
    b_re2, b_im2 = ssm_b_re.reshape(-1, SSM_C), ssm_b_im.reshape(-1, SSM_C)
    ar_c, ai_c, bbr, bbi = _ssm_prep(lr_c, li_c, dt_c, b_re2, b_im2)
    a_r, a_i = ar_c.reshape(1, -1), ai_c.reshape(1, -1)
    npk = SSM_G // PACK
    bdr = _block_diag(bbr.reshape(npk, PACK, SSM_P, SSM_C)).astype(BF16)
    bdi = _block_diag(bbi.reshape(npk, PACK, SSM_P, SSM_C)).astype(BF16)
    cmr = _block_diag(ssm_c_re.reshape(npk, PACK, SSM_C, SSM_P)).astype(BF16)
    cmi = _block_diag(ssm_c_im.reshape(npk, PACK, SSM_C, SSM_P)).astype(BF16)
    TS = _tile(S, 512)
    hr, hi, ypre, (Wgu2,) = _ssm_fwd(z, a_r, a_i, bdr, bdi, cmr, cmi, ssm_d, TS, rider=_GatherRider([wgu2]))
    (yg,), _ = _rowwise("ssm_gelu", lambda a: ([jax.nn.gelu(a)], []), S, TR, [_full(ypre)], [], [(SSM_W, SSM_W, _c0, BF16)])
    gl = _mm_nn("ssm_glu", yg, Wglu, tn=SSM_W, tk=SSM_W)
    (ycat,), _ = _rowwise("mix_out", lambda *a: ([_mix_out(*a)], []), S, TR, [_full(ya), _full(ypre), _full(gl)],
                          [attn_out_norm, ssm_out_norm, ssm_b_glu], [(MIX_W, MIX_W, _c0, BF16)])
    h2, xn2 = _mm_nn("mix_proj", ycat, Wout, tn=D, tk=D, res=h1, scale=1.0, gain=ffn2_norm)
    gu2, hid2, _ = _ffn_up("ffn2_up", xn2, Wgu2)
    h3 = _mm_nn("ffn2_down", hid2, Wd2, tn=D // 2, tk=NDEV * FSP, res=h2, scale=0.5)
    hn = ffn_norm("ple", h3, ple_norm)
    pgl = _mm_nn("ple_gate", hn, Wpg, tn=D // 2, tk=D)
    pb = ps
    pe = _mm_nn("ple_proj", pb, Wpp, tn=Wpp.shape[2], tk=Wpp.shape[1])

    def tail(h3b, glb, peb, tb, gf):
        rows, vjp = jax.vjp(lambda a, b, c, g: _tail_loss(a, b, c, g, tb), h3b, glb, peb, gf)
        dh, dgl, dpe, dgf = vjp(jnp.ones_like(rows))
        return [dh, dgl, dpe], [jnp.broadcast_to(jnp.sum(rows, axis=0, keepdims=True), (1, LANE)), dgf]

    (dh3_dir, dpgl, dpe), (loss_row, g_final) = _rowwise(
        "tail", tail, S, TR, [_full(h3), _full(pgl), _full(pe), _full(tgt)], [final_norm.reshape(1, D)],
        [(D, D, _c0, F32), (D, D, _c0, BF16), (D, D, _c0, BF16)], [(LANE, LANE, _c0), (D, D, _c0)])
    loss = lax.psum(loss_row[0, 0], AXES)

    def norm_bwd(tag, h, gain, dn, dres):
        def f(hb, dnb, drb, g):
            _, vjp = jax.vjp(_rms, hb, g)
            dh, dg = vjp(dnb)
            dh = dh + drb
            return [dh, dh], [dg]
        (dh, dhb), (dg,) = _rowwise(f"{tag}_norm_bwd", f, S, TR, [_full(h), _full(dn), _full(dres)], [gain],
                                    [(D, D, _c0, F32), (D, D, _c0, BF16)], [(D, D, _c0)])
        return dh, dhb, dg

    restack = lambda g: g.reshape((NDEV, g.shape[1] // NDEV) + g.shape[2:])
    jobs, scat = [], {}

    def scatter(key, g):
        scat[key] = _Scatter("rs_" + key, g)
        jobs.append(scat[key])

    late = []
    dhn = _mm_nt("ple_gate_dx", dpgl, Wpg, tn=D, tk=D)
    late.append(lambda: scatter("pg", restack(_mm_tn("ple_gate_dw", hn, dpgl, 1, jobs=jobs))))
    late.append(lambda: scatter("pp", _mm_tn("ple_proj_dw", pb, dpe, NDEV, jobs=jobs)))
    dh3, dh3b, g_ple_norm = norm_bwd("ple", h3, ple_norm, dhn, dh3_dir)

    def ffn_bwd(tag, h, gain, Wgu, Wd, saved, dout, doutb):
        xn, gu, hid = saved
        dgu = _ffn_down_dx(f"{tag}_down_dx", doutb, Wd, gu, NDEV, scale=0.5, jobs=jobs)
        scatter(tag + "gu", _mm_tn(f"{tag}_up_dw", xn, dgu, NDEV, tn=FSP, jobs=jobs))
        scatter(tag + "d", restack(_mm_tn(f"{tag}_down_dw", hid, doutb, 1, tko=FSP, tn=256, scale=0.5, jobs=jobs)))
        dxn = _mm_nt(f"{tag}_up_dx", dgu, Wgu, tm=512, tn=256, jb=NDEV, jobs=jobs)
        dh, dhb, g_norm = norm_bwd(tag, h, gain, dxn, dout)
        return dh, dhb, g_norm

    dh2, dh2b, g_ffn2_norm = ffn_bwd("ffn2", h2, ffn2_norm, Wgu2, Wd2, (xn2, gu2, hid2), dh3, dh3b)

    dycat = _mm_nt("mix_proj_dx", dh2b, Wout, tn=D, tk=D, jobs=jobs)
    late.append(lambda: scatter("out", restack(_mm_tn("mix_proj_dw", ycat, dh2b, 1, jobs=jobs))))

    def mix_out_bwd(yab, ypb, glb, dyc, ga, gb, bglu):
        _, vjp = jax.vjp(_mix_out, yab, ypb, glb, ga, gb, bglu)
        dya_, dyp_, dgl_, dga, dgb, dbg = vjp(dyc)
        return [dya_, dyp_, dgl_], [dga, dgb, dbg]
    (dya, dyp_dir, dglb), (g_attn_norm, g_ssm_norm, g_bglu) = _rowwise(
        "mix_out_bwd", mix_out_bwd, S, TR, [_full(ya), _full(ypre), _full(gl), _full(dycat)],
        [attn_out_norm, ssm_out_norm, ssm_b_glu],
        [(ATTN_W, ATTN_W, _c0, F32), (SSM_W, SSM_W, _c0, F32), (SSM_W, SSM_W, _c0, BF16)],
        [(ATTN_W, ATTN_W, _c0), (SSM_W, SSM_W, _c0), (SSM_W, SSM_W, _c0)])
    dyg = _mm_nt("ssm_glu_dx", dglb, Wglu, tn=SSM_W, tk=SSM_W, jobs=jobs)
    late.append(lambda: scatter("glu", restack(_mm_tn("ssm_glu_dw", yg, dglb, 1, jobs=jobs))))

    def gelu_bwd(ypb, dygb, ddir):
        _, vjp = jax.vjp(jax.nn.gelu, ypb)
        return [ddir + vjp(dygb)[0]], []
    (dyp,), _ = _rowwise("ssm_gelu_bwd", gelu_bwd, S, TR, [_full(ypre), _full(dyg), _full(dyp_dir)], [],
                         [(SSM_W, SSM_W, _c0, F32)])
    du, dbdr, dbdi, dcmr, dcmi, da_r, da_i, g_ssm_d = _ssm_bwd(z, dyp, hr, hi, a_r, a_i, bdr, bdi, cmr, cmi, ssm_d, TS,
                                                              jobs=jobs)
    dbbr = _block_diag_take(dbdr, SSM_P, SSM_C).reshape(-1, SSM_C)
    dbbi = _block_diag_take(dbdi, SSM_P, SSM_C).reshape(-1, SSM_C)
    g_c_re = _block_diag_take(dcmr, SSM_C, SSM_P).reshape(ssm_c_re.shape)
    g_c_im = _block_diag_take(dcmi, SSM_C, SSM_P).reshape(ssm_c_im.shape)
    dlr, dli, ddt, g_b_re, g_b_im = _ssm_prep_bwd(lr_c, li_c, dt_c, b_re2, b_im2, col(da_r), col(da_i), dbbr, dbbi)
    g_lam_re, g_lam_im = dlr.reshape(ssm_lambda_re.shape), dli.reshape(ssm_lambda_im.shape)
    g_log_dt = jnp.sum(ddt.reshape(SSM_G, SSM_P), axis=1).reshape(ssm_log_dt.shape)
    g_b_re, g_b_im = g_b_re.reshape(ssm_b_re.shape), g_b_im.reshape(ssm_b_im.shape)

    dq, dk, dv = _attn_bwd(z, dya, ya, mg, den, jobs=jobs)
    (dz,), _ = _rowwise("mix_dz", lambda *a: ([jnp.concatenate(a, axis=-1)], []), S, TR,
                        [_full(dq), _full(dk), _full(dv), _full(du)], [], [(ZW, ZW, _c0, BF16)])
    dun = _mm_nt("mix_in_dx", dz, Win, tm=512, tn=512, jb=NDEV, jobs=jobs)
    scatter("in", _mm_tn("mix_in_dw", un, dz, NDEV, tn=512, jobs=jobs))
    dh1, dh1b, g_mix_norm = norm_bwd("mix", h1, mix_norm, dun, dh2)

    dx, _dxb, g_ffn1_norm = ffn_bwd("ffn1", xs, ffn1_norm, Wgu1, Wd1, (xn1, gu1, hid1), dh1, dh1b)
    for run in (late[2], late[0], late[3], late[1]):
        run()

    out = {}

    def upd(name, key, *, tr, cw, gw, goff=0):
        w, m, v = A[name][0], A["m_" + name][0], A["v_" + name][0]
        g, dlt, mn, vn = _adamw("adamw_" + name, w, m, v, scat[key].finish(), tr=tr, cw=cw, gw=gw, goff=goff, jobs=jobs)
        for k, val in (("grad_", g), ("delta_", dlt), ("new_m_", mn), ("new_v_", vn)):
            out[k + name] = val[None]

    DT = _tile(D, 256)
    FT = _tile(FSH, 512)
    DC = _tile(D, 1024, LANE)
    upd("ffn2_w_gate", "ffn2gu", tr=DT, cw=FSH, gw=FSP, goff=0)
    upd("ffn2_w_up", "ffn2gu", tr=DT, cw=FSH, gw=FSP, goff=1)
    upd("ffn2_w_down", "ffn2d", tr=FT, cw=DC, gw=DC)
    upd("w_in", "in", tr=DT, cw=w_in.shape[-1], gw=w_in.shape[-1])
    upd("ffn1_w_gate", "ffn1gu", tr=DT, cw=FSH, gw=FSP, goff=0)
    upd("ffn1_w_up", "ffn1gu", tr=DT, cw=FSH, gw=FSP, goff=1)
    upd("ffn1_w_down", "ffn1d", tr=FT, cw=DC, gw=DC)
    upd("w_out", "out", tr=w_out.shape[1], cw=DC, gw=DC)
    upd("ple_w_gate", "pg", tr=ple_w_gate.shape[1], cw=DC, gw=DC)
    upd("ssm_w_glu", "glu", tr=ssm_w_glu.shape[1], cw=SSM_W, gw=SSM_W)
    upd("ple_w_proj", "pp", tr=ple_w_proj.shape[1], cw=ple_w_proj.shape[2], gw=ple_w_proj.shape[2])

    small = [("ffn1_norm", g_ffn1_norm), ("mix_norm", g_mix_norm), ("attn_out_norm", g_attn_norm),
             ("ssm_lambda_re", g_lam_re), ("ssm_lambda_im", g_lam_im), ("ssm_log_dt", g_log_dt),
             ("ssm_b_re", g_b_re), ("ssm_b_im", g_b_im), ("ssm_c_re", g_c_re), ("ssm_c_im", g_c_im),
             ("ssm_d", g_ssm_d), ("ssm_b_glu", g_bglu), ("ssm_out_norm", g_ssm_norm), ("ffn2_norm", g_ffn2_norm),
             ("ple_norm", g_ple_norm), ("final_norm", g_final)]
    chunk = 8 * LANE

    def pack(arrs):
        parts = []
        for a in arrs:
            flat = a.reshape(-1)
            padn = -(-flat.shape[0] // chunk) * chunk
            parts.append(jnp.pad(flat, (0, padn - flat.shape[0])).reshape(-1, LANE))
        return jnp.concatenate(parts, axis=0)

    g_pack = pack([g for _, g in small])
    (g_all,) = _all_gather("ag_small", [g_pack])
    g_sum = _sum8("small_sum", g_all)
    w_pack = pack([A[n] for n, _ in small])
    m_pack = pack([A["m_" + n] for n, _ in small])
    v_pack = pack([A["v_" + n] for n, _ in small])
    d_pack, mn_pack, vn_pack = _adamw_small("adamw_small", w_pack, m_pack, v_pack, g_sum)
    off = 0
    for n, _ in small:
        shape = A[n].shape
        size = math.prod(shape)
        rows = -(-size // chunk) * 8
        for k, buf in (("grad_", g_sum), ("delta_", d_pack), ("new_m_", mn_pack), ("new_v_", vn_pack)):
            out[k + n] = buf[off:off + rows].reshape(-1)[:size].reshape(shape)
        off += rows

    names = ['ffn1_norm', 'ffn1_w_gate', 'ffn1_w_up', 'ffn1_w_down', 'mix_norm', 'w_in', 'attn_out_norm',
             'ssm_lambda_re', 'ssm_lambda_im', 'ssm_log_dt', 'ssm_b_re', 'ssm_b_im', 'ssm_c_re', 'ssm_c_im', 'ssm_d',
             'ssm_w_glu', 'ssm_b_glu', 'ssm_out_norm', 'w_out', 'ffn2_norm', 'ffn2_w_gate', 'ffn2_w_up', 'ffn2_w_down',
             'ple_norm', 'ple_w_gate', 'ple_w_proj', 'final_norm']
    return (loss, dx[None], *[out[k + n] for k in ("grad_", "delta_", "new_m_", "new_v_") for n in names])
```

```python
import functools
import math

import jax
import jax.numpy as jnp
from jax import lax
from jax.experimental import pallas as pl
from jax.experimental.pallas import tpu as pltpu

F32, BF16 = jnp.float32, jnp.bfloat16
MESH = pl.DeviceIdType.MESH
NDEV = 8
AXES = ("x", "y", "c")
LANE = 128
VMEM_LIMIT = 60 * 1024 * 1024

ATTN_W = 1024
HEAD_DIM = 64
SSM_W = 1024
MIX_W = ATTN_W + SSM_W
SSM_G, SSM_P, SSM_C = 64, 64, 16
PACK = 8
DILATIONS = (1, 4, 16)
QB = 128
NORM_EPS = 1e-6
MASK_VALUE = -1e30
LR, B1, B2, EPS, WD, STEP = 0.001, 0.9, 0.999, 1e-08, 0.01, 10


def _cp(sem=None):
    return pltpu.CompilerParams(dimension_semantics=sem, vmem_limit_bytes=VMEM_LIMIT)


def _tile(n, target, mult=8):
    if n <= target:
        return n
    for t in range(target - target % mult, 0, -mult):
        if n % t == 0:
            return t
    return n


def _rms(x, g):
    return x * lax.rsqrt(jnp.mean(x * x, axis=-1, keepdims=True) + NORM_EPS) * g


def _rowwise(name, fn, S, tr, rows, fulls, outs, accs=(), ncol=1, jobs=None):
    nr, nf, no, na = len(rows), len(fulls), len(outs), len(accs)

    def body(*refs):
        ins = [r[...] for r in refs[:nr + nf]]
        o_refs = refs[nr + nf:nr + nf + no]
        a_refs = refs[nr + nf + no:]
        o_vals, a_vals = fn(*ins)
        for r, v in zip(o_refs, o_vals):
            r[...] = v.astype(r.dtype)
        if na:
            @pl.when(pl.program_id(1) == 0)
            def _():
                for r in a_refs:
                    r[...] = jnp.zeros_like(r)
            for r, v in zip(a_refs, a_vals):
                r[...] += v

    in_specs = [pl.BlockSpec((tr, w), functools.partial(lambda j, i, cm: (i, cm(j)), cm=cm)) for _, w, cm in rows]
    in_specs += [pl.BlockSpec(f.shape, functools.partial(lambda j, i, nd: (0,) * nd, nd=f.ndim)) for f in fulls]
    out_specs = [pl.BlockSpec((tr, w), functools.partial(lambda j, i, cm: (i, cm(j)), cm=cm)) for _, w, cm, _ in outs]
    out_specs += [pl.BlockSpec((1, w), functools.partial(lambda j, i, cm: (0, cm(j)), cm=cm)) for _, w, cm in accs]
    out_shape = [jax.ShapeDtypeStruct((S, c), dt) for c, _, _, dt in outs]
    out_shape += [jax.ShapeDtypeStruct((1, c), F32) for c, _, _ in accs]
    res, _ = _hosted(name, body, (ncol, S // tr), in_specs, out_specs, out_shape, [],
                     [a for a, _, _ in rows] + list(fulls), ("parallel", "arbitrary" if na else "parallel"), None, jobs)
    return res[:no], res[no:]


def _c0(j):
    return 0


def _full(a):
    return (a, a.shape[1], _c0)


def _hosted(name, body, grid, in_specs, out_specs, out_shape, scratch, args, sem, rider=None, jobs=None):
    nsteps = math.prod(grid)

    def step_of(*g):
        t = 0
        for gi, n in zip(g, grid):
            t = t * n + gi
        return t

    job = None
    if rider is None and jobs:
        def block_bytes(spec, like):
            shape = [d for d in (spec.block_shape or ()) if d is not None]
            return math.prod(shape) * jnp.dtype(like.dtype).itemsize if shape else 0

        held = 2 * sum(block_bytes(s, a) for s, a in zip(list(in_specs) + list(out_specs), list(args) + list(out_shape)))
        held += sum(math.prod(s.shape) * jnp.dtype(s.dtype).itemsize for s in scratch
                    if getattr(s, "dtype", None) in (F32, BF16))
        job, rider = _pick(jobs, nsteps, small_host=held + 16 * RIDER_TILE_BYTES + (8 << 20) <= VMEM_LIMIT)
    if rider is None:
        outs = pl.pallas_call(body, name=name, grid=grid, in_specs=in_specs, out_specs=out_specs, out_shape=out_shape,
                              scratch_shapes=scratch, compiler_params=_cp(sem))(*args)
        return outs, None
    rider.bind(step_of, nsteps)
    n_in, n_out, n_scr = len(in_specs), len(out_specs), len(scratch)

    def full(*refs):
        a, b = n_in, n_in + rider.n_in
        c, d = b + n_out, b + n_out + rider.n_out
        rider.run(refs[a:b], refs[c:d], refs[d + n_scr:], step_of(*[pl.program_id(i) for i in range(len(grid))]), nsteps)
        body(*(refs[:a] + refs[b:c] + refs[d:d + n_scr]))

    outs = pl.pallas_call(
        full, name=name, grid=grid, in_specs=in_specs + rider.in_specs, out_specs=out_specs + rider.out_specs,
        out_shape=out_shape + rider.out_shape, scratch_shapes=scratch + rider.scratch,
        compiler_params=_cp(("arbitrary",) * len(grid)))(*args, *rider.operands)
    extra = rider.take(outs[n_out:])
    if job is not None:
        job.advance(extra)
        extra = None
    return outs[:n_out], extra


def _mm_nn(name, a, w, *, out_dtype=F32, tm=512, tn=768, tk=2048, res=None, scale=1.0, gain=None, rider=None,
           jobs=None):
    M, K = a.shape
    J, K2, Np = w.shape
    assert K == K2
    tm, tn, tk = _tile(M, tm), _tile(Np, tn, LANE), _tile(K, tk, LANE)
    npj = Np // tn
    nk = K // tk
    grid = (M // tm, J * npj, nk)
    assert gain is None or (J * npj == 1 and res is not None)

    def body(*refs):
        refs = list(refs)
        a_ref, w_ref = refs[:2]
        r_ref = refs[2] if res is not None else None
        g_ref = refs[3] if gain is not None else None
        acc = refs[-1]
        o_ref = refs[-3] if gain is not None else refs[-2]
        k = pl.program_id(2)
        part = jnp.dot(a_ref[...].astype(BF16), w_ref[...], preferred_element_type=F32)

        def finish(v):
            if res is not None:
                v = r_ref[...] + scale * v
            o_ref[...] = v.astype(o_ref.dtype)
            if gain is not None:
                refs[-2][...] = _rms(v, g_ref[...]).astype(BF16)

        if nk == 1:
            finish(part)
            return

        @pl.when(k == 0)
        def _():
            acc[...] = part

        @pl.when(k > 0)
        def _():
            acc[...] += part

        @pl.when(k == nk - 1)
        def _():
            finish(acc[...])

    in_specs = [pl.BlockSpec((tm, tk), lambda i, n, k: (i, k)),
                pl.BlockSpec((None, tk, tn), lambda i, n, k: (n // npj, k, n % npj))]
    args = [a, w]
    if res is not None:
        in_specs.append(pl.BlockSpec((tm, tn), lambda i, n, k: (i, n)))
        args.append(res)
    out_specs = [pl.BlockSpec((tm, tn), lambda i, n, k: (i, n))]
    out_shape = [jax.ShapeDtypeStruct((M, J * Np), out_dtype)]
    if gain is not None:
        in_specs.append(pl.BlockSpec((1, tn), lambda i, n, k: (0, 0)))
        args.append(gain)
        out_specs.append(pl.BlockSpec((tm, tn), lambda i, n, k: (i, n)))
        out_shape.append(jax.ShapeDtypeStruct((M, J * Np), BF16))
    outs, extra = _hosted(name, body, grid, in_specs, out_specs, out_shape, [pltpu.VMEM((tm, tn), F32)], args,
                          ("parallel", "parallel", "arbitrary"), rider, jobs)
    outs = tuple(outs) + (() if rider is None else (extra,))
    return outs[0] if len(outs) == 1 else outs


def _mm_nt(name, dy, w, *, out_dtype=F32, tm=512, tn=2048, tk=768, jb=1, scale=1.0, jobs=None):
    M, N = dy.shape
    J, K, Np = w.shape
    assert N == J * Np
    tm, tn, tk = _tile(M, tm), _tile(K, tn, LANE), (Np if jb > 1 else _tile(Np, tk, LANE))
    npj = Np // tk
    nc = J * npj // jb
    nt = (((1,), (1,)), ((), ()))

    def body(a_ref, w_ref, o_ref, acc):
        c = pl.program_id(2)
        if jb == 1:
            part = lax.dot_general(a_ref[...].astype(BF16), w_ref[...], nt, preferred_element_type=F32)
        else:
            part = sum(lax.dot_general(a_ref[:, j * Np:(j + 1) * Np].astype(BF16), w_ref[j], nt,
                                       preferred_element_type=F32) for j in range(jb))
        if nc == 1:
            o_ref[...] = (scale * part).astype(o_ref.dtype)
            return

        @pl.when(c == 0)
        def _():
            acc[...] = part

        @pl.when(c > 0)
        def _():
            acc[...] += part

        @pl.when(c == nc - 1)
        def _():
            o_ref[...] = (scale * acc[...]).astype(o_ref.dtype)

    w_spec = (pl.BlockSpec((None, tn, tk), lambda i, n, c: (c // npj, n, c % npj)) if jb == 1 else
              pl.BlockSpec((jb, tn, Np), lambda i, n, c: (c, n, 0)))
    (out,), _ = _hosted(
        name, body, (M // tm, K // tn, nc), [pl.BlockSpec((tm, jb * tk), lambda i, n, c: (i, c)), w_spec],
        [pl.BlockSpec((tm, tn), lambda i, n, c: (i, n))], [jax.ShapeDtypeStruct((M, K), out_dtype)],
        [pltpu.VMEM((tm, tn) if nc > 1 else (8, LANE), F32)], (dy, w), ("parallel", "parallel", "arbitrary"), None, jobs)
    return out


def _mm_tn(name, x, dy, J, *, tm=8192, tko=256, tn=512, scale=1.0, jobs=None):
    M, K = x.shape
    M2, N = dy.shape
    assert M == M2 and N % J == 0
    Np = N // J
    tm, tko, tn = _tile(M, tm, LANE), _tile(K, tko, LANE), _tile(Np, tn, LANE)
    npj = Np // tn
    nm = M // tm

    def body(x_ref, d_ref, o_ref, acc):
        m = pl.program_id(2)
        part = lax.dot_general(x_ref[...].astype(BF16), d_ref[...].astype(BF16), (((0,), (0,)), ((), ())),
                               preferred_element_type=F32)
        if nm == 1:
            o_ref[...] = scale * part
            return

        @pl.when(m == 0)
        def _():
            acc[...] = part

        @pl.when(m > 0)
        def _():
            acc[...] += part

        @pl.when(m == nm - 1)
        def _():
            o_ref[...] = scale * acc[...]

    nk, nn = K // tko, J * npj
    k_outer = x.size + nk * dy.size <= dy.size + nn * x.size
    kn = (lambda a, b: (a, b)) if k_outer else (lambda a, b: (b, a))
    (out,), _ = _hosted(
        name, body, (nk, nn, nm) if k_outer else (nn, nk, nm),
        [pl.BlockSpec((tm, tko), lambda a, b, m: (m, kn(a, b)[0])),
         pl.BlockSpec((tm, tn), lambda a, b, m: (m, kn(a, b)[1]))],
        [pl.BlockSpec((None, tko, tn), lambda a, b, m: (kn(a, b)[1] // npj, kn(a, b)[0], kn(a, b)[1] % npj))],
        [jax.ShapeDtypeStruct((J, K, Np), F32)], [pltpu.VMEM((tko, tn) if nm > 1 else (8, LANE), F32)], (x, dy),
        ("parallel", "parallel", "arbitrary"), None, jobs)
    return out


def _swiglu_act(g, u):
    return jax.nn.silu(g) * u


def _ffn_up(name, xn, wgu, *, tm=1024, rider=None):
    M, K = xn.shape
    J, _, F2 = wgu.shape
    F = F2 // 2
    tm = _tile(M, tm)

    def body(a_ref, w_ref, gu_ref, h_ref):
        r = jnp.dot(a_ref[...], w_ref[...], preferred_element_type=F32)
        gu_ref[...] = r.astype(gu_ref.dtype)
        h_ref[...] = _swiglu_act(r[:, :F], r[:, F:]).astype(h_ref.dtype)

    (gu, hid), extra = _hosted(
        name, body, (M // tm, J),
        [pl.BlockSpec((tm, K), lambda i, j: (i, 0)), pl.BlockSpec((None, K, F2), lambda i, j: (j, 0, 0))],
        [pl.BlockSpec((tm, F2), lambda i, j: (i, j)), pl.BlockSpec((tm, F), lambda i, j: (i, j))],
        [jax.ShapeDtypeStruct((M, J * F2), BF16), jax.ShapeDtypeStruct((M, J * F), BF16)], [], (xn, wgu),
        ("parallel", "parallel"), rider)
    return gu, hid, extra


def _ffn_down_dx(name, dout, wd, gu, J, *, scale, tm=512, jobs=None):
    M, D = dout.shape
    F = wd.shape[1] // J
    tm = _tile(M, tm)

    def body(d_ref, w_ref, gu_ref, o_ref):
        dh = scale * lax.dot_general(d_ref[...], w_ref[...], (((1,), (1,)), ((), ())), preferred_element_type=F32)
        gu = gu_ref[...].astype(F32)
        _, vjp = jax.vjp(_swiglu_act, gu[:, :F], gu[:, F:])
        o_ref[...] = jnp.concatenate(vjp(dh), axis=-1).astype(o_ref.dtype)

    (out,), _ = _hosted(
        name, body, (M // tm, J),
        [pl.BlockSpec((tm, D), lambda i, j: (i, 0)), pl.BlockSpec((None, F, D), lambda i, j: (0, j, 0)),
         pl.BlockSpec((tm, 2 * F), lambda i, j: (i, j))],
        [pl.BlockSpec((tm, 2 * F), lambda i, j: (i, j))], [jax.ShapeDtypeStruct((M, J * 2 * F), BF16)], [],
        (dout, wd, gu), ("parallel", "parallel"), None, jobs)
    return out


def _all_gather(name, shards):
    n = len(shards)

    def body(*refs):
        start, forward, finish = _gather_phases(refs[:n], refs[n:2 * n], *refs[2 * n:])
        start()
        forward()
        finish()

    any_spec = pl.BlockSpec(memory_space=pl.ANY)
    return pl.pallas_call(
        body, name=name, in_specs=[any_spec] * n, out_specs=[any_spec] * n,
        out_shape=[jax.ShapeDtypeStruct((NDEV,) + s.shape, s.dtype) for s in shards],
        scratch_shapes=_gather_sems(n),
    )(*shards)


def _gather_sems(n):
    return [pltpu.SemaphoreType.DMA((n, 7)), pltpu.SemaphoreType.DMA((n, 7)), pltpu.SemaphoreType.DMA((n,))]


def _gather_phases(ins, outs, send_sems, recv_sems, local_sems):
    n = len(ins)
    x, y, c = lax.axis_index("x"), lax.axis_index("y"), lax.axis_index("c")
    me, sibling = (x, y, c), (x, y, 1 - c)
    chips = [(1 - x, y), (x, 1 - y), (1 - x, 1 - y)]

    def blk(i, px, py, pc):
        return outs[i].at[4 * px + 2 * py + pc]

    def copy(i, k, block, to, src=None):
        return pltpu.make_async_remote_copy(
            src_ref=blk(i, *block) if src is None else src, dst_ref=blk(i, *block),
            send_sem=send_sems.at[i, k], recv_sem=recv_sems.at[i, k], device_id=to, device_id_type=MESH)

    def local(i):
        return pltpu.make_async_copy(ins[i], blk(i, *me), local_sems.at[i])

    def firsts(i):
        return [copy(i, 0, me, sibling, src=ins[i])] + [copy(i, 1 + j, me, (*chip, c), src=ins[i])
                                                        for j, chip in enumerate(chips)]

    def start():
        for i in range(n):
            local(i).start()
        for i in range(n):
            for cp in firsts(i):
                cp.start()

    def forward():
        for i in range(n):
            for j, chip in enumerate(chips):
                copy(i, 1 + j, (*chip, c), me).wait_recv()
                copy(i, 4 + j, (*chip, c), sibling).start()

    def finish():
        for i in range(n):
            copy(i, 0, sibling, me).wait_recv()
            for j, chip in enumerate(chips):
                copy(i, 4 + j, (*chip, 1 - c), me).wait_recv()
        for i in range(n):
            for cp in firsts(i):
                cp.wait_send()
            for j, chip in enumerate(chips):
                copy(i, 4 + j, (*chip, c), sibling).wait_send()
            local(i).wait()

    return start, forward, finish


class _GatherRider:
    def __init__(self, shards):
        self.operands = list(shards)
        n = len(self.operands)
        self.n_in = self.n_out = n
        any_spec = pl.BlockSpec(memory_space=pl.ANY)
        self.in_specs = [any_spec] * n
        self.out_specs = [any_spec] * n
        self.out_shape = [jax.ShapeDtypeStruct((NDEV,) + s.shape, s.dtype) for s in self.operands]
        self.scratch = _gather_sems(n)

    def bind(self, step_of, nsteps):
        return self

    def take(self, outs):
        return list(outs)

    def run(self, ins, outs, sems, step, nsteps):
        start, forward, finish = _gather_phases(ins, outs, *sems)
        pl.when(step == 0)(start)
        pl.when(step == (4 * nsteps) // 5)(forward)
        pl.when(step == nsteps - 1)(finish)


class _SwapRider:
    def __init__(self, arr, streams, grid, tile, out_shape, out_block, out_map):
        self.arr, self.streams, self.grid, self.tile = arr, streams, grid, tile
        self.ns, self.n = len(streams), grid[0] * grid[1]
        self.operands = [arr] * (2 * self.ns)
        self.n_in, self.n_out = 2 * self.ns, 1
        self.out_shape = [jax.ShapeDtypeStruct(out_shape, F32)]
        self.out_block, self.out_map = out_block, out_map
        tr, C = tile
        slots = [pltpu.VMEM((2, tr, C), w) for _, w, _, _ in streams]
        self.scratch = slots + slots + [pltpu.SemaphoreType.DMA((self.ns, 2)), pltpu.SemaphoreType.DMA((self.ns, 2)),
                                        pltpu.SemaphoreType.REGULAR((self.ns,))]

    def bind(self, step_of, nsteps):
        assert nsteps >= self.n
        self.period = period = nsteps // self.n
        n, nr = self.n, self.grid[1]

        def ids(*g):
            k = jnp.minimum(step_of(*g) // period, n - 1)
            pos = {a: lax.axis_index(a) for a in AXES}
            return k // nr, k % nr, [v for a in AXES for v in (pos[a], 1 - pos[a])]

        block = (None,) * (self.arr.ndim - 2) + tuple(self.tile)
        self.in_specs = []
        for _, _, keep_map, send_map in self.streams:
            for m in (keep_map, send_map):
                self.in_specs.append(pl.BlockSpec(block, functools.partial(lambda *g, m: m(*ids(*g)), m=m)))
        self.out_specs = [pl.BlockSpec(self.out_block, lambda *g: self.out_map(*ids(*g)))]
        return self

    def take(self, outs):
        return outs[0]

    def run(self, ins, outs, scratch, step, nsteps):
        ns, n, period = self.ns, self.n, self.period
        keeps, sends, o_ref = ins[0::2], ins[1::2], outs[0]
        lands, stages = scratch[:ns], scratch[ns:2 * ns]
        send_sems, recv_sems, credits = scratch[2 * ns:]
        k = step // period
        slot = k % 2
        here = {a: lax.axis_index(a) for a in AXES}
        peers = [tuple(1 - here[a] if a == axis else here[a] for a in AXES) for axis, _, _, _ in self.streams]

        def rdma(s):
            return pltpu.make_async_remote_copy(
                src_ref=stages[s].at[slot], dst_ref=lands[s].at[slot], send_sem=send_sems.at[s, slot],
                recv_sem=recv_sems.at[s, slot], device_id=peers[s], device_id_type=MESH)

        @pl.when((k < n) & (step % period == 0))
        def _():
            @pl.when(k >= 2)
            def _():
                for s in range(ns):
                    pl.semaphore_wait(credits.at[s], 1)

            for s in range(ns):
                stages[s][slot] = sends[s][...].astype(stages[s].dtype)
                rdma(s).start()

        @pl.when((k < n) & (step % period == period - 1))
        def _():
            for s in range(ns):
                rdma(s).wait_recv()
                total = keeps[s][...] + lands[s][slot].astype(F32)
                if ns == 1:
                    o_ref[...] = total
                else:
                    o_ref[s] = total
            for s in range(ns):
                rdma(s).wait_send()

            @pl.when(k + 2 < n)
            def _():
                for s in range(ns):
                    pl.semaphore_signal(credits.at[s], inc=1, device_id=peers[s], device_id_type=MESH)


def _run_alone(name, rider):
    rider.bind(lambda t: t, rider.n)

    def body(*refs):
        a, b = rider.n_in, rider.n_in + rider.n_out
        rider.run(refs[:a], refs[a:b], refs[b:], pl.program_id(0), rider.n)

    outs = pl.pallas_call(
        body, name=name, grid=(rider.n,), in_specs=rider.in_specs, out_specs=rider.out_specs,
        out_shape=rider.out_shape, scratch_shapes=rider.scratch, compiler_params=_cp(("arbitrary",)),
    )(*rider.operands)
    return rider.take(outs)


class _Scatter:
    def __init__(self, name, g):
        self.name, self.cur, self.stage = name, g, 0
        _, self.R, self.C = g.shape

    def done(self):
        return self.stage == 3

    def rider(self, rows):
        R, C = self.R, self.C
        R2 = R // 2
        tr = _tile(R2, rows, 16)
        nrh = R2 // tr
        if self.stage == 0:
            return _SwapRider(
                self.cur.reshape(4, 2, R, C),
                [("c", BF16, lambda b, i, s: (b, s[4], i, 0), lambda b, i, s: (b, s[5], i, 0))],
                (4, 2 * nrh), (tr, C), (2, 4, R2, C), (None, None, tr, C), lambda b, i, s: (i // nrh, b, i % nrh, 0))
        if self.stage == 1:
            return _SwapRider(
                self.cur.reshape(2, 2, 2, R2, C),
                [("y", BF16, lambda b, i, s: (0, b, s[2], i, 0), lambda b, i, s: (0, b, s[3], i, 0)),
                 ("x", BF16, lambda b, i, s: (1, s[0], b, i, 0), lambda b, i, s: (1, s[1], b, i, 0))],
                (2, nrh), (tr, C), (2, 2, R2, C), (2, None, tr, C), lambda b, i, s: (0, b, i, 0))
        return _SwapRider(
            self.cur,
            [("x", BF16, lambda b, i, s: (0, s[0], i, 0), lambda b, i, s: (0, s[1], i, 0)),
             ("y", BF16, lambda b, i, s: (1, s[2], i, 0), lambda b, i, s: (1, s[3], i, 0))],
            (1, nrh), (tr, C), (2, R2, C), (2, tr, C), lambda b, i, s: (0, i, 0))

    def advance(self, out):
        self.cur, self.stage = out, self.stage + 1

    def finish(self):
        while not self.done():
            self.advance(_run_alone(f"{self.name}_s{self.stage}", self.rider(256)))
        return self.cur.reshape(self.R, self.C)


RIDER_TILE_BYTES = 3 * 512 * 1024


def _pick(jobs, nsteps, small_host=False):
    for job in sorted(jobs or (), key=lambda j: -j.R * j.C):
        if job.done():
            continue
        streams = 1 if job.stage == 0 else 2
        for budget in (RIDER_TILE_BYTES, 2 * RIDER_TILE_BYTES if small_host else 0):
            riders = [job.rider(rows) for rows in (512, 256, 128, 64) if rows * job.C * 4 * streams <= budget]
            for rider in riders:
                if 2 * rider.n <= nsteps:
                    return job, rider
            if riders and riders[-1].n <= nsteps:
                return job, riders[-1]
    return None, None


def _sum8(name, g):
    _, R, C = g.shape
    tr = _tile(R, 512)

    def body(g_ref, o_ref):
        acc = g_ref[0]
        for d in range(1, NDEV):
            acc = acc + g_ref[d]
        o_ref[...] = acc

    return pl.pallas_call(
        body, name=name, grid=(R // tr,), in_specs=[pl.BlockSpec((NDEV, tr, C), lambda i: (0, i, 0))],
        out_specs=pl.BlockSpec((tr, C), lambda i: (i, 0)), out_shape=jax.ShapeDtypeStruct((R, C), F32),
        compiler_params=_cp(("parallel",)),
    )(g)


def _adamw_math(w, g, m, v):
    m = B1 * m + (1.0 - B1) * g
    v = B2 * v + (1.0 - B2) * jnp.square(g)
    m_hat = m / (1.0 - B1 ** STEP)
    v_hat = v / (1.0 - B2 ** STEP)
    delta = -LR * (m_hat / (jnp.sqrt(v_hat) + EPS) + WD * w)
    return delta, m, v


def _adamw(name, w, m, v, gp, *, tr, cw, gw, goff=0, jobs=None):
    R, C = w.shape
    nc = C // cw
    nr = R // tr

    def body(w_ref, m_ref, v_ref, g_ref, g_out, d_out, m_out, v_out):
        g = g_ref[...][:, :cw]
        d, mn, vn = _adamw_math(w_ref[...], g, m_ref[...], v_ref[...])
        g_out[...] = g
        d_out[...] = d
        m_out[...] = mn
        v_out[...] = vn

    wspec = pl.BlockSpec((tr, cw), lambda i, j: (i, j))
    gspec = pl.BlockSpec((tr, gw), lambda i, j: (i, goff + j))
    outs, _ = _hosted(name, body, (nr, nc), [wspec, wspec, wspec, gspec], [wspec] * 4,
                      [jax.ShapeDtypeStruct((R, C), F32)] * 4, [], (w, m, v, gp), ("parallel", "parallel"), None, jobs)
    return outs


def _adamw_small(name, w, m, v, g):
    R, C = w.shape

    def body(w_ref, m_ref, v_ref, g_ref, d_out, m_out, v_out):
        d, mn, vn = _adamw_math(w_ref[...], g_ref[...], m_ref[...], v_ref[...])
        d_out[...] = d
        m_out[...] = mn
        v_out[...] = vn

    tr = _tile(R, 512)
    spec = pl.BlockSpec((tr, C), lambda i: (i, 0))
    return pl.pallas_call(
        body, name=name, grid=(R // tr,), in_specs=[spec] * 4, out_specs=[spec] * 3,
        out_shape=[jax.ShapeDtypeStruct((R, C), F32)] * 3, compiler_params=_cp(("parallel",)),
    )(w, m, v, g)


def _prep(name, parts, rows_p, cols_p):
    R, C = parts[0].shape
    n = len(parts)

    def body(*refs):
        o_ref = refs[n]
        if (R, C) != (rows_p, cols_p):
            o_ref[...] = jnp.zeros_like(o_ref)
        for i in range(n):
            o_ref[0:R, i * cols_p:i * cols_p + C] = refs[i][...].astype(BF16)

    return pl.pallas_call(
        body, name=name, out_shape=jax.ShapeDtypeStruct((rows_p, n * cols_p), BF16), compiler_params=_cp(),
    )(*parts)


def _attn_masks():
    lane = lax.broadcasted_iota(jnp.int32, (1, LANE), 1)
    return [(lane < HEAD_DIM), (lane >= HEAD_DIM)]


QH = QB
KW = QB + QH


def _band_valid(base):
    qi = lax.broadcasted_iota(jnp.int32, (QH, KW), 0)
    ki = lax.broadcasted_iota(jnp.int32, (QH, KW), 1)
    dist = qi + QB - ki
    return (dist >= 0) & (dist <= QB) & (base + ki - QB >= 0)


ATTN_T = max(DILATIONS) * QB


def _attn_groups(T):
    out = []
    for d in DILATIONS:
        for r in range(d):
            for l0 in range(0, T // d, QH):
                qrows = pl.ds(r + d * l0, QH, stride=d) if d > 1 else pl.ds(l0, QH)
                k0 = T + r + d * (l0 - QB)
                krows = pl.ds(k0, KW, stride=d) if d > 1 else pl.ds(k0, KW)
                out.append((d, qrows, krows, l0))
    return out


def _attn_specs(T, width_off):
    cur = pl.BlockSpec((T, LANE), lambda hp, b: (b, width_off + hp))
    prev = pl.BlockSpec((T, LANE), lambda hp, b: (jnp.maximum(b - 1, 0), width_off + hp))
    return cur, prev


def _attn_fwd(z):
    S, ZW = z.shape
    T = min(ATTN_T, S)
    scale = HEAD_DIM ** -0.5
    groups = _attn_groups(T)

    def body(q_ref, kc_ref, kp_ref, vc_ref, vp_ref, y_ref, m_ref, l_ref, kcat, vcat):
        b = pl.program_id(1)
        kcat[0:T, :] = kp_ref[...]
        kcat[T:, :] = kc_ref[...]
        vcat[0:T, :] = vp_ref[...]
        vcat[T:, :] = vc_ref[...]
        masks = _attn_masks()
        for d, qrows, krows, l0 in groups:
            q = q_ref[qrows, :]
            kk = kcat[krows, :].astype(BF16)
            vv = vcat[krows, :].astype(BF16)
            valid = _band_valid(b * (T // d) + l0)
            o_new = m_new = l_new = None
            for hm in masks:
                qh = jnp.where(hm, q, 0.0).astype(BF16)
                s = lax.dot_general(qh, kk, (((1,), (1,)), ((), ())), preferred_element_type=F32) * scale
                s = jnp.where(valid, s, MASK_VALUE)
                m = jnp.max(s, axis=-1, keepdims=True)
                p = jnp.exp(s - m)
                l = jnp.sum(p, axis=-1, keepdims=True)
                o = jnp.dot(p.astype(BF16), vv, preferred_element_type=F32)
                if o_new is None:
                    o_new, m_new, l_new = o, jnp.broadcast_to(m, (QH, LANE)), jnp.broadcast_to(l, (QH, LANE))
                else:
                    o_new = jnp.where(hm, o, o_new)
                    m_new = jnp.where(hm, m, m_new)
                    l_new = jnp.where(hm, l, l_new)
            if d == DILATIONS[0]:
                y_ref[qrows, :] = o_new
                m_ref[qrows, :] = m_new
                l_ref[qrows, :] = l_new
            else:
                m_old = m_ref[qrows, :]
                m_all = jnp.maximum(m_old, m_new)
                w_old, w_new = jnp.exp(m_old - m_all), jnp.exp(m_new - m_all)
                y_ref[qrows, :] = w_old * y_ref[qrows, :] + w_new * o_new
                l_ref[qrows, :] = w_old * l_ref[qrows, :] + w_new * l_new
                m_ref[qrows, :] = m_all
        y_ref[...] = y_ref[...] / l_ref[...]

    qc, _ = _attn_specs(T, 0)
    kc, kp = _attn_specs(T, ATTN_W // LANE)
    vc, vp = _attn_specs(T, 2 * ATTN_W // LANE)
    shp = jax.ShapeDtypeStruct((S, ATTN_W), F32)
    return pl.pallas_call(
        body, name="attn_fwd", grid=(ATTN_W // LANE, S // T),
        in_specs=[qc, kc, kp, vc, vp], out_specs=[qc, qc, qc], out_shape=[shp, shp, shp],
        scratch_shapes=[pltpu.VMEM((2 * T, LANE), F32), pltpu.VMEM((2 * T, LANE), F32)],
        compiler_params=_cp(("parallel", "parallel")),
    )(z, z, z, z, z)


def _attn_bwd(z, dya, ya, mg, den, jobs=None):
    S, ZW = z.shape
    T = min(ATTN_T, S)
    scale = HEAD_DIM ** -0.5
    groups = _attn_groups(T)

    def body(q_ref, kc_ref, kp_ref, vc_ref, vp_ref, dy_ref, y_ref, m_ref, n_ref, dq_ref, dk_ref, dv_ref,
             kcat, vcat, dkcat, dvcat):
        b = pl.program_id(1)

        @pl.when(b == 0)
        def _():
            dk_ref[...] = jnp.zeros_like(dk_ref)
            dv_ref[...] = jnp.zeros_like(dv_ref)

        kcat[0:T, :] = kp_ref[...]
        kcat[T:, :] = kc_ref[...]
        vcat[0:T, :] = vp_ref[...]
        vcat[T:, :] = vc_ref[...]
        dkcat[...] = jnp.zeros_like(dkcat)
        dvcat[...] = jnp.zeros_like(dvcat)
        dq_ref[...] = jnp.zeros_like(dq_ref)
        masks = _attn_masks()
        for d, rows, krows, l0 in groups:
            q, dy, y = q_ref[rows, :], dy_ref[rows, :], y_ref[rows, :]
            mrow, nrow = m_ref[rows, :], n_ref[rows, :]
            kk = kcat[krows, :].astype(BF16)
            vv = vcat[krows, :].astype(BF16)
            valid = _band_valid(b * (T // d) + l0)
            dq_acc = jnp.zeros((QH, LANE), F32)
            dk_acc = jnp.zeros((KW, LANE), F32)
            dv_acc = jnp.zeros((KW, LANE), F32)
            for hm in masks:
                qh = jnp.where(hm, q, 0.0).astype(BF16)
                dyh = jnp.where(hm, dy, 0.0)
                dyb = dyh.astype(BF16)
                dsum = jnp.sum(dyh * y, axis=-1, keepdims=True)
                mh = jnp.max(jnp.where(hm, mrow, MASK_VALUE), axis=-1, keepdims=True)
                nh = jnp.max(jnp.where(hm, nrow, 0.0), axis=-1, keepdims=True)
                s = lax.dot_general(qh, kk, (((1,), (1,)), ((), ())), preferred_element_type=F32) * scale
                p = jnp.where(valid, jnp.exp(s - mh), 0.0) * (1.0 / nh)
                pb = p.astype(BF16)
                dv_h = lax.dot_general(pb, dyb, (((0,), (0,)), ((), ())), preferred_element_type=F32)
                dp = lax.dot_general(dyb, vv, (((1,), (1,)), ((), ())), preferred_element_type=F32)
                ds = (p * (dp - dsum) * scale).astype(BF16)
                dq_h = jnp.dot(ds, kk, preferred_element_type=F32)
                dk_h = lax.dot_general(ds, qh, (((0,), (0,)), ((), ())), preferred_element_type=F32)
                dq_acc += jnp.where(hm, dq_h, 0.0)
                dk_acc += dk_h
                dv_acc += dv_h
            dq_ref[rows, :] += dq_acc
            dkcat[krows, :] += dk_acc
            dvcat[krows, :] += dv_acc

        base = pl.multiple_of(b * T, T)
        dk_ref[pl.ds(base, T), :] += dkcat[T:, :]
        dv_ref[pl.ds(base, T), :] += dvcat[T:, :]

        @pl.when(b > 0)
        def _():
            prev = pl.multiple_of(b * T - T, T)
            dk_ref[pl.ds(prev, T), :] += dkcat[0:T, :]
            dv_ref[pl.ds(prev, T), :] += dvcat[0:T, :]

    qc, _ = _attn_specs(T, 0)
    kc, kp = _attn_specs(T, ATTN_W // LANE)
    vc, vp = _attn_specs(T, 2 * ATTN_W // LANE)
    whole = pl.BlockSpec((S, LANE), lambda hp, b: (0, hp))
    shp = jax.ShapeDtypeStruct((S, ATTN_W), F32)
    outs, _ = _hosted(
        "attn_bwd", body, (ATTN_W // LANE, S // T), [qc, kc, kp, vc, vp, qc, qc, qc, qc], [qc, whole, whole],
        [shp, shp, shp], [pltpu.VMEM((2 * T, LANE), F32)] * 4, (z, z, z, z, z, dya, ya, mg, den),
        ("parallel", "arbitrary"), None, jobs)
    return outs


def _ssm_disc(lr, li, logdt, br, bi):
    dt = jnp.exp(logdt)
    mag = jnp.exp(lr * dt)
    ar = mag * jnp.cos(li * dt)
    ai = mag * jnp.sin(li * dt)
    nr, ni = ar - 1.0, ai
    den = lr * lr + li * li
    cr = (nr * lr + ni * li) / den
    ci = (ni * lr - nr * li) / den
    return ar, ai, cr * br - ci * bi, cr * bi + ci * br


def _ssm_prep(lr, li, logdt, br, bi):
    n, c = br.shape
    outs, _ = _rowwise("ssm_prep", lambda *a: (list(_ssm_disc(*a)), []), n, _tile(n, 512),
                       [_full(a) for a in (lr, li, logdt, br, bi)], [],
                       [(1, 1, _c0, F32), (1, 1, _c0, F32), (c, c, _c0, F32), (c, c, _c0, F32)])
    return outs


def _ssm_prep_bwd(lr, li, logdt, br, bi, dar, dai, dbbr, dbbi):
    n, c = br.shape

    def f(lrb, lib, dtb, brb, bib, *cts):
        _, vjp = jax.vjp(_ssm_disc, lrb, lib, dtb, brb, bib)
        return list(vjp(cts)), []

    outs, _ = _rowwise("ssm_prep_bwd", f, n, _tile(n, 512),
                       [_full(a) for a in (lr, li, logdt, br, bi, dar, dai, dbbr, dbbi)], [],
                       [(1, 1, _c0, F32)] * 3 + [(c, c, _c0, F32)] * 2)
    return outs


def _cmul(ar, ai, br, bi):
    return ar * br - ai * bi, ar * bi + ai * br


def _scan_consts(ar, ai, reverse):
    w = ar.shape[-1]
    a1 = (jnp.broadcast_to(ar, (8, w)), jnp.broadcast_to(ai, (8, w)))
    a2 = _cmul(*a1, *a1)
    a4 = _cmul(*a2, *a2)
    a8 = _cmul(*a4, *a4)
    row = lax.broadcasted_iota(jnp.int32, (8, w), 0)
    e = (8 - row) if reverse else (row + 1)
    one, zero = jnp.ones((8, w), F32), jnp.zeros((8, w), F32)
    pw = (one, zero)
    for bit, ap in ((1, a1), (2, a2), (4, a4), (8, a8)):
        sel = (e & bit) != 0
        nxt = _cmul(*pw, *ap)
        pw = (jnp.where(sel, nxt[0], pw[0]), jnp.where(sel, nxt[1], pw[1]))
    steps = []
    for sh, (pr, pi) in zip((1, 2, 4), (a1, a2, a4)):
        keep = (row < 8 - sh) if reverse else (row >= sh)
        steps.append((jnp.where(keep, pr, 0.0), jnp.where(keep, pi, 0.0)))
    return steps, pw, row


def _scan_group(xr, xi, cr, ci, consts, reverse):
    steps, pw, _ = consts
    for sh, (pr, pi) in zip((1, 2, 4), steps):
        by = 8 - sh if reverse else sh
        tr_, ti_ = _cmul(pr, pi, pltpu.roll(xr, by, 0), pltpu.roll(xi, by, 0))
        xr = xr + tr_
        xi = xi + ti_
    tr_, ti_ = _cmul(pw[0], pw[1], cr, ci)
    return xr + tr_, xi + ti_


def _ssm_fwd(z, a_r, a_i, bdr, bdi, cmr, cmi, dskip, ts, rider=None):
    S, ZW = z.shape
    NS = SSM_G * SSM_P
    PW = PACK * SSM_P
    uoff = (ZW - SSM_W) // LANE
    nsteps = S // ts

    def body(u_ref, ar_ref, ai_ref, bdr_ref, bdi_ref, cmr_ref, cmi_ref, d_ref, hr_ref, hi_ref, y_ref, car_r, car_i):
        s = pl.program_id(1)

        @pl.when(s == 0)
        def _():
            car_r[...] = jnp.zeros_like(car_r)
            car_i[...] = jnp.zeros_like(car_i)

        u = u_ref[...]
        ub = u.astype(BF16)
        nt = (((1,), (1,)), ((), ()))
        hr_ref[...] = lax.dot_general(ub, bdr_ref[...], nt, preferred_element_type=F32)
        hi_ref[...] = lax.dot_general(ub, bdi_ref[...], nt, preferred_element_type=F32)
        consts = _scan_consts(ar_ref[...], ai_ref[...], False)

        def step(j, carry):
            rows = pl.ds(pl.multiple_of(j * 8, 8), 8)
            hr, hi = _scan_group(hr_ref[rows, :], hi_ref[rows, :], carry[0], carry[1], consts, False)
            hr_ref[rows, :] = hr
            hi_ref[rows, :] = hi
            return jnp.broadcast_to(hr[7:8, :], (8, PW)), jnp.broadcast_to(hi[7:8, :], (8, PW))

        cr, ci = lax.fori_loop(0, ts // 8, step, (car_r[...], car_i[...]))
        car_r[...] = cr
        car_i[...] = ci
        y = lax.dot_general(hr_ref[...].astype(BF16), cmr_ref[...], nt, preferred_element_type=F32)
        y -= lax.dot_general(hi_ref[...].astype(BF16), cmi_ref[...], nt, preferred_element_type=F32)
        y_ref[...] = y + d_ref[...] * u

    row_a = pl.BlockSpec((1, PW), lambda i, s: (0, i))
    (hr, hi, y), extra = _hosted(
        "ssm_fwd", body, (SSM_G // PACK, nsteps),
        [pl.BlockSpec((ts, LANE), lambda i, s: (s, uoff + i)), row_a, row_a,
         pl.BlockSpec((None, PW, LANE), lambda i, s: (i, 0, 0)), pl.BlockSpec((None, PW, LANE), lambda i, s: (i, 0, 0)),
         pl.BlockSpec((None, LANE, PW), lambda i, s: (i, 0, 0)), pl.BlockSpec((None, LANE, PW), lambda i, s: (i, 0, 0)),
         pl.BlockSpec((1, LANE), lambda i, s: (0, i))],
        [pl.BlockSpec((ts, PW), lambda i, s: (s, i)), pl.BlockSpec((ts, PW), lambda i, s: (s, i)),
         pl.BlockSpec((ts, LANE), lambda i, s: (s, i))],
        [jax.ShapeDtypeStruct((S, NS), F32), jax.ShapeDtypeStruct((S, NS), F32), jax.ShapeDtypeStruct((S, SSM_W), F32)],
        [pltpu.VMEM((8, PW), F32), pltpu.VMEM((8, PW), F32)], (z, a_r, a_i, bdr, bdi, cmr, cmi, dskip),
        ("parallel", "arbitrary"), rider)
    return hr, hi, y, extra


def _ssm_bwd(z, dyp, hr, hi, a_r, a_i, bdr, bdi, cmr, cmi, dskip, ts, jobs=None):
    S, ZW = z.shape
    NS = SSM_G * SSM_P
    PW = PACK * SSM_P
    uoff = (ZW - SSM_W) // LANE
    nsteps = S // ts
    npk = SSM_G // PACK

    def body(u_ref, dy_ref, hr_ref, hi_ref, hpr_ref, hpi_ref, ar_ref, ai_ref, bdr_ref, bdi_ref, cmr_ref, cmi_ref,
             d_ref, du_ref, dbdr_ref, dbdi_ref, dcmr_ref, dcmi_ref, dar_ref, dai_ref, dd_ref,
             lr_s, li_s, hcr, hci, car_r, car_i):
        s = pl.program_id(1)
        first_tile = s == nsteps - 1

        @pl.when(s == 0)
        def _():
            car_r[...] = jnp.zeros_like(car_r)
            car_i[...] = jnp.zeros_like(car_i)
            for r in (dbdr_ref, dbdi_ref, dcmr_ref, dcmi_ref, dar_ref, dai_ref, dd_ref):
                r[...] = jnp.zeros_like(r)

        u, dy = u_ref[...], dy_ref[...]
        ub, dyb = u.astype(BF16), dy.astype(BF16)
        lr_s[...] = jnp.dot(dyb, cmr_ref[...], preferred_element_type=F32)
        li_s[...] = -jnp.dot(dyb, cmi_ref[...], preferred_element_type=F32)
        keep_prev = jnp.where(first_tile, 0.0, 1.0)
        hcr[0:8, :] = hpr_ref[...] * keep_prev
        hci[0:8, :] = hpi_ref[...] * keep_prev
        hcr[8:, :] = hr_ref[...]
        hci[8:, :] = hi_ref[...]
        consts = _scan_consts(ar_ref[...], -ai_ref[...], True)
        row = consts[2]
        ngrp = ts // 8

        def step(jj, carry):
            cr, ci, accr, acci = carry
            j = ngrp - 1 - jj
            rows = pl.ds(pl.multiple_of(j * 8, 8), 8)
            nxt = pl.ds(pl.multiple_of(j * 8 + 8, 8), 8)
            lr, li = _scan_group(lr_s[rows, :], li_s[rows, :], cr, ci, consts, True)
            lr_s[rows, :] = lr
            li_s[rows, :] = li
            pr, pi = hcr[rows, :], hci[rows, :]
            hsr = jnp.where(row == 0, jnp.broadcast_to(pr[7:8, :], (8, PW)), pltpu.roll(hcr[nxt, :], 1, 0))
            hsi = jnp.where(row == 0, jnp.broadcast_to(pi[7:8, :], (8, PW)), pltpu.roll(hci[nxt, :], 1, 0))
            accr = accr + lr * hsr + li * hsi
            acci = acci + li * hsr - lr * hsi
            return jnp.broadcast_to(lr[0:1, :], (8, PW)), jnp.broadcast_to(li[0:1, :], (8, PW)), accr, acci

        zero = jnp.zeros((8, PW), F32)
        cr, ci, accr, acci = lax.fori_loop(0, ngrp, step, (car_r[...], car_i[...], zero, zero))
        car_r[...] = cr
        car_i[...] = ci
        dar_ref[...] += jnp.sum(accr, axis=0, keepdims=True)
        dai_ref[...] += jnp.sum(acci, axis=0, keepdims=True)
        lrb, lib = lr_s[...].astype(BF16), li_s[...].astype(BF16)
        du = jnp.dot(lrb, bdr_ref[...], preferred_element_type=F32)
        du += jnp.dot(lib, bdi_ref[...], preferred_element_type=F32)
        du_ref[...] = du + dy * d_ref[...]
        tn = (((0,), (0,)), ((), ()))
        dbdr_ref[...] += lax.dot_general(lrb, ub, tn, preferred_element_type=F32)
        dbdi_ref[...] += lax.dot_general(lib, ub, tn, preferred_element_type=F32)
        dcmr_ref[...] += lax.dot_general(dyb, hr_ref[...].astype(BF16), tn, preferred_element_type=F32)
        dcmi_ref[...] -= lax.dot_general(dyb, hi_ref[...].astype(BF16), tn, preferred_element_type=F32)
        dd_ref[...] += jnp.sum(dy * u, axis=0, keepdims=True)

    rev = lambda s: nsteps - 1 - s
    row_a = pl.BlockSpec((1, PW), lambda i, s: (0, i))
    tile = pl.BlockSpec((ts, PW), lambda i, s: (rev(s), i))
    prev8 = pl.BlockSpec((8, PW), lambda i, s: (jnp.maximum(rev(s) * (ts // 8) - 1, 0), i))
    cols = pl.BlockSpec((ts, LANE), lambda i, s: (rev(s), i))
    bd = pl.BlockSpec((None, PW, LANE), lambda i, s: (i, 0, 0))
    cm = pl.BlockSpec((None, LANE, PW), lambda i, s: (i, 0, 0))
    outs, _ = _hosted(
        "ssm_bwd", body, (npk, nsteps),
        [pl.BlockSpec((ts, LANE), lambda i, s: (rev(s), uoff + i)), cols, tile, tile, prev8, prev8,
         row_a, row_a, bd, bd, cm, cm, pl.BlockSpec((1, LANE), lambda i, s: (0, i))],
        [cols, bd, bd, cm, cm, row_a, row_a, pl.BlockSpec((1, LANE), lambda i, s: (0, i))],
        [jax.ShapeDtypeStruct((S, SSM_W), F32),
         jax.ShapeDtypeStruct((npk, PW, LANE), F32), jax.ShapeDtypeStruct((npk, PW, LANE), F32),
         jax.ShapeDtypeStruct((npk, LANE, PW), F32), jax.ShapeDtypeStruct((npk, LANE, PW), F32),
         jax.ShapeDtypeStruct((1, NS), F32), jax.ShapeDtypeStruct((1, NS), F32), jax.ShapeDtypeStruct((1, SSM_W), F32)],
        [pltpu.VMEM((ts, PW), F32), pltpu.VMEM((ts, PW), F32), pltpu.VMEM((ts + 8, PW), F32),
         pltpu.VMEM((ts + 8, PW), F32), pltpu.VMEM((8, PW), F32), pltpu.VMEM((8, PW), F32)],
        (z, dyp, hr, hi, hr, hi, a_r, a_i, bdr, bdi, cmr, cmi, dskip), ("parallel", "arbitrary"), None, jobs)
    return outs


def _block_diag(m4):
    npk, g, a, b = m4.shape
    eye = jnp.eye(g, dtype=m4.dtype)
    return (m4[:, :, :, None, :] * eye[None, :, None, :, None]).reshape(npk, g * a, g * b)


def _block_diag_take(m, a, b):
    npk = m.shape[0]
    m5 = m.reshape(npk, PACK, a, PACK, b)
    return jnp.stack([m5[:, g, :, g, :] for g in range(PACK)], axis=1)


def _mix_out(ya, ypre, gl, ga, gb, bglu):
    yg = jax.nn.gelu(ypre)
    yb = yg * jax.nn.sigmoid(gl + bglu)
    return jnp.concatenate([_rms(ya, ga), _rms(yb, gb)], axis=-1)


def _tail_loss(h3, gl, pe, gf, tgt):
    h4 = h3 + jax.nn.sigmoid(gl) * pe
    err = jnp.square(_rms(h4, gf) - tgt)
    return 0.5 * jnp.mean(err, axis=-1, keepdims=True)


def kernel(x, p, ffn1_norm, ffn1_w_gate, ffn1_w_up, ffn1_w_down, mix_norm, w_in, attn_out_norm, ssm_lambda_re, ssm_lambda_im, ssm_log_dt, ssm_b_re, ssm_b_im, ssm_c_re, ssm_c_im, ssm_d, ssm_w_glu, ssm_b_glu, ssm_out_norm, w_out, ffn2_norm, ffn2_w_gate, ffn2_w_up, ffn2_w_down, ple_norm, ple_w_gate, ple_w_proj, final_norm, loss_target, m_ffn1_norm, m_ffn1_w_gate, m_ffn1_w_up, m_ffn1_w_down, m_mix_norm, m_w_in, m_attn_out_norm, m_ssm_lambda_re, m_ssm_lambda_im, m_ssm_log_dt, m_ssm_b_re, m_ssm_b_im, m_ssm_c_re, m_ssm_c_im, m_ssm_d, m_ssm_w_glu, m_ssm_b_glu, m_ssm_out_norm, m_w_out, m_ffn2_norm, m_ffn2_w_gate, m_ffn2_w_up, m_ffn2_w_down, m_ple_norm, m_ple_w_gate, m_ple_w_proj, m_final_norm, v_ffn1_norm, v_ffn1_w_gate, v_ffn1_w_up, v_ffn1_w_down, v_mix_norm, v_w_in, v_attn_out_norm, v_ssm_lambda_re, v_ssm_lambda_im, v_ssm_log_dt, v_ssm_b_re, v_ssm_b_im, v_ssm_c_re, v_ssm_c_im, v_ssm_d, v_ssm_w_glu, v_ssm_b_glu, v_ssm_out_norm, v_w_out, v_ffn2_norm, v_ffn2_w_gate, v_ffn2_w_up, v_ffn2_w_down, v_ple_norm, v_ple_w_gate, v_ple_w_proj, v_final_norm):
    A = dict(locals())
    xs = x[0]
    ps = p[0, 0]
    tgt = loss_target[0]
    S, D = xs.shape
    FSH = ffn1_w_gate.shape[-1]
    FSP = -(-FSH // LANE) * LANE
    TR = _tile(S, 256)
    ZW = 3 * ATTN_W + SSM_W

    wgu1 = _prep("prep_gu1", [ffn1_w_gate[0], ffn1_w_up[0]], D, FSP)
    wgu2 = _prep("prep_gu2", [ffn2_w_gate[0], ffn2_w_up[0]], D, FSP)
    wd1 = _prep("prep_d1", [ffn1_w_down[0]], FSP, D)
    wd2 = _prep("prep_d2", [ffn2_w_down[0]], FSP, D)
    win = _prep("prep_in", [w_in[0]], D, w_in.shape[-1])
    wglu = _prep("prep_glu", [ssm_w_glu[0]], ssm_w_glu.shape[1], SSM_W)
    wout = _prep("prep_out", [w_out[0]], w_out.shape[1], D)
    wpg = _prep("prep_pg", [ple_w_gate[0]], ple_w_gate.shape[1], D)
    wpp = _prep("prep_pp", [ple_w_proj[0]], ple_w_proj.shape[1], ple_w_proj.shape[2])
    (Wgu1,) = _all_gather("ag_weights", [wgu1])
    rowstack = lambda w: w.reshape(1, w.shape[0] * w.shape[1], w.shape[2])

    def ffn_norm(tag, h, gain):
        return _rowwise(f"{tag}_norm", lambda a, g: ([_rms(a, g)], []), S, TR, [_full(h)], [gain], [(D, D, _c0, BF16)])[0][0]

    xn1 = ffn_norm("ffn1", xs, ffn1_norm)
    gu1, hid1, (Wd1, Win) = _ffn_up("ffn1_up", xn1, Wgu1, rider=_GatherRider([wd1, win]))
    Wd1 = rowstack(Wd1)
    h1, (Wd2,) = _mm_nn("ffn1_down", hid1, Wd1, tn=D // 2, tk=NDEV * FSP, res=xs, scale=0.5,
                        rider=_GatherRider([wd2]))
    Wd2 = rowstack(Wd2)
    un = ffn_norm("mix", h1, mix_norm)
    z, (Wglu, Wout, Wpg, Wpp) = _mm_nn("mix_in", un, Win, tn=512, tk=D,
                                       rider=_GatherRider([wglu, wout, wpg, wpp]))
    Wglu, Wout, Wpg = rowstack(Wglu), rowstack(Wout), rowstack(Wpg)
    ya, mg, den = _attn_fwd(z)

    col = lambda a: a.reshape(-1, 1)
    lr_c, li_c = col(ssm_lambda_re), col(ssm_lambda_im)
    dt_c = col(jnp.broadcast_to(ssm_log_dt.reshape(SSM_G, 1), (SSM_G, SSM_P)))
    b_re2, b_im2 = ssm_b_re.reshape(-1, SSM_C), ssm_b_im.reshape(-1, SSM_C)
    ar_c, ai_c, bbr, bbi = _ssm_prep(lr_c, li_c, dt_c, b_re2, b_im2)
    a_r, a_i = ar_c.reshape(1, -1), ai_c.reshape(1, -1)
    npk = SSM_G // PACK
    bdr = _block_diag(bbr.reshape(npk, PACK, SSM_P, SSM_C)).astype(BF16)
    bdi = _block_diag(bbi.reshape(npk, PACK, SSM_P, SSM_C)).astype(BF16)
    cmr = _block_diag(ssm_c_re.reshape(npk, PACK, SSM_C, SSM_P)).astype(BF16)
    cmi = _block_diag(ssm_c_im.reshape(npk, PACK, SSM_C, SSM_P)).astype(BF16)
    TS = _tile(S, 512)
    hr, hi, ypre, (Wgu2,) = _ssm_fwd(z, a_r, a_i, bdr, bdi, cmr, cmi, ssm_d, TS, rider=_GatherRider([wgu2]))
    (yg,), _ = _rowwise("ssm_gelu", lambda a: ([jax.nn.gelu(a)], []), S, TR, [_full(ypre)], [], [(SSM_W, SSM_W, _c0, BF16)])
    gl = _mm_nn("ssm_glu", yg, Wglu, tn=SSM_W, tk=SSM_W)
    (ycat,), _ = _rowwise("mix_out", lambda *a: ([_mix_out(*a)], []), S, TR, [_full(ya), _full(ypre), _full(gl)],
                          [attn_out_norm, ssm_out_norm, ssm_b_glu], [(MIX_W, MIX_W, _c0, BF16)])
    h2, xn2 = _mm_nn("mix_proj", ycat, Wout, tn=D, tk=D, res=h1, scale=1.0, gain=ffn2_norm)
    gu2, hid2, _ = _ffn_up("ffn2_up", xn2, Wgu2)
    h3 = _mm_nn("ffn2_down", hid2, Wd2, tn=D // 2, tk=NDEV * FSP, res=h2, scale=0.5)
    hn = ffn_norm("ple", h3, ple_norm)
    pgl = _mm_nn("ple_gate", hn, Wpg, tn=D // 2, tk=D)
    pb = ps
    pe = _mm_nn("ple_proj", pb, Wpp, tn=Wpp.shape[2], tk=Wpp.shape[1])

    def tail(h3b, glb, peb, tb, gf):
        rows, vjp = jax.vjp(lambda a, b, c, g: _tail_loss(a, b, c, g, tb), h3b, glb, peb, gf)
        dh, dgl, dpe, dgf = vjp(jnp.ones_like(rows))
        return [dh, dgl, dpe], [jnp.broadcast_to(jnp.sum(rows, axis=0, keepdims=True), (1, LANE)), dgf]

    (dh3_dir, dpgl, dpe), (loss_row, g_final) = _rowwise(
        "tail", tail, S, TR, [_full(h3), _full(pgl), _full(pe), _full(tgt)], [final_norm.reshape(1, D)],
        [(D, D, _c0, F32), (D, D, _c0, BF16), (D, D, _c0, BF16)], [(LANE, LANE, _c0), (D, D, _c0)])
    loss = lax.psum(loss_row[0, 0], AXES)

    def norm_bwd(tag, h, gain, dn, dres):
        def f(hb, dnb, drb, g):
            _, vjp = jax.vjp(_rms, hb, g)
            dh, dg = vjp(dnb)
            dh = dh + drb
            return [dh, dh], [dg]
        (dh, dhb), (dg,) = _rowwise(f"{tag}_norm_bwd", f, S, TR, [_full(h), _full(dn), _full(dres)], [gain],
                                    [(D, D, _c0, F32), (D, D, _c0, BF16)], [(D, D, _c0)], jobs=jobs)
        return dh, dhb, dg

    restack = lambda g: g.reshape((NDEV, g.shape[1] // NDEV) + g.shape[2:])
    jobs, scat = [], {}

    def scatter(key, g):
        scat[key] = _Scatter("rs_" + key, g)
        jobs.append(scat[key])

    late = []
    dhn = _mm_nt("ple_gate_dx", dpgl, Wpg, tn=D, tk=D)
    late.append(lambda: scatter("pg", restack(_mm_tn("ple_gate_dw", hn, dpgl, 1, jobs=jobs))))
    late.append(lambda: scatter("pp", _mm_tn("ple_proj_dw", pb, dpe, NDEV, jobs=jobs)))
    dh3, dh3b, g_ple_norm = norm_bwd("ple", h3, ple_norm, dhn, dh3_dir)

    def ffn_bwd(tag, h, gain, Wgu, Wd, saved, dout, doutb):
        xn, gu, hid = saved
        dgu = _ffn_down_dx(f"{tag}_down_dx", doutb, Wd, gu, NDEV, scale=0.5, jobs=jobs)
        scatter(tag + "gu", _mm_tn(f"{tag}_up_dw", xn, dgu, NDEV, tn=FSP, jobs=jobs))
        scatter(tag + "d", restack(_mm_tn(f"{tag}_down_dw", hid, doutb, 1, tko=FSP, tn=256, scale=0.5, jobs=jobs)))
        dxn = _mm_nt(f"{tag}_up_dx", dgu, Wgu, tm=512, tn=256, jb=NDEV, jobs=jobs)
        dh, dhb, g_norm = norm_bwd(tag, h, gain, dxn, dout)
        return dh, dhb, g_norm

    dh2, dh2b, g_ffn2_norm = ffn_bwd("ffn2", h2, ffn2_norm, Wgu2, Wd2, (xn2, gu2, hid2), dh3, dh3b)

    dycat = _mm_nt("mix_proj_dx", dh2b, Wout, tn=D, tk=D, jobs=jobs)
    late.append(lambda: scatter("out", restack(_mm_tn("mix_proj_dw", ycat, dh2b, 1, jobs=jobs))))

    def mix_out_bwd(yab, ypb, glb, dyc, ga, gb, bglu):
        _, vjp = jax.vjp(_mix_out, yab, ypb, glb, ga, gb, bglu)
        dya_, dyp_, dgl_, dga, dgb, dbg = vjp(dyc)
        return [dya_, dyp_, dgl_], [dga, dgb, dbg]
    (dya, dyp_dir, dglb), (g_attn_norm, g_ssm_norm, g_bglu) = _rowwise(
        "mix_out_bwd", mix_out_bwd, S, TR, [_full(ya), _full(ypre), _full(gl), _full(dycat)],
        [attn_out_norm, ssm_out_norm, ssm_b_glu],
        [(ATTN_W, ATTN_W, _c0, F32), (SSM_W, SSM_W, _c0, F32), (SSM_W, SSM_W, _c0, BF16)],
        [(ATTN_W, ATTN_W, _c0), (SSM_W, SSM_W, _c0), (SSM_W, SSM_W, _c0)])
    dyg = _mm_nt("ssm_glu_dx", dglb, Wglu, tn=SSM_W, tk=SSM_W, jobs=jobs)
    late.append(lambda: scatter("glu", restack(_mm_tn("ssm_glu_dw", yg, dglb, 1, jobs=jobs))))

    def gelu_bwd(ypb, dygb, ddir):
        _, vjp = jax.vjp(jax.nn.gelu, ypb)
        return [ddir + vjp(dygb)[0]], []
    (dyp,), _ = _rowwise("ssm_gelu_bwd", gelu_bwd, S, TR, [_full(ypre), _full(dyg), _full(dyp_dir)], [],
                         [(SSM_W, SSM_W, _c0, F32)])
    du, dbdr, dbdi, dcmr, dcmi, da_r, da_i, g_ssm_d = _ssm_bwd(z, dyp, hr, hi, a_r, a_i, bdr, bdi, cmr, cmi, ssm_d, TS,
                                                              jobs=jobs)
    dbbr = _block_diag_take(dbdr, SSM_P, SSM_C).reshape(-1, SSM_C)
    dbbi = _block_diag_take(dbdi, SSM_P, SSM_C).reshape(-1, SSM_C)
    g_c_re = _block_diag_take(dcmr, SSM_C, SSM_P).reshape(ssm_c_re.shape)
    g_c_im = _block_diag_take(dcmi, SSM_C, SSM_P).reshape(ssm_c_im.shape)
    dlr, dli, ddt, g_b_re, g_b_im = _ssm_prep_bwd(lr_c, li_c, dt_c, b_re2, b_im2, col(da_r), col(da_i), dbbr, dbbi)
    g_lam_re, g_lam_im = dlr.reshape(ssm_lambda_re.shape), dli.reshape(ssm_lambda_im.shape)
    g_log_dt = jnp.sum(ddt.reshape(SSM_G, SSM_P), axis=1).reshape(ssm_log_dt.shape)
    g_b_re, g_b_im = g_b_re.reshape(ssm_b_re.shape), g_b_im.reshape(ssm_b_im.shape)

    dq, dk, dv = _attn_bwd(z, dya, ya, mg, den, jobs=jobs)
    (dz,), _ = _rowwise("mix_dz", lambda *a: ([jnp.concatenate(a, axis=-1)], []), S, TR,
                        [_full(dq), _full(dk), _full(dv), _full(du)], [], [(ZW, ZW, _c0, BF16)])
    dun = _mm_nt("mix_in_dx", dz, Win, tm=512, tn=512, jb=NDEV, jobs=jobs)
    scatter("in", _mm_tn("mix_in_dw", un, dz, NDEV, tn=512, jobs=jobs))
    dh1, dh1b, g_mix_norm = norm_bwd("mix", h1, mix_norm, dun, dh2)

    dx, _dxb, g_ffn1_norm = ffn_bwd("ffn1", xs, ffn1_norm, Wgu1, Wd1, (xn1, gu1, hid1), dh1, dh1b)
    for run in (late[2], late[0], late[3], late[1]):
        run()

    out = {}

    def upd(name, key, *, tr, cw, gw, goff=0):
        w, m, v = A[name][0], A["m_" + name][0], A["v_" + name][0]
        g, dlt, mn, vn = _adamw("adamw_" + name, w, m, v, scat[key].finish(), tr=tr, cw=cw, gw=gw, goff=goff, jobs=jobs)
        for k, val in (("grad_", g), ("delta_", dlt), ("new_m_", mn), ("new_v_", vn)):
            out[k + name] = val[None]

    DT = _tile(D, 256)
    FT = _tile(FSH, 512)
    DC = _tile(D, 1024, LANE)
    upd("ffn2_w_gate", "ffn2gu", tr=DT, cw=FSH, gw=FSP, goff=0)
    upd("ffn2_w_up", "ffn2gu", tr=DT, cw=FSH, gw=FSP, goff=1)
    upd("ffn2_w_down", "ffn2d", tr=FT, cw=DC, gw=DC)
    upd("w_in", "in", tr=DT, cw=w_in.shape[-1], gw=w_in.shape[-1])
    upd("ffn1_w_gate", "ffn1gu", tr=DT, cw=FSH, gw=FSP, goff=0)
    upd("ffn1_w_up", "ffn1gu", tr=DT, cw=FSH, gw=FSP, goff=1)
    upd("ffn1_w_down", "ffn1d", tr=FT, cw=DC, gw=DC)
    upd("w_out", "out", tr=w_out.shape[1], cw=DC, gw=DC)
    upd("ple_w_gate", "pg", tr=ple_w_gate.shape[1], cw=DC, gw=DC)
    upd("ssm_w_glu", "glu", tr=ssm_w_glu.shape[1], cw=SSM_W, gw=SSM_W)
    upd("ple_w_proj", "pp", tr=ple_w_proj.shape[1], cw=ple_w_proj.shape[2], gw=ple_w_proj.shape[2])

    small = [("ffn1_norm", g_ffn1_norm), ("mix_norm", g_mix_norm), ("attn_out_norm", g_attn_norm),
             ("ssm_lambda_re", g_lam_re), ("ssm_lambda_im", g_lam_im), ("ssm_log_dt", g_log_dt),
             ("ssm_b_re", g_b_re), ("ssm_b_im", g_b_im), ("ssm_c_re", g_c_re), ("ssm_c_im", g_c_im),
             ("ssm_d", g_ssm_d), ("ssm_b_glu", g_bglu), ("ssm_out_norm", g_ssm_norm), ("ffn2_norm", g_ffn2_norm),
             ("ple_norm", g_ple_norm), ("final_norm", g_final)]
    chunk = 8 * LANE

    def pack(arrs):
        parts = []
        for a in arrs:
            flat = a.reshape(-1)
            padn = -(-flat.shape[0] // chunk) * chunk
            parts.append(jnp.pad(flat, (0, padn - flat.shape[0])).reshape(-1, LANE))
        return jnp.concatenate(parts, axis=0)

    g_pack = pack([g for _, g in small])
    (g_all,) = _all_gather("ag_small", [g_pack])
    g_sum = _sum8("small_sum", g_all)
    w_pack = pack([A[n] for n, _ in small])
    m_pack = pack([A["m_" + n] for n, _ in small])
    v_pack = pack([A["v_" + n] for n, _ in small])
    d_pack, mn_pack, vn_pack = _adamw_small("adamw_small", w_pack, m_pack, v_pack, g_sum)
    off = 0
    for n, _ in small:
        shape = A[n].shape
        size = math.prod(shape)
        rows = -(-size // chunk) * 8
        for k, buf in (("grad_", g_sum), ("delta_", d_pack), ("new_m_", mn_pack), ("new_v_", vn_pack)):
            out[k + n] = buf[off:off + rows].reshape(-1)[:size].reshape(shape)
        off += rows

    names = ['ffn1_norm', 'ffn1_w_gate', 'ffn1_w_up', 'ffn1_w_down', 'mix_norm', 'w_in', 'attn_out_norm',
             'ssm_lambda_re', 'ssm_lambda_im', 'ssm_log_dt', 'ssm_b_re', 'ssm_b_im', 'ssm_c_re', 'ssm_c_im', 'ssm_d',
             'ssm_w_glu', 'ssm_b_glu', 'ssm_out_norm', 'w_out', 'ffn2_norm', 'ffn2_w_gate', 'ffn2_w_up', 'ffn2_w_down',
             'ple_norm', 'ple_w_gate', 'ple_w_proj', 'final_norm']
    return (loss, dx[None], *[out[k + n] for k in ("grad_", "delta_", "new_m_", "new_v_") for n in names])
```

```python
import functools
import math

import jax
import jax.numpy as jnp
from jax import lax
from jax.experimental import pallas as pl
from jax.experimental.pallas import tpu as pltpu

F32, BF16 = jnp.float32, jnp.bfloat16
MESH = pl.DeviceIdType.MESH
NDEV = 8
AXES = ("x", "y", "c")
LANE = 128
VMEM_LIMIT = 60 * 1024 * 1024

ATTN_W = 1024
HEAD_DIM = 64
SSM_W = 1024
MIX_W = ATTN_W + SSM_W
SSM_G, SSM_P, SSM_C = 64, 64, 16
PACK = 8
DILATIONS = (1, 4, 16)
QB = 128
NORM_EPS = 1e-6
MASK_VALUE = -1e30
LR, B1, B2, EPS, WD, STEP = 0.001, 0.9, 0.999, 1e-08, 0.01, 10


def _cp(sem=None):
    return pltpu.CompilerParams(dimension_semantics=sem, vmem_limit_bytes=VMEM_LIMIT)


def _tile(n, target, mult=8):
    if n <= target:
        return n
    for t in range(target - target % mult, 0, -mult):
        if n % t == 0:
            return t
    return n


def _rms(x, g):
    return x * lax.rsqrt(jnp.mean(x * x, axis=-1, keepdims=True) + NORM_EPS) * g


def _rowwise(name, fn, S, tr, rows, fulls, outs, accs=(), ncol=1, jobs=None):
    nr, nf, no, na = len(rows), len(fulls), len(outs), len(accs)

    def body(*refs):
        ins = [r[...] for r in refs[:nr + nf]]
        o_refs = refs[nr + nf:nr + nf + no]
        a_refs = refs[nr + nf + no:]
        o_vals, a_vals = fn(*ins)
        for r, v in zip(o_refs, o_vals):
            r[...] = v.astype(r.dtype)
        if na:
            @pl.when(pl.program_id(1) == 0)
            def _():
                for r in a_refs:
                    r[...] = jnp.zeros_like(r)
            for r, v in zip(a_refs, a_vals):
                r[...] += v

    in_specs = [pl.BlockSpec((tr, w), functools.partial(lambda j, i, cm: (i, cm(j)), cm=cm)) for _, w, cm in rows]
    in_specs += [pl.BlockSpec(f.shape, functools.partial(lambda j, i, nd: (0,) * nd, nd=f.ndim)) for f in fulls]
    out_specs = [pl.BlockSpec((tr, w), functools.partial(lambda j, i, cm: (i, cm(j)), cm=cm)) for _, w, cm, _ in outs]
    out_specs += [pl.BlockSpec((1, w), functools.partial(lambda j, i, cm: (0, cm(j)), cm=cm)) for _, w, cm in accs]
    out_shape = [jax.ShapeDtypeStruct((S, c), dt) for c, _, _, dt in outs]
    out_shape += [jax.ShapeDtypeStruct((1, c), F32) for c, _, _ in accs]
    res, _ = _hosted(name, body, (ncol, S // tr), in_specs, out_specs, out_shape, [],
                     [a for a, _, _ in rows] + list(fulls), ("parallel", "arbitrary" if na else "parallel"), None, jobs)
    return res[:no], res[no:]


def _c0(j):
    return 0


def _full(a):
    return (a, a.shape[1], _c0)


def _hosted(name, body, grid, in_specs, out_specs, out_shape, scratch, args, sem, rider=None, jobs=None):
    nsteps = math.prod(grid)

    def step_of(*g):
        t = 0
        for gi, n in zip(g, grid):
            t = t * n + gi
        return t

    job = None
    if rider is None and jobs:
        def block_bytes(spec, like):
            shape = [d for d in (spec.block_shape or ()) if d is not None]
            return math.prod(shape) * jnp.dtype(like.dtype).itemsize if shape else 0

        held = 2 * sum(block_bytes(s, a) for s, a in zip(list(in_specs) + list(out_specs), list(args) + list(out_shape)))
        held += sum(math.prod(s.shape) * jnp.dtype(s.dtype).itemsize for s in scratch
                    if getattr(s, "dtype", None) in (F32, BF16))
        job, rider = _pick(jobs, nsteps, small_host=held + 16 * RIDER_TILE_BYTES + (8 << 20) <= VMEM_LIMIT)
    if rider is None:
        outs = pl.pallas_call(body, name=name, grid=grid, in_specs=in_specs, out_specs=out_specs, out_shape=out_shape,
                              scratch_shapes=scratch, compiler_params=_cp(sem))(*args)
        return outs, None
    rider.bind(step_of, nsteps)
    n_in, n_out, n_scr = len(in_specs), len(out_specs), len(scratch)

    def full(*refs):
        a, b = n_in, n_in + rider.n_in
        c, d = b + n_out, b + n_out + rider.n_out
        rider.run(refs[a:b], refs[c:d], refs[d + n_scr:], step_of(*[pl.program_id(i) for i in range(len(grid))]), nsteps)
        body(*(refs[:a] + refs[b:c] + refs[d:d + n_scr]))

    outs = pl.pallas_call(
        full, name=name, grid=grid, in_specs=in_specs + rider.in_specs, out_specs=out_specs + rider.out_specs,
        out_shape=out_shape + rider.out_shape, scratch_shapes=scratch + rider.scratch,
        compiler_params=_cp(("arbitrary",) * len(grid)))(*args, *rider.operands)
    extra = rider.take(outs[n_out:])
    if job is not None:
        job.advance(extra)
        extra = None
    return outs[:n_out], extra


def _mm_nn(name, a, w, *, out_dtype=F32, tm=512, tn=768, tk=2048, res=None, scale=1.0, gain=None, rider=None,
           jobs=None):
    M, K = a.shape
    J, K2, Np = w.shape
    assert K == K2
    tm, tn, tk = _tile(M, tm), _tile(Np, tn, LANE), _tile(K, tk, LANE)
    npj = Np // tn
    nk = K // tk
    grid = (M // tm, J * npj, nk)
    assert gain is None or (J * npj == 1 and res is not None)

    def body(*refs):
        refs = list(refs)
        a_ref, w_ref = refs[:2]
        r_ref = refs[2] if res is not None else None
        g_ref = refs[3] if gain is not None else None
        acc = refs[-1]
        o_ref = refs[-3] if gain is not None else refs[-2]
        k = pl.program_id(2)
        part = jnp.dot(a_ref[...].astype(BF16), w_ref[...], preferred_element_type=F32)

        def finish(v):
            if res is not None:
                v = r_ref[...] + scale * v
            o_ref[...] = v.astype(o_ref.dtype)
            if gain is not None:
                refs[-2][...] = _rms(v, g_ref[...]).astype(BF16)

        if nk == 1:
            finish(part)
            return

        @pl.when(k == 0)
        def _():
            acc[...] = part

        @pl.when(k > 0)
        def _():
            acc[...] += part

        @pl.when(k == nk - 1)
        def _():
            finish(acc[...])

    in_specs = [pl.BlockSpec((tm, tk), lambda i, n, k: (i, k)),
                pl.BlockSpec((None, tk, tn), lambda i, n, k: (n // npj, k, n % npj))]
    args = [a, w]
    if res is not None:
        in_specs.append(pl.BlockSpec((tm, tn), lambda i, n, k: (i, n)))
        args.append(res)
    out_specs = [pl.BlockSpec((tm, tn), lambda i, n, k: (i, n))]
    out_shape = [jax.ShapeDtypeStruct((M, J * Np), out_dtype)]
    if gain is not None:
        in_specs.append(pl.BlockSpec((1, tn), lambda i, n, k: (0, 0)))
        args.append(gain)
        out_specs.append(pl.BlockSpec((tm, tn), lambda i, n, k: (i, n)))
        out_shape.append(jax.ShapeDtypeStruct((M, J * Np), BF16))
    outs, extra = _hosted(name, body, grid, in_specs, out_specs, out_shape, [pltpu.VMEM((tm, tn), F32)], args,
                          ("parallel", "parallel", "arbitrary"), rider, jobs)
    outs = tuple(outs) + (() if rider is None else (extra,))
    return outs[0] if len(outs) == 1 else outs


def _mm_nt(name, dy, w, *, out_dtype=F32, tm=512, tn=2048, tk=768, jb=1, scale=1.0, jobs=None):
    M, N = dy.shape
    J, K, Np = w.shape
    assert N == J * Np
    tm, tn, tk = _tile(M, tm), _tile(K, tn, LANE), (Np if jb > 1 else _tile(Np, tk, LANE))
    npj = Np // tk
    nc = J * npj // jb
    nt = (((1,), (1,)), ((), ()))

    def body(a_ref, w_ref, o_ref, acc):
        c = pl.program_id(2)
        if jb == 1:
            part = lax.dot_general(a_ref[...].astype(BF16), w_ref[...], nt, preferred_element_type=F32)
        else:
            part = sum(lax.dot_general(a_ref[:, j * Np:(j + 1) * Np].astype(BF16), w_ref[j], nt,
                                       preferred_element_type=F32) for j in range(jb))
        if nc == 1:
            o_ref[...] = (scale * part).astype(o_ref.dtype)
            return

        @pl.when(c == 0)
        def _():
            acc[...] = part

        @pl.when(c > 0)
        def _():
            acc[...] += part

        @pl.when(c == nc - 1)
        def _():
            o_ref[...] = (scale * acc[...]).astype(o_ref.dtype)

    w_spec = (pl.BlockSpec((None, tn, tk), lambda i, n, c: (c // npj, n, c % npj)) if jb == 1 else
              pl.BlockSpec((jb, tn, Np), lambda i, n, c: (c, n, 0)))
    (out,), _ = _hosted(
        name, body, (M // tm, K // tn, nc), [pl.BlockSpec((tm, jb * tk), lambda i, n, c: (i, c)), w_spec],
        [pl.BlockSpec((tm, tn), lambda i, n, c: (i, n))], [jax.ShapeDtypeStruct((M, K), out_dtype)],
        [pltpu.VMEM((tm, tn) if nc > 1 else (8, LANE), F32)], (dy, w), ("parallel", "parallel", "arbitrary"), None, jobs)
    return out


def _mm_tn(name, x, dy, J, *, tm=8192, tko=256, tn=512, scale=1.0, jobs=None):
    M, K = x.shape
    M2, N = dy.shape
    assert M == M2 and N % J == 0
    Np = N // J
    tm, tko, tn = _tile(M, tm, LANE), _tile(K, tko, LANE), _tile(Np, tn, LANE)
    npj = Np // tn
    nm = M // tm

    def body(x_ref, d_ref, o_ref, acc):
        m = pl.program_id(2)
        part = lax.dot_general(x_ref[...].astype(BF16), d_ref[...].astype(BF16), (((0,), (0,)), ((), ())),
                               preferred_element_type=F32)
        if nm == 1:
            o_ref[...] = scale * part
            return

        @pl.when(m == 0)
        def _():
            acc[...] = part

        @pl.when(m > 0)
        def _():
            acc[...] += part

        @pl.when(m == nm - 1)
        def _():
            o_ref[...] = scale * acc[...]

    nk, nn = K // tko, J * npj
    k_outer = x.size + nk * dy.size <= dy.size + nn * x.size
    kn = (lambda a, b: (a, b)) if k_outer else (lambda a, b: (b, a))
    (out,), _ = _hosted(
        name, body, (nk, nn, nm) if k_outer else (nn, nk, nm),
        [pl.BlockSpec((tm, tko), lambda a, b, m: (m, kn(a, b)[0])),
         pl.BlockSpec((tm, tn), lambda a, b, m: (m, kn(a, b)[1]))],
        [pl.BlockSpec((None, tko, tn), lambda a, b, m: (kn(a, b)[1] // npj, kn(a, b)[0], kn(a, b)[1] % npj))],
        [jax.ShapeDtypeStruct((J, K, Np), F32)], [pltpu.VMEM((tko, tn) if nm > 1 else (8, LANE), F32)], (x, dy),
        ("parallel", "parallel", "arbitrary"), None, jobs)
    return out


def _swiglu_act(g, u):
    return jax.nn.silu(g) * u


def _ffn_up(name, xn, wgu, *, tm=1024, rider=None):
    M, K = xn.shape
    J, _, F2 = wgu.shape
    F = F2 // 2
    tm = _tile(M, tm)

    def body(a_ref, w_ref, gu_ref, h_ref):
        r = jnp.dot(a_ref[...], w_ref[...], preferred_element_type=F32)
        gu_ref[...] = r.astype(gu_ref.dtype)
        h_ref[...] = _swiglu_act(r[:, :F], r[:, F:]).astype(h_ref.dtype)

    (gu, hid), extra = _hosted(
        name, body, (M // tm, J),
        [pl.BlockSpec((tm, K), lambda i, j: (i, 0)), pl.BlockSpec((None, K, F2), lambda i, j: (j, 0, 0))],
        [pl.BlockSpec((tm, F2), lambda i, j: (i, j)), pl.BlockSpec((tm, F), lambda i, j: (i, j))],
        [jax.ShapeDtypeStruct((M, J * F2), BF16), jax.ShapeDtypeStruct((M, J * F), BF16)], [], (xn, wgu),
        ("parallel", "parallel"), rider)
    return gu, hid, extra


def _ffn_down_dx(name, dout, wd, gu, J, *, scale, tm=1024, jobs=None):
    M, D = dout.shape
    F = wd.shape[1] // J
    tm = _tile(M, tm)

    def body(d_ref, w_ref, gu_ref, o_ref):
        dh = scale * lax.dot_general(d_ref[...], w_ref[...], (((1,), (1,)), ((), ())), preferred_element_type=F32)
        gu = gu_ref[...].astype(F32)
        _, vjp = jax.vjp(_swiglu_act, gu[:, :F], gu[:, F:])
        o_ref[...] = jnp.concatenate(vjp(dh), axis=-1).astype(o_ref.dtype)

    (out,), _ = _hosted(
        name, body, (M // tm, J),
        [pl.BlockSpec((tm, D), lambda i, j: (i, 0)), pl.BlockSpec((None, F, D), lambda i, j: (0, j, 0)),
         pl.BlockSpec((tm, 2 * F), lambda i, j: (i, j))],
        [pl.BlockSpec((tm, 2 * F), lambda i, j: (i, j))], [jax.ShapeDtypeStruct((M, J * 2 * F), BF16)], [],
        (dout, wd, gu), ("parallel", "parallel"), None, jobs)
    return out


def _all_gather(name, shards):
    n = len(shards)

    def body(*refs):
        start, forward, finish = _gather_phases(refs[:n], refs[n:2 * n], *refs[2 * n:])
        start()
        forward()
        finish()

    any_spec = pl.BlockSpec(memory_space=pl.ANY)
    return pl.pallas_call(
        body, name=name, in_specs=[any_spec] * n, out_specs=[any_spec] * n,
        out_shape=[jax.ShapeDtypeStruct((NDEV,) + s.shape, s.dtype) for s in shards],
        scratch_shapes=_gather_sems(n),
    )(*shards)


def _gather_sems(n):
    return [pltpu.SemaphoreType.DMA((n, 7)), pltpu.SemaphoreType.DMA((n, 7)), pltpu.SemaphoreType.DMA((n,))]


def _gather_phases(ins, outs, send_sems, recv_sems, local_sems):
    n = len(ins)
    x, y, c = lax.axis_index("x"), lax.axis_index("y"), lax.axis_index("c")
    me, sibling = (x, y, c), (x, y, 1 - c)
    chips = [(1 - x, y), (x, 1 - y), (1 - x, 1 - y)]

    def blk(i, px, py, pc):
        return outs[i].at[4 * px + 2 * py + pc]

    def copy(i, k, block, to, src=None):
        return pltpu.make_async_remote_copy(
            src_ref=blk(i, *block) if src is None else src, dst_ref=blk(i, *block),
            send_sem=send_sems.at[i, k], recv_sem=recv_sems.at[i, k], device_id=to, device_id_type=MESH)

    def local(i):
        return pltpu.make_async_copy(ins[i], blk(i, *me), local_sems.at[i])

    def firsts(i):
        return [copy(i, 0, me, sibling, src=ins[i])] + [copy(i, 1 + j, me, (*chip, c), src=ins[i])
                                                        for j, chip in enumerate(chips)]

    def start():
        for i in range(n):
            local(i).start()
        for i in range(n):
            for cp in firsts(i):
                cp.start()

    def forward():
        for i in range(n):
            for j, chip in enumerate(chips):
                copy(i, 1 + j, (*chip, c), me).wait_recv()
                copy(i, 4 + j, (*chip, c), sibling).start()

    def finish():
        for i in range(n):
            copy(i, 0, sibling, me).wait_recv()
            for j, chip in enumerate(chips):
                copy(i, 4 + j, (*chip, 1 - c), me).wait_recv()
        for i in range(n):
            for cp in firsts(i):
                cp.wait_send()
            for j, chip in enumerate(chips):
                copy(i, 4 + j, (*chip, c), sibling).wait_send()
            local(i).wait()

    return start, forward, finish


class _GatherRider:
    def __init__(self, shards):
        self.operands = list(shards)
        n = len(self.operands)
        self.n_in = self.n_out = n
        any_spec = pl.BlockSpec(memory_space=pl.ANY)
        self.in_specs = [any_spec] * n
        self.out_specs = [any_spec] * n
        self.out_shape = [jax.ShapeDtypeStruct((NDEV,) + s.shape, s.dtype) for s in self.operands]
        self.scratch = _gather_sems(n)

    def bind(self, step_of, nsteps):
        return self

    def take(self, outs):
        return list(outs)

    def run(self, ins, outs, sems, step, nsteps):
        start, forward, finish = _gather_phases(ins, outs, *sems)
        pl.when(step == 0)(start)
        pl.when(step == (4 * nsteps) // 5)(forward)
        pl.when(step == nsteps - 1)(finish)


class _SwapRider:
    def __init__(self, arr, streams, grid, tile, out_shape, out_block, out_map):
        self.arr, self.streams, self.grid, self.tile = arr, streams, grid, tile
        self.ns, self.n = len(streams), grid[0] * grid[1]
        self.operands = [arr] * (2 * self.ns)
        self.n_in, self.n_out = 2 * self.ns, 1
        self.out_shape = [jax.ShapeDtypeStruct(out_shape, F32)]
        self.out_block, self.out_map = out_block, out_map
        tr, C = tile
        slots = [pltpu.VMEM((2, tr, C), w) for _, w, _, _ in streams]
        self.scratch = slots + slots + [pltpu.SemaphoreType.DMA((self.ns, 2)), pltpu.SemaphoreType.DMA((self.ns, 2)),
                                        pltpu.SemaphoreType.REGULAR((self.ns,))]

    def bind(self, step_of, nsteps):
        assert nsteps >= self.n
        self.period = period = nsteps // self.n
        n, nr = self.n, self.grid[1]

        def ids(*g):
            k = jnp.minimum(step_of(*g) // period, n - 1)
            pos = {a: lax.axis_index(a) for a in AXES}
            return k // nr, k % nr, [v for a in AXES for v in (pos[a], 1 - pos[a])]

        block = (None,) * (self.arr.ndim - 2) + tuple(self.tile)
        self.in_specs = []
        for _, _, keep_map, send_map in self.streams:
            for m in (keep_map, send_map):
                self.in_specs.append(pl.BlockSpec(block, functools.partial(lambda *g, m: m(*ids(*g)), m=m)))
        self.out_specs = [pl.BlockSpec(self.out_block, lambda *g: self.out_map(*ids(*g)))]
        return self

    def take(self, outs):
        return outs[0]

    def run(self, ins, outs, scratch, step, nsteps):
        ns, n, period = self.ns, self.n, self.period
        keeps, sends, o_ref = ins[0::2], ins[1::2], outs[0]
        lands, stages = scratch[:ns], scratch[ns:2 * ns]
        send_sems, recv_sems, credits = scratch[2 * ns:]
        k = step // period
        slot = k % 2
        here = {a: lax.axis_index(a) for a in AXES}
        peers = [tuple(1 - here[a] if a == axis else here[a] for a in AXES) for axis, _, _, _ in self.streams]

        def rdma(s):
            return pltpu.make_async_remote_copy(
                src_ref=stages[s].at[slot], dst_ref=lands[s].at[slot], send_sem=send_sems.at[s, slot],
                recv_sem=recv_sems.at[s, slot], device_id=peers[s], device_id_type=MESH)

        @pl.when((k < n) & (step % period == 0))
        def _():
            @pl.when(k >= 2)
            def _():
                for s in range(ns):
                    pl.semaphore_wait(credits.at[s], 1)

            for s in range(ns):
                stages[s][slot] = sends[s][...].astype(stages[s].dtype)
                rdma(s).start()

        @pl.when((k < n) & (step % period == period - 1))
        def _():
            for s in range(ns):
                rdma(s).wait_recv()
                total = keeps[s][...] + lands[s][slot].astype(F32)
                if ns == 1:
                    o_ref[...] = total
                else:
                    o_ref[s] = total
            for s in range(ns):
                rdma(s).wait_send()

            @pl.when(k + 2 < n)
            def _():
                for s in range(ns):
                    pl.semaphore_signal(credits.at[s], inc=1, device_id=peers[s], device_id_type=MESH)


def _run_alone(name, rider):
    rider.bind(lambda t: t, rider.n)

    def body(*refs):
        a, b = rider.n_in, rider.n_in + rider.n_out
        rider.run(refs[:a], refs[a:b], refs[b:], pl.program_id(0), rider.n)

    outs = pl.pallas_call(
        body, name=name, grid=(rider.n,), in_specs=rider.in_specs, out_specs=rider.out_specs,
        out_shape=rider.out_shape, scratch_shapes=rider.scratch, compiler_params=_cp(("arbitrary",)),
    )(*rider.operands)
    return rider.take(outs)


class _Scatter:
    def __init__(self, name, g):
        self.name, self.cur, self.stage = name, g, 0
        _, self.R, self.C = g.shape

    def done(self):
        return self.stage == 3

    def rider(self, rows):
        R, C = self.R, self.C
        R2 = R // 2
        tr = _tile(R2, rows, 16)
        nrh = R2 // tr
        if self.stage == 0:
            return _SwapRider(
                self.cur.reshape(4, 2, R, C),
                [("c", BF16, lambda b, i, s: (b, s[4], i, 0), lambda b, i, s: (b, s[5], i, 0))],
                (4, 2 * nrh), (tr, C), (2, 4, R2, C), (None, None, tr, C), lambda b, i, s: (i // nrh, b, i % nrh, 0))
        if self.stage == 1:
            return _SwapRider(
                self.cur.reshape(2, 2, 2, R2, C),
                [("y", BF16, lambda b, i, s: (0, b, s[2], i, 0), lambda b, i, s: (0, b, s[3], i, 0)),
                 ("x", BF16, lambda b, i, s: (1, s[0], b, i, 0), lambda b, i, s: (1, s[1], b, i, 0))],
                (2, nrh), (tr, C), (2, 2, R2, C), (2, None, tr, C), lambda b, i, s: (0, b, i, 0))
        return _SwapRider(
            self.cur,
            [("x", BF16, lambda b, i, s: (0, s[0], i, 0), lambda b, i, s: (0, s[1], i, 0)),
             ("y", BF16, lambda b, i, s: (1, s[2], i, 0), lambda b, i, s: (1, s[3], i, 0))],
            (1, nrh), (tr, C), (2, R2, C), (2, tr, C), lambda b, i, s: (0, i, 0))

    def advance(self, out):
        self.cur, self.stage = out, self.stage + 1

    def finish(self):
        while not self.done():
            self.advance(_run_alone(f"{self.name}_s{self.stage}", self.rider(256)))
        return self.cur.reshape(self.R, self.C)


RIDER_TILE_BYTES = 3 * 512 * 1024


def _pick(jobs, nsteps, small_host=False):
    for job in sorted(jobs or (), key=lambda j: -j.R * j.C):
        if job.done():
            continue
        streams = 1 if job.stage == 0 else 2
        for budget in (RIDER_TILE_BYTES, 2 * RIDER_TILE_BYTES if small_host else 0):
            riders = [job.rider(rows) for rows in (512, 256, 128, 64) if rows * job.C * 4 * streams <= budget]
            for rider in riders:
                if 2 * rider.n <= nsteps:
                    return job, rider
            if riders and riders[-1].n <= nsteps:
                return job, riders[-1]
    return None, None


def _sum8(name, g):
    _, R, C = g.shape
    tr = _tile(R, 512)

    def body(g_ref, o_ref):
        acc = g_ref[0]
        for d in range(1, NDEV):
            acc = acc + g_ref[d]
        o_ref[...] = acc

    return pl.pallas_call(
        body, name=name, grid=(R // tr,), in_specs=[pl.BlockSpec((NDEV, tr, C), lambda i: (0, i, 0))],
        out_specs=pl.BlockSpec((tr, C), lambda i: (i, 0)), out_shape=jax.ShapeDtypeStruct((R, C), F32),
        compiler_params=_cp(("parallel",)),
    )(g)


def _adamw_math(w, g, m, v):
    m = B1 * m + (1.0 - B1) * g
    v = B2 * v + (1.0 - B2) * jnp.square(g)
    m_hat = m / (1.0 - B1 ** STEP)
    v_hat = v / (1.0 - B2 ** STEP)
    delta = -LR * (m_hat / (jnp.sqrt(v_hat) + EPS) + WD * w)
    return delta, m, v


def _adamw(name, w, m, v, gp, *, tr, cw, gw, goff=0, jobs=None):
    R, C = w.shape
    nc = C // cw
    nr = R // tr

    def body(w_ref, m_ref, v_ref, g_ref, g_out, d_out, m_out, v_out):
        g = g_ref[...][:, :cw]
        d, mn, vn = _adamw_math(w_ref[...], g, m_ref[...], v_ref[...])
        g_out[...] = g
        d_out[...] = d
        m_out[...] = mn
        v_out[...] = vn

    wspec = pl.BlockSpec((tr, cw), lambda i, j: (i, j))
    gspec = pl.BlockSpec((tr, gw), lambda i, j: (i, goff + j))
    outs, _ = _hosted(name, body, (nr, nc), [wspec, wspec, wspec, gspec], [wspec] * 4,
                      [jax.ShapeDtypeStruct((R, C), F32)] * 4, [], (w, m, v, gp), ("parallel", "parallel"), None, jobs)
    return outs


def _adamw_small(name, w, m, v, g):
    R, C = w.shape

    def body(w_ref, m_ref, v_ref, g_ref, d_out, m_out, v_out):
        d, mn, vn = _adamw_math(w_ref[...], g_ref[...], m_ref[...], v_ref[...])
        d_out[...] = d
        m_out[...] = mn
        v_out[...] = vn

    tr = _tile(R, 512)
    spec = pl.BlockSpec((tr, C), lambda i: (i, 0))
    return pl.pallas_call(
        body, name=name, grid=(R // tr,), in_specs=[spec] * 4, out_specs=[spec] * 3,
        out_shape=[jax.ShapeDtypeStruct((R, C), F32)] * 3, compiler_params=_cp(("parallel",)),
    )(w, m, v, g)


def _prep(name, parts, rows_p, cols_p):
    R, C = parts[0].shape
    n = len(parts)

    def body(*refs):
        o_ref = refs[n]
        if (R, C) != (rows_p, cols_p):
            o_ref[...] = jnp.zeros_like(o_ref)
        for i in range(n):
            o_ref[0:R, i * cols_p:i * cols_p + C] = refs[i][...].astype(BF16)

    return pl.pallas_call(
        body, name=name, out_shape=jax.ShapeDtypeStruct((rows_p, n * cols_p), BF16), compiler_params=_cp(),
    )(*parts)


def _attn_masks():
    lane = lax.broadcasted_iota(jnp.int32, (1, LANE), 1)
    return [(lane < HEAD_DIM), (lane >= HEAD_DIM)]


QH = QB
KW = QB + QH


def _band_valid(base):
    qi = lax.broadcasted_iota(jnp.int32, (QH, KW), 0)
    ki = lax.broadcasted_iota(jnp.int32, (QH, KW), 1)
    dist = qi + QB - ki
    return (dist >= 0) & (dist <= QB) & (base + ki - QB >= 0)


ATTN_T = max(DILATIONS) * QB


def _attn_groups(T):
    out = []
    for d in DILATIONS:
        for r in range(d):
            for l0 in range(0, T // d, QH):
                qrows = pl.ds(r + d * l0, QH, stride=d) if d > 1 else pl.ds(l0, QH)
                k0 = T + r + d * (l0 - QB)
                krows = pl.ds(k0, KW, stride=d) if d > 1 else pl.ds(k0, KW)
                out.append((d, qrows, krows, l0))
    return out


def _attn_specs(T, width_off):
    cur = pl.BlockSpec((T, LANE), lambda hp, b: (b, width_off + hp))
    prev = pl.BlockSpec((T, LANE), lambda hp, b: (jnp.maximum(b - 1, 0), width_off + hp))
    return cur, prev


def _attn_fwd(z):
    S, ZW = z.shape
    T = min(ATTN_T, S)
    scale = HEAD_DIM ** -0.5
    groups = _attn_groups(T)

    def body(q_ref, kc_ref, kp_ref, vc_ref, vp_ref, y_ref, m_ref, l_ref, kcat, vcat):
        b = pl.program_id(1)
        kcat[0:T, :] = kp_ref[...]
        kcat[T:, :] = kc_ref[...]
        vcat[0:T, :] = vp_ref[...]
        vcat[T:, :] = vc_ref[...]
        masks = _attn_masks()
        for d, qrows, krows, l0 in groups:
            q = q_ref[qrows, :]
            kk = kcat[krows, :].astype(BF16)
            vv = vcat[krows, :].astype(BF16)
            valid = _band_valid(b * (T // d) + l0)
            o_new = m_new = l_new = None
            for hm in masks:
                qh = jnp.where(hm, q, 0.0).astype(BF16)
                s = lax.dot_general(qh, kk, (((1,), (1,)), ((), ())), preferred_element_type=F32) * scale
                s = jnp.where(valid, s, MASK_VALUE)
                m = jnp.max(s, axis=-1, keepdims=True)
                p = jnp.exp(s - m)
                l = jnp.sum(p, axis=-1, keepdims=True)
                o = jnp.dot(p.astype(BF16), vv, preferred_element_type=F32)
                if o_new is None:
                    o_new, m_new, l_new = o, jnp.broadcast_to(m, (QH, LANE)), jnp.broadcast_to(l, (QH, LANE))
                else:
                    o_new = jnp.where(hm, o, o_new)
                    m_new = jnp.where(hm, m, m_new)
                    l_new = jnp.where(hm, l, l_new)
            if d == DILATIONS[0]:
                y_ref[qrows, :] = o_new
                m_ref[qrows, :] = m_new
                l_ref[qrows, :] = l_new
            else:
                m_old = m_ref[qrows, :]
                m_all = jnp.maximum(m_old, m_new)
                w_old, w_new = jnp.exp(m_old - m_all), jnp.exp(m_new - m_all)
                y_ref[qrows, :] = w_old * y_ref[qrows, :] + w_new * o_new
                l_ref[qrows, :] = w_old * l_ref[qrows, :] + w_new * l_new
                m_ref[qrows, :] = m_all
        y_ref[...] = y_ref[...] / l_ref[...]

    qc, _ = _attn_specs(T, 0)
    kc, kp = _attn_specs(T, ATTN_W // LANE)
    vc, vp = _attn_specs(T, 2 * ATTN_W // LANE)
    shp = jax.ShapeDtypeStruct((S, ATTN_W), F32)
    return pl.pallas_call(
        body, name="attn_fwd", grid=(ATTN_W // LANE, S // T),
        in_specs=[qc, kc, kp, vc, vp], out_specs=[qc, qc, qc], out_shape=[shp, shp, shp],
        scratch_shapes=[pltpu.VMEM((2 * T, LANE), F32), pltpu.VMEM((2 * T, LANE), F32)],
        compiler_params=_cp(("parallel", "parallel")),
    )(z, z, z, z, z)


def _attn_bwd(z, dya, ya, mg, den, jobs=None):
    S, ZW = z.shape
    T = min(ATTN_T, S)
    scale = HEAD_DIM ** -0.5
    groups = _attn_groups(T)

    def body(q_ref, kc_ref, kp_ref, vc_ref, vp_ref, dy_ref, y_ref, m_ref, n_ref, dq_ref, dk_ref, dv_ref,
             kcat, vcat, dkcat, dvcat):
        b = pl.program_id(1)

        @pl.when(b == 0)
        def _():
            dk_ref[...] = jnp.zeros_like(dk_ref)
            dv_ref[...] = jnp.zeros_like(dv_ref)

        kcat[0:T, :] = kp_ref[...]
        kcat[T:, :] = kc_ref[...]
        vcat[0:T, :] = vp_ref[...]
        vcat[T:, :] = vc_ref[...]
        dkcat[...] = jnp.zeros_like(dkcat)
        dvcat[...] = jnp.zeros_like(dvcat)
        dq_ref[...] = jnp.zeros_like(dq_ref)
        masks = _attn_masks()
        for d, rows, krows, l0 in groups:
            q, dy, y = q_ref[rows, :], dy_ref[rows, :], y_ref[rows, :]
            mrow, nrow = m_ref[rows, :], n_ref[rows, :]
            kk = kcat[krows, :].astype(BF16)
            vv = vcat[krows, :].astype(BF16)
            valid = _band_valid(b * (T // d) + l0)
            dq_acc = jnp.zeros((QH, LANE), F32)
            dk_acc = jnp.zeros((KW, LANE), F32)
            dv_acc = jnp.zeros((KW, LANE), F32)
            for hm in masks:
                qh = jnp.where(hm, q, 0.0).astype(BF16)
                dyh = jnp.where(hm, dy, 0.0)
                dyb = dyh.astype(BF16)
                dsum = jnp.sum(dyh * y, axis=-1, keepdims=True)
                mh = jnp.max(jnp.where(hm, mrow, MASK_VALUE), axis=-1, keepdims=True)
                nh = jnp.max(jnp.where(hm, nrow, 0.0), axis=-1, keepdims=True)
                s = lax.dot_general(qh, kk, (((1,), (1,)), ((), ())), preferred_element_type=F32) * scale
                p = jnp.where(valid, jnp.exp(s - mh), 0.0) * (1.0 / nh)
                pb = p.astype(BF16)
                dv_h = lax.dot_general(pb, dyb, (((0,), (0,)), ((), ())), preferred_element_type=F32)
                dp = lax.dot_general(dyb, vv, (((1,), (1,)), ((), ())), preferred_element_type=F32)
                ds = (p * (dp - dsum) * scale).astype(BF16)
                dq_h = jnp.dot(ds, kk, preferred_element_type=F32)
                dk_h = lax.dot_general(ds, qh, (((0,), (0,)), ((), ())), preferred_element_type=F32)
                dq_acc += jnp.where(hm, dq_h, 0.0)
                dk_acc += dk_h
                dv_acc += dv_h
            dq_ref[rows, :] += dq_acc
            dkcat[krows, :] += dk_acc
            dvcat[krows, :] += dv_acc

        base = pl.multiple_of(b * T, T)
        dk_ref[pl.ds(base, T), :] += dkcat[T:, :]
        dv_ref[pl.ds(base, T), :] += dvcat[T:, :]

        @pl.when(b > 0)
        def _():
            prev = pl.multiple_of(b * T - T, T)
            dk_ref[pl.ds(prev, T), :] += dkcat[0:T, :]
            dv_ref[pl.ds(prev, T), :] += dvcat[0:T, :]

    qc, _ = _attn_specs(T, 0)
    kc, kp = _attn_specs(T, ATTN_W // LANE)
    vc, vp = _attn_specs(T, 2 * ATTN_W // LANE)
    whole = pl.BlockSpec((S, LANE), lambda hp, b: (0, hp))
    shp = jax.ShapeDtypeStruct((S, ATTN_W), F32)
    outs, _ = _hosted(
        "attn_bwd", body, (ATTN_W // LANE, S // T), [qc, kc, kp, vc, vp, qc, qc, qc, qc], [qc, whole, whole],
        [shp, shp, shp], [pltpu.VMEM((2 * T, LANE), F32)] * 4, (z, z, z, z, z, dya, ya, mg, den),
        ("parallel", "arbitrary"), None, jobs)
    return outs


def _ssm_disc(lr, li, logdt, br, bi):
    dt = jnp.exp(logdt)
    mag = jnp.exp(lr * dt)
    ar = mag * jnp.cos(li * dt)
    ai = mag * jnp.sin(li * dt)
    nr, ni = ar - 1.0, ai
    den = lr * lr + li * li
    cr = (nr * lr + ni * li) / den
    ci = (ni * lr - nr * li) / den
    return ar, ai, cr * br - ci * bi, cr * bi + ci * br


def _ssm_prep(lr, li, logdt, br, bi):
    n, c = br.shape
    outs, _ = _rowwise("ssm_prep", lambda *a: (list(_ssm_disc(*a)), []), n, _tile(n, 512),
                       [_full(a) for a in (lr, li, logdt, br, bi)], [],
                       [(1, 1, _c0, F32), (1, 1, _c0, F32), (c, c, _c0, F32), (c, c, _c0, F32)])
    return outs


def _ssm_prep_bwd(lr, li, logdt, br, bi, dar, dai, dbbr, dbbi):
    n, c = br.shape

    def f(lrb, lib, dtb, brb, bib, *cts):
        _, vjp = jax.vjp(_ssm_disc, lrb, lib, dtb, brb, bib)
        return list(vjp(cts)), []

    outs, _ = _rowwise("ssm_prep_bwd", f, n, _tile(n, 512),
                       [_full(a) for a in (lr, li, logdt, br, bi, dar, dai, dbbr, dbbi)], [],
                       [(1, 1, _c0, F32)] * 3 + [(c, c, _c0, F32)] * 2)
    return outs


def _cmul(ar, ai, br, bi):
    return ar * br - ai * bi, ar * bi + ai * br


def _scan_consts(ar, ai, reverse):
    w = ar.shape[-1]
    a1 = (jnp.broadcast_to(ar, (8, w)), jnp.broadcast_to(ai, (8, w)))
    a2 = _cmul(*a1, *a1)
    a4 = _cmul(*a2, *a2)
    a8 = _cmul(*a4, *a4)
    row = lax.broadcasted_iota(jnp.int32, (8, w), 0)
    e = (8 - row) if reverse else (row + 1)
    one, zero = jnp.ones((8, w), F32), jnp.zeros((8, w), F32)
    pw = (one, zero)
    for bit, ap in ((1, a1), (2, a2), (4, a4), (8, a8)):
        sel = (e & bit) != 0
        nxt = _cmul(*pw, *ap)
        pw = (jnp.where(sel, nxt[0], pw[0]), jnp.where(sel, nxt[1], pw[1]))
    steps = []
    for sh, (pr, pi) in zip((1, 2, 4), (a1, a2, a4)):
        keep = (row < 8 - sh) if reverse else (row >= sh)
        steps.append((jnp.where(keep, pr, 0.0), jnp.where(keep, pi, 0.0)))
    return steps, pw, row


def _scan_group(xr, xi, cr, ci, consts, reverse):
    steps, pw, _ = consts
    for sh, (pr, pi) in zip((1, 2, 4), steps):
        by = 8 - sh if reverse else sh
        tr_, ti_ = _cmul(pr, pi, pltpu.roll(xr, by, 0), pltpu.roll(xi, by, 0))
        xr = xr + tr_
        xi = xi + ti_
    tr_, ti_ = _cmul(pw[0], pw[1], cr, ci)
    return xr + tr_, xi + ti_


def _ssm_fwd(z, a_r, a_i, bdr, bdi, cmr, cmi, dskip, ts, rider=None):
    S, ZW = z.shape
    NS = SSM_G * SSM_P
    PW = PACK * SSM_P
    uoff = (ZW - SSM_W) // LANE
    nsteps = S // ts

    def body(u_ref, ar_ref, ai_ref, bdr_ref, bdi_ref, cmr_ref, cmi_ref, d_ref, hr_ref, hi_ref, y_ref, car_r, car_i):
        s = pl.program_id(1)

        @pl.when(s == 0)
        def _():
            car_r[...] = jnp.zeros_like(car_r)
            car_i[...] = jnp.zeros_like(car_i)

        u = u_ref[...]
        ub = u.astype(BF16)
        nt = (((1,), (1,)), ((), ()))
        hr_ref[...] = lax.dot_general(ub, bdr_ref[...], nt, preferred_element_type=F32)
        hi_ref[...] = lax.dot_general(ub, bdi_ref[...], nt, preferred_element_type=F32)
        consts = _scan_consts(ar_ref[...], ai_ref[...], False)

        def step(j, carry):
            rows = pl.ds(pl.multiple_of(j * 8, 8), 8)
            hr, hi = _scan_group(hr_ref[rows, :], hi_ref[rows, :], carry[0], carry[1], consts, False)
            hr_ref[rows, :] = hr
            hi_ref[rows, :] = hi
            return jnp.broadcast_to(hr[7:8, :], (8, PW)), jnp.broadcast_to(hi[7:8, :], (8, PW))

        cr, ci = lax.fori_loop(0, ts // 8, step, (car_r[...], car_i[...]))
        car_r[...] = cr
        car_i[...] = ci
        y = lax.dot_general(hr_ref[...].astype(BF16), cmr_ref[...], nt, preferred_element_type=F32)
        y -= lax.dot_general(hi_ref[...].astype(BF16), cmi_ref[...], nt, preferred_element_type=F32)
        y_ref[...] = y + d_ref[...] * u

    row_a = pl.BlockSpec((1, PW), lambda i, s: (0, i))
    (hr, hi, y), extra = _hosted(
        "ssm_fwd", body, (SSM_G // PACK, nsteps),
        [pl.BlockSpec((ts, LANE), lambda i, s: (s, uoff + i)), row_a, row_a,
         pl.BlockSpec((None, PW, LANE), lambda i, s: (i, 0, 0)), pl.BlockSpec((None, PW, LANE), lambda i, s: (i, 0, 0)),
         pl.BlockSpec((None, LANE, PW), lambda i, s: (i, 0, 0)), pl.BlockSpec((None, LANE, PW), lambda i, s: (i, 0, 0)),
         pl.BlockSpec((1, LANE), lambda i, s: (0, i))],
        [pl.BlockSpec((ts, PW), lambda i, s: (s, i)), pl.BlockSpec((ts, PW), lambda i, s: (s, i)),
         pl.BlockSpec((ts, LANE), lambda i, s: (s, i))],
        [jax.ShapeDtypeStruct((S, NS), F32), jax.ShapeDtypeStruct((S, NS), F32), jax.ShapeDtypeStruct((S, SSM_W), F32)],
        [pltpu.VMEM((8, PW), F32), pltpu.VMEM((8, PW), F32)], (z, a_r, a_i, bdr, bdi, cmr, cmi, dskip),
        ("parallel", "arbitrary"), rider)
    return hr, hi, y, extra


def _ssm_bwd(z, dyp, hr, hi, a_r, a_i, bdr, bdi, cmr, cmi, dskip, ts, jobs=None):
    S, ZW = z.shape
    NS = SSM_G * SSM_P
    PW = PACK * SSM_P
    uoff = (ZW - SSM_W) // LANE
    nsteps = S // ts
    npk = SSM_G // PACK

    def body(u_ref, dy_ref, hr_ref, hi_ref, hpr_ref, hpi_ref, ar_ref, ai_ref, bdr_ref, bdi_ref, cmr_ref, cmi_ref,
             d_ref, du_ref, dbdr_ref, dbdi_ref, dcmr_ref, dcmi_ref, dar_ref, dai_ref, dd_ref,
             lr_s, li_s, hcr, hci, car_r, car_i):
        s = pl.program_id(1)
        first_tile = s == nsteps - 1

        @pl.when(s == 0)
        def _():
            car_r[...] = jnp.zeros_like(car_r)
            car_i[...] = jnp.zeros_like(car_i)
            for r in (dbdr_ref, dbdi_ref, dcmr_ref, dcmi_ref, dar_ref, dai_ref, dd_ref):
                r[...] = jnp.zeros_like(r)

        u, dy = u_ref[...], dy_ref[...]
        ub, dyb = u.astype(BF16), dy.astype(BF16)
        lr_s[...] = jnp.dot(dyb, cmr_ref[...], preferred_element_type=F32)
        li_s[...] = -jnp.dot(dyb, cmi_ref[...], preferred_element_type=F32)
        keep_prev = jnp.where(first_tile, 0.0, 1.0)
        hcr[0:8, :] = hpr_ref[...] * keep_prev
        hci[0:8, :] = hpi_ref[...] * keep_prev
        hcr[8:, :] = hr_ref[...]
        hci[8:, :] = hi_ref[...]
        consts = _scan_consts(ar_ref[...], -ai_ref[...], True)
        row = consts[2]
        ngrp = ts // 8

        def step(jj, carry):
            cr, ci, accr, acci = carry
            j = ngrp - 1 - jj
            rows = pl.ds(pl.multiple_of(j * 8, 8), 8)
            nxt = pl.ds(pl.multiple_of(j * 8 + 8, 8), 8)
            lr, li = _scan_group(lr_s[rows, :], li_s[rows, :], cr, ci, consts, True)
            lr_s[rows, :] = lr
            li_s[rows, :] = li
            pr, pi = hcr[rows, :], hci[rows, :]
            hsr = jnp.where(row == 0, jnp.broadcast_to(pr[7:8, :], (8, PW)), pltpu.roll(hcr[nxt, :], 1, 0))
            hsi = jnp.where(row == 0, jnp.broadcast_to(pi[7:8, :], (8, PW)), pltpu.roll(hci[nxt, :], 1, 0))
            accr = accr + lr * hsr + li * hsi
            acci = acci + li * hsr - lr * hsi
            return jnp.broadcast_to(lr[0:1, :], (8, PW)), jnp.broadcast_to(li[0:1, :], (8, PW)), accr, acci

        zero = jnp.zeros((8, PW), F32)
        cr, ci, accr, acci = lax.fori_loop(0, ngrp, step, (car_r[...], car_i[...], zero, zero))
        car_r[...] = cr
        car_i[...] = ci
        dar_ref[...] += jnp.sum(accr, axis=0, keepdims=True)
        dai_ref[...] += jnp.sum(acci, axis=0, keepdims=True)
        lrb, lib = lr_s[...].astype(BF16), li_s[...].astype(BF16)
        du = jnp.dot(lrb, bdr_ref[...], preferred_element_type=F32)
        du += jnp.dot(lib, bdi_ref[...], preferred_element_type=F32)
        du_ref[...] = du + dy * d_ref[...]
        tn = (((0,), (0,)), ((), ()))
        dbdr_ref[...] += lax.dot_general(lrb, ub, tn, preferred_element_type=F32)
        dbdi_ref[...] += lax.dot_general(lib, ub, tn, preferred_element_type=F32)
        dcmr_ref[...] += lax.dot_general(dyb, hr_ref[...].astype(BF16), tn, preferred_element_type=F32)
        dcmi_ref[...] -= lax.dot_general(dyb, hi_ref[...].astype(BF16), tn, preferred_element_type=F32)
        dd_ref[...] += jnp.sum(dy * u, axis=0, keepdims=True)

    rev = lambda s: nsteps - 1 - s
    row_a = pl.BlockSpec((1, PW), lambda i, s: (0, i))
    tile = pl.BlockSpec((ts, PW), lambda i, s: (rev(s), i))
    prev8 = pl.BlockSpec((8, PW), lambda i, s: (jnp.maximum(rev(s) * (ts // 8) - 1, 0), i))
    cols = pl.BlockSpec((ts, LANE), lambda i, s: (rev(s), i))
    bd = pl.BlockSpec((None, PW, LANE), lambda i, s: (i, 0, 0))
    cm = pl.BlockSpec((None, LANE, PW), lambda i, s: (i, 0, 0))
    outs, _ = _hosted(
        "ssm_bwd", body, (npk, nsteps),
        [pl.BlockSpec((ts, LANE), lambda i, s: (rev(s), uoff + i)), cols, tile, tile, prev8, prev8,
         row_a, row_a, bd, bd, cm, cm, pl.BlockSpec((1, LANE), lambda i, s: (0, i))],
        [cols, bd, bd, cm, cm, row_a, row_a, pl.BlockSpec((1, LANE), lambda i, s: (0, i))],
        [jax.ShapeDtypeStruct((S, SSM_W), F32),
         jax.ShapeDtypeStruct((npk, PW, LANE), F32), jax.ShapeDtypeStruct((npk, PW, LANE), F32),
         jax.ShapeDtypeStruct((npk, LANE, PW), F32), jax.ShapeDtypeStruct((npk, LANE, PW), F32),
         jax.ShapeDtypeStruct((1, NS), F32), jax.ShapeDtypeStruct((1, NS), F32), jax.ShapeDtypeStruct((1, SSM_W), F32)],
        [pltpu.VMEM((ts, PW), F32), pltpu.VMEM((ts, PW), F32), pltpu.VMEM((ts + 8, PW), F32),
         pltpu.VMEM((ts + 8, PW), F32), pltpu.VMEM((8, PW), F32), pltpu.VMEM((8, PW), F32)],
        (z, dyp, hr, hi, hr, hi, a_r, a_i, bdr, bdi, cmr, cmi, dskip), ("parallel", "arbitrary"), None, jobs)
    return outs


def _block_diag(m4):
    npk, g, a, b = m4.shape
    eye = jnp.eye(g, dtype=m4.dtype)
    return (m4[:, :, :, None, :] * eye[None, :, None, :, None]).reshape(npk, g * a, g * b)


def _block_diag_take(m, a, b):
    npk = m.shape[0]
    m5 = m.reshape(npk, PACK, a, PACK, b)
    return jnp.stack([m5[:, g, :, g, :] for g in range(PACK)], axis=1)


def _mix_out(ya, ypre, gl, ga, gb, bglu):
    yg = jax.nn.gelu(ypre)
    yb = yg * jax.nn.sigmoid(gl + bglu)
    return jnp.concatenate([_rms(ya, ga), _rms(yb, gb)], axis=-1)


def _tail_loss(h3, gl, pe, gf, tgt):
    h4 = h3 + jax.nn.sigmoid(gl) * pe
    err = jnp.square(_rms(h4, gf) - tgt)
    return 0.5 * jnp.mean(err, axis=-1, keepdims=True)


def kernel(x, p, ffn1_norm, ffn1_w_gate, ffn1_w_up, ffn1_w_down, mix_norm, w_in, attn_out_norm, ssm_lambda_re, ssm_lambda_im, ssm_log_dt, ssm_b_re, ssm_b_im, ssm_c_re, ssm_c_im, ssm_d, ssm_w_glu, ssm_b_glu, ssm_out_norm, w_out, ffn2_norm, ffn2_w_gate, ffn2_w_up, ffn2_w_down, ple_norm, ple_w_gate, ple_w_proj, final_norm, loss_target, m_ffn1_norm, m_ffn1_w_gate, m_ffn1_w_up, m_ffn1_w_down, m_mix_norm, m_w_in, m_attn_out_norm, m_ssm_lambda_re, m_ssm_lambda_im, m_ssm_log_dt, m_ssm_b_re, m_ssm_b_im, m_ssm_c_re, m_ssm_c_im, m_ssm_d, m_ssm_w_glu, m_ssm_b_glu, m_ssm_out_norm, m_w_out, m_ffn2_norm, m_ffn2_w_gate, m_ffn2_w_up, m_ffn2_w_down, m_ple_norm, m_ple_w_gate, m_ple_w_proj, m_final_norm, v_ffn1_norm, v_ffn1_w_gate, v_ffn1_w_up, v_ffn1_w_down, v_mix_norm, v_w_in, v_attn_out_norm, v_ssm_lambda_re, v_ssm_lambda_im, v_ssm_log_dt, v_ssm_b_re, v_ssm_b_im, v_ssm_c_re, v_ssm_c_im, v_ssm_d, v_ssm_w_glu, v_ssm_b_glu, v_ssm_out_norm, v_w_out, v_ffn2_norm, v_ffn2_w_gate, v_ffn2_w_up, v_ffn2_w_down, v_ple_norm, v_ple_w_gate, v_ple_w_proj, v_final_norm):
    A = dict(locals())
    xs = x[0]
    ps = p[0, 0]
    tgt = loss_target[0]
    S, D = xs.shape
    FSH = ffn1_w_gate.shape[-1]
    FSP = -(-FSH // LANE) * LANE
    TR = _tile(S, 256)
    ZW = 3 * ATTN_W + SSM_W

    wgu1 = _prep("prep_gu1", [ffn1_w_gate[0], ffn1_w_up[0]], D, FSP)
    wgu2 = _prep("prep_gu2", [ffn2_w_gate[0], ffn2_w_up[0]], D, FSP)
    wd1 = _prep("prep_d1", [ffn1_w_down[0]], FSP, D)
    wd2 = _prep("prep_d2", [ffn2_w_down[0]], FSP, D)
    win = _prep("prep_in", [w_in[0]], D, w_in.shape[-1])
    wglu = _prep("prep_glu", [ssm_w_glu[0]], ssm_w_glu.shape[1], SSM_W)
    wout = _prep("prep_out", [w_out[0]], w_out.shape[1], D)
    wpg = _prep("prep_pg", [ple_w_gate[0]], ple_w_gate.shape[1], D)
    wpp = _prep("prep_pp", [ple_w_proj[0]], ple_w_proj.shape[1], ple_w_proj.shape[2])
    (Wgu1,) = _all_gather("ag_weights", [wgu1])
    rowstack = lambda w: w.reshape(1, w.shape[0] * w.shape[1], w.shape[2])

    def ffn_norm(tag, h, gain):
        return _rowwise(f"{tag}_norm", lambda a, g: ([_rms(a, g)], []), S, TR, [_full(h)], [gain], [(D, D, _c0, BF16)])[0][0]

    xn1 = ffn_norm("ffn1", xs, ffn1_norm)
    gu1, hid1, (Wd1, Win) = _ffn_up("ffn1_up", xn1, Wgu1, rider=_GatherRider([wd1, win]))
    Wd1 = rowstack(Wd1)
    h1, (Wd2,) = _mm_nn("ffn1_down", hid1, Wd1, tn=D // 2, tk=NDEV * FSP, res=xs, scale=0.5,
                        rider=_GatherRider([wd2]))
    Wd2 = rowstack(Wd2)
    un = ffn_norm("mix", h1, mix_norm)
    z, (Wglu, Wout, Wpg, Wpp) = _mm_nn("mix_in", un, Win, tn=512, tk=D,
                                       rider=_GatherRider([wglu, wout, wpg, wpp]))
    Wglu, Wout, Wpg = rowstack(Wglu), rowstack(Wout), rowstack(Wpg)
    ya, mg, den = _attn_fwd(z)

    col = lambda a: a.reshape(-1, 1)
    lr_c, li_c = col(ssm_lambda_re), col(ssm_lambda_im)
    dt_c = col(jnp.broadcast_to(ssm_log_dt.reshape(SSM_G, 1), (SSM_G, SSM_P)))
    b_re2, b_im2 = ssm_b_re.reshape(-1, SSM_C), ssm_b_im.reshape(-1, SSM_C)
    ar_c, ai_c, bbr, bbi = _ssm_prep(lr_c, li_c, dt_c, b_re2, b_im2)
    a_r, a_i = ar_c.reshape(1, -1), ai_c.reshape(1, -1)
    npk = SSM_G // PACK
    bdr = _block_diag(bbr.reshape(npk, PACK, SSM_P, SSM_C)).astype(BF16)
    bdi = _block_diag(bbi.reshape(npk, PACK, SSM_P, SSM_C)).astype(BF16)
    cmr = _block_diag(ssm_c_re.reshape(npk, PACK, SSM_C, SSM_P)).astype(BF16)
    cmi = _block_diag(ssm_c_im.reshape(npk, PACK, SSM_C, SSM_P)).astype(BF16)
    TS = _tile(S, 512)
    hr, hi, ypre, (Wgu2,) = _ssm_fwd(z, a_r, a_i, bdr, bdi, cmr, cmi, ssm_d, TS, rider=_GatherRider([wgu2]))
    (yg,), _ = _rowwise("ssm_gelu", lambda a: ([jax.nn.gelu(a)], []), S, TR, [_full(ypre)], [], [(SSM_W, SSM_W, _c0, BF16)])
    gl = _mm_nn("ssm_glu", yg, Wglu, tn=SSM_W, tk=SSM_W)
    (ycat,), _ = _rowwise("mix_out", lambda *a: ([_mix_out(*a)], []), S, TR, [_full(ya), _full(ypre), _full(gl)],
                          [attn_out_norm, ssm_out_norm, ssm_b_glu], [(MIX_W, MIX_W, _c0, BF16)])
    h2, xn2 = _mm_nn("mix_proj", ycat, Wout, tn=D, tk=D, res=h1, scale=1.0, gain=ffn2_norm)
    gu2, hid2, _ = _ffn_up("ffn2_up", xn2, Wgu2)
    h3 = _mm_nn("ffn2_down", hid2, Wd2, tn=D // 2, tk=NDEV * FSP, res=h2, scale=0.5)
    hn = ffn_norm("ple", h3, ple_norm)
    pgl = _mm_nn("ple_gate", hn, Wpg, tn=D // 2, tk=D)
    pb = ps
    pe = _mm_nn("ple_proj", pb, Wpp, tn=Wpp.shape[2], tk=Wpp.shape[1])

    def tail(h3b, glb, peb, tb, gf):
        rows, vjp = jax.vjp(lambda a, b, c, g: _tail_loss(a, b, c, g, tb), h3b, glb, peb, gf)
        dh, dgl, dpe, dgf = vjp(jnp.ones_like(rows))
        return [dh, dgl, dpe], [jnp.broadcast_to(jnp.sum(rows, axis=0, keepdims=True), (1, LANE)), dgf]

    (dh3_dir, dpgl, dpe), (loss_row, g_final) = _rowwise(
        "tail", tail, S, TR, [_full(h3), _full(pgl), _full(pe), _full(tgt)], [final_norm.reshape(1, D)],
        [(D, D, _c0, F32), (D, D, _c0, BF16), (D, D, _c0, BF16)], [(LANE, LANE, _c0), (D, D, _c0)])
    loss = lax.psum(loss_row[0, 0], AXES)

    def norm_bwd(tag, h, gain, dn, dres):
        def f(hb, dnb, drb, g):
            _, vjp = jax.vjp(_rms, hb, g)
            dh, dg = vjp(dnb)
            dh = dh + drb
            return [dh, dh], [dg]
        (dh, dhb), (dg,) = _rowwise(f"{tag}_norm_bwd", f, S, TR, [_full(h), _full(dn), _full(dres)], [gain],
                                    [(D, D, _c0, F32), (D, D, _c0, BF16)], [(D, D, _c0)], jobs=jobs)
        return dh, dhb, dg

    restack = lambda g: g.reshape((NDEV, g.shape[1] // NDEV) + g.shape[2:])
    jobs, scat = [], {}

    def scatter(key, g):
        scat[key] = _Scatter("rs_" + key, g)
        jobs.append(scat[key])

    late = []
    dhn = _mm_nt("ple_gate_dx", dpgl, Wpg, tn=D, tk=D)
    late.append(lambda: scatter("pg", restack(_mm_tn("ple_gate_dw", hn, dpgl, 1, jobs=jobs))))
    late.append(lambda: scatter("pp", _mm_tn("ple_proj_dw", pb, dpe, NDEV, jobs=jobs)))
    dh3, dh3b, g_ple_norm = norm_bwd("ple", h3, ple_norm, dhn, dh3_dir)

    def ffn_bwd(tag, h, gain, Wgu, Wd, saved, dout, doutb):
        xn, gu, hid = saved
        dgu = _ffn_down_dx(f"{tag}_down_dx", doutb, Wd, gu, NDEV, scale=0.5, jobs=jobs)
        scatter(tag + "gu", _mm_tn(f"{tag}_up_dw", xn, dgu, NDEV, tn=FSP, jobs=jobs))
        scatter(tag + "d", restack(_mm_tn(f"{tag}_down_dw", hid, doutb, 1, tko=FSP, tn=256, scale=0.5, jobs=jobs)))
        dxn = _mm_nt(f"{tag}_up_dx", dgu, Wgu, tm=512, tn=256, jb=NDEV, jobs=jobs)
        dh, dhb, g_norm = norm_bwd(tag, h, gain, dxn, dout)
        return dh, dhb, g_norm

    dh2, dh2b, g_ffn2_norm = ffn_bwd("ffn2", h2, ffn2_norm, Wgu2, Wd2, (xn2, gu2, hid2), dh3, dh3b)

    dycat = _mm_nt("mix_proj_dx", dh2b, Wout, tn=D, tk=D, jobs=jobs)
    late.append(lambda: scatter("out", restack(_mm_tn("mix_proj_dw", ycat, dh2b, 1, jobs=jobs))))

    def mix_out_bwd(yab, ypb, glb, dyc, ga, gb, bglu):
        _, vjp = jax.vjp(_mix_out, yab, ypb, glb, ga, gb, bglu)
        dya_, dyp_, dgl_, dga, dgb, dbg = vjp(dyc)
        return [dya_, dyp_, dgl_], [dga, dgb, dbg]
    (dya, dyp_dir, dglb), (g_attn_norm, g_ssm_norm, g_bglu) = _rowwise(
        "mix_out_bwd", mix_out_bwd, S, TR, [_full(ya), _full(ypre), _full(gl), _full(dycat)],
        [attn_out_norm, ssm_out_norm, ssm_b_glu],
        [(ATTN_W, ATTN_W, _c0, F32), (SSM_W, SSM_W, _c0, F32), (SSM_W, SSM_W, _c0, BF16)],
        [(ATTN_W, ATTN_W, _c0), (SSM_W, SSM_W, _c0), (SSM_W, SSM_W, _c0)])
    dyg = _mm_nt("ssm_glu_dx", dglb, Wglu, tn=SSM_W, tk=SSM_W, jobs=jobs)
    late.append(lambda: scatter("glu", restack(_mm_tn("ssm_glu_dw", yg, dglb, 1, jobs=jobs))))

    def gelu_bwd(ypb, dygb, ddir):
        _, vjp = jax.vjp(jax.nn.gelu, ypb)
        return [ddir + vjp(dygb)[0]], []
    (dyp,), _ = _rowwise("ssm_gelu_bwd", gelu_bwd, S, TR, [_full(ypre), _full(dyg), _full(dyp_dir)], [],
                         [(SSM_W, SSM_W, _c0, F32)])
    du, dbdr, dbdi, dcmr, dcmi, da_r, da_i, g_ssm_d = _ssm_bwd(z, dyp, hr, hi, a_r, a_i, bdr, bdi, cmr, cmi, ssm_d, TS,
                                                              jobs=jobs)
    dbbr = _block_diag_take(dbdr, SSM_P, SSM_C).reshape(-1, SSM_C)
    dbbi = _block_diag_take(dbdi, SSM_P, SSM_C).reshape(-1, SSM_C)
    g_c_re = _block_diag_take(dcmr, SSM_C, SSM_P).reshape(ssm_c_re.shape)
    g_c_im = _block_diag_take(dcmi, SSM_C, SSM_P).reshape(ssm_c_im.shape)
    dlr, dli, ddt, g_b_re, g_b_im = _ssm_prep_bwd(lr_c, li_c, dt_c, b_re2, b_im2, col(da_r), col(da_i), dbbr, dbbi)
    g_lam_re, g_lam_im = dlr.reshape(ssm_lambda_re.shape), dli.reshape(ssm_lambda_im.shape)
    g_log_dt = jnp.sum(ddt.reshape(SSM_G, SSM_P), axis=1).reshape(ssm_log_dt.shape)
    g_b_re, g_b_im = g_b_re.reshape(ssm_b_re.shape), g_b_im.reshape(ssm_b_im.shape)

    dq, dk, dv = _attn_bwd(z, dya, ya, mg, den, jobs=jobs)
    (dz,), _ = _rowwise("mix_dz", lambda *a: ([jnp.concatenate(a, axis=-1)], []), S, TR,
                        [_full(dq), _full(dk), _full(dv), _full(du)], [], [(ZW, ZW, _c0, BF16)])
    dun = _mm_nt("mix_in_dx", dz, Win, tm=512, tn=512, jb=NDEV, jobs=jobs)
    scatter("in", _mm_tn("mix_in_dw", un, dz, NDEV, tn=512, jobs=jobs))
    dh1, dh1b, g_mix_norm = norm_bwd("mix", h1, mix_norm, dun, dh2)

    dx, _dxb, g_ffn1_norm = ffn_bwd("ffn1", xs, ffn1_norm, Wgu1, Wd1, (xn1, gu1, hid1), dh1, dh1b)
    for run in (late[2], late[0], late[3], late[1]):
        run()

    out = {}

    def upd(name, key, *, tr, cw, gw, goff=0):
        w, m, v = A[name][0], A["m_" + name][0], A["v_" + name][0]
        g, dlt, mn, vn = _adamw("adamw_" + name, w, m, v, scat[key].finish(), tr=tr, cw=cw, gw=gw, goff=goff, jobs=jobs)
        for k, val in (("grad_", g), ("delta_", dlt), ("new_m_", mn), ("new_v_", vn)):
            out[k + name] = val[None]

    DT = _tile(D, 256)
    FT = _tile(FSH, 512)
    DC = _tile(D, 1024, LANE)
    upd("ffn2_w_gate", "ffn2gu", tr=DT, cw=FSH, gw=FSP, goff=0)
    upd("ffn2_w_up", "ffn2gu", tr=DT, cw=FSH, gw=FSP, goff=1)
    upd("ffn2_w_down", "ffn2d", tr=FT, cw=DC, gw=DC)
    upd("w_in", "in", tr=DT, cw=w_in.shape[-1], gw=w_in.shape[-1])
    upd("ffn1_w_gate", "ffn1gu", tr=DT, cw=FSH, gw=FSP, goff=0)
    upd("ffn1_w_up", "ffn1gu", tr=DT, cw=FSH, gw=FSP, goff=1)
    upd("ffn1_w_down", "ffn1d", tr=FT, cw=DC, gw=DC)
    upd("w_out", "out", tr=w_out.shape[1], cw=DC, gw=DC)
    upd("ple_w_gate", "pg", tr=ple_w_gate.shape[1], cw=DC, gw=DC)
    upd("ssm_w_glu", "glu", tr=ssm_w_glu.shape[1], cw=SSM_W, gw=SSM_W)
    upd("ple_w_proj", "pp", tr=ple_w_proj.shape[1], cw=ple_w_proj.shape[2], gw=ple_w_proj.shape[2])

    small = [("ffn1_norm", g_ffn1_norm), ("mix_norm", g_mix_norm), ("attn_out_norm", g_attn_norm),
             ("ssm_lambda_re", g_lam_re), ("ssm_lambda_im", g_lam_im), ("ssm_log_dt", g_log_dt),
             ("ssm_b_re", g_b_re), ("ssm_b_im", g_b_im), ("ssm_c_re", g_c_re), ("ssm_c_im", g_c_im),
             ("ssm_d", g_ssm_d), ("ssm_b_glu", g_bglu), ("ssm_out_norm", g_ssm_norm), ("ffn2_norm", g_ffn2_norm),
             ("ple_norm", g_ple_norm), ("final_norm", g_final)]
    chunk = 8 * LANE

    def pack(arrs):
        parts = []
        for a in arrs:
            flat = a.reshape(-1)
            padn = -(-flat.shape[0] // chunk) * chunk
            parts.append(jnp.pad(flat, (0, padn - flat.shape[0])).reshape(-1, LANE))
        return jnp.concatenate(parts, axis=0)

    g_pack = pack([g for _, g in small])
    (g_all,) = _all_gather("ag_small", [g_pack])
    g_sum = _sum8("small_sum", g_all)
    w_pack = pack([A[n] for n, _ in small])
    m_pack = pack([A["m_" + n] for n, _ in small])
    v_pack = pack([A["v_" + n] for n, _ in small])
    d_pack, mn_pack, vn_pack = _adamw_small("adamw_small", w_pack, m_pack, v_pack, g_sum)
    off = 0
    for n, _ in small:
        shape = A[n].shape
        size = math.prod(shape)
        rows = -(-size // chunk) * 8
        for k, buf in (("grad_", g_sum), ("delta_", d_pack), ("new_m_", mn_pack), ("new_v_", vn_pack)):
            out[k + n] = buf[off:off + rows].reshape(-1)[:size].reshape(shape)
        off += rows

    names = ['ffn1_norm', 'ffn1_w_gate', 'ffn1_w_up', 'ffn1_w_down', 'mix_norm', 'w_in', 'attn_out_norm',
             'ssm_lambda_re', 'ssm_lambda_im', 'ssm_log_dt', 'ssm_b_re', 'ssm_b_im', 'ssm_c_re', 'ssm_c_im', 'ssm_d',
             'ssm_w_glu', 'ssm_b_glu', 'ssm_out_norm', 'w_out', 'ffn2_norm', 'ffn2_w_gate', 'ffn2_w_up', 'ffn2_w_down',
             'ple_norm', 'ple_w_gate', 'ple_w_proj', 'final_norm']
    return (loss, dx[None], *[out[k + n] for k in ("grad_", "delta_", "new_m_", "new_v_") for n in names])
```

```python
import functools
import math

import jax
import jax.numpy as jnp
from jax import lax
from jax.experimental import pallas as pl
from jax.experimental.pallas import tpu as pltpu

F32, BF16 = jnp.float32, jnp.bfloat16
MESH = pl.DeviceIdType.MESH
NDEV = 8
AXES = ("x", "y", "c")
LANE = 128
VMEM_LIMIT = 60 * 1024 * 1024

ATTN_W = 1024
HEAD_DIM = 64
SSM_W = 1024
MIX_W = ATTN_W + SSM_W
SSM_G, SSM_P, SSM_C = 64, 64, 16
PACK = 8
DILATIONS = (1, 4, 16)
QB = 128
NORM_EPS = 1e-6
MASK_VALUE = -1e30
LR, B1, B2, EPS, WD, STEP = 0.001, 0.9, 0.999, 1e-08, 0.01, 10


def _cp(sem=None):
    return pltpu.CompilerParams(dimension_semantics=sem, vmem_limit_bytes=VMEM_LIMIT)


def _tile(n, target, mult=8):
    if n <= target:
        return n
    for t in range(target - target % mult, 0, -mult):
        if n % t == 0:
            return t
    return n


def _rms(x, g):
    return x * lax.rsqrt(jnp.mean(x * x, axis=-1, keepdims=True) + NORM_EPS) * g


def _rowwise(name, fn, S, tr, rows, fulls, outs, accs=(), ncol=1, jobs=None):
    nr, nf, no, na = len(rows), len(fulls), len(outs), len(accs)

    def body(*refs):
        ins = [r[...] for r in refs[:nr + nf]]
        o_refs = refs[nr + nf:nr + nf + no]
        a_refs = refs[nr + nf + no:]
        o_vals, a_vals = fn(*ins)
        for r, v in zip(o_refs, o_vals):
            r[...] = v.astype(r.dtype)
        if na:
            @pl.when(pl.program_id(1) == 0)
            def _():
                for r in a_refs:
                    r[...] = jnp.zeros_like(r)
            for r, v in zip(a_refs, a_vals):
                r[...] += v

    in_specs = [pl.BlockSpec((tr, w), functools.partial(lambda j, i, cm: (i, cm(j)), cm=cm)) for _, w, cm in rows]
    in_specs += [pl.BlockSpec(f.shape, functools.partial(lambda j, i, nd: (0,) * nd, nd=f.ndim)) for f in fulls]
    out_specs = [pl.BlockSpec((tr, w), functools.partial(lambda j, i, cm: (i, cm(j)), cm=cm)) for _, w, cm, _ in outs]
    out_specs += [pl.BlockSpec((1, w), functools.partial(lambda j, i, cm: (0, cm(j)), cm=cm)) for _, w, cm in accs]
    out_shape = [jax.ShapeDtypeStruct((S, c), dt) for c, _, _, dt in outs]
    out_shape += [jax.ShapeDtypeStruct((1, c), F32) for c, _, _ in accs]
    res, _ = _hosted(name, body, (ncol, S // tr), in_specs, out_specs, out_shape, [],
                     [a for a, _, _ in rows] + list(fulls), ("parallel", "arbitrary" if na else "parallel"), None, jobs)
    return res[:no], res[no:]


def _c0(j):
    return 0


def _full(a):
    return (a, a.shape[1], _c0)


def _hosted(name, body, grid, in_specs, out_specs, out_shape, scratch, args, sem, rider=None, jobs=None):
    nsteps = math.prod(grid)

    def step_of(*g):
        t = 0
        for gi, n in zip(g, grid):
            t = t * n + gi
        return t

    job = None
    if rider is None and jobs:
        def block_bytes(spec, like):
            shape = [d for d in (spec.block_shape or ()) if d is not None]
            return math.prod(shape) * jnp.dtype(like.dtype).itemsize if shape else 0

        held = 2 * sum(block_bytes(s, a) for s, a in zip(list(in_specs) + list(out_specs), list(args) + list(out_shape)))
        held += sum(math.prod(s.shape) * jnp.dtype(s.dtype).itemsize for s in scratch
                    if getattr(s, "dtype", None) in (F32, BF16))
        job, rider = _pick(jobs, nsteps, small_host=held + 16 * RIDER_TILE_BYTES + (8 << 20) <= VMEM_LIMIT)
    if rider is None:
        outs = pl.pallas_call(body, name=name, grid=grid, in_specs=in_specs, out_specs=out_specs, out_shape=out_shape,
                              scratch_shapes=scratch, compiler_params=_cp(sem))(*args)
        return outs, None
    rider.bind(step_of, nsteps)
    n_in, n_out, n_scr = len(in_specs), len(out_specs), len(scratch)

    def full(*refs):
        a, b = n_in, n_in + rider.n_in
        c, d = b + n_out, b + n_out + rider.n_out
        rider.run(refs[a:b], refs[c:d], refs[d + n_scr:], step_of(*[pl.program_id(i) for i in range(len(grid))]), nsteps)
        body(*(refs[:a] + refs[b:c] + refs[d:d + n_scr]))

    outs = pl.pallas_call(
        full, name=name, grid=grid, in_specs=in_specs + rider.in_specs, out_specs=out_specs + rider.out_specs,
        out_shape=out_shape + rider.out_shape, scratch_shapes=scratch + rider.scratch,
        compiler_params=_cp(("arbitrary",) * len(grid)))(*args, *rider.operands)
    extra = rider.take(outs[n_out:])
    if job is not None:
        job.advance(extra)
        extra = None
    return outs[:n_out], extra


def _mm_nn(name, a, w, *, out_dtype=F32, tm=512, tn=768, tk=2048, res=None, scale=1.0, gain=None, rider=None,
           jobs=None):
    M, K = a.shape
    J, K2, Np = w.shape
    assert K == K2
    tm, tn, tk = _tile(M, tm), _tile(Np, tn, LANE), _tile(K, tk, LANE)
    npj = Np // tn
    nk = K // tk
    grid = (M // tm, J * npj, nk)
    assert gain is None or (J * npj == 1 and res is not None)

    def body(*refs):
        refs = list(refs)
        a_ref, w_ref = refs[:2]
        r_ref = refs[2] if res is not None else None
        g_ref = refs[3] if gain is not None else None
        acc = refs[-1]
        o_ref = refs[-3] if gain is not None else refs[-2]
        k = pl.program_id(2)
        part = jnp.dot(a_ref[...].astype(BF16), w_ref[...], preferred_element_type=F32)

        def finish(v):
            if res is not None:
                v = r_ref[...] + scale * v
            o_ref[...] = v.astype(o_ref.dtype)
            if gain is not None:
                refs[-2][...] = _rms(v, g_ref[...]).astype(BF16)

        if nk == 1:
            finish(part)
            return

        @pl.when(k == 0)
        def _():
            acc[...] = part

        @pl.when(k > 0)
        def _():
            acc[...] += part

        @pl.when(k == nk - 1)
        def _():
            finish(acc[...])

    in_specs = [pl.BlockSpec((tm, tk), lambda i, n, k: (i, k)),
                pl.BlockSpec((None, tk, tn), lambda i, n, k: (n // npj, k, n % npj))]
    args = [a, w]
    if res is not None:
        in_specs.append(pl.BlockSpec((tm, tn), lambda i, n, k: (i, n)))
        args.append(res)
    out_specs = [pl.BlockSpec((tm, tn), lambda i, n, k: (i, n))]
    out_shape = [jax.ShapeDtypeStruct((M, J * Np), out_dtype)]
    if gain is not None:
        in_specs.append(pl.BlockSpec((1, tn), lambda i, n, k: (0, 0)))
        args.append(gain)
        out_specs.append(pl.BlockSpec((tm, tn), lambda i, n, k: (i, n)))
        out_shape.append(jax.ShapeDtypeStruct((M, J * Np), BF16))
    outs, extra = _hosted(name, body, grid, in_specs, out_specs, out_shape, [pltpu.VMEM((tm, tn), F32)], args,
                          ("parallel", "parallel", "arbitrary"), rider, jobs)
    outs = tuple(outs) + (() if rider is None else (extra,))
    return outs[0] if len(outs) == 1 else outs


def _mm_nt(name, dy, w, *, out_dtype=F32, tm=512, tn=2048, tk=768, jb=1, scale=1.0, jobs=None):
    M, N = dy.shape
    J, K, Np = w.shape
    assert N == J * Np
    tm, tn, tk = _tile(M, tm), _tile(K, tn, LANE), (Np if jb > 1 else _tile(Np, tk, LANE))
    npj = Np // tk
    nc = J * npj // jb
    nt = (((1,), (1,)), ((), ()))

    def body(a_ref, w_ref, o_ref, acc):
        c = pl.program_id(2)
        if jb == 1:
            part = lax.dot_general(a_ref[...].astype(BF16), w_ref[...], nt, preferred_element_type=F32)
        else:
            part = sum(lax.dot_general(a_ref[:, j * Np:(j + 1) * Np].astype(BF16), w_ref[j], nt,
                                       preferred_element_type=F32) for j in range(jb))
        if nc == 1:
            o_ref[...] = (scale * part).astype(o_ref.dtype)
            return

        @pl.when(c == 0)
        def _():
            acc[...] = part

        @pl.when(c > 0)
        def _():
            acc[...] += part

        @pl.when(c == nc - 1)
        def _():
            o_ref[...] = (scale * acc[...]).astype(o_ref.dtype)

    w_spec = (pl.BlockSpec((None, tn, tk), lambda i, n, c: (c // npj, n, c % npj)) if jb == 1 else
              pl.BlockSpec((jb, tn, Np), lambda i, n, c: (c, n, 0)))
    (out,), _ = _hosted(
        name, body, (M // tm, K // tn, nc), [pl.BlockSpec((tm, jb * tk), lambda i, n, c: (i, c)), w_spec],
        [pl.BlockSpec((tm, tn), lambda i, n, c: (i, n))], [jax.ShapeDtypeStruct((M, K), out_dtype)],
        [pltpu.VMEM((tm, tn) if nc > 1 else (8, LANE), F32)], (dy, w), ("parallel", "parallel", "arbitrary"), None, jobs)
    return out


def _mm_tn(name, x, dy, J, *, tm=8192, tko=256, tn=512, scale=1.0, jobs=None):
    M, K = x.shape
    M2, N = dy.shape
    assert M == M2 and N % J == 0
    Np = N // J
    tm, tko, tn = _tile(M, tm, LANE), _tile(K, tko, LANE), _tile(Np, tn, LANE)
    npj = Np // tn
    nm = M // tm

    def body(x_ref, d_ref, o_ref, acc):
        m = pl.program_id(2)
        part = lax.dot_general(x_ref[...].astype(BF16), d_ref[...].astype(BF16), (((0,), (0,)), ((), ())),
                               preferred_element_type=F32)
        if nm == 1:
            o_ref[...] = scale * part
            return

        @pl.when(m == 0)
        def _():
            acc[...] = part

        @pl.when(m > 0)
        def _():
            acc[...] += part

        @pl.when(m == nm - 1)
        def _():
            o_ref[...] = scale * acc[...]

    nk, nn = K // tko, J * npj
    k_outer = x.size + nk * dy.size <= dy.size + nn * x.size
    kn = (lambda a, b: (a, b)) if k_outer else (lambda a, b: (b, a))
    (out,), _ = _hosted(
        name, body, (nk, nn, nm) if k_outer else (nn, nk, nm),
        [pl.BlockSpec((tm, tko), lambda a, b, m: (m, kn(a, b)[0])),
         pl.BlockSpec((tm, tn), lambda a, b, m: (m, kn(a, b)[1]))],
        [pl.BlockSpec((None, tko, tn), lambda a, b, m: (kn(a, b)[1] // npj, kn(a, b)[0], kn(a, b)[1] % npj))],
        [jax.ShapeDtypeStruct((J, K, Np), F32)], [pltpu.VMEM((tko, tn) if nm > 1 else (8, LANE), F32)], (x, dy),
        ("parallel", "parallel", "arbitrary"), None, jobs)
    return out


def _swiglu_act(g, u):
    return jax.nn.silu(g) * u


def _ffn_up(name, xn, wgu, *, tm=1024, rider=None):
    M, K = xn.shape
    J, _, F2 = wgu.shape
    F = F2 // 2
    tm = _tile(M, tm)

    def body(a_ref, w_ref, gu_ref, h_ref):
        r = jnp.dot(a_ref[...], w_ref[...], preferred_element_type=F32)
        gu_ref[...] = r.astype(gu_ref.dtype)
        h_ref[...] = _swiglu_act(r[:, :F], r[:, F:]).astype(h_ref.dtype)

    (gu, hid), extra = _hosted(
        name, body, (M // tm, J),
        [pl.BlockSpec((tm, K), lambda i, j: (i, 0)), pl.BlockSpec((None, K, F2), lambda i, j: (j, 0, 0))],
        [pl.BlockSpec((tm, F2), lambda i, j: (i, j)), pl.BlockSpec((tm, F), lambda i, j: (i, j))],
        [jax.ShapeDtypeStruct((M, J * F2), BF16), jax.ShapeDtypeStruct((M, J * F), BF16)], [], (xn, wgu),
        ("parallel", "parallel"), rider)
    return gu, hid, extra


def _ffn_down_dx(name, dout, wd, gu, J, *, scale, tm=1024, jobs=None):
    M, D = dout.shape
    F = wd.shape[1] // J
    tm = _tile(M, tm)

    def body(d_ref, w_ref, gu_ref, o_ref):
        dh = scale * lax.dot_general(d_ref[...], w_ref[...], (((1,), (1,)), ((), ())), preferred_element_type=F32)
        gu = gu_ref[...].astype(F32)
        _, vjp = jax.vjp(_swiglu_act, gu[:, :F], gu[:, F:])
        o_ref[...] = jnp.concatenate(vjp(dh), axis=-1).astype(o_ref.dtype)

    (out,), _ = _hosted(
        name, body, (M // tm, J),
        [pl.BlockSpec((tm, D), lambda i, j: (i, 0)), pl.BlockSpec((None, F, D), lambda i, j: (0, j, 0)),
         pl.BlockSpec((tm, 2 * F), lambda i, j: (i, j))],
        [pl.BlockSpec((tm, 2 * F), lambda i, j: (i, j))], [jax.ShapeDtypeStruct((M, J * 2 * F), BF16)], [],
        (dout, wd, gu), ("parallel", "parallel"), None, jobs)
    return out


def _all_gather(name, shards):
    n = len(shards)

    def body(*refs):
        start, forward, finish = _gather_phases(refs[:n], refs[n:2 * n], *refs[2 * n:])
        start()
        forward()
        finish()

    any_spec = pl.BlockSpec(memory_space=pl.ANY)
    return pl.pallas_call(
        body, name=name, in_specs=[any_spec] * n, out_specs=[any_spec] * n,
        out_shape=[jax.ShapeDtypeStruct((NDEV,) + s.shape, s.dtype) for s in shards],
        scratch_shapes=_gather_sems(n),
    )(*shards)


def _gather_sems(n):
    return [pltpu.SemaphoreType.DMA((n, 7)), pltpu.SemaphoreType.DMA((n, 7)), pltpu.SemaphoreType.DMA((n,))]


def _gather_phases(ins, outs, send_sems, recv_sems, local_sems):
    n = len(ins)
    x, y, c = lax.axis_index("x"), lax.axis_index("y"), lax.axis_index("c")
    me, sibling = (x, y, c), (x, y, 1 - c)
    chips = [(1 - x, y), (x, 1 - y), (1 - x, 1 - y)]

    def blk(i, px, py, pc):
        return outs[i].at[4 * px + 2 * py + pc]

    def copy(i, k, block, to, src=None):
        return pltpu.make_async_remote_copy(
            src_ref=blk(i, *block) if src is None else src, dst_ref=blk(i, *block),
            send_sem=send_sems.at[i, k], recv_sem=recv_sems.at[i, k], device_id=to, device_id_type=MESH)

    def local(i):
        return pltpu.make_async_copy(ins[i], blk(i, *me), local_sems.at[i])

    def firsts(i):
        return [copy(i, 0, me, sibling, src=ins[i])] + [copy(i, 1 + j, me, (*chip, c), src=ins[i])
                                                        for j, chip in enumerate(chips)]

    def start():
        for i in range(n):
            local(i).start()
        for i in range(n):
            for cp in firsts(i):
                cp.start()

    def forward():
        for i in range(n):
            for j, chip in enumerate(chips):
                copy(i, 1 + j, (*chip, c), me).wait_recv()
                copy(i, 4 + j, (*chip, c), sibling).start()

    def finish():
        for i in range(n):
            copy(i, 0, sibling, me).wait_recv()
            for j, chip in enumerate(chips):
                copy(i, 4 + j, (*chip, 1 - c), me).wait_recv()
        for i in range(n):
            for cp in firsts(i):
                cp.wait_send()
            for j, chip in enumerate(chips):
                copy(i, 4 + j, (*chip, c), sibling).wait_send()
            local(i).wait()

    return start, forward, finish


class _GatherRider:
    def __init__(self, shards):
        self.operands = list(shards)
        n = len(self.operands)
        self.n_in = self.n_out = n
        any_spec = pl.BlockSpec(memory_space=pl.ANY)
        self.in_specs = [any_spec] * n
        self.out_specs = [any_spec] * n
        self.out_shape = [jax.ShapeDtypeStruct((NDEV,) + s.shape, s.dtype) for s in self.operands]
        self.scratch = _gather_sems(n)

    def bind(self, step_of, nsteps):
        return self

    def take(self, outs):
        return list(outs)

    def run(self, ins, outs, sems, step, nsteps):
        start, forward, finish = _gather_phases(ins, outs, *sems)
        pl.when(step == 0)(start)
        pl.when(step == (4 * nsteps) // 5)(forward)
        pl.when(step == nsteps - 1)(finish)


class _SwapRider:
    def __init__(self, arr, streams, grid, tile, out_shape, out_block, out_map):
        self.arr, self.streams, self.grid, self.tile = arr, streams, grid, tile
        self.ns, self.n = len(streams), grid[0] * grid[1]
        self.operands = [arr] * (2 * self.ns)
        self.n_in, self.n_out = 2 * self.ns, 1
        self.out_shape = [jax.ShapeDtypeStruct(out_shape, F32)]
        self.out_block, self.out_map = out_block, out_map
        tr, C = tile
        slots = [pltpu.VMEM((2, tr, C), w) for _, w, _, _ in streams]
        self.scratch = slots + slots + [pltpu.SemaphoreType.DMA((self.ns, 2)), pltpu.SemaphoreType.DMA((self.ns, 2)),
                                        pltpu.SemaphoreType.REGULAR((self.ns,))]

    def bind(self, step_of, nsteps):
        assert nsteps >= self.n
        self.period = period = nsteps // self.n
        n, nr = self.n, self.grid[1]

        def ids(*g):
            k = jnp.minimum(step_of(*g) // period, n - 1)
            pos = {a: lax.axis_index(a) for a in AXES}
            return k // nr, k % nr, [v for a in AXES for v in (pos[a], 1 - pos[a])]

        block = (None,) * (self.arr.ndim - 2) + tuple(self.tile)
        self.in_specs = []
        for _, _, keep_map, send_map in self.streams:
            for m in (keep_map, send_map):
                self.in_specs.append(pl.BlockSpec(block, functools.partial(lambda *g, m: m(*ids(*g)), m=m)))
        self.out_specs = [pl.BlockSpec(self.out_block, lambda *g: self.out_map(*ids(*g)))]
        return self

    def take(self, outs):
        return outs[0]

    def run(self, ins, outs, scratch, step, nsteps):
        ns, n, period = self.ns, self.n, self.period
        keeps, sends, o_ref = ins[0::2], ins[1::2], outs[0]
        lands, stages = scratch[:ns], scratch[ns:2 * ns]
        send_sems, recv_sems, credits = scratch[2 * ns:]
        k = step // period
        slot = k % 2
        here = {a: lax.axis_index(a) for a in AXES}
        peers = [tuple(1 - here[a] if a == axis else here[a] for a in AXES) for axis, _, _, _ in self.streams]

        def rdma(s):
            return pltpu.make_async_remote_copy(
                src_ref=stages[s].at[slot], dst_ref=lands[s].at[slot], send_sem=send_sems.at[s, slot],
                recv_sem=recv_sems.at[s, slot], device_id=peers[s], device_id_type=MESH)

        @pl.when((k < n) & (step % period == 0))
        def _():
            @pl.when(k >= 2)
            def _():
                for s in range(ns):
                    pl.semaphore_wait(credits.at[s], 1)

            for s in range(ns):
                stages[s][slot] = sends[s][...].astype(stages[s].dtype)
                rdma(s).start()

        @pl.when((k < n) & (step % period == period - 1))
        def _():
            for s in range(ns):
                rdma(s).wait_recv()
                total = keeps[s][...] + lands[s][slot].astype(F32)
                if ns == 1:
                    o_ref[...] = total
                else:
                    o_ref[s] = total
            for s in range(ns):
                rdma(s).wait_send()

            @pl.when(k + 2 < n)
            def _():
                for s in range(ns):
                    pl.semaphore_signal(credits.at[s], inc=1, device_id=peers[s], device_id_type=MESH)


def _run_alone(name, rider):
    rider.bind(lambda t: t, rider.n)

    def body(*refs):
        a, b = rider.n_in, rider.n_in + rider.n_out
        rider.run(refs[:a], refs[a:b], refs[b:], pl.program_id(0), rider.n)

    outs = pl.pallas_call(
        body, name=name, grid=(rider.n,), in_specs=rider.in_specs, out_specs=rider.out_specs,
        out_shape=rider.out_shape, scratch_shapes=rider.scratch, compiler_params=_cp(("arbitrary",)),
    )(*rider.operands)
    return rider.take(outs)


class _Scatter:
    def __init__(self, name, g):
        self.name, self.cur, self.stage = name, g, 0
        _, self.R, self.C = g.shape

    def done(self):
        return self.stage == 3

    def rider(self, rows):
        R, C = self.R, self.C
        R2 = R // 2
        tr = _tile(R2, rows, 16)
        nrh = R2 // tr
        if self.stage == 0:
            return _SwapRider(
                self.cur.reshape(4, 2, R, C),
                [("c", BF16, lambda b, i, s: (b, s[4], i, 0), lambda b, i, s: (b, s[5], i, 0))],
                (4, 2 * nrh), (tr, C), (2, 4, R2, C), (None, None, tr, C), lambda b, i, s: (i // nrh, b, i % nrh, 0))
        if self.stage == 1:
            return _SwapRider(
                self.cur.reshape(2, 2, 2, R2, C),
                [("y", BF16, lambda b, i, s: (0, b, s[2], i, 0), lambda b, i, s: (0, b, s[3], i, 0)),
                 ("x", BF16, lambda b, i, s: (1, s[0], b, i, 0), lambda b, i, s: (1, s[1], b, i, 0))],
                (2, nrh), (tr, C), (2, 2, R2, C), (2, None, tr, C), lambda b, i, s: (0, b, i, 0))
        return _SwapRider(
            self.cur,
            [("x", BF16, lambda b, i, s: (0, s[0], i, 0), lambda b, i, s: (0, s[1], i, 0)),
             ("y", BF16, lambda b, i, s: (1, s[2], i, 0), lambda b, i, s: (1, s[3], i, 0))],
            (1, nrh), (tr, C), (2, R2, C), (2, tr, C), lambda b, i, s: (0, i, 0))

    def advance(self, out):
        self.cur, self.stage = out, self.stage + 1

    def finish(self):
        while not self.done():
            self.advance(_run_alone(f"{self.name}_s{self.stage}", self.rider(256)))
        return self.cur.reshape(self.R, self.C)


RIDER_TILE_BYTES = 3 * 512 * 1024


def _pick(jobs, nsteps, small_host=False):
    for job in sorted(jobs or (), key=lambda j: -j.R * j.C):
        if job.done():
            continue
        streams = 1 if job.stage == 0 else 2
        for budget in (RIDER_TILE_BYTES, 2 * RIDER_TILE_BYTES if small_host else 0):
            riders = [job.rider(rows) for rows in (512, 256, 128, 64) if rows * job.C * 4 * streams <= budget]
            for rider in riders:
                if 2 * rider.n <= nsteps:
                    return job, rider
            if riders and riders[-1].n <= nsteps:
                return job, riders[-1]
    return None, None


def _sum8(name, g):
    _, R, C = g.shape
    tr = _tile(R, 512)

    def body(g_ref, o_ref):
        acc = g_ref[0]
        for d in range(1, NDEV):
            acc = acc + g_ref[d]
        o_ref[...] = acc

    return pl.pallas_call(
        body, name=name, grid=(R // tr,), in_specs=[pl.BlockSpec((NDEV, tr, C), lambda i: (0, i, 0))],
        out_specs=pl.BlockSpec((tr, C), lambda i: (i, 0)), out_shape=jax.ShapeDtypeStruct((R, C), F32),
        compiler_params=_cp(("parallel",)),
    )(g)


def _adamw_math(w, g, m, v):
    m = B1 * m + (1.0 - B1) * g
    v = B2 * v + (1.0 - B2) * jnp.square(g)
    m_hat = m / (1.0 - B1 ** STEP)
    v_hat = v / (1.0 - B2 ** STEP)
    delta = -LR * (m_hat / (jnp.sqrt(v_hat) + EPS) + WD * w)
    return delta, m, v


def _adamw(name, w, m, v, gp, *, tr, cw, gw, goff=0, jobs=None):
    R, C = w.shape
    nc = C // cw
    nr = R // tr

    def body(w_ref, m_ref, v_ref, g_ref, g_out, d_out, m_out, v_out):
        g = g_ref[...][:, :cw]
        d, mn, vn = _adamw_math(w_ref[...], g, m_ref[...], v_ref[...])
        g_out[...] = g
        d_out[...] = d
        m_out[...] = mn
        v_out[...] = vn

    wspec = pl.BlockSpec((tr, cw), lambda i, j: (i, j))
    gspec = pl.BlockSpec((tr, gw), lambda i, j: (i, goff + j))
    outs, _ = _hosted(name, body, (nr, nc), [wspec, wspec, wspec, gspec], [wspec] * 4,
                      [jax.ShapeDtypeStruct((R, C), F32)] * 4, [], (w, m, v, gp), ("parallel", "parallel"), None, jobs)
    return outs


def _adamw_small(name, w, m, v, g):
    R, C = w.shape

    def body(w_ref, m_ref, v_ref, g_ref, d_out, m_out, v_out):
        d, mn, vn = _adamw_math(w_ref[...], g_ref[...], m_ref[...], v_ref[...])
        d_out[...] = d
        m_out[...] = mn
        v_out[...] = vn

    tr = _tile(R, 512)
    spec = pl.BlockSpec((tr, C), lambda i: (i, 0))
    return pl.pallas_call(
        body, name=name, grid=(R // tr,), in_specs=[spec] * 4, out_specs=[spec] * 3,
        out_shape=[jax.ShapeDtypeStruct((R, C), F32)] * 3, compiler_params=_cp(("parallel",)),
    )(w, m, v, g)


def _prep(name, parts, rows_p, cols_p):
    R, C = parts[0].shape
    n = len(parts)

    def body(*refs):
        o_ref = refs[n]
        if (R, C) != (rows_p, cols_p):
            o_ref[...] = jnp.zeros_like(o_ref)
        for i in range(n):
            o_ref[0:R, i * cols_p:i * cols_p + C] = refs[i][...].astype(BF16)

    return pl.pallas_call(
        body, name=name, out_shape=jax.ShapeDtypeStruct((rows_p, n * cols_p), BF16), compiler_params=_cp(),
    )(*parts)


def _attn_masks():
    lane = lax.broadcasted_iota(jnp.int32, (1, LANE), 1)
    return [(lane < HEAD_DIM), (lane >= HEAD_DIM)]


QH = QB
KW = QB + QH


def _band_valid(base):
    qi = lax.broadcasted_iota(jnp.int32, (QH, KW), 0)
    ki = lax.broadcasted_iota(jnp.int32, (QH, KW), 1)
    dist = qi + QB - ki
    return (dist >= 0) & (dist <= QB) & (base + ki - QB >= 0)


ATTN_T = max(DILATIONS) * QB


def _attn_groups(T):
    out = []
    for d in DILATIONS:
        for r in range(d):
            for l0 in range(0, T // d, QH):
                qrows = pl.ds(r + d * l0, QH, stride=d) if d > 1 else pl.ds(l0, QH)
                k0 = T + r + d * (l0 - QB)
                krows = pl.ds(k0, KW, stride=d) if d > 1 else pl.ds(k0, KW)
                out.append((d, qrows, krows, l0))
    return out


def _attn_specs(T, width_off):
    cur = pl.BlockSpec((T, LANE), lambda hp, b: (b, width_off + hp))
    prev = pl.BlockSpec((T, LANE), lambda hp, b: (jnp.maximum(b - 1, 0), width_off + hp))
    return cur, prev


def _attn_fwd(z):
    S, ZW = z.shape
    T = min(ATTN_T, S)
    scale = HEAD_DIM ** -0.5
    groups = _attn_groups(T)

    def body(q_ref, kc_ref, kp_ref, vc_ref, vp_ref, y_ref, m_ref, l_ref, kcat, vcat):
        b = pl.program_id(1)
        kcat[0:T, :] = kp_ref[...]
        kcat[T:, :] = kc_ref[...]
        vcat[0:T, :] = vp_ref[...]
        vcat[T:, :] = vc_ref[...]
        masks = _attn_masks()
        for d, qrows, krows, l0 in groups:
            q = q_ref[qrows, :]
            kk = kcat[krows, :].astype(BF16)
            vv = vcat[krows, :].astype(BF16)
            valid = _band_valid(b * (T // d) + l0)
            o_new = m_new = l_new = None
            for hm in masks:
                qh = jnp.where(hm, q, 0.0).astype(BF16)
                s = lax.dot_general(qh, kk, (((1,), (1,)), ((), ())), preferred_element_type=F32) * scale
                s = jnp.where(valid, s, MASK_VALUE)
                m = jnp.max(s, axis=-1, keepdims=True)
                p = jnp.exp(s - m)
                l = jnp.sum(p, axis=-1, keepdims=True)
                o = jnp.dot(p.astype(BF16), vv, preferred_element_type=F32)
                if o_new is None:
                    o_new, m_new, l_new = o, jnp.broadcast_to(m, (QH, LANE)), jnp.broadcast_to(l, (QH, LANE))
                else:
                    o_new = jnp.where(hm, o, o_new)
                    m_new = jnp.where(hm, m, m_new)
                    l_new = jnp.where(hm, l, l_new)
            if d == DILATIONS[0]:
                y_ref[qrows, :] = o_new
                m_ref[qrows, :] = m_new
                l_ref[qrows, :] = l_new
            else:
                m_old = m_ref[qrows, :]
                m_all = jnp.maximum(m_old, m_new)
                w_old, w_new = jnp.exp(m_old - m_all), jnp.exp(m_new - m_all)
                y_ref[qrows, :] = w_old * y_ref[qrows, :] + w_new * o_new
                l_ref[qrows, :] = w_old * l_ref[qrows, :] + w_new * l_new
                m_ref[qrows, :] = m_all
        y_ref[...] = y_ref[...] / l_ref[...]

    qc, _ = _attn_specs(T, 0)
    kc, kp = _attn_specs(T, ATTN_W // LANE)
    vc, vp = _attn_specs(T, 2 * ATTN_W // LANE)
    shp = jax.ShapeDtypeStruct((S, ATTN_W), F32)
    return pl.pallas_call(
        body, name="attn_fwd", grid=(ATTN_W // LANE, S // T),
        in_specs=[qc, kc, kp, vc, vp], out_specs=[qc, qc, qc], out_shape=[shp, shp, shp],
        scratch_shapes=[pltpu.VMEM((2 * T, LANE), F32), pltpu.VMEM((2 * T, LANE), F32)],
        compiler_params=_cp(("parallel", "parallel")),
    )(z, z, z, z, z)


def _attn_bwd(z, dya, ya, mg, den, jobs=None):
    S, ZW = z.shape
    T = min(ATTN_T, S)
    scale = HEAD_DIM ** -0.5
    groups = _attn_groups(T)

    def body(q_ref, kc_ref, kp_ref, vc_ref, vp_ref, dy_ref, y_ref, m_ref, n_ref, dq_ref, dk_ref, dv_ref,
             kcat, vcat, dkcat, dvcat):
        b = pl.program_id(1)

        @pl.when(b == 0)
        def _():
            dk_ref[...] = jnp.zeros_like(dk_ref)
            dv_ref[...] = jnp.zeros_like(dv_ref)

        kcat[0:T, :] = kp_ref[...]
        kcat[T:, :] = kc_ref[...]
        vcat[0:T, :] = vp_ref[...]
        vcat[T:, :] = vc_ref[...]
        dkcat[...] = jnp.zeros_like(dkcat)
        dvcat[...] = jnp.zeros_like(dvcat)
        dq_ref[...] = jnp.zeros_like(dq_ref)
        masks = _attn_masks()
        for d, rows, krows, l0 in groups:
            q, dy, y = q_ref[rows, :], dy_ref[rows, :], y_ref[rows, :]
            mrow, nrow = m_ref[rows, :], n_ref[rows, :]
            kk = kcat[krows, :].astype(BF16)
            vv = vcat[krows, :].astype(BF16)
            valid = _band_valid(b * (T // d) + l0)
            dq_acc = jnp.zeros((QH, LANE), F32)
            dk_acc = jnp.zeros((KW, LANE), F32)
            dv_acc = jnp.zeros((KW, LANE), F32)
            for hm in masks:
                qh = jnp.where(hm, q, 0.0).astype(BF16)
                dyh = jnp.where(hm, dy, 0.0)
                dyb = dyh.astype(BF16)
                dsum = jnp.sum(dyh * y, axis=-1, keepdims=True)
                mh = jnp.max(jnp.where(hm, mrow, MASK_VALUE), axis=-1, keepdims=True)
                nh = jnp.max(jnp.where(hm, nrow, 0.0), axis=-1, keepdims=True)
                s = lax.dot_general(qh, kk, (((1,), (1,)), ((), ())), preferred_element_type=F32) * scale
                p = jnp.where(valid, jnp.exp(s - mh), 0.0) * (1.0 / nh)
                pb = p.astype(BF16)
                dv_h = lax.dot_general(pb, dyb, (((0,), (0,)), ((), ())), preferred_element_type=F32)
                dp = lax.dot_general(dyb, vv, (((1,), (1,)), ((), ())), preferred_element_type=F32)
                ds = (p * (dp - dsum) * scale).astype(BF16)
                dq_h = jnp.dot(ds, kk, preferred_element_type=F32)
                dk_h = lax.dot_general(ds, qh, (((0,), (0,)), ((), ())), preferred_element_type=F32)
                dq_acc += jnp.where(hm, dq_h, 0.0)
                dk_acc += dk_h
                dv_acc += dv_h
            dq_ref[rows, :] += dq_acc
            dkcat[krows, :] += dk_acc
            dvcat[krows, :] += dv_acc

        base = pl.multiple_of(b * T, T)
        dk_ref[pl.ds(base, T), :] += dkcat[T:, :]
        dv_ref[pl.ds(base, T), :] += dvcat[T:, :]

        @pl.when(b > 0)
        def _():
            prev = pl.multiple_of(b * T - T, T)
            dk_ref[pl.ds(prev, T), :] += dkcat[0:T, :]
            dv_ref[pl.ds(prev, T), :] += dvcat[0:T, :]

    qc, _ = _attn_specs(T, 0)
    kc, kp = _attn_specs(T, ATTN_W // LANE)
    vc, vp = _attn_specs(T, 2 * ATTN_W // LANE)
    whole = pl.BlockSpec((S, LANE), lambda hp, b: (0, hp))
    shp = jax.ShapeDtypeStruct((S, ATTN_W), F32)
    outs, _ = _hosted(
        "attn_bwd", body, (ATTN_W // LANE, S // T), [qc, kc, kp, vc, vp, qc, qc, qc, qc], [qc, whole, whole],
        [shp, shp, shp], [pltpu.VMEM((2 * T, LANE), F32)] * 4, (z, z, z, z, z, dya, ya, mg, den),
        ("parallel", "arbitrary"), None, jobs)
    return outs


def _ssm_disc(lr, li, logdt, br, bi):
    dt = jnp.exp(logdt)
    mag = jnp.exp(lr * dt)
    ar = mag * jnp.cos(li * dt)
    ai = mag * jnp.sin(li * dt)
    nr, ni = ar - 1.0, ai
    den = lr * lr + li * li
    cr = (nr * lr + ni * li) / den
    ci = (ni * lr - nr * li) / den
    return ar, ai, cr * br - ci * bi, cr * bi + ci * br


def _ssm_prep(lr, li, logdt, br, bi):
    n, c = br.shape
    outs, _ = _rowwise("ssm_prep", lambda *a: (list(_ssm_disc(*a)), []), n, _tile(n, 512),
                       [_full(a) for a in (lr, li, logdt, br, bi)], [],
                       [(1, 1, _c0, F32), (1, 1, _c0, F32), (c, c, _c0, F32), (c, c, _c0, F32)])
    return outs


def _ssm_prep_bwd(lr, li, logdt, br, bi, dar, dai, dbbr, dbbi):
    n, c = br.shape

    def f(lrb, lib, dtb, brb, bib, *cts):
        _, vjp = jax.vjp(_ssm_disc, lrb, lib, dtb, brb, bib)
        return list(vjp(cts)), []

    outs, _ = _rowwise("ssm_prep_bwd", f, n, _tile(n, 512),
                       [_full(a) for a in (lr, li, logdt, br, bi, dar, dai, dbbr, dbbi)], [],
                       [(1, 1, _c0, F32)] * 3 + [(c, c, _c0, F32)] * 2)
    return outs


def _cmul(ar, ai, br, bi):
    return ar * br - ai * bi, ar * bi + ai * br


def _scan_consts(ar, ai, reverse):
    w = ar.shape[-1]
    a1 = (jnp.broadcast_to(ar, (8, w)), jnp.broadcast_to(ai, (8, w)))
    a2 = _cmul(*a1, *a1)
    a4 = _cmul(*a2, *a2)
    a8 = _cmul(*a4, *a4)
    row = lax.broadcasted_iota(jnp.int32, (8, w), 0)
    e = (8 - row) if reverse else (row + 1)
    one, zero = jnp.ones((8, w), F32), jnp.zeros((8, w), F32)
    pw = (one, zero)
    for bit, ap in ((1, a1), (2, a2), (4, a4), (8, a8)):
        sel = (e & bit) != 0
        nxt = _cmul(*pw, *ap)
        pw = (jnp.where(sel, nxt[0], pw[0]), jnp.where(sel, nxt[1], pw[1]))
    steps = []
    for sh, (pr, pi) in zip((1, 2, 4), (a1, a2, a4)):
        keep = (row < 8 - sh) if reverse else (row >= sh)
        steps.append((jnp.where(keep, pr, 0.0), jnp.where(keep, pi, 0.0)))
    return steps, pw, row


def _scan_group(xr, xi, cr, ci, consts, reverse):
    steps, pw, _ = consts
    for sh, (pr, pi) in zip((1, 2, 4), steps):
        by = 8 - sh if reverse else sh
        tr_, ti_ = _cmul(pr, pi, pltpu.roll(xr, by, 0), pltpu.roll(xi, by, 0))
        xr = xr + tr_
        xi = xi + ti_
    tr_, ti_ = _cmul(pw[0], pw[1], cr, ci)
    return xr + tr_, xi + ti_


def _ssm_fwd(z, a_r, a_i, bdr, bdi, cmr, cmi, dskip, ts, rider=None):
    S, ZW = z.shape
    NS = SSM_G * SSM_P
    PW = PACK * SSM_P
    uoff = (ZW - SSM_W) // LANE
    nsteps = S // ts

    def body(u_ref, ar_ref, ai_ref, bdr_ref, bdi_ref, cmr_ref, cmi_ref, d_ref, hr_ref, hi_ref, y_ref, car_r, car_i):
        s = pl.program_id(1)

        @pl.when(s == 0)
        def _():
            car_r[...] = jnp.zeros_like(car_r)
            car_i[...] = jnp.zeros_like(car_i)

        u = u_ref[...]
        ub = u.astype(BF16)
        nt = (((1,), (1,)), ((), ()))
        hr_ref[...] = lax.dot_general(ub, bdr_ref[...], nt, preferred_element_type=F32)
        hi_ref[...] = lax.dot_general(ub, bdi_ref[...], nt, preferred_element_type=F32)
        consts = _scan_consts(ar_ref[...], ai_ref[...], False)

        def step(j, carry):
            rows = pl.ds(pl.multiple_of(j * 8, 8), 8)
            hr, hi = _scan_group(hr_ref[rows, :], hi_ref[rows, :], carry[0], carry[1], consts, False)
            hr_ref[rows, :] = hr
            hi_ref[rows, :] = hi
            return jnp.broadcast_to(hr[7:8, :], (8, PW)), jnp.broadcast_to(hi[7:8, :], (8, PW))

        cr, ci = lax.fori_loop(0, ts // 8, step, (car_r[...], car_i[...]))
        car_r[...] = cr
        car_i[...] = ci
        y = lax.dot_general(hr_ref[...].astype(BF16), cmr_ref[...], nt, preferred_element_type=F32)
        y -= lax.dot_general(hi_ref[...].astype(BF16), cmi_ref[...], nt, preferred_element_type=F32)
        y_ref[...] = y + d_ref[...] * u

    row_a = pl.BlockSpec((1, PW), lambda i, s: (0, i))
    (hr, hi, y), extra = _hosted(
        "ssm_fwd", body, (SSM_G // PACK, nsteps),
        [pl.BlockSpec((ts, LANE), lambda i, s: (s, uoff + i)), row_a, row_a,
         pl.BlockSpec((None, PW, LANE), lambda i, s: (i, 0, 0)), pl.BlockSpec((None, PW, LANE), lambda i, s: (i, 0, 0)),
         pl.BlockSpec((None, LANE, PW), lambda i, s: (i, 0, 0)), pl.BlockSpec((None, LANE, PW), lambda i, s: (i, 0, 0)),
         pl.BlockSpec((1, LANE), lambda i, s: (0, i))],
        [pl.BlockSpec((ts, PW), lambda i, s: (s, i)), pl.BlockSpec((ts, PW), lambda i, s: (s, i)),
         pl.BlockSpec((ts, LANE), lambda i, s: (s, i))],
        [jax.ShapeDtypeStruct((S, NS), F32), jax.ShapeDtypeStruct((S, NS), F32), jax.ShapeDtypeStruct((S, SSM_W), F32)],
        [pltpu.VMEM((8, PW), F32), pltpu.VMEM((8, PW), F32)], (z, a_r, a_i, bdr, bdi, cmr, cmi, dskip),
        ("parallel", "arbitrary"), rider)
    return hr, hi, y, extra


def _ssm_bwd(z, dyp, hr, hi, a_r, a_i, bdr, bdi, cmr, cmi, dskip, ts, jobs=None):
    S, ZW = z.shape
    NS = SSM_G * SSM_P
    PW = PACK * SSM_P
    uoff = (ZW - SSM_W) // LANE
    nsteps = S // ts
    npk = SSM_G // PACK

    def body(u_ref, dy_ref, hr_ref, hi_ref, hpr_ref, hpi_ref, ar_ref, ai_ref, bdr_ref, bdi_ref, cmr_ref, cmi_ref,
             d_ref, du_ref, dbdr_ref, dbdi_ref, dcmr_ref, dcmi_ref, dar_ref, dai_ref, dd_ref,
             lr_s, li_s, hcr, hci, car_r, car_i):
        s = pl.program_id(1)
        first_tile = s == nsteps - 1

        @pl.when(s == 0)
        def _():
            car_r[...] = jnp.zeros_like(car_r)
            car_i[...] = jnp.zeros_like(car_i)
            for r in (dbdr_ref, dbdi_ref, dcmr_ref, dcmi_ref, dar_ref, dai_ref, dd_ref):
                r[...] = jnp.zeros_like(r)

        u, dy = u_ref[...], dy_ref[...]
        ub, dyb = u.astype(BF16), dy.astype(BF16)
        lr_s[...] = jnp.dot(dyb, cmr_ref[...], preferred_element_type=F32)
        li_s[...] = -jnp.dot(dyb, cmi_ref[...], preferred_element_type=F32)
        keep_prev = jnp.where(first_tile, 0.0, 1.0)
        hcr[0:8, :] = hpr_ref[...] * keep_prev
        hci[0:8, :] = hpi_ref[...] * keep_prev
        hcr[8:, :] = hr_ref[...]
        hci[8:, :] = hi_ref[...]
        consts = _scan_consts(ar_ref[...], -ai_ref[...], True)
        row = consts[2]
        ngrp = ts // 8

        def step(jj, carry):
            cr, ci, accr, acci = carry
            j = ngrp - 1 - jj
            rows = pl.ds(pl.multiple_of(j * 8, 8), 8)
            nxt = pl.ds(pl.multiple_of(j * 8 + 8, 8), 8)
            lr, li = _scan_group(lr_s[rows, :], li_s[rows, :], cr, ci, consts, True)
            lr_s[rows, :] = lr
            li_s[rows, :] = li
            pr, pi = hcr[rows, :], hci[rows, :]
            hsr = jnp.where(row == 0, jnp.broadcast_to(pr[7:8, :], (8, PW)), pltpu.roll(hcr[nxt, :], 1, 0))
            hsi = jnp.where(row == 0, jnp.broadcast_to(pi[7:8, :], (8, PW)), pltpu.roll(hci[nxt, :], 1, 0))
            accr = accr + lr * hsr + li * hsi
            acci = acci + li * hsr - lr * hsi
            return jnp.broadcast_to(lr[0:1, :], (8, PW)), jnp.broadcast_to(li[0:1, :], (8, PW)), accr, acci

        zero = jnp.zeros((8, PW), F32)
        cr, ci, accr, acci = lax.fori_loop(0, ngrp, step, (car_r[...], car_i[...], zero, zero))
        car_r[...] = cr
        car_i[...] = ci
        dar_ref[...] += jnp.sum(accr, axis=0, keepdims=True)
        dai_ref[...] += jnp.sum(acci, axis=0, keepdims=True)
        lrb, lib = lr_s[...].astype(BF16), li_s[...].astype(BF16)
        du = jnp.dot(lrb, bdr_ref[...], preferred_element_type=F32)
        du += jnp.dot(lib, bdi_ref[...], preferred_element_type=F32)
        du_ref[...] = du + dy * d_ref[...]
        tn = (((0,), (0,)), ((), ()))
        dbdr_ref[...] += lax.dot_general(lrb, ub, tn, preferred_element_type=F32)
        dbdi_ref[...] += lax.dot_general(lib, ub, tn, preferred_element_type=F32)
        dcmr_ref[...] += lax.dot_general(dyb, hr_ref[...].astype(BF16), tn, preferred_element_type=F32)
        dcmi_ref[...] -= lax.dot_general(dyb, hi_ref[...].astype(BF16), tn, preferred_element_type=F32)
        dd_ref[...] += jnp.sum(dy * u, axis=0, keepdims=True)

    rev = lambda s: nsteps - 1 - s
    row_a = pl.BlockSpec((1, PW), lambda i, s: (0, i))
    tile = pl.BlockSpec((ts, PW), lambda i, s: (rev(s), i))
    prev8 = pl.BlockSpec((8, PW), lambda i, s: (jnp.maximum(rev(s) * (ts // 8) - 1, 0), i))
    cols = pl.BlockSpec((ts, LANE), lambda i, s: (rev(s), i))
    bd = pl.BlockSpec((None, PW, LANE), lambda i, s: (i, 0, 0))
    cm = pl.BlockSpec((None, LANE, PW), lambda i, s: (i, 0, 0))
    outs, _ = _hosted(
        "ssm_bwd", body, (npk, nsteps),
        [pl.BlockSpec((ts, LANE), lambda i, s: (rev(s), uoff + i)), cols, tile, tile, prev8, prev8,
         row_a, row_a, bd, bd, cm, cm, pl.BlockSpec((1, LANE), lambda i, s: (0, i))],
        [cols, bd, bd, cm, cm, row_a, row_a, pl.BlockSpec((1, LANE), lambda i, s: (0, i))],
        [jax.ShapeDtypeStruct((S, SSM_W), F32),
         jax.ShapeDtypeStruct((npk, PW, LANE), F32), jax.ShapeDtypeStruct((npk, PW, LANE), F32),
         jax.ShapeDtypeStruct((npk, LANE, PW), F32), jax.ShapeDtypeStruct((npk, LANE, PW), F32),
         jax.ShapeDtypeStruct((1, NS), F32), jax.ShapeDtypeStruct((1, NS), F32), jax.ShapeDtypeStruct((1, SSM_W), F32)],
        [pltpu.VMEM((ts, PW), F32), pltpu.VMEM((ts, PW), F32), pltpu.VMEM((ts + 8, PW), F32),
         pltpu.VMEM((ts + 8, PW), F32), pltpu.VMEM((8, PW), F32), pltpu.VMEM((8, PW), F32)],
        (z, dyp, hr, hi, hr, hi, a_r, a_i, bdr, bdi, cmr, cmi, dskip), ("parallel", "arbitrary"), None, jobs)
    return outs


def _block_diag(m4):
    npk, g, a, b = m4.shape
    eye = jnp.eye(g, dtype=m4.dtype)
    return (m4[:, :, :, None, :] * eye[None, :, None, :, None]).reshape(npk, g * a, g * b)


def _block_diag_take(m, a, b):
    npk = m.shape[0]
    m5 = m.reshape(npk, PACK, a, PACK, b)
    return jnp.stack([m5[:, g, :, g, :] for g in range(PACK)], axis=1)


def _mix_out(ya, ypre, gl, ga, gb, bglu):
    yg = jax.nn.gelu(ypre)
    yb = yg * jax.nn.sigmoid(gl + bglu)
    return jnp.concatenate([_rms(ya, ga), _rms(yb, gb)], axis=-1)


def _tail_loss(h3, gl, pe, gf, tgt):
    h4 = h3 + jax.nn.sigmoid(gl) * pe
    err = jnp.square(_rms(h4, gf) - tgt)
    return 0.5 * jnp.mean(err, axis=-1, keepdims=True)


def kernel(x, p, ffn1_norm, ffn1_w_gate, ffn1_w_up, ffn1_w_down, mix_norm, w_in, attn_out_norm, ssm_lambda_re, ssm_lambda_im, ssm_log_dt, ssm_b_re, ssm_b_im, ssm_c_re, ssm_c_im, ssm_d, ssm_w_glu, ssm_b_glu, ssm_out_norm, w_out, ffn2_norm, ffn2_w_gate, ffn2_w_up, ffn2_w_down, ple_norm, ple_w_gate, ple_w_proj, final_norm, loss_target, m_ffn1_norm, m_ffn1_w_gate, m_ffn1_w_up, m_ffn1_w_down, m_mix_norm, m_w_in, m_attn_out_norm, m_ssm_lambda_re, m_ssm_lambda_im, m_ssm_log_dt, m_ssm_b_re, m_ssm_b_im, m_ssm_c_re, m_ssm_c_im, m_ssm_d, m_ssm_w_glu, m_ssm_b_glu, m_ssm_out_norm, m_w_out, m_ffn2_norm, m_ffn2_w_gate, m_ffn2_w_up, m_ffn2_w_down, m_ple_norm, m_ple_w_gate, m_ple_w_proj, m_final_norm, v_ffn1_norm, v_ffn1_w_gate, v_ffn1_w_up, v_ffn1_w_down, v_mix_norm, v_w_in, v_attn_out_norm, v_ssm_lambda_re, v_ssm_lambda_im, v_ssm_log_dt, v_ssm_b_re, v_ssm_b_im, v_ssm_c_re, v_ssm_c_im, v_ssm_d, v_ssm_w_glu, v_ssm_b_glu, v_ssm_out_norm, v_w_out, v_ffn2_norm, v_ffn2_w_gate, v_ffn2_w_up, v_ffn2_w_down, v_ple_norm, v_ple_w_gate, v_ple_w_proj, v_final_norm):
    A = dict(locals())
    xs = x[0]
    ps = p[0, 0]
    tgt = loss_target[0]
    S, D = xs.shape
    FSH = ffn1_w_gate.shape[-1]
    FSP = -(-FSH // LANE) * LANE
    TR = _tile(S, 256)
    ZW = 3 * ATTN_W + SSM_W

    wgu1 = _prep("prep_gu1", [ffn1_w_gate[0], ffn1_w_up[0]], D, FSP)
    wgu2 = _prep("prep_gu2", [ffn2_w_gate[0], ffn2_w_up[0]], D, FSP)
    wd1 = _prep("prep_d1", [ffn1_w_down[0]], FSP, D)
    wd2 = _prep("prep_d2", [ffn2_w_down[0]], FSP, D)
    win = _prep("prep_in", [w_in[0]], D, w_in.shape[-1])
    wglu = _prep("prep_glu", [ssm_w_glu[0]], ssm_w_glu.shape[1], SSM_W)
    wout = _prep("prep_out", [w_out[0]], w_out.shape[1], D)
    wpg = _prep("prep_pg", [ple_w_gate[0]], ple_w_gate.shape[1], D)
    wpp = _prep("prep_pp", [ple_w_proj[0]], ple_w_proj.shape[1], ple_w_proj.shape[2])
    (Wgu1,) = _all_gather("ag_weights", [wgu1])
    rowstack = lambda w: w.reshape(1, w.shape[0] * w.shape[1], w.shape[2])

    def ffn_norm(tag, h, gain):
        return _rowwise(f"{tag}_norm", lambda a, g: ([_rms(a, g)], []), S, TR, [_full(h)], [gain], [(D, D, _c0, BF16)])[0][0]

    xn1 = ffn_norm("ffn1", xs, ffn1_norm)
    gu1, hid1, (Wd1, Win) = _ffn_up("ffn1_up", xn1, Wgu1, rider=_GatherRider([wd1, win]))
    Wd1 = rowstack(Wd1)
    h1, (Wd2,) = _mm_nn("ffn1_down", hid1, Wd1, tm=1024, tn=D // 4, tk=NDEV * FSP, res=xs, scale=0.5,
                        rider=_GatherRider([wd2]))
    Wd2 = rowstack(Wd2)
    un = ffn_norm("mix", h1, mix_norm)
    z, (Wglu, Wout, Wpg, Wpp) = _mm_nn("mix_in", un, Win, tn=512, tk=D,
                                       rider=_GatherRider([wglu, wout, wpg, wpp]))
    Wglu, Wout, Wpg = rowstack(Wglu), rowstack(Wout), rowstack(Wpg)
    ya, mg, den = _attn_fwd(z)

    col = lambda a: a.reshape(-1, 1)
    lr_c, li_c = col(ssm_lambda_re), col(ssm_lambda_im)
    dt_c = col(jnp.broadcast_to(ssm_log_dt.reshape(SSM_G, 1), (SSM_G, SSM_P)))
    b_re2, b_im2 = ssm_b_re.reshape(-1, SSM_C), ssm_b_im.reshape(-1, SSM_C)
    ar_c, ai_c, bbr, bbi = _ssm_prep(lr_c, li_c, dt_c, b_re2, b_im2)
    a_r, a_i = ar_c.reshape(1, -1), ai_c.reshape(1, -1)
    npk = SSM_G // PACK
    bdr = _block_diag(bbr.reshape(npk, PACK, SSM_P, SSM_C)).astype(BF16)
    bdi = _block_diag(bbi.reshape(npk, PACK, SSM_P, SSM_C)).astype(BF16)
    cmr = _block_diag(ssm_c_re.reshape(npk, PACK, SSM_C, SSM_P)).astype(BF16)
    cmi = _block_diag(ssm_c_im.reshape(npk, PACK, SSM_C, SSM_P)).astype(BF16)
    TS = _tile(S, 512)
    hr, hi, ypre, (Wgu2,) = _ssm_fwd(z, a_r, a_i, bdr, bdi, cmr, cmi, ssm_d, TS, rider=_GatherRider([wgu2]))
    (yg,), _ = _rowwise("ssm_gelu", lambda a: ([jax.nn.gelu(a)], []), S, TR, [_full(ypre)], [], [(SSM_W, SSM_W, _c0, BF16)])
    gl = _mm_nn("ssm_glu", yg, Wglu, tn=SSM_W, tk=SSM_W)
    (ycat,), _ = _rowwise("mix_out", lambda *a: ([_mix_out(*a)], []), S, TR, [_full(ya), _full(ypre), _full(gl)],
                          [attn_out_norm, ssm_out_norm, ssm_b_glu], [(MIX_W, MIX_W, _c0, BF16)])
    h2, xn2 = _mm_nn("mix_proj", ycat, Wout, tn=D, tk=D, res=h1, scale=1.0, gain=ffn2_norm)
    gu2, hid2, _ = _ffn_up("ffn2_up", xn2, Wgu2)
    h3 = _mm_nn("ffn2_down", hid2, Wd2, tm=1024, tn=D // 4, tk=NDEV * FSP, res=h2, scale=0.5)
    hn = ffn_norm("ple", h3, ple_norm)
    pgl = _mm_nn("ple_gate", hn, Wpg, tn=D // 2, tk=D)
    pb = ps
    pe = _mm_nn("ple_proj", pb, Wpp, tn=Wpp.shape[2], tk=Wpp.shape[1])

    def tail(h3b, glb, peb, tb, gf):
        rows, vjp = jax.vjp(lambda a, b, c, g: _tail_loss(a, b, c, g, tb), h3b, glb, peb, gf)
        dh, dgl, dpe, dgf = vjp(jnp.ones_like(rows))
        return [dh, dgl, dpe], [jnp.broadcast_to(jnp.sum(rows, axis=0, keepdims=True), (1, LANE)), dgf]

    (dh3_dir, dpgl, dpe), (loss_row, g_final) = _rowwise(
        "tail", tail, S, TR, [_full(h3), _full(pgl), _full(pe), _full(tgt)], [final_norm.reshape(1, D)],
        [(D, D, _c0, F32), (D, D, _c0, BF16), (D, D, _c0, BF16)], [(LANE, LANE, _c0), (D, D, _c0)])
    loss = lax.psum(loss_row[0, 0], AXES)

    def norm_bwd(tag, h, gain, dn, dres):
        def f(hb, dnb, drb, g):
            _, vjp = jax.vjp(_rms, hb, g)
            dh, dg = vjp(dnb)
            dh = dh + drb
            return [dh, dh], [dg]
        (dh, dhb), (dg,) = _rowwise(f"{tag}_norm_bwd", f, S, TR, [_full(h), _full(dn), _full(dres)], [gain],
                                    [(D, D, _c0, F32), (D, D, _c0, BF16)], [(D, D, _c0)], jobs=jobs)
        return dh, dhb, dg

    restack = lambda g: g.reshape((NDEV, g.shape[1] // NDEV) + g.shape[2:])
    jobs, scat = [], {}

    def scatter(key, g):
        scat[key] = _Scatter("rs_" + key, g)
        jobs.append(scat[key])

    late = []
    dhn = _mm_nt("ple_gate_dx", dpgl, Wpg, tn=D, tk=D)
    late.append(lambda: scatter("pg", restack(_mm_tn("ple_gate_dw", hn, dpgl, 1, jobs=jobs))))
    late.append(lambda: scatter("pp", _mm_tn("ple_proj_dw", pb, dpe, NDEV, jobs=jobs)))
    dh3, dh3b, g_ple_norm = norm_bwd("ple", h3, ple_norm, dhn, dh3_dir)

    def ffn_bwd(tag, h, gain, Wgu, Wd, saved, dout, doutb):
        xn, gu, hid = saved
        dgu = _ffn_down_dx(f"{tag}_down_dx", doutb, Wd, gu, NDEV, scale=0.5, jobs=jobs)
        scatter(tag + "gu", _mm_tn(f"{tag}_up_dw", xn, dgu, NDEV, tn=FSP, jobs=jobs))
        scatter(tag + "d", restack(_mm_tn(f"{tag}_down_dw", hid, doutb, 1, tko=FSP, tn=256, scale=0.5, jobs=jobs)))
        dxn = _mm_nt(f"{tag}_up_dx", dgu, Wgu, tm=512, tn=256, jb=NDEV, jobs=jobs)
        dh, dhb, g_norm = norm_bwd(tag, h, gain, dxn, dout)
        return dh, dhb, g_norm

    dh2, dh2b, g_ffn2_norm = ffn_bwd("ffn2", h2, ffn2_norm, Wgu2, Wd2, (xn2, gu2, hid2), dh3, dh3b)

    dycat = _mm_nt("mix_proj_dx", dh2b, Wout, tn=D, tk=D, jobs=jobs)
    late.append(lambda: scatter("out", restack(_mm_tn("mix_proj_dw", ycat, dh2b, 1, jobs=jobs))))

    def mix_out_bwd(yab, ypb, glb, dyc, ga, gb, bglu):
        _, vjp = jax.vjp(_mix_out, yab, ypb, glb, ga, gb, bglu)
        dya_, dyp_, dgl_, dga, dgb, dbg = vjp(dyc)
        return [dya_, dyp_, dgl_], [dga, dgb, dbg]
    (dya, dyp_dir, dglb), (g_attn_norm, g_ssm_norm, g_bglu) = _rowwise(
        "mix_out_bwd", mix_out_bwd, S, TR, [_full(ya), _full(ypre), _full(gl), _full(dycat)],
        [attn_out_norm, ssm_out_norm, ssm_b_glu],
        [(ATTN_W, ATTN_W, _c0, F32), (SSM_W, SSM_W, _c0, F32), (SSM_W, SSM_W, _c0, BF16)],
        [(ATTN_W, ATTN_W, _c0), (SSM_W, SSM_W, _c0), (SSM_W, SSM_W, _c0)])
    dyg = _mm_nt("ssm_glu_dx", dglb, Wglu, tn=SSM_W, tk=SSM_W, jobs=jobs)
    late.append(lambda: scatter("glu", restack(_mm_tn("ssm_glu_dw", yg, dglb, 1, jobs=jobs))))

    def gelu_bwd(ypb, dygb, ddir):
        _, vjp = jax.vjp(jax.nn.gelu, ypb)
        return [ddir + vjp(dygb)[0]], []
    (dyp,), _ = _rowwise("ssm_gelu_bwd", gelu_bwd, S, TR, [_full(ypre), _full(dyg), _full(dyp_dir)], [],
                         [(SSM_W, SSM_W, _c0, F32)])
    du, dbdr, dbdi, dcmr, dcmi, da_r, da_i, g_ssm_d = _ssm_bwd(z, dyp, hr, hi, a_r, a_i, bdr, bdi, cmr, cmi, ssm_d, TS,
                                                              jobs=jobs)
    dbbr = _block_diag_take(dbdr, SSM_P, SSM_C).reshape(-1, SSM_C)
    dbbi = _block_diag_take(dbdi, SSM_P, SSM_C).reshape(-1, SSM_C)
    g_c_re = _block_diag_take(dcmr, SSM_C, SSM_P).reshape(ssm_c_re.shape)
    g_c_im = _block_diag_take(dcmi, SSM_C, SSM_P).reshape(ssm_c_im.shape)
    dlr, dli, ddt, g_b_re, g_b_im = _ssm_prep_bwd(lr_c, li_c, dt_c, b_re2, b_im2, col(da_r), col(da_i), dbbr, dbbi)
    g_lam_re, g_lam_im = dlr.reshape(ssm_lambda_re.shape), dli.reshape(ssm_lambda_im.shape)
    g_log_dt = jnp.sum(ddt.reshape(SSM_G, SSM_P), axis=1).reshape(ssm_log_dt.shape)
    g_b_re, g_b_im = g_b_re.reshape(ssm_b_re.shape), g_b_im.reshape(ssm_b_im.shape)

    dq, dk, dv = _attn_bwd(z, dya, ya, mg, den, jobs=jobs)
    (dz,), _ = _rowwise("mix_dz", lambda *a: ([jnp.concatenate(a, axis=-1)], []), S, TR,
                        [_full(dq), _full(dk), _full(dv), _full(du)], [], [(ZW, ZW, _c0, BF16)])
    dun = _mm_nt("mix_in_dx", dz, Win, tm=512, tn=512, jb=NDEV, jobs=jobs)
    scatter("in", _mm_tn("mix_in_dw", un, dz, NDEV, tn=512, jobs=jobs))
    dh1, dh1b, g_mix_norm = norm_bwd("mix", h1, mix_norm, dun, dh2)

    dx, _dxb, g_ffn1_norm = ffn_bwd("ffn1", xs, ffn1_norm, Wgu1, Wd1, (xn1, gu1, hid1), dh1, dh1b)
    for run in (late[2], late[0], late[3], late[1]):
        run()

    out = {}

    def upd(name, key, *, tr, cw, gw, goff=0):
        w, m, v = A[name][0], A["m_" + name][0], A["v_" + name][0]
        g, dlt, mn, vn = _adamw("adamw_" + name, w, m, v, scat[key].finish(), tr=tr, cw=cw, gw=gw, goff=goff, jobs=jobs)
        for k, val in (("grad_", g), ("delta_", dlt), ("new_m_", mn), ("new_v_", vn)):
            out[k + name] = val[None]

    DT = _tile(D, 256)
    FT = _tile(FSH, 512)
    DC = _tile(D, 1024, LANE)
    upd("ffn2_w_gate", "ffn2gu", tr=DT, cw=FSH, gw=FSP, goff=0)
    upd("ffn2_w_up", "ffn2gu", tr=DT, cw=FSH, gw=FSP, goff=1)
    upd("ffn2_w_down", "ffn2d", tr=FT, cw=DC, gw=DC)
    upd("w_in", "in", tr=DT, cw=w_in.shape[-1], gw=w_in.shape[-1])
    upd("ffn1_w_gate", "ffn1gu", tr=DT, cw=FSH, gw=FSP, goff=0)
    upd("ffn1_w_up", "ffn1gu", tr=DT, cw=FSH, gw=FSP, goff=1)
    upd("ffn1_w_down", "ffn1d", tr=FT, cw=DC, gw=DC)
    upd("w_out", "out", tr=w_out.shape[1], cw=DC, gw=DC)
    upd("ple_w_gate", "pg", tr=ple_w_gate.shape[1], cw=DC, gw=DC)
    upd("ssm_w_glu", "glu", tr=ssm_w_glu.shape[1], cw=SSM_W, gw=SSM_W)
    upd("ple_w_proj", "pp", tr=ple_w_proj.shape[1], cw=ple_w_proj.shape[2], gw=ple_w_proj.shape[2])

    small = [("ffn1_norm", g_ffn1_norm), ("mix_norm", g_mix_norm), ("attn_out_norm", g_attn_norm),
             ("ssm_lambda_re", g_lam_re), ("ssm_lambda_im", g_lam_im), ("ssm_log_dt", g_log_dt),
             ("ssm_b_re", g_b_re), ("ssm_b_im", g_b_im), ("ssm_c_re", g_c_re), ("ssm_c_im", g_c_im),
             ("ssm_d", g_ssm_d), ("ssm_b_glu", g_bglu), ("ssm_out_norm", g_ssm_norm), ("ffn2_norm", g_ffn2_norm),
             ("ple_norm", g_ple_norm), ("final_norm", g_final)]
    chunk = 8 * LANE

    def pack(arrs):
        parts = []
        for a in arrs:
            flat = a.reshape(-1)
            padn = -(-flat.shape[0] // chunk) * chunk
            parts.append(jnp.pad(flat, (0, padn - flat.shape[0])).reshape(-1, LANE))
        return jnp.concatenate(parts, axis=0)

    g_pack = pack([g for _, g in small])
    (g_all,) = _all_gather("ag_small", [g_pack])
    g_sum = _sum8("small_sum", g_all)
    w_pack = pack([A[n] for n, _ in small])
    m_pack = pack([A["m_" + n] for n, _ in small])
    v_pack = pack([A["v_" + n] for n, _ in small])
    d_pack, mn_pack, vn_pack = _adamw_small("adamw_small", w_pack, m_pack, v_pack, g_sum)
    off = 0
    for n, _ in small:
        shape = A[n].shape
        size = math.prod(shape)
        rows = -(-size // chunk) * 8
        for k, buf in (("grad_", g_sum), ("delta_", d_pack), ("new_m_", mn_pack), ("new_v_", vn_pack)):
            out[k + n] = buf[off:off + rows].reshape(-1)[:size].reshape(shape)
        off += rows

    names = ['ffn1_norm', 'ffn1_w_gate', 'ffn1_w_up', 'ffn1_w_down', 'mix_norm', 'w_in', 'attn_out_norm',
             'ssm_lambda_re', 'ssm_lambda_im', 'ssm_log_dt', 'ssm_b_re', 'ssm_b_im', 'ssm_c_re', 'ssm_c_im', 'ssm_d',
             'ssm_w_glu', 'ssm_b_glu', 'ssm_out_norm', 'w_out', 'ffn2_norm', 'ffn2_w_gate', 'ffn2_w_up', 'ffn2_w_down',
             'ple_norm', 'ple_w_gate', 'ple_w_proj', 'final_norm']
    return (loss, dx[None], *[out[k + n] for k in ("grad_", "delta_", "new_m_", "new_v_") for n in names])
```
